```python
import jax, jax.numpy as jnp
from jax import lax

D_MODEL = 1024
BATCH = 8
SEQ = 4096
DEPTH = 1

N_ATTN_HEADS = 8
HEAD_DIM = 64
ATTN_WIDTH = N_ATTN_HEADS * HEAD_DIM
N_CONV_GROUPS = 8
CONV_GROUP_DIM = 64
CONV_WIDTH = N_CONV_GROUPS * CONV_GROUP_DIM
CONV_KSIZE = 3
D_FF = 2816
Q_BLOCK = 128
N_SUBLAYERS = 3
N_MOD = 3
EPS = 1e-6
FFN_RES_WEIGHT = 0.5
MIX_IN_WIDTH = 3 * CONV_WIDTH + 3 * ATTN_WIDTH + 2 * D_MODEL

kernel_name = "hybrid_shortconv_stickbreaking_macaron_block"


def rmsnorm(x, g):
    xf = x.astype(jnp.float32)
    inv = lax.rsqrt(jnp.mean(xf * xf, axis=-1, keepdims=True) + EPS)
    return (xf * inv).astype(x.dtype) * g


def modulate(x, shift, scale):
    return x * (1 + scale[:, None, :]) + shift[:, None, :]


def swiglu(x, w_gu, w_down):
    g, u = jnp.split(x @ w_gu, 2, axis=-1)
    return (jax.nn.silu(g) * u) @ w_down


def short_conv(b_gate, c_gate, xin, conv_w):
    v = c_gate * xin
    S = v.shape[1]
    vp = jnp.pad(v, ((0, 0), (CONV_KSIZE - 1, 0), (0, 0)))
    y = conv_w[0] * vp[:, 0:S, :]
    for k in range(1, CONV_KSIZE):
        y = y + conv_w[k] * vp[:, k:k + S, :]
    return b_gate * y


def stick_breaking_attention(q, k, v):
    S = q.shape[2]
    scale = HEAD_DIM ** -0.5
    qf = q.astype(jnp.float32)
    kf = k.astype(jnp.float32)
    vf = v.astype(jnp.float32)
    outs = []
    for i in range(S // Q_BLOCK):
        start = i * Q_BLOCK
        end = start + Q_BLOCK
        q_blk = qf[:, :, start:end, :]
        k_c = kf[:, :, :end, :]
        v_c = vf[:, :, :end, :]
        z = jnp.einsum('bhqd,bhkd->bhqk', q_blk, k_c) * scale
        t_pos = start + jnp.arange(Q_BLOCK)
        s_pos = jnp.arange(end)
        valid = s_pos[None, :] < t_pos[:, None]
        sp = jnp.where(valid, jax.nn.softplus(z), 0.0)
        rem = lax.cumsum(sp, axis=3, reverse=True) - sp
        log_a = jax.nn.log_sigmoid(z) - rem
        a = jnp.where(valid, jnp.exp(log_a), 0.0)
        outs.append(jnp.einsum('bhqk,bhkd->bhqd', a, v_c))
    return jnp.concatenate(outs, axis=2).astype(q.dtype)


def mixer(u, w_mix_in, b_merge, conv_w, w_conv_out, w_attn_out, w_out):
    B, S, _ = u.shape
    proj = u @ w_mix_in
    idx = [CONV_WIDTH, 2 * CONV_WIDTH, 3 * CONV_WIDTH,
           3 * CONV_WIDTH + ATTN_WIDTH, 3 * CONV_WIDTH + 2 * ATTN_WIDTH,
           3 * CONV_WIDTH + 3 * ATTN_WIDTH, 3 * CONV_WIDTH + 3 * ATTN_WIDTH + D_MODEL]
    cb, cc, cx, q, k, v, ga, gb = jnp.split(proj, idx, axis=-1)
    ya = short_conv(cb, cc, cx, conv_w) @ w_conv_out
    def heads(t):
        return t.reshape(B, S, N_ATTN_HEADS, HEAD_DIM).transpose(0, 2, 1, 3)
    o = stick_breaking_attention(heads(q), heads(k), heads(v))
    o = o.transpose(0, 2, 1, 3).reshape(B, S, ATTN_WIDTH)
    yb = o @ w_attn_out
    merged = jax.nn.sigmoid(ga + b_merge[0]) * ya + jax.nn.sigmoid(gb + b_merge[1]) * yb
    return merged @ w_out


def _fwd_setup_inputs(seed: int = 0) -> dict:
    key = jax.random.key(seed)
    ks = jax.random.split(key, 24)
    f32 = jnp.float32
    L, D = DEPTH, D_MODEL

    def nrm(k, shape, s):
        return jax.random.normal(k, shape, f32) * s

    return {
        "x": nrm(ks[0], (BATCH, SEQ, D), 1.0),
        "c": nrm(ks[1], (BATCH, D), 1.0),
        "w_ada": nrm(ks[2], (L, D, N_SUBLAYERS * N_MOD * D), 0.5 * D ** -0.5),
        "b_ada": nrm(ks[3], (L, N_SUBLAYERS * N_MOD * D), 0.02),
        "norm1_g": 1.0 + nrm(ks[4], (L, D), 0.02),
        "ffn1_w_gu": nrm(ks[5], (L, D, 2 * D_FF), D ** -0.5),
        "ffn1_w_down": nrm(ks[6], (L, D_FF, D), D_FF ** -0.5),
        "norm2_g": 1.0 + nrm(ks[7], (L, D), 0.02),
        "w_mix_in": nrm(ks[8], (L, D, MIX_IN_WIDTH), D ** -0.5),
        "b_merge": nrm(ks[9], (L, 2, D), 0.02),
        "conv_w": nrm(ks[10], (L, CONV_KSIZE, CONV_WIDTH), CONV_KSIZE ** -0.5),
        "w_conv_out": nrm(ks[11], (L, CONV_WIDTH, D), CONV_WIDTH ** -0.5),
        "w_attn_out": nrm(ks[12], (L, ATTN_WIDTH, D), ATTN_WIDTH ** -0.5),
        "w_out": nrm(ks[13], (L, D, D), D ** -0.5),
        "norm3_g": 1.0 + nrm(ks[14], (L, D), 0.02),
        "ffn2_w_gu": nrm(ks[15], (L, D, 2 * D_FF), D ** -0.5),
        "ffn2_w_down": nrm(ks[16], (L, D_FF, D), D_FF ** -0.5),
        "final_g": 1.0 + nrm(ks[17], (D,), 0.02),
    }


def _fwd_reference(x, c, w_ada, b_ada, norm1_g, ffn1_w_gu, ffn1_w_down, norm2_g,
              w_mix_in, b_merge, conv_w, w_conv_out, w_attn_out, w_out,
              norm3_g, ffn2_w_gu, ffn2_w_down, final_g):
    B = x.shape[0]
    c_act = jax.nn.silu(c)
    h = x
    for l in range(DEPTH):
        mod = (c_act @ w_ada[l] + b_ada[l]).reshape(B, N_SUBLAYERS, N_MOD, D_MODEL)
        u = modulate(rmsnorm(h, norm1_g[l]), mod[:, 0, 0], mod[:, 0, 1])
        h = h + FFN_RES_WEIGHT * mod[:, 0, 2][:, None, :] * swiglu(u, ffn1_w_gu[l], ffn1_w_down[l])
        u = modulate(rmsnorm(h, norm2_g[l]), mod[:, 1, 0], mod[:, 1, 1])
        y = mixer(u, w_mix_in[l], b_merge[l], conv_w[l], w_conv_out[l], w_attn_out[l], w_out[l])
        h = h + mod[:, 1, 2][:, None, :] * y
        u = modulate(rmsnorm(h, norm3_g[l]), mod[:, 2, 0], mod[:, 2, 1])
        h = h + FFN_RES_WEIGHT * mod[:, 2, 2][:, None, :] * swiglu(u, ffn2_w_gu[l], ffn2_w_down[l])
    return rmsnorm(h, final_g)


import jax as _jax
import jax.numpy as _jnp

TWIN_FORMAT = 'train_step'
FWD_PARAMS = ['x', 'c', 'w_ada', 'b_ada', 'norm1_g', 'ffn1_w_gu', 'ffn1_w_down', 'norm2_g', 'w_mix_in', 'b_merge', 'conv_w', 'w_conv_out', 'w_attn_out', 'w_out', 'norm3_g', 'ffn2_w_gu', 'ffn2_w_down', 'final_g']
TWIN_WEIGHTS = ['w_ada', 'b_ada', 'norm1_g', 'ffn1_w_gu', 'ffn1_w_down', 'norm2_g', 'w_mix_in', 'b_merge', 'conv_w', 'w_conv_out', 'w_attn_out', 'w_out', 'norm3_g', 'ffn2_w_gu', 'ffn2_w_down', 'final_g']
TWIN_DIFF_INPUT = 'x'
TWIN_INPUTS = ['x', 'c', 'w_ada', 'b_ada', 'norm1_g', 'ffn1_w_gu', 'ffn1_w_down', 'norm2_g', 'w_mix_in', 'b_merge', 'conv_w', 'w_conv_out', 'w_attn_out', 'w_out', 'norm3_g', 'ffn2_w_gu', 'ffn2_w_down', 'final_g', 'loss_target', 'm_w_ada', 'm_b_ada', 'm_norm1_g', 'm_ffn1_w_gu', 'm_ffn1_w_down', 'm_norm2_g', 'm_w_mix_in', 'm_b_merge', 'm_conv_w', 'm_w_conv_out', 'm_w_attn_out', 'm_w_out', 'm_norm3_g', 'm_ffn2_w_gu', 'm_ffn2_w_down', 'm_final_g', 'v_w_ada', 'v_b_ada', 'v_norm1_g', 'v_ffn1_w_gu', 'v_ffn1_w_down', 'v_norm2_g', 'v_w_mix_in', 'v_b_merge', 'v_conv_w', 'v_w_conv_out', 'v_w_attn_out', 'v_w_out', 'v_norm3_g', 'v_ffn2_w_gu', 'v_ffn2_w_down', 'v_final_g']
TWIN_OUTPUTS = ['loss', 'grad_x', 'grad_w_ada', 'grad_b_ada', 'grad_norm1_g', 'grad_ffn1_w_gu', 'grad_ffn1_w_down', 'grad_norm2_g', 'grad_w_mix_in', 'grad_b_merge', 'grad_conv_w', 'grad_w_conv_out', 'grad_w_attn_out', 'grad_w_out', 'grad_norm3_g', 'grad_ffn2_w_gu', 'grad_ffn2_w_down', 'grad_final_g', 'delta_w_ada', 'delta_b_ada', 'delta_norm1_g', 'delta_ffn1_w_gu', 'delta_ffn1_w_down', 'delta_norm2_g', 'delta_w_mix_in', 'delta_b_merge', 'delta_conv_w', 'delta_w_conv_out', 'delta_w_attn_out', 'delta_w_out', 'delta_norm3_g', 'delta_ffn2_w_gu', 'delta_ffn2_w_down', 'delta_final_g', 'new_m_w_ada', 'new_m_b_ada', 'new_m_norm1_g', 'new_m_ffn1_w_gu', 'new_m_ffn1_w_down', 'new_m_norm2_g', 'new_m_w_mix_in', 'new_m_b_merge', 'new_m_conv_w', 'new_m_w_conv_out', 'new_m_w_attn_out', 'new_m_w_out', 'new_m_norm3_g', 'new_m_ffn2_w_gu', 'new_m_ffn2_w_down', 'new_m_final_g', 'new_v_w_ada', 'new_v_b_ada', 'new_v_norm1_g', 'new_v_ffn1_w_gu', 'new_v_ffn1_w_down', 'new_v_norm2_g', 'new_v_w_mix_in', 'new_v_b_merge', 'new_v_conv_w', 'new_v_w_conv_out', 'new_v_w_attn_out', 'new_v_w_out', 'new_v_norm3_g', 'new_v_ffn2_w_gu', 'new_v_ffn2_w_down', 'new_v_final_g']
TWIN_LEAF_KINDS = {'loss': 'loss', 'grad_x': 'grad_x', 'grad_w_ada': 'grad_w', 'grad_b_ada': 'grad_w', 'grad_norm1_g': 'grad_w', 'grad_ffn1_w_gu': 'grad_w', 'grad_ffn1_w_down': 'grad_w', 'grad_norm2_g': 'grad_w', 'grad_w_mix_in': 'grad_w', 'grad_b_merge': 'grad_w', 'grad_conv_w': 'grad_w', 'grad_w_conv_out': 'grad_w', 'grad_w_attn_out': 'grad_w', 'grad_w_out': 'grad_w', 'grad_norm3_g': 'grad_w', 'grad_ffn2_w_gu': 'grad_w', 'grad_ffn2_w_down': 'grad_w', 'grad_final_g': 'grad_w', 'delta_w_ada': 'delta_w', 'delta_b_ada': 'delta_w', 'delta_norm1_g': 'delta_w', 'delta_ffn1_w_gu': 'delta_w', 'delta_ffn1_w_down': 'delta_w', 'delta_norm2_g': 'delta_w', 'delta_w_mix_in': 'delta_w', 'delta_b_merge': 'delta_w', 'delta_conv_w': 'delta_w', 'delta_w_conv_out': 'delta_w', 'delta_w_attn_out': 'delta_w', 'delta_w_out': 'delta_w', 'delta_norm3_g': 'delta_w', 'delta_ffn2_w_gu': 'delta_w', 'delta_ffn2_w_down': 'delta_w', 'delta_final_g': 'delta_w', 'new_m_w_ada': 'new_m', 'new_m_b_ada': 'new_m', 'new_m_norm1_g': 'new_m', 'new_m_ffn1_w_gu': 'new_m', 'new_m_ffn1_w_down': 'new_m', 'new_m_norm2_g': 'new_m', 'new_m_w_mix_in': 'new_m', 'new_m_b_merge': 'new_m', 'new_m_conv_w': 'new_m', 'new_m_w_conv_out': 'new_m', 'new_m_w_attn_out': 'new_m', 'new_m_w_out': 'new_m', 'new_m_norm3_g': 'new_m', 'new_m_ffn2_w_gu': 'new_m', 'new_m_ffn2_w_down': 'new_m', 'new_m_final_g': 'new_m', 'new_v_w_ada': 'new_v', 'new_v_b_ada': 'new_v', 'new_v_norm1_g': 'new_v', 'new_v_ffn1_w_gu': 'new_v', 'new_v_ffn1_w_down': 'new_v', 'new_v_norm2_g': 'new_v', 'new_v_w_mix_in': 'new_v', 'new_v_b_merge': 'new_v', 'new_v_conv_w': 'new_v', 'new_v_w_conv_out': 'new_v', 'new_v_w_attn_out': 'new_v', 'new_v_w_out': 'new_v', 'new_v_norm3_g': 'new_v', 'new_v_ffn2_w_gu': 'new_v', 'new_v_ffn2_w_down': 'new_v', 'new_v_final_g': 'new_v'}


def _forward(args):
    return _fwd_reference(*[args[k] for k in FWD_PARAMS])


def _output_shape():
    def fwd():
        inp = _fwd_setup_inputs(0)
        return _fwd_reference(*[inp[k] for k in FWD_PARAMS])
    out = _jax.eval_shape(fwd)
    return out.shape, out.dtype

N_MICROBATCH = 1
ADAM_LR = 0.001
ADAM_B1 = 0.9
ADAM_B2 = 0.999
ADAM_EPS = 1e-08
ADAM_WD = 0.01
ADAM_STEP = 10
PER_EXAMPLE_BATCH_AXIS = {'x': 0, 'c': 0, 'loss_target': 0}
SHARED_INPUTS = []
_WEIGHT_DTYPES = {'w_ada': _jnp.float32, 'b_ada': _jnp.float32, 'norm1_g': _jnp.float32, 'ffn1_w_gu': _jnp.float32, 'ffn1_w_down': _jnp.float32, 'norm2_g': _jnp.float32, 'w_mix_in': _jnp.float32, 'b_merge': _jnp.float32, 'conv_w': _jnp.float32, 'w_conv_out': _jnp.float32, 'w_attn_out': _jnp.float32, 'w_out': _jnp.float32, 'norm3_g': _jnp.float32, 'ffn2_w_gu': _jnp.float32, 'ffn2_w_down': _jnp.float32, 'final_g': _jnp.float32}
MOMENT_SCALE = {'w_ada': 4.197014e-02, 'b_ada': 7.333103e-02, 'norm1_g': 2.700039e-02, 'ffn1_w_gu': 1.197634e-02, 'ffn1_w_down': 1.956405e-02, 'norm2_g': 6.814598e-02, 'w_mix_in': 3.190746e-02, 'b_merge': 1.136475e-02, 'conv_w': 5.159439e-02, 'w_conv_out': 3.669439e-02, 'w_attn_out': 2.175773e-02, 'w_out': 4.282918e-02, 'norm3_g': 2.532103e-02, 'ffn2_w_gu': 1.129386e-02, 'ffn2_w_down': 1.841702e-02, 'final_g': 3.202841e+01}


def _to_microbatches(a, axis):
    t = _jnp.moveaxis(a, axis, 0)
    t = t.reshape((N_MICROBATCH, t.shape[0] // N_MICROBATCH) + t.shape[1:])
    return _jnp.moveaxis(t, 1, axis + 1)


def setup_inputs(seed: int = 0) -> dict:
    inp = _fwd_setup_inputs(seed)
    key = _jax.random.fold_in(_jax.random.key(seed), 7919)
    shape, _ = _output_shape()
    out = dict(inp)
    out["loss_target"] = _jax.random.normal(_jax.random.fold_in(key, 0), shape, _jnp.float32)
    for i, name in enumerate(TWIN_WEIGHTS):
        w = inp[name].astype(_jnp.float32)
        if MOMENT_SCALE is None:
            s = _jnp.sqrt(_jnp.mean(_jnp.square(w)) + 1e-30)
        else:
            s = MOMENT_SCALE[name]
        km, kv = _jax.random.split(_jax.random.fold_in(key, i + 1))
        out[name] = w
        out["m_" + name] = s * _jax.random.normal(km, w.shape, _jnp.float32)
        out["v_" + name] = (s * s) * _jax.random.uniform(kv, w.shape, _jnp.float32, 0.5, 1.5)
    if N_MICROBATCH > 1:
        for name, axis in PER_EXAMPLE_BATCH_AXIS.items():
            out[name] = _to_microbatches(out[name], axis)
    return {'x': out['x'], 'c': out['c'], 'w_ada': out['w_ada'], 'b_ada': out['b_ada'], 'norm1_g': out['norm1_g'], 'ffn1_w_gu': out['ffn1_w_gu'], 'ffn1_w_down': out['ffn1_w_down'], 'norm2_g': out['norm2_g'], 'w_mix_in': out['w_mix_in'], 'b_merge': out['b_merge'], 'conv_w': out['conv_w'], 'w_conv_out': out['w_conv_out'], 'w_attn_out': out['w_attn_out'], 'w_out': out['w_out'], 'norm3_g': out['norm3_g'], 'ffn2_w_gu': out['ffn2_w_gu'], 'ffn2_w_down': out['ffn2_w_down'], 'final_g': out['final_g'], 'loss_target': out['loss_target'], 'm_w_ada': out['m_w_ada'], 'm_b_ada': out['m_b_ada'], 'm_norm1_g': out['m_norm1_g'], 'm_ffn1_w_gu': out['m_ffn1_w_gu'], 'm_ffn1_w_down': out['m_ffn1_w_down'], 'm_norm2_g': out['m_norm2_g'], 'm_w_mix_in': out['m_w_mix_in'], 'm_b_merge': out['m_b_merge'], 'm_conv_w': out['m_conv_w'], 'm_w_conv_out': out['m_w_conv_out'], 'm_w_attn_out': out['m_w_attn_out'], 'm_w_out': out['m_w_out'], 'm_norm3_g': out['m_norm3_g'], 'm_ffn2_w_gu': out['m_ffn2_w_gu'], 'm_ffn2_w_down': out['m_ffn2_w_down'], 'm_final_g': out['m_final_g'], 'v_w_ada': out['v_w_ada'], 'v_b_ada': out['v_b_ada'], 'v_norm1_g': out['v_norm1_g'], 'v_ffn1_w_gu': out['v_ffn1_w_gu'], 'v_ffn1_w_down': out['v_ffn1_w_down'], 'v_norm2_g': out['v_norm2_g'], 'v_w_mix_in': out['v_w_mix_in'], 'v_b_merge': out['v_b_merge'], 'v_conv_w': out['v_conv_w'], 'v_w_conv_out': out['v_w_conv_out'], 'v_w_attn_out': out['v_w_attn_out'], 'v_w_out': out['v_w_out'], 'v_norm3_g': out['v_norm3_g'], 'v_ffn2_w_gu': out['v_ffn2_w_gu'], 'v_ffn2_w_down': out['v_ffn2_w_down'], 'v_final_g': out['v_final_g']}


def _loss(weights, diff, rest, loss_target):
    with _jax.named_scope("forward"):
        args = {**rest, TWIN_DIFF_INPUT: diff, **{k: w.astype(_WEIGHT_DTYPES[k]) for k, w in weights.items()}}
        y = _forward(args)
    with _jax.named_scope("loss_head"):
        err = _jnp.square(y.astype(_jnp.float32) - loss_target)
        return 0.5 * _jnp.sum(_jnp.mean(err, axis=-1)) if err.ndim else 0.5 * err


def _adamw(w, g, m, v):
    m = ADAM_B1 * m + (1.0 - ADAM_B1) * g
    v = ADAM_B2 * v + (1.0 - ADAM_B2) * _jnp.square(g)
    m_hat = m / (1.0 - ADAM_B1 ** ADAM_STEP)
    v_hat = v / (1.0 - ADAM_B2 ** ADAM_STEP)
    delta = -ADAM_LR * (m_hat / (_jnp.sqrt(v_hat) + ADAM_EPS) + ADAM_WD * w)
    return delta, m, v


def reference(x, c, w_ada, b_ada, norm1_g, ffn1_w_gu, ffn1_w_down, norm2_g, w_mix_in, b_merge, conv_w, w_conv_out, w_attn_out, w_out, norm3_g, ffn2_w_gu, ffn2_w_down, final_g, loss_target, m_w_ada, m_b_ada, m_norm1_g, m_ffn1_w_gu, m_ffn1_w_down, m_norm2_g, m_w_mix_in, m_b_merge, m_conv_w, m_w_conv_out, m_w_attn_out, m_w_out, m_norm3_g, m_ffn2_w_gu, m_ffn2_w_down, m_final_g, v_w_ada, v_b_ada, v_norm1_g, v_ffn1_w_gu, v_ffn1_w_down, v_norm2_g, v_w_mix_in, v_b_merge, v_conv_w, v_w_conv_out, v_w_attn_out, v_w_out, v_norm3_g, v_ffn2_w_gu, v_ffn2_w_down, v_final_g):
    given = dict(x=x, c=c, w_ada=w_ada, b_ada=b_ada, norm1_g=norm1_g, ffn1_w_gu=ffn1_w_gu, ffn1_w_down=ffn1_w_down, norm2_g=norm2_g, w_mix_in=w_mix_in, b_merge=b_merge, conv_w=conv_w, w_conv_out=w_conv_out, w_attn_out=w_attn_out, w_out=w_out, norm3_g=norm3_g, ffn2_w_gu=ffn2_w_gu, ffn2_w_down=ffn2_w_down, final_g=final_g, loss_target=loss_target, m_w_ada=m_w_ada, m_b_ada=m_b_ada, m_norm1_g=m_norm1_g, m_ffn1_w_gu=m_ffn1_w_gu, m_ffn1_w_down=m_ffn1_w_down, m_norm2_g=m_norm2_g, m_w_mix_in=m_w_mix_in, m_b_merge=m_b_merge, m_conv_w=m_conv_w, m_w_conv_out=m_w_conv_out, m_w_attn_out=m_w_attn_out, m_w_out=m_w_out, m_norm3_g=m_norm3_g, m_ffn2_w_gu=m_ffn2_w_gu, m_ffn2_w_down=m_ffn2_w_down, m_final_g=m_final_g, v_w_ada=v_w_ada, v_b_ada=v_b_ada, v_norm1_g=v_norm1_g, v_ffn1_w_gu=v_ffn1_w_gu, v_ffn1_w_down=v_ffn1_w_down, v_norm2_g=v_norm2_g, v_w_mix_in=v_w_mix_in, v_b_merge=v_b_merge, v_conv_w=v_conv_w, v_w_conv_out=v_w_conv_out, v_w_attn_out=v_w_attn_out, v_w_out=v_w_out, v_norm3_g=v_norm3_g, v_ffn2_w_gu=v_ffn2_w_gu, v_ffn2_w_down=v_ffn2_w_down, v_final_g=v_final_g)
    weights = {n: given[n] for n in TWIN_WEIGHTS}
    shared = {n: given[n] for n in SHARED_INPUTS}
    per_example = {n: given[n] for n in ['x', 'c']}
    grad_fn = _jax.value_and_grad(_loss, argnums=(0, 1))

    def one_microbatch(ex, loss_target):
        ex = dict(ex)
        diff = ex.pop(TWIN_DIFF_INPUT)
        return grad_fn(weights, diff, {**shared, **ex}, loss_target)

    if N_MICROBATCH == 1:
        loss, (grad_w, grad_x) = one_microbatch(per_example, given["loss_target"])
    else:
        def body(carry, xs):
            loss_sum, grad_sum = carry
            l_k, (gw_k, gx_k) = one_microbatch(xs[0], xs[1])
            with _jax.named_scope("update"):
                return (loss_sum + l_k, _jax.tree.map(_jnp.add, grad_sum, gw_k)), gx_k

        init = (_jnp.zeros((), _jnp.float32), _jax.tree.map(_jnp.zeros_like, weights))
        (loss, grad_w), grad_x = _jax.lax.scan(body, init, (per_example, given["loss_target"]))
    with _jax.named_scope("update"):
        delta_w, new_m, new_v = {}, {}, {}
        for n in TWIN_WEIGHTS:
            delta_w[n], new_m[n], new_v[n] = _adamw(weights[n], grad_w[n], given["m_" + n], given["v_" + n])
    return (loss, grad_x, *[grad_w[n] for n in TWIN_WEIGHTS], *[delta_w[n] for n in TWIN_WEIGHTS],
            *[new_m[n] for n in TWIN_WEIGHTS], *[new_v[n] for n in TWIN_WEIGHTS])
```

```python
import functools

import jax
import jax.numpy as jnp
from jax import lax
from jax.experimental import pallas as pl
from jax.experimental.pallas import tpu as pltpu

F32 = jnp.float32
BF16 = jnp.bfloat16
MESH = pl.DeviceIdType.MESH
ANY = pl.BlockSpec(memory_space=pl.ANY)

N_DEV = 8
D_MODEL = 1024
D_FF = 2816
FF_BLK = D_FF // 4
N_HEADS = 8
HEAD_DIM = 64
CONV_W = 512
ATTN_W = 512
MIX_W = 3 * CONV_W + 3 * ATTN_W + 2 * D_MODEL
MIX_BLK = MIX_W // N_DEV
EPS = 1e-6
ATTN_SCALE = HEAD_DIM ** -0.5

ADAM_LR = 0.001
ADAM_B1 = 0.9
ADAM_B2 = 0.999
ADAM_EPS = 1e-08
ADAM_WD = 0.01
ADAM_STEP = 10
ADAM_BC1 = 1.0 - ADAM_B1 ** ADAM_STEP
ADAM_BC2 = 1.0 - ADAM_B2 ** ADAM_STEP

VMEM_LIMIT = 56 * 1024 * 1024
ROW_TILE = 512
ELT_TILE = 256
ATT_BLK = 256

NN = (((1,), (0,)), ((), ()))
NT = (((1,), (1,)), ((), ()))
TN = (((0,), (0,)), ((), ()))


def _dot(a, b, dims=NN):
    return lax.dot_general(a, b, dims, preferred_element_type=F32)


def _params(*sem):
    return pltpu.CompilerParams(dimension_semantics=sem, vmem_limit_bytes=VMEM_LIMIT)


def _sigmoid(x):
    return 1.0 / (1.0 + jnp.exp(-x))


def _me():
    x, y, c = lax.axis_index("x"), lax.axis_index("y"), lax.axis_index("c")
    return x, y, c, 4 * x + 2 * y + c


def _peer(k):
    x, y, c, _ = _me()
    px = 1 - x if (k >> 2) & 1 else x
    py = 1 - y if (k >> 1) & 1 else y
    pc = 1 - c if k & 1 else c
    return (px, py, pc), 4 * px + 2 * py + pc


def all_gather_rows(v, name):
    r, n = v.shape

    def body(v_ref, out_ref, send_sems, recv_sems):
        _, _, _, me = _me()
        out_ref[me] = v_ref[...]
        copies = []
        for k in range(1, N_DEV):
            peer, _ = _peer(k)
            cp = pltpu.make_async_remote_copy(
                src_ref=v_ref, dst_ref=out_ref.at[me],
                send_sem=send_sems.at[k - 1], recv_sem=recv_sems.at[k - 1],
                device_id=peer, device_id_type=MESH)
            cp.start()
            copies.append(cp)
        for cp in copies:
            cp.wait()

    return pl.pallas_call(
        body, name=name,
        out_shape=jax.ShapeDtypeStruct((N_DEV, r, n), v.dtype),
        in_specs=[pl.BlockSpec(memory_space=pltpu.VMEM)],
        out_specs=pl.BlockSpec(memory_space=pltpu.VMEM),
        scratch_shapes=[pltpu.SemaphoreType.DMA((N_DEV - 1,)), pltpu.SemaphoreType.DMA((N_DEV - 1,))],
    )(v)


def all_gather_weights(shards, name):
    n = len(shards)

    def body(*refs):
        ins, outs = refs[:n], refs[n:2 * n]
        send_sems, recv_sems, local_sems = refs[2 * n:]
        _, _, _, me = _me()
        copies = []
        for w in range(n):
            loc = pltpu.make_async_copy(ins[w], outs[w].at[me], local_sems.at[w])
            loc.start()
            copies.append(loc)
            for k in range(1, N_DEV):
                peer, _ = _peer(k)
                cp = pltpu.make_async_remote_copy(
                    src_ref=ins[w], dst_ref=outs[w].at[me],
                    send_sem=send_sems.at[w * (N_DEV - 1) + k - 1],
                    recv_sem=recv_sems.at[w * (N_DEV - 1) + k - 1],
                    device_id=peer, device_id_type=MESH)
                cp.start()
                copies.append(cp)
        for cp in copies:
            cp.wait()

    return pl.pallas_call(
        body, name=name,
        out_shape=[jax.ShapeDtypeStruct((N_DEV,) + s.shape, s.dtype) for s in shards],
        in_specs=[ANY] * n, out_specs=[ANY] * n,
        scratch_shapes=[pltpu.SemaphoreType.DMA((n * (N_DEV - 1),)),
                        pltpu.SemaphoreType.DMA((n * (N_DEV - 1),)),
                        pltpu.SemaphoreType.DMA((n,))],
    )(*shards)


def scatter_grads(fulls, name):
    n = len(fulls)

    def body(*refs):
        ins, outs = refs[:n], refs[n:2 * n]
        send_sems, recv_sems, local_sems = refs[2 * n:]
        _, _, _, me = _me()
        copies = []
        for w in range(n):
            loc = pltpu.make_async_copy(ins[w].at[me], outs[w].at[me], local_sems.at[w])
            loc.start()
            copies.append(loc)
            for k in range(1, N_DEV):
                peer, pidx = _peer(k)
                cp = pltpu.make_async_remote_copy(
                    src_ref=ins[w].at[pidx], dst_ref=outs[w].at[me],
                    send_sem=send_sems.at[w * (N_DEV - 1) + k - 1],
                    recv_sem=recv_sems.at[w * (N_DEV - 1) + k - 1],
                    device_id=peer, device_id_type=MESH)
                cp.start()
                copies.append(cp)
        for cp in copies:
            cp.wait()

    return pl.pallas_call(
        body, name=name,
        out_shape=[jax.ShapeDtypeStruct(f.shape, f.dtype) for f in fulls],
        in_specs=[ANY] * n, out_specs=[ANY] * n,
        scratch_shapes=[pltpu.SemaphoreType.DMA((n * (N_DEV - 1),)),
                        pltpu.SemaphoreType.DMA((n * (N_DEV - 1),)),
                        pltpu.SemaphoreType.DMA((n,))],
    )(*fulls)


def ada_forward(c_all, w_ada, b_cols, name):
    n = w_ada.shape[1]

    def body(c_ref, w_ref, b_ref, o_ref):
        c = c_ref[...]
        act = c * _sigmoid(c)
        o_ref[...] = jnp.dot(act, w_ref[...], precision=lax.Precision.HIGHEST,
                             preferred_element_type=F32) + b_ref[...]

    return pl.pallas_call(
        body, name=name, out_shape=jax.ShapeDtypeStruct((N_DEV, n), F32),
        compiler_params=pltpu.CompilerParams(vmem_limit_bytes=VMEM_LIMIT),
    )(c_all, w_ada, b_cols)


def ada_backward(c_all, dmod_cols, name):
    n = dmod_cols.shape[1]

    def body(c_ref, d_ref, o_ref):
        c = c_ref[...]
        act = c * _sigmoid(c)
        o_ref[...] = lax.dot_general(act, d_ref[...], TN, precision=lax.Precision.HIGHEST,
                                     preferred_element_type=F32)

    return pl.pallas_call(
        body, name=name, out_shape=jax.ShapeDtypeStruct((D_MODEL, n), F32),
        compiler_params=pltpu.CompilerParams(vmem_limit_bytes=VMEM_LIMIT),
    )(c_all, dmod_cols)


def _row_spec(t, width=D_MODEL):
    return pl.BlockSpec((t, width), lambda i: (i, 0))


def _vec_spec(rows=1, width=D_MODEL):
    return pl.BlockSpec((rows, width), lambda i: (0, 0))


def norm_modulate(x, g, shift, scale, name):
    s = x.shape[0]
    t = min(ELT_TILE, s)

    def body(x_ref, g_ref, sh_ref, sc_ref, u_ref):
        xv = x_ref[...]
        r = lax.rsqrt(jnp.mean(xv * xv, axis=-1, keepdims=True) + EPS)
        a = (xv * r) * g_ref[...]
        u_ref[...] = (a * (1.0 + sc_ref[...]) + sh_ref[...]).astype(BF16)

    return pl.pallas_call(
        body, name=name, grid=(s // t,),
        in_specs=[_row_spec(t), _vec_spec(), _vec_spec(), _vec_spec()],
        out_specs=_row_spec(t),
        out_shape=jax.ShapeDtypeStruct((s, D_MODEL), BF16),
        compiler_params=_params("parallel"),
    )(x, g, shift, scale)


def loss_head(x, target, final_g, gate, y_prev, coef, name):
    s = x.shape[0]
    t = min(ELT_TILE, s)

    def body(x_ref, t_ref, fg_ref, gt_ref, y_ref, dx_ref, dy_ref, dgt_ref, dfg_ref, sq_ref):
        @pl.when(pl.program_id(0) == 0)
        def _():
            dgt_ref[...] = jnp.zeros_like(dgt_ref)
            dfg_ref[...] = jnp.zeros_like(dfg_ref)
            sq_ref[...] = jnp.zeros_like(sq_ref)

        xv = x_ref[...]
        fg = fg_ref[...]
        r = lax.rsqrt(jnp.mean(xv * xv, axis=-1, keepdims=True) + EPS)
        nrm = xv * r
        err = nrm * fg - t_ref[...]
        sq_ref[...] += jnp.sum(err * err, axis=0, keepdims=True)
        dout = err * (1.0 / D_MODEL)
        dfg_ref[...] += jnp.sum(dout * nrm, axis=0, keepdims=True)
        dn = dout * fg
        dx = r * (dn - nrm * jnp.mean(dn * nrm, axis=-1, keepdims=True))
        dx_ref[...] = dx
        dy_ref[...] = (coef * gt_ref[...] * dx).astype(BF16)
        dgt_ref[...] += coef * jnp.sum(dx * y_ref[...].astype(F32), axis=0, keepdims=True)

    vec = jax.ShapeDtypeStruct((1, D_MODEL), F32)
    return pl.pallas_call(
        body, name=name, grid=(s // t,),
        in_specs=[_row_spec(t), _row_spec(t), _vec_spec(), _vec_spec(), _row_spec(t)],
        out_specs=[_row_spec(t), _row_spec(t), _vec_spec(), _vec_spec(), _vec_spec()],
        out_shape=[jax.ShapeDtypeStruct((s, D_MODEL), F32), jax.ShapeDtypeStruct((s, D_MODEL), BF16),
                   vec, vec, vec],
        compiler_params=_params("arbitrary"),
    )(x, target, final_g, gate, y_prev)


def norm_modulate_bwd(du, x, dx_out, g, scale, name, prev=None):
    s = x.shape[0]
    t = min(ELT_TILE, s)
    has_prev = prev is not None

    def body(*refs):
        du_ref, x_ref, dxo_ref, g_ref, sc_ref = refs[:5]
        refs = refs[5:]
        if has_prev:
            gt_ref, y_ref = refs[:2]
            refs = refs[2:]
        dx_ref, dsh_ref, dsc_ref, dg_ref = refs[:4]

        @pl.when(pl.program_id(0) == 0)
        def _():
            dsh_ref[...] = jnp.zeros_like(dsh_ref)
            dsc_ref[...] = jnp.zeros_like(dsc_ref)
            dg_ref[...] = jnp.zeros_like(dg_ref)
            if has_prev:
                refs[5][...] = jnp.zeros_like(refs[5])

        xv = x_ref[...]
        duv = du_ref[...]
        gv = g_ref[...]
        r = lax.rsqrt(jnp.mean(xv * xv, axis=-1, keepdims=True) + EPS)
        nrm = xv * r
        a = nrm * gv
        dsh_ref[...] += jnp.sum(duv, axis=0, keepdims=True)
        dsc_ref[...] += jnp.sum(duv * a, axis=0, keepdims=True)
        da = duv * (1.0 + sc_ref[...])
        dg_ref[...] += jnp.sum(da * nrm, axis=0, keepdims=True)
        dn = da * gv
        dx = dxo_ref[...] + r * (dn - nrm * jnp.mean(dn * nrm, axis=-1, keepdims=True))
        dx_ref[...] = dx
        if has_prev:
            coef = prev[2]
            refs[4][...] = (coef * gt_ref[...] * dx).astype(BF16)
            refs[5][...] += coef * jnp.sum(dx * y_ref[...].astype(F32), axis=0, keepdims=True)

    vec = jax.ShapeDtypeStruct((1, D_MODEL), F32)
    operands = [du, x, dx_out, g, scale]
    in_specs = [_row_spec(t), _row_spec(t), _row_spec(t), _vec_spec(), _vec_spec()]
    out_specs = [_row_spec(t), _vec_spec(), _vec_spec(), _vec_spec()]
    out_shape = [jax.ShapeDtypeStruct((s, D_MODEL), F32), vec, vec, vec]
    if has_prev:
        operands += [prev[0], prev[1]]
        in_specs += [_vec_spec(), _row_spec(t)]
        out_specs += [_row_spec(t), _vec_spec()]
        out_shape += [jax.ShapeDtypeStruct((s, D_MODEL), BF16), vec]
    return pl.pallas_call(
        body, name=name, grid=(s // t,), in_specs=in_specs, out_specs=out_specs,
        out_shape=out_shape, compiler_params=_params("arbitrary"),
    )(*operands)


def ffn_up(u, w_gu, name):
    s = u.shape[0]
    t = min(ROW_TILE, s)

    def body(u_ref, wg_ref, wu_ref, gu_ref, act_ref):
        uv = u_ref[...]
        g = _dot(uv, wg_ref[...])
        up = _dot(uv, wu_ref[...])
        gu_ref[0] = g.astype(BF16)
        gu_ref[1] = up.astype(BF16)
        act_ref[...] = (g * _sigmoid(g) * up).astype(BF16)

    return pl.pallas_call(
        body, name=name, grid=(4, s // t),
        in_specs=[pl.BlockSpec((t, D_MODEL), lambda j, i: (i, 0)),
                  pl.BlockSpec((None, D_MODEL, FF_BLK), lambda j, i: (j, 0, 0)),
                  pl.BlockSpec((None, D_MODEL, FF_BLK), lambda j, i: (j + 4, 0, 0))],
        out_specs=[pl.BlockSpec((2, None, t, FF_BLK), lambda j, i: (0, j, i, 0)),
                   pl.BlockSpec((None, t, FF_BLK), lambda j, i: (j, i, 0))],
        out_shape=[jax.ShapeDtypeStruct((2, 4, s, FF_BLK), BF16),
                   jax.ShapeDtypeStruct((4, s, FF_BLK), BF16)],
        compiler_params=_params("parallel", "parallel"),
    )(u, w_gu, w_gu)


def residual_matmul(a, b, x, gate, coef, name):
    nk, s, kb = a.shape
    t = min(ROW_TILE, s)

    def body(a_ref, b_ref, x_ref, gt_ref, xo_ref, y_ref, acc_ref):
        k = pl.program_id(1)

        @pl.when(k == 0)
        def _():
            acc_ref[...] = jnp.zeros_like(acc_ref)

        acc_ref[...] += _dot(a_ref[...], b_ref[...])

        @pl.when(k == nk - 1)
        def _():
            y = acc_ref[...]
            y_ref[...] = y.astype(BF16)
            xo_ref[...] = x_ref[...] + coef * gt_ref[...] * y

    return pl.pallas_call(
        body, name=name, grid=(s // t, nk),
        in_specs=[pl.BlockSpec((None, t, kb), lambda i, k: (k, i, 0)),
                  pl.BlockSpec((None, kb, D_MODEL), lambda i, k: (k, 0, 0)),
                  pl.BlockSpec((t, D_MODEL), lambda i, k: (i, 0)),
                  pl.BlockSpec((1, D_MODEL), lambda i, k: (0, 0))],
        out_specs=[pl.BlockSpec((t, D_MODEL), lambda i, k: (i, 0)),
                   pl.BlockSpec((t, D_MODEL), lambda i, k: (i, 0))],
        out_shape=[jax.ShapeDtypeStruct((s, D_MODEL), F32), jax.ShapeDtypeStruct((s, D_MODEL), BF16)],
        scratch_shapes=[pltpu.VMEM((t, D_MODEL), F32)],
        compiler_params=_params("parallel", "arbitrary"),
    )(a, b, x, gate)


def ffn_dact(dy, w_down, gu, name):
    s = dy.shape[0]
    t = min(ROW_TILE, s)

    def body(dy_ref, w_ref, gu_ref, dgu_ref):
        dact = _dot(dy_ref[...], w_ref[...], NT)
        g = gu_ref[0].astype(F32)
        up = gu_ref[1].astype(F32)
        sg = _sigmoid(g)
        dgu_ref[0] = (dact * up * sg * (1.0 + g * (1.0 - sg))).astype(BF16)
        dgu_ref[1] = (dact * g * sg).astype(BF16)

    return pl.pallas_call(
        body, name=name, grid=(4, s // t),
        in_specs=[pl.BlockSpec((t, D_MODEL), lambda j, i: (i, 0)),
                  pl.BlockSpec((None, FF_BLK, D_MODEL), lambda j, i: (j, 0, 0)),
                  pl.BlockSpec((2, None, t, FF_BLK), lambda j, i: (0, j, i, 0))],
        out_specs=pl.BlockSpec((2, None, t, FF_BLK), lambda j, i: (0, j, i, 0)),
        out_shape=jax.ShapeDtypeStruct((2, 4, s, FF_BLK), BF16),
        compiler_params=_params("parallel", "parallel"),
    )(dy, w_down, gu)


def matmul_nt_acc(a, b, name):
    nk, s, n = a.shape
    d = b.shape[1]
    t = min(ROW_TILE, s)

    def body(a_ref, b_ref, o_ref, acc_ref):
        k = pl.program_id(1)

        @pl.when(k == 0)
        def _():
            acc_ref[...] = jnp.zeros_like(acc_ref)

        acc_ref[...] += _dot(a_ref[...], b_ref[...], NT)

        @pl.when(k == nk - 1)
        def _():
            o_ref[...] = acc_ref[...]

    return pl.pallas_call(
        body, name=name, grid=(s // t, nk),
        in_specs=[pl.BlockSpec((None, t, n), lambda i, k: (k, i, 0)),
                  pl.BlockSpec((None, d, n), lambda i, k: (k, 0, 0))],
        out_specs=pl.BlockSpec((t, d), lambda i, k: (i, 0)),
        out_shape=jax.ShapeDtypeStruct((s, d), F32),
        scratch_shapes=[pltpu.VMEM((t, d), F32)],
        compiler_params=_params("parallel", "arbitrary"),
    )(a, b)


def matmul_tn(a, b, name):
    ja, s, m = a.shape
    jb, _, n = b.shape
    t = min(ROW_TILE, s)
    nk = s // t

    def body(a_ref, b_ref, o_ref, acc_ref):
        k = pl.program_id(2)

        @pl.when(k == 0)
        def _():
            acc_ref[...] = jnp.zeros_like(acc_ref)

        acc_ref[...] += _dot(a_ref[...], b_ref[...], TN)

        @pl.when(k == nk - 1)
        def _():
            o_ref[...] = acc_ref[...].astype(BF16)

    return pl.pallas_call(
        body, name=name, grid=(ja, jb, nk),
        in_specs=[pl.BlockSpec((None, t, m), lambda p, q, k: (p, k, 0)),
                  pl.BlockSpec((None, t, n), lambda p, q, k: (q, k, 0))],
        out_specs=pl.BlockSpec((None, None, m, n), lambda p, q, k: (p, q, 0, 0)),
        out_shape=jax.ShapeDtypeStruct((ja, jb, m, n), BF16),
        scratch_shapes=[pltpu.VMEM((m, n), F32)],
        compiler_params=_params("parallel", "parallel", "arbitrary"),
    )(a, b)


def mix_in_proj(u, w_mix, name):
    s = u.shape[0]
    t = min(ROW_TILE, s)

    def body(u_ref, w_ref, o_ref):
        o_ref[...] = _dot(u_ref[...], w_ref[...]).astype(BF16)

    return pl.pallas_call(
        body, name=name, grid=(N_DEV, s // t),
        in_specs=[pl.BlockSpec((t, D_MODEL), lambda j, i: (i, 0)),
                  pl.BlockSpec((None, D_MODEL, MIX_BLK), lambda j, i: (j, 0, 0))],
        out_specs=pl.BlockSpec((t, MIX_BLK), lambda j, i: (i, j)),
        out_shape=jax.ShapeDtypeStruct((s, MIX_W), BF16),
        compiler_params=_params("parallel", "parallel"),
    )(u, w_mix)


def _conv_taps(cc_ref, cx_ref, s):
    v = cc_ref[...].astype(F32) * cx_ref[...].astype(F32)
    tok = lax.broadcasted_iota(jnp.int32, v.shape, 0)
    v1 = jnp.where(tok >= 1, pltpu.roll(v, 1, 0), 0.0)
    v2 = jnp.where(tok >= 2, pltpu.roll(v, 2, 0), 0.0)
    return v, v1, v2, tok


def _proj_cols(s, first):
    return pl.BlockSpec((s, 128), lambda j: (0, first + j))


def short_conv(proj, conv_w, name):
    s = proj.shape[0]

    def body(cb_ref, cc_ref, cx_ref, w_ref, o_ref):
        v, v1, v2, _ = _conv_taps(cc_ref, cx_ref, s)
        y = w_ref[0:1, :] * v2 + w_ref[1:2, :] * v1 + w_ref[2:3, :] * v
        o_ref[...] = (cb_ref[...].astype(F32) * y).astype(BF16)

    return pl.pallas_call(
        body, name=name, grid=(CONV_W // 128,),
        in_specs=[_proj_cols(s, 0), _proj_cols(s, 4), _proj_cols(s, 8),
                  pl.BlockSpec((3, 128), lambda j: (0, j))],
        out_specs=pl.BlockSpec((s, 128), lambda j: (0, j)),
        out_shape=jax.ShapeDtypeStruct((s, CONV_W), BF16),
        compiler_params=_params("parallel"),
    )(proj, proj, proj, conv_w)


def short_conv_bwd(dsa, proj, conv_w, name):
    s = proj.shape[0]

    def body(dsa_ref, cb_ref, cc_ref, cx_ref, w_ref, dcb_ref, dcc_ref, dcx_ref, dw_ref):
        v, v1, v2, tok = _conv_taps(cc_ref, cx_ref, s)
        w0, w1, w2 = w_ref[0:1, :], w_ref[1:2, :], w_ref[2:3, :]
        y = w0 * v2 + w1 * v1 + w2 * v
        dsa_v = dsa_ref[...].astype(F32)
        dcb_ref[...] = (dsa_v * y).astype(BF16)
        dy = dsa_v * cb_ref[...].astype(F32)
        dw_ref[0:1, :] = jnp.sum(dy * v2, axis=0, keepdims=True)
        dw_ref[1:2, :] = jnp.sum(dy * v1, axis=0, keepdims=True)
        dw_ref[2:3, :] = jnp.sum(dy * v, axis=0, keepdims=True)
        dy1 = jnp.where(tok < s - 1, pltpu.roll(dy, s - 1, 0), 0.0)
        dy2 = jnp.where(tok < s - 2, pltpu.roll(dy, s - 2, 0), 0.0)
        dv = w2 * dy + w1 * dy1 + w0 * dy2
        dcc_ref[...] = (dv * cx_ref[...].astype(F32)).astype(BF16)
        dcx_ref[...] = (dv * cc_ref[...].astype(F32)).astype(BF16)

    col = pl.BlockSpec((s, 128), lambda j: (0, j))
    act = jax.ShapeDtypeStruct((s, CONV_W), BF16)
    return pl.pallas_call(
        body, name=name, grid=(CONV_W // 128,),
        in_specs=[col, _proj_cols(s, 0), _proj_cols(s, 4), _proj_cols(s, 8),
                  pl.BlockSpec((3, 128), lambda j: (0, j))],
        out_specs=[col, col, col, pl.BlockSpec((3, 128), lambda j: (0, j))],
        out_shape=[act, act, act, jax.ShapeDtypeStruct((3, CONV_W), F32)],
        compiler_params=_params("parallel"),
    )(dsa, proj, proj, proj, conv_w)


def _gate_specs(t):
    return [pl.BlockSpec((t, D_MODEL), lambda i: (i, 3)), pl.BlockSpec((t, D_MODEL), lambda i: (i, 4))]


def merge_forward(sa, o, proj, w_co, w_ao, b_merge, name):
    s = sa.shape[0]
    t = min(ROW_TILE, s)

    def body(sa_ref, o_ref, ga_ref, gb_ref, wco_ref, wao_ref, bm_ref, mg_ref, ya_ref, yb_ref):
        ya = _dot(sa_ref[...], wco_ref[...])
        yb = _dot(o_ref[...], wao_ref[...])
        sga = _sigmoid(ga_ref[...].astype(F32) + bm_ref[0:1, :])
        sgb = _sigmoid(gb_ref[...].astype(F32) + bm_ref[1:2, :])
        mg_ref[...] = (sga * ya + sgb * yb).astype(BF16)
        ya_ref[...] = ya.astype(BF16)
        yb_ref[...] = yb.astype(BF16)

    act = jax.ShapeDtypeStruct((s, D_MODEL), BF16)
    return pl.pallas_call(
        body, name=name, grid=(s // t,),
        in_specs=[_row_spec(t, CONV_W), _row_spec(t, ATTN_W)] + _gate_specs(t)
        + [_vec_spec(CONV_W), _vec_spec(ATTN_W), _vec_spec(2)],
        out_specs=[_row_spec(t)] * 3, out_shape=[act, act, act],
        compiler_params=_params("parallel"),
    )(sa, o, proj, proj, w_co, w_ao, b_merge)


def merge_backward(dy, w_out, proj, ya, yb, b_merge, name):
    s = dy.shape[0]
    t = min(ROW_TILE, s)

    def body(dy_ref, w_ref, ga_ref, gb_ref, ya_ref, yb_ref, bm_ref,
             dya_ref, dyb_ref, dga_ref, dgb_ref, dbm_ref):
        @pl.when(pl.program_id(0) == 0)
        def _():
            dbm_ref[...] = jnp.zeros_like(dbm_ref)

        dmg = _dot(dy_ref[...], w_ref[...], NT)
        sga = _sigmoid(ga_ref[...].astype(F32) + bm_ref[0:1, :])
        sgb = _sigmoid(gb_ref[...].astype(F32) + bm_ref[1:2, :])
        dya_ref[...] = (dmg * sga).astype(BF16)
        dyb_ref[...] = (dmg * sgb).astype(BF16)
        dga = dmg * ya_ref[...].astype(F32) * sga * (1.0 - sga)
        dgb = dmg * yb_ref[...].astype(F32) * sgb * (1.0 - sgb)
        dga_ref[...] = dga.astype(BF16)
        dgb_ref[...] = dgb.astype(BF16)
        dbm_ref[0:1, :] += jnp.sum(dga, axis=0, keepdims=True)
        dbm_ref[1:2, :] += jnp.sum(dgb, axis=0, keepdims=True)

    act = jax.ShapeDtypeStruct((s, D_MODEL), BF16)
    return pl.pallas_call(
        body, name=name, grid=(s // t,),
        in_specs=[_row_spec(t), _vec_spec(D_MODEL)] + _gate_specs(t)
        + [_row_spec(t), _row_spec(t), _vec_spec(2)],
        out_specs=[_row_spec(t)] * 4 + [_vec_spec(2)],
        out_shape=[act] * 4 + [jax.ShapeDtypeStruct((2, D_MODEL), F32)],
        compiler_params=_params("arbitrary"),
    )(dy, w_out, proj, proj, ya, yb, b_merge)


def out_proj_bwd(dya, dyb, w_co, w_ao, name):
    s = dya.shape[0]
    t = min(ROW_TILE, s)

    def body(dya_ref, dyb_ref, wco_ref, wao_ref, dsa_ref, do_ref):
        dsa_ref[...] = _dot(dya_ref[...], wco_ref[...], NT).astype(BF16)
        do_ref[...] = _dot(dyb_ref[...], wao_ref[...], NT).astype(BF16)

    return pl.pallas_call(
        body, name=name, grid=(s // t,),
        in_specs=[_row_spec(t), _row_spec(t), _vec_spec(CONV_W), _vec_spec(ATTN_W)],
        out_specs=[_row_spec(t, CONV_W), _row_spec(t, ATTN_W)],
        out_shape=[jax.ShapeDtypeStruct((s, CONV_W), BF16), jax.ShapeDtypeStruct((s, ATTN_W), BF16)],
        compiler_params=_params("parallel"),
    )(dya, dyb, w_co, w_ao)


def _softplus(z):
    return jnp.maximum(z, 0.0) + jnp.log(1.0 + jnp.exp(-jnp.abs(z)))


def _tri_sum(x, tri):
    hi = x.astype(BF16)
    lo = (x - hi.astype(F32)).astype(BF16)
    return _dot(hi, tri) + _dot(lo, tri)


def _head_masks():
    lane = lax.broadcasted_iota(jnp.int32, (ATT_BLK, 2 * HEAD_DIM), 1)
    return lane < HEAD_DIM, lane >= HEAD_DIM


def stick_breaking_fwd(proj, name):
    s = proj.shape[0]
    blk = ATT_BLK
    nq = s // blk

    def body(q_ref, k_ref, v_ref, o_ref, tot_ref):
        i = pl.program_id(1)
        row = lax.broadcasted_iota(jnp.int32, (blk, blk), 0)
        col = lax.broadcasted_iota(jnp.int32, (blk, blk), 1)
        tri = (row >= col).astype(BF16)
        causal = col < row
        q_all = q_ref[...] * ATTN_SCALE
        accs, tots = [], []
        for mask in _head_masks():
            q = jnp.where(mask, q_all, 0.0).astype(BF16)

            def step(j, carry, diagonal, q=q):
                later, acc = carry
                rows = pl.ds(pl.multiple_of(j * blk, blk), blk)
                z = _dot(q, k_ref[rows, :], NT)
                sp = _softplus(z)
                if diagonal:
                    sp = jnp.where(causal, sp, 0.0)
                a = jnp.exp(z - (_tri_sum(sp, tri) + later))
                if diagonal:
                    a = jnp.where(causal, a, 0.0)
                acc = acc + _dot(a.astype(BF16), v_ref[rows, :])
                return later + jnp.sum(sp, axis=1, keepdims=True), acc

            carry = (jnp.zeros((blk, 1), F32), jnp.zeros((blk, 2 * HEAD_DIM), F32))
            carry = step(i, carry, True)
            carry = lax.fori_loop(0, i, lambda n, c: step(i - 1 - n, c, False), carry)
            tots.append(carry[0])
            accs.append(carry[1])
        first, _ = _head_masks()
        o_ref[...] = jnp.where(first, accs[0], accs[1]).astype(BF16)
        tot_ref[...] = jnp.where(first, tots[0], tots[1])

    pair = 2 * HEAD_DIM
    return pl.pallas_call(
        body, name=name, grid=(N_HEADS // 2, nq),
        in_specs=[pl.BlockSpec((blk, pair), lambda h, i: (i, 12 + h)),
                  pl.BlockSpec((s, pair), lambda h, i: (0, 16 + h)),
                  pl.BlockSpec((s, pair), lambda h, i: (0, 20 + h))],
        out_specs=[pl.BlockSpec((blk, pair), lambda h, i: (i, h)),
                   pl.BlockSpec((blk, pair), lambda h, i: (i, h))],
        out_shape=[jax.ShapeDtypeStruct((s, ATTN_W), BF16), jax.ShapeDtypeStruct((s, ATTN_W), F32)],
        compiler_params=_params("parallel", "arbitrary"),
    )(proj, proj, proj)


def stick_breaking_bwd(proj, do, tot, name):
    s = proj.shape[0]
    blk = ATT_BLK
    nq = s // blk

    def body(q_ref, k_ref, v_ref, do_ref, tot_ref, dq_ref, dk_ref, dv_ref):
        i = pl.program_id(1)

        @pl.when(i == 0)
        def _():
            dk_ref[...] = jnp.zeros_like(dk_ref)
            dv_ref[...] = jnp.zeros_like(dv_ref)

        row = lax.broadcasted_iota(jnp.int32, (blk, blk), 0)
        col = lax.broadcasted_iota(jnp.int32, (blk, blk), 1)
        before = (row < col).astype(BF16)
        upto = (row <= col).astype(BF16)
        causal = col < row
        q_all = q_ref[...] * ATTN_SCALE
        do_all = do_ref[...]
        tot_all = tot_ref[...]
        dqs = []
        for mask in _head_masks():
            q = jnp.where(mask, q_all, 0.0).astype(BF16)
            dov = jnp.where(mask, do_all, 0.0).astype(BF16)
            total = jnp.max(jnp.where(mask, tot_all, 0.0), axis=1, keepdims=True)

            def step(j, carry, diagonal, q=q, dov=dov, total=total):
                earlier, g_sum, dq = carry
                rows = pl.ds(pl.multiple_of(j * blk, blk), blk)
                kb = k_ref[rows, :]
                vb = v_ref[rows, :]
                z = _dot(q, kb, NT)
                sp = _softplus(z)
                if diagonal:
                    sp = jnp.where(causal, sp, 0.0)
                c = (total - earlier) - _tri_sum(sp, before)
                a = jnp.exp(z - c)
                if diagonal:
                    a = jnp.where(causal, a, 0.0)
                g = a * _dot(dov, vb, NT)
                f = g_sum + _tri_sum(g, upto)
                dz = g - jnp.exp(z - sp) * f
                if diagonal:
                    dz = jnp.where(causal, dz, 0.0)
                dzb = dz.astype(BF16)
                dq = dq + _dot(dzb, kb)
                dk_ref[rows, :] += _dot(dzb, q, TN)
                dv_ref[rows, :] += _dot(a.astype(BF16), dov, TN)
                return (earlier + jnp.sum(sp, axis=1, keepdims=True),
                        g_sum + jnp.sum(g, axis=1, keepdims=True), dq)

            carry = (jnp.zeros((blk, 1), F32), jnp.zeros((blk, 1), F32),
                     jnp.zeros((blk, 2 * HEAD_DIM), F32))
            carry = lax.fori_loop(0, i, lambda j, c: step(j, c, False), carry)
            carry = step(i, carry, True)
            dqs.append(carry[2])
        first, _ = _head_masks()
        dq_ref[...] = (jnp.where(first, dqs[0], dqs[1]) * ATTN_SCALE).astype(BF16)

    pair = 2 * HEAD_DIM
    blk_spec = pl.BlockSpec((blk, pair), lambda h, i: (i, h))
    full_spec = pl.BlockSpec((s, pair), lambda h, i: (0, h))
    return pl.pallas_call(
        body, name=name, grid=(N_HEADS // 2, nq),
        in_specs=[pl.BlockSpec((blk, pair), lambda h, i: (i, 12 + h)),
                  pl.BlockSpec((s, pair), lambda h, i: (0, 16 + h)),
                  pl.BlockSpec((s, pair), lambda h, i: (0, 20 + h)),
                  blk_spec, blk_spec],
        out_specs=[blk_spec, full_spec, full_spec],
        out_shape=[jax.ShapeDtypeStruct((s, ATTN_W), BF16), jax.ShapeDtypeStruct((s, ATTN_W), F32),
                   jax.ShapeDtypeStruct((s, ATTN_W), F32)],
        compiler_params=_params("parallel", "arbitrary"),
    )(proj, proj, proj, do, tot)


def adamw(w, m, v, parts, name):
    r, c = w.shape
    p = parts.shape[0]
    t = r
    for cand in (256, 176):
        if r % cand == 0 and r > cand:
            t = cand
            break

    def body(w_ref, m_ref, v_ref, p_ref, g_ref, d_ref, mo_ref, vo_ref):
        g = p_ref[0].astype(F32)
        for n in range(1, p):
            g = g + p_ref[n].astype(F32)
        m_new = ADAM_B1 * m_ref[...] + (1.0 - ADAM_B1) * g
        v_new = ADAM_B2 * v_ref[...] + (1.0 - ADAM_B2) * (g * g)
        m_hat = m_new / ADAM_BC1
        v_hat = v_new / ADAM_BC2
        g_ref[...] = g
        d_ref[...] = -ADAM_LR * (m_hat / (jnp.sqrt(v_hat) + ADAM_EPS) + ADAM_WD * w_ref[...])
        mo_ref[...] = m_new
        vo_ref[...] = v_new

    spec = pl.BlockSpec((t, c), lambda i: (i, 0))
    out = jax.ShapeDtypeStruct((r, c), F32)
    return pl.pallas_call(
        body, name=name, grid=(r // t,),
        in_specs=[spec, spec, spec, pl.BlockSpec((p, t, c), lambda i: (0, i, 0))],
        out_specs=[spec] * 4, out_shape=[out] * 4,
        compiler_params=_params("parallel"),
    )(w, m, v, parts)


def _ffn_backward(dy, u, gu, act, w_gu, w_down, tag):
    s = dy.shape[0]
    dgu = ffn_dact(dy, w_down, gu, f"ffn_dact_{tag}")
    dgu8 = dgu.reshape(8, s, FF_BLK)
    g_down = matmul_tn(act, dy[None], f"grad_w_down_{tag}").reshape(4, FF_BLK, D_MODEL)
    du = matmul_nt_acc(dgu8, w_gu, f"ffn_du_{tag}")
    g_gu = matmul_tn(u[None], dgu8, f"grad_w_gu_{tag}").reshape(8, D_MODEL, FF_BLK)
    return du, g_gu, g_down


def kernel(x, c, w_ada, b_ada, norm1_g, ffn1_w_gu, ffn1_w_down, norm2_g, w_mix_in, b_merge, conv_w, w_conv_out, w_attn_out, w_out, norm3_g, ffn2_w_gu, ffn2_w_down, final_g, loss_target, m_w_ada, m_b_ada, m_norm1_g, m_ffn1_w_gu, m_ffn1_w_down, m_norm2_g, m_w_mix_in, m_b_merge, m_conv_w, m_w_conv_out, m_w_attn_out, m_w_out, m_norm3_g, m_ffn2_w_gu, m_ffn2_w_down, m_final_g, v_w_ada, v_b_ada, v_norm1_g, v_ffn1_w_gu, v_ffn1_w_down, v_norm2_g, v_w_mix_in, v_b_merge, v_conv_w, v_w_conv_out, v_w_attn_out, v_w_out, v_norm3_g, v_ffn2_w_gu, v_ffn2_w_down, v_final_g):
    s = x.shape[1]
    me = 4 * lax.axis_index("x") + 2 * lax.axis_index("y") + lax.axis_index("c")
    x0 = x[0]
    target = loss_target[0]
    final_g2 = final_g.reshape(1, D_MODEL)

    shard_names = ["ffn1_w_gu", "ffn1_w_down", "w_mix_in", "w_conv_out", "w_attn_out", "w_out",
                   "ffn2_w_gu", "ffn2_w_down"]
    shards = [ffn1_w_gu[0], ffn1_w_down[0], w_mix_in[0], w_conv_out[0], w_attn_out[0], w_out[0],
              ffn2_w_gu[0], ffn2_w_down[0]]
    gathered = dict(zip(shard_names, all_gather_weights([a.astype(BF16) for a in shards], "gather_weights")))
    wgu1 = gathered["ffn1_w_gu"]
    wd1 = gathered["ffn1_w_down"].reshape(4, FF_BLK, D_MODEL)
    wmix = gathered["w_mix_in"]
    wco = gathered["w_conv_out"].transpose(1, 0, 2).reshape(CONV_W, D_MODEL)
    wao = gathered["w_attn_out"].transpose(1, 0, 2).reshape(ATTN_W, D_MODEL)
    wout = gathered["w_out"].reshape(D_MODEL, D_MODEL)
    wgu3 = gathered["ffn2_w_gu"]
    wd3 = gathered["ffn2_w_down"].reshape(4, FF_BLK, D_MODEL)

    small_in = jnp.concatenate([c.reshape(-1), b_merge.reshape(-1), conv_w.reshape(-1),
                                jnp.zeros((64,), F32)]).reshape(1, -1)
    small_all = all_gather_rows(small_in, "gather_small")[:, 0, :]
    c_all = small_all[:, :D_MODEL]
    bm_full = small_all[:, 1024:1280].reshape(8, 2, 128).transpose(1, 0, 2).reshape(2, D_MODEL)
    cw_full = small_all[:, 1280:1472].reshape(8, 3, 64).transpose(1, 0, 2).reshape(3, CONV_W)
    n_ada = w_ada.shape[2]
    b_cols = lax.dynamic_slice(b_ada, (0, me * n_ada), (1, n_ada))
    mod_part = ada_forward(c_all, w_ada[0], b_cols, "ada_forward")
    mod_all = all_gather_rows(mod_part, "gather_mod")
    mod = lax.dynamic_index_in_dim(mod_all, me, axis=1, keepdims=False).reshape(9, 1, D_MODEL)
    sh1, sc1, gt1, sh2, sc2, gt2, sh3, sc3, gt3 = [mod[n] for n in range(9)]

    u1 = norm_modulate(x0, norm1_g, sh1, sc1, "norm_mod_1")
    gu1, act1 = ffn_up(u1, wgu1, "ffn_up_1")
    x1, y1 = residual_matmul(act1, wd1, x0, gt1, 0.5, "ffn_down_1")

    u2 = norm_modulate(x1, norm2_g, sh2, sc2, "norm_mod_2")
    proj = mix_in_proj(u2, wmix, "mix_in")
    sa = short_conv(proj, cw_full, "short_conv")
    o, tot = stick_breaking_fwd(proj, "attn_fwd")
    merged, ya, yb = merge_forward(sa, o, proj, wco, wao, bm_full, "merge")
    x2, y2 = residual_matmul(merged[None], wout[None], x1, gt2, 1.0, "out_proj")

    u3 = norm_modulate(x2, norm3_g, sh3, sc3, "norm_mod_3")
    gu3, act3 = ffn_up(u3, wgu3, "ffn_up_3")
    x3, y3 = residual_matmul(act3, wd3, x2, gt3, 0.5, "ffn_down_3")

    dx3, dy3, dgt3, dfinal, sq = loss_head(x3, target, final_g2, gt3, y3, 0.5, "loss_head")
    du3, g_wgu3, g_wd3 = _ffn_backward(dy3, u3, gu3, act3, wgu3, wd3, "3")
    dx2, dsh3, dsc3, dn3, dy2, dgt2 = norm_modulate_bwd(du3, x2, dx3, norm3_g, sc3, "norm_bwd_3",
                                                        prev=(gt2, y2, 1.0))

    dya, dyb, dga, dgb, dbm = merge_backward(dy2, wout, proj, ya, yb, bm_full, "merge_bwd")
    g_wout = matmul_tn(merged[None], dy2[None], "grad_w_out").reshape(N_DEV, D_MODEL // N_DEV, D_MODEL)
    dsa, do = out_proj_bwd(dya, dyb, wco, wao, "out_proj_bwd")
    g_wco = matmul_tn(sa[None], dya[None], "grad_w_conv_out").reshape(CONV_W, N_DEV, 128).transpose(1, 0, 2)
    g_wao = matmul_tn(o[None], dyb[None], "grad_w_attn_out").reshape(ATTN_W, N_DEV, 128).transpose(1, 0, 2)
    dcb, dcc, dcx, dconv = short_conv_bwd(dsa, proj, cw_full, "short_conv_bwd")
    dq, dk, dv = stick_breaking_bwd(proj, do, tot, "attn_bwd")
    dproj = jnp.concatenate([dcb, dcc, dcx, dq, dk.astype(BF16), dv.astype(BF16), dga, dgb], axis=1)
    dproj8 = dproj.reshape(s, N_DEV, MIX_BLK).transpose(1, 0, 2)
    du2 = matmul_nt_acc(dproj8, wmix, "mix_in_du")
    g_wmix = matmul_tn(u2[None], dproj8, "grad_w_mix_in").reshape(N_DEV, D_MODEL, MIX_BLK)
    dx1, dsh2, dsc2, dn2, dy1, dgt1 = norm_modulate_bwd(du2, x1, dx2, norm2_g, sc2, "norm_bwd_2",
                                                        prev=(gt1, y1, 0.5))

    du1, g_wgu1, g_wd1 = _ffn_backward(dy1, u1, gu1, act1, wgu1, wd1, "1")
    grad_x, dsh1, dsc1, dn1 = norm_modulate_bwd(du1, x0, dx1, norm1_g, sc1, "norm_bwd_1")

    loss_local = (0.5 / D_MODEL) * jnp.sum(sq)
    stats = jnp.concatenate(
        [v.reshape(-1) for v in (dsh1, dsc1, dgt1, dsh2, dsc2, dgt2, dsh3, dsc3, dgt3,
                                 dn1, dn2, dn3, dfinal, dbm, dconv)]
        + [jnp.broadcast_to(loss_local, (128,))]).reshape(1, -1)
    stats_all = all_gather_rows(stats, "gather_stats")
    n_mod = 9 * D_MODEL
    loss = jnp.sum(stats_all[:, 0, -1])
    dmod_all = stats_all[:, :, :n_mod]
    off = n_mod
    small_parts = {}
    for key in ("norm1_g", "norm2_g", "norm3_g", "final_g"):
        small_parts[key] = stats_all[:, :, off:off + D_MODEL]
        off += D_MODEL
    dbm_all = stats_all[:, 0, off:off + 2 * D_MODEL].reshape(N_DEV, 2, D_MODEL)
    off += 2 * D_MODEL
    dcw_all = stats_all[:, 0, off:off + 3 * CONV_W].reshape(N_DEV, 3, CONV_W)
    dbm_mine = lax.dynamic_slice(dbm_all, (0, 0, me * 128), (N_DEV, 2, 128))
    dcw_mine = lax.dynamic_slice(dcw_all, (0, 0, me * 64), (N_DEV, 3, 64))
    dmod_cols = lax.dynamic_slice(dmod_all[:, 0, :], (0, me * n_ada), (N_DEV, n_ada))
    g_ada = ada_backward(c_all, dmod_cols, "ada_backward")

    landed = scatter_grads([g_wgu1, g_wd1.reshape(N_DEV, D_FF // N_DEV, D_MODEL), g_wmix, g_wco, g_wao,
                            g_wout, g_wgu3, g_wd3.reshape(N_DEV, D_FF // N_DEV, D_MODEL)], "scatter_grads")
    landed = dict(zip(shard_names, landed))

    given = dict(w_ada=w_ada, b_ada=b_ada, norm1_g=norm1_g, ffn1_w_gu=ffn1_w_gu, ffn1_w_down=ffn1_w_down,
                 norm2_g=norm2_g, w_mix_in=w_mix_in, b_merge=b_merge, conv_w=conv_w, w_conv_out=w_conv_out,
                 w_attn_out=w_attn_out, w_out=w_out, norm3_g=norm3_g, ffn2_w_gu=ffn2_w_gu,
                 ffn2_w_down=ffn2_w_down, final_g=final_g)
    moments_m = dict(w_ada=m_w_ada, b_ada=m_b_ada, norm1_g=m_norm1_g, ffn1_w_gu=m_ffn1_w_gu,
                     ffn1_w_down=m_ffn1_w_down, norm2_g=m_norm2_g, w_mix_in=m_w_mix_in, b_merge=m_b_merge,
                     conv_w=m_conv_w, w_conv_out=m_w_conv_out, w_attn_out=m_w_attn_out, w_out=m_w_out,
                     norm3_g=m_norm3_g, ffn2_w_gu=m_ffn2_w_gu, ffn2_w_down=m_ffn2_w_down, final_g=m_final_g)
    moments_v = dict(w_ada=v_w_ada, b_ada=v_b_ada, norm1_g=v_norm1_g, ffn1_w_gu=v_ffn1_w_gu,
                     ffn1_w_down=v_ffn1_w_down, norm2_g=v_norm2_g, w_mix_in=v_w_mix_in, b_merge=v_b_merge,
                     conv_w=v_conv_w, w_conv_out=v_w_conv_out, w_attn_out=v_w_attn_out, w_out=v_w_out,
                     norm3_g=v_norm3_g, ffn2_w_gu=v_ffn2_w_gu, ffn2_w_down=v_ffn2_w_down, final_g=v_final_g)
    parts = dict(landed)
    parts["w_ada"] = g_ada[None]
    parts["b_ada"] = dmod_all
    parts["b_merge"] = dbm_mine
    parts["conv_w"] = dcw_mine
    parts.update(small_parts)

    order = ["w_ada", "b_ada", "norm1_g", "ffn1_w_gu", "ffn1_w_down", "norm2_g", "w_mix_in", "b_merge",
             "conv_w", "w_conv_out", "w_attn_out", "w_out", "norm3_g", "ffn2_w_gu", "ffn2_w_down", "final_g"]
    grads, deltas, new_m, new_v = [], [], [], []
    for key in order:
        shape = given[key].shape
        shape2 = (1, shape[0]) if len(shape) == 1 else shape[-2:]
        outs = adamw(given[key].reshape(shape2), moments_m[key].reshape(shape2),
                     moments_v[key].reshape(shape2), parts[key], f"adamw_{key}")
        for dst, val in zip((grads, deltas, new_m, new_v), outs):
            dst.append(val.reshape(shape))

    return (loss, grad_x[None], *grads, *deltas, *new_m, *new_v)
```

```python
import functools
from typing import Callable, NamedTuple

import jax
import jax.numpy as jnp
from jax import lax
from jax.experimental import pallas as pl
from jax.experimental.pallas import tpu as pltpu

F32 = jnp.float32
BF16 = jnp.bfloat16
MESH = pl.DeviceIdType.MESH
ANY = pl.BlockSpec(memory_space=pl.ANY)

N_DEV = 8
D_MODEL = 1024
D_FF = 2816
FF_BLK = D_FF // 4
N_HEADS = 8
HEAD_DIM = 64
CONV_W = 512
ATTN_W = 512
MIX_W = 3 * CONV_W + 3 * ATTN_W + 2 * D_MODEL
MIX_BLK = MIX_W // N_DEV
EPS = 1e-6
ATTN_SCALE = HEAD_DIM ** -0.5

ADAM_LR = 0.001
ADAM_B1 = 0.9
ADAM_B2 = 0.999
ADAM_EPS = 1e-08
ADAM_WD = 0.01
ADAM_STEP = 10
ADAM_BC1 = 1.0 - ADAM_B1 ** ADAM_STEP
ADAM_BC2 = 1.0 - ADAM_B2 ** ADAM_STEP

VMEM_LIMIT = 56 * 1024 * 1024
ROW_TILE = 512
ELT_TILE = 256
ATT_BLK = 256

NN = (((1,), (0,)), ((), ()))
NT = (((1,), (1,)), ((), ()))
TN = (((0,), (0,)), ((), ()))


def _dot(a, b, dims=NN):
    return lax.dot_general(a, b, dims, preferred_element_type=F32)


def _params(*sem):
    return pltpu.CompilerParams(dimension_semantics=sem, vmem_limit_bytes=VMEM_LIMIT)


def _sigmoid(x):
    return 1.0 / (1.0 + jnp.exp(-x))


def _me():
    x, y, c = lax.axis_index("x"), lax.axis_index("y"), lax.axis_index("c")
    return x, y, c, 4 * x + 2 * y + c


def _peer(k):
    x, y, c, _ = _me()
    px = 1 - x if (k >> 2) & 1 else x
    py = 1 - y if (k >> 1) & 1 else y
    pc = 1 - c if k & 1 else c
    return (px, py, pc), 4 * px + 2 * py + pc


class Exchange(NamedTuple):
    operands: tuple
    out_shapes: tuple
    aliases: dict
    n_remote: int
    n_local: int
    copies: Callable


CHIP_FLIPS = (2, 4, 6)
SIBLING = 1


def _remote(src, dst, send_sems, recv_sems, n, peer):
    return pltpu.make_async_remote_copy(src_ref=src, dst_ref=dst, send_sem=send_sems.at[n], recv_sem=recv_sems.at[n],
                                        device_id=peer, device_id_type=MESH)


def gather_stage1(shards):
    n = len(shards)
    rels = (SIBLING,) + CHIP_FLIPS

    def copies(ins, outs, send_sems, recv_sems, local_sems, rb, lb):
        _, _, _, me = _me()
        cps = []
        for w in range(n):
            cps.append(pltpu.make_async_copy(ins[w], outs[w].at[me], local_sems.at[lb + w]))
            for a, k in enumerate(rels):
                peer, _ = _peer(k)
                cps.append(_remote(ins[w], outs[w].at[me], send_sems, recv_sems, rb + len(rels) * w + a, peer))
        return cps

    shapes = tuple(jax.ShapeDtypeStruct((N_DEV,) + s.shape, s.dtype) for s in shards)
    return Exchange(tuple(shards), shapes, {}, len(rels) * n, n, copies)


def gather_stage2(fulls):
    n = len(fulls)

    def copies(ins, outs, send_sems, recv_sems, local_sems, rb, lb):
        sibling, _ = _peer(SIBLING)
        cps = []
        for w in range(n):
            for a, k in enumerate(CHIP_FLIPS):
                _, blk = _peer(k)
                cps.append(_remote(outs[w].at[blk], outs[w].at[blk], send_sems, recv_sems, rb + 3 * w + a, sibling))
        return cps

    shapes = tuple(jax.ShapeDtypeStruct(f.shape, f.dtype) for f in fulls)
    return Exchange(tuple(fulls), shapes, {w: w for w in range(n)}, 3 * n, 0, copies)


def scatter_stage1(fulls):
    n = len(fulls)

    def copies(ins, outs, send_sems, recv_sems, local_sems, rb, lb):
        _, _, c, _ = _me()
        sibling, _ = _peer(SIBLING)
        cps = []
        for w in range(n):
            for q in range(4):
                cps.append(_remote(ins[w].at[2 * q + (1 - c)], outs[w].at[q], send_sems, recv_sems, rb + 4 * w + q, sibling))
        return cps

    shapes = tuple(jax.ShapeDtypeStruct((4,) + f.shape[1:], f.dtype) for f in fulls)
    return Exchange(tuple(fulls), shapes, {}, 4 * n, 0, copies)


def scatter_stage2(sums):
    n = len(sums)

    def copies(ins, outs, send_sems, recv_sems, local_sems, rb, lb):
        x, y, _, _ = _me()
        mine = 2 * x + y
        cps = []
        for w in range(n):
            cps.append(pltpu.make_async_copy(ins[w].at[mine], outs[w].at[mine], local_sems.at[lb + w]))
            for a, k in enumerate(CHIP_FLIPS):
                peer, _ = _peer(k)
                cps.append(_remote(ins[w].at[2 * peer[0] + peer[1]], outs[w].at[mine], send_sems, recv_sems,
                                   rb + 3 * w + a, peer))
        return cps

    shapes = tuple(jax.ShapeDtypeStruct(s.shape, s.dtype) for s in sums)
    return Exchange(tuple(sums), shapes, {}, 3 * n, n, copies)


def merge_exchanges(a, b):
    na_in, na_out = len(a.operands), len(a.out_shapes)

    def copies(ins, outs, send_sems, recv_sems, local_sems, rb, lb):
        return (a.copies(ins[:na_in], outs[:na_out], send_sems, recv_sems, local_sems, rb, lb)
                + b.copies(ins[na_in:], outs[na_out:], send_sems, recv_sems, local_sems, rb + a.n_remote, lb + a.n_local))

    aliases = dict(a.aliases)
    aliases.update({na_in + i: na_out + o for i, o in b.aliases.items()})
    return Exchange(a.operands + b.operands, a.out_shapes + b.out_shapes, aliases,
                    a.n_remote + b.n_remote, a.n_local + b.n_local, copies)


def _exchange_scratch(ex):
    return [pltpu.SemaphoreType.DMA((ex.n_remote,)), pltpu.SemaphoreType.DMA((ex.n_remote,)),
            pltpu.SemaphoreType.DMA((max(ex.n_local, 1),))]


def run_exchange(ex, name):
    n_in, n_out = len(ex.operands), len(ex.out_shapes)

    def body(*refs):
        cps = ex.copies(refs[:n_in], refs[n_in:n_in + n_out], *refs[n_in + n_out:], 0, 0)
        for cp in cps:
            cp.start()
        for cp in cps:
            cp.wait()

    return pl.pallas_call(
        body, name=name, out_shape=list(ex.out_shapes), in_specs=[ANY] * n_in, out_specs=[ANY] * n_out,
        scratch_shapes=_exchange_scratch(ex), input_output_aliases=dict(ex.aliases),
    )(*ex.operands)


def _call(body, *, name, grid, in_specs, out_specs, out_shape, operands, scratch_shapes=(), semantics=(),
          exchange=None):
    if exchange is None:
        return pl.pallas_call(
            body, name=name, grid=grid, in_specs=in_specs, out_specs=out_specs, out_shape=out_shape,
            scratch_shapes=list(scratch_shapes), compiler_params=_params(*semantics))(*operands)
    single = not isinstance(out_shape, (list, tuple))
    out_shapes = [out_shape] if single else list(out_shape)
    out_specs_l = [out_specs] if single else list(out_specs)
    n_in, n_out, n_scr = len(operands), len(out_shapes), len(scratch_shapes)
    x_in, x_out = len(exchange.operands), len(exchange.out_shapes)

    def hosted(*refs):
        ins, refs = refs[:n_in], refs[n_in:]
        xin, refs = refs[:x_in], refs[x_in:]
        outs, refs = refs[:n_out], refs[n_out:]
        xout, refs = refs[:x_out], refs[x_out:]
        scr, sems = refs[:n_scr], refs[n_scr:]
        first = functools.reduce(jnp.logical_and, [pl.program_id(a) == 0 for a in range(len(grid))])
        last = functools.reduce(jnp.logical_and, [pl.program_id(a) == g - 1 for a, g in enumerate(grid)])

        @pl.when(first)
        def _():
            for cp in exchange.copies(xin, xout, *sems, 0, 0):
                cp.start()

        body(*ins, *outs, *scr)

        @pl.when(last)
        def _():
            for cp in exchange.copies(xin, xout, *sems, 0, 0):
                cp.wait()

    res = pl.pallas_call(
        hosted, name=name, grid=grid,
        in_specs=list(in_specs) + [ANY] * x_in, out_specs=out_specs_l + [ANY] * x_out,
        out_shape=out_shapes + list(exchange.out_shapes),
        scratch_shapes=list(scratch_shapes) + _exchange_scratch(exchange),
        input_output_aliases={n_in + i: n_out + o for i, o in exchange.aliases.items()},
        compiler_params=_params(*(["arbitrary"] * len(grid))),
    )(*operands, *exchange.operands)
    outs, xouts = res[:n_out], res[n_out:]
    return (outs[0] if single else outs), xouts


def all_gather_rows(v, name):
    r, n = v.shape

    def body(v_ref, out_ref, send_sems, recv_sems):
        _, _, _, me = _me()
        out_ref[me] = v_ref[...]
        copies = []
        for k in range(1, N_DEV):
            peer, _ = _peer(k)
            copies.append(_remote(v_ref, out_ref.at[me], send_sems, recv_sems, k - 1, peer))
        for cp in copies:
            cp.start()
        for cp in copies:
            cp.wait()

    return pl.pallas_call(
        body, name=name,
        out_shape=jax.ShapeDtypeStruct((N_DEV, r, n), v.dtype),
        in_specs=[pl.BlockSpec(memory_space=pltpu.VMEM)],
        out_specs=pl.BlockSpec(memory_space=pltpu.VMEM),
        scratch_shapes=[pltpu.SemaphoreType.DMA((N_DEV - 1,)), pltpu.SemaphoreType.DMA((N_DEV - 1,))],
    )(v)


def pair_sum(full, pair, name):
    _, r, c = full.shape
    t = 256 if r % 256 == 0 and r > 256 else r
    core = lax.axis_index("c").astype(jnp.int32).reshape(1)

    def body(core_ref, f_ref, p_ref, o_ref):
        o_ref[...] = (f_ref[...].astype(F32) + p_ref[...].astype(F32)).astype(BF16)

    return pl.pallas_call(
        body, name=name,
        grid_spec=pltpu.PrefetchScalarGridSpec(
            num_scalar_prefetch=1, grid=(4, r // t),
            in_specs=[pl.BlockSpec((None, None, t, c), lambda q, i, core_ref: (q, core_ref[0], i, 0)),
                      pl.BlockSpec((None, t, c), lambda q, i, core_ref: (q, i, 0))],
            out_specs=pl.BlockSpec((None, t, c), lambda q, i, core_ref: (q, i, 0))),
        out_shape=jax.ShapeDtypeStruct((4, r, c), BF16),
        compiler_params=_params("parallel", "parallel"),
    )(core, full.reshape(4, 2, r, c), pair)


def ada_forward(c_all, w_ada, b_cols, name):
    n = w_ada.shape[1]

    def body(c_ref, w_ref, b_ref, o_ref):
        c = c_ref[...]
        act = c * _sigmoid(c)
        o_ref[...] = jnp.dot(act, w_ref[...], precision=lax.Precision.HIGHEST,
                             preferred_element_type=F32) + b_ref[...]

    return pl.pallas_call(
        body, name=name, out_shape=jax.ShapeDtypeStruct((N_DEV, n), F32),
        compiler_params=pltpu.CompilerParams(vmem_limit_bytes=VMEM_LIMIT),
    )(c_all, w_ada, b_cols)


def ada_backward(c_all, dmod_cols, name):
    n = dmod_cols.shape[1]

    def body(c_ref, d_ref, o_ref):
        c = c_ref[...]
        act = c * _sigmoid(c)
        o_ref[...] = lax.dot_general(act, d_ref[...], TN, precision=lax.Precision.HIGHEST,
                                     preferred_element_type=F32)

    return pl.pallas_call(
        body, name=name, out_shape=jax.ShapeDtypeStruct((D_MODEL, n), F32),
        compiler_params=pltpu.CompilerParams(vmem_limit_bytes=VMEM_LIMIT),
    )(c_all, dmod_cols)


def _row_spec(t, width=D_MODEL):
    return pl.BlockSpec((t, width), lambda i: (i, 0))


def _vec_spec(rows=1, width=D_MODEL):
    return pl.BlockSpec((rows, width), lambda i: (0, 0))


def norm_modulate(x, g, shift, scale, name):
    s = x.shape[0]
    t = min(ELT_TILE, s)

    def body(x_ref, g_ref, sh_ref, sc_ref, u_ref):
        xv = x_ref[...]
        r = lax.rsqrt(jnp.mean(xv * xv, axis=-1, keepdims=True) + EPS)
        a = (xv * r) * g_ref[...]
        u_ref[...] = (a * (1.0 + sc_ref[...]) + sh_ref[...]).astype(BF16)

    return pl.pallas_call(
        body, name=name, grid=(s // t,),
        in_specs=[_row_spec(t), _vec_spec(), _vec_spec(), _vec_spec()],
        out_specs=_row_spec(t),
        out_shape=jax.ShapeDtypeStruct((s, D_MODEL), BF16),
        compiler_params=_params("parallel"),
    )(x, g, shift, scale)


def loss_head(x, target, final_g, gate, y_prev, coef, name):
    s = x.shape[0]
    t = min(ELT_TILE, s)

    def body(x_ref, t_ref, fg_ref, gt_ref, y_ref, dx_ref, dy_ref, dgt_ref, dfg_ref, sq_ref):
        @pl.when(pl.program_id(0) == 0)
        def _():
            dgt_ref[...] = jnp.zeros_like(dgt_ref)
            dfg_ref[...] = jnp.zeros_like(dfg_ref)
            sq_ref[...] = jnp.zeros_like(sq_ref)

        xv = x_ref[...]
        fg = fg_ref[...]
        r = lax.rsqrt(jnp.mean(xv * xv, axis=-1, keepdims=True) + EPS)
        nrm = xv * r
        err = nrm * fg - t_ref[...]
        sq_ref[...] += jnp.sum(err * err, axis=0, keepdims=True)
        dout = err * (1.0 / D_MODEL)
        dfg_ref[...] += jnp.sum(dout * nrm, axis=0, keepdims=True)
        dn = dout * fg
        dx = r * (dn - nrm * jnp.mean(dn * nrm, axis=-1, keepdims=True))
        dx_ref[...] = dx
        dy_ref[...] = (coef * gt_ref[...] * dx).astype(BF16)
        dgt_ref[...] += coef * jnp.sum(dx * y_ref[...].astype(F32), axis=0, keepdims=True)

    vec = jax.ShapeDtypeStruct((1, D_MODEL), F32)
    return pl.pallas_call(
        body, name=name, grid=(s // t,),
        in_specs=[_row_spec(t), _row_spec(t), _vec_spec(), _vec_spec(), _row_spec(t)],
        out_specs=[_row_spec(t), _row_spec(t), _vec_spec(), _vec_spec(), _vec_spec()],
        out_shape=[jax.ShapeDtypeStruct((s, D_MODEL), F32), jax.ShapeDtypeStruct((s, D_MODEL), BF16),
                   vec, vec, vec],
        compiler_params=_params("arbitrary"),
    )(x, target, final_g, gate, y_prev)


def norm_modulate_bwd(du, x, dx_out, g, scale, name, prev=None, exchange=None):
    s = x.shape[0]
    t = min(ELT_TILE, s)
    has_prev = prev is not None

    def body(*refs):
        du_ref, x_ref, dxo_ref, g_ref, sc_ref = refs[:5]
        refs = refs[5:]
        if has_prev:
            gt_ref, y_ref = refs[:2]
            refs = refs[2:]
        dx_ref, dsh_ref, dsc_ref, dg_ref = refs[:4]

        @pl.when(pl.program_id(0) == 0)
        def _():
            dsh_ref[...] = jnp.zeros_like(dsh_ref)
            dsc_ref[...] = jnp.zeros_like(dsc_ref)
            dg_ref[...] = jnp.zeros_like(dg_ref)
            if has_prev:
                refs[5][...] = jnp.zeros_like(refs[5])

        xv = x_ref[...]
        duv = du_ref[...]
        gv = g_ref[...]
        r = lax.rsqrt(jnp.mean(xv * xv, axis=-1, keepdims=True) + EPS)
        nrm = xv * r
        a = nrm * gv
        dsh_ref[...] += jnp.sum(duv, axis=0, keepdims=True)
        dsc_ref[...] += jnp.sum(duv * a, axis=0, keepdims=True)
        da = duv * (1.0 + sc_ref[...])
        dg_ref[...] += jnp.sum(da * nrm, axis=0, keepdims=True)
        dn = da * gv
        dx = dxo_ref[...] + r * (dn - nrm * jnp.mean(dn * nrm, axis=-1, keepdims=True))
        dx_ref[...] = dx
        if has_prev:
            coef = prev[2]
            refs[4][...] = (coef * gt_ref[...] * dx).astype(BF16)
            refs[5][...] += coef * jnp.sum(dx * y_ref[...].astype(F32), axis=0, keepdims=True)

    vec = jax.ShapeDtypeStruct((1, D_MODEL), F32)
    operands = [du, x, dx_out, g, scale]
    in_specs = [_row_spec(t), _row_spec(t), _row_spec(t), _vec_spec(), _vec_spec()]
    out_specs = [_row_spec(t), _vec_spec(), _vec_spec(), _vec_spec()]
    out_shape = [jax.ShapeDtypeStruct((s, D_MODEL), F32), vec, vec, vec]
    if has_prev:
        operands += [prev[0], prev[1]]
        in_specs += [_vec_spec(), _row_spec(t)]
        out_specs += [_row_spec(t), _vec_spec()]
        out_shape += [jax.ShapeDtypeStruct((s, D_MODEL), BF16), vec]
    return _call(body, name=name, grid=(s // t,), in_specs=in_specs, out_specs=out_specs, out_shape=out_shape,
                 operands=operands, semantics=("arbitrary",), exchange=exchange)


def ffn_up(u, w_gu, name, exchange=None):
    s = u.shape[0]
    t = min(ROW_TILE, s)

    def body(u_ref, wg_ref, wu_ref, gu_ref, act_ref):
        uv = u_ref[...]
        g = _dot(uv, wg_ref[...])
        up = _dot(uv, wu_ref[...])
        gu_ref[0] = g.astype(BF16)
        gu_ref[1] = up.astype(BF16)
        act_ref[...] = (g * _sigmoid(g) * up).astype(BF16)

    return _call(
        body, name=name, grid=(4, s // t),
        in_specs=[pl.BlockSpec((t, D_MODEL), lambda j, i: (i, 0)),
                  pl.BlockSpec((None, D_MODEL, FF_BLK), lambda j, i: (j, 0, 0)),
                  pl.BlockSpec((None, D_MODEL, FF_BLK), lambda j, i: (j + 4, 0, 0))],
        out_specs=[pl.BlockSpec((2, None, t, FF_BLK), lambda j, i: (0, j, i, 0)),
                   pl.BlockSpec((None, t, FF_BLK), lambda j, i: (j, i, 0))],
        out_shape=[jax.ShapeDtypeStruct((2, 4, s, FF_BLK), BF16),
                   jax.ShapeDtypeStruct((4, s, FF_BLK), BF16)],
        operands=(u, w_gu, w_gu), semantics=("parallel", "parallel"), exchange=exchange)


def residual_matmul(a, b, x, gate, coef, name, exchange=None):
    nk, s, kb = a.shape
    t = min(ROW_TILE, s)

    def body(a_ref, b_ref, x_ref, gt_ref, xo_ref, y_ref, acc_ref):
        k = pl.program_id(1)

        @pl.when(k == 0)
        def _():
            acc_ref[...] = jnp.zeros_like(acc_ref)

        acc_ref[...] += _dot(a_ref[...], b_ref[...])

        @pl.when(k == nk - 1)
        def _():
            y = acc_ref[...]
            y_ref[...] = y.astype(BF16)
            xo_ref[...] = x_ref[...] + coef * gt_ref[...] * y

    return _call(
        body, name=name, grid=(s // t, nk),
        in_specs=[pl.BlockSpec((None, t, kb), lambda i, k: (k, i, 0)),
                  pl.BlockSpec((None, kb, D_MODEL), lambda i, k: (k, 0, 0)),
                  pl.BlockSpec((t, D_MODEL), lambda i, k: (i, 0)),
                  pl.BlockSpec((1, D_MODEL), lambda i, k: (0, 0))],
        out_specs=[pl.BlockSpec((t, D_MODEL), lambda i, k: (i, 0)),
                   pl.BlockSpec((t, D_MODEL), lambda i, k: (i, 0))],
        out_shape=[jax.ShapeDtypeStruct((s, D_MODEL), F32), jax.ShapeDtypeStruct((s, D_MODEL), BF16)],
        operands=(a, b, x, gate), scratch_shapes=[pltpu.VMEM((t, D_MODEL), F32)],
        semantics=("parallel", "arbitrary"), exchange=exchange)


def ffn_dact(dy, w_down, gu, name, exchange=None):
    s = dy.shape[0]
    t = min(ROW_TILE, s)

    def body(dy_ref, w_ref, gu_ref, dgu_ref):
        dact = _dot(dy_ref[...], w_ref[...], NT)
        g = gu_ref[0].astype(F32)
        up = gu_ref[1].astype(F32)
        sg = _sigmoid(g)
        dgu_ref[0] = (dact * up * sg * (1.0 + g * (1.0 - sg))).astype(BF16)
        dgu_ref[1] = (dact * g * sg).astype(BF16)

    return _call(
        body, name=name, grid=(4, s // t),
        in_specs=[pl.BlockSpec((t, D_MODEL), lambda j, i: (i, 0)),
                  pl.BlockSpec((None, FF_BLK, D_MODEL), lambda j, i: (j, 0, 0)),
                  pl.BlockSpec((2, None, t, FF_BLK), lambda j, i: (0, j, i, 0))],
        out_specs=pl.BlockSpec((2, None, t, FF_BLK), lambda j, i: (0, j, i, 0)),
        out_shape=jax.ShapeDtypeStruct((2, 4, s, FF_BLK), BF16),
        operands=(dy, w_down, gu), semantics=("parallel", "parallel"), exchange=exchange)


def matmul_nt_acc(a, b, name, exchange=None):
    nk, s, n = a.shape
    d = b.shape[1]
    t = min(ROW_TILE, s)

    def body(a_ref, b_ref, o_ref, acc_ref):
        k = pl.program_id(1)

        @pl.when(k == 0)
        def _():
            acc_ref[...] = jnp.zeros_like(acc_ref)

        acc_ref[...] += _dot(a_ref[...], b_ref[...], NT)

        @pl.when(k == nk - 1)
        def _():
            o_ref[...] = acc_ref[...]

    return _call(
        body, name=name, grid=(s // t, nk),
        in_specs=[pl.BlockSpec((None, t, n), lambda i, k: (k, i, 0)),
                  pl.BlockSpec((None, d, n), lambda i, k: (k, 0, 0))],
        out_specs=pl.BlockSpec((t, d), lambda i, k: (i, 0)),
        out_shape=jax.ShapeDtypeStruct((s, d), F32),
        operands=(a, b), scratch_shapes=[pltpu.VMEM((t, d), F32)],
        semantics=("parallel", "arbitrary"), exchange=exchange)


def matmul_tn(a, b, name, exchange=None):
    ja, s, m = a.shape
    jb, _, n = b.shape
    t = min(ROW_TILE, s)
    nk = s // t

    def body(a_ref, b_ref, o_ref, acc_ref):
        k = pl.program_id(2)

        @pl.when(k == 0)
        def _():
            acc_ref[...] = jnp.zeros_like(acc_ref)

        acc_ref[...] += _dot(a_ref[...], b_ref[...], TN)

        @pl.when(k == nk - 1)
        def _():
            o_ref[...] = acc_ref[...].astype(BF16)

    return _call(
        body, name=name, grid=(ja, jb, nk),
        in_specs=[pl.BlockSpec((None, t, m), lambda p, q, k: (p, k, 0)),
                  pl.BlockSpec((None, t, n), lambda p, q, k: (q, k, 0))],
        out_specs=pl.BlockSpec((None, None, m, n), lambda p, q, k: (p, q, 0, 0)),
        out_shape=jax.ShapeDtypeStruct((ja, jb, m, n), BF16),
        operands=(a, b), scratch_shapes=[pltpu.VMEM((m, n), F32)],
        semantics=("parallel", "parallel", "arbitrary"), exchange=exchange)


def mix_in_proj(u, w_mix, name):
    s = u.shape[0]
    t = min(ROW_TILE, s)

    def body(u_ref, w_ref, o_ref):
        o_ref[...] = _dot(u_ref[...], w_ref[...]).astype(BF16)

    return pl.pallas_call(
        body, name=name, grid=(N_DEV, s // t),
        in_specs=[pl.BlockSpec((t, D_MODEL), lambda j, i: (i, 0)),
                  pl.BlockSpec((None, D_MODEL, MIX_BLK), lambda j, i: (j, 0, 0))],
        out_specs=pl.BlockSpec((t, MIX_BLK), lambda j, i: (i, j)),
        out_shape=jax.ShapeDtypeStruct((s, MIX_W), BF16),
        compiler_params=_params("parallel", "parallel"),
    )(u, w_mix)


def _conv_taps(cc_ref, cx_ref, s):
    v = cc_ref[...].astype(F32) * cx_ref[...].astype(F32)
    tok = lax.broadcasted_iota(jnp.int32, v.shape, 0)
    v1 = jnp.where(tok >= 1, pltpu.roll(v, 1, 0), 0.0)
    v2 = jnp.where(tok >= 2, pltpu.roll(v, 2, 0), 0.0)
    return v, v1, v2, tok


def _proj_cols(s, first):
    return pl.BlockSpec((s, 128), lambda j: (0, first + j))


def short_conv(proj, conv_w, name):
    s = proj.shape[0]

    def body(cb_ref, cc_ref, cx_ref, w_ref, o_ref):
        v, v1, v2, _ = _conv_taps(cc_ref, cx_ref, s)
        y = w_ref[0:1, :] * v2 + w_ref[1:2, :] * v1 + w_ref[2:3, :] * v
        o_ref[...] = (cb_ref[...].astype(F32) * y).astype(BF16)

    return pl.pallas_call(
        body, name=name, grid=(CONV_W // 128,),
        in_specs=[_proj_cols(s, 0), _proj_cols(s, 4), _proj_cols(s, 8),
                  pl.BlockSpec((3, 128), lambda j: (0, j))],
        out_specs=pl.BlockSpec((s, 128), lambda j: (0, j)),
        out_shape=jax.ShapeDtypeStruct((s, CONV_W), BF16),
        compiler_params=_params("parallel"),
    )(proj, proj, proj, conv_w)


def short_conv_bwd(dsa, proj, conv_w, name):
    s = proj.shape[0]

    def body(dsa_ref, cb_ref, cc_ref, cx_ref, w_ref, dcb_ref, dcc_ref, dcx_ref, dw_ref):
        v, v1, v2, tok = _conv_taps(cc_ref, cx_ref, s)
        w0, w1, w2 = w_ref[0:1, :], w_ref[1:2, :], w_ref[2:3, :]
        y = w0 * v2 + w1 * v1 + w2 * v
        dsa_v = dsa_ref[...].astype(F32)
        dcb_ref[...] = (dsa_v * y).astype(BF16)
        dy = dsa_v * cb_ref[...].astype(F32)
        dw_ref[0:1, :] = jnp.sum(dy * v2, axis=0, keepdims=True)
        dw_ref[1:2, :] = jnp.sum(dy * v1, axis=0, keepdims=True)
        dw_ref[2:3, :] = jnp.sum(dy * v, axis=0, keepdims=True)
        dy1 = jnp.where(tok < s - 1, pltpu.roll(dy, s - 1, 0), 0.0)
        dy2 = jnp.where(tok < s - 2, pltpu.roll(dy, s - 2, 0), 0.0)
        dv = w2 * dy + w1 * dy1 + w0 * dy2
        dcc_ref[...] = (dv * cx_ref[...].astype(F32)).astype(BF16)
        dcx_ref[...] = (dv * cc_ref[...].astype(F32)).astype(BF16)

    col = pl.BlockSpec((s, 128), lambda j: (0, j))
    act = jax.ShapeDtypeStruct((s, CONV_W), BF16)
    return pl.pallas_call(
        body, name=name, grid=(CONV_W // 128,),
        in_specs=[col, _proj_cols(s, 0), _proj_cols(s, 4), _proj_cols(s, 8),
                  pl.BlockSpec((3, 128), lambda j: (0, j))],
        out_specs=[col, col, col, pl.BlockSpec((3, 128), lambda j: (0, j))],
        out_shape=[act, act, act, jax.ShapeDtypeStruct((3, CONV_W), F32)],
        compiler_params=_params("parallel"),
    )(dsa, proj, proj, proj, conv_w)


def _gate_specs(t):
    return [pl.BlockSpec((t, D_MODEL), lambda i: (i, 3)), pl.BlockSpec((t, D_MODEL), lambda i: (i, 4))]


def merge_forward(sa, o, proj, w_co, w_ao, b_merge, name, exchange=None):
    s = sa.shape[0]
    t = min(ROW_TILE, s)

    def body(sa_ref, o_ref, ga_ref, gb_ref, wco_ref, wao_ref, bm_ref, mg_ref, ya_ref, yb_ref):
        ya = _dot(sa_ref[...], wco_ref[...])
        yb = _dot(o_ref[...], wao_ref[...])
        sga = _sigmoid(ga_ref[...].astype(F32) + bm_ref[0:1, :])
        sgb = _sigmoid(gb_ref[...].astype(F32) + bm_ref[1:2, :])
        mg_ref[...] = (sga * ya + sgb * yb).astype(BF16)
        ya_ref[...] = ya.astype(BF16)
        yb_ref[...] = yb.astype(BF16)

    act = jax.ShapeDtypeStruct((s, D_MODEL), BF16)
    return _call(
        body, name=name, grid=(s // t,),
        in_specs=[_row_spec(t, CONV_W), _row_spec(t, ATTN_W)] + _gate_specs(t)
        + [_vec_spec(CONV_W), _vec_spec(ATTN_W), _vec_spec(2)],
        out_specs=[_row_spec(t)] * 3, out_shape=[act, act, act],
        operands=(sa, o, proj, proj, w_co, w_ao, b_merge), semantics=("parallel",), exchange=exchange)


def merge_backward(dy, w_out, proj, ya, yb, b_merge, name, exchange=None):
    s = dy.shape[0]
    t = min(ROW_TILE, s)

    def body(dy_ref, w_ref, ga_ref, gb_ref, ya_ref, yb_ref, bm_ref,
             dya_ref, dyb_ref, dga_ref, dgb_ref, dbm_ref):
        @pl.when(pl.program_id(0) == 0)
        def _():
            dbm_ref[...] = jnp.zeros_like(dbm_ref)

        dmg = _dot(dy_ref[...], w_ref[...], NT)
        sga = _sigmoid(ga_ref[...].astype(F32) + bm_ref[0:1, :])
        sgb = _sigmoid(gb_ref[...].astype(F32) + bm_ref[1:2, :])
        dya_ref[...] = (dmg * sga).astype(BF16)
        dyb_ref[...] = (dmg * sgb).astype(BF16)
        dga = dmg * ya_ref[...].astype(F32) * sga * (1.0 - sga)
        dgb = dmg * yb_ref[...].astype(F32) * sgb * (1.0 - sgb)
        dga_ref[...] = dga.astype(BF16)
        dgb_ref[...] = dgb.astype(BF16)
        dbm_ref[0:1, :] += jnp.sum(dga, axis=0, keepdims=True)
        dbm_ref[1:2, :] += jnp.sum(dgb, axis=0, keepdims=True)

    act = jax.ShapeDtypeStruct((s, D_MODEL), BF16)
    return _call(
        body, name=name, grid=(s // t,),
        in_specs=[_row_spec(t), _vec_spec(D_MODEL)] + _gate_specs(t)
        + [_row_spec(t), _row_spec(t), _vec_spec(2)],
        out_specs=[_row_spec(t)] * 4 + [_vec_spec(2)],
        out_shape=[act] * 4 + [jax.ShapeDtypeStruct((2, D_MODEL), F32)],
        operands=(dy, w_out, proj, proj, ya, yb, b_merge), semantics=("arbitrary",), exchange=exchange)


def out_proj_bwd(dya, dyb, w_co, w_ao, name):
    s = dya.shape[0]
    t = min(ROW_TILE, s)

    def body(dya_ref, dyb_ref, wco_ref, wao_ref, dsa_ref, do_ref):
        dsa_ref[...] = _dot(dya_ref[...], wco_ref[...], NT).astype(BF16)
        do_ref[...] = _dot(dyb_ref[...], wao_ref[...], NT).astype(BF16)

    return pl.pallas_call(
        body, name=name, grid=(s // t,),
        in_specs=[_row_spec(t), _row_spec(t), _vec_spec(CONV_W), _vec_spec(ATTN_W)],
        out_specs=[_row_spec(t, CONV_W), _row_spec(t, ATTN_W)],
        out_shape=[jax.ShapeDtypeStruct((s, CONV_W), BF16), jax.ShapeDtypeStruct((s, ATTN_W), BF16)],
        compiler_params=_params("parallel"),
    )(dya, dyb, w_co, w_ao)


def _softplus(z):
    return jnp.maximum(z, 0.0) + jnp.log(1.0 + jnp.exp(-jnp.abs(z)))


def _tri_sum(x, tri):
    hi = x.astype(BF16)
    lo = (x - hi.astype(F32)).astype(BF16)
    return _dot(hi, tri) + _dot(lo, tri)


def _head_masks():
    lane = lax.broadcasted_iota(jnp.int32, (ATT_BLK, 2 * HEAD_DIM), 1)
    return lane < HEAD_DIM, lane >= HEAD_DIM


def stick_breaking_fwd(proj, name, exchange=None):
    s = proj.shape[0]
    blk = ATT_BLK
    nq = s // blk

    def body(q_ref, k_ref, v_ref, o_ref, tot_ref):
        i = pl.program_id(1)
        row = lax.broadcasted_iota(jnp.int32, (blk, blk), 0)
        col = lax.broadcasted_iota(jnp.int32, (blk, blk), 1)
        tri = (row >= col).astype(BF16)
        causal = col < row
        q_all = q_ref[...] * ATTN_SCALE
        accs, tots = [], []
        for mask in _head_masks():
            q = jnp.where(mask, q_all, 0.0).astype(BF16)

            def step(j, carry, diagonal, q=q):
                later, acc = carry
                rows = pl.ds(pl.multiple_of(j * blk, blk), blk)
                z = _dot(q, k_ref[rows, :], NT)
                sp = _softplus(z)
                if diagonal:
                    sp = jnp.where(causal, sp, 0.0)
                a = jnp.exp(z - (_tri_sum(sp, tri) + later))
                if diagonal:
                    a = jnp.where(causal, a, 0.0)
                acc = acc + _dot(a.astype(BF16), v_ref[rows, :])
                return later + jnp.sum(sp, axis=1, keepdims=True), acc

            carry = (jnp.zeros((blk, 1), F32), jnp.zeros((blk, 2 * HEAD_DIM), F32))
            carry = step(i, carry, True)
            carry = lax.fori_loop(0, i, lambda n, c: step(i - 1 - n, c, False), carry)
            tots.append(carry[0])
            accs.append(carry[1])
        first, _ = _head_masks()
        o_ref[...] = jnp.where(first, accs[0], accs[1]).astype(BF16)
        tot_ref[...] = jnp.where(first, tots[0], tots[1])

    pair = 2 * HEAD_DIM
    return _call(
        body, name=name, grid=(N_HEADS // 2, nq),
        in_specs=[pl.BlockSpec((blk, pair), lambda h, i: (i, 12 + h)),
                  pl.BlockSpec((s, pair), lambda h, i: (0, 16 + h)),
                  pl.BlockSpec((s, pair), lambda h, i: (0, 20 + h))],
        out_specs=[pl.BlockSpec((blk, pair), lambda h, i: (i, h)),
                   pl.BlockSpec((blk, pair), lambda h, i: (i, h))],
        out_shape=[jax.ShapeDtypeStruct((s, ATTN_W), BF16), jax.ShapeDtypeStruct((s, ATTN_W), F32)],
        operands=(proj, proj, proj), semantics=("parallel", "arbitrary"), exchange=exchange)


def stick_breaking_bwd(proj, do, tot, name, exchange=None):
    s = proj.shape[0]
    blk = ATT_BLK
    nq = s // blk

    def body(q_ref, k_ref, v_ref, do_ref, tot_ref, dq_ref, dk_ref, dv_ref):
        i = pl.program_id(1)

        @pl.when(i == 0)
        def _():
            dk_ref[...] = jnp.zeros_like(dk_ref)
            dv_ref[...] = jnp.zeros_like(dv_ref)

        row = lax.broadcasted_iota(jnp.int32, (blk, blk), 0)
        col = lax.broadcasted_iota(jnp.int32, (blk, blk), 1)
        before = (row < col).astype(BF16)
        upto = (row <= col).astype(BF16)
        causal = col < row
        q_all = q_ref[...] * ATTN_SCALE
        do_all = do_ref[...]
        tot_all = tot_ref[...]
        dqs = []
        for mask in _head_masks():
            q = jnp.where(mask, q_all, 0.0).astype(BF16)
            dov = jnp.where(mask, do_all, 0.0).astype(BF16)
            total = jnp.max(jnp.where(mask, tot_all, 0.0), axis=1, keepdims=True)

            def step(j, carry, diagonal, q=q, dov=dov, total=total):
                earlier, g_sum, dq = carry
                rows = pl.ds(pl.multiple_of(j * blk, blk), blk)
                kb = k_ref[rows, :]
                vb = v_ref[rows, :]
                z = _dot(q, kb, NT)
                sp = _softplus(z)
                if diagonal:
                    sp = jnp.where(causal, sp, 0.0)
                c = (total - earlier) - _tri_sum(sp, before)
                a = jnp.exp(z - c)
                if diagonal:
                    a = jnp.where(causal, a, 0.0)
                g = a * _dot(dov, vb, NT)
                f = g_sum + _tri_sum(g, upto)
                dz = g - jnp.exp(z - sp) * f
                if diagonal:
                    dz = jnp.where(causal, dz, 0.0)
                dzb = dz.astype(BF16)
                dq = dq + _dot(dzb, kb)
                dk_ref[rows, :] += _dot(dzb, q, TN)
                dv_ref[rows, :] += _dot(a.astype(BF16), dov, TN)
                return (earlier + jnp.sum(sp, axis=1, keepdims=True),
                        g_sum + jnp.sum(g, axis=1, keepdims=True), dq)

            carry = (jnp.zeros((blk, 1), F32), jnp.zeros((blk, 1), F32),
                     jnp.zeros((blk, 2 * HEAD_DIM), F32))
            carry = lax.fori_loop(0, i, lambda j, c: step(j, c, False), carry)
            carry = step(i, carry, True)
            dqs.append(carry[2])
        first, _ = _head_masks()
        dq_ref[...] = (jnp.where(first, dqs[0], dqs[1]) * ATTN_SCALE).astype(BF16)

    pair = 2 * HEAD_DIM
    blk_spec = pl.BlockSpec((blk, pair), lambda h, i: (i, h))
    full_spec = pl.BlockSpec((s, pair), lambda h, i: (0, h))
    return _call(
        body, name=name, grid=(N_HEADS // 2, nq),
        in_specs=[pl.BlockSpec((blk, pair), lambda h, i: (i, 12 + h)),
                  pl.BlockSpec((s, pair), lambda h, i: (0, 16 + h)),
                  pl.BlockSpec((s, pair), lambda h, i: (0, 20 + h)),
                  blk_spec, blk_spec],
        out_specs=[blk_spec, full_spec, full_spec],
        out_shape=[jax.ShapeDtypeStruct((s, ATTN_W), BF16), jax.ShapeDtypeStruct((s, ATTN_W), F32),
                   jax.ShapeDtypeStruct((s, ATTN_W), F32)],
        operands=(proj, proj, proj, do, tot), semantics=("parallel", "arbitrary"), exchange=exchange)


def adamw(w, m, v, parts, name):
    r, c = w.shape
    p = parts.shape[0]
    t = r
    for cand in (256, 176):
        if r % cand == 0 and r > cand:
            t = cand
            break

    def body(w_ref, m_ref, v_ref, p_ref, g_ref, d_ref, mo_ref, vo_ref):
        g = p_ref[0].astype(F32)
        for n in range(1, p):
            g = g + p_ref[n].astype(F32)
        m_new = ADAM_B1 * m_ref[...] + (1.0 - ADAM_B1) * g
        v_new = ADAM_B2 * v_ref[...] + (1.0 - ADAM_B2) * (g * g)
        m_hat = m_new / ADAM_BC1
        v_hat = v_new / ADAM_BC2
        g_ref[...] = g
        d_ref[...] = -ADAM_LR * (m_hat / (jnp.sqrt(v_hat) + ADAM_EPS) + ADAM_WD * w_ref[...])
        mo_ref[...] = m_new
        vo_ref[...] = v_new

    spec = pl.BlockSpec((t, c), lambda i: (i, 0))
    out = jax.ShapeDtypeStruct((r, c), F32)
    return pl.pallas_call(
        body, name=name, grid=(r // t,),
        in_specs=[spec, spec, spec, pl.BlockSpec((p, t, c), lambda i: (0, i, 0))],
        out_specs=[spec] * 4, out_shape=[out] * 4,
        compiler_params=_params("parallel"),
    )(w, m, v, parts)


def kernel(x, c, w_ada, b_ada, norm1_g, ffn1_w_gu, ffn1_w_down, norm2_g, w_mix_in, b_merge, conv_w, w_conv_out, w_attn_out, w_out, norm3_g, ffn2_w_gu, ffn2_w_down, final_g, loss_target, m_w_ada, m_b_ada, m_norm1_g, m_ffn1_w_gu, m_ffn1_w_down, m_norm2_g, m_w_mix_in, m_b_merge, m_conv_w, m_w_conv_out, m_w_attn_out, m_w_out, m_norm3_g, m_ffn2_w_gu, m_ffn2_w_down, m_final_g, v_w_ada, v_b_ada, v_norm1_g, v_ffn1_w_gu, v_ffn1_w_down, v_norm2_g, v_w_mix_in, v_b_merge, v_conv_w, v_w_conv_out, v_w_attn_out, v_w_out, v_norm3_g, v_ffn2_w_gu, v_ffn2_w_down, v_final_g):
    s = x.shape[1]
    me = 4 * lax.axis_index("x") + 2 * lax.axis_index("y") + lax.axis_index("c")
    x0 = x[0]
    target = loss_target[0]
    final_g2 = final_g.reshape(1, D_MODEL)

    def shard(w):
        return w[0].astype(BF16)

    def rows8(g):
        return g.reshape(N_DEV, -1, D_MODEL)

    got = run_exchange(gather_stage1([shard(ffn1_w_gu), shard(ffn1_w_down)]), "gather_ffn1_chips")
    wgu1, wd1 = run_exchange(gather_stage2(got), "gather_ffn1_cores")
    wd1 = wd1.reshape(4, FF_BLK, D_MODEL)

    small_in = jnp.concatenate([c.reshape(-1), b_merge.reshape(-1), conv_w.reshape(-1),
                                jnp.zeros((64,), F32)]).reshape(1, -1)
    small_all = all_gather_rows(small_in, "gather_small")[:, 0, :]
    c_all = small_all[:, :D_MODEL]
    bm_full = small_all[:, 1024:1280].reshape(8, 2, 128).transpose(1, 0, 2).reshape(2, D_MODEL)
    cw_full = small_all[:, 1280:1472].reshape(8, 3, 64).transpose(1, 0, 2).reshape(3, CONV_W)
    n_ada = w_ada.shape[2]
    b_cols = lax.dynamic_slice(b_ada, (0, me * n_ada), (1, n_ada))
    mod_part = ada_forward(c_all, w_ada[0], b_cols, "ada_forward")
    mod_all = all_gather_rows(mod_part, "gather_mod")
    mod = lax.dynamic_index_in_dim(mod_all, me, axis=1, keepdims=False).reshape(9, 1, D_MODEL)
    sh1, sc1, gt1, sh2, sc2, gt2, sh3, sc3, gt3 = [mod[n] for n in range(9)]

    u1 = norm_modulate(x0, norm1_g, sh1, sc1, "norm_mod_1")
    (gu1, act1), got = ffn_up(u1, wgu1, "ffn_up_1", exchange=gather_stage1(
        [shard(w_mix_in), shard(w_conv_out), shard(w_attn_out), shard(w_out)]))
    (x1, y1), (wmix, wco, wao, wout) = residual_matmul(act1, wd1, x0, gt1, 0.5, "ffn_down_1",
                                                     exchange=gather_stage2(got))
    wco = wco.transpose(1, 0, 2).reshape(CONV_W, D_MODEL)
    wao = wao.transpose(1, 0, 2).reshape(ATTN_W, D_MODEL)
    wout = wout.reshape(D_MODEL, D_MODEL)

    u2 = norm_modulate(x1, norm2_g, sh2, sc2, "norm_mod_2")
    proj = mix_in_proj(u2, wmix, "mix_in")
    sa = short_conv(proj, cw_full, "short_conv")
    (o, tot), got = stick_breaking_fwd(proj, "attn_fwd",
                                       exchange=gather_stage1([shard(ffn2_w_gu), shard(ffn2_w_down)]))
    (merged, ya, yb), (wgu3, wd3) = merge_forward(sa, o, proj, wco, wao, bm_full, "merge",
                                                  exchange=gather_stage2(got))
    wd3 = wd3.reshape(4, FF_BLK, D_MODEL)
    x2, y2 = residual_matmul(merged[None], wout[None], x1, gt2, 1.0, "out_proj")

    u3 = norm_modulate(x2, norm3_g, sh3, sc3, "norm_mod_3")
    gu3, act3 = ffn_up(u3, wgu3, "ffn_up_3")
    x3, y3 = residual_matmul(act3, wd3, x2, gt3, 0.5, "ffn_down_3")

    dx3, dy3, dgt3, dfinal, sq = loss_head(x3, target, final_g2, gt3, y3, 0.5, "loss_head")
    dgu3 = ffn_dact(dy3, wd3, gu3, "ffn_dact_3").reshape(8, s, FF_BLK)
    g_wd3 = rows8(matmul_tn(act3, dy3[None], "grad_w_down_3"))
    du3 = matmul_nt_acc(dgu3, wgu3, "ffn_du_3")
    g_wgu3 = matmul_tn(u3[None], dgu3, "grad_w_gu_3").reshape(8, D_MODEL, FF_BLK)
    dx2, dsh3, dsc3, dn3, dy2, dgt2 = norm_modulate_bwd(du3, x2, dx3, norm3_g, sc3, "norm_bwd_3",
                                                        prev=(gt2, y2, 1.0))

    (dya, dyb, dga, dgb, dbm), pairs = merge_backward(dy2, wout, proj, ya, yb, bm_full, "merge_bwd",
                                                      exchange=scatter_stage1([g_wgu3, g_wd3]))
    sums3 = [pair_sum(g_wgu3, pairs[0], "pair_sum_w_gu_3"), pair_sum(g_wd3, pairs[1], "pair_sum_w_down_3")]
    g_wout = rows8(matmul_tn(merged[None], dy2[None], "grad_w_out"))
    dsa, do = out_proj_bwd(dya, dyb, wco, wao, "out_proj_bwd")
    g_wco = matmul_tn(sa[None], dya[None], "grad_w_conv_out").reshape(CONV_W, N_DEV, 128).transpose(1, 0, 2)
    g_wao = matmul_tn(o[None], dyb[None], "grad_w_attn_out").reshape(ATTN_W, N_DEV, 128).transpose(1, 0, 2)
    dcb, dcc, dcx, dconv = short_conv_bwd(dsa, proj, cw_full, "short_conv_bwd")
    (dq, dk, dv), landed3 = stick_breaking_bwd(proj, do, tot, "attn_bwd", exchange=scatter_stage2(sums3))
    dproj = jnp.concatenate([dcb, dcc, dcx, dq, dk.astype(BF16), dv.astype(BF16), dga, dgb], axis=1)
    dproj8 = dproj.reshape(s, N_DEV, MIX_BLK).transpose(1, 0, 2)
    du2 = matmul_nt_acc(dproj8, wmix, "mix_in_du")
    g_wmix = matmul_tn(u2[None], dproj8, "grad_w_mix_in").reshape(N_DEV, D_MODEL, MIX_BLK)
    mixer_grads = [g_wmix, g_wco, g_wao, g_wout]
    (dx1, dsh2, dsc2, dn2, dy1, dgt1), pairs = norm_modulate_bwd(
        du2, x1, dx2, norm2_g, sc2, "norm_bwd_2", prev=(gt1, y1, 0.5), exchange=scatter_stage1(mixer_grads))
    sums_mix = [pair_sum(g, p, f"pair_sum_mixer_{n}") for n, (g, p) in enumerate(zip(mixer_grads, pairs))]

    dgu1, landed_mix = ffn_dact(dy1, wd1, gu1, "ffn_dact_1", exchange=scatter_stage2(sums_mix))
    dgu1 = dgu1.reshape(8, s, FF_BLK)
    g_wgu1 = matmul_tn(u1[None], dgu1, "grad_w_gu_1").reshape(8, D_MODEL, FF_BLK)
    g_wd1, pairs = matmul_tn(act1, dy1[None], "grad_w_down_1", exchange=scatter_stage1([g_wgu1]))
    g_wd1 = rows8(g_wd1)
    sum_gu1 = pair_sum(g_wgu1, pairs[0], "pair_sum_w_gu_1")
    du1, (landed_gu1, pair_d1) = matmul_nt_acc(
        dgu1, wgu1, "ffn_du_1", exchange=merge_exchanges(scatter_stage2([sum_gu1]), scatter_stage1([g_wd1])))
    sum_d1 = pair_sum(g_wd1, pair_d1, "pair_sum_w_down_1")
    (grad_x, dsh1, dsc1, dn1), landed_d1 = norm_modulate_bwd(du1, x0, dx1, norm1_g, sc1, "norm_bwd_1",
                                                            exchange=scatter_stage2([sum_d1]))

    loss_local = (0.5 / D_MODEL) * jnp.sum(sq)
    stats = jnp.concatenate(
        [v.reshape(-1) for v in (dsh1, dsc1, dgt1, dsh2, dsc2, dgt2, dsh3, dsc3, dgt3,
                                 dn1, dn2, dn3, dfinal, dbm, dconv)]
        + [jnp.broadcast_to(loss_local, (128,))]).reshape(1, -1)
    stats_all = all_gather_rows(stats, "gather_stats")
    n_mod = 9 * D_MODEL
    loss = jnp.sum(stats_all[:, 0, -1])
    dmod_all = stats_all[:, :, :n_mod]
    off = n_mod
    parts = {}
    for key in ("norm1_g", "norm2_g", "norm3_g", "final_g"):
        parts[key] = stats_all[:, :, off:off + D_MODEL]
        off += D_MODEL
    dbm_all = stats_all[:, 0, off:off + 2 * D_MODEL].reshape(N_DEV, 2, D_MODEL)
    off += 2 * D_MODEL
    dcw_all = stats_all[:, 0, off:off + 3 * CONV_W].reshape(N_DEV, 3, CONV_W)
    parts["b_merge"] = lax.dynamic_slice(dbm_all, (0, 0, me * 128), (N_DEV, 2, 128))
    parts["conv_w"] = lax.dynamic_slice(dcw_all, (0, 0, me * 64), (N_DEV, 3, 64))
    dmod_cols = lax.dynamic_slice(dmod_all[:, 0, :], (0, me * n_ada), (N_DEV, n_ada))
    parts["w_ada"] = ada_backward(c_all, dmod_cols, "ada_backward")[None]
    parts["b_ada"] = dmod_all
    parts["ffn2_w_gu"], parts["ffn2_w_down"] = landed3
    parts["w_mix_in"], parts["w_conv_out"], parts["w_attn_out"], parts["w_out"] = landed_mix
    parts["ffn1_w_gu"] = landed_gu1
    parts["ffn1_w_down"] = landed_d1[0]

    given = dict(w_ada=w_ada, b_ada=b_ada, norm1_g=norm1_g, ffn1_w_gu=ffn1_w_gu, ffn1_w_down=ffn1_w_down,
                 norm2_g=norm2_g, w_mix_in=w_mix_in, b_merge=b_merge, conv_w=conv_w, w_conv_out=w_conv_out,
                 w_attn_out=w_attn_out, w_out=w_out, norm3_g=norm3_g, ffn2_w_gu=ffn2_w_gu,
                 ffn2_w_down=ffn2_w_down, final_g=final_g)
    moments_m = dict(w_ada=m_w_ada, b_ada=m_b_ada, norm1_g=m_norm1_g, ffn1_w_gu=m_ffn1_w_gu,
                     ffn1_w_down=m_ffn1_w_down, norm2_g=m_norm2_g, w_mix_in=m_w_mix_in, b_merge=m_b_merge,
                     conv_w=m_conv_w, w_conv_out=m_w_conv_out, w_attn_out=m_w_attn_out, w_out=m_w_out,
                     norm3_g=m_norm3_g, ffn2_w_gu=m_ffn2_w_gu, ffn2_w_down=m_ffn2_w_down, final_g=m_final_g)
    moments_v = dict(w_ada=v_w_ada, b_ada=v_b_ada, norm1_g=v_norm1_g, ffn1_w_gu=v_ffn1_w_gu,
                     ffn1_w_down=v_ffn1_w_down, norm2_g=v_norm2_g, w_mix_in=v_w_mix_in, b_merge=v_b_merge,
                     conv_w=v_conv_w, w_conv_out=v_w_conv_out, w_attn_out=v_w_attn_out, w_out=v_w_out,
                     norm3_g=v_norm3_g, ffn2_w_gu=v_ffn2_w_gu, ffn2_w_down=v_ffn2_w_down, final_g=v_final_g)
    order = ["w_ada", "b_ada", "norm1_g", "ffn1_w_gu", "ffn1_w_down", "norm2_g", "w_mix_in", "b_merge",
             "conv_w", "w_conv_out", "w_attn_out", "w_out", "norm3_g", "ffn2_w_gu", "ffn2_w_down", "final_g"]
    grads, deltas, new_m, new_v = [], [], [], []
    for key in order:
        shape = given[key].shape
        shape2 = (1, shape[0]) if len(shape) == 1 else shape[-2:]
        outs = adamw(given[key].reshape(shape2), moments_m[key].reshape(shape2),
                     moments_v[key].reshape(shape2), parts[key], f"adamw_{key}")
        for dst, val in zip((grads, deltas, new_m, new_v), outs):
            dst.append(val.reshape(shape))

    return (loss, grad_x[None], *grads, *deltas, *new_m, *new_v)
```

```python
import functools
from typing import Callable, NamedTuple

import jax
import jax.numpy as jnp
from jax import lax
from jax.experimental import pallas as pl
from jax.experimental.pallas import tpu as pltpu

F32 = jnp.float32
BF16 = jnp.bfloat16
MESH = pl.DeviceIdType.MESH
ANY = pl.BlockSpec(memory_space=pl.ANY)

N_DEV = 8
D_MODEL = 1024
D_FF = 2816
FF_BLK = D_FF // 4
N_HEADS = 8
HEAD_DIM = 64
CONV_W = 512
ATTN_W = 512
MIX_W = 3 * CONV_W + 3 * ATTN_W + 2 * D_MODEL
MIX_BLK = MIX_W // N_DEV
EPS = 1e-6
ATTN_SCALE = HEAD_DIM ** -0.5

ADAM_LR = 0.001
ADAM_B1 = 0.9
ADAM_B2 = 0.999
ADAM_EPS = 1e-08
ADAM_WD = 0.01
ADAM_STEP = 10
ADAM_BC1 = 1.0 - ADAM_B1 ** ADAM_STEP
ADAM_BC2 = 1.0 - ADAM_B2 ** ADAM_STEP

VMEM_LIMIT = 56 * 1024 * 1024
ROW_TILE = 512
ELT_TILE = 256
ATT_BLK = 256

NN = (((1,), (0,)), ((), ()))
NT = (((1,), (1,)), ((), ()))
TN = (((0,), (0,)), ((), ()))


def _dot(a, b, dims=NN):
    return lax.dot_general(a, b, dims, preferred_element_type=F32)


def _params(*sem):
    return pltpu.CompilerParams(dimension_semantics=sem, vmem_limit_bytes=VMEM_LIMIT)


def _sigmoid(x):
    return 1.0 / (1.0 + jnp.exp(-x))


def _me():
    x, y, c = lax.axis_index("x"), lax.axis_index("y"), lax.axis_index("c")
    return x, y, c, 4 * x + 2 * y + c


def _peer(k):
    x, y, c, _ = _me()
    px = 1 - x if (k >> 2) & 1 else x
    py = 1 - y if (k >> 1) & 1 else y
    pc = 1 - c if k & 1 else c
    return (px, py, pc), 4 * px + 2 * py + pc


class Exchange(NamedTuple):
    operands: tuple
    out_shapes: tuple
    aliases: dict
    n_remote: int
    n_local: int
    copies: Callable


CHIP_FLIPS = (2, 4, 6)
SIBLING = 1


def _remote(src, dst, send_sems, recv_sems, n, peer):
    return pltpu.make_async_remote_copy(src_ref=src, dst_ref=dst, send_sem=send_sems.at[n], recv_sem=recv_sems.at[n],
                                        device_id=peer, device_id_type=MESH)


def gather_stage1(shards):
    n = len(shards)
    rels = (SIBLING,) + CHIP_FLIPS

    def copies(ins, outs, send_sems, recv_sems, local_sems, rb, lb):
        _, _, _, me = _me()
        cps = []
        for w in range(n):
            cps.append(pltpu.make_async_copy(ins[w], outs[w].at[me], local_sems.at[lb + w]))
            for a, k in enumerate(rels):
                peer, _ = _peer(k)
                cps.append(_remote(ins[w], outs[w].at[me], send_sems, recv_sems, rb + len(rels) * w + a, peer))
        return cps

    shapes = tuple(jax.ShapeDtypeStruct((N_DEV,) + s.shape, s.dtype) for s in shards)
    return Exchange(tuple(shards), shapes, {}, len(rels) * n, n, copies)


def gather_stage2(fulls):
    n = len(fulls)

    def copies(ins, outs, send_sems, recv_sems, local_sems, rb, lb):
        sibling, _ = _peer(SIBLING)
        cps = []
        for w in range(n):
            for a, k in enumerate(CHIP_FLIPS):
                _, blk = _peer(k)
                cps.append(_remote(outs[w].at[blk], outs[w].at[blk], send_sems, recv_sems, rb + 3 * w + a, sibling))
        return cps

    shapes = tuple(jax.ShapeDtypeStruct(f.shape, f.dtype) for f in fulls)
    return Exchange(tuple(fulls), shapes, {w: w for w in range(n)}, 3 * n, 0, copies)


def scatter_stage1(fulls):
    n = len(fulls)

    def copies(ins, outs, send_sems, recv_sems, local_sems, rb, lb):
        _, _, c, _ = _me()
        sibling, _ = _peer(SIBLING)
        cps = []
        for w in range(n):
            for q in range(4):
                cps.append(_remote(ins[w].at[2 * q + (1 - c)], outs[w].at[q], send_sems, recv_sems, rb + 4 * w + q, sibling))
        return cps

    shapes = tuple(jax.ShapeDtypeStruct((4,) + f.shape[1:], f.dtype) for f in fulls)
    return Exchange(tuple(fulls), shapes, {}, 4 * n, 0, copies)


def scatter_stage2(sums):
    n = len(sums)

    def copies(ins, outs, send_sems, recv_sems, local_sems, rb, lb):
        x, y, _, _ = _me()
        mine = 2 * x + y
        cps = []
        for w in range(n):
            cps.append(pltpu.make_async_copy(ins[w].at[mine], outs[w].at[mine], local_sems.at[lb + w]))
            for a, k in enumerate(CHIP_FLIPS):
                peer, _ = _peer(k)
                cps.append(_remote(ins[w].at[2 * peer[0] + peer[1]], outs[w].at[mine], send_sems, recv_sems,
                                   rb + 3 * w + a, peer))
        return cps

    shapes = tuple(jax.ShapeDtypeStruct(s.shape, s.dtype) for s in sums)
    return Exchange(tuple(sums), shapes, {}, 3 * n, n, copies)


def merge_exchanges(a, b):
    na_in, na_out = len(a.operands), len(a.out_shapes)

    def copies(ins, outs, send_sems, recv_sems, local_sems, rb, lb):
        return (a.copies(ins[:na_in], outs[:na_out], send_sems, recv_sems, local_sems, rb, lb)
                + b.copies(ins[na_in:], outs[na_out:], send_sems, recv_sems, local_sems, rb + a.n_remote, lb + a.n_local))

    aliases = dict(a.aliases)
    aliases.update({na_in + i: na_out + o for i, o in b.aliases.items()})
    return Exchange(a.operands + b.operands, a.out_shapes + b.out_shapes, aliases,
                    a.n_remote + b.n_remote, a.n_local + b.n_local, copies)


def _exchange_scratch(ex):
    return [pltpu.SemaphoreType.DMA((ex.n_remote,)), pltpu.SemaphoreType.DMA((ex.n_remote,)),
            pltpu.SemaphoreType.DMA((max(ex.n_local, 1),))]


def run_exchange(ex, name):
    n_in, n_out = len(ex.operands), len(ex.out_shapes)

    def body(*refs):
        cps = ex.copies(refs[:n_in], refs[n_in:n_in + n_out], *refs[n_in + n_out:], 0, 0)
        for cp in cps:
            cp.start()
        for cp in cps:
            cp.wait()

    return pl.pallas_call(
        body, name=name, out_shape=list(ex.out_shapes), in_specs=[ANY] * n_in, out_specs=[ANY] * n_out,
        scratch_shapes=_exchange_scratch(ex), input_output_aliases=dict(ex.aliases),
    )(*ex.operands)


def _call(body, *, name, grid, in_specs, out_specs, out_shape, operands, scratch_shapes=(), semantics=(),
          exchange=None):
    if exchange is None:
        return pl.pallas_call(
            body, name=name, grid=grid, in_specs=in_specs, out_specs=out_specs, out_shape=out_shape,
            scratch_shapes=list(scratch_shapes), compiler_params=_params(*semantics))(*operands)
    single = not isinstance(out_shape, (list, tuple))
    out_shapes = [out_shape] if single else list(out_shape)
    out_specs_l = [out_specs] if single else list(out_specs)
    n_in, n_out, n_scr = len(operands), len(out_shapes), len(scratch_shapes)
    x_in, x_out = len(exchange.operands), len(exchange.out_shapes)

    def hosted(*refs):
        ins, refs = refs[:n_in], refs[n_in:]
        xin, refs = refs[:x_in], refs[x_in:]
        outs, refs = refs[:n_out], refs[n_out:]
        xout, refs = refs[:x_out], refs[x_out:]
        scr, sems = refs[:n_scr], refs[n_scr:]
        first = functools.reduce(jnp.logical_and, [pl.program_id(a) == 0 for a in range(len(grid))])
        last = functools.reduce(jnp.logical_and, [pl.program_id(a) == g - 1 for a, g in enumerate(grid)])

        @pl.when(first)
        def _():
            for cp in exchange.copies(xin, xout, *sems, 0, 0):
                cp.start()

        body(*ins, *outs, *scr)

        @pl.when(last)
        def _():
            for cp in exchange.copies(xin, xout, *sems, 0, 0):
                cp.wait()

    res = pl.pallas_call(
        hosted, name=name, grid=grid,
        in_specs=list(in_specs) + [ANY] * x_in, out_specs=out_specs_l + [ANY] * x_out,
        out_shape=out_shapes + list(exchange.out_shapes),
        scratch_shapes=list(scratch_shapes) + _exchange_scratch(exchange),
        input_output_aliases={n_in + i: n_out + o for i, o in exchange.aliases.items()},
        compiler_params=_params(*(["arbitrary"] * len(grid))),
    )(*operands, *exchange.operands)
    outs, xouts = res[:n_out], res[n_out:]
    return (outs[0] if single else outs), xouts


def all_gather_rows(v, name):
    r, n = v.shape

    def body(v_ref, out_ref, send_sems, recv_sems):
        _, _, _, me = _me()
        out_ref[me] = v_ref[...]
        copies = []
        for k in range(1, N_DEV):
            peer, _ = _peer(k)
            copies.append(_remote(v_ref, out_ref.at[me], send_sems, recv_sems, k - 1, peer))
        for cp in copies:
            cp.start()
        for cp in copies:
            cp.wait()

    return pl.pallas_call(
        body, name=name,
        out_shape=jax.ShapeDtypeStruct((N_DEV, r, n), v.dtype),
        in_specs=[pl.BlockSpec(memory_space=pltpu.VMEM)],
        out_specs=pl.BlockSpec(memory_space=pltpu.VMEM),
        scratch_shapes=[pltpu.SemaphoreType.DMA((N_DEV - 1,)), pltpu.SemaphoreType.DMA((N_DEV - 1,))],
    )(v)


def pair_sum(full, pair, name):
    _, r, c = full.shape
    t = 256 if r % 256 == 0 and r > 256 else r
    core = lax.axis_index("c").astype(jnp.int32).reshape(1)

    def body(core_ref, f_ref, p_ref, o_ref):
        o_ref[...] = (f_ref[...].astype(F32) + p_ref[...].astype(F32)).astype(BF16)

    return pl.pallas_call(
        body, name=name,
        grid_spec=pltpu.PrefetchScalarGridSpec(
            num_scalar_prefetch=1, grid=(4, r // t),
            in_specs=[pl.BlockSpec((None, None, t, c), lambda q, i, core_ref: (q, core_ref[0], i, 0)),
                      pl.BlockSpec((None, t, c), lambda q, i, core_ref: (q, i, 0))],
            out_specs=pl.BlockSpec((None, t, c), lambda q, i, core_ref: (q, i, 0))),
        out_shape=jax.ShapeDtypeStruct((4, r, c), BF16),
        compiler_params=_params("parallel", "parallel"),
    )(core, full.reshape(4, 2, r, c), pair)


def ada_forward(c_all, w_ada, b_cols, name):
    n = w_ada.shape[1]

    def body(c_ref, w_ref, b_ref, o_ref):
        c = c_ref[...]
        act = c * _sigmoid(c)
        o_ref[...] = jnp.dot(act, w_ref[...], precision=lax.Precision.HIGHEST,
                             preferred_element_type=F32) + b_ref[...]

    return pl.pallas_call(
        body, name=name, out_shape=jax.ShapeDtypeStruct((N_DEV, n), F32),
        compiler_params=pltpu.CompilerParams(vmem_limit_bytes=VMEM_LIMIT),
    )(c_all, w_ada, b_cols)


def ada_backward(c_all, dmod_cols, name):
    n = dmod_cols.shape[1]

    def body(c_ref, d_ref, o_ref):
        c = c_ref[...]
        act = c * _sigmoid(c)
        o_ref[...] = lax.dot_general(act, d_ref[...], TN, precision=lax.Precision.HIGHEST,
                                     preferred_element_type=F32)

    return pl.pallas_call(
        body, name=name, out_shape=jax.ShapeDtypeStruct((D_MODEL, n), F32),
        compiler_params=pltpu.CompilerParams(vmem_limit_bytes=VMEM_LIMIT),
    )(c_all, dmod_cols)


def _row_spec(t, width=D_MODEL):
    return pl.BlockSpec((t, width), lambda i: (i, 0))


def _vec_spec(rows=1, width=D_MODEL):
    return pl.BlockSpec((rows, width), lambda i: (0, 0))


def norm_modulate(x, g, shift, scale, name):
    s = x.shape[0]
    t = min(ELT_TILE, s)

    def body(x_ref, g_ref, sh_ref, sc_ref, u_ref):
        xv = x_ref[...]
        r = lax.rsqrt(jnp.mean(xv * xv, axis=-1, keepdims=True) + EPS)
        a = (xv * r) * g_ref[...]
        u_ref[...] = (a * (1.0 + sc_ref[...]) + sh_ref[...]).astype(BF16)

    return pl.pallas_call(
        body, name=name, grid=(s // t,),
        in_specs=[_row_spec(t), _vec_spec(), _vec_spec(), _vec_spec()],
        out_specs=_row_spec(t),
        out_shape=jax.ShapeDtypeStruct((s, D_MODEL), BF16),
        compiler_params=_params("parallel"),
    )(x, g, shift, scale)


def loss_head(x, target, final_g, gate, y_prev, coef, name):
    s = x.shape[0]
    t = min(ELT_TILE, s)

    def body(x_ref, t_ref, fg_ref, gt_ref, y_ref, dx_ref, dy_ref, dgt_ref, dfg_ref, sq_ref):
        @pl.when(pl.program_id(0) == 0)
        def _():
            dgt_ref[...] = jnp.zeros_like(dgt_ref)
            dfg_ref[...] = jnp.zeros_like(dfg_ref)
            sq_ref[...] = jnp.zeros_like(sq_ref)

        xv = x_ref[...]
        fg = fg_ref[...]
        r = lax.rsqrt(jnp.mean(xv * xv, axis=-1, keepdims=True) + EPS)
        nrm = xv * r
        err = nrm * fg - t_ref[...]
        sq_ref[...] += jnp.sum(err * err, axis=0, keepdims=True)
        dout = err * (1.0 / D_MODEL)
        dfg_ref[...] += jnp.sum(dout * nrm, axis=0, keepdims=True)
        dn = dout * fg
        dx = r * (dn - nrm * jnp.mean(dn * nrm, axis=-1, keepdims=True))
        dx_ref[...] = dx
        dy_ref[...] = (coef * gt_ref[...] * dx).astype(BF16)
        dgt_ref[...] += coef * jnp.sum(dx * y_ref[...].astype(F32), axis=0, keepdims=True)

    vec = jax.ShapeDtypeStruct((1, D_MODEL), F32)
    return pl.pallas_call(
        body, name=name, grid=(s // t,),
        in_specs=[_row_spec(t), _row_spec(t), _vec_spec(), _vec_spec(), _row_spec(t)],
        out_specs=[_row_spec(t), _row_spec(t), _vec_spec(), _vec_spec(), _vec_spec()],
        out_shape=[jax.ShapeDtypeStruct((s, D_MODEL), F32), jax.ShapeDtypeStruct((s, D_MODEL), BF16),
                   vec, vec, vec],
        compiler_params=_params("arbitrary"),
    )(x, target, final_g, gate, y_prev)


def norm_modulate_bwd(du, x, dx_out, g, scale, name, prev=None, exchange=None):
    s = x.shape[0]
    t = min(ELT_TILE, s)
    has_prev = prev is not None

    def body(*refs):
        du_ref, x_ref, dxo_ref, g_ref, sc_ref = refs[:5]
        refs = refs[5:]
        if has_prev:
            gt_ref, y_ref = refs[:2]
            refs = refs[2:]
        dx_ref, dsh_ref, dsc_ref, dg_ref = refs[:4]

        @pl.when(pl.program_id(0) == 0)
        def _():
            dsh_ref[...] = jnp.zeros_like(dsh_ref)
            dsc_ref[...] = jnp.zeros_like(dsc_ref)
            dg_ref[...] = jnp.zeros_like(dg_ref)
            if has_prev:
                refs[5][...] = jnp.zeros_like(refs[5])

        xv = x_ref[...]
        duv = du_ref[...]
        gv = g_ref[...]
        r = lax.rsqrt(jnp.mean(xv * xv, axis=-1, keepdims=True) + EPS)
        nrm = xv * r
        a = nrm * gv
        dsh_ref[...] += jnp.sum(duv, axis=0, keepdims=True)
        dsc_ref[...] += jnp.sum(duv * a, axis=0, keepdims=True)
        da = duv * (1.0 + sc_ref[...])
        dg_ref[...] += jnp.sum(da * nrm, axis=0, keepdims=True)
        dn = da * gv
        dx = dxo_ref[...] + r * (dn - nrm * jnp.mean(dn * nrm, axis=-1, keepdims=True))
        dx_ref[...] = dx
        if has_prev:
            coef = prev[2]
            refs[4][...] = (coef * gt_ref[...] * dx).astype(BF16)
            refs[5][...] += coef * jnp.sum(dx * y_ref[...].astype(F32), axis=0, keepdims=True)

    vec = jax.ShapeDtypeStruct((1, D_MODEL), F32)
    operands = [du, x, dx_out, g, scale]
    in_specs = [_row_spec(t), _row_spec(t), _row_spec(t), _vec_spec(), _vec_spec()]
    out_specs = [_row_spec(t), _vec_spec(), _vec_spec(), _vec_spec()]
    out_shape = [jax.ShapeDtypeStruct((s, D_MODEL), F32), vec, vec, vec]
    if has_prev:
        operands += [prev[0], prev[1]]
        in_specs += [_vec_spec(), _row_spec(t)]
        out_specs += [_row_spec(t), _vec_spec()]
        out_shape += [jax.ShapeDtypeStruct((s, D_MODEL), BF16), vec]
    return _call(body, name=name, grid=(s // t,), in_specs=in_specs, out_specs=out_specs, out_shape=out_shape,
                 operands=operands, semantics=("arbitrary",), exchange=exchange)


def ffn_up(u, w_gu, name, exchange=None):
    s = u.shape[0]
    t = min(ROW_TILE, s)

    def body(u_ref, wg_ref, wu_ref, gu_ref, act_ref):
        uv = u_ref[...]
        g = _dot(uv, wg_ref[...])
        up = _dot(uv, wu_ref[...])
        gu_ref[0] = g.astype(BF16)
        gu_ref[1] = up.astype(BF16)
        act_ref[...] = (g * _sigmoid(g) * up).astype(BF16)

    return _call(
        body, name=name, grid=(4, s // t),
        in_specs=[pl.BlockSpec((t, D_MODEL), lambda j, i: (i, 0)),
                  pl.BlockSpec((None, D_MODEL, FF_BLK), lambda j, i: (j, 0, 0)),
                  pl.BlockSpec((None, D_MODEL, FF_BLK), lambda j, i: (j + 4, 0, 0))],
        out_specs=[pl.BlockSpec((2, None, t, FF_BLK), lambda j, i: (0, j, i, 0)),
                   pl.BlockSpec((None, t, FF_BLK), lambda j, i: (j, i, 0))],
        out_shape=[jax.ShapeDtypeStruct((2, 4, s, FF_BLK), BF16),
                   jax.ShapeDtypeStruct((4, s, FF_BLK), BF16)],
        operands=(u, w_gu, w_gu), semantics=("parallel", "parallel"), exchange=exchange)


def residual_matmul(a, b, x, gate, coef, name, exchange=None):
    nk, s, kb = a.shape
    t = min(ROW_TILE, s)

    def body(a_ref, b_ref, x_ref, gt_ref, xo_ref, y_ref, acc_ref):
        k = pl.program_id(1)

        @pl.when(k == 0)
        def _():
            acc_ref[...] = jnp.zeros_like(acc_ref)

        acc_ref[...] += _dot(a_ref[...], b_ref[...])

        @pl.when(k == nk - 1)
        def _():
            y = acc_ref[...]
            y_ref[...] = y.astype(BF16)
            xo_ref[...] = x_ref[...] + coef * gt_ref[...] * y

    return _call(
        body, name=name, grid=(s // t, nk),
        in_specs=[pl.BlockSpec((None, t, kb), lambda i, k: (k, i, 0)),
                  pl.BlockSpec((None, kb, D_MODEL), lambda i, k: (k, 0, 0)),
                  pl.BlockSpec((t, D_MODEL), lambda i, k: (i, 0)),
                  pl.BlockSpec((1, D_MODEL), lambda i, k: (0, 0))],
        out_specs=[pl.BlockSpec((t, D_MODEL), lambda i, k: (i, 0)),
                   pl.BlockSpec((t, D_MODEL), lambda i, k: (i, 0))],
        out_shape=[jax.ShapeDtypeStruct((s, D_MODEL), F32), jax.ShapeDtypeStruct((s, D_MODEL), BF16)],
        operands=(a, b, x, gate), scratch_shapes=[pltpu.VMEM((t, D_MODEL), F32)],
        semantics=("parallel", "arbitrary"), exchange=exchange)


def ffn_dact(dy, w_down, gu, name, exchange=None):
    s = dy.shape[0]
    t = min(ROW_TILE, s)

    def body(dy_ref, w_ref, gu_ref, dgu_ref):
        dact = _dot(dy_ref[...], w_ref[...], NT)
        g = gu_ref[0].astype(F32)
        up = gu_ref[1].astype(F32)
        sg = _sigmoid(g)
        dgu_ref[0] = (dact * up * sg * (1.0 + g * (1.0 - sg))).astype(BF16)
        dgu_ref[1] = (dact * g * sg).astype(BF16)

    return _call(
        body, name=name, grid=(4, s // t),
        in_specs=[pl.BlockSpec((t, D_MODEL), lambda j, i: (i, 0)),
                  pl.BlockSpec((None, FF_BLK, D_MODEL), lambda j, i: (j, 0, 0)),
                  pl.BlockSpec((2, None, t, FF_BLK), lambda j, i: (0, j, i, 0))],
        out_specs=pl.BlockSpec((2, None, t, FF_BLK), lambda j, i: (0, j, i, 0)),
        out_shape=jax.ShapeDtypeStruct((2, 4, s, FF_BLK), BF16),
        operands=(dy, w_down, gu), semantics=("parallel", "parallel"), exchange=exchange)


def matmul_nt_acc(a, b, name, exchange=None):
    nk, s, n = a.shape
    d = b.shape[1]
    t = min(ROW_TILE, s)

    def body(a_ref, b_ref, o_ref, acc_ref):
        k = pl.program_id(1)

        @pl.when(k == 0)
        def _():
            acc_ref[...] = jnp.zeros_like(acc_ref)

        acc_ref[...] += _dot(a_ref[...], b_ref[...], NT)

        @pl.when(k == nk - 1)
        def _():
            o_ref[...] = acc_ref[...]

    return _call(
        body, name=name, grid=(s // t, nk),
        in_specs=[pl.BlockSpec((None, t, n), lambda i, k: (k, i, 0)),
                  pl.BlockSpec((None, d, n), lambda i, k: (k, 0, 0))],
        out_specs=pl.BlockSpec((t, d), lambda i, k: (i, 0)),
        out_shape=jax.ShapeDtypeStruct((s, d), F32),
        operands=(a, b), scratch_shapes=[pltpu.VMEM((t, d), F32)],
        semantics=("parallel", "arbitrary"), exchange=exchange)


def matmul_tn(a, b, name, exchange=None):
    ja, s, m = a.shape
    jb, _, n = b.shape
    t = min(ROW_TILE, s)
    nk = s // t

    def body(a_ref, b_ref, o_ref, acc_ref):
        k = pl.program_id(2)

        @pl.when(k == 0)
        def _():
            acc_ref[...] = jnp.zeros_like(acc_ref)

        acc_ref[...] += _dot(a_ref[...], b_ref[...], TN)

        @pl.when(k == nk - 1)
        def _():
            o_ref[...] = acc_ref[...].astype(BF16)

    return _call(
        body, name=name, grid=(ja, jb, nk),
        in_specs=[pl.BlockSpec((None, t, m), lambda p, q, k: (p, k, 0)),
                  pl.BlockSpec((None, t, n), lambda p, q, k: (q, k, 0))],
        out_specs=pl.BlockSpec((None, None, m, n), lambda p, q, k: (p, q, 0, 0)),
        out_shape=jax.ShapeDtypeStruct((ja, jb, m, n), BF16),
        operands=(a, b), scratch_shapes=[pltpu.VMEM((m, n), F32)],
        semantics=("parallel", "parallel", "arbitrary"), exchange=exchange)


def mix_in_proj(u, w_mix, name):
    s = u.shape[0]
    t = min(ROW_TILE, s)

    def body(u_ref, w_ref, o_ref):
        o_ref[...] = _dot(u_ref[...], w_ref[...]).astype(BF16)

    return pl.pallas_call(
        body, name=name, grid=(N_DEV, s // t),
        in_specs=[pl.BlockSpec((t, D_MODEL), lambda j, i: (i, 0)),
                  pl.BlockSpec((None, D_MODEL, MIX_BLK), lambda j, i: (j, 0, 0))],
        out_specs=pl.BlockSpec((t, MIX_BLK), lambda j, i: (i, j)),
        out_shape=jax.ShapeDtypeStruct((s, MIX_W), BF16),
        compiler_params=_params("parallel", "parallel"),
    )(u, w_mix)


def _conv_taps(cc_ref, cx_ref, s):
    v = cc_ref[...].astype(F32) * cx_ref[...].astype(F32)
    tok = lax.broadcasted_iota(jnp.int32, v.shape, 0)
    v1 = jnp.where(tok >= 1, pltpu.roll(v, 1, 0), 0.0)
    v2 = jnp.where(tok >= 2, pltpu.roll(v, 2, 0), 0.0)
    return v, v1, v2, tok


def _proj_cols(s, first):
    return pl.BlockSpec((s, 128), lambda j: (0, first + j))


def short_conv(proj, conv_w, name):
    s = proj.shape[0]

    def body(cb_ref, cc_ref, cx_ref, w_ref, o_ref):
        v, v1, v2, _ = _conv_taps(cc_ref, cx_ref, s)
        y = w_ref[0:1, :] * v2 + w_ref[1:2, :] * v1 + w_ref[2:3, :] * v
        o_ref[...] = (cb_ref[...].astype(F32) * y).astype(BF16)

    return pl.pallas_call(
        body, name=name, grid=(CONV_W // 128,),
        in_specs=[_proj_cols(s, 0), _proj_cols(s, 4), _proj_cols(s, 8),
                  pl.BlockSpec((3, 128), lambda j: (0, j))],
        out_specs=pl.BlockSpec((s, 128), lambda j: (0, j)),
        out_shape=jax.ShapeDtypeStruct((s, CONV_W), BF16),
        compiler_params=_params("parallel"),
    )(proj, proj, proj, conv_w)


def short_conv_bwd(dsa, proj, conv_w, name):
    s = proj.shape[0]

    def body(dsa_ref, cb_ref, cc_ref, cx_ref, w_ref, dcb_ref, dcc_ref, dcx_ref, dw_ref):
        v, v1, v2, tok = _conv_taps(cc_ref, cx_ref, s)
        w0, w1, w2 = w_ref[0:1, :], w_ref[1:2, :], w_ref[2:3, :]
        y = w0 * v2 + w1 * v1 + w2 * v
        dsa_v = dsa_ref[...].astype(F32)
        dcb_ref[...] = (dsa_v * y).astype(BF16)
        dy = dsa_v * cb_ref[...].astype(F32)
        dw_ref[0:1, :] = jnp.sum(dy * v2, axis=0, keepdims=True)
        dw_ref[1:2, :] = jnp.sum(dy * v1, axis=0, keepdims=True)
        dw_ref[2:3, :] = jnp.sum(dy * v, axis=0, keepdims=True)
        dy1 = jnp.where(tok < s - 1, pltpu.roll(dy, s - 1, 0), 0.0)
        dy2 = jnp.where(tok < s - 2, pltpu.roll(dy, s - 2, 0), 0.0)
        dv = w2 * dy + w1 * dy1 + w0 * dy2
        dcc_ref[...] = (dv * cx_ref[...].astype(F32)).astype(BF16)
        dcx_ref[...] = (dv * cc_ref[...].astype(F32)).astype(BF16)

    col = pl.BlockSpec((s, 128), lambda j: (0, j))
    act = jax.ShapeDtypeStruct((s, CONV_W), BF16)
    return pl.pallas_call(
        body, name=name, grid=(CONV_W // 128,),
        in_specs=[col, _proj_cols(s, 0), _proj_cols(s, 4), _proj_cols(s, 8),
                  pl.BlockSpec((3, 128), lambda j: (0, j))],
        out_specs=[col, col, col, pl.BlockSpec((3, 128), lambda j: (0, j))],
        out_shape=[act, act, act, jax.ShapeDtypeStruct((3, CONV_W), F32)],
        compiler_params=_params("parallel"),
    )(dsa, proj, proj, proj, conv_w)


def _gate_specs(t):
    return [pl.BlockSpec((t, D_MODEL), lambda i: (i, 3)), pl.BlockSpec((t, D_MODEL), lambda i: (i, 4))]


def merge_forward(sa, o, proj, w_co, w_ao, b_merge, name, exchange=None):
    s = sa.shape[0]
    t = min(ROW_TILE, s)

    def body(sa_ref, o_ref, ga_ref, gb_ref, wco_ref, wao_ref, bm_ref, mg_ref, ya_ref, yb_ref):
        ya = _dot(sa_ref[...], wco_ref[...])
        yb = _dot(o_ref[...], wao_ref[...])
        sga = _sigmoid(ga_ref[...].astype(F32) + bm_ref[0:1, :])
        sgb = _sigmoid(gb_ref[...].astype(F32) + bm_ref[1:2, :])
        mg_ref[...] = (sga * ya + sgb * yb).astype(BF16)
        ya_ref[...] = ya.astype(BF16)
        yb_ref[...] = yb.astype(BF16)

    act = jax.ShapeDtypeStruct((s, D_MODEL), BF16)
    return _call(
        body, name=name, grid=(s // t,),
        in_specs=[_row_spec(t, CONV_W), _row_spec(t, ATTN_W)] + _gate_specs(t)
        + [_vec_spec(CONV_W), _vec_spec(ATTN_W), _vec_spec(2)],
        out_specs=[_row_spec(t)] * 3, out_shape=[act, act, act],
        operands=(sa, o, proj, proj, w_co, w_ao, b_merge), semantics=("parallel",), exchange=exchange)


def merge_backward(dy, w_out, proj, ya, yb, b_merge, name, exchange=None):
    s = dy.shape[0]
    t = min(ROW_TILE, s)

    def body(dy_ref, w_ref, ga_ref, gb_ref, ya_ref, yb_ref, bm_ref,
             dya_ref, dyb_ref, dga_ref, dgb_ref, dbm_ref):
        @pl.when(pl.program_id(0) == 0)
        def _():
            dbm_ref[...] = jnp.zeros_like(dbm_ref)

        dmg = _dot(dy_ref[...], w_ref[...], NT)
        sga = _sigmoid(ga_ref[...].astype(F32) + bm_ref[0:1, :])
        sgb = _sigmoid(gb_ref[...].astype(F32) + bm_ref[1:2, :])
        dya_ref[...] = (dmg * sga).astype(BF16)
        dyb_ref[...] = (dmg * sgb).astype(BF16)
        dga = dmg * ya_ref[...].astype(F32) * sga * (1.0 - sga)
        dgb = dmg * yb_ref[...].astype(F32) * sgb * (1.0 - sgb)
        dga_ref[...] = dga.astype(BF16)
        dgb_ref[...] = dgb.astype(BF16)
        dbm_ref[0:1, :] += jnp.sum(dga, axis=0, keepdims=True)
        dbm_ref[1:2, :] += jnp.sum(dgb, axis=0, keepdims=True)

    act = jax.ShapeDtypeStruct((s, D_MODEL), BF16)
    return _call(
        body, name=name, grid=(s // t,),
        in_specs=[_row_spec(t), _vec_spec(D_MODEL)] + _gate_specs(t)
        + [_row_spec(t), _row_spec(t), _vec_spec(2)],
        out_specs=[_row_spec(t)] * 4 + [_vec_spec(2)],
        out_shape=[act] * 4 + [jax.ShapeDtypeStruct((2, D_MODEL), F32)],
        operands=(dy, w_out, proj, proj, ya, yb, b_merge), semantics=("arbitrary",), exchange=exchange)


def out_proj_bwd(dya, dyb, w_co, w_ao, name):
    s = dya.shape[0]
    t = min(ROW_TILE, s)

    def body(dya_ref, dyb_ref, wco_ref, wao_ref, dsa_ref, do_ref):
        dsa_ref[...] = _dot(dya_ref[...], wco_ref[...], NT).astype(BF16)
        do_ref[...] = _dot(dyb_ref[...], wao_ref[...], NT).astype(BF16)

    return pl.pallas_call(
        body, name=name, grid=(s // t,),
        in_specs=[_row_spec(t), _row_spec(t), _vec_spec(CONV_W), _vec_spec(ATTN_W)],
        out_specs=[_row_spec(t, CONV_W), _row_spec(t, ATTN_W)],
        out_shape=[jax.ShapeDtypeStruct((s, CONV_W), BF16), jax.ShapeDtypeStruct((s, ATTN_W), BF16)],
        compiler_params=_params("parallel"),
    )(dya, dyb, w_co, w_ao)


ATT_HEADS = 4
ATT_LANES = ATT_HEADS * HEAD_DIM


def _softplus(z):
    return jnp.maximum(z, 0.0) + jnp.log(1.0 + jnp.exp(-jnp.abs(z)))


def _split_sum(x, tri2):
    hi = x.astype(BF16)
    lo = (x - hi.astype(F32)).astype(BF16)
    return _dot(jnp.concatenate([hi, lo], axis=1), tri2)


def _head_masks(rows):
    lane = lax.broadcasted_iota(jnp.int32, (rows, ATT_LANES), 1)
    return [(lane >= h * HEAD_DIM) & (lane < (h + 1) * HEAD_DIM) for h in range(ATT_HEADS)]


def _per_head(x, masks):
    return [jnp.where(m, x, jnp.zeros_like(x)) for m in masks]


def _att_specs(s, blk):
    first = {"q": 3 * CONV_W // ATT_LANES, "k": (3 * CONV_W + ATTN_W) // ATT_LANES,
             "v": (3 * CONV_W + 2 * ATTN_W) // ATT_LANES}
    return [pl.BlockSpec((blk, ATT_LANES), lambda h, i: (i, first["q"] + h)),
            pl.BlockSpec((s, ATT_LANES), lambda h, i: (0, first["k"] + h)),
            pl.BlockSpec((s, ATT_LANES), lambda h, i: (0, first["v"] + h))]


def stick_breaking_fwd(proj, name, exchange=None):
    s = proj.shape[0]
    blk = ATT_BLK
    nq = s // blk

    def body(q_ref, k_ref, v_ref, o_ref, tot_ref):
        i = pl.program_id(1)
        row = lax.broadcasted_iota(jnp.int32, (blk, blk), 0)
        col = lax.broadcasted_iota(jnp.int32, (blk, blk), 1)
        tri = (row >= col).astype(BF16)
        tri2 = jnp.concatenate([tri, tri], axis=0)
        causal = col < row
        masks = _head_masks(blk)
        qs = _per_head(q_ref[...] * ATTN_SCALE, masks)

        def step(j, carry, diagonal):
            laters, acc = carry
            rows = pl.ds(pl.multiple_of(j * blk, blk), blk)
            kb = k_ref[rows, :]
            probs, new_laters = [], []
            for h in range(ATT_HEADS):
                z = _dot(qs[h], kb, NT)
                sp = _softplus(z)
                if diagonal:
                    sp = jnp.where(causal, sp, 0.0)
                a = jnp.exp(z - (_split_sum(sp, tri2) + laters[h]))
                if diagonal:
                    a = jnp.where(causal, a, 0.0)
                probs.append(a.astype(BF16))
                new_laters.append(laters[h] + jnp.sum(sp, axis=1, keepdims=True))
            v_heads = jnp.concatenate(_per_head(v_ref[rows, :], masks), axis=0)
            acc = acc + _dot(jnp.concatenate(probs, axis=1), v_heads)
            return tuple(new_laters), acc

        carry = (tuple(jnp.zeros((blk, 1), F32) for _ in range(ATT_HEADS)), jnp.zeros((blk, ATT_LANES), F32))
        carry = step(i, carry, True)
        laters, acc = lax.fori_loop(0, i, lambda n, c: step(i - 1 - n, c, False), carry)
        o_ref[...] = acc.astype(BF16)
        tot = jnp.zeros((blk, ATT_LANES), F32)
        for h in range(ATT_HEADS):
            tot = jnp.where(masks[h], laters[h], tot)
        tot_ref[...] = tot

    out_spec = pl.BlockSpec((blk, ATT_LANES), lambda h, i: (i, h))
    return _call(
        body, name=name, grid=(N_HEADS // ATT_HEADS, nq),
        in_specs=_att_specs(s, blk), out_specs=[out_spec, out_spec],
        out_shape=[jax.ShapeDtypeStruct((s, ATTN_W), BF16), jax.ShapeDtypeStruct((s, ATTN_W), F32)],
        operands=(proj, proj, proj), semantics=("parallel", "arbitrary"), exchange=exchange)


def stick_breaking_bwd(proj, do, tot, name, exchange=None):
    s = proj.shape[0]
    blk = ATT_BLK
    nq = s // blk

    def body(q_ref, k_ref, v_ref, do_ref, tot_ref, dq_ref, dk_ref, dv_ref):
        i = pl.program_id(1)

        @pl.when(i == 0)
        def _():
            dk_ref[...] = jnp.zeros_like(dk_ref)
            dv_ref[...] = jnp.zeros_like(dv_ref)

        row = lax.broadcasted_iota(jnp.int32, (blk, blk), 0)
        col = lax.broadcasted_iota(jnp.int32, (blk, blk), 1)
        before = (row < col).astype(BF16)
        before2 = jnp.concatenate([before, before], axis=0)
        upto = (row <= col).astype(BF16)
        causal = col < row
        masks = _head_masks(blk)
        qs = _per_head(q_ref[...] * ATTN_SCALE, masks)
        dos = _per_head(do_ref[...], masks)
        q_heads = jnp.concatenate(qs, axis=0)
        do_heads = jnp.concatenate(dos, axis=0)
        tot_all = tot_ref[...]
        totals = [jnp.max(jnp.where(m, tot_all, 0.0), axis=1, keepdims=True) for m in masks]

        def step(j, carry, diagonal):
            earliers, g_sums, dq = carry
            rows = pl.ds(pl.multiple_of(j * blk, blk), blk)
            kb = k_ref[rows, :]
            vb = v_ref[rows, :]
            probs, dzs, new_earliers, new_g_sums = [], [], [], []
            for h in range(ATT_HEADS):
                z = _dot(qs[h], kb, NT)
                sp = _softplus(z)
                if diagonal:
                    sp = jnp.where(causal, sp, 0.0)
                c = (totals[h] - earliers[h]) - _split_sum(sp, before2)
                a = jnp.exp(z - c)
                if diagonal:
                    a = jnp.where(causal, a, 0.0)
                g = a * _dot(dos[h], vb, NT)
                f = g_sums[h] + _dot(g.astype(BF16), upto)
                dz = g - jnp.exp(z - sp) * f
                if diagonal:
                    dz = jnp.where(causal, dz, 0.0)
                probs.append(a.astype(BF16))
                dzs.append(dz.astype(BF16))
                new_earliers.append(earliers[h] + jnp.sum(sp, axis=1, keepdims=True))
                new_g_sums.append(g_sums[h] + jnp.sum(g, axis=1, keepdims=True))
            k_heads = jnp.concatenate(_per_head(kb, masks), axis=0)
            dq = dq + _dot(jnp.concatenate(dzs, axis=1), k_heads)
            dk_ref[rows, :] += _dot(jnp.concatenate(dzs, axis=0), q_heads, TN)
            dv_ref[rows, :] += _dot(jnp.concatenate(probs, axis=0), do_heads, TN)
            return tuple(new_earliers), tuple(new_g_sums), dq

        zeros = tuple(jnp.zeros((blk, 1), F32) for _ in range(ATT_HEADS))
        carry = (zeros, zeros, jnp.zeros((blk, ATT_LANES), F32))
        carry = lax.fori_loop(0, i, lambda j, c: step(j, c, False), carry)
        dq = step(i, carry, True)[2]
        dq_ref[...] = (dq * ATTN_SCALE).astype(BF16)

    blk_spec = pl.BlockSpec((blk, ATT_LANES), lambda h, i: (i, h))
    full_spec = pl.BlockSpec((s, ATT_LANES), lambda h, i: (0, h))
    return _call(
        body, name=name, grid=(N_HEADS // ATT_HEADS, nq),
        in_specs=_att_specs(s, blk) + [blk_spec, blk_spec],
        out_specs=[blk_spec, full_spec, full_spec],
        out_shape=[jax.ShapeDtypeStruct((s, ATTN_W), BF16), jax.ShapeDtypeStruct((s, ATTN_W), F32),
                   jax.ShapeDtypeStruct((s, ATTN_W), F32)],
        operands=(proj, proj, proj, do, tot), semantics=("parallel", "arbitrary"), exchange=exchange)


def adamw(w, m, v, parts, name):
    r, c = w.shape
    p = parts.shape[0]
    t = r
    for cand in (256, 176):
        if r % cand == 0 and r > cand:
            t = cand
            break

    def body(w_ref, m_ref, v_ref, p_ref, g_ref, d_ref, mo_ref, vo_ref):
        g = p_ref[0].astype(F32)
        for n in range(1, p):
            g = g + p_ref[n].astype(F32)
        m_new = ADAM_B1 * m_ref[...] + (1.0 - ADAM_B1) * g
        v_new = ADAM_B2 * v_ref[...] + (1.0 - ADAM_B2) * (g * g)
        m_hat = m_new / ADAM_BC1
        v_hat = v_new / ADAM_BC2
        g_ref[...] = g
        d_ref[...] = -ADAM_LR * (m_hat / (jnp.sqrt(v_hat) + ADAM_EPS) + ADAM_WD * w_ref[...])
        mo_ref[...] = m_new
        vo_ref[...] = v_new

    spec = pl.BlockSpec((t, c), lambda i: (i, 0))
    out = jax.ShapeDtypeStruct((r, c), F32)
    return pl.pallas_call(
        body, name=name, grid=(r // t,),
        in_specs=[spec, spec, spec, pl.BlockSpec((p, t, c), lambda i: (0, i, 0))],
        out_specs=[spec] * 4, out_shape=[out] * 4,
        compiler_params=_params("parallel"),
    )(w, m, v, parts)


def kernel(x, c, w_ada, b_ada, norm1_g, ffn1_w_gu, ffn1_w_down, norm2_g, w_mix_in, b_merge, conv_w, w_conv_out, w_attn_out, w_out, norm3_g, ffn2_w_gu, ffn2_w_down, final_g, loss_target, m_w_ada, m_b_ada, m_norm1_g, m_ffn1_w_gu, m_ffn1_w_down, m_norm2_g, m_w_mix_in, m_b_merge, m_conv_w, m_w_conv_out, m_w_attn_out, m_w_out, m_norm3_g, m_ffn2_w_gu, m_ffn2_w_down, m_final_g, v_w_ada, v_b_ada, v_norm1_g, v_ffn1_w_gu, v_ffn1_w_down, v_norm2_g, v_w_mix_in, v_b_merge, v_conv_w, v_w_conv_out, v_w_attn_out, v_w_out, v_norm3_g, v_ffn2_w_gu, v_ffn2_w_down, v_final_g):
    s = x.shape[1]
    me = 4 * lax.axis_index("x") + 2 * lax.axis_index("y") + lax.axis_index("c")
    x0 = x[0]
    target = loss_target[0]
    final_g2 = final_g.reshape(1, D_MODEL)

    def shard(w):
        return w[0].astype(BF16)

    def rows8(g):
        return g.reshape(N_DEV, -1, D_MODEL)

    got = run_exchange(gather_stage1([shard(ffn1_w_gu), shard(ffn1_w_down)]), "gather_ffn1_chips")
    wgu1, wd1 = run_exchange(gather_stage2(got), "gather_ffn1_cores")
    wd1 = wd1.reshape(4, FF_BLK, D_MODEL)

    small_in = jnp.concatenate([c.reshape(-1), b_merge.reshape(-1), conv_w.reshape(-1),
                                jnp.zeros((64,), F32)]).reshape(1, -1)
    small_all = all_gather_rows(small_in, "gather_small")[:, 0, :]
    c_all = small_all[:, :D_MODEL]
    bm_full = small_all[:, 1024:1280].reshape(8, 2, 128).transpose(1, 0, 2).reshape(2, D_MODEL)
    cw_full = small_all[:, 1280:1472].reshape(8, 3, 64).transpose(1, 0, 2).reshape(3, CONV_W)
    n_ada = w_ada.shape[2]
    b_cols = lax.dynamic_slice(b_ada, (0, me * n_ada), (1, n_ada))
    mod_part = ada_forward(c_all, w_ada[0], b_cols, "ada_forward")
    mod_all = all_gather_rows(mod_part, "gather_mod")
    mod = lax.dynamic_index_in_dim(mod_all, me, axis=1, keepdims=False).reshape(9, 1, D_MODEL)
    sh1, sc1, gt1, sh2, sc2, gt2, sh3, sc3, gt3 = [mod[n] for n in range(9)]

    u1 = norm_modulate(x0, norm1_g, sh1, sc1, "norm_mod_1")
    (gu1, act1), got = ffn_up(u1, wgu1, "ffn_up_1", exchange=gather_stage1(
        [shard(w_mix_in), shard(w_conv_out), shard(w_attn_out), shard(w_out)]))
    (x1, y1), (wmix, wco, wao, wout) = residual_matmul(act1, wd1, x0, gt1, 0.5, "ffn_down_1",
                                                     exchange=gather_stage2(got))
    wco = wco.transpose(1, 0, 2).reshape(CONV_W, D_MODEL)
    wao = wao.transpose(1, 0, 2).reshape(ATTN_W, D_MODEL)
    wout = wout.reshape(D_MODEL, D_MODEL)

    u2 = norm_modulate(x1, norm2_g, sh2, sc2, "norm_mod_2")
    proj = mix_in_proj(u2, wmix, "mix_in")
    sa = short_conv(proj, cw_full, "short_conv")
    (o, tot), got = stick_breaking_fwd(proj, "attn_fwd",
                                       exchange=gather_stage1([shard(ffn2_w_gu), shard(ffn2_w_down)]))
    (merged, ya, yb), (wgu3, wd3) = merge_forward(sa, o, proj, wco, wao, bm_full, "merge",
                                                  exchange=gather_stage2(got))
    wd3 = wd3.reshape(4, FF_BLK, D_MODEL)
    x2, y2 = residual_matmul(merged[None], wout[None], x1, gt2, 1.0, "out_proj")

    u3 = norm_modulate(x2, norm3_g, sh3, sc3, "norm_mod_3")
    gu3, act3 = ffn_up(u3, wgu3, "ffn_up_3")
    x3, y3 = residual_matmul(act3, wd3, x2, gt3, 0.5, "ffn_down_3")

    dx3, dy3, dgt3, dfinal, sq = loss_head(x3, target, final_g2, gt3, y3, 0.5, "loss_head")
    dgu3 = ffn_dact(dy3, wd3, gu3, "ffn_dact_3").reshape(8, s, FF_BLK)
    g_wd3 = rows8(matmul_tn(act3, dy3[None], "grad_w_down_3"))
    du3 = matmul_nt_acc(dgu3, wgu3, "ffn_du_3")
    g_wgu3 = matmul_tn(u3[None], dgu3, "grad_w_gu_3").reshape(8, D_MODEL, FF_BLK)
    dx2, dsh3, dsc3, dn3, dy2, dgt2 = norm_modulate_bwd(du3, x2, dx3, norm3_g, sc3, "norm_bwd_3",
                                                        prev=(gt2, y2, 1.0))

    (dya, dyb, dga, dgb, dbm), pairs = merge_backward(dy2, wout, proj, ya, yb, bm_full, "merge_bwd",
                                                      exchange=scatter_stage1([g_wgu3, g_wd3]))
    sums3 = [pair_sum(g_wgu3, pairs[0], "pair_sum_w_gu_3"), pair_sum(g_wd3, pairs[1], "pair_sum_w_down_3")]
    g_wout = rows8(matmul_tn(merged[None], dy2[None], "grad_w_out"))
    dsa, do = out_proj_bwd(dya, dyb, wco, wao, "out_proj_bwd")
    g_wco = matmul_tn(sa[None], dya[None], "grad_w_conv_out").reshape(CONV_W, N_DEV, 128).transpose(1, 0, 2)
    g_wao = matmul_tn(o[None], dyb[None], "grad_w_attn_out").reshape(ATTN_W, N_DEV, 128).transpose(1, 0, 2)
    dcb, dcc, dcx, dconv = short_conv_bwd(dsa, proj, cw_full, "short_conv_bwd")
    (dq, dk, dv), landed3 = stick_breaking_bwd(proj, do, tot, "attn_bwd", exchange=scatter_stage2(sums3))
    dproj = jnp.concatenate([dcb, dcc, dcx, dq, dk.astype(BF16), dv.astype(BF16), dga, dgb], axis=1)
    dproj8 = dproj.reshape(s, N_DEV, MIX_BLK).transpose(1, 0, 2)
    du2 = matmul_nt_acc(dproj8, wmix, "mix_in_du")
    g_wmix = matmul_tn(u2[None], dproj8, "grad_w_mix_in").reshape(N_DEV, D_MODEL, MIX_BLK)
    mixer_grads = [g_wmix, g_wco, g_wao, g_wout]
    (dx1, dsh2, dsc2, dn2, dy1, dgt1), pairs = norm_modulate_bwd(
        du2, x1, dx2, norm2_g, sc2, "norm_bwd_2", prev=(gt1, y1, 0.5), exchange=scatter_stage1(mixer_grads))
    sums_mix = [pair_sum(g, p, f"pair_sum_mixer_{n}") for n, (g, p) in enumerate(zip(mixer_grads, pairs))]

    dgu1, landed_mix = ffn_dact(dy1, wd1, gu1, "ffn_dact_1", exchange=scatter_stage2(sums_mix))
    dgu1 = dgu1.reshape(8, s, FF_BLK)
    g_wgu1 = matmul_tn(u1[None], dgu1, "grad_w_gu_1").reshape(8, D_MODEL, FF_BLK)
    g_wd1, pairs = matmul_tn(act1, dy1[None], "grad_w_down_1", exchange=scatter_stage1([g_wgu1]))
    g_wd1 = rows8(g_wd1)
    sum_gu1 = pair_sum(g_wgu1, pairs[0], "pair_sum_w_gu_1")
    du1, (landed_gu1, pair_d1) = matmul_nt_acc(
        dgu1, wgu1, "ffn_du_1", exchange=merge_exchanges(scatter_stage2([sum_gu1]), scatter_stage1([g_wd1])))
    sum_d1 = pair_sum(g_wd1, pair_d1, "pair_sum_w_down_1")
    (grad_x, dsh1, dsc1, dn1), landed_d1 = norm_modulate_bwd(du1, x0, dx1, norm1_g, sc1, "norm_bwd_1",
                                                            exchange=scatter_stage2([sum_d1]))

    loss_local = (0.5 / D_MODEL) * jnp.sum(sq)
    stats = jnp.concatenate(
        [v.reshape(-1) for v in (dsh1, dsc1, dgt1, dsh2, dsc2, dgt2, dsh3, dsc3, dgt3,
                                 dn1, dn2, dn3, dfinal, dbm, dconv)]
        + [jnp.broadcast_to(loss_local, (128,))]).reshape(1, -1)
    stats_all = all_gather_rows(stats, "gather_stats")
    n_mod = 9 * D_MODEL
    loss = jnp.sum(stats_all[:, 0, -1])
    dmod_all = stats_all[:, :, :n_mod]
    off = n_mod
    parts = {}
    for key in ("norm1_g", "norm2_g", "norm3_g", "final_g"):
        parts[key] = stats_all[:, :, off:off + D_MODEL]
        off += D_MODEL
    dbm_all = stats_all[:, 0, off:off + 2 * D_MODEL].reshape(N_DEV, 2, D_MODEL)
    off += 2 * D_MODEL
    dcw_all = stats_all[:, 0, off:off + 3 * CONV_W].reshape(N_DEV, 3, CONV_W)
    parts["b_merge"] = lax.dynamic_slice(dbm_all, (0, 0, me * 128), (N_DEV, 2, 128))
    parts["conv_w"] = lax.dynamic_slice(dcw_all, (0, 0, me * 64), (N_DEV, 3, 64))
    dmod_cols = lax.dynamic_slice(dmod_all[:, 0, :], (0, me * n_ada), (N_DEV, n_ada))
    parts["w_ada"] = ada_backward(c_all, dmod_cols, "ada_backward")[None]
    parts["b_ada"] = dmod_all
    parts["ffn2_w_gu"], parts["ffn2_w_down"] = landed3
    parts["w_mix_in"], parts["w_conv_out"], parts["w_attn_out"], parts["w_out"] = landed_mix
    parts["ffn1_w_gu"] = landed_gu1
    parts["ffn1_w_down"] = landed_d1[0]

    given = dict(w_ada=w_ada, b_ada=b_ada, norm1_g=norm1_g, ffn1_w_gu=ffn1_w_gu, ffn1_w_down=ffn1_w_down,
                 norm2_g=norm2_g, w_mix_in=w_mix_in, b_merge=b_merge, conv_w=conv_w, w_conv_out=w_conv_out,
                 w_attn_out=w_attn_out, w_out=w_out, norm3_g=norm3_g, ffn2_w_gu=ffn2_w_gu,
                 ffn2_w_down=ffn2_w_down, final_g=final_g)
    moments_m = dict(w_ada=m_w_ada, b_ada=m_b_ada, norm1_g=m_norm1_g, ffn1_w_gu=m_ffn1_w_gu,
                     ffn1_w_down=m_ffn1_w_down, norm2_g=m_norm2_g, w_mix_in=m_w_mix_in, b_merge=m_b_merge,
                     conv_w=m_conv_w, w_conv_out=m_w_conv_out, w_attn_out=m_w_attn_out, w_out=m_w_out,
                     norm3_g=m_norm3_g, ffn2_w_gu=m_ffn2_w_gu, ffn2_w_down=m_ffn2_w_down, final_g=m_final_g)
    moments_v = dict(w_ada=v_w_ada, b_ada=v_b_ada, norm1_g=v_norm1_g, ffn1_w_gu=v_ffn1_w_gu,
                     ffn1_w_down=v_ffn1_w_down, norm2_g=v_norm2_g, w_mix_in=v_w_mix_in, b_merge=v_b_merge,
                     conv_w=v_conv_w, w_conv_out=v_w_conv_out, w_attn_out=v_w_attn_out, w_out=v_w_out,
                     norm3_g=v_norm3_g, ffn2_w_gu=v_ffn2_w_gu, ffn2_w_down=v_ffn2_w_down, final_g=v_final_g)
    order = ["w_ada", "b_ada", "norm1_g", "ffn1_w_gu", "ffn1_w_down", "norm2_g", "w_mix_in", "b_merge",
             "conv_w", "w_conv_out", "w_attn_out", "w_out", "norm3_g", "ffn2_w_gu", "ffn2_w_down", "final_g"]
    grads, deltas, new_m, new_v = [], [], [], []
    for key in order:
        shape = given[key].shape
        shape2 = (1, shape[0]) if len(shape) == 1 else shape[-2:]
        outs = adamw(given[key].reshape(shape2), moments_m[key].reshape(shape2),
                     moments_v[key].reshape(shape2), parts[key], f"adamw_{key}")
        for dst, val in zip((grads, deltas, new_m, new_v), outs):
            dst.append(val.reshape(shape))

    return (loss, grad_x[None], *grads, *deltas, *new_m, *new_v)
```

```python
import functools
from typing import Callable, NamedTuple

import jax
import jax.numpy as jnp
from jax import lax
from jax.experimental import pallas as pl
from jax.experimental.pallas import tpu as pltpu

F32 = jnp.float32
BF16 = jnp.bfloat16
MESH = pl.DeviceIdType.MESH
ANY = pl.BlockSpec(memory_space=pl.ANY)

N_DEV = 8
D_MODEL = 1024
D_FF = 2816
FF_BLK = D_FF // 4
N_HEADS = 8
HEAD_DIM = 64
CONV_W = 512
ATTN_W = 512
MIX_W = 3 * CONV_W + 3 * ATTN_W + 2 * D_MODEL
MIX_BLK = MIX_W // N_DEV
EPS = 1e-6
ATTN_SCALE = HEAD_DIM ** -0.5

ADAM_LR = 0.001
ADAM_B1 = 0.9
ADAM_B2 = 0.999
ADAM_EPS = 1e-08
ADAM_WD = 0.01
ADAM_STEP = 10
ADAM_BC1 = 1.0 - ADAM_B1 ** ADAM_STEP
ADAM_BC2 = 1.0 - ADAM_B2 ** ADAM_STEP

VMEM_LIMIT = 56 * 1024 * 1024
ROW_TILE = 512
ACC_TILE = 1024
ELT_TILE = 256
ATT_BLK = 256

NN = (((1,), (0,)), ((), ()))
NT = (((1,), (1,)), ((), ()))
TN = (((0,), (0,)), ((), ()))


def _dot(a, b, dims=NN):
    return lax.dot_general(a, b, dims, preferred_element_type=F32)


def _params(*sem):
    return pltpu.CompilerParams(dimension_semantics=sem, vmem_limit_bytes=VMEM_LIMIT)


def _sigmoid(x):
    return 1.0 / (1.0 + jnp.exp(-x))


def _me():
    x, y, c = lax.axis_index("x"), lax.axis_index("y"), lax.axis_index("c")
    return x, y, c, 4 * x + 2 * y + c


def _peer(k):
    x, y, c, _ = _me()
    px = 1 - x if (k >> 2) & 1 else x
    py = 1 - y if (k >> 1) & 1 else y
    pc = 1 - c if k & 1 else c
    return (px, py, pc), 4 * px + 2 * py + pc


class Exchange(NamedTuple):
    operands: tuple
    out_shapes: tuple
    aliases: dict
    n_remote: int
    n_local: int
    copies: Callable


CHIP_FLIPS = (2, 4, 6)
SIBLING = 1


def _remote(src, dst, send_sems, recv_sems, n, peer):
    return pltpu.make_async_remote_copy(src_ref=src, dst_ref=dst, send_sem=send_sems.at[n], recv_sem=recv_sems.at[n],
                                        device_id=peer, device_id_type=MESH)


def gather_stage1(shards):
    n = len(shards)
    rels = (SIBLING,) + CHIP_FLIPS

    def copies(ins, outs, send_sems, recv_sems, local_sems, rb, lb):
        _, _, _, me = _me()
        cps = []
        for w in range(n):
            cps.append(pltpu.make_async_copy(ins[w], outs[w].at[me], local_sems.at[lb + w]))
            for a, k in enumerate(rels):
                peer, _ = _peer(k)
                cps.append(_remote(ins[w], outs[w].at[me], send_sems, recv_sems, rb + len(rels) * w + a, peer))
        return cps

    shapes = tuple(jax.ShapeDtypeStruct((N_DEV,) + s.shape, s.dtype) for s in shards)
    return Exchange(tuple(shards), shapes, {}, len(rels) * n, n, copies)


def gather_stage2(fulls):
    n = len(fulls)

    def copies(ins, outs, send_sems, recv_sems, local_sems, rb, lb):
        sibling, _ = _peer(SIBLING)
        cps = []
        for w in range(n):
            for a, k in enumerate(CHIP_FLIPS):
                _, blk = _peer(k)
                cps.append(_remote(outs[w].at[blk], outs[w].at[blk], send_sems, recv_sems, rb + 3 * w + a, sibling))
        return cps

    shapes = tuple(jax.ShapeDtypeStruct(f.shape, f.dtype) for f in fulls)
    return Exchange(tuple(fulls), shapes, {w: w for w in range(n)}, 3 * n, 0, copies)


def scatter_stage1(fulls):
    n = len(fulls)

    def copies(ins, outs, send_sems, recv_sems, local_sems, rb, lb):
        _, _, c, _ = _me()
        sibling, _ = _peer(SIBLING)
        cps = []
        for w in range(n):
            for q in range(4):
                cps.append(_remote(ins[w].at[2 * q + (1 - c)], outs[w].at[q], send_sems, recv_sems, rb + 4 * w + q, sibling))
        return cps

    shapes = tuple(jax.ShapeDtypeStruct((4,) + f.shape[1:], f.dtype) for f in fulls)
    return Exchange(tuple(fulls), shapes, {}, 4 * n, 0, copies)


def scatter_stage2(sums):
    n = len(sums)

    def copies(ins, outs, send_sems, recv_sems, local_sems, rb, lb):
        x, y, _, _ = _me()
        mine = 2 * x + y
        cps = []
        for w in range(n):
            cps.append(pltpu.make_async_copy(ins[w].at[mine], outs[w].at[mine], local_sems.at[lb + w]))
            for a, k in enumerate(CHIP_FLIPS):
                peer, _ = _peer(k)
                cps.append(_remote(ins[w].at[2 * peer[0] + peer[1]], outs[w].at[mine], send_sems, recv_sems,
                                   rb + 3 * w + a, peer))
        return cps

    shapes = tuple(jax.ShapeDtypeStruct(s.shape, s.dtype) for s in sums)
    return Exchange(tuple(sums), shapes, {}, 3 * n, n, copies)


def merge_exchanges(a, b):
    na_in, na_out = len(a.operands), len(a.out_shapes)

    def copies(ins, outs, send_sems, recv_sems, local_sems, rb, lb):
        return (a.copies(ins[:na_in], outs[:na_out], send_sems, recv_sems, local_sems, rb, lb)
                + b.copies(ins[na_in:], outs[na_out:], send_sems, recv_sems, local_sems, rb + a.n_remote, lb + a.n_local))

    aliases = dict(a.aliases)
    aliases.update({na_in + i: na_out + o for i, o in b.aliases.items()})
    return Exchange(a.operands + b.operands, a.out_shapes + b.out_shapes, aliases,
                    a.n_remote + b.n_remote, a.n_local + b.n_local, copies)


def _exchange_scratch(ex):
    return [pltpu.SemaphoreType.DMA((ex.n_remote,)), pltpu.SemaphoreType.DMA((ex.n_remote,)),
            pltpu.SemaphoreType.DMA((max(ex.n_local, 1),))]


def run_exchange(ex, name):
    n_in, n_out = len(ex.operands), len(ex.out_shapes)

    def body(*refs):
        cps = ex.copies(refs[:n_in], refs[n_in:n_in + n_out], *refs[n_in + n_out:], 0, 0)
        for cp in cps:
            cp.start()
        for cp in cps:
            cp.wait()

    return pl.pallas_call(
        body, name=name, out_shape=list(ex.out_shapes), in_specs=[ANY] * n_in, out_specs=[ANY] * n_out,
        scratch_shapes=_exchange_scratch(ex), input_output_aliases=dict(ex.aliases),
    )(*ex.operands)


def _call(body, *, name, grid, in_specs, out_specs, out_shape, operands, scratch_shapes=(), semantics=(),
          exchange=None):
    if exchange is None:
        return pl.pallas_call(
            body, name=name, grid=grid, in_specs=in_specs, out_specs=out_specs, out_shape=out_shape,
            scratch_shapes=list(scratch_shapes), compiler_params=_params(*semantics))(*operands)
    single = not isinstance(out_shape, (list, tuple))
    out_shapes = [out_shape] if single else list(out_shape)
    out_specs_l = [out_specs] if single else list(out_specs)
    n_in, n_out, n_scr = len(operands), len(out_shapes), len(scratch_shapes)
    x_in, x_out = len(exchange.operands), len(exchange.out_shapes)

    def hosted(*refs):
        ins, refs = refs[:n_in], refs[n_in:]
        xin, refs = refs[:x_in], refs[x_in:]
        outs, refs = refs[:n_out], refs[n_out:]
        xout, refs = refs[:x_out], refs[x_out:]
        scr, sems = refs[:n_scr], refs[n_scr:]
        first = functools.reduce(jnp.logical_and, [pl.program_id(a) == 0 for a in range(len(grid))])
        last = functools.reduce(jnp.logical_and, [pl.program_id(a) == g - 1 for a, g in enumerate(grid)])

        @pl.when(first)
        def _():
            for cp in exchange.copies(xin, xout, *sems, 0, 0):
                cp.start()

        body(*ins, *outs, *scr)

        @pl.when(last)
        def _():
            for cp in exchange.copies(xin, xout, *sems, 0, 0):
                cp.wait()

    res = pl.pallas_call(
        hosted, name=name, grid=grid,
        in_specs=list(in_specs) + [ANY] * x_in, out_specs=out_specs_l + [ANY] * x_out,
        out_shape=out_shapes + list(exchange.out_shapes),
        scratch_shapes=list(scratch_shapes) + _exchange_scratch(exchange),
        input_output_aliases={n_in + i: n_out + o for i, o in exchange.aliases.items()},
        compiler_params=_params(*(["arbitrary"] * len(grid))),
    )(*operands, *exchange.operands)
    outs, xouts = res[:n_out], res[n_out:]
    return (outs[0] if single else outs), xouts


def all_gather_rows(v, name):
    r, n = v.shape

    def body(v_ref, out_ref, send_sems, recv_sems):
        _, _, _, me = _me()
        out_ref[me] = v_ref[...]
        copies = []
        for k in range(1, N_DEV):
            peer, _ = _peer(k)
            copies.append(_remote(v_ref, out_ref.at[me], send_sems, recv_sems, k - 1, peer))
        for cp in copies:
            cp.start()
        for cp in copies:
            cp.wait()

    return pl.pallas_call(
        body, name=name,
        out_shape=jax.ShapeDtypeStruct((N_DEV, r, n), v.dtype),
        in_specs=[pl.BlockSpec(memory_space=pltpu.VMEM)],
        out_specs=pl.BlockSpec(memory_space=pltpu.VMEM),
        scratch_shapes=[pltpu.SemaphoreType.DMA((N_DEV - 1,)), pltpu.SemaphoreType.DMA((N_DEV - 1,))],
    )(v)


def pair_sum(full, pair, name):
    _, r, c = full.shape
    t = 256 if r % 256 == 0 and r > 256 else r
    core = lax.axis_index("c").astype(jnp.int32).reshape(1)

    def body(core_ref, f_ref, p_ref, o_ref):
        o_ref[...] = (f_ref[...].astype(F32) + p_ref[...].astype(F32)).astype(BF16)

    return pl.pallas_call(
        body, name=name,
        grid_spec=pltpu.PrefetchScalarGridSpec(
            num_scalar_prefetch=1, grid=(4, r // t),
            in_specs=[pl.BlockSpec((None, None, t, c), lambda q, i, core_ref: (q, core_ref[0], i, 0)),
                      pl.BlockSpec((None, t, c), lambda q, i, core_ref: (q, i, 0))],
            out_specs=pl.BlockSpec((None, t, c), lambda q, i, core_ref: (q, i, 0))),
        out_shape=jax.ShapeDtypeStruct((4, r, c), BF16),
        compiler_params=_params("parallel", "parallel"),
    )(core, full.reshape(4, 2, r, c), pair)


def ada_forward(c_all, w_ada, b_cols, name):
    n = w_ada.shape[1]

    def body(c_ref, w_ref, b_ref, o_ref):
        c = c_ref[...]
        act = c * _sigmoid(c)
        o_ref[...] = jnp.dot(act, w_ref[...], precision=lax.Precision.HIGHEST,
                             preferred_element_type=F32) + b_ref[...]

    return pl.pallas_call(
        body, name=name, out_shape=jax.ShapeDtypeStruct((N_DEV, n), F32),
        compiler_params=pltpu.CompilerParams(vmem_limit_bytes=VMEM_LIMIT),
    )(c_all, w_ada, b_cols)


def ada_backward(c_all, dmod_cols, name):
    n = dmod_cols.shape[1]

    def body(c_ref, d_ref, o_ref):
        c = c_ref[...]
        act = c * _sigmoid(c)
        o_ref[...] = lax.dot_general(act, d_ref[...], TN, precision=lax.Precision.HIGHEST,
                                     preferred_element_type=F32)

    return pl.pallas_call(
        body, name=name, out_shape=jax.ShapeDtypeStruct((D_MODEL, n), F32),
        compiler_params=pltpu.CompilerParams(vmem_limit_bytes=VMEM_LIMIT),
    )(c_all, dmod_cols)


def _row_spec(t, width=D_MODEL):
    return pl.BlockSpec((t, width), lambda i: (i, 0))


def _vec_spec(rows=1, width=D_MODEL):
    return pl.BlockSpec((rows, width), lambda i: (0, 0))


def norm_modulate(x, g, shift, scale, name):
    s = x.shape[0]
    t = min(ELT_TILE, s)

    def body(x_ref, g_ref, sh_ref, sc_ref, u_ref):
        xv = x_ref[...]
        r = lax.rsqrt(jnp.mean(xv * xv, axis=-1, keepdims=True) + EPS)
        a = (xv * r) * g_ref[...]
        u_ref[...] = (a * (1.0 + sc_ref[...]) + sh_ref[...]).astype(BF16)

    return pl.pallas_call(
        body, name=name, grid=(s // t,),
        in_specs=[_row_spec(t), _vec_spec(), _vec_spec(), _vec_spec()],
        out_specs=_row_spec(t),
        out_shape=jax.ShapeDtypeStruct((s, D_MODEL), BF16),
        compiler_params=_params("parallel"),
    )(x, g, shift, scale)


def loss_head(x, target, final_g, gate, y_prev, coef, name):
    s = x.shape[0]
    t = min(ELT_TILE, s)

    def body(x_ref, t_ref, fg_ref, gt_ref, y_ref, dx_ref, dy_ref, dgt_ref, dfg_ref, sq_ref):
        @pl.when(pl.program_id(0) == 0)
        def _():
            dgt_ref[...] = jnp.zeros_like(dgt_ref)
            dfg_ref[...] = jnp.zeros_like(dfg_ref)
            sq_ref[...] = jnp.zeros_like(sq_ref)

        xv = x_ref[...]
        fg = fg_ref[...]
        r = lax.rsqrt(jnp.mean(xv * xv, axis=-1, keepdims=True) + EPS)
        nrm = xv * r
        err = nrm * fg - t_ref[...]
        sq_ref[...] += jnp.sum(err * err, axis=0, keepdims=True)
        dout = err * (1.0 / D_MODEL)
        dfg_ref[...] += jnp.sum(dout * nrm, axis=0, keepdims=True)
        dn = dout * fg
        dx = r * (dn - nrm * jnp.mean(dn * nrm, axis=-1, keepdims=True))
        dx_ref[...] = dx
        dy_ref[...] = (coef * gt_ref[...] * dx).astype(BF16)
        dgt_ref[...] += coef * jnp.sum(dx * y_ref[...].astype(F32), axis=0, keepdims=True)

    vec = jax.ShapeDtypeStruct((1, D_MODEL), F32)
    return pl.pallas_call(
        body, name=name, grid=(s // t,),
        in_specs=[_row_spec(t), _row_spec(t), _vec_spec(), _vec_spec(), _row_spec(t)],
        out_specs=[_row_spec(t), _row_spec(t), _vec_spec(), _vec_spec(), _vec_spec()],
        out_shape=[jax.ShapeDtypeStruct((s, D_MODEL), F32), jax.ShapeDtypeStruct((s, D_MODEL), BF16),
                   vec, vec, vec],
        compiler_params=_params("arbitrary"),
    )(x, target, final_g, gate, y_prev)


def norm_modulate_bwd(du, x, dx_out, g, scale, name, prev=None, exchange=None):
    s = x.shape[0]
    t = min(ELT_TILE, s)
    has_prev = prev is not None

    def body(*refs):
        du_ref, x_ref, dxo_ref, g_ref, sc_ref = refs[:5]
        refs = refs[5:]
        if has_prev:
            gt_ref, y_ref = refs[:2]
            refs = refs[2:]
        dx_ref, dsh_ref, dsc_ref, dg_ref = refs[:4]

        @pl.when(pl.program_id(0) == 0)
        def _():
            dsh_ref[...] = jnp.zeros_like(dsh_ref)
            dsc_ref[...] = jnp.zeros_like(dsc_ref)
            dg_ref[...] = jnp.zeros_like(dg_ref)
            if has_prev:
                refs[5][...] = jnp.zeros_like(refs[5])

        xv = x_ref[...]
        duv = du_ref[...]
        gv = g_ref[...]
        r = lax.rsqrt(jnp.mean(xv * xv, axis=-1, keepdims=True) + EPS)
        nrm = xv * r
        a = nrm * gv
        dsh_ref[...] += jnp.sum(duv, axis=0, keepdims=True)
        dsc_ref[...] += jnp.sum(duv * a, axis=0, keepdims=True)
        da = duv * (1.0 + sc_ref[...])
        dg_ref[...] += jnp.sum(da * nrm, axis=0, keepdims=True)
        dn = da * gv
        dx = dxo_ref[...] + r * (dn - nrm * jnp.mean(dn * nrm, axis=-1, keepdims=True))
        dx_ref[...] = dx
        if has_prev:
            coef = prev[2]
            refs[4][...] = (coef * gt_ref[...] * dx).astype(BF16)
            refs[5][...] += coef * jnp.sum(dx * y_ref[...].astype(F32), axis=0, keepdims=True)

    vec = jax.ShapeDtypeStruct((1, D_MODEL), F32)
    operands = [du, x, dx_out, g, scale]
    in_specs = [_row_spec(t), _row_spec(t), _row_spec(t), _vec_spec(), _vec_spec()]
    out_specs = [_row_spec(t), _vec_spec(), _vec_spec(), _vec_spec()]
    out_shape = [jax.ShapeDtypeStruct((s, D_MODEL), F32), vec, vec, vec]
    if has_prev:
        operands += [prev[0], prev[1]]
        in_specs += [_vec_spec(), _row_spec(t)]
        out_specs += [_row_spec(t), _vec_spec()]
        out_shape += [jax.ShapeDtypeStruct((s, D_MODEL), BF16), vec]
    return _call(body, name=name, grid=(s // t,), in_specs=in_specs, out_specs=out_specs, out_shape=out_shape,
                 operands=operands, semantics=("arbitrary",), exchange=exchange)


def ffn_up(u, w_gu, name, exchange=None):
    s = u.shape[0]
    t = min(ROW_TILE, s)

    def body(u_ref, wg_ref, wu_ref, gu_ref, act_ref):
        uv = u_ref[...]
        g = _dot(uv, wg_ref[...])
        up = _dot(uv, wu_ref[...])
        gu_ref[0] = g.astype(BF16)
        gu_ref[1] = up.astype(BF16)
        act_ref[...] = (g * _sigmoid(g) * up).astype(BF16)

    return _call(
        body, name=name, grid=(4, s // t),
        in_specs=[pl.BlockSpec((t, D_MODEL), lambda j, i: (i, 0)),
                  pl.BlockSpec((None, D_MODEL, FF_BLK), lambda j, i: (j, 0, 0)),
                  pl.BlockSpec((None, D_MODEL, FF_BLK), lambda j, i: (j + 4, 0, 0))],
        out_specs=[pl.BlockSpec((2, None, t, FF_BLK), lambda j, i: (0, j, i, 0)),
                   pl.BlockSpec((None, t, FF_BLK), lambda j, i: (j, i, 0))],
        out_shape=[jax.ShapeDtypeStruct((2, 4, s, FF_BLK), BF16),
                   jax.ShapeDtypeStruct((4, s, FF_BLK), BF16)],
        operands=(u, w_gu, w_gu), semantics=("parallel", "parallel"), exchange=exchange)


def residual_matmul(a, b, x, gate, coef, name, exchange=None):
    nk, s, kb = a.shape
    t = min(ROW_TILE, s)

    def body(a_ref, b_ref, x_ref, gt_ref, xo_ref, y_ref):
        y = _dot(a_ref[0], b_ref[0])
        for k in range(1, nk):
            y = y + _dot(a_ref[k], b_ref[k])
        y_ref[...] = y.astype(BF16)
        xo_ref[...] = x_ref[...] + coef * gt_ref[...] * y

    return _call(
        body, name=name, grid=(s // t,),
        in_specs=[pl.BlockSpec((nk, t, kb), lambda i: (0, i, 0)),
                  pl.BlockSpec((nk, kb, D_MODEL), lambda i: (0, 0, 0)),
                  _row_spec(t), _vec_spec()],
        out_specs=[_row_spec(t), _row_spec(t)],
        out_shape=[jax.ShapeDtypeStruct((s, D_MODEL), F32), jax.ShapeDtypeStruct((s, D_MODEL), BF16)],
        operands=(a, b, x, gate), semantics=("parallel",), exchange=exchange)


def ffn_dact(dy, w_down, gu, name, exchange=None):
    s = dy.shape[0]
    t = min(ROW_TILE, s)

    def body(dy_ref, w_ref, gu_ref, dgu_ref):
        dact = _dot(dy_ref[...], w_ref[...], NT)
        g = gu_ref[0].astype(F32)
        up = gu_ref[1].astype(F32)
        sg = _sigmoid(g)
        dgu_ref[0] = (dact * up * sg * (1.0 + g * (1.0 - sg))).astype(BF16)
        dgu_ref[1] = (dact * g * sg).astype(BF16)

    return _call(
        body, name=name, grid=(4, s // t),
        in_specs=[pl.BlockSpec((t, D_MODEL), lambda j, i: (i, 0)),
                  pl.BlockSpec((None, FF_BLK, D_MODEL), lambda j, i: (j, 0, 0)),
                  pl.BlockSpec((2, None, t, FF_BLK), lambda j, i: (0, j, i, 0))],
        out_specs=pl.BlockSpec((2, None, t, FF_BLK), lambda j, i: (0, j, i, 0)),
        out_shape=jax.ShapeDtypeStruct((2, 4, s, FF_BLK), BF16),
        operands=(dy, w_down, gu), semantics=("parallel", "parallel"), exchange=exchange)


def matmul_nt_acc(a, b, name, exchange=None):
    nk, s, n = a.shape
    d = b.shape[1]
    t = min(ROW_TILE, s)

    def body(a_ref, b_ref, o_ref):
        acc = _dot(a_ref[0], b_ref[0], NT)
        for k in range(1, nk):
            acc = acc + _dot(a_ref[k], b_ref[k], NT)
        o_ref[...] = acc

    return _call(
        body, name=name, grid=(s // t,),
        in_specs=[pl.BlockSpec((nk, t, n), lambda i: (0, i, 0)),
                  pl.BlockSpec((nk, d, n), lambda i: (0, 0, 0))],
        out_specs=pl.BlockSpec((t, d), lambda i: (i, 0)),
        out_shape=jax.ShapeDtypeStruct((s, d), F32),
        operands=(a, b), semantics=("parallel",), exchange=exchange)


def matmul_tn(a, b, name, group=(1, 1), exchange=None):
    ja, s, m = a.shape
    jb, _, n = b.shape
    ga, gb = group
    t = min(ACC_TILE, s)
    nk = s // t

    def body(a_ref, b_ref, o_ref, acc_ref):
        k = pl.program_id(2)

        @pl.when(k == 0)
        def _():
            acc_ref[...] = jnp.zeros_like(acc_ref)

        for p in range(ga):
            for q in range(gb):
                acc_ref[p, q] += _dot(a_ref[p], b_ref[q], TN)

        @pl.when(k == nk - 1)
        def _():
            o_ref[...] = acc_ref[...].astype(BF16)

    return _call(
        body, name=name, grid=(ja // ga, jb // gb, nk),
        in_specs=[pl.BlockSpec((ga, t, m), lambda p, q, k: (p, k, 0)),
                  pl.BlockSpec((gb, t, n), lambda p, q, k: (q, k, 0))],
        out_specs=pl.BlockSpec((ga, gb, m, n), lambda p, q, k: (p, q, 0, 0)),
        out_shape=jax.ShapeDtypeStruct((ja, jb, m, n), BF16),
        operands=(a, b), scratch_shapes=[pltpu.VMEM((ga, gb, m, n), F32)],
        semantics=("parallel", "parallel", "arbitrary"), exchange=exchange)


def mix_in_proj(u, w_mix, name):
    s = u.shape[0]
    t = min(ROW_TILE, s)

    def body(u_ref, w_ref, o_ref):
        uv = u_ref[...]
        for j in range(N_DEV):
            o_ref[:, j * MIX_BLK:(j + 1) * MIX_BLK] = _dot(uv, w_ref[j]).astype(BF16)

    return pl.pallas_call(
        body, name=name, grid=(s // t,),
        in_specs=[_row_spec(t), pl.BlockSpec((N_DEV, D_MODEL, MIX_BLK), lambda i: (0, 0, 0))],
        out_specs=_row_spec(t, MIX_W),
        out_shape=jax.ShapeDtypeStruct((s, MIX_W), BF16),
        compiler_params=_params("parallel"),
    )(u, w_mix)


def _conv_taps(cc_ref, cx_ref, s):
    v = cc_ref[...].astype(F32) * cx_ref[...].astype(F32)
    tok = lax.broadcasted_iota(jnp.int32, v.shape, 0)
    v1 = jnp.where(tok >= 1, pltpu.roll(v, 1, 0), 0.0)
    v2 = jnp.where(tok >= 2, pltpu.roll(v, 2, 0), 0.0)
    return v, v1, v2, tok


def _proj_cols(s, first):
    return pl.BlockSpec((s, 128), lambda j: (0, first + j))


def short_conv(proj, conv_w, name):
    s = proj.shape[0]

    def body(cb_ref, cc_ref, cx_ref, w_ref, o_ref):
        v, v1, v2, _ = _conv_taps(cc_ref, cx_ref, s)
        y = w_ref[0:1, :] * v2 + w_ref[1:2, :] * v1 + w_ref[2:3, :] * v
        o_ref[...] = (cb_ref[...].astype(F32) * y).astype(BF16)

    return pl.pallas_call(
        body, name=name, grid=(CONV_W // 128,),
        in_specs=[_proj_cols(s, 0), _proj_cols(s, 4), _proj_cols(s, 8),
                  pl.BlockSpec((3, 128), lambda j: (0, j))],
        out_specs=pl.BlockSpec((s, 128), lambda j: (0, j)),
        out_shape=jax.ShapeDtypeStruct((s, CONV_W), BF16),
        compiler_params=_params("parallel"),
    )(proj, proj, proj, conv_w)


def short_conv_bwd(dsa, proj, conv_w, name):
    s = proj.shape[0]

    def body(dsa_ref, cb_ref, cc_ref, cx_ref, w_ref, dcb_ref, dcc_ref, dcx_ref, dw_ref):
        v, v1, v2, tok = _conv_taps(cc_ref, cx_ref, s)
        w0, w1, w2 = w_ref[0:1, :], w_ref[1:2, :], w_ref[2:3, :]
        y = w0 * v2 + w1 * v1 + w2 * v
        dsa_v = dsa_ref[...].astype(F32)
        dcb_ref[...] = (dsa_v * y).astype(BF16)
        dy = dsa_v * cb_ref[...].astype(F32)
        dw_ref[0:1, :] = jnp.sum(dy * v2, axis=0, keepdims=True)
        dw_ref[1:2, :] = jnp.sum(dy * v1, axis=0, keepdims=True)
        dw_ref[2:3, :] = jnp.sum(dy * v, axis=0, keepdims=True)
        dy1 = jnp.where(tok < s - 1, pltpu.roll(dy, s - 1, 0), 0.0)
        dy2 = jnp.where(tok < s - 2, pltpu.roll(dy, s - 2, 0), 0.0)
        dv = w2 * dy + w1 * dy1 + w0 * dy2
        dcc_ref[...] = (dv * cx_ref[...].astype(F32)).astype(BF16)
        dcx_ref[...] = (dv * cc_ref[...].astype(F32)).astype(BF16)

    col = pl.BlockSpec((s, 128), lambda j: (0, j))
    act = jax.ShapeDtypeStruct((s, CONV_W), BF16)
    return pl.pallas_call(
        body, name=name, grid=(CONV_W // 128,),
        in_specs=[col, _proj_cols(s, 0), _proj_cols(s, 4), _proj_cols(s, 8),
                  pl.BlockSpec((3, 128), lambda j: (0, j))],
        out_specs=[col, col, col, pl.BlockSpec((3, 128), lambda j: (0, j))],
        out_shape=[act, act, act, jax.ShapeDtypeStruct((3, CONV_W), F32)],
        compiler_params=_params("parallel"),
    )(dsa, proj, proj, proj, conv_w)


def _gate_specs(t):
    return [pl.BlockSpec((t, D_MODEL), lambda i: (i, 3)), pl.BlockSpec((t, D_MODEL), lambda i: (i, 4))]


def merge_forward(sa, o, proj, w_co, w_ao, b_merge, name, exchange=None):
    s = sa.shape[0]
    t = min(ROW_TILE, s)

    def body(sa_ref, o_ref, ga_ref, gb_ref, wco_ref, wao_ref, bm_ref, mg_ref, ya_ref, yb_ref):
        ya = _dot(sa_ref[...], wco_ref[...])
        yb = _dot(o_ref[...], wao_ref[...])
        sga = _sigmoid(ga_ref[...].astype(F32) + bm_ref[0:1, :])
        sgb = _sigmoid(gb_ref[...].astype(F32) + bm_ref[1:2, :])
        mg_ref[...] = (sga * ya + sgb * yb).astype(BF16)
        ya_ref[...] = ya.astype(BF16)
        yb_ref[...] = yb.astype(BF16)

    act = jax.ShapeDtypeStruct((s, D_MODEL), BF16)
    return _call(
        body, name=name, grid=(s // t,),
        in_specs=[_row_spec(t, CONV_W), _row_spec(t, ATTN_W)] + _gate_specs(t)
        + [_vec_spec(CONV_W), _vec_spec(ATTN_W), _vec_spec(2)],
        out_specs=[_row_spec(t)] * 3, out_shape=[act, act, act],
        operands=(sa, o, proj, proj, w_co, w_ao, b_merge), semantics=("parallel",), exchange=exchange)


def merge_backward(dy, w_out, proj, ya, yb, b_merge, name, exchange=None):
    s = dy.shape[0]
    t = min(ROW_TILE, s)

    def body(dy_ref, w_ref, ga_ref, gb_ref, ya_ref, yb_ref, bm_ref,
             dya_ref, dyb_ref, dga_ref, dgb_ref, dbm_ref):
        @pl.when(pl.program_id(0) == 0)
        def _():
            dbm_ref[...] = jnp.zeros_like(dbm_ref)

        dmg = _dot(dy_ref[...], w_ref[...], NT)
        sga = _sigmoid(ga_ref[...].astype(F32) + bm_ref[0:1, :])
        sgb = _sigmoid(gb_ref[...].astype(F32) + bm_ref[1:2, :])
        dya_ref[...] = (dmg * sga).astype(BF16)
        dyb_ref[...] = (dmg * sgb).astype(BF16)
        dga = dmg * ya_ref[...].astype(F32) * sga * (1.0 - sga)
        dgb = dmg * yb_ref[...].astype(F32) * sgb * (1.0 - sgb)
        dga_ref[...] = dga.astype(BF16)
        dgb_ref[...] = dgb.astype(BF16)
        dbm_ref[0:1, :] += jnp.sum(dga, axis=0, keepdims=True)
        dbm_ref[1:2, :] += jnp.sum(dgb, axis=0, keepdims=True)

    act = jax.ShapeDtypeStruct((s, D_MODEL), BF16)
    return _call(
        body, name=name, grid=(s // t,),
        in_specs=[_row_spec(t), _vec_spec(D_MODEL)] + _gate_specs(t)
        + [_row_spec(t), _row_spec(t), _vec_spec(2)],
        out_specs=[_row_spec(t)] * 4 + [_vec_spec(2)],
        out_shape=[act] * 4 + [jax.ShapeDtypeStruct((2, D_MODEL), F32)],
        operands=(dy, w_out, proj, proj, ya, yb, b_merge), semantics=("arbitrary",), exchange=exchange)


def out_proj_bwd(dya, dyb, w_co, w_ao, name):
    s = dya.shape[0]
    t = min(ROW_TILE, s)

    def body(dya_ref, dyb_ref, wco_ref, wao_ref, dsa_ref, do_ref):
        dsa_ref[...] = _dot(dya_ref[...], wco_ref[...], NT).astype(BF16)
        do_ref[...] = _dot(dyb_ref[...], wao_ref[...], NT).astype(BF16)

    return pl.pallas_call(
        body, name=name, grid=(s // t,),
        in_specs=[_row_spec(t), _row_spec(t), _vec_spec(CONV_W), _vec_spec(ATTN_W)],
        out_specs=[_row_spec(t, CONV_W), _row_spec(t, ATTN_W)],
        out_shape=[jax.ShapeDtypeStruct((s, CONV_W), BF16), jax.ShapeDtypeStruct((s, ATTN_W), BF16)],
        compiler_params=_params("parallel"),
    )(dya, dyb, w_co, w_ao)


ATT_HEADS = 4
ATT_LANES = ATT_HEADS * HEAD_DIM


def _softplus(z):
    return jnp.maximum(z, 0.0) + jnp.log(1.0 + jnp.exp(-jnp.abs(z)))


def _split_sum(x, tri2):
    hi = x.astype(BF16)
    lo = (x - hi.astype(F32)).astype(BF16)
    return _dot(jnp.concatenate([hi, lo], axis=1), tri2)


def _head_masks(rows):
    lane = lax.broadcasted_iota(jnp.int32, (rows, ATT_LANES), 1)
    return [(lane >= h * HEAD_DIM) & (lane < (h + 1) * HEAD_DIM) for h in range(ATT_HEADS)]


def _per_head(x, masks):
    return [jnp.where(m, x, jnp.zeros_like(x)) for m in masks]


def _att_specs(s, blk):
    first = {"q": 3 * CONV_W // ATT_LANES, "k": (3 * CONV_W + ATTN_W) // ATT_LANES,
             "v": (3 * CONV_W + 2 * ATTN_W) // ATT_LANES}
    return [pl.BlockSpec((blk, ATT_LANES), lambda h, i: (i, first["q"] + h)),
            pl.BlockSpec((s, ATT_LANES), lambda h, i: (0, first["k"] + h)),
            pl.BlockSpec((s, ATT_LANES), lambda h, i: (0, first["v"] + h))]


def stick_breaking_fwd(proj, name, exchange=None):
    s = proj.shape[0]
    blk = ATT_BLK
    nq = s // blk

    def body(q_ref, k_ref, v_ref, o_ref, tot_ref):
        i = pl.program_id(1)
        row = lax.broadcasted_iota(jnp.int32, (blk, blk), 0)
        col = lax.broadcasted_iota(jnp.int32, (blk, blk), 1)
        tri = (row >= col).astype(BF16)
        tri2 = jnp.concatenate([tri, tri], axis=0)
        causal = col < row
        masks = _head_masks(blk)
        qs = _per_head(q_ref[...] * ATTN_SCALE, masks)

        def step(j, carry, diagonal):
            laters, acc = carry
            rows = pl.ds(pl.multiple_of(j * blk, blk), blk)
            kb = k_ref[rows, :]
            probs, new_laters = [], []
            for h in range(ATT_HEADS):
                z = _dot(qs[h], kb, NT)
                sp = _softplus(z)
                if diagonal:
                    sp = jnp.where(causal, sp, 0.0)
                a = jnp.exp(z - (_split_sum(sp, tri2) + laters[h]))
                if diagonal:
                    a = jnp.where(causal, a, 0.0)
                probs.append(a.astype(BF16))
                new_laters.append(laters[h] + jnp.sum(sp, axis=1, keepdims=True))
            v_heads = jnp.concatenate(_per_head(v_ref[rows, :], masks), axis=0)
            acc = acc + _dot(jnp.concatenate(probs, axis=1), v_heads)
            return tuple(new_laters), acc

        carry = (tuple(jnp.zeros((blk, 1), F32) for _ in range(ATT_HEADS)), jnp.zeros((blk, ATT_LANES), F32))
        carry = step(i, carry, True)
        laters, acc = lax.fori_loop(0, i, lambda n, c: step(i - 1 - n, c, False), carry)
        o_ref[...] = acc.astype(BF16)
        tot = jnp.zeros((blk, ATT_LANES), F32)
        for h in range(ATT_HEADS):
            tot = jnp.where(masks[h], laters[h], tot)
        tot_ref[...] = tot

    out_spec = pl.BlockSpec((blk, ATT_LANES), lambda h, i: (i, h))
    return _call(
        body, name=name, grid=(N_HEADS // ATT_HEADS, nq),
        in_specs=_att_specs(s, blk), out_specs=[out_spec, out_spec],
        out_shape=[jax.ShapeDtypeStruct((s, ATTN_W), BF16), jax.ShapeDtypeStruct((s, ATTN_W), F32)],
        operands=(proj, proj, proj), semantics=("parallel", "arbitrary"), exchange=exchange)


def stick_breaking_bwd(proj, do, tot, name, exchange=None):
    s = proj.shape[0]
    blk = ATT_BLK
    nq = s // blk

    def body(q_ref, k_ref, v_ref, do_ref, tot_ref, dq_ref, dk_ref, dv_ref):
        i = pl.program_id(1)

        @pl.when(i == 0)
        def _():
            dk_ref[...] = jnp.zeros_like(dk_ref)
            dv_ref[...] = jnp.zeros_like(dv_ref)

        row = lax.broadcasted_iota(jnp.int32, (blk, blk), 0)
        col = lax.broadcasted_iota(jnp.int32, (blk, blk), 1)
        before = (row < col).astype(BF16)
        before2 = jnp.concatenate([before, before], axis=0)
        upto = (row <= col).astype(BF16)
        causal = col < row
        masks = _head_masks(blk)
        qs = _per_head(q_ref[...] * ATTN_SCALE, masks)
        dos = _per_head(do_ref[...], masks)
        q_heads = jnp.concatenate(qs, axis=0)
        do_heads = jnp.concatenate(dos, axis=0)
        tot_all = tot_ref[...]
        totals = [jnp.max(jnp.where(m, tot_all, 0.0), axis=1, keepdims=True) for m in masks]

        def step(j, carry, diagonal):
            earliers, g_sums, dq = carry
            rows = pl.ds(pl.multiple_of(j * blk, blk), blk)
            kb = k_ref[rows, :]
            vb = v_ref[rows, :]
            probs, dzs, new_earliers, new_g_sums = [], [], [], []
            for h in range(ATT_HEADS):
                z = _dot(qs[h], kb, NT)
                sp = _softplus(z)
                if diagonal:
                    sp = jnp.where(causal, sp, 0.0)
                c = (totals[h] - earliers[h]) - _split_sum(sp, before2)
                a = jnp.exp(z - c)
                if diagonal:
                    a = jnp.where(causal, a, 0.0)
                g = a * _dot(dos[h], vb, NT)
                f = g_sums[h] + _dot(g.astype(BF16), upto)
                dz = g - jnp.exp(z - sp) * f
                if diagonal:
                    dz = jnp.where(causal, dz, 0.0)
                probs.append(a.astype(BF16))
                dzs.append(dz.astype(BF16))
                new_earliers.append(earliers[h] + jnp.sum(sp, axis=1, keepdims=True))
                new_g_sums.append(g_sums[h] + jnp.sum(g, axis=1, keepdims=True))
            k_heads = jnp.concatenate(_per_head(kb, masks), axis=0)
            dq = dq + _dot(jnp.concatenate(dzs, axis=1), k_heads)
            dk_ref[rows, :] += _dot(jnp.concatenate(dzs, axis=0), q_heads, TN)
            dv_ref[rows, :] += _dot(jnp.concatenate(probs, axis=0), do_heads, TN)
            return tuple(new_earliers), tuple(new_g_sums), dq

        zeros = tuple(jnp.zeros((blk, 1), F32) for _ in range(ATT_HEADS))
        carry = (zeros, zeros, jnp.zeros((blk, ATT_LANES), F32))
        carry = lax.fori_loop(0, i, lambda j, c: step(j, c, False), carry)
        dq = step(i, carry, True)[2]
        dq_ref[...] = (dq * ATTN_SCALE).astype(BF16)

    blk_spec = pl.BlockSpec((blk, ATT_LANES), lambda h, i: (i, h))
    full_spec = pl.BlockSpec((s, ATT_LANES), lambda h, i: (0, h))
    return _call(
        body, name=name, grid=(N_HEADS // ATT_HEADS, nq),
        in_specs=_att_specs(s, blk) + [blk_spec, blk_spec],
        out_specs=[blk_spec, full_spec, full_spec],
        out_shape=[jax.ShapeDtypeStruct((s, ATTN_W), BF16), jax.ShapeDtypeStruct((s, ATTN_W), F32),
                   jax.ShapeDtypeStruct((s, ATTN_W), F32)],
        operands=(proj, proj, proj, do, tot), semantics=("parallel", "arbitrary"), exchange=exchange)


def adamw(w, m, v, parts, name):
    r, c = w.shape
    p = parts.shape[0]
    t = r
    for cand in (256, 176):
        if r % cand == 0 and r > cand:
            t = cand
            break

    def body(w_ref, m_ref, v_ref, p_ref, g_ref, d_ref, mo_ref, vo_ref):
        g = p_ref[0].astype(F32)
        for n in range(1, p):
            g = g + p_ref[n].astype(F32)
        m_new = ADAM_B1 * m_ref[...] + (1.0 - ADAM_B1) * g
        v_new = ADAM_B2 * v_ref[...] + (1.0 - ADAM_B2) * (g * g)
        m_hat = m_new / ADAM_BC1
        v_hat = v_new / ADAM_BC2
        g_ref[...] = g
        d_ref[...] = -ADAM_LR * (m_hat / (jnp.sqrt(v_hat) + ADAM_EPS) + ADAM_WD * w_ref[...])
        mo_ref[...] = m_new
        vo_ref[...] = v_new

    spec = pl.BlockSpec((t, c), lambda i: (i, 0))
    out = jax.ShapeDtypeStruct((r, c), F32)
    return pl.pallas_call(
        body, name=name, grid=(r // t,),
        in_specs=[spec, spec, spec, pl.BlockSpec((p, t, c), lambda i: (0, i, 0))],
        out_specs=[spec] * 4, out_shape=[out] * 4,
        compiler_params=_params("parallel"),
    )(w, m, v, parts)


def kernel(x, c, w_ada, b_ada, norm1_g, ffn1_w_gu, ffn1_w_down, norm2_g, w_mix_in, b_merge, conv_w, w_conv_out, w_attn_out, w_out, norm3_g, ffn2_w_gu, ffn2_w_down, final_g, loss_target, m_w_ada, m_b_ada, m_norm1_g, m_ffn1_w_gu, m_ffn1_w_down, m_norm2_g, m_w_mix_in, m_b_merge, m_conv_w, m_w_conv_out, m_w_attn_out, m_w_out, m_norm3_g, m_ffn2_w_gu, m_ffn2_w_down, m_final_g, v_w_ada, v_b_ada, v_norm1_g, v_ffn1_w_gu, v_ffn1_w_down, v_norm2_g, v_w_mix_in, v_b_merge, v_conv_w, v_w_conv_out, v_w_attn_out, v_w_out, v_norm3_g, v_ffn2_w_gu, v_ffn2_w_down, v_final_g):
    s = x.shape[1]
    me = 4 * lax.axis_index("x") + 2 * lax.axis_index("y") + lax.axis_index("c")
    x0 = x[0]
    target = loss_target[0]
    final_g2 = final_g.reshape(1, D_MODEL)

    def shard(w):
        return w[0].astype(BF16)

    def rows8(g):
        return g.reshape(N_DEV, -1, D_MODEL)

    got = run_exchange(gather_stage1([shard(ffn1_w_gu), shard(ffn1_w_down)]), "gather_ffn1_chips")
    wgu1, wd1 = run_exchange(gather_stage2(got), "gather_ffn1_cores")
    wd1 = wd1.reshape(4, FF_BLK, D_MODEL)

    small_in = jnp.concatenate([c.reshape(-1), b_merge.reshape(-1), conv_w.reshape(-1),
                                jnp.zeros((64,), F32)]).reshape(1, -1)
    small_all = all_gather_rows(small_in, "gather_small")[:, 0, :]
    c_all = small_all[:, :D_MODEL]
    bm_full = small_all[:, 1024:1280].reshape(8, 2, 128).transpose(1, 0, 2).reshape(2, D_MODEL)
    cw_full = small_all[:, 1280:1472].reshape(8, 3, 64).transpose(1, 0, 2).reshape(3, CONV_W)
    n_ada = w_ada.shape[2]
    b_cols = lax.dynamic_slice(b_ada, (0, me * n_ada), (1, n_ada))
    mod_part = ada_forward(c_all, w_ada[0], b_cols, "ada_forward")
    mod_all = all_gather_rows(mod_part, "gather_mod")
    mod = lax.dynamic_index_in_dim(mod_all, me, axis=1, keepdims=False).reshape(9, 1, D_MODEL)
    sh1, sc1, gt1, sh2, sc2, gt2, sh3, sc3, gt3 = [mod[n] for n in range(9)]

    u1 = norm_modulate(x0, norm1_g, sh1, sc1, "norm_mod_1")
    (gu1, act1), got = ffn_up(u1, wgu1, "ffn_up_1", exchange=gather_stage1(
        [shard(w_mix_in), shard(w_conv_out), shard(w_attn_out), shard(w_out)]))
    (x1, y1), (wmix, wco, wao, wout) = residual_matmul(act1, wd1, x0, gt1, 0.5, "ffn_down_1",
                                                     exchange=gather_stage2(got))
    wco = wco.transpose(1, 0, 2).reshape(CONV_W, D_MODEL)
    wao = wao.transpose(1, 0, 2).reshape(ATTN_W, D_MODEL)
    wout = wout.reshape(D_MODEL, D_MODEL)

    u2 = norm_modulate(x1, norm2_g, sh2, sc2, "norm_mod_2")
    proj = mix_in_proj(u2, wmix, "mix_in")
    sa = short_conv(proj, cw_full, "short_conv")
    (o, tot), got = stick_breaking_fwd(proj, "attn_fwd",
                                       exchange=gather_stage1([shard(ffn2_w_gu), shard(ffn2_w_down)]))
    (merged, ya, yb), (wgu3, wd3) = merge_forward(sa, o, proj, wco, wao, bm_full, "merge",
                                                  exchange=gather_stage2(got))
    wd3 = wd3.reshape(4, FF_BLK, D_MODEL)
    x2, y2 = residual_matmul(merged[None], wout[None], x1, gt2, 1.0, "out_proj")

    u3 = norm_modulate(x2, norm3_g, sh3, sc3, "norm_mod_3")
    gu3, act3 = ffn_up(u3, wgu3, "ffn_up_3")
    x3, y3 = residual_matmul(act3, wd3, x2, gt3, 0.5, "ffn_down_3")

    dx3, dy3, dgt3, dfinal, sq = loss_head(x3, target, final_g2, gt3, y3, 0.5, "loss_head")
    dgu3 = ffn_dact(dy3, wd3, gu3, "ffn_dact_3").reshape(8, s, FF_BLK)
    g_wd3 = rows8(matmul_tn(act3, dy3[None], "grad_w_down_3", group=(4, 1)))
    du3 = matmul_nt_acc(dgu3, wgu3, "ffn_du_3")
    g_wgu3 = matmul_tn(u3[None], dgu3, "grad_w_gu_3", group=(1, 4)).reshape(8, D_MODEL, FF_BLK)
    dx2, dsh3, dsc3, dn3, dy2, dgt2 = norm_modulate_bwd(du3, x2, dx3, norm3_g, sc3, "norm_bwd_3",
                                                        prev=(gt2, y2, 1.0))

    (dya, dyb, dga, dgb, dbm), pairs = merge_backward(dy2, wout, proj, ya, yb, bm_full, "merge_bwd",
                                                      exchange=scatter_stage1([g_wgu3, g_wd3]))
    sums3 = [pair_sum(g_wgu3, pairs[0], "pair_sum_w_gu_3"), pair_sum(g_wd3, pairs[1], "pair_sum_w_down_3")]
    g_wout = rows8(matmul_tn(merged[None], dy2[None], "grad_w_out"))
    dsa, do = out_proj_bwd(dya, dyb, wco, wao, "out_proj_bwd")
    g_wco = matmul_tn(sa[None], dya[None], "grad_w_conv_out").reshape(CONV_W, N_DEV, 128).transpose(1, 0, 2)
    g_wao = matmul_tn(o[None], dyb[None], "grad_w_attn_out").reshape(ATTN_W, N_DEV, 128).transpose(1, 0, 2)
    dcb, dcc, dcx, dconv = short_conv_bwd(dsa, proj, cw_full, "short_conv_bwd")
    (dq, dk, dv), landed3 = stick_breaking_bwd(proj, do, tot, "attn_bwd", exchange=scatter_stage2(sums3))
    dproj = jnp.concatenate([dcb, dcc, dcx, dq, dk.astype(BF16), dv.astype(BF16), dga, dgb], axis=1)
    dproj8 = dproj.reshape(s, N_DEV, MIX_BLK).transpose(1, 0, 2)
    du2 = matmul_nt_acc(dproj8, wmix, "mix_in_du")
    g_wmix = matmul_tn(u2[None], dproj8, "grad_w_mix_in", group=(1, 4)).reshape(N_DEV, D_MODEL, MIX_BLK)
    mixer_grads = [g_wmix, g_wco, g_wao, g_wout]
    (dx1, dsh2, dsc2, dn2, dy1, dgt1), pairs = norm_modulate_bwd(
        du2, x1, dx2, norm2_g, sc2, "norm_bwd_2", prev=(gt1, y1, 0.5), exchange=scatter_stage1(mixer_grads))
    sums_mix = [pair_sum(g, p, f"pair_sum_mixer_{n}") for n, (g, p) in enumerate(zip(mixer_grads, pairs))]

    dgu1, landed_mix = ffn_dact(dy1, wd1, gu1, "ffn_dact_1", exchange=scatter_stage2(sums_mix))
    dgu1 = dgu1.reshape(8, s, FF_BLK)
    g_wgu1 = matmul_tn(u1[None], dgu1, "grad_w_gu_1", group=(1, 4)).reshape(8, D_MODEL, FF_BLK)
    g_wd1, pairs = matmul_tn(act1, dy1[None], "grad_w_down_1", group=(4, 1), exchange=scatter_stage1([g_wgu1]))
    g_wd1 = rows8(g_wd1)
    sum_gu1 = pair_sum(g_wgu1, pairs[0], "pair_sum_w_gu_1")
    du1, (landed_gu1, pair_d1) = matmul_nt_acc(
        dgu1, wgu1, "ffn_du_1", exchange=merge_exchanges(scatter_stage2([sum_gu1]), scatter_stage1([g_wd1])))
    sum_d1 = pair_sum(g_wd1, pair_d1, "pair_sum_w_down_1")
    (grad_x, dsh1, dsc1, dn1), landed_d1 = norm_modulate_bwd(du1, x0, dx1, norm1_g, sc1, "norm_bwd_1",
                                                            exchange=scatter_stage2([sum_d1]))

    loss_local = (0.5 / D_MODEL) * jnp.sum(sq)
    stats = jnp.concatenate(
        [v.reshape(-1) for v in (dsh1, dsc1, dgt1, dsh2, dsc2, dgt2, dsh3, dsc3, dgt3,
                                 dn1, dn2, dn3, dfinal, dbm, dconv)]
        + [jnp.broadcast_to(loss_local, (128,))]).reshape(1, -1)
    stats_all = all_gather_rows(stats, "gather_stats")
    n_mod = 9 * D_MODEL
    loss = jnp.sum(stats_all[:, 0, -1])
    dmod_all = stats_all[:, :, :n_mod]
    off = n_mod
    parts = {}
    for key in ("norm1_g", "norm2_g", "norm3_g", "final_g"):
        parts[key] = stats_all[:, :, off:off + D_MODEL]
        off += D_MODEL
    dbm_all = stats_all[:, 0, off:off + 2 * D_MODEL].reshape(N_DEV, 2, D_MODEL)
    off += 2 * D_MODEL
    dcw_all = stats_all[:, 0, off:off + 3 * CONV_W].reshape(N_DEV, 3, CONV_W)
    parts["b_merge"] = lax.dynamic_slice(dbm_all, (0, 0, me * 128), (N_DEV, 2, 128))
    parts["conv_w"] = lax.dynamic_slice(dcw_all, (0, 0, me * 64), (N_DEV, 3, 64))
    dmod_cols = lax.dynamic_slice(dmod_all[:, 0, :], (0, me * n_ada), (N_DEV, n_ada))
    parts["w_ada"] = ada_backward(c_all, dmod_cols, "ada_backward")[None]
    parts["b_ada"] = dmod_all
    parts["ffn2_w_gu"], parts["ffn2_w_down"] = landed3
    parts["w_mix_in"], parts["w_conv_out"], parts["w_attn_out"], parts["w_out"] = landed_mix
    parts["ffn1_w_gu"] = landed_gu1
    parts["ffn1_w_down"] = landed_d1[0]

    given = dict(w_ada=w_ada, b_ada=b_ada, norm1_g=norm1_g, ffn1_w_gu=ffn1_w_gu, ffn1_w_down=ffn1_w_down,
                 norm2_g=norm2_g, w_mix_in=w_mix_in, b_merge=b_merge, conv_w=conv_w, w_conv_out=w_conv_out,
                 w_attn_out=w_attn_out, w_out=w_out, norm3_g=norm3_g, ffn2_w_gu=ffn2_w_gu,
                 ffn2_w_down=ffn2_w_down, final_g=final_g)
    moments_m = dict(w_ada=m_w_ada, b_ada=m_b_ada, norm1_g=m_norm1_g, ffn1_w_gu=m_ffn1_w_gu,
                     ffn1_w_down=m_ffn1_w_down, norm2_g=m_norm2_g, w_mix_in=m_w_mix_in, b_merge=m_b_merge,
                     conv_w=m_conv_w, w_conv_out=m_w_conv_out, w_attn_out=m_w_attn_out, w_out=m_w_out,
                     norm3_g=m_norm3_g, ffn2_w_gu=m_ffn2_w_gu, ffn2_w_down=m_ffn2_w_down, final_g=m_final_g)
    moments_v = dict(w_ada=v_w_ada, b_ada=v_b_ada, norm1_g=v_norm1_g, ffn1_w_gu=v_ffn1_w_gu,
                     ffn1_w_down=v_ffn1_w_down, norm2_g=v_norm2_g, w_mix_in=v_w_mix_in, b_merge=v_b_merge,
                     conv_w=v_conv_w, w_conv_out=v_w_conv_out, w_attn_out=v_w_attn_out, w_out=v_w_out,
                     norm3_g=v_norm3_g, ffn2_w_gu=v_ffn2_w_gu, ffn2_w_down=v_ffn2_w_down, final_g=v_final_g)
    order = ["w_ada", "b_ada", "norm1_g", "ffn1_w_gu", "ffn1_w_down", "norm2_g", "w_mix_in", "b_merge",
             "conv_w", "w_conv_out", "w_attn_out", "w_out", "norm3_g", "ffn2_w_gu", "ffn2_w_down", "final_g"]
    grads, deltas, new_m, new_v = [], [], [], []
    for key in order:
        shape = given[key].shape
        shape2 = (1, shape[0]) if len(shape) == 1 else shape[-2:]
        outs = adamw(given[key].reshape(shape2), moments_m[key].reshape(shape2),
                     moments_v[key].reshape(shape2), parts[key], f"adamw_{key}")
        for dst, val in zip((grads, deltas, new_m, new_v), outs):
            dst.append(val.reshape(shape))

    return (loss, grad_x[None], *grads, *deltas, *new_m, *new_v)
```

```python
import functools
from typing import Callable, NamedTuple

import jax
import jax.numpy as jnp
from jax import lax
from jax.experimental import pallas as pl
from jax.experimental.pallas import tpu as pltpu

F32 = jnp.float32
BF16 = jnp.bfloat16
MESH = pl.DeviceIdType.MESH
ANY = pl.BlockSpec(memory_space=pl.ANY)

N_DEV = 8
D_MODEL = 1024
D_FF = 2816
FF_BLK = D_FF // 4
N_HEADS = 8
HEAD_DIM = 64
CONV_W = 512
ATTN_W = 512
MIX_W = 3 * CONV_W + 3 * ATTN_W + 2 * D_MODEL
MIX_BLK = MIX_W // N_DEV
EPS = 1e-6
ATTN_SCALE = HEAD_DIM ** -0.5

ADAM_LR = 0.001
ADAM_B1 = 0.9
ADAM_B2 = 0.999
ADAM_EPS = 1e-08
ADAM_WD = 0.01
ADAM_STEP = 10
ADAM_BC1 = 1.0 - ADAM_B1 ** ADAM_STEP
ADAM_BC2 = 1.0 - ADAM_B2 ** ADAM_STEP

VMEM_LIMIT = 56 * 1024 * 1024
ROW_TILE = 512
ACC_TILE = 1024
ELT_TILE = 256
ATT_BLK = 256

NN = (((1,), (0,)), ((), ()))
NT = (((1,), (1,)), ((), ()))
TN = (((0,), (0,)), ((), ()))


def _dot(a, b, dims=NN):
    return lax.dot_general(a, b, dims, preferred_element_type=F32)


def _params(*sem):
    return pltpu.CompilerParams(dimension_semantics=sem, vmem_limit_bytes=VMEM_LIMIT)


def _sigmoid(x):
    return 1.0 / (1.0 + jnp.exp(-x))


def _me():
    x, y, c = lax.axis_index("x"), lax.axis_index("y"), lax.axis_index("c")
    return x, y, c, 4 * x + 2 * y + c


def _peer(k):
    x, y, c, _ = _me()
    px = 1 - x if (k >> 2) & 1 else x
    py = 1 - y if (k >> 1) & 1 else y
    pc = 1 - c if k & 1 else c
    return (px, py, pc), 4 * px + 2 * py + pc


class Exchange(NamedTuple):
    operands: tuple
    out_shapes: tuple
    aliases: dict
    n_remote: int
    n_local: int
    copies: Callable


CHIP_FLIPS = (2, 4, 6)
SIBLING = 1


def _remote(src, dst, send_sems, recv_sems, n, peer):
    return pltpu.make_async_remote_copy(src_ref=src, dst_ref=dst, send_sem=send_sems.at[n], recv_sem=recv_sems.at[n],
                                        device_id=peer, device_id_type=MESH)


def gather_stage1(shards):
    n = len(shards)
    rels = (SIBLING,) + CHIP_FLIPS

    def copies(ins, outs, send_sems, recv_sems, local_sems, rb, lb):
        _, _, _, me = _me()
        cps = []
        for w in range(n):
            cps.append(pltpu.make_async_copy(ins[w], outs[w].at[me], local_sems.at[lb + w]))
            for a, k in enumerate(rels):
                peer, _ = _peer(k)
                cps.append(_remote(ins[w], outs[w].at[me], send_sems, recv_sems, rb + len(rels) * w + a, peer))
        return cps

    shapes = tuple(jax.ShapeDtypeStruct((N_DEV,) + s.shape, s.dtype) for s in shards)
    return Exchange(tuple(shards), shapes, {}, len(rels) * n, n, copies)


def gather_stage2(fulls):
    n = len(fulls)

    def copies(ins, outs, send_sems, recv_sems, local_sems, rb, lb):
        sibling, _ = _peer(SIBLING)
        cps = []
        for w in range(n):
            for a, k in enumerate(CHIP_FLIPS):
                _, blk = _peer(k)
                cps.append(_remote(outs[w].at[blk], outs[w].at[blk], send_sems, recv_sems, rb + 3 * w + a, sibling))
        return cps

    shapes = tuple(jax.ShapeDtypeStruct(f.shape, f.dtype) for f in fulls)
    return Exchange(tuple(fulls), shapes, {w: w for w in range(n)}, 3 * n, 0, copies)


def scatter_stage1(fulls):
    n = len(fulls)

    def copies(ins, outs, send_sems, recv_sems, local_sems, rb, lb):
        _, _, c, _ = _me()
        sibling, _ = _peer(SIBLING)
        cps = []
        for w in range(n):
            for q in range(4):
                cps.append(_remote(ins[w].at[2 * q + (1 - c)], outs[w].at[q], send_sems, recv_sems, rb + 4 * w + q, sibling))
        return cps

    shapes = tuple(jax.ShapeDtypeStruct((4,) + f.shape[1:], f.dtype) for f in fulls)
    return Exchange(tuple(fulls), shapes, {}, 4 * n, 0, copies)


def scatter_stage2(sums):
    n = len(sums)

    def copies(ins, outs, send_sems, recv_sems, local_sems, rb, lb):
        x, y, _, _ = _me()
        mine = 2 * x + y
        cps = []
        for w in range(n):
            cps.append(pltpu.make_async_copy(ins[w].at[mine], outs[w].at[mine], local_sems.at[lb + w]))
            for a, k in enumerate(CHIP_FLIPS):
                peer, _ = _peer(k)
                cps.append(_remote(ins[w].at[2 * peer[0] + peer[1]], outs[w].at[mine], send_sems, recv_sems,
                                   rb + 3 * w + a, peer))
        return cps

    shapes = tuple(jax.ShapeDtypeStruct(s.shape, s.dtype) for s in sums)
    return Exchange(tuple(sums), shapes, {}, 3 * n, n, copies)


def merge_exchanges(a, b):
    na_in, na_out = len(a.operands), len(a.out_shapes)

    def copies(ins, outs, send_sems, recv_sems, local_sems, rb, lb):
        return (a.copies(ins[:na_in], outs[:na_out], send_sems, recv_sems, local_sems, rb, lb)
                + b.copies(ins[na_in:], outs[na_out:], send_sems, recv_sems, local_sems, rb + a.n_remote, lb + a.n_local))

    aliases = dict(a.aliases)
    aliases.update({na_in + i: na_out + o for i, o in b.aliases.items()})
    return Exchange(a.operands + b.operands, a.out_shapes + b.out_shapes, aliases,
                    a.n_remote + b.n_remote, a.n_local + b.n_local, copies)


def _exchange_scratch(ex):
    return [pltpu.SemaphoreType.DMA((ex.n_remote,)), pltpu.SemaphoreType.DMA((ex.n_remote,)),
            pltpu.SemaphoreType.DMA((max(ex.n_local, 1),))]


def run_exchange(ex, name):
    n_in, n_out = len(ex.operands), len(ex.out_shapes)

    def body(*refs):
        cps = ex.copies(refs[:n_in], refs[n_in:n_in + n_out], *refs[n_in + n_out:], 0, 0)
        for cp in cps:
            cp.start()
        for cp in cps:
            cp.wait()

    return pl.pallas_call(
        body, name=name, out_shape=list(ex.out_shapes), in_specs=[ANY] * n_in, out_specs=[ANY] * n_out,
        scratch_shapes=_exchange_scratch(ex), input_output_aliases=dict(ex.aliases),
    )(*ex.operands)


def _call(body, *, name, grid, in_specs, out_specs, out_shape, operands, scratch_shapes=(), semantics=(),
          exchange=None):
    if exchange is None:
        return pl.pallas_call(
            body, name=name, grid=grid, in_specs=in_specs, out_specs=out_specs, out_shape=out_shape,
            scratch_shapes=list(scratch_shapes), compiler_params=_params(*semantics))(*operands)
    single = not isinstance(out_shape, (list, tuple))
    out_shapes = [out_shape] if single else list(out_shape)
    out_specs_l = [out_specs] if single else list(out_specs)
    n_in, n_out, n_scr = len(operands), len(out_shapes), len(scratch_shapes)
    x_in, x_out = len(exchange.operands), len(exchange.out_shapes)

    def hosted(*refs):
        ins, refs = refs[:n_in], refs[n_in:]
        xin, refs = refs[:x_in], refs[x_in:]
        outs, refs = refs[:n_out], refs[n_out:]
        xout, refs = refs[:x_out], refs[x_out:]
        scr, sems = refs[:n_scr], refs[n_scr:]
        first = functools.reduce(jnp.logical_and, [pl.program_id(a) == 0 for a in range(len(grid))])
        last = functools.reduce(jnp.logical_and, [pl.program_id(a) == g - 1 for a, g in enumerate(grid)])

        @pl.when(first)
        def _():
            for cp in exchange.copies(xin, xout, *sems, 0, 0):
                cp.start()

        body(*ins, *outs, *scr)

        @pl.when(last)
        def _():
            for cp in exchange.copies(xin, xout, *sems, 0, 0):
                cp.wait()

    res = pl.pallas_call(
        hosted, name=name, grid=grid,
        in_specs=list(in_specs) + [ANY] * x_in, out_specs=out_specs_l + [ANY] * x_out,
        out_shape=out_shapes + list(exchange.out_shapes),
        scratch_shapes=list(scratch_shapes) + _exchange_scratch(exchange),
        input_output_aliases={n_in + i: n_out + o for i, o in exchange.aliases.items()},
        compiler_params=_params(*(["arbitrary"] * len(grid))),
    )(*operands, *exchange.operands)
    outs, xouts = res[:n_out], res[n_out:]
    return (outs[0] if single else outs), xouts


def all_gather_rows(v, name):
    r, n = v.shape

    def body(v_ref, out_ref, send_sems, recv_sems):
        _, _, _, me = _me()
        out_ref[me] = v_ref[...]
        copies = []
        for k in range(1, N_DEV):
            peer, _ = _peer(k)
            copies.append(_remote(v_ref, out_ref.at[me], send_sems, recv_sems, k - 1, peer))
        for cp in copies:
            cp.start()
        for cp in copies:
            cp.wait()

    return pl.pallas_call(
        body, name=name,
        out_shape=jax.ShapeDtypeStruct((N_DEV, r, n), v.dtype),
        in_specs=[pl.BlockSpec(memory_space=pltpu.VMEM)],
        out_specs=pl.BlockSpec(memory_space=pltpu.VMEM),
        scratch_shapes=[pltpu.SemaphoreType.DMA((N_DEV - 1,)), pltpu.SemaphoreType.DMA((N_DEV - 1,))],
    )(v)


def pair_sum(full, pair, name):
    _, r, c = full.shape
    t = 256 if r % 256 == 0 and r > 256 else r
    core = lax.axis_index("c").astype(jnp.int32).reshape(1)

    def body(core_ref, f_ref, p_ref, o_ref):
        o_ref[...] = (f_ref[...].astype(F32) + p_ref[...].astype(F32)).astype(BF16)

    return pl.pallas_call(
        body, name=name,
        grid_spec=pltpu.PrefetchScalarGridSpec(
            num_scalar_prefetch=1, grid=(4, r // t),
            in_specs=[pl.BlockSpec((None, None, t, c), lambda q, i, core_ref: (q, core_ref[0], i, 0)),
                      pl.BlockSpec((None, t, c), lambda q, i, core_ref: (q, i, 0))],
            out_specs=pl.BlockSpec((None, t, c), lambda q, i, core_ref: (q, i, 0))),
        out_shape=jax.ShapeDtypeStruct((4, r, c), BF16),
        compiler_params=_params("parallel", "parallel"),
    )(core, full.reshape(4, 2, r, c), pair)


def ada_forward(c_all, w_ada, b_cols, name):
    n = w_ada.shape[1]

    def body(c_ref, w_ref, b_ref, o_ref):
        c = c_ref[...]
        act = c * _sigmoid(c)
        o_ref[...] = jnp.dot(act, w_ref[...], precision=lax.Precision.HIGHEST,
                             preferred_element_type=F32) + b_ref[...]

    return pl.pallas_call(
        body, name=name, out_shape=jax.ShapeDtypeStruct((N_DEV, n), F32),
        compiler_params=pltpu.CompilerParams(vmem_limit_bytes=VMEM_LIMIT),
    )(c_all, w_ada, b_cols)


def ada_backward(c_all, dmod_cols, name):
    n = dmod_cols.shape[1]

    def body(c_ref, d_ref, o_ref):
        c = c_ref[...]
        act = c * _sigmoid(c)
        o_ref[...] = lax.dot_general(act, d_ref[...], TN, precision=lax.Precision.HIGHEST,
                                     preferred_element_type=F32)

    return pl.pallas_call(
        body, name=name, out_shape=jax.ShapeDtypeStruct((D_MODEL, n), F32),
        compiler_params=pltpu.CompilerParams(vmem_limit_bytes=VMEM_LIMIT),
    )(c_all, dmod_cols)


def _row_spec(t, width=D_MODEL):
    return pl.BlockSpec((t, width), lambda i: (i, 0))


def _vec_spec(rows=1, width=D_MODEL):
    return pl.BlockSpec((rows, width), lambda i: (0, 0))


def norm_modulate(x, g, shift, scale, name):
    s = x.shape[0]
    t = min(ELT_TILE, s)

    def body(x_ref, g_ref, sh_ref, sc_ref, u_ref):
        xv = x_ref[...]
        r = lax.rsqrt(jnp.mean(xv * xv, axis=-1, keepdims=True) + EPS)
        a = (xv * r) * g_ref[...]
        u_ref[...] = (a * (1.0 + sc_ref[...]) + sh_ref[...]).astype(BF16)

    return pl.pallas_call(
        body, name=name, grid=(s // t,),
        in_specs=[_row_spec(t), _vec_spec(), _vec_spec(), _vec_spec()],
        out_specs=_row_spec(t),
        out_shape=jax.ShapeDtypeStruct((s, D_MODEL), BF16),
        compiler_params=_params("parallel"),
    )(x, g, shift, scale)


def loss_head(x, target, final_g, gate, y_prev, coef, name):
    s = x.shape[0]
    t = min(ELT_TILE, s)

    def body(x_ref, t_ref, fg_ref, gt_ref, y_ref, dx_ref, dy_ref, dgt_ref, dfg_ref, sq_ref):
        @pl.when(pl.program_id(0) == 0)
        def _():
            dgt_ref[...] = jnp.zeros_like(dgt_ref)
            dfg_ref[...] = jnp.zeros_like(dfg_ref)
            sq_ref[...] = jnp.zeros_like(sq_ref)

        xv = x_ref[...]
        fg = fg_ref[...]
        r = lax.rsqrt(jnp.mean(xv * xv, axis=-1, keepdims=True) + EPS)
        nrm = xv * r
        err = nrm * fg - t_ref[...]
        sq_ref[...] += jnp.sum(err * err, axis=0, keepdims=True)
        dout = err * (1.0 / D_MODEL)
        dfg_ref[...] += jnp.sum(dout * nrm, axis=0, keepdims=True)
        dn = dout * fg
        dx = r * (dn - nrm * jnp.mean(dn * nrm, axis=-1, keepdims=True))
        dx_ref[...] = dx
        dy_ref[...] = (coef * gt_ref[...] * dx).astype(BF16)
        dgt_ref[...] += coef * jnp.sum(dx * y_ref[...].astype(F32), axis=0, keepdims=True)

    vec = jax.ShapeDtypeStruct((1, D_MODEL), F32)
    return pl.pallas_call(
        body, name=name, grid=(s // t,),
        in_specs=[_row_spec(t), _row_spec(t), _vec_spec(), _vec_spec(), _row_spec(t)],
        out_specs=[_row_spec(t), _row_spec(t), _vec_spec(), _vec_spec(), _vec_spec()],
        out_shape=[jax.ShapeDtypeStruct((s, D_MODEL), F32), jax.ShapeDtypeStruct((s, D_MODEL), BF16),
                   vec, vec, vec],
        compiler_params=_params("arbitrary"),
    )(x, target, final_g, gate, y_prev)


def norm_modulate_bwd(du, x, dx_out, g, scale, name, prev=None, exchange=None):
    s = x.shape[0]
    t = min(ELT_TILE, s)
    has_prev = prev is not None

    def body(*refs):
        du_ref, x_ref, dxo_ref, g_ref, sc_ref = refs[:5]
        refs = refs[5:]
        if has_prev:
            gt_ref, y_ref = refs[:2]
            refs = refs[2:]
        dx_ref, dsh_ref, dsc_ref, dg_ref = refs[:4]

        @pl.when(pl.program_id(0) == 0)
        def _():
            dsh_ref[...] = jnp.zeros_like(dsh_ref)
            dsc_ref[...] = jnp.zeros_like(dsc_ref)
            dg_ref[...] = jnp.zeros_like(dg_ref)
            if has_prev:
                refs[5][...] = jnp.zeros_like(refs[5])

        xv = x_ref[...]
        duv = du_ref[...]
        gv = g_ref[...]
        r = lax.rsqrt(jnp.mean(xv * xv, axis=-1, keepdims=True) + EPS)
        nrm = xv * r
        a = nrm * gv
        dsh_ref[...] += jnp.sum(duv, axis=0, keepdims=True)
        dsc_ref[...] += jnp.sum(duv * a, axis=0, keepdims=True)
        da = duv * (1.0 + sc_ref[...])
        dg_ref[...] += jnp.sum(da * nrm, axis=0, keepdims=True)
        dn = da * gv
        dx = dxo_ref[...] + r * (dn - nrm * jnp.mean(dn * nrm, axis=-1, keepdims=True))
        dx_ref[...] = dx
        if has_prev:
            coef = prev[2]
            refs[4][...] = (coef * gt_ref[...] * dx).astype(BF16)
            refs[5][...] += coef * jnp.sum(dx * y_ref[...].astype(F32), axis=0, keepdims=True)

    vec = jax.ShapeDtypeStruct((1, D_MODEL), F32)
    operands = [du, x, dx_out, g, scale]
    in_specs = [_row_spec(t), _row_spec(t), _row_spec(t), _vec_spec(), _vec_spec()]
    out_specs = [_row_spec(t), _vec_spec(), _vec_spec(), _vec_spec()]
    out_shape = [jax.ShapeDtypeStruct((s, D_MODEL), F32), vec, vec, vec]
    if has_prev:
        operands += [prev[0], prev[1]]
        in_specs += [_vec_spec(), _row_spec(t)]
        out_specs += [_row_spec(t), _vec_spec()]
        out_shape += [jax.ShapeDtypeStruct((s, D_MODEL), BF16), vec]
    return _call(body, name=name, grid=(s // t,), in_specs=in_specs, out_specs=out_specs, out_shape=out_shape,
                 operands=operands, semantics=("arbitrary",), exchange=exchange)


def ffn_up(u, w_gu, name, exchange=None):
    s = u.shape[0]
    t = min(ROW_TILE, s)

    def body(u_ref, wg_ref, wu_ref, gu_ref, act_ref):
        uv = u_ref[...]
        g = _dot(uv, wg_ref[...])
        up = _dot(uv, wu_ref[...])
        gu_ref[0] = g.astype(BF16)
        gu_ref[1] = up.astype(BF16)
        act_ref[...] = (g * _sigmoid(g) * up).astype(BF16)

    return _call(
        body, name=name, grid=(4, s // t),
        in_specs=[pl.BlockSpec((t, D_MODEL), lambda j, i: (i, 0)),
                  pl.BlockSpec((None, D_MODEL, FF_BLK), lambda j, i: (j, 0, 0)),
                  pl.BlockSpec((None, D_MODEL, FF_BLK), lambda j, i: (j + 4, 0, 0))],
        out_specs=[pl.BlockSpec((2, None, t, FF_BLK), lambda j, i: (0, j, i, 0)),
                   pl.BlockSpec((None, t, FF_BLK), lambda j, i: (j, i, 0))],
        out_shape=[jax.ShapeDtypeStruct((2, 4, s, FF_BLK), BF16),
                   jax.ShapeDtypeStruct((4, s, FF_BLK), BF16)],
        operands=(u, w_gu, w_gu), semantics=("parallel", "parallel"), exchange=exchange)


def residual_matmul(a, b, x, gate, coef, name, exchange=None):
    nk, s, kb = a.shape
    t = min(ROW_TILE, s)

    def body(a_ref, b_ref, x_ref, gt_ref, xo_ref, y_ref):
        y = _dot(a_ref[0], b_ref[0])
        for k in range(1, nk):
            y = y + _dot(a_ref[k], b_ref[k])
        y_ref[...] = y.astype(BF16)
        xo_ref[...] = x_ref[...] + coef * gt_ref[...] * y

    return _call(
        body, name=name, grid=(s // t,),
        in_specs=[pl.BlockSpec((nk, t, kb), lambda i: (0, i, 0)),
                  pl.BlockSpec((nk, kb, D_MODEL), lambda i: (0, 0, 0)),
                  _row_spec(t), _vec_spec()],
        out_specs=[_row_spec(t), _row_spec(t)],
        out_shape=[jax.ShapeDtypeStruct((s, D_MODEL), F32), jax.ShapeDtypeStruct((s, D_MODEL), BF16)],
        operands=(a, b, x, gate), semantics=("parallel",), exchange=exchange)


def ffn_dact(dy, w_down, gu, name, exchange=None):
    s = dy.shape[0]
    t = min(ROW_TILE, s)

    def body(dy_ref, w_ref, gu_ref, dgu_ref):
        dact = _dot(dy_ref[...], w_ref[...], NT)
        g = gu_ref[0].astype(F32)
        up = gu_ref[1].astype(F32)
        sg = _sigmoid(g)
        dgu_ref[0] = (dact * up * sg * (1.0 + g * (1.0 - sg))).astype(BF16)
        dgu_ref[1] = (dact * g * sg).astype(BF16)

    return _call(
        body, name=name, grid=(4, s // t),
        in_specs=[pl.BlockSpec((t, D_MODEL), lambda j, i: (i, 0)),
                  pl.BlockSpec((None, FF_BLK, D_MODEL), lambda j, i: (j, 0, 0)),
                  pl.BlockSpec((2, None, t, FF_BLK), lambda j, i: (0, j, i, 0))],
        out_specs=pl.BlockSpec((2, None, t, FF_BLK), lambda j, i: (0, j, i, 0)),
        out_shape=jax.ShapeDtypeStruct((2, 4, s, FF_BLK), BF16),
        operands=(dy, w_down, gu), semantics=("parallel", "parallel"), exchange=exchange)


def matmul_nt_acc(a, b, name, exchange=None):
    nk, d, n = b.shape
    s = a.shape[-2]
    t = min(ROW_TILE, s)
    by_columns = a.ndim == 2

    def body(a_ref, b_ref, o_ref):
        def a_blk(k):
            return a_ref[:, k * n:(k + 1) * n] if by_columns else a_ref[k]

        acc = _dot(a_blk(0), b_ref[0], NT)
        for k in range(1, nk):
            acc = acc + _dot(a_blk(k), b_ref[k], NT)
        o_ref[...] = acc

    a_spec = _row_spec(t, nk * n) if by_columns else pl.BlockSpec((nk, t, n), lambda i: (0, i, 0))
    return _call(
        body, name=name, grid=(s // t,),
        in_specs=[a_spec, pl.BlockSpec((nk, d, n), lambda i: (0, 0, 0))],
        out_specs=pl.BlockSpec((t, d), lambda i: (i, 0)),
        out_shape=jax.ShapeDtypeStruct((s, d), F32),
        operands=(a, b), semantics=("parallel",), exchange=exchange)


def matmul_tn(a, b, name, group=(1, 1), b_cols=None, exchange=None):
    ja, s, m = a.shape
    by_columns = b.ndim == 2
    jb, n = (b.shape[1] // b_cols, b_cols) if by_columns else (b.shape[0], b.shape[2])
    ga, gb = group
    t = min(ACC_TILE, s)
    nk = s // t

    def body(a_ref, b_ref, o_ref, acc_ref):
        k = pl.program_id(2)

        @pl.when(k == 0)
        def _():
            acc_ref[...] = jnp.zeros_like(acc_ref)

        for p in range(ga):
            for q in range(gb):
                b_blk = b_ref[:, q * n:(q + 1) * n] if by_columns else b_ref[q]
                acc_ref[p, q] += _dot(a_ref[p], b_blk, TN)

        @pl.when(k == nk - 1)
        def _():
            o_ref[...] = acc_ref[...].astype(BF16)

    return _call(
        body, name=name, grid=(ja // ga, jb // gb, nk),
        in_specs=[pl.BlockSpec((ga, t, m), lambda p, q, k: (p, k, 0)),
                  pl.BlockSpec((t, gb * n), lambda p, q, k: (k, q)) if by_columns
                  else pl.BlockSpec((gb, t, n), lambda p, q, k: (q, k, 0))],
        out_specs=pl.BlockSpec((ga, gb, m, n), lambda p, q, k: (p, q, 0, 0)),
        out_shape=jax.ShapeDtypeStruct((ja, jb, m, n), BF16),
        operands=(a, b), scratch_shapes=[pltpu.VMEM((ga, gb, m, n), F32)],
        semantics=("parallel", "parallel", "arbitrary"), exchange=exchange)


def mix_in_proj(u, w_mix, name, exchange=None):
    s = u.shape[0]
    t = min(ROW_TILE, s)

    def body(u_ref, w_ref, o_ref):
        uv = u_ref[...]
        for j in range(N_DEV):
            o_ref[:, j * MIX_BLK:(j + 1) * MIX_BLK] = _dot(uv, w_ref[j]).astype(BF16)

    return _call(
        body, name=name, grid=(s // t,),
        in_specs=[_row_spec(t), pl.BlockSpec((N_DEV, D_MODEL, MIX_BLK), lambda i: (0, 0, 0))],
        out_specs=_row_spec(t, MIX_W),
        out_shape=jax.ShapeDtypeStruct((s, MIX_W), BF16),
        operands=(u, w_mix), semantics=("parallel",), exchange=exchange)


def _conv_taps(cc_ref, cx_ref, s):
    v = cc_ref[...].astype(F32) * cx_ref[...].astype(F32)
    tok = lax.broadcasted_iota(jnp.int32, v.shape, 0)
    v1 = jnp.where(tok >= 1, pltpu.roll(v, 1, 0), 0.0)
    v2 = jnp.where(tok >= 2, pltpu.roll(v, 2, 0), 0.0)
    return v, v1, v2, tok


def _proj_cols(s, first):
    return pl.BlockSpec((s, 128), lambda j: (0, first + j))


def short_conv(proj, conv_w, name):
    s = proj.shape[0]

    def body(cb_ref, cc_ref, cx_ref, w_ref, o_ref):
        v, v1, v2, _ = _conv_taps(cc_ref, cx_ref, s)
        y = w_ref[0:1, :] * v2 + w_ref[1:2, :] * v1 + w_ref[2:3, :] * v
        o_ref[...] = (cb_ref[...].astype(F32) * y).astype(BF16)

    return pl.pallas_call(
        body, name=name, grid=(CONV_W // 128,),
        in_specs=[_proj_cols(s, 0), _proj_cols(s, 4), _proj_cols(s, 8),
                  pl.BlockSpec((3, 128), lambda j: (0, j))],
        out_specs=pl.BlockSpec((s, 128), lambda j: (0, j)),
        out_shape=jax.ShapeDtypeStruct((s, CONV_W), BF16),
        compiler_params=_params("parallel"),
    )(proj, proj, proj, conv_w)


def short_conv_bwd(dsa, proj, conv_w, name):
    s = proj.shape[0]

    def body(dsa_ref, cb_ref, cc_ref, cx_ref, w_ref, dcb_ref, dcc_ref, dcx_ref, dw_ref):
        v, v1, v2, tok = _conv_taps(cc_ref, cx_ref, s)
        w0, w1, w2 = w_ref[0:1, :], w_ref[1:2, :], w_ref[2:3, :]
        y = w0 * v2 + w1 * v1 + w2 * v
        dsa_v = dsa_ref[...].astype(F32)
        dcb_ref[...] = (dsa_v * y).astype(BF16)
        dy = dsa_v * cb_ref[...].astype(F32)
        dw_ref[0:1, :] = jnp.sum(dy * v2, axis=0, keepdims=True)
        dw_ref[1:2, :] = jnp.sum(dy * v1, axis=0, keepdims=True)
        dw_ref[2:3, :] = jnp.sum(dy * v, axis=0, keepdims=True)
        dy1 = jnp.where(tok < s - 1, pltpu.roll(dy, s - 1, 0), 0.0)
        dy2 = jnp.where(tok < s - 2, pltpu.roll(dy, s - 2, 0), 0.0)
        dv = w2 * dy + w1 * dy1 + w0 * dy2
        dcc_ref[...] = (dv * cx_ref[...].astype(F32)).astype(BF16)
        dcx_ref[...] = (dv * cc_ref[...].astype(F32)).astype(BF16)

    col = pl.BlockSpec((s, 128), lambda j: (0, j))
    act = jax.ShapeDtypeStruct((s, CONV_W), BF16)
    return pl.pallas_call(
        body, name=name, grid=(CONV_W // 128,),
        in_specs=[col, _proj_cols(s, 0), _proj_cols(s, 4), _proj_cols(s, 8),
                  pl.BlockSpec((3, 128), lambda j: (0, j))],
        out_specs=[col, col, col, pl.BlockSpec((3, 128), lambda j: (0, j))],
        out_shape=[act, act, act, jax.ShapeDtypeStruct((3, CONV_W), F32)],
        compiler_params=_params("parallel"),
    )(dsa, proj, proj, proj, conv_w)


def _gate_specs(t):
    return [pl.BlockSpec((t, D_MODEL), lambda i: (i, 3)), pl.BlockSpec((t, D_MODEL), lambda i: (i, 4))]


def merge_forward(sa, o, proj, w_co, w_ao, b_merge, name, exchange=None):
    s = sa.shape[0]
    t = min(ROW_TILE, s)

    def body(sa_ref, o_ref, ga_ref, gb_ref, wco_ref, wao_ref, bm_ref, mg_ref, ya_ref, yb_ref):
        ya = _dot(sa_ref[...], wco_ref[...])
        yb = _dot(o_ref[...], wao_ref[...])
        sga = _sigmoid(ga_ref[...].astype(F32) + bm_ref[0:1, :])
        sgb = _sigmoid(gb_ref[...].astype(F32) + bm_ref[1:2, :])
        mg_ref[...] = (sga * ya + sgb * yb).astype(BF16)
        ya_ref[...] = ya.astype(BF16)
        yb_ref[...] = yb.astype(BF16)

    act = jax.ShapeDtypeStruct((s, D_MODEL), BF16)
    return _call(
        body, name=name, grid=(s // t,),
        in_specs=[_row_spec(t, CONV_W), _row_spec(t, ATTN_W)] + _gate_specs(t)
        + [_vec_spec(CONV_W), _vec_spec(ATTN_W), _vec_spec(2)],
        out_specs=[_row_spec(t)] * 3, out_shape=[act, act, act],
        operands=(sa, o, proj, proj, w_co, w_ao, b_merge), semantics=("parallel",), exchange=exchange)


def merge_backward(dy, w_out, proj, ya, yb, b_merge, name, exchange=None):
    s = dy.shape[0]
    t = min(ROW_TILE, s)

    def body(dy_ref, w_ref, ga_ref, gb_ref, ya_ref, yb_ref, bm_ref,
             dya_ref, dyb_ref, dga_ref, dgb_ref, dbm_ref):
        @pl.when(pl.program_id(0) == 0)
        def _():
            dbm_ref[...] = jnp.zeros_like(dbm_ref)

        dmg = _dot(dy_ref[...], w_ref[...], NT)
        sga = _sigmoid(ga_ref[...].astype(F32) + bm_ref[0:1, :])
        sgb = _sigmoid(gb_ref[...].astype(F32) + bm_ref[1:2, :])
        dya_ref[...] = (dmg * sga).astype(BF16)
        dyb_ref[...] = (dmg * sgb).astype(BF16)
        dga = dmg * ya_ref[...].astype(F32) * sga * (1.0 - sga)
        dgb = dmg * yb_ref[...].astype(F32) * sgb * (1.0 - sgb)
        dga_ref[...] = dga.astype(BF16)
        dgb_ref[...] = dgb.astype(BF16)
        dbm_ref[0:1, :] += jnp.sum(dga, axis=0, keepdims=True)
        dbm_ref[1:2, :] += jnp.sum(dgb, axis=0, keepdims=True)

    act = jax.ShapeDtypeStruct((s, D_MODEL), BF16)
    return _call(
        body, name=name, grid=(s // t,),
        in_specs=[_row_spec(t), _vec_spec(D_MODEL)] + _gate_specs(t)
        + [_row_spec(t), _row_spec(t), _vec_spec(2)],
        out_specs=[_row_spec(t)] * 4 + [_vec_spec(2)],
        out_shape=[act] * 4 + [jax.ShapeDtypeStruct((2, D_MODEL), F32)],
        operands=(dy, w_out, proj, proj, ya, yb, b_merge), semantics=("arbitrary",), exchange=exchange)


def out_proj_bwd(dya, dyb, w_co, w_ao, name):
    s = dya.shape[0]
    t = min(ROW_TILE, s)

    def body(dya_ref, dyb_ref, wco_ref, wao_ref, dsa_ref, do_ref):
        dsa_ref[...] = _dot(dya_ref[...], wco_ref[...], NT).astype(BF16)
        do_ref[...] = _dot(dyb_ref[...], wao_ref[...], NT).astype(BF16)

    return pl.pallas_call(
        body, name=name, grid=(s // t,),
        in_specs=[_row_spec(t), _row_spec(t), _vec_spec(CONV_W), _vec_spec(ATTN_W)],
        out_specs=[_row_spec(t, CONV_W), _row_spec(t, ATTN_W)],
        out_shape=[jax.ShapeDtypeStruct((s, CONV_W), BF16), jax.ShapeDtypeStruct((s, ATTN_W), BF16)],
        compiler_params=_params("parallel"),
    )(dya, dyb, w_co, w_ao)


ATT_HEADS = 4
ATT_LANES = ATT_HEADS * HEAD_DIM


def _softplus(z):
    return jnp.maximum(z, 0.0) + jnp.log(1.0 + jnp.exp(-jnp.abs(z)))


def _head_masks(rows):
    lane = lax.broadcasted_iota(jnp.int32, (rows, ATT_LANES), 1)
    return [(lane >= h * HEAD_DIM) & (lane < (h + 1) * HEAD_DIM) for h in range(ATT_HEADS)]


def _per_head(x, masks):
    return [jnp.where(m, x, jnp.zeros_like(x)) for m in masks]


def _att_specs(s, blk):
    first = {"q": 3 * CONV_W // ATT_LANES, "k": (3 * CONV_W + ATTN_W) // ATT_LANES,
             "v": (3 * CONV_W + 2 * ATTN_W) // ATT_LANES}
    return [pl.BlockSpec((blk, ATT_LANES), lambda h, i: (i, first["q"] + h)),
            pl.BlockSpec((s, ATT_LANES), lambda h, i: (0, first["k"] + h)),
            pl.BlockSpec((s, ATT_LANES), lambda h, i: (0, first["v"] + h))]


def stick_breaking_fwd(proj, name, exchange=None):
    s = proj.shape[0]
    blk = ATT_BLK
    nq = s // blk

    def body(q_ref, k_ref, v_ref, o_ref, tot_ref):
        i = pl.program_id(1)
        row = lax.broadcasted_iota(jnp.int32, (blk, blk), 0)
        col = lax.broadcasted_iota(jnp.int32, (blk, blk), 1)
        tri = (row >= col).astype(BF16)
        causal = col < row
        masks = _head_masks(blk)
        qs = _per_head(q_ref[...] * ATTN_SCALE, masks)

        def step(j, carry, diagonal):
            laters, acc = carry
            rows = pl.ds(pl.multiple_of(j * blk, blk), blk)
            kb = k_ref[rows, :]
            probs, new_laters = [], []
            for h in range(ATT_HEADS):
                z = _dot(qs[h], kb, NT)
                sp = _softplus(z)
                if diagonal:
                    sp = jnp.where(causal, sp, 0.0)
                a = jnp.exp(z - (_dot(sp.astype(BF16), tri) + laters[h]))
                if diagonal:
                    a = jnp.where(causal, a, 0.0)
                probs.append(a.astype(BF16))
                new_laters.append(laters[h] + jnp.sum(sp, axis=1, keepdims=True))
            v_heads = jnp.concatenate(_per_head(v_ref[rows, :], masks), axis=0)
            acc = acc + _dot(jnp.concatenate(probs, axis=1), v_heads)
            return tuple(new_laters), acc

        carry = (tuple(jnp.zeros((blk, 1), F32) for _ in range(ATT_HEADS)), jnp.zeros((blk, ATT_LANES), F32))
        carry = step(i, carry, True)
        laters, acc = lax.fori_loop(0, i, lambda n, c: step(i - 1 - n, c, False), carry)
        o_ref[...] = acc.astype(BF16)
        tot = jnp.zeros((blk, ATT_LANES), F32)
        for h in range(ATT_HEADS):
            tot = jnp.where(masks[h], laters[h], tot)
        tot_ref[...] = tot

    out_spec = pl.BlockSpec((blk, ATT_LANES), lambda h, i: (i, h))
    return _call(
        body, name=name, grid=(N_HEADS // ATT_HEADS, nq),
        in_specs=_att_specs(s, blk), out_specs=[out_spec, out_spec],
        out_shape=[jax.ShapeDtypeStruct((s, ATTN_W), BF16), jax.ShapeDtypeStruct((s, ATTN_W), F32)],
        operands=(proj, proj, proj), semantics=("parallel", "arbitrary"), exchange=exchange)


def stick_breaking_bwd(proj, do, tot, name, exchange=None):
    s = proj.shape[0]
    blk = ATT_BLK
    nq = s // blk

    def body(q_ref, k_ref, v_ref, do_ref, tot_ref, dq_ref, dk_ref, dv_ref):
        i = pl.program_id(1)

        @pl.when(i == 0)
        def _():
            dk_ref[...] = jnp.zeros_like(dk_ref)
            dv_ref[...] = jnp.zeros_like(dv_ref)

        row = lax.broadcasted_iota(jnp.int32, (blk, blk), 0)
        col = lax.broadcasted_iota(jnp.int32, (blk, blk), 1)
        before = (row < col).astype(BF16)
        upto = (row <= col).astype(BF16)
        causal = col < row
        masks = _head_masks(blk)
        qs = _per_head(q_ref[...] * ATTN_SCALE, masks)
        dos = _per_head(do_ref[...], masks)
        q_heads = jnp.concatenate(qs, axis=0)
        do_heads = jnp.concatenate(dos, axis=0)
        tot_all = tot_ref[...]
        totals = [jnp.max(jnp.where(m, tot_all, 0.0), axis=1, keepdims=True) for m in masks]

        def step(j, carry, diagonal):
            earliers, g_sums, dq = carry
            rows = pl.ds(pl.multiple_of(j * blk, blk), blk)
            kb = k_ref[rows, :]
            vb = v_ref[rows, :]
            probs, dzs, new_earliers, new_g_sums = [], [], [], []
            for h in range(ATT_HEADS):
                z = _dot(qs[h], kb, NT)
                sp = _softplus(z)
                if diagonal:
                    sp = jnp.where(causal, sp, 0.0)
                c = (totals[h] - earliers[h]) - _dot(sp.astype(BF16), before)
                a = jnp.exp(z - c)
                if diagonal:
                    a = jnp.where(causal, a, 0.0)
                g = a * _dot(dos[h], vb, NT)
                f = g_sums[h] + _dot(g.astype(BF16), upto)
                dz = g - jnp.exp(z - sp) * f
                if diagonal:
                    dz = jnp.where(causal, dz, 0.0)
                probs.append(a.astype(BF16))
                dzs.append(dz.astype(BF16))
                new_earliers.append(earliers[h] + jnp.sum(sp, axis=1, keepdims=True))
                new_g_sums.append(g_sums[h] + jnp.sum(g, axis=1, keepdims=True))
            k_heads = jnp.concatenate(_per_head(kb, masks), axis=0)
            dq = dq + _dot(jnp.concatenate(dzs, axis=1), k_heads)
            dk_ref[rows, :] += _dot(jnp.concatenate(dzs, axis=0), q_heads, TN)
            dv_ref[rows, :] += _dot(jnp.concatenate(probs, axis=0), do_heads, TN)
            return tuple(new_earliers), tuple(new_g_sums), dq

        zeros = tuple(jnp.zeros((blk, 1), F32) for _ in range(ATT_HEADS))
        carry = (zeros, zeros, jnp.zeros((blk, ATT_LANES), F32))
        carry = lax.fori_loop(0, i, lambda j, c: step(j, c, False), carry)
        dq = step(i, carry, True)[2]
        dq_ref[...] = (dq * ATTN_SCALE).astype(BF16)

    blk_spec = pl.BlockSpec((blk, ATT_LANES), lambda h, i: (i, h))
    full_spec = pl.BlockSpec((s, ATT_LANES), lambda h, i: (0, h))
    return _call(
        body, name=name, grid=(N_HEADS // ATT_HEADS, nq),
        in_specs=_att_specs(s, blk) + [blk_spec, blk_spec],
        out_specs=[blk_spec, full_spec, full_spec],
        out_shape=[jax.ShapeDtypeStruct((s, ATTN_W), BF16), jax.ShapeDtypeStruct((s, ATTN_W), F32),
                   jax.ShapeDtypeStruct((s, ATTN_W), F32)],
        operands=(proj, proj, proj, do, tot), semantics=("parallel", "arbitrary"), exchange=exchange)


def adamw(w, m, v, parts, name):
    r, c = w.shape
    p = parts.shape[0]
    t = r
    for cand in (256, 176):
        if r % cand == 0 and r > cand:
            t = cand
            break

    def body(w_ref, m_ref, v_ref, p_ref, g_ref, d_ref, mo_ref, vo_ref):
        g = p_ref[0].astype(F32)
        for n in range(1, p):
            g = g + p_ref[n].astype(F32)
        m_new = ADAM_B1 * m_ref[...] + (1.0 - ADAM_B1) * g
        v_new = ADAM_B2 * v_ref[...] + (1.0 - ADAM_B2) * (g * g)
        m_hat = m_new / ADAM_BC1
        v_hat = v_new / ADAM_BC2
        g_ref[...] = g
        d_ref[...] = -ADAM_LR * (m_hat / (jnp.sqrt(v_hat) + ADAM_EPS) + ADAM_WD * w_ref[...])
        mo_ref[...] = m_new
        vo_ref[...] = v_new

    spec = pl.BlockSpec((t, c), lambda i: (i, 0))
    out = jax.ShapeDtypeStruct((r, c), F32)
    return pl.pallas_call(
        body, name=name, grid=(r // t,),
        in_specs=[spec, spec, spec, pl.BlockSpec((p, t, c), lambda i: (0, i, 0))],
        out_specs=[spec] * 4, out_shape=[out] * 4,
        compiler_params=_params("parallel"),
    )(w, m, v, parts)


def kernel(x, c, w_ada, b_ada, norm1_g, ffn1_w_gu, ffn1_w_down, norm2_g, w_mix_in, b_merge, conv_w, w_conv_out, w_attn_out, w_out, norm3_g, ffn2_w_gu, ffn2_w_down, final_g, loss_target, m_w_ada, m_b_ada, m_norm1_g, m_ffn1_w_gu, m_ffn1_w_down, m_norm2_g, m_w_mix_in, m_b_merge, m_conv_w, m_w_conv_out, m_w_attn_out, m_w_out, m_norm3_g, m_ffn2_w_gu, m_ffn2_w_down, m_final_g, v_w_ada, v_b_ada, v_norm1_g, v_ffn1_w_gu, v_ffn1_w_down, v_norm2_g, v_w_mix_in, v_b_merge, v_conv_w, v_w_conv_out, v_w_attn_out, v_w_out, v_norm3_g, v_ffn2_w_gu, v_ffn2_w_down, v_final_g):
    s = x.shape[1]
    me = 4 * lax.axis_index("x") + 2 * lax.axis_index("y") + lax.axis_index("c")
    x0 = x[0]
    target = loss_target[0]
    final_g2 = final_g.reshape(1, D_MODEL)

    def shard(w):
        return w[0].astype(BF16)

    def rows8(g):
        return g.reshape(N_DEV, -1, D_MODEL)

    got = run_exchange(gather_stage1([shard(ffn1_w_gu), shard(ffn1_w_down)]), "gather_ffn1_chips")
    wgu1, wd1 = run_exchange(gather_stage2(got), "gather_ffn1_cores")
    wd1 = wd1.reshape(4, FF_BLK, D_MODEL)

    small_in = jnp.concatenate([c.reshape(-1), b_merge.reshape(-1), conv_w.reshape(-1),
                                jnp.zeros((64,), F32)]).reshape(1, -1)
    small_all = all_gather_rows(small_in, "gather_small")[:, 0, :]
    c_all = small_all[:, :D_MODEL]
    bm_full = small_all[:, 1024:1280].reshape(8, 2, 128).transpose(1, 0, 2).reshape(2, D_MODEL)
    cw_full = small_all[:, 1280:1472].reshape(8, 3, 64).transpose(1, 0, 2).reshape(3, CONV_W)
    n_ada = w_ada.shape[2]
    b_cols = lax.dynamic_slice(b_ada, (0, me * n_ada), (1, n_ada))
    mod_part = ada_forward(c_all, w_ada[0], b_cols, "ada_forward")
    mod_all = all_gather_rows(mod_part, "gather_mod")
    mod = lax.dynamic_index_in_dim(mod_all, me, axis=1, keepdims=False).reshape(9, 1, D_MODEL)
    sh1, sc1, gt1, sh2, sc2, gt2, sh3, sc3, gt3 = [mod[n] for n in range(9)]

    u1 = norm_modulate(x0, norm1_g, sh1, sc1, "norm_mod_1")
    (gu1, act1), got = ffn_up(u1, wgu1, "ffn_up_1", exchange=gather_stage1([shard(w_mix_in)]))
    (x1, y1), (wmix, *got) = residual_matmul(
        act1, wd1, x0, gt1, 0.5, "ffn_down_1", exchange=merge_exchanges(
            gather_stage2(got), gather_stage1([shard(w_conv_out), shard(w_attn_out), shard(w_out)])))

    u2 = norm_modulate(x1, norm2_g, sh2, sc2, "norm_mod_2")
    proj, (wco, wao, wout) = mix_in_proj(u2, wmix, "mix_in", exchange=gather_stage2(got))
    wco = wco.transpose(1, 0, 2).reshape(CONV_W, D_MODEL)
    wao = wao.transpose(1, 0, 2).reshape(ATTN_W, D_MODEL)
    wout = wout.reshape(D_MODEL, D_MODEL)
    sa = short_conv(proj, cw_full, "short_conv")
    (o, tot), got = stick_breaking_fwd(proj, "attn_fwd",
                                       exchange=gather_stage1([shard(ffn2_w_gu), shard(ffn2_w_down)]))
    (merged, ya, yb), (wgu3, wd3) = merge_forward(sa, o, proj, wco, wao, bm_full, "merge",
                                                  exchange=gather_stage2(got))
    wd3 = wd3.reshape(4, FF_BLK, D_MODEL)
    x2, y2 = residual_matmul(merged[None], wout[None], x1, gt2, 1.0, "out_proj")

    u3 = norm_modulate(x2, norm3_g, sh3, sc3, "norm_mod_3")
    gu3, act3 = ffn_up(u3, wgu3, "ffn_up_3")
    x3, y3 = residual_matmul(act3, wd3, x2, gt3, 0.5, "ffn_down_3")

    dx3, dy3, dgt3, dfinal, sq = loss_head(x3, target, final_g2, gt3, y3, 0.5, "loss_head")
    dgu3 = ffn_dact(dy3, wd3, gu3, "ffn_dact_3").reshape(8, s, FF_BLK)
    g_wd3 = rows8(matmul_tn(act3, dy3[None], "grad_w_down_3", group=(4, 1)))
    du3 = matmul_nt_acc(dgu3, wgu3, "ffn_du_3")
    g_wgu3 = matmul_tn(u3[None], dgu3, "grad_w_gu_3", group=(1, 4)).reshape(8, D_MODEL, FF_BLK)
    dx2, dsh3, dsc3, dn3, dy2, dgt2 = norm_modulate_bwd(du3, x2, dx3, norm3_g, sc3, "norm_bwd_3",
                                                        prev=(gt2, y2, 1.0))

    (dya, dyb, dga, dgb, dbm), pairs = merge_backward(dy2, wout, proj, ya, yb, bm_full, "merge_bwd",
                                                      exchange=scatter_stage1([g_wgu3, g_wd3]))
    sums3 = [pair_sum(g_wgu3, pairs[0], "pair_sum_w_gu_3"), pair_sum(g_wd3, pairs[1], "pair_sum_w_down_3")]
    g_wout = rows8(matmul_tn(merged[None], dy2[None], "grad_w_out"))
    dsa, do = out_proj_bwd(dya, dyb, wco, wao, "out_proj_bwd")
    g_wco = matmul_tn(sa[None], dya[None], "grad_w_conv_out").reshape(CONV_W, N_DEV, 128).transpose(1, 0, 2)
    g_wao = matmul_tn(o[None], dyb[None], "grad_w_attn_out").reshape(ATTN_W, N_DEV, 128).transpose(1, 0, 2)
    dcb, dcc, dcx, dconv = short_conv_bwd(dsa, proj, cw_full, "short_conv_bwd")
    (dq, dk, dv), landed3 = stick_breaking_bwd(proj, do, tot, "attn_bwd", exchange=scatter_stage2(sums3))
    dproj = jnp.concatenate([dcb, dcc, dcx, dq, dk.astype(BF16), dv.astype(BF16), dga, dgb], axis=1)
    du2 = matmul_nt_acc(dproj, wmix, "mix_in_du")
    g_wmix = matmul_tn(u2[None], dproj, "grad_w_mix_in", group=(1, 4), b_cols=MIX_BLK).reshape(
        N_DEV, D_MODEL, MIX_BLK)
    mixer_grads = [g_wmix, g_wco, g_wao, g_wout]
    (dx1, dsh2, dsc2, dn2, dy1, dgt1), pairs = norm_modulate_bwd(
        du2, x1, dx2, norm2_g, sc2, "norm_bwd_2", prev=(gt1, y1, 0.5), exchange=scatter_stage1(mixer_grads))
    sums_mix = [pair_sum(g, p, f"pair_sum_mixer_{n}") for n, (g, p) in enumerate(zip(mixer_grads, pairs))]

    dgu1, landed_mix = ffn_dact(dy1, wd1, gu1, "ffn_dact_1", exchange=scatter_stage2(sums_mix[:1]))
    dgu1 = dgu1.reshape(8, s, FF_BLK)
    g_wgu1, landed_small = matmul_tn(u1[None], dgu1, "grad_w_gu_1", group=(1, 4),
                                     exchange=scatter_stage2(sums_mix[1:]))
    g_wgu1 = g_wgu1.reshape(8, D_MODEL, FF_BLK)
    g_wd1, pairs = matmul_tn(act1, dy1[None], "grad_w_down_1", group=(4, 1), exchange=scatter_stage1([g_wgu1]))
    g_wd1 = rows8(g_wd1)
    sum_gu1 = pair_sum(g_wgu1, pairs[0], "pair_sum_w_gu_1")
    du1, (landed_gu1, pair_d1) = matmul_nt_acc(
        dgu1, wgu1, "ffn_du_1", exchange=merge_exchanges(scatter_stage2([sum_gu1]), scatter_stage1([g_wd1])))
    sum_d1 = pair_sum(g_wd1, pair_d1, "pair_sum_w_down_1")
    (grad_x, dsh1, dsc1, dn1), landed_d1 = norm_modulate_bwd(du1, x0, dx1, norm1_g, sc1, "norm_bwd_1",
                                                            exchange=scatter_stage2([sum_d1]))

    loss_local = (0.5 / D_MODEL) * jnp.sum(sq)
    stats = jnp.concatenate(
        [v.reshape(-1) for v in (dsh1, dsc1, dgt1, dsh2, dsc2, dgt2, dsh3, dsc3, dgt3,
                                 dn1, dn2, dn3, dfinal, dbm, dconv)]
        + [jnp.broadcast_to(loss_local, (128,))]).reshape(1, -1)
    stats_all = all_gather_rows(stats, "gather_stats")
    n_mod = 9 * D_MODEL
    loss = jnp.sum(stats_all[:, 0, -1])
    dmod_all = stats_all[:, :, :n_mod]
    off = n_mod
    parts = {}
    for key in ("norm1_g", "norm2_g", "norm3_g", "final_g"):
        parts[key] = stats_all[:, :, off:off + D_MODEL]
        off += D_MODEL
    dbm_all = stats_all[:, 0, off:off + 2 * D_MODEL].reshape(N_DEV, 2, D_MODEL)
    off += 2 * D_MODEL
    dcw_all = stats_all[:, 0, off:off + 3 * CONV_W].reshape(N_DEV, 3, CONV_W)
    parts["b_merge"] = lax.dynamic_slice(dbm_all, (0, 0, me * 128), (N_DEV, 2, 128))
    parts["conv_w"] = lax.dynamic_slice(dcw_all, (0, 0, me * 64), (N_DEV, 3, 64))
    dmod_cols = lax.dynamic_slice(dmod_all[:, 0, :], (0, me * n_ada), (N_DEV, n_ada))
    parts["w_ada"] = ada_backward(c_all, dmod_cols, "ada_backward")[None]
    parts["b_ada"] = dmod_all
    parts["ffn2_w_gu"], parts["ffn2_w_down"] = landed3
    parts["w_mix_in"] = landed_mix[0]
    parts["w_conv_out"], parts["w_attn_out"], parts["w_out"] = landed_small
    parts["ffn1_w_gu"] = landed_gu1
    parts["ffn1_w_down"] = landed_d1[0]

    given = dict(w_ada=w_ada, b_ada=b_ada, norm1_g=norm1_g, ffn1_w_gu=ffn1_w_gu, ffn1_w_down=ffn1_w_down,
                 norm2_g=norm2_g, w_mix_in=w_mix_in, b_merge=b_merge, conv_w=conv_w, w_conv_out=w_conv_out,
                 w_attn_out=w_attn_out, w_out=w_out, norm3_g=norm3_g, ffn2_w_gu=ffn2_w_gu,
                 ffn2_w_down=ffn2_w_down, final_g=final_g)
    moments_m = dict(w_ada=m_w_ada, b_ada=m_b_ada, norm1_g=m_norm1_g, ffn1_w_gu=m_ffn1_w_gu,
                     ffn1_w_down=m_ffn1_w_down, norm2_g=m_norm2_g, w_mix_in=m_w_mix_in, b_merge=m_b_merge,
                     conv_w=m_conv_w, w_conv_out=m_w_conv_out, w_attn_out=m_w_attn_out, w_out=m_w_out,
                     norm3_g=m_norm3_g, ffn2_w_gu=m_ffn2_w_gu, ffn2_w_down=m_ffn2_w_down, final_g=m_final_g)
    moments_v = dict(w_ada=v_w_ada, b_ada=v_b_ada, norm1_g=v_norm1_g, ffn1_w_gu=v_ffn1_w_gu,
                     ffn1_w_down=v_ffn1_w_down, norm2_g=v_norm2_g, w_mix_in=v_w_mix_in, b_merge=v_b_merge,
                     conv_w=v_conv_w, w_conv_out=v_w_conv_out, w_attn_out=v_w_attn_out, w_out=v_w_out,
                     norm3_g=v_norm3_g, ffn2_w_gu=v_ffn2_w_gu, ffn2_w_down=v_ffn2_w_down, final_g=v_final_g)
    order = ["w_ada", "b_ada", "norm1_g", "ffn1_w_gu", "ffn1_w_down", "norm2_g", "w_mix_in", "b_merge",
             "conv_w", "w_conv_out", "w_attn_out", "w_out", "norm3_g", "ffn2_w_gu", "ffn2_w_down", "final_g"]
    grads, deltas, new_m, new_v = [], [], [], []
    for key in order:
        shape = given[key].shape
        shape2 = (1, shape[0]) if len(shape) == 1 else shape[-2:]
        outs = adamw(given[key].reshape(shape2), moments_m[key].reshape(shape2),
                     moments_v[key].reshape(shape2), parts[key], f"adamw_{key}")
        for dst, val in zip((grads, deltas, new_m, new_v), outs):
            dst.append(val.reshape(shape))

    return (loss, grad_x[None], *grads, *deltas, *new_m, *new_v)
```

```python
import functools
from typing import Callable, NamedTuple

import jax
import jax.numpy as jnp
from jax import lax
from jax.experimental import pallas as pl
from jax.experimental.pallas import tpu as pltpu

F32 = jnp.float32
BF16 = jnp.bfloat16
MESH = pl.DeviceIdType.MESH
ANY = pl.BlockSpec(memory_space=pl.ANY)

N_DEV = 8
D_MODEL = 1024
D_FF = 2816
FF_BLK = D_FF // 4
N_HEADS = 8
HEAD_DIM = 64
CONV_W = 512
ATTN_W = 512
MIX_W = 3 * CONV_W + 3 * ATTN_W + 2 * D_MODEL
MIX_BLK = MIX_W // N_DEV
EPS = 1e-6
ATTN_SCALE = HEAD_DIM ** -0.5

ADAM_LR = 0.001
ADAM_B1 = 0.9
ADAM_B2 = 0.999
ADAM_EPS = 1e-08
ADAM_WD = 0.01
ADAM_STEP = 10
ADAM_BC1 = 1.0 - ADAM_B1 ** ADAM_STEP
ADAM_BC2 = 1.0 - ADAM_B2 ** ADAM_STEP

VMEM_LIMIT = 56 * 1024 * 1024
ROW_TILE = 512
ACC_TILE = 1024
ELT_TILE = 256
ATT_BLK = 256

NN = (((1,), (0,)), ((), ()))
NT = (((1,), (1,)), ((), ()))
TN = (((0,), (0,)), ((), ()))


def _dot(a, b, dims=NN):
    return lax.dot_general(a, b, dims, preferred_element_type=F32)


def _params(*sem):
    return pltpu.CompilerParams(dimension_semantics=sem, vmem_limit_bytes=VMEM_LIMIT)


def _sigmoid(x):
    return 1.0 / (1.0 + jnp.exp(-x))


def _me():
    x, y, c = lax.axis_index("x"), lax.axis_index("y"), lax.axis_index("c")
    return x, y, c, 4 * x + 2 * y + c


def _peer(k):
    x, y, c, _ = _me()
    px = 1 - x if (k >> 2) & 1 else x
    py = 1 - y if (k >> 1) & 1 else y
    pc = 1 - c if k & 1 else c
    return (px, py, pc), 4 * px + 2 * py + pc


class Exchange(NamedTuple):
    operands: tuple
    out_shapes: tuple
    aliases: dict
    n_remote: int
    n_local: int
    copies: Callable


CHIP_FLIPS = (2, 4, 6)
SIBLING = 1


def _remote(src, dst, send_sems, recv_sems, n, peer):
    return pltpu.make_async_remote_copy(src_ref=src, dst_ref=dst, send_sem=send_sems.at[n], recv_sem=recv_sems.at[n],
                                        device_id=peer, device_id_type=MESH)


def gather_stage1(shards):
    n = len(shards)
    rels = (SIBLING,) + CHIP_FLIPS

    def copies(ins, outs, send_sems, recv_sems, local_sems, rb, lb):
        _, _, _, me = _me()
        cps = []
        for w in range(n):
            cps.append(pltpu.make_async_copy(ins[w], outs[w].at[me], local_sems.at[lb + w]))
            for a, k in enumerate(rels):
                peer, _ = _peer(k)
                cps.append(_remote(ins[w], outs[w].at[me], send_sems, recv_sems, rb + len(rels) * w + a, peer))
        return cps

    shapes = tuple(jax.ShapeDtypeStruct((N_DEV,) + s.shape, s.dtype) for s in shards)
    return Exchange(tuple(shards), shapes, {}, len(rels) * n, n, copies)


def gather_stage2(fulls):
    n = len(fulls)

    def copies(ins, outs, send_sems, recv_sems, local_sems, rb, lb):
        sibling, _ = _peer(SIBLING)
        cps = []
        for w in range(n):
            for a, k in enumerate(CHIP_FLIPS):
                _, blk = _peer(k)
                cps.append(_remote(outs[w].at[blk], outs[w].at[blk], send_sems, recv_sems, rb + 3 * w + a, sibling))
        return cps

    shapes = tuple(jax.ShapeDtypeStruct(f.shape, f.dtype) for f in fulls)
    return Exchange(tuple(fulls), shapes, {w: w for w in range(n)}, 3 * n, 0, copies)


def scatter_stage1(fulls):
    n = len(fulls)

    def copies(ins, outs, send_sems, recv_sems, local_sems, rb, lb):
        _, _, c, _ = _me()
        sibling, _ = _peer(SIBLING)
        cps = []
        for w in range(n):
            for q in range(4):
                cps.append(_remote(ins[w].at[2 * q + (1 - c)], outs[w].at[q], send_sems, recv_sems, rb + 4 * w + q, sibling))
        return cps

    shapes = tuple(jax.ShapeDtypeStruct((4,) + f.shape[1:], f.dtype) for f in fulls)
    return Exchange(tuple(fulls), shapes, {}, 4 * n, 0, copies)


def scatter_stage2(sums):
    n = len(sums)

    def copies(ins, outs, send_sems, recv_sems, local_sems, rb, lb):
        x, y, _, _ = _me()
        mine = 2 * x + y
        cps = []
        for w in range(n):
            cps.append(pltpu.make_async_copy(ins[w].at[mine], outs[w].at[mine], local_sems.at[lb + w]))
            for a, k in enumerate(CHIP_FLIPS):
                peer, _ = _peer(k)
                cps.append(_remote(ins[w].at[2 * peer[0] + peer[1]], outs[w].at[mine], send_sems, recv_sems,
                                   rb + 3 * w + a, peer))
        return cps

    shapes = tuple(jax.ShapeDtypeStruct(s.shape, s.dtype) for s in sums)
    return Exchange(tuple(sums), shapes, {}, 3 * n, n, copies)


def merge_exchanges(a, b):
    na_in, na_out = len(a.operands), len(a.out_shapes)

    def copies(ins, outs, send_sems, recv_sems, local_sems, rb, lb):
        return (a.copies(ins[:na_in], outs[:na_out], send_sems, recv_sems, local_sems, rb, lb)
                + b.copies(ins[na_in:], outs[na_out:], send_sems, recv_sems, local_sems, rb + a.n_remote, lb + a.n_local))

    aliases = dict(a.aliases)
    aliases.update({na_in + i: na_out + o for i, o in b.aliases.items()})
    return Exchange(a.operands + b.operands, a.out_shapes + b.out_shapes, aliases,
                    a.n_remote + b.n_remote, a.n_local + b.n_local, copies)


def _exchange_scratch(ex):
    return [pltpu.SemaphoreType.DMA((ex.n_remote,)), pltpu.SemaphoreType.DMA((ex.n_remote,)),
            pltpu.SemaphoreType.DMA((max(ex.n_local, 1),))]


def run_exchange(ex, name):
    n_in, n_out = len(ex.operands), len(ex.out_shapes)

    def body(*refs):
        cps = ex.copies(refs[:n_in], refs[n_in:n_in + n_out], *refs[n_in + n_out:], 0, 0)
        for cp in cps:
            cp.start()
        for cp in cps:
            cp.wait()

    return pl.pallas_call(
        body, name=name, out_shape=list(ex.out_shapes), in_specs=[ANY] * n_in, out_specs=[ANY] * n_out,
        scratch_shapes=_exchange_scratch(ex), input_output_aliases=dict(ex.aliases),
    )(*ex.operands)


def _call(body, *, name, grid, in_specs, out_specs, out_shape, operands, scratch_shapes=(), semantics=(),
          exchange=None):
    if exchange is None:
        return pl.pallas_call(
            body, name=name, grid=grid, in_specs=in_specs, out_specs=out_specs, out_shape=out_shape,
            scratch_shapes=list(scratch_shapes), compiler_params=_params(*semantics))(*operands)
    single = not isinstance(out_shape, (list, tuple))
    out_shapes = [out_shape] if single else list(out_shape)
    out_specs_l = [out_specs] if single else list(out_specs)
    n_in, n_out, n_scr = len(operands), len(out_shapes), len(scratch_shapes)
    x_in, x_out = len(exchange.operands), len(exchange.out_shapes)

    def hosted(*refs):
        ins, refs = refs[:n_in], refs[n_in:]
        xin, refs = refs[:x_in], refs[x_in:]
        outs, refs = refs[:n_out], refs[n_out:]
        xout, refs = refs[:x_out], refs[x_out:]
        scr, sems = refs[:n_scr], refs[n_scr:]
        first = functools.reduce(jnp.logical_and, [pl.program_id(a) == 0 for a in range(len(grid))])
        last = functools.reduce(jnp.logical_and, [pl.program_id(a) == g - 1 for a, g in enumerate(grid)])

        @pl.when(first)
        def _():
            for cp in exchange.copies(xin, xout, *sems, 0, 0):
                cp.start()

        body(*ins, *outs, *scr)

        @pl.when(last)
        def _():
            for cp in exchange.copies(xin, xout, *sems, 0, 0):
                cp.wait()

    res = pl.pallas_call(
        hosted, name=name, grid=grid,
        in_specs=list(in_specs) + [ANY] * x_in, out_specs=out_specs_l + [ANY] * x_out,
        out_shape=out_shapes + list(exchange.out_shapes),
        scratch_shapes=list(scratch_shapes) + _exchange_scratch(exchange),
        input_output_aliases={n_in + i: n_out + o for i, o in exchange.aliases.items()},
        compiler_params=_params(*(["arbitrary"] * len(grid))),
    )(*operands, *exchange.operands)
    outs, xouts = res[:n_out], res[n_out:]
    return (outs[0] if single else outs), xouts


def all_gather_rows(v, name):
    r, n = v.shape

    def body(v_ref, out_ref, send_sems, recv_sems):
        _, _, _, me = _me()
        out_ref[me] = v_ref[...]
        copies = []
        for k in range(1, N_DEV):
            peer, _ = _peer(k)
            copies.append(_remote(v_ref, out_ref.at[me], send_sems, recv_sems, k - 1, peer))
        for cp in copies:
            cp.start()
        for cp in copies:
            cp.wait()

    return pl.pallas_call(
        body, name=name,
        out_shape=jax.ShapeDtypeStruct((N_DEV, r, n), v.dtype),
        in_specs=[pl.BlockSpec(memory_space=pltpu.VMEM)],
        out_specs=pl.BlockSpec(memory_space=pltpu.VMEM),
        scratch_shapes=[pltpu.SemaphoreType.DMA((N_DEV - 1,)), pltpu.SemaphoreType.DMA((N_DEV - 1,))],
    )(v)


def pair_sum(full, pair, name):
    _, r, c = full.shape
    t = 256 if r % 256 == 0 and r > 256 else r
    core = lax.axis_index("c").astype(jnp.int32).reshape(1)

    def body(core_ref, f_ref, p_ref, o_ref):
        o_ref[...] = (f_ref[...].astype(F32) + p_ref[...].astype(F32)).astype(BF16)

    return pl.pallas_call(
        body, name=name,
        grid_spec=pltpu.PrefetchScalarGridSpec(
            num_scalar_prefetch=1, grid=(4, r // t),
            in_specs=[pl.BlockSpec((None, None, t, c), lambda q, i, core_ref: (q, core_ref[0], i, 0)),
                      pl.BlockSpec((None, t, c), lambda q, i, core_ref: (q, i, 0))],
            out_specs=pl.BlockSpec((None, t, c), lambda q, i, core_ref: (q, i, 0))),
        out_shape=jax.ShapeDtypeStruct((4, r, c), BF16),
        compiler_params=_params("parallel", "parallel"),
    )(core, full.reshape(4, 2, r, c), pair)


def ada_forward(c_all, w_ada, b_cols, name):
    n = w_ada.shape[1]

    def body(c_ref, w_ref, b_ref, o_ref):
        c = c_ref[...]
        act = c * _sigmoid(c)
        o_ref[...] = jnp.dot(act, w_ref[...], precision=lax.Precision.HIGHEST,
                             preferred_element_type=F32) + b_ref[...]

    return pl.pallas_call(
        body, name=name, out_shape=jax.ShapeDtypeStruct((N_DEV, n), F32),
        compiler_params=pltpu.CompilerParams(vmem_limit_bytes=VMEM_LIMIT),
    )(c_all, w_ada, b_cols)


def ada_backward(c_all, dmod_cols, name):
    n = dmod_cols.shape[1]

    def body(c_ref, d_ref, o_ref):
        c = c_ref[...]
        act = c * _sigmoid(c)
        o_ref[...] = lax.dot_general(act, d_ref[...], TN, precision=lax.Precision.HIGHEST,
                                     preferred_element_type=F32)

    return pl.pallas_call(
        body, name=name, out_shape=jax.ShapeDtypeStruct((D_MODEL, n), F32),
        compiler_params=pltpu.CompilerParams(vmem_limit_bytes=VMEM_LIMIT),
    )(c_all, dmod_cols)


def _row_spec(t, width=D_MODEL):
    return pl.BlockSpec((t, width), lambda i: (i, 0))


def _vec_spec(rows=1, width=D_MODEL):
    return pl.BlockSpec((rows, width), lambda i: (0, 0))


def norm_modulate(x, g, shift, scale, name):
    s = x.shape[0]
    t = min(ELT_TILE, s)

    def body(x_ref, g_ref, sh_ref, sc_ref, u_ref):
        xv = x_ref[...]
        r = lax.rsqrt(jnp.mean(xv * xv, axis=-1, keepdims=True) + EPS)
        a = (xv * r) * g_ref[...]
        u_ref[...] = (a * (1.0 + sc_ref[...]) + sh_ref[...]).astype(BF16)

    return pl.pallas_call(
        body, name=name, grid=(s // t,),
        in_specs=[_row_spec(t), _vec_spec(), _vec_spec(), _vec_spec()],
        out_specs=_row_spec(t),
        out_shape=jax.ShapeDtypeStruct((s, D_MODEL), BF16),
        compiler_params=_params("parallel"),
    )(x, g, shift, scale)


def loss_head(x, target, final_g, gate, y_prev, coef, name):
    s = x.shape[0]
    t = min(ELT_TILE, s)

    def body(x_ref, t_ref, fg_ref, gt_ref, y_ref, dx_ref, dy_ref, dgt_ref, dfg_ref, sq_ref):
        @pl.when(pl.program_id(0) == 0)
        def _():
            dgt_ref[...] = jnp.zeros_like(dgt_ref)
            dfg_ref[...] = jnp.zeros_like(dfg_ref)
            sq_ref[...] = jnp.zeros_like(sq_ref)

        xv = x_ref[...]
        fg = fg_ref[...]
        r = lax.rsqrt(jnp.mean(xv * xv, axis=-1, keepdims=True) + EPS)
        nrm = xv * r
        err = nrm * fg - t_ref[...]
        sq_ref[...] += jnp.sum(err * err, axis=0, keepdims=True)
        dout = err * (1.0 / D_MODEL)
        dfg_ref[...] += jnp.sum(dout * nrm, axis=0, keepdims=True)
        dn = dout * fg
        dx = r * (dn - nrm * jnp.mean(dn * nrm, axis=-1, keepdims=True))
        dx_ref[...] = dx
        dy_ref[...] = (coef * gt_ref[...] * dx).astype(BF16)
        dgt_ref[...] += coef * jnp.sum(dx * y_ref[...].astype(F32), axis=0, keepdims=True)

    vec = jax.ShapeDtypeStruct((1, D_MODEL), F32)
    return pl.pallas_call(
        body, name=name, grid=(s // t,),
        in_specs=[_row_spec(t), _row_spec(t), _vec_spec(), _vec_spec(), _row_spec(t)],
        out_specs=[_row_spec(t), _row_spec(t), _vec_spec(), _vec_spec(), _vec_spec()],
        out_shape=[jax.ShapeDtypeStruct((s, D_MODEL), F32), jax.ShapeDtypeStruct((s, D_MODEL), BF16),
                   vec, vec, vec],
        compiler_params=_params("arbitrary"),
    )(x, target, final_g, gate, y_prev)


def norm_modulate_bwd(du, x, dx_out, g, scale, name, prev=None, exchange=None):
    s = x.shape[0]
    t = min(ELT_TILE, s)
    has_prev = prev is not None

    def body(*refs):
        du_ref, x_ref, dxo_ref, g_ref, sc_ref = refs[:5]
        refs = refs[5:]
        if has_prev:
            gt_ref, y_ref = refs[:2]
            refs = refs[2:]
        dx_ref, dsh_ref, dsc_ref, dg_ref = refs[:4]

        @pl.when(pl.program_id(0) == 0)
        def _():
            dsh_ref[...] = jnp.zeros_like(dsh_ref)
            dsc_ref[...] = jnp.zeros_like(dsc_ref)
            dg_ref[...] = jnp.zeros_like(dg_ref)
            if has_prev:
                refs[5][...] = jnp.zeros_like(refs[5])

        xv = x_ref[...]
        duv = du_ref[...]
        gv = g_ref[...]
        r = lax.rsqrt(jnp.mean(xv * xv, axis=-1, keepdims=True) + EPS)
        nrm = xv * r
        a = nrm * gv
        dsh_ref[...] += jnp.sum(duv, axis=0, keepdims=True)
        dsc_ref[...] += jnp.sum(duv * a, axis=0, keepdims=True)
        da = duv * (1.0 + sc_ref[...])
        dg_ref[...] += jnp.sum(da * nrm, axis=0, keepdims=True)
        dn = da * gv
        dx = dxo_ref[...] + r * (dn - nrm * jnp.mean(dn * nrm, axis=-1, keepdims=True))
        dx_ref[...] = dx
        if has_prev:
            coef = prev[2]
            refs[4][...] = (coef * gt_ref[...] * dx).astype(BF16)
            refs[5][...] += coef * jnp.sum(dx * y_ref[...].astype(F32), axis=0, keepdims=True)

    vec = jax.ShapeDtypeStruct((1, D_MODEL), F32)
    operands = [du, x, dx_out, g, scale]
    in_specs = [_row_spec(t), _row_spec(t), _row_spec(t), _vec_spec(), _vec_spec()]
    out_specs = [_row_spec(t), _vec_spec(), _vec_spec(), _vec_spec()]
    out_shape = [jax.ShapeDtypeStruct((s, D_MODEL), F32), vec, vec, vec]
    if has_prev:
        operands += [prev[0], prev[1]]
        in_specs += [_vec_spec(), _row_spec(t)]
        out_specs += [_row_spec(t), _vec_spec()]
        out_shape += [jax.ShapeDtypeStruct((s, D_MODEL), BF16), vec]
    return _call(body, name=name, grid=(s // t,), in_specs=in_specs, out_specs=out_specs, out_shape=out_shape,
                 operands=operands, semantics=("arbitrary",), exchange=exchange)


def ffn_up(u, w_gu, name, exchange=None):
    s = u.shape[0]
    t = min(ROW_TILE, s)

    def body(u_ref, wg_ref, wu_ref, gu_ref, act_ref):
        uv = u_ref[...]
        g = _dot(uv, wg_ref[...])
        up = _dot(uv, wu_ref[...])
        gu_ref[0] = g.astype(BF16)
        gu_ref[1] = up.astype(BF16)
        act_ref[...] = (g * _sigmoid(g) * up).astype(BF16)

    return _call(
        body, name=name, grid=(4, s // t),
        in_specs=[pl.BlockSpec((t, D_MODEL), lambda j, i: (i, 0)),
                  pl.BlockSpec((None, D_MODEL, FF_BLK), lambda j, i: (j, 0, 0)),
                  pl.BlockSpec((None, D_MODEL, FF_BLK), lambda j, i: (j + 4, 0, 0))],
        out_specs=[pl.BlockSpec((2, None, t, FF_BLK), lambda j, i: (0, j, i, 0)),
                   pl.BlockSpec((None, t, FF_BLK), lambda j, i: (j, i, 0))],
        out_shape=[jax.ShapeDtypeStruct((2, 4, s, FF_BLK), BF16),
                   jax.ShapeDtypeStruct((4, s, FF_BLK), BF16)],
        operands=(u, w_gu, w_gu), semantics=("parallel", "parallel"), exchange=exchange)


def residual_matmul(a, b, x, gate, coef, name, exchange=None):
    nk, s, kb = a.shape
    t = min(ROW_TILE, s)

    def body(a_ref, b_ref, x_ref, gt_ref, xo_ref, y_ref):
        y = _dot(a_ref[0], b_ref[0])
        for k in range(1, nk):
            y = y + _dot(a_ref[k], b_ref[k])
        y_ref[...] = y.astype(BF16)
        xo_ref[...] = x_ref[...] + coef * gt_ref[...] * y

    return _call(
        body, name=name, grid=(s // t,),
        in_specs=[pl.BlockSpec((nk, t, kb), lambda i: (0, i, 0)),
                  pl.BlockSpec((nk, kb, D_MODEL), lambda i: (0, 0, 0)),
                  _row_spec(t), _vec_spec()],
        out_specs=[_row_spec(t), _row_spec(t)],
        out_shape=[jax.ShapeDtypeStruct((s, D_MODEL), F32), jax.ShapeDtypeStruct((s, D_MODEL), BF16)],
        operands=(a, b, x, gate), semantics=("parallel",), exchange=exchange)


def ffn_dact(dy, w_down, gu, name, exchange=None):
    s = dy.shape[0]
    t = min(ROW_TILE, s)

    def body(dy_ref, w_ref, gu_ref, dgu_ref):
        dact = _dot(dy_ref[...], w_ref[...], NT)
        g = gu_ref[0].astype(F32)
        up = gu_ref[1].astype(F32)
        sg = _sigmoid(g)
        dgu_ref[0] = (dact * up * sg * (1.0 + g * (1.0 - sg))).astype(BF16)
        dgu_ref[1] = (dact * g * sg).astype(BF16)

    return _call(
        body, name=name, grid=(4, s // t),
        in_specs=[pl.BlockSpec((t, D_MODEL), lambda j, i: (i, 0)),
                  pl.BlockSpec((None, FF_BLK, D_MODEL), lambda j, i: (j, 0, 0)),
                  pl.BlockSpec((2, None, t, FF_BLK), lambda j, i: (0, j, i, 0))],
        out_specs=pl.BlockSpec((2, None, t, FF_BLK), lambda j, i: (0, j, i, 0)),
        out_shape=jax.ShapeDtypeStruct((2, 4, s, FF_BLK), BF16),
        operands=(dy, w_down, gu), semantics=("parallel", "parallel"), exchange=exchange)


def matmul_nt_acc(a, b, name, exchange=None):
    nk, d, n = b.shape
    s = a.shape[-2]
    t = min(ROW_TILE, s)
    by_columns = a.ndim == 2

    def body(a_ref, b_ref, o_ref):
        def a_blk(k):
            return a_ref[:, k * n:(k + 1) * n] if by_columns else a_ref[k]

        acc = _dot(a_blk(0), b_ref[0], NT)
        for k in range(1, nk):
            acc = acc + _dot(a_blk(k), b_ref[k], NT)
        o_ref[...] = acc

    a_spec = _row_spec(t, nk * n) if by_columns else pl.BlockSpec((nk, t, n), lambda i: (0, i, 0))
    return _call(
        body, name=name, grid=(s // t,),
        in_specs=[a_spec, pl.BlockSpec((nk, d, n), lambda i: (0, 0, 0))],
        out_specs=pl.BlockSpec((t, d), lambda i: (i, 0)),
        out_shape=jax.ShapeDtypeStruct((s, d), F32),
        operands=(a, b), semantics=("parallel",), exchange=exchange)


def matmul_tn(a, b, name, group=(1, 1), b_cols=None, exchange=None):
    ja, s, m = a.shape
    by_columns = b.ndim == 2
    jb, n = (b.shape[1] // b_cols, b_cols) if by_columns else (b.shape[0], b.shape[2])
    ga, gb = group
    t = min(ACC_TILE, s)
    nk = s // t

    def body(a_ref, b_ref, o_ref, acc_ref):
        k = pl.program_id(2)

        @pl.when(k == 0)
        def _():
            acc_ref[...] = jnp.zeros_like(acc_ref)

        for p in range(ga):
            for q in range(gb):
                b_blk = b_ref[:, q * n:(q + 1) * n] if by_columns else b_ref[q]
                acc_ref[p, q] += _dot(a_ref[p], b_blk, TN)

        @pl.when(k == nk - 1)
        def _():
            o_ref[...] = acc_ref[...].astype(BF16)

    return _call(
        body, name=name, grid=(ja // ga, jb // gb, nk),
        in_specs=[pl.BlockSpec((ga, t, m), lambda p, q, k: (p, k, 0)),
                  pl.BlockSpec((t, gb * n), lambda p, q, k: (k, q)) if by_columns
                  else pl.BlockSpec((gb, t, n), lambda p, q, k: (q, k, 0))],
        out_specs=pl.BlockSpec((ga, gb, m, n), lambda p, q, k: (p, q, 0, 0)),
        out_shape=jax.ShapeDtypeStruct((ja, jb, m, n), BF16),
        operands=(a, b), scratch_shapes=[pltpu.VMEM((ga, gb, m, n), F32)],
        semantics=("parallel", "parallel", "arbitrary"), exchange=exchange)


def mix_in_proj(u, w_mix, name, exchange=None):
    s = u.shape[0]
    t = min(ROW_TILE, s)

    def body(u_ref, w_ref, o_ref):
        uv = u_ref[...]
        for j in range(N_DEV):
            o_ref[:, j * MIX_BLK:(j + 1) * MIX_BLK] = _dot(uv, w_ref[j]).astype(BF16)

    return _call(
        body, name=name, grid=(s // t,),
        in_specs=[_row_spec(t), pl.BlockSpec((N_DEV, D_MODEL, MIX_BLK), lambda i: (0, 0, 0))],
        out_specs=_row_spec(t, MIX_W),
        out_shape=jax.ShapeDtypeStruct((s, MIX_W), BF16),
        operands=(u, w_mix), semantics=("parallel",), exchange=exchange)


def _conv_taps(cc_ref, cx_ref, s):
    v = cc_ref[...].astype(F32) * cx_ref[...].astype(F32)
    tok = lax.broadcasted_iota(jnp.int32, v.shape, 0)
    v1 = jnp.where(tok >= 1, pltpu.roll(v, 1, 0), 0.0)
    v2 = jnp.where(tok >= 2, pltpu.roll(v, 2, 0), 0.0)
    return v, v1, v2, tok


def _proj_cols(s, first):
    return pl.BlockSpec((s, 128), lambda j: (0, first + j))


def short_conv(proj, conv_w, name):
    s = proj.shape[0]

    def body(cb_ref, cc_ref, cx_ref, w_ref, o_ref):
        v, v1, v2, _ = _conv_taps(cc_ref, cx_ref, s)
        y = w_ref[0:1, :] * v2 + w_ref[1:2, :] * v1 + w_ref[2:3, :] * v
        o_ref[...] = (cb_ref[...].astype(F32) * y).astype(BF16)

    return pl.pallas_call(
        body, name=name, grid=(CONV_W // 128,),
        in_specs=[_proj_cols(s, 0), _proj_cols(s, 4), _proj_cols(s, 8),
                  pl.BlockSpec((3, 128), lambda j: (0, j))],
        out_specs=pl.BlockSpec((s, 128), lambda j: (0, j)),
        out_shape=jax.ShapeDtypeStruct((s, CONV_W), BF16),
        compiler_params=_params("parallel"),
    )(proj, proj, proj, conv_w)


def short_conv_bwd(dsa, proj, conv_w, name):
    s = proj.shape[0]

    def body(dsa_ref, cb_ref, cc_ref, cx_ref, w_ref, dcb_ref, dcc_ref, dcx_ref, dw_ref):
        v, v1, v2, tok = _conv_taps(cc_ref, cx_ref, s)
        w0, w1, w2 = w_ref[0:1, :], w_ref[1:2, :], w_ref[2:3, :]
        y = w0 * v2 + w1 * v1 + w2 * v
        dsa_v = dsa_ref[...].astype(F32)
        dcb_ref[...] = (dsa_v * y).astype(BF16)
        dy = dsa_v * cb_ref[...].astype(F32)
        dw_ref[0:1, :] = jnp.sum(dy * v2, axis=0, keepdims=True)
        dw_ref[1:2, :] = jnp.sum(dy * v1, axis=0, keepdims=True)
        dw_ref[2:3, :] = jnp.sum(dy * v, axis=0, keepdims=True)
        dy1 = jnp.where(tok < s - 1, pltpu.roll(dy, s - 1, 0), 0.0)
        dy2 = jnp.where(tok < s - 2, pltpu.roll(dy, s - 2, 0), 0.0)
        dv = w2 * dy + w1 * dy1 + w0 * dy2
        dcc_ref[...] = (dv * cx_ref[...].astype(F32)).astype(BF16)
        dcx_ref[...] = (dv * cc_ref[...].astype(F32)).astype(BF16)

    col = pl.BlockSpec((s, 128), lambda j: (0, j))
    act = jax.ShapeDtypeStruct((s, CONV_W), BF16)
    return pl.pallas_call(
        body, name=name, grid=(CONV_W // 128,),
        in_specs=[col, _proj_cols(s, 0), _proj_cols(s, 4), _proj_cols(s, 8),
                  pl.BlockSpec((3, 128), lambda j: (0, j))],
        out_specs=[col, col, col, pl.BlockSpec((3, 128), lambda j: (0, j))],
        out_shape=[act, act, act, jax.ShapeDtypeStruct((3, CONV_W), F32)],
        compiler_params=_params("parallel"),
    )(dsa, proj, proj, proj, conv_w)


def _gate_specs(t):
    return [pl.BlockSpec((t, D_MODEL), lambda i: (i, 3)), pl.BlockSpec((t, D_MODEL), lambda i: (i, 4))]


def merge_forward(sa, o, proj, w_co, w_ao, b_merge, name, exchange=None):
    s = sa.shape[0]
    t = min(ROW_TILE, s)

    def body(sa_ref, o_ref, ga_ref, gb_ref, wco_ref, wao_ref, bm_ref, mg_ref, ya_ref, yb_ref):
        ya = _dot(sa_ref[...], wco_ref[...])
        yb = _dot(o_ref[...], wao_ref[...])
        sga = _sigmoid(ga_ref[...].astype(F32) + bm_ref[0:1, :])
        sgb = _sigmoid(gb_ref[...].astype(F32) + bm_ref[1:2, :])
        mg_ref[...] = (sga * ya + sgb * yb).astype(BF16)
        ya_ref[...] = ya.astype(BF16)
        yb_ref[...] = yb.astype(BF16)

    act = jax.ShapeDtypeStruct((s, D_MODEL), BF16)
    return _call(
        body, name=name, grid=(s // t,),
        in_specs=[_row_spec(t, CONV_W), _row_spec(t, ATTN_W)] + _gate_specs(t)
        + [_vec_spec(CONV_W), _vec_spec(ATTN_W), _vec_spec(2)],
        out_specs=[_row_spec(t)] * 3, out_shape=[act, act, act],
        operands=(sa, o, proj, proj, w_co, w_ao, b_merge), semantics=("parallel",), exchange=exchange)


def merge_backward(dy, w_out, proj, ya, yb, b_merge, name, exchange=None):
    s = dy.shape[0]
    t = min(ROW_TILE, s)

    def body(dy_ref, w_ref, ga_ref, gb_ref, ya_ref, yb_ref, bm_ref,
             dya_ref, dyb_ref, dga_ref, dgb_ref, dbm_ref):
        @pl.when(pl.program_id(0) == 0)
        def _():
            dbm_ref[...] = jnp.zeros_like(dbm_ref)

        dmg = _dot(dy_ref[...], w_ref[...], NT)
        sga = _sigmoid(ga_ref[...].astype(F32) + bm_ref[0:1, :])
        sgb = _sigmoid(gb_ref[...].astype(F32) + bm_ref[1:2, :])
        dya_ref[...] = (dmg * sga).astype(BF16)
        dyb_ref[...] = (dmg * sgb).astype(BF16)
        dga = dmg * ya_ref[...].astype(F32) * sga * (1.0 - sga)
        dgb = dmg * yb_ref[...].astype(F32) * sgb * (1.0 - sgb)
        dga_ref[...] = dga.astype(BF16)
        dgb_ref[...] = dgb.astype(BF16)
        dbm_ref[0:1, :] += jnp.sum(dga, axis=0, keepdims=True)
        dbm_ref[1:2, :] += jnp.sum(dgb, axis=0, keepdims=True)

    act = jax.ShapeDtypeStruct((s, D_MODEL), BF16)
    return _call(
        body, name=name, grid=(s // t,),
        in_specs=[_row_spec(t), _vec_spec(D_MODEL)] + _gate_specs(t)
        + [_row_spec(t), _row_spec(t), _vec_spec(2)],
        out_specs=[_row_spec(t)] * 4 + [_vec_spec(2)],
        out_shape=[act] * 4 + [jax.ShapeDtypeStruct((2, D_MODEL), F32)],
        operands=(dy, w_out, proj, proj, ya, yb, b_merge), semantics=("arbitrary",), exchange=exchange)


def out_proj_bwd(dya, dyb, w_co, w_ao, name):
    s = dya.shape[0]
    t = min(ROW_TILE, s)

    def body(dya_ref, dyb_ref, wco_ref, wao_ref, dsa_ref, do_ref):
        dsa_ref[...] = _dot(dya_ref[...], wco_ref[...], NT).astype(BF16)
        do_ref[...] = _dot(dyb_ref[...], wao_ref[...], NT).astype(BF16)

    return pl.pallas_call(
        body, name=name, grid=(s // t,),
        in_specs=[_row_spec(t), _row_spec(t), _vec_spec(CONV_W), _vec_spec(ATTN_W)],
        out_specs=[_row_spec(t, CONV_W), _row_spec(t, ATTN_W)],
        out_shape=[jax.ShapeDtypeStruct((s, CONV_W), BF16), jax.ShapeDtypeStruct((s, ATTN_W), BF16)],
        compiler_params=_params("parallel"),
    )(dya, dyb, w_co, w_ao)


ATT_HEADS = 4
ATT_LANES = ATT_HEADS * HEAD_DIM
ATT_UNDERFLOW = 110.0


def _softplus(z):
    return jnp.maximum(z, 0.0) + jnp.log(1.0 + jnp.exp(-jnp.abs(z)))


def _head_masks(rows):
    lane = lax.broadcasted_iota(jnp.int32, (rows, ATT_LANES), 1)
    return [(lane >= h * HEAD_DIM) & (lane < (h + 1) * HEAD_DIM) for h in range(ATT_HEADS)]


def _per_head(x, masks):
    return [jnp.where(m, x, jnp.zeros_like(x)) for m in masks]


def _att_specs(s, blk):
    first = {"q": 3 * CONV_W // ATT_LANES, "k": (3 * CONV_W + ATTN_W) // ATT_LANES,
             "v": (3 * CONV_W + 2 * ATTN_W) // ATT_LANES}
    return [pl.BlockSpec((blk, ATT_LANES), lambda h, i: (i, first["q"] + h)),
            pl.BlockSpec((s, ATT_LANES), lambda h, i: (0, first["k"] + h)),
            pl.BlockSpec((s, ATT_LANES), lambda h, i: (0, first["v"] + h))]


def _head_norms(x, masks):
    sq = jnp.square(x.astype(F32))
    return [jnp.sum(jnp.where(m, sq, 0.0), axis=1, keepdims=True) for m in masks]


def stick_breaking_fwd(proj, name, exchange=None):
    s = proj.shape[0]
    blk = ATT_BLK
    nq = s // blk

    def body(q_ref, k_ref, v_ref, o_ref, tot_ref, first_ref, kmax_ref):
        i = pl.program_id(1)
        row = lax.broadcasted_iota(jnp.int32, (blk, blk), 0)
        col = lax.broadcasted_iota(jnp.int32, (blk, blk), 1)
        tri = (row >= col).astype(BF16)
        causal = col < row
        masks = _head_masks(blk)
        q_all = q_ref[...] * ATTN_SCALE
        qs = _per_head(q_all, masks)

        @pl.when(i == 0)
        def _():
            def longest(n, best):
                norms = _head_norms(k_ref[pl.ds(pl.multiple_of(n * blk, blk), blk), :], masks)
                return tuple(jnp.maximum(b, v) for b, v in zip(best, norms))

            best = lax.fori_loop(0, nq, longest, tuple(jnp.zeros((blk, 1), F32) for _ in range(ATT_HEADS)))
            for h in range(ATT_HEADS):
                kmax_ref[h] = jnp.sqrt(jnp.max(best[h], axis=0, keepdims=True))

        needed = [jnp.sqrt(n) * kmax_ref[h] + ATT_UNDERFLOW for h, n in enumerate(_head_norms(q_all, masks))]

        def finished(laters):
            slack = laters[0] - needed[0]
            for h in range(1, ATT_HEADS):
                slack = jnp.minimum(slack, laters[h] - needed[h])
            return (jnp.min(slack) >= 0.0).astype(jnp.int32)

        def step(j, carry, diagonal):
            laters, acc = carry
            rows = pl.ds(pl.multiple_of(j * blk, blk), blk)
            kb = k_ref[rows, :]
            probs, new_laters = [], []
            for h in range(ATT_HEADS):
                z = _dot(qs[h], kb, NT)
                sp = _softplus(z)
                if diagonal:
                    sp = jnp.where(causal, sp, 0.0)
                a = jnp.exp(z - (_dot(sp.astype(BF16), tri) + laters[h]))
                if diagonal:
                    a = jnp.where(causal, a, 0.0)
                probs.append(a.astype(BF16))
                new_laters.append(laters[h] + jnp.sum(sp, axis=1, keepdims=True))
            v_heads = jnp.concatenate(_per_head(v_ref[rows, :], masks), axis=0)
            acc = acc + _dot(jnp.concatenate(probs, axis=1), v_heads)
            return tuple(new_laters), acc

        carry = (tuple(jnp.zeros((blk, 1), F32) for _ in range(ATT_HEADS)), jnp.zeros((blk, ATT_LANES), F32))
        laters, acc = step(i, carry, True)

        def further(state):
            n, _, laters, acc = state
            laters, acc = step(i - 1 - n, (laters, acc), False)
            return n + 1, finished(laters), laters, acc

        walked, _, laters, acc = lax.while_loop(
            lambda state: jnp.logical_and(state[0] < i, state[1] == 0), further,
            (jnp.int32(0), finished(laters), laters, acc))
        o_ref[...] = acc.astype(BF16)
        tot = jnp.zeros((blk, ATT_LANES), F32)
        for h in range(ATT_HEADS):
            tot = jnp.where(masks[h], laters[h], tot)
        tot_ref[...] = tot
        first_ref[...] = jnp.full(first_ref.shape, i - walked, jnp.int32).astype(F32)

    out_spec = pl.BlockSpec((blk, ATT_LANES), lambda h, i: (i, h))
    groups = N_HEADS // ATT_HEADS
    return _call(
        body, name=name, grid=(groups, nq),
        in_specs=_att_specs(s, blk),
        out_specs=[out_spec, out_spec, pl.BlockSpec((None, None, 8, 128), lambda h, i: (h, i, 0, 0))],
        out_shape=[jax.ShapeDtypeStruct((s, ATTN_W), BF16), jax.ShapeDtypeStruct((s, ATTN_W), F32),
                   jax.ShapeDtypeStruct((groups, nq, 8, 128), F32)],
        operands=(proj, proj, proj), scratch_shapes=[pltpu.VMEM((ATT_HEADS, 1, 1), F32)],
        semantics=("parallel", "arbitrary"), exchange=exchange)


def stick_breaking_bwd(proj, do, tot, first, name, exchange=None):
    s = proj.shape[0]
    blk = ATT_BLK
    nq = s // blk

    def body(q_ref, k_ref, v_ref, do_ref, tot_ref, first_ref, dq_ref, dk_ref, dv_ref):
        i = pl.program_id(1)
        start = jnp.clip(jnp.max(first_ref[...]).astype(jnp.int32), 0, i)

        @pl.when(i == 0)
        def _():
            dk_ref[...] = jnp.zeros_like(dk_ref)
            dv_ref[...] = jnp.zeros_like(dv_ref)

        row = lax.broadcasted_iota(jnp.int32, (blk, blk), 0)
        col = lax.broadcasted_iota(jnp.int32, (blk, blk), 1)
        before = (row < col).astype(BF16)
        upto = (row <= col).astype(BF16)
        causal = col < row
        masks = _head_masks(blk)
        qs = _per_head(q_ref[...] * ATTN_SCALE, masks)
        dos = _per_head(do_ref[...], masks)
        q_heads = jnp.concatenate(qs, axis=0)
        do_heads = jnp.concatenate(dos, axis=0)
        tot_all = tot_ref[...]
        totals = [jnp.max(jnp.where(m, tot_all, 0.0), axis=1, keepdims=True) for m in masks]

        def step(j, carry, diagonal):
            earliers, g_sums, dq = carry
            rows = pl.ds(pl.multiple_of(j * blk, blk), blk)
            kb = k_ref[rows, :]
            vb = v_ref[rows, :]
            probs, dzs, new_earliers, new_g_sums = [], [], [], []
            for h in range(ATT_HEADS):
                z = _dot(qs[h], kb, NT)
                sp = _softplus(z)
                if diagonal:
                    sp = jnp.where(causal, sp, 0.0)
                c = (totals[h] - earliers[h]) - _dot(sp.astype(BF16), before)
                a = jnp.exp(z - c)
                if diagonal:
                    a = jnp.where(causal, a, 0.0)
                g = a * _dot(dos[h], vb, NT)
                f = g_sums[h] + _dot(g.astype(BF16), upto)
                dz = g - jnp.exp(z - sp) * f
                if diagonal:
                    dz = jnp.where(causal, dz, 0.0)
                probs.append(a.astype(BF16))
                dzs.append(dz.astype(BF16))
                new_earliers.append(earliers[h] + jnp.sum(sp, axis=1, keepdims=True))
                new_g_sums.append(g_sums[h] + jnp.sum(g, axis=1, keepdims=True))
            k_heads = jnp.concatenate(_per_head(kb, masks), axis=0)
            dq = dq + _dot(jnp.concatenate(dzs, axis=1), k_heads)
            dk_ref[rows, :] += _dot(jnp.concatenate(dzs, axis=0), q_heads, TN)
            dv_ref[rows, :] += _dot(jnp.concatenate(probs, axis=0), do_heads, TN)
            return tuple(new_earliers), tuple(new_g_sums), dq

        zeros = tuple(jnp.zeros((blk, 1), F32) for _ in range(ATT_HEADS))
        carry = (zeros, zeros, jnp.zeros((blk, ATT_LANES), F32))
        carry = lax.fori_loop(start, i, lambda j, c: step(j, c, False), carry)
        dq = step(i, carry, True)[2]
        dq_ref[...] = (dq * ATTN_SCALE).astype(BF16)

    blk_spec = pl.BlockSpec((blk, ATT_LANES), lambda h, i: (i, h))
    full_spec = pl.BlockSpec((s, ATT_LANES), lambda h, i: (0, h))
    return _call(
        body, name=name, grid=(N_HEADS // ATT_HEADS, nq),
        in_specs=_att_specs(s, blk) + [blk_spec, blk_spec,
                                       pl.BlockSpec((None, None, 8, 128), lambda h, i: (h, i, 0, 0))],
        out_specs=[blk_spec, full_spec, full_spec],
        out_shape=[jax.ShapeDtypeStruct((s, ATTN_W), BF16), jax.ShapeDtypeStruct((s, ATTN_W), F32),
                   jax.ShapeDtypeStruct((s, ATTN_W), F32)],
        operands=(proj, proj, proj, do, tot, first), semantics=("parallel", "arbitrary"), exchange=exchange)


def adamw(w, m, v, parts, name):
    r, c = w.shape
    p = parts.shape[0]
    t = r
    for cand in (256, 176):
        if r % cand == 0 and r > cand:
            t = cand
            break

    def body(w_ref, m_ref, v_ref, p_ref, g_ref, d_ref, mo_ref, vo_ref):
        g = p_ref[0].astype(F32)
        for n in range(1, p):
            g = g + p_ref[n].astype(F32)
        m_new = ADAM_B1 * m_ref[...] + (1.0 - ADAM_B1) * g
        v_new = ADAM_B2 * v_ref[...] + (1.0 - ADAM_B2) * (g * g)
        m_hat = m_new / ADAM_BC1
        v_hat = v_new / ADAM_BC2
        g_ref[...] = g
        d_ref[...] = -ADAM_LR * (m_hat / (jnp.sqrt(v_hat) + ADAM_EPS) + ADAM_WD * w_ref[...])
        mo_ref[...] = m_new
        vo_ref[...] = v_new

    spec = pl.BlockSpec((t, c), lambda i: (i, 0))
    out = jax.ShapeDtypeStruct((r, c), F32)
    return pl.pallas_call(
        body, name=name, grid=(r // t,),
        in_specs=[spec, spec, spec, pl.BlockSpec((p, t, c), lambda i: (0, i, 0))],
        out_specs=[spec] * 4, out_shape=[out] * 4,
        compiler_params=_params("parallel"),
    )(w, m, v, parts)


def kernel(x, c, w_ada, b_ada, norm1_g, ffn1_w_gu, ffn1_w_down, norm2_g, w_mix_in, b_merge, conv_w, w_conv_out, w_attn_out, w_out, norm3_g, ffn2_w_gu, ffn2_w_down, final_g, loss_target, m_w_ada, m_b_ada, m_norm1_g, m_ffn1_w_gu, m_ffn1_w_down, m_norm2_g, m_w_mix_in, m_b_merge, m_conv_w, m_w_conv_out, m_w_attn_out, m_w_out, m_norm3_g, m_ffn2_w_gu, m_ffn2_w_down, m_final_g, v_w_ada, v_b_ada, v_norm1_g, v_ffn1_w_gu, v_ffn1_w_down, v_norm2_g, v_w_mix_in, v_b_merge, v_conv_w, v_w_conv_out, v_w_attn_out, v_w_out, v_norm3_g, v_ffn2_w_gu, v_ffn2_w_down, v_final_g):
    s = x.shape[1]
    me = 4 * lax.axis_index("x") + 2 * lax.axis_index("y") + lax.axis_index("c")
    x0 = x[0]
    target = loss_target[0]
    final_g2 = final_g.reshape(1, D_MODEL)

    def shard(w):
        return w[0].astype(BF16)

    def rows8(g):
        return g.reshape(N_DEV, -1, D_MODEL)

    got = run_exchange(gather_stage1([shard(ffn1_w_gu), shard(ffn1_w_down)]), "gather_ffn1_chips")
    wgu1, wd1 = run_exchange(gather_stage2(got), "gather_ffn1_cores")
    wd1 = wd1.reshape(4, FF_BLK, D_MODEL)

    small_in = jnp.concatenate([c.reshape(-1), b_merge.reshape(-1), conv_w.reshape(-1),
                                jnp.zeros((64,), F32)]).reshape(1, -1)
    small_all = all_gather_rows(small_in, "gather_small")[:, 0, :]
    c_all = small_all[:, :D_MODEL]
    bm_full = small_all[:, 1024:1280].reshape(8, 2, 128).transpose(1, 0, 2).reshape(2, D_MODEL)
    cw_full = small_all[:, 1280:1472].reshape(8, 3, 64).transpose(1, 0, 2).reshape(3, CONV_W)
    n_ada = w_ada.shape[2]
    b_cols = lax.dynamic_slice(b_ada, (0, me * n_ada), (1, n_ada))
    mod_part = ada_forward(c_all, w_ada[0], b_cols, "ada_forward")
    mod_all = all_gather_rows(mod_part, "gather_mod")
    mod = lax.dynamic_index_in_dim(mod_all, me, axis=1, keepdims=False).reshape(9, 1, D_MODEL)
    sh1, sc1, gt1, sh2, sc2, gt2, sh3, sc3, gt3 = [mod[n] for n in range(9)]

    u1 = norm_modulate(x0, norm1_g, sh1, sc1, "norm_mod_1")
    (gu1, act1), got = ffn_up(u1, wgu1, "ffn_up_1", exchange=gather_stage1([shard(w_mix_in)]))
    (x1, y1), (wmix, *got) = residual_matmul(
        act1, wd1, x0, gt1, 0.5, "ffn_down_1", exchange=merge_exchanges(
            gather_stage2(got), gather_stage1([shard(w_conv_out), shard(w_attn_out), shard(w_out)])))

    u2 = norm_modulate(x1, norm2_g, sh2, sc2, "norm_mod_2")
    proj, (wco, wao, wout) = mix_in_proj(u2, wmix, "mix_in", exchange=gather_stage2(got))
    wco = wco.transpose(1, 0, 2).reshape(CONV_W, D_MODEL)
    wao = wao.transpose(1, 0, 2).reshape(ATTN_W, D_MODEL)
    wout = wout.reshape(D_MODEL, D_MODEL)
    sa = short_conv(proj, cw_full, "short_conv")
    (o, tot, first), got = stick_breaking_fwd(proj, "attn_fwd",
                                       exchange=gather_stage1([shard(ffn2_w_gu), shard(ffn2_w_down)]))
    (merged, ya, yb), (wgu3, wd3) = merge_forward(sa, o, proj, wco, wao, bm_full, "merge",
                                                  exchange=gather_stage2(got))
    wd3 = wd3.reshape(4, FF_BLK, D_MODEL)
    x2, y2 = residual_matmul(merged[None], wout[None], x1, gt2, 1.0, "out_proj")

    u3 = norm_modulate(x2, norm3_g, sh3, sc3, "norm_mod_3")
    gu3, act3 = ffn_up(u3, wgu3, "ffn_up_3")
    x3, y3 = residual_matmul(act3, wd3, x2, gt3, 0.5, "ffn_down_3")

    dx3, dy3, dgt3, dfinal, sq = loss_head(x3, target, final_g2, gt3, y3, 0.5, "loss_head")
    dgu3 = ffn_dact(dy3, wd3, gu3, "ffn_dact_3").reshape(8, s, FF_BLK)
    g_wd3 = rows8(matmul_tn(act3, dy3[None], "grad_w_down_3", group=(4, 1)))
    du3 = matmul_nt_acc(dgu3, wgu3, "ffn_du_3")
    g_wgu3 = matmul_tn(u3[None], dgu3, "grad_w_gu_3", group=(1, 4)).reshape(8, D_MODEL, FF_BLK)
    dx2, dsh3, dsc3, dn3, dy2, dgt2 = norm_modulate_bwd(du3, x2, dx3, norm3_g, sc3, "norm_bwd_3",
                                                        prev=(gt2, y2, 1.0))

    (dya, dyb, dga, dgb, dbm), pairs = merge_backward(dy2, wout, proj, ya, yb, bm_full, "merge_bwd",
                                                      exchange=scatter_stage1([g_wgu3, g_wd3]))
    sums3 = [pair_sum(g_wgu3, pairs[0], "pair_sum_w_gu_3"), pair_sum(g_wd3, pairs[1], "pair_sum_w_down_3")]
    g_wout = rows8(matmul_tn(merged[None], dy2[None], "grad_w_out"))
    dsa, do = out_proj_bwd(dya, dyb, wco, wao, "out_proj_bwd")
    g_wco = matmul_tn(sa[None], dya[None], "grad_w_conv_out").reshape(CONV_W, N_DEV, 128).transpose(1, 0, 2)
    g_wao = matmul_tn(o[None], dyb[None], "grad_w_attn_out").reshape(ATTN_W, N_DEV, 128).transpose(1, 0, 2)
    dcb, dcc, dcx, dconv = short_conv_bwd(dsa, proj, cw_full, "short_conv_bwd")
    (dq, dk, dv), landed3 = stick_breaking_bwd(proj, do, tot, first, "attn_bwd", exchange=scatter_stage2(sums3))
    dproj = jnp.concatenate([dcb, dcc, dcx, dq, dk.astype(BF16), dv.astype(BF16), dga, dgb], axis=1)
    du2 = matmul_nt_acc(dproj, wmix, "mix_in_du")
    g_wmix = matmul_tn(u2[None], dproj, "grad_w_mix_in", group=(1, 4), b_cols=MIX_BLK).reshape(
        N_DEV, D_MODEL, MIX_BLK)
    mixer_grads = [g_wmix, g_wco, g_wao, g_wout]
    (dx1, dsh2, dsc2, dn2, dy1, dgt1), pairs = norm_modulate_bwd(
        du2, x1, dx2, norm2_g, sc2, "norm_bwd_2", prev=(gt1, y1, 0.5), exchange=scatter_stage1(mixer_grads))
    sums_mix = [pair_sum(g, p, f"pair_sum_mixer_{n}") for n, (g, p) in enumerate(zip(mixer_grads, pairs))]

    dgu1, landed_mix = ffn_dact(dy1, wd1, gu1, "ffn_dact_1", exchange=scatter_stage2(sums_mix[:1]))
    dgu1 = dgu1.reshape(8, s, FF_BLK)
    g_wgu1, landed_small = matmul_tn(u1[None], dgu1, "grad_w_gu_1", group=(1, 4),
                                     exchange=scatter_stage2(sums_mix[1:]))
    g_wgu1 = g_wgu1.reshape(8, D_MODEL, FF_BLK)
    g_wd1, pairs = matmul_tn(act1, dy1[None], "grad_w_down_1", group=(4, 1), exchange=scatter_stage1([g_wgu1]))
    g_wd1 = rows8(g_wd1)
    sum_gu1 = pair_sum(g_wgu1, pairs[0], "pair_sum_w_gu_1")
    du1, (landed_gu1, pair_d1) = matmul_nt_acc(
        dgu1, wgu1, "ffn_du_1", exchange=merge_exchanges(scatter_stage2([sum_gu1]), scatter_stage1([g_wd1])))
    sum_d1 = pair_sum(g_wd1, pair_d1, "pair_sum_w_down_1")
    (grad_x, dsh1, dsc1, dn1), landed_d1 = norm_modulate_bwd(du1, x0, dx1, norm1_g, sc1, "norm_bwd_1",
                                                            exchange=scatter_stage2([sum_d1]))

    loss_local = (0.5 / D_MODEL) * jnp.sum(sq)
    stats = jnp.concatenate(
        [v.reshape(-1) for v in (dsh1, dsc1, dgt1, dsh2, dsc2, dgt2, dsh3, dsc3, dgt3,
                                 dn1, dn2, dn3, dfinal, dbm, dconv)]
        + [jnp.broadcast_to(loss_local, (128,))]).reshape(1, -1)
    stats_all = all_gather_rows(stats, "gather_stats")
    n_mod = 9 * D_MODEL
    loss = jnp.sum(stats_all[:, 0, -1])
    dmod_all = stats_all[:, :, :n_mod]
    off = n_mod
    parts = {}
    for key in ("norm1_g", "norm2_g", "norm3_g", "final_g"):
        parts[key] = stats_all[:, :, off:off + D_MODEL]
        off += D_MODEL
    dbm_all = stats_all[:, 0, off:off + 2 * D_MODEL].reshape(N_DEV, 2, D_MODEL)
    off += 2 * D_MODEL
    dcw_all = stats_all[:, 0, off:off + 3 * CONV_W].reshape(N_DEV, 3, CONV_W)
    parts["b_merge"] = lax.dynamic_slice(dbm_all, (0, 0, me * 128), (N_DEV, 2, 128))
    parts["conv_w"] = lax.dynamic_slice(dcw_all, (0, 0, me * 64), (N_DEV, 3, 64))
    dmod_cols = lax.dynamic_slice(dmod_all[:, 0, :], (0, me * n_ada), (N_DEV, n_ada))
    parts["w_ada"] = ada_backward(c_all, dmod_cols, "ada_backward")[None]
    parts["b_ada"] = dmod_all
    parts["ffn2_w_gu"], parts["ffn2_w_down"] = landed3
    parts["w_mix_in"] = landed_mix[0]
    parts["w_conv_out"], parts["w_attn_out"], parts["w_out"] = landed_small
    parts["ffn1_w_gu"] = landed_gu1
    parts["ffn1_w_down"] = landed_d1[0]

    given = dict(w_ada=w_ada, b_ada=b_ada, norm1_g=norm1_g, ffn1_w_gu=ffn1_w_gu, ffn1_w_down=ffn1_w_down,
                 norm2_g=norm2_g, w_mix_in=w_mix_in, b_merge=b_merge, conv_w=conv_w, w_conv_out=w_conv_out,
                 w_attn_out=w_attn_out, w_out=w_out, norm3_g=norm3_g, ffn2_w_gu=ffn2_w_gu,
                 ffn2_w_down=ffn2_w_down, final_g=final_g)
    moments_m = dict(w_ada=m_w_ada, b_ada=m_b_ada, norm1_g=m_norm1_g, ffn1_w_gu=m_ffn1_w_gu,
                     ffn1_w_down=m_ffn1_w_down, norm2_g=m_norm2_g, w_mix_in=m_w_mix_in, b_merge=m_b_merge,
                     conv_w=m_conv_w, w_conv_out=m_w_conv_out, w_attn_out=m_w_attn_out, w_out=m_w_out,
                     norm3_g=m_norm3_g, ffn2_w_gu=m_ffn2_w_gu, ffn2_w_down=m_ffn2_w_down, final_g=m_final_g)
    moments_v = dict(w_ada=v_w_ada, b_ada=v_b_ada, norm1_g=v_norm1_g, ffn1_w_gu=v_ffn1_w_gu,
                     ffn1_w_down=v_ffn1_w_down, norm2_g=v_norm2_g, w_mix_in=v_w_mix_in, b_merge=v_b_merge,
                     conv_w=v_conv_w, w_conv_out=v_w_conv_out, w_attn_out=v_w_attn_out, w_out=v_w_out,
                     norm3_g=v_norm3_g, ffn2_w_gu=v_ffn2_w_gu, ffn2_w_down=v_ffn2_w_down, final_g=v_final_g)
    order = ["w_ada", "b_ada", "norm1_g", "ffn1_w_gu", "ffn1_w_down", "norm2_g", "w_mix_in", "b_merge",
             "conv_w", "w_conv_out", "w_attn_out", "w_out", "norm3_g", "ffn2_w_gu", "ffn2_w_down", "final_g"]
    grads, deltas, new_m, new_v = [], [], [], []
    for key in order:
        shape = given[key].shape
        shape2 = (1, shape[0]) if len(shape) == 1 else shape[-2:]
        outs = adamw(given[key].reshape(shape2), moments_m[key].reshape(shape2),
                     moments_v[key].reshape(shape2), parts[key], f"adamw_{key}")
        for dst, val in zip((grads, deltas, new_m, new_v), outs):
            dst.append(val.reshape(shape))

    return (loss, grad_x[None], *grads, *deltas, *new_m, *new_v)
```

```python
import functools
from typing import Callable, NamedTuple

import jax
import jax.numpy as jnp
from jax import lax
from jax.experimental import pallas as pl
from jax.experimental.pallas import tpu as pltpu

F32 = jnp.float32
BF16 = jnp.bfloat16
MESH = pl.DeviceIdType.MESH
ANY = pl.BlockSpec(memory_space=pl.ANY)

N_DEV = 8
D_MODEL = 1024
D_FF = 2816
FF_BLK = D_FF // 4
N_HEADS = 8
HEAD_DIM = 64
CONV_W = 512
ATTN_W = 512
MIX_W = 3 * CONV_W + 3 * ATTN_W + 2 * D_MODEL
MIX_BLK = MIX_W // N_DEV
EPS = 1e-6
ATTN_SCALE = HEAD_DIM ** -0.5

ADAM_LR = 0.001
ADAM_B1 = 0.9
ADAM_B2 = 0.999
ADAM_EPS = 1e-08
ADAM_WD = 0.01
ADAM_STEP = 10
ADAM_BC1 = 1.0 - ADAM_B1 ** ADAM_STEP
ADAM_BC2 = 1.0 - ADAM_B2 ** ADAM_STEP

VMEM_LIMIT = 56 * 1024 * 1024
ROW_TILE = 512
ACC_TILE = 1024
ELT_TILE = 256
ATT_BLK = 256

NN = (((1,), (0,)), ((), ()))
NT = (((1,), (1,)), ((), ()))
TN = (((0,), (0,)), ((), ()))


def _dot(a, b, dims=NN):
    return lax.dot_general(a, b, dims, preferred_element_type=F32)


def _params(*sem):
    return pltpu.CompilerParams(dimension_semantics=sem, vmem_limit_bytes=VMEM_LIMIT)


def _sigmoid(x):
    return 1.0 / (1.0 + jnp.exp(-x))


def _me():
    x, y, c = lax.axis_index("x"), lax.axis_index("y"), lax.axis_index("c")
    return x, y, c, 4 * x + 2 * y + c


def _peer(k):
    x, y, c, _ = _me()
    px = 1 - x if (k >> 2) & 1 else x
    py = 1 - y if (k >> 1) & 1 else y
    pc = 1 - c if k & 1 else c
    return (px, py, pc), 4 * px + 2 * py + pc


class Exchange(NamedTuple):
    operands: tuple
    out_shapes: tuple
    aliases: dict
    n_remote: int
    n_local: int
    copies: Callable


CHIP_FLIPS = (2, 4, 6)
SIBLING = 1


def _remote(src, dst, send_sems, recv_sems, n, peer):
    return pltpu.make_async_remote_copy(src_ref=src, dst_ref=dst, send_sem=send_sems.at[n], recv_sem=recv_sems.at[n],
                                        device_id=peer, device_id_type=MESH)


def gather_stage1(shards):
    n = len(shards)
    rels = (SIBLING,) + CHIP_FLIPS

    def copies(ins, outs, send_sems, recv_sems, local_sems, rb, lb):
        _, _, _, me = _me()
        cps = []
        for w in range(n):
            cps.append(pltpu.make_async_copy(ins[w], outs[w].at[me], local_sems.at[lb + w]))
            for a, k in enumerate(rels):
                peer, _ = _peer(k)
                cps.append(_remote(ins[w], outs[w].at[me], send_sems, recv_sems, rb + len(rels) * w + a, peer))
        return cps

    shapes = tuple(jax.ShapeDtypeStruct((N_DEV,) + s.shape, s.dtype) for s in shards)
    return Exchange(tuple(shards), shapes, {}, len(rels) * n, n, copies)


def gather_stage2(fulls):
    n = len(fulls)

    def copies(ins, outs, send_sems, recv_sems, local_sems, rb, lb):
        sibling, _ = _peer(SIBLING)
        cps = []
        for w in range(n):
            for a, k in enumerate(CHIP_FLIPS):
                _, blk = _peer(k)
                cps.append(_remote(outs[w].at[blk], outs[w].at[blk], send_sems, recv_sems, rb + 3 * w + a, sibling))
        return cps

    shapes = tuple(jax.ShapeDtypeStruct(f.shape, f.dtype) for f in fulls)
    return Exchange(tuple(fulls), shapes, {w: w for w in range(n)}, 3 * n, 0, copies)


def scatter_stage1(fulls):
    n = len(fulls)

    def copies(ins, outs, send_sems, recv_sems, local_sems, rb, lb):
        _, _, c, _ = _me()
        sibling, _ = _peer(SIBLING)
        cps = []
        for w in range(n):
            for q in range(4):
                cps.append(_remote(ins[w].at[2 * q + (1 - c)], outs[w].at[q], send_sems, recv_sems, rb + 4 * w + q, sibling))
        return cps

    shapes = tuple(jax.ShapeDtypeStruct((4,) + f.shape[1:], f.dtype) for f in fulls)
    return Exchange(tuple(fulls), shapes, {}, 4 * n, 0, copies)


def scatter_stage2(sums):
    n = len(sums)

    def copies(ins, outs, send_sems, recv_sems, local_sems, rb, lb):
        x, y, _, _ = _me()
        mine = 2 * x + y
        cps = []
        for w in range(n):
            cps.append(pltpu.make_async_copy(ins[w].at[mine], outs[w].at[mine], local_sems.at[lb + w]))
            for a, k in enumerate(CHIP_FLIPS):
                peer, _ = _peer(k)
                cps.append(_remote(ins[w].at[2 * peer[0] + peer[1]], outs[w].at[mine], send_sems, recv_sems,
                                   rb + 3 * w + a, peer))
        return cps

    shapes = tuple(jax.ShapeDtypeStruct(s.shape, s.dtype) for s in sums)
    return Exchange(tuple(sums), shapes, {}, 3 * n, n, copies)


def merge_exchanges(a, b):
    na_in, na_out = len(a.operands), len(a.out_shapes)

    def copies(ins, outs, send_sems, recv_sems, local_sems, rb, lb):
        return (a.copies(ins[:na_in], outs[:na_out], send_sems, recv_sems, local_sems, rb, lb)
                + b.copies(ins[na_in:], outs[na_out:], send_sems, recv_sems, local_sems, rb + a.n_remote, lb + a.n_local))

    aliases = dict(a.aliases)
    aliases.update({na_in + i: na_out + o for i, o in b.aliases.items()})
    return Exchange(a.operands + b.operands, a.out_shapes + b.out_shapes, aliases,
                    a.n_remote + b.n_remote, a.n_local + b.n_local, copies)


def _exchange_scratch(ex):
    return [pltpu.SemaphoreType.DMA((ex.n_remote,)), pltpu.SemaphoreType.DMA((ex.n_remote,)),
            pltpu.SemaphoreType.DMA((max(ex.n_local, 1),))]


def run_exchange(ex, name):
    n_in, n_out = len(ex.operands), len(ex.out_shapes)

    def body(*refs):
        cps = ex.copies(refs[:n_in], refs[n_in:n_in + n_out], *refs[n_in + n_out:], 0, 0)
        for cp in cps:
            cp.start()
        for cp in cps:
            cp.wait()

    return pl.pallas_call(
        body, name=name, out_shape=list(ex.out_shapes), in_specs=[ANY] * n_in, out_specs=[ANY] * n_out,
        scratch_shapes=_exchange_scratch(ex), input_output_aliases=dict(ex.aliases),
    )(*ex.operands)


def _call(body, *, name, grid, in_specs, out_specs, out_shape, operands, scratch_shapes=(), semantics=(),
          exchange=None):
    if exchange is None:
        return pl.pallas_call(
            body, name=name, grid=grid, in_specs=in_specs, out_specs=out_specs, out_shape=out_shape,
            scratch_shapes=list(scratch_shapes), compiler_params=_params(*semantics))(*operands)
    single = not isinstance(out_shape, (list, tuple))
    out_shapes = [out_shape] if single else list(out_shape)
    out_specs_l = [out_specs] if single else list(out_specs)
    n_in, n_out, n_scr = len(operands), len(out_shapes), len(scratch_shapes)
    x_in, x_out = len(exchange.operands), len(exchange.out_shapes)

    def hosted(*refs):
        ins, refs = refs[:n_in], refs[n_in:]
        xin, refs = refs[:x_in], refs[x_in:]
        outs, refs = refs[:n_out], refs[n_out:]
        xout, refs = refs[:x_out], refs[x_out:]
        scr, sems = refs[:n_scr], refs[n_scr:]
        first = functools.reduce(jnp.logical_and, [pl.program_id(a) == 0 for a in range(len(grid))])
        last = functools.reduce(jnp.logical_and, [pl.program_id(a) == g - 1 for a, g in enumerate(grid)])

        @pl.when(first)
        def _():
            for cp in exchange.copies(xin, xout, *sems, 0, 0):
                cp.start()

        body(*ins, *outs, *scr)

        @pl.when(last)
        def _():
            for cp in exchange.copies(xin, xout, *sems, 0, 0):
                cp.wait()

    res = pl.pallas_call(
        hosted, name=name, grid=grid,
        in_specs=list(in_specs) + [ANY] * x_in, out_specs=out_specs_l + [ANY] * x_out,
        out_shape=out_shapes + list(exchange.out_shapes),
        scratch_shapes=list(scratch_shapes) + _exchange_scratch(exchange),
        input_output_aliases={n_in + i: n_out + o for i, o in exchange.aliases.items()},
        compiler_params=_params(*(["arbitrary"] * len(grid))),
    )(*operands, *exchange.operands)
    outs, xouts = res[:n_out], res[n_out:]
    return (outs[0] if single else outs), xouts


def all_gather_rows(v, name):
    r, n = v.shape

    def body(v_ref, out_ref, send_sems, recv_sems):
        _, _, _, me = _me()
        out_ref[me] = v_ref[...]
        copies = []
        for k in range(1, N_DEV):
            peer, _ = _peer(k)
            copies.append(_remote(v_ref, out_ref.at[me], send_sems, recv_sems, k - 1, peer))
        for cp in copies:
            cp.start()
        for cp in copies:
            cp.wait()

    return pl.pallas_call(
        body, name=name,
        out_shape=jax.ShapeDtypeStruct((N_DEV, r, n), v.dtype),
        in_specs=[pl.BlockSpec(memory_space=pltpu.VMEM)],
        out_specs=pl.BlockSpec(memory_space=pltpu.VMEM),
        scratch_shapes=[pltpu.SemaphoreType.DMA((N_DEV - 1,)), pltpu.SemaphoreType.DMA((N_DEV - 1,))],
    )(v)


def pair_sum(full, pair, name):
    _, r, c = full.shape
    t = r
    core = lax.axis_index("c").astype(jnp.int32).reshape(1)

    def body(core_ref, f_ref, p_ref, o_ref):
        o_ref[...] = (f_ref[...].astype(F32) + p_ref[...].astype(F32)).astype(BF16)

    return pl.pallas_call(
        body, name=name,
        grid_spec=pltpu.PrefetchScalarGridSpec(
            num_scalar_prefetch=1, grid=(4, r // t),
            in_specs=[pl.BlockSpec((None, None, t, c), lambda q, i, core_ref: (q, core_ref[0], i, 0)),
                      pl.BlockSpec((None, t, c), lambda q, i, core_ref: (q, i, 0))],
            out_specs=pl.BlockSpec((None, t, c), lambda q, i, core_ref: (q, i, 0))),
        out_shape=jax.ShapeDtypeStruct((4, r, c), BF16),
        compiler_params=_params("parallel", "parallel"),
    )(core, full.reshape(4, 2, r, c), pair)


def ada_forward(c_all, w_ada, b_cols, name):
    n = w_ada.shape[1]

    def body(c_ref, w_ref, b_ref, o_ref):
        c = c_ref[...]
        act = c * _sigmoid(c)
        o_ref[...] = jnp.dot(act, w_ref[...], precision=lax.Precision.HIGHEST,
                             preferred_element_type=F32) + b_ref[...]

    return pl.pallas_call(
        body, name=name, out_shape=jax.ShapeDtypeStruct((N_DEV, n), F32),
        compiler_params=pltpu.CompilerParams(vmem_limit_bytes=VMEM_LIMIT),
    )(c_all, w_ada, b_cols)


def ada_backward(c_all, dmod_cols, name):
    n = dmod_cols.shape[1]

    def body(c_ref, d_ref, o_ref):
        c = c_ref[...]
        act = c * _sigmoid(c)
        o_ref[...] = lax.dot_general(act, d_ref[...], TN, precision=lax.Precision.HIGHEST,
                                     preferred_element_type=F32)

    return pl.pallas_call(
        body, name=name, out_shape=jax.ShapeDtypeStruct((D_MODEL, n), F32),
        compiler_params=pltpu.CompilerParams(vmem_limit_bytes=VMEM_LIMIT),
    )(c_all, dmod_cols)


def _row_spec(t, width=D_MODEL):
    return pl.BlockSpec((t, width), lambda i: (i, 0))


def _vec_spec(rows=1, width=D_MODEL):
    return pl.BlockSpec((rows, width), lambda i: (0, 0))


def norm_modulate(x, g, shift, scale, name):
    s = x.shape[0]
    t = min(ELT_TILE, s)

    def body(x_ref, g_ref, sh_ref, sc_ref, u_ref):
        xv = x_ref[...]
        r = lax.rsqrt(jnp.mean(xv * xv, axis=-1, keepdims=True) + EPS)
        a = (xv * r) * g_ref[...]
        u_ref[...] = (a * (1.0 + sc_ref[...]) + sh_ref[...]).astype(BF16)

    return pl.pallas_call(
        body, name=name, grid=(s // t,),
        in_specs=[_row_spec(t), _vec_spec(), _vec_spec(), _vec_spec()],
        out_specs=_row_spec(t),
        out_shape=jax.ShapeDtypeStruct((s, D_MODEL), BF16),
        compiler_params=_params("parallel"),
    )(x, g, shift, scale)


def loss_head(x, target, final_g, gate, y_prev, coef, name):
    s = x.shape[0]
    t = min(ELT_TILE, s)

    def body(x_ref, t_ref, fg_ref, gt_ref, y_ref, dx_ref, dy_ref, dgt_ref, dfg_ref, sq_ref):
        @pl.when(pl.program_id(0) == 0)
        def _():
            dgt_ref[...] = jnp.zeros_like(dgt_ref)
            dfg_ref[...] = jnp.zeros_like(dfg_ref)
            sq_ref[...] = jnp.zeros_like(sq_ref)

        xv = x_ref[...]
        fg = fg_ref[...]
        r = lax.rsqrt(jnp.mean(xv * xv, axis=-1, keepdims=True) + EPS)
        nrm = xv * r
        err = nrm * fg - t_ref[...]
        sq_ref[...] += jnp.sum(err * err, axis=0, keepdims=True)
        dout = err * (1.0 / D_MODEL)
        dfg_ref[...] += jnp.sum(dout * nrm, axis=0, keepdims=True)
        dn = dout * fg
        dx = r * (dn - nrm * jnp.mean(dn * nrm, axis=-1, keepdims=True))
        dx_ref[...] = dx
        dy_ref[...] = (coef * gt_ref[...] * dx).astype(BF16)
        dgt_ref[...] += coef * jnp.sum(dx * y_ref[...].astype(F32), axis=0, keepdims=True)

    vec = jax.ShapeDtypeStruct((1, D_MODEL), F32)
    return pl.pallas_call(
        body, name=name, grid=(s // t,),
        in_specs=[_row_spec(t), _row_spec(t), _vec_spec(), _vec_spec(), _row_spec(t)],
        out_specs=[_row_spec(t), _row_spec(t), _vec_spec(), _vec_spec(), _vec_spec()],
        out_shape=[jax.ShapeDtypeStruct((s, D_MODEL), F32), jax.ShapeDtypeStruct((s, D_MODEL), BF16),
                   vec, vec, vec],
        compiler_params=_params("arbitrary"),
    )(x, target, final_g, gate, y_prev)


def norm_modulate_bwd(du, x, dx_out, g, scale, name, prev=None, exchange=None):
    s = x.shape[0]
    t = min(ELT_TILE, s)
    has_prev = prev is not None

    def body(*refs):
        du_ref, x_ref, dxo_ref, g_ref, sc_ref = refs[:5]
        refs = refs[5:]
        if has_prev:
            gt_ref, y_ref = refs[:2]
            refs = refs[2:]
        dx_ref, dsh_ref, dsc_ref, dg_ref = refs[:4]

        @pl.when(pl.program_id(0) == 0)
        def _():
            dsh_ref[...] = jnp.zeros_like(dsh_ref)
            dsc_ref[...] = jnp.zeros_like(dsc_ref)
            dg_ref[...] = jnp.zeros_like(dg_ref)
            if has_prev:
                refs[5][...] = jnp.zeros_like(refs[5])

        xv = x_ref[...]
        duv = du_ref[...]
        gv = g_ref[...]
        r = lax.rsqrt(jnp.mean(xv * xv, axis=-1, keepdims=True) + EPS)
        nrm = xv * r
        a = nrm * gv
        dsh_ref[...] += jnp.sum(duv, axis=0, keepdims=True)
        dsc_ref[...] += jnp.sum(duv * a, axis=0, keepdims=True)
        da = duv * (1.0 + sc_ref[...])
        dg_ref[...] += jnp.sum(da * nrm, axis=0, keepdims=True)
        dn = da * gv
        dx = dxo_ref[...] + r * (dn - nrm * jnp.mean(dn * nrm, axis=-1, keepdims=True))
        dx_ref[...] = dx
        if has_prev:
            coef = prev[2]
            refs[4][...] = (coef * gt_ref[...] * dx).astype(BF16)
            refs[5][...] += coef * jnp.sum(dx * y_ref[...].astype(F32), axis=0, keepdims=True)

    vec = jax.ShapeDtypeStruct((1, D_MODEL), F32)
    operands = [du, x, dx_out, g, scale]
    in_specs = [_row_spec(t), _row_spec(t), _row_spec(t), _vec_spec(), _vec_spec()]
    out_specs = [_row_spec(t), _vec_spec(), _vec_spec(), _vec_spec()]
    out_shape = [jax.ShapeDtypeStruct((s, D_MODEL), F32), vec, vec, vec]
    if has_prev:
        operands += [prev[0], prev[1]]
        in_specs += [_vec_spec(), _row_spec(t)]
        out_specs += [_row_spec(t), _vec_spec()]
        out_shape += [jax.ShapeDtypeStruct((s, D_MODEL), BF16), vec]
    return _call(body, name=name, grid=(s // t,), in_specs=in_specs, out_specs=out_specs, out_shape=out_shape,
                 operands=operands, semantics=("arbitrary",), exchange=exchange)


def ffn_up(u, w_gu, name, exchange=None):
    s = u.shape[0]
    t = min(ROW_TILE, s)

    def body(u_ref, wg_ref, wu_ref, slope_ref, act_ref):
        uv = u_ref[...]
        g = _dot(uv, wg_ref[...], NT)
        up = _dot(uv, wu_ref[...], NT)
        sg = _sigmoid(g)
        silu = g * sg
        slope_ref[0] = (up * sg * (1.0 + g * (1.0 - sg))).astype(BF16)
        slope_ref[1] = silu.astype(BF16)
        act_ref[...] = (silu * up).astype(BF16)

    return _call(
        body, name=name, grid=(4, s // t),
        in_specs=[pl.BlockSpec((t, D_MODEL), lambda j, i: (i, 0)),
                  pl.BlockSpec((None, FF_BLK, D_MODEL), lambda j, i: (j, 0, 0)),
                  pl.BlockSpec((None, FF_BLK, D_MODEL), lambda j, i: (j + 4, 0, 0))],
        out_specs=[pl.BlockSpec((2, None, t, FF_BLK), lambda j, i: (0, j, i, 0)),
                   pl.BlockSpec((None, t, FF_BLK), lambda j, i: (j, i, 0))],
        out_shape=[jax.ShapeDtypeStruct((2, 4, s, FF_BLK), BF16),
                   jax.ShapeDtypeStruct((4, s, FF_BLK), BF16)],
        operands=(u, w_gu, w_gu), semantics=("parallel", "parallel"), exchange=exchange)


def residual_matmul(a, b, x, gate, coef, name, exchange=None):
    nk, s, kb = a.shape
    t = min(ROW_TILE, s)

    def body(a_ref, b_ref, x_ref, gt_ref, xo_ref, y_ref):
        y = _dot(a_ref[0], b_ref[0])
        for k in range(1, nk):
            y = y + _dot(a_ref[k], b_ref[k])
        y_ref[...] = y.astype(BF16)
        xo_ref[...] = x_ref[...] + coef * gt_ref[...] * y

    return _call(
        body, name=name, grid=(s // t,),
        in_specs=[pl.BlockSpec((nk, t, kb), lambda i: (0, i, 0)),
                  pl.BlockSpec((nk, kb, D_MODEL), lambda i: (0, 0, 0)),
                  _row_spec(t), _vec_spec()],
        out_specs=[_row_spec(t), _row_spec(t)],
        out_shape=[jax.ShapeDtypeStruct((s, D_MODEL), F32), jax.ShapeDtypeStruct((s, D_MODEL), BF16)],
        operands=(a, b, x, gate), semantics=("parallel",), exchange=exchange)


def ffn_dact(dy, w_down, slopes, name, exchange=None):
    s = dy.shape[0]
    t = min(ROW_TILE, s)

    def body(dy_ref, w_ref, slope_ref, dgu_ref):
        dact = _dot(dy_ref[...], w_ref[...], NT)
        dgu_ref[0] = (dact * slope_ref[0].astype(F32)).astype(BF16)
        dgu_ref[1] = (dact * slope_ref[1].astype(F32)).astype(BF16)

    return _call(
        body, name=name, grid=(4, s // t),
        in_specs=[pl.BlockSpec((t, D_MODEL), lambda j, i: (i, 0)),
                  pl.BlockSpec((None, FF_BLK, D_MODEL), lambda j, i: (j, 0, 0)),
                  pl.BlockSpec((2, None, t, FF_BLK), lambda j, i: (0, j, i, 0))],
        out_specs=pl.BlockSpec((2, None, t, FF_BLK), lambda j, i: (0, j, i, 0)),
        out_shape=jax.ShapeDtypeStruct((2, 4, s, FF_BLK), BF16),
        operands=(dy, w_down, slopes), semantics=("parallel", "parallel"), exchange=exchange)


def matmul_nt_acc(a, b, name, b_dims=NT, exchange=None):
    nk = b.shape[0]
    d, n = (b.shape[1], b.shape[2]) if b_dims == NT else (b.shape[2], b.shape[1])
    s = a.shape[-2]
    t = min(ROW_TILE, s)
    by_columns = a.ndim == 2

    def body(a_ref, b_ref, o_ref):
        def a_blk(k):
            return a_ref[:, k * n:(k + 1) * n] if by_columns else a_ref[k]

        acc = _dot(a_blk(0), b_ref[0], b_dims)
        for k in range(1, nk):
            acc = acc + _dot(a_blk(k), b_ref[k], b_dims)
        o_ref[...] = acc

    a_spec = _row_spec(t, nk * n) if by_columns else pl.BlockSpec((nk, t, n), lambda i: (0, i, 0))
    return _call(
        body, name=name, grid=(s // t,),
        in_specs=[a_spec, pl.BlockSpec(b.shape, lambda i: (0, 0, 0))],
        out_specs=pl.BlockSpec((t, d), lambda i: (i, 0)),
        out_shape=jax.ShapeDtypeStruct((s, d), F32),
        operands=(a, b), semantics=("parallel",), exchange=exchange)


def matmul_tn(a, b, name, group=(1, 1), b_cols=None, exchange=None):
    ja, s, m = a.shape
    by_columns = b.ndim == 2
    jb, n = (b.shape[1] // b_cols, b_cols) if by_columns else (b.shape[0], b.shape[2])
    ga, gb = group
    t = min(ACC_TILE, s)
    nk = s // t

    def body(a_ref, b_ref, o_ref, acc_ref):
        k = pl.program_id(2)

        @pl.when(k == 0)
        def _():
            acc_ref[...] = jnp.zeros_like(acc_ref)

        for p in range(ga):
            for q in range(gb):
                b_blk = b_ref[:, q * n:(q + 1) * n] if by_columns else b_ref[q]
                acc_ref[p, q] += _dot(a_ref[p], b_blk, TN)

        @pl.when(k == nk - 1)
        def _():
            o_ref[...] = acc_ref[...].astype(BF16)

    return _call(
        body, name=name, grid=(ja // ga, jb // gb, nk),
        in_specs=[pl.BlockSpec((ga, t, m), lambda p, q, k: (p, k, 0)),
                  pl.BlockSpec((t, gb * n), lambda p, q, k: (k, q)) if by_columns
                  else pl.BlockSpec((gb, t, n), lambda p, q, k: (q, k, 0))],
        out_specs=pl.BlockSpec((ga, gb, m, n), lambda p, q, k: (p, q, 0, 0)),
        out_shape=jax.ShapeDtypeStruct((ja, jb, m, n), BF16),
        operands=(a, b), scratch_shapes=[pltpu.VMEM((ga, gb, m, n), F32)],
        semantics=("parallel", "parallel", "arbitrary"), exchange=exchange)


def mix_in_proj(u, w_mix, name, exchange=None):
    s = u.shape[0]
    t = min(ROW_TILE, s)

    def body(u_ref, w_ref, o_ref):
        uv = u_ref[...]
        for j in range(N_DEV):
            o_ref[:, j * MIX_BLK:(j + 1) * MIX_BLK] = _dot(uv, w_ref[j]).astype(BF16)

    return _call(
        body, name=name, grid=(s // t,),
        in_specs=[_row_spec(t), pl.BlockSpec((N_DEV, D_MODEL, MIX_BLK), lambda i: (0, 0, 0))],
        out_specs=_row_spec(t, MIX_W),
        out_shape=jax.ShapeDtypeStruct((s, MIX_W), BF16),
        operands=(u, w_mix), semantics=("parallel",), exchange=exchange)


def _conv_taps(cc_ref, cx_ref, s):
    v = cc_ref[...].astype(F32) * cx_ref[...].astype(F32)
    tok = lax.broadcasted_iota(jnp.int32, v.shape, 0)
    v1 = jnp.where(tok >= 1, pltpu.roll(v, 1, 0), 0.0)
    v2 = jnp.where(tok >= 2, pltpu.roll(v, 2, 0), 0.0)
    return v, v1, v2, tok


def _proj_cols(s, first):
    return pl.BlockSpec((s, 128), lambda j: (0, first + j))


def short_conv(proj, conv_w, name):
    s = proj.shape[0]

    def body(cb_ref, cc_ref, cx_ref, w_ref, o_ref):
        v, v1, v2, _ = _conv_taps(cc_ref, cx_ref, s)
        y = w_ref[0:1, :] * v2 + w_ref[1:2, :] * v1 + w_ref[2:3, :] * v
        o_ref[...] = (cb_ref[...].astype(F32) * y).astype(BF16)

    return pl.pallas_call(
        body, name=name, grid=(CONV_W // 128,),
        in_specs=[_proj_cols(s, 0), _proj_cols(s, 4), _proj_cols(s, 8),
                  pl.BlockSpec((3, 128), lambda j: (0, j))],
        out_specs=pl.BlockSpec((s, 128), lambda j: (0, j)),
        out_shape=jax.ShapeDtypeStruct((s, CONV_W), BF16),
        compiler_params=_params("parallel"),
    )(proj, proj, proj, conv_w)


def short_conv_bwd(dsa, proj, conv_w, name):
    s = proj.shape[0]

    def body(dsa_ref, cb_ref, cc_ref, cx_ref, w_ref, dcb_ref, dcc_ref, dcx_ref, dw_ref):
        v, v1, v2, tok = _conv_taps(cc_ref, cx_ref, s)
        w0, w1, w2 = w_ref[0:1, :], w_ref[1:2, :], w_ref[2:3, :]
        y = w0 * v2 + w1 * v1 + w2 * v
        dsa_v = dsa_ref[...].astype(F32)
        dcb_ref[...] = (dsa_v * y).astype(BF16)
        dy = dsa_v * cb_ref[...].astype(F32)
        dw_ref[0:1, :] = jnp.sum(dy * v2, axis=0, keepdims=True)
        dw_ref[1:2, :] = jnp.sum(dy * v1, axis=0, keepdims=True)
        dw_ref[2:3, :] = jnp.sum(dy * v, axis=0, keepdims=True)
        dy1 = jnp.where(tok < s - 1, pltpu.roll(dy, s - 1, 0), 0.0)
        dy2 = jnp.where(tok < s - 2, pltpu.roll(dy, s - 2, 0), 0.0)
        dv = w2 * dy + w1 * dy1 + w0 * dy2
        dcc_ref[...] = (dv * cx_ref[...].astype(F32)).astype(BF16)
        dcx_ref[...] = (dv * cc_ref[...].astype(F32)).astype(BF16)

    col = pl.BlockSpec((s, 128), lambda j: (0, j))
    act = jax.ShapeDtypeStruct((s, CONV_W), BF16)
    return pl.pallas_call(
        body, name=name, grid=(CONV_W // 128,),
        in_specs=[col, _proj_cols(s, 0), _proj_cols(s, 4), _proj_cols(s, 8),
                  pl.BlockSpec((3, 128), lambda j: (0, j))],
        out_specs=[col, col, col, pl.BlockSpec((3, 128), lambda j: (0, j))],
        out_shape=[act, act, act, jax.ShapeDtypeStruct((3, CONV_W), F32)],
        compiler_params=_params("parallel"),
    )(dsa, proj, proj, proj, conv_w)


def _gate_specs(t):
    return [pl.BlockSpec((t, D_MODEL), lambda i: (i, 3)), pl.BlockSpec((t, D_MODEL), lambda i: (i, 4))]


def merge_forward(sa, o, proj, w_co, w_ao, b_merge, name, exchange=None):
    s = sa.shape[0]
    t = min(ROW_TILE, s)

    def body(sa_ref, o_ref, ga_ref, gb_ref, wco_ref, wao_ref, bm_ref, mg_ref, ya_ref, yb_ref):
        ya = _dot(sa_ref[...], wco_ref[...])
        yb = _dot(o_ref[...], wao_ref[...])
        sga = _sigmoid(ga_ref[...].astype(F32) + bm_ref[0:1, :])
        sgb = _sigmoid(gb_ref[...].astype(F32) + bm_ref[1:2, :])
        mg_ref[...] = (sga * ya + sgb * yb).astype(BF16)
        ya_ref[...] = ya.astype(BF16)
        yb_ref[...] = yb.astype(BF16)

    act = jax.ShapeDtypeStruct((s, D_MODEL), BF16)
    return _call(
        body, name=name, grid=(s // t,),
        in_specs=[_row_spec(t, CONV_W), _row_spec(t, ATTN_W)] + _gate_specs(t)
        + [_vec_spec(CONV_W), _vec_spec(ATTN_W), _vec_spec(2)],
        out_specs=[_row_spec(t)] * 3, out_shape=[act, act, act],
        operands=(sa, o, proj, proj, w_co, w_ao, b_merge), semantics=("parallel",), exchange=exchange)


def merge_backward(dy, w_out, proj, ya, yb, b_merge, name, exchange=None):
    s = dy.shape[0]
    t = min(ROW_TILE, s)

    def body(dy_ref, w_ref, ga_ref, gb_ref, ya_ref, yb_ref, bm_ref,
             dya_ref, dyb_ref, dga_ref, dgb_ref, dbm_ref):
        @pl.when(pl.program_id(0) == 0)
        def _():
            dbm_ref[...] = jnp.zeros_like(dbm_ref)

        dmg = _dot(dy_ref[...], w_ref[...], NT)
        sga = _sigmoid(ga_ref[...].astype(F32) + bm_ref[0:1, :])
        sgb = _sigmoid(gb_ref[...].astype(F32) + bm_ref[1:2, :])
        dya_ref[...] = (dmg * sga).astype(BF16)
        dyb_ref[...] = (dmg * sgb).astype(BF16)
        dga = dmg * ya_ref[...].astype(F32) * sga * (1.0 - sga)
        dgb = dmg * yb_ref[...].astype(F32) * sgb * (1.0 - sgb)
        dga_ref[...] = dga.astype(BF16)
        dgb_ref[...] = dgb.astype(BF16)
        dbm_ref[0:1, :] += jnp.sum(dga, axis=0, keepdims=True)
        dbm_ref[1:2, :] += jnp.sum(dgb, axis=0, keepdims=True)

    act = jax.ShapeDtypeStruct((s, D_MODEL), BF16)
    return _call(
        body, name=name, grid=(s // t,),
        in_specs=[_row_spec(t), _vec_spec(D_MODEL)] + _gate_specs(t)
        + [_row_spec(t), _row_spec(t), _vec_spec(2)],
        out_specs=[_row_spec(t)] * 4 + [_vec_spec(2)],
        out_shape=[act] * 4 + [jax.ShapeDtypeStruct((2, D_MODEL), F32)],
        operands=(dy, w_out, proj, proj, ya, yb, b_merge), semantics=("arbitrary",), exchange=exchange)


def out_proj_bwd(dya, dyb, w_co, w_ao, name):
    s = dya.shape[0]
    t = min(ROW_TILE, s)

    def body(dya_ref, dyb_ref, wco_ref, wao_ref, dsa_ref, do_ref):
        dsa_ref[...] = _dot(dya_ref[...], wco_ref[...], NT).astype(BF16)
        do_ref[...] = _dot(dyb_ref[...], wao_ref[...], NT).astype(BF16)

    return pl.pallas_call(
        body, name=name, grid=(s // t,),
        in_specs=[_row_spec(t), _row_spec(t), _vec_spec(CONV_W), _vec_spec(ATTN_W)],
        out_specs=[_row_spec(t, CONV_W), _row_spec(t, ATTN_W)],
        out_shape=[jax.ShapeDtypeStruct((s, CONV_W), BF16), jax.ShapeDtypeStruct((s, ATTN_W), BF16)],
        compiler_params=_params("parallel"),
    )(dya, dyb, w_co, w_ao)


ATT_HEADS = 4
ATT_LANES = ATT_HEADS * HEAD_DIM
ATT_UNDERFLOW = 110.0


def _softplus(z):
    return jnp.maximum(z, 0.0) + jnp.log(1.0 + jnp.exp(-jnp.abs(z)))


def _head_masks(rows):
    lane = lax.broadcasted_iota(jnp.int32, (rows, ATT_LANES), 1)
    return [(lane >= h * HEAD_DIM) & (lane < (h + 1) * HEAD_DIM) for h in range(ATT_HEADS)]


def _per_head(x, masks):
    return [jnp.where(m, x, jnp.zeros_like(x)) for m in masks]


def _att_specs(s, blk):
    first = {"q": 3 * CONV_W // ATT_LANES, "k": (3 * CONV_W + ATTN_W) // ATT_LANES,
             "v": (3 * CONV_W + 2 * ATTN_W) // ATT_LANES}
    return [pl.BlockSpec((blk, ATT_LANES), lambda h, i: (i, first["q"] + h)),
            pl.BlockSpec((s, ATT_LANES), lambda h, i: (0, first["k"] + h)),
            pl.BlockSpec((s, ATT_LANES), lambda h, i: (0, first["v"] + h))]


def _head_norms(x, masks):
    sq = jnp.square(x.astype(F32))
    return [jnp.sum(jnp.where(m, sq, 0.0), axis=1, keepdims=True) for m in masks]


def stick_breaking_fwd(proj, name, exchange=None):
    s = proj.shape[0]
    blk = ATT_BLK
    nq = s // blk

    def body(q_ref, k_ref, v_ref, o_ref, tot_ref, first_ref, kmax_ref):
        i = pl.program_id(1)
        row = lax.broadcasted_iota(jnp.int32, (blk, blk), 0)
        col = lax.broadcasted_iota(jnp.int32, (blk, blk), 1)
        tri = (row >= col).astype(BF16)
        causal = col < row
        masks = _head_masks(blk)
        q_all = q_ref[...] * ATTN_SCALE
        qs = _per_head(q_all, masks)

        @pl.when(i == 0)
        def _():
            def longest(n, best):
                norms = _head_norms(k_ref[pl.ds(pl.multiple_of(n * blk, blk), blk), :], masks)
                return tuple(jnp.maximum(b, v) for b, v in zip(best, norms))

            best = lax.fori_loop(0, nq, longest, tuple(jnp.zeros((blk, 1), F32) for _ in range(ATT_HEADS)))
            for h in range(ATT_HEADS):
                kmax_ref[h] = jnp.sqrt(jnp.max(best[h], axis=0, keepdims=True))

        needed = [jnp.sqrt(n) * kmax_ref[h] + ATT_UNDERFLOW for h, n in enumerate(_head_norms(q_all, masks))]

        def finished(laters):
            slack = laters[0] - needed[0]
            for h in range(1, ATT_HEADS):
                slack = jnp.minimum(slack, laters[h] - needed[h])
            return (jnp.min(slack) >= 0.0).astype(jnp.int32)

        def step(j, carry, diagonal):
            laters, acc = carry
            rows = pl.ds(pl.multiple_of(j * blk, blk), blk)
            kb = k_ref[rows, :]
            probs, new_laters = [], []
            for h in range(ATT_HEADS):
                z = _dot(qs[h], kb, NT)
                sp = _softplus(z)
                if diagonal:
                    sp = jnp.where(causal, sp, 0.0)
                a = jnp.exp(z - (_dot(sp.astype(BF16), tri) + laters[h]))
                if diagonal:
                    a = jnp.where(causal, a, 0.0)
                probs.append(a.astype(BF16))
                new_laters.append(laters[h] + jnp.sum(sp, axis=1, keepdims=True))
            v_heads = jnp.concatenate(_per_head(v_ref[rows, :], masks), axis=0)
            acc = acc + _dot(jnp.concatenate(probs, axis=1), v_heads)
            return tuple(new_laters), acc

        carry = (tuple(jnp.zeros((blk, 1), F32) for _ in range(ATT_HEADS)), jnp.zeros((blk, ATT_LANES), F32))
        laters, acc = step(i, carry, True)

        def further(state):
            n, _, laters, acc = state
            laters, acc = step(i - 1 - n, (laters, acc), False)
            return n + 1, finished(laters), laters, acc

        walked, _, laters, acc = lax.while_loop(
            lambda state: jnp.logical_and(state[0] < i, state[1] == 0), further,
            (jnp.int32(0), finished(laters), laters, acc))
        o_ref[...] = acc.astype(BF16)
        tot = jnp.zeros((blk, ATT_LANES), F32)
        for h in range(ATT_HEADS):
            tot = jnp.where(masks[h], laters[h], tot)
        tot_ref[...] = tot
        first_ref[...] = jnp.full(first_ref.shape, i - walked, jnp.int32).astype(F32)

    out_spec = pl.BlockSpec((blk, ATT_LANES), lambda h, i: (i, h))
    groups = N_HEADS // ATT_HEADS
    return _call(
        body, name=name, grid=(groups, nq),
        in_specs=_att_specs(s, blk),
        out_specs=[out_spec, out_spec, pl.BlockSpec((None, None, 8, 128), lambda h, i: (h, i, 0, 0))],
        out_shape=[jax.ShapeDtypeStruct((s, ATTN_W), BF16), jax.ShapeDtypeStruct((s, ATTN_W), F32),
                   jax.ShapeDtypeStruct((groups, nq, 8, 128), F32)],
        operands=(proj, proj, proj), scratch_shapes=[pltpu.VMEM((ATT_HEADS, 1, 1), F32)],
        semantics=("parallel", "arbitrary"), exchange=exchange)


def stick_breaking_bwd(proj, do, tot, first, name, exchange=None):
    s = proj.shape[0]
    blk = ATT_BLK
    nq = s // blk

    def body(q_ref, k_ref, v_ref, do_ref, tot_ref, first_ref, dq_ref, dk_ref, dv_ref):
        i = pl.program_id(1)
        start = jnp.clip(jnp.max(first_ref[...]).astype(jnp.int32), 0, i)

        @pl.when(i == 0)
        def _():
            dk_ref[...] = jnp.zeros_like(dk_ref)
            dv_ref[...] = jnp.zeros_like(dv_ref)

        row = lax.broadcasted_iota(jnp.int32, (blk, blk), 0)
        col = lax.broadcasted_iota(jnp.int32, (blk, blk), 1)
        before = (row < col).astype(BF16)
        upto = (row <= col).astype(BF16)
        causal = col < row
        masks = _head_masks(blk)
        qs = _per_head(q_ref[...] * ATTN_SCALE, masks)
        dos = _per_head(do_ref[...], masks)
        q_heads = jnp.concatenate(qs, axis=0)
        do_heads = jnp.concatenate(dos, axis=0)
        tot_all = tot_ref[...]
        totals = [jnp.max(jnp.where(m, tot_all, 0.0), axis=1, keepdims=True) for m in masks]

        def step(j, carry, diagonal):
            earliers, g_sums, dq = carry
            rows = pl.ds(pl.multiple_of(j * blk, blk), blk)
            kb = k_ref[rows, :]
            vb = v_ref[rows, :]
            probs, dzs, new_earliers, new_g_sums = [], [], [], []
            for h in range(ATT_HEADS):
                z = _dot(qs[h], kb, NT)
                sp = _softplus(z)
                if diagonal:
                    sp = jnp.where(causal, sp, 0.0)
                c = (totals[h] - earliers[h]) - _dot(sp.astype(BF16), before)
                a = jnp.exp(z - c)
                if diagonal:
                    a = jnp.where(causal, a, 0.0)
                g = a * _dot(dos[h], vb, NT)
                f = g_sums[h] + _dot(g.astype(BF16), upto)
                dz = g - jnp.exp(z - sp) * f
                if diagonal:
                    dz = jnp.where(causal, dz, 0.0)
                probs.append(a.astype(BF16))
                dzs.append(dz.astype(BF16))
                new_earliers.append(earliers[h] + jnp.sum(sp, axis=1, keepdims=True))
                new_g_sums.append(g_sums[h] + jnp.sum(g, axis=1, keepdims=True))
            k_heads = jnp.concatenate(_per_head(kb, masks), axis=0)
            dq = dq + _dot(jnp.concatenate(dzs, axis=1), k_heads)
            dk_ref[rows, :] += _dot(jnp.concatenate(dzs, axis=0), q_heads, TN)
            dv_ref[rows, :] += _dot(jnp.concatenate(probs, axis=0), do_heads, TN)
            return tuple(new_earliers), tuple(new_g_sums), dq

        zeros = tuple(jnp.zeros((blk, 1), F32) for _ in range(ATT_HEADS))
        carry = (zeros, zeros, jnp.zeros((blk, ATT_LANES), F32))
        carry = lax.fori_loop(start, i, lambda j, c: step(j, c, False), carry)
        dq = step(i, carry, True)[2]
        dq_ref[...] = (dq * ATTN_SCALE).astype(BF16)

    blk_spec = pl.BlockSpec((blk, ATT_LANES), lambda h, i: (i, h))
    full_spec = pl.BlockSpec((s, ATT_LANES), lambda h, i: (0, h))
    return _call(
        body, name=name, grid=(N_HEADS // ATT_HEADS, nq),
        in_specs=_att_specs(s, blk) + [blk_spec, blk_spec,
                                       pl.BlockSpec((None, None, 8, 128), lambda h, i: (h, i, 0, 0))],
        out_specs=[blk_spec, full_spec, full_spec],
        out_shape=[jax.ShapeDtypeStruct((s, ATTN_W), BF16), jax.ShapeDtypeStruct((s, ATTN_W), F32),
                   jax.ShapeDtypeStruct((s, ATTN_W), F32)],
        operands=(proj, proj, proj, do, tot, first), semantics=("parallel", "arbitrary"), exchange=exchange)


def adamw(w, m, v, parts, name):
    r, c = w.shape
    p = parts.shape[0]
    t = r
    for cand in (256, 176):
        if r % cand == 0 and r > cand:
            t = cand
            break

    def body(w_ref, m_ref, v_ref, p_ref, g_ref, d_ref, mo_ref, vo_ref):
        g = p_ref[0].astype(F32)
        for n in range(1, p):
            g = g + p_ref[n].astype(F32)
        m_new = ADAM_B1 * m_ref[...] + (1.0 - ADAM_B1) * g
        v_new = ADAM_B2 * v_ref[...] + (1.0 - ADAM_B2) * (g * g)
        m_hat = m_new / ADAM_BC1
        v_hat = v_new / ADAM_BC2
        g_ref[...] = g
        d_ref[...] = -ADAM_LR * (m_hat / (jnp.sqrt(v_hat) + ADAM_EPS) + ADAM_WD * w_ref[...])
        mo_ref[...] = m_new
        vo_ref[...] = v_new

    spec = pl.BlockSpec((t, c), lambda i: (i, 0))
    out = jax.ShapeDtypeStruct((r, c), F32)
    return pl.pallas_call(
        body, name=name, grid=(r // t,),
        in_specs=[spec, spec, spec, pl.BlockSpec((p, t, c), lambda i: (0, i, 0))],
        out_specs=[spec] * 4, out_shape=[out] * 4,
        compiler_params=_params("parallel"),
    )(w, m, v, parts)


def kernel(x, c, w_ada, b_ada, norm1_g, ffn1_w_gu, ffn1_w_down, norm2_g, w_mix_in, b_merge, conv_w, w_conv_out, w_attn_out, w_out, norm3_g, ffn2_w_gu, ffn2_w_down, final_g, loss_target, m_w_ada, m_b_ada, m_norm1_g, m_ffn1_w_gu, m_ffn1_w_down, m_norm2_g, m_w_mix_in, m_b_merge, m_conv_w, m_w_conv_out, m_w_attn_out, m_w_out, m_norm3_g, m_ffn2_w_gu, m_ffn2_w_down, m_final_g, v_w_ada, v_b_ada, v_norm1_g, v_ffn1_w_gu, v_ffn1_w_down, v_norm2_g, v_w_mix_in, v_b_merge, v_conv_w, v_w_conv_out, v_w_attn_out, v_w_out, v_norm3_g, v_ffn2_w_gu, v_ffn2_w_down, v_final_g):
    s = x.shape[1]
    me = 4 * lax.axis_index("x") + 2 * lax.axis_index("y") + lax.axis_index("c")
    x0 = x[0]
    target = loss_target[0]
    final_g2 = final_g.reshape(1, D_MODEL)

    def shard(w):
        return w[0].astype(BF16)

    def flipped(w):
        return jnp.swapaxes(w, 1, 2)

    def rows8(g):
        return g.reshape(N_DEV, -1, D_MODEL)

    got = run_exchange(gather_stage1([shard(flipped(ffn1_w_gu)), shard(ffn1_w_down)]), "gather_ffn1_chips")
    wgu1, wd1 = run_exchange(gather_stage2(got), "gather_ffn1_cores")
    wd1 = wd1.reshape(4, FF_BLK, D_MODEL)

    small_in = jnp.concatenate([c.reshape(-1), b_merge.reshape(-1), conv_w.reshape(-1),
                                jnp.zeros((64,), F32)]).reshape(1, -1)
    small_all = all_gather_rows(small_in, "gather_small")[:, 0, :]
    c_all = small_all[:, :D_MODEL]
    bm_full = small_all[:, 1024:1280].reshape(8, 2, 128).transpose(1, 0, 2).reshape(2, D_MODEL)
    cw_full = small_all[:, 1280:1472].reshape(8, 3, 64).transpose(1, 0, 2).reshape(3, CONV_W)
    n_ada = w_ada.shape[2]
    b_cols = lax.dynamic_slice(b_ada, (0, me * n_ada), (1, n_ada))
    mod_part = ada_forward(c_all, w_ada[0], b_cols, "ada_forward")
    mod_all = all_gather_rows(mod_part, "gather_mod")
    mod = lax.dynamic_index_in_dim(mod_all, me, axis=1, keepdims=False).reshape(9, 1, D_MODEL)
    sh1, sc1, gt1, sh2, sc2, gt2, sh3, sc3, gt3 = [mod[n] for n in range(9)]

    u1 = norm_modulate(x0, norm1_g, sh1, sc1, "norm_mod_1")
    (gu1, act1), got = ffn_up(u1, wgu1, "ffn_up_1", exchange=gather_stage1([shard(w_mix_in)]))
    (x1, y1), (wmix, *got) = residual_matmul(
        act1, wd1, x0, gt1, 0.5, "ffn_down_1", exchange=merge_exchanges(
            gather_stage2(got), gather_stage1([shard(w_conv_out), shard(w_attn_out), shard(w_out)])))

    u2 = norm_modulate(x1, norm2_g, sh2, sc2, "norm_mod_2")
    proj, (wco, wao, wout) = mix_in_proj(u2, wmix, "mix_in", exchange=gather_stage2(got))
    wco = wco.transpose(1, 0, 2).reshape(CONV_W, D_MODEL)
    wao = wao.transpose(1, 0, 2).reshape(ATTN_W, D_MODEL)
    wout = wout.reshape(D_MODEL, D_MODEL)
    sa = short_conv(proj, cw_full, "short_conv")
    (o, tot, first), got = stick_breaking_fwd(proj, "attn_fwd",
                                       exchange=gather_stage1([shard(flipped(ffn2_w_gu)), shard(ffn2_w_down)]))
    (merged, ya, yb), (wgu3, wd3) = merge_forward(sa, o, proj, wco, wao, bm_full, "merge",
                                                  exchange=gather_stage2(got))
    wd3 = wd3.reshape(4, FF_BLK, D_MODEL)
    x2, y2 = residual_matmul(merged[None], wout[None], x1, gt2, 1.0, "out_proj")

    u3 = norm_modulate(x2, norm3_g, sh3, sc3, "norm_mod_3")
    gu3, act3 = ffn_up(u3, wgu3, "ffn_up_3")
    x3, y3 = residual_matmul(act3, wd3, x2, gt3, 0.5, "ffn_down_3")

    dx3, dy3, dgt3, dfinal, sq = loss_head(x3, target, final_g2, gt3, y3, 0.5, "loss_head")
    dgu3 = ffn_dact(dy3, wd3, gu3, "ffn_dact_3").reshape(8, s, FF_BLK)
    g_wd3 = rows8(matmul_tn(act3, dy3[None], "grad_w_down_3", group=(4, 1)))
    du3 = matmul_nt_acc(dgu3, wgu3, "ffn_du_3", b_dims=NN)
    g_wgu3 = matmul_tn(dgu3, u3[None], "grad_w_gu_3", group=(4, 1)).reshape(8, FF_BLK, D_MODEL)
    dx2, dsh3, dsc3, dn3, dy2, dgt2 = norm_modulate_bwd(du3, x2, dx3, norm3_g, sc3, "norm_bwd_3",
                                                        prev=(gt2, y2, 1.0))

    (dya, dyb, dga, dgb, dbm), pairs = merge_backward(dy2, wout, proj, ya, yb, bm_full, "merge_bwd",
                                                      exchange=scatter_stage1([g_wgu3, g_wd3]))
    sums3 = [pair_sum(g_wgu3, pairs[0], "pair_sum_w_gu_3"), pair_sum(g_wd3, pairs[1], "pair_sum_w_down_3")]
    g_wout = rows8(matmul_tn(merged[None], dy2[None], "grad_w_out"))
    dsa, do = out_proj_bwd(dya, dyb, wco, wao, "out_proj_bwd")
    g_wco = matmul_tn(sa[None], dya[None], "grad_w_conv_out").reshape(CONV_W, N_DEV, 128).transpose(1, 0, 2)
    g_wao = matmul_tn(o[None], dyb[None], "grad_w_attn_out").reshape(ATTN_W, N_DEV, 128).transpose(1, 0, 2)
    dcb, dcc, dcx, dconv = short_conv_bwd(dsa, proj, cw_full, "short_conv_bwd")
    (dq, dk, dv), landed3 = stick_breaking_bwd(proj, do, tot, first, "attn_bwd", exchange=scatter_stage2(sums3))
    dproj = jnp.concatenate([dcb, dcc, dcx, dq, dk.astype(BF16), dv.astype(BF16), dga, dgb], axis=1)
    du2 = matmul_nt_acc(dproj, wmix, "mix_in_du")
    g_wmix = matmul_tn(u2[None], dproj, "grad_w_mix_in", group=(1, 4), b_cols=MIX_BLK).reshape(
        N_DEV, D_MODEL, MIX_BLK)
    mixer_grads = [g_wmix, g_wco, g_wao, g_wout]
    (dx1, dsh2, dsc2, dn2, dy1, dgt1), pairs = norm_modulate_bwd(
        du2, x1, dx2, norm2_g, sc2, "norm_bwd_2", prev=(gt1, y1, 0.5), exchange=scatter_stage1(mixer_grads))
    sums_mix = [pair_sum(g, p, f"pair_sum_mixer_{n}") for n, (g, p) in enumerate(zip(mixer_grads, pairs))]

    dgu1, landed_mix = ffn_dact(dy1, wd1, gu1, "ffn_dact_1", exchange=scatter_stage2(sums_mix[:1]))
    dgu1 = dgu1.reshape(8, s, FF_BLK)
    g_wgu1, landed_small = matmul_tn(dgu1, u1[None], "grad_w_gu_1", group=(4, 1),
                                     exchange=scatter_stage2(sums_mix[1:]))
    g_wgu1 = g_wgu1.reshape(8, FF_BLK, D_MODEL)
    g_wd1, pairs = matmul_tn(act1, dy1[None], "grad_w_down_1", group=(4, 1), exchange=scatter_stage1([g_wgu1]))
    g_wd1 = rows8(g_wd1)
    sum_gu1 = pair_sum(g_wgu1, pairs[0], "pair_sum_w_gu_1")
    du1, (landed_gu1, pair_d1) = matmul_nt_acc(
        dgu1, wgu1, "ffn_du_1", b_dims=NN, exchange=merge_exchanges(scatter_stage2([sum_gu1]), scatter_stage1([g_wd1])))
    sum_d1 = pair_sum(g_wd1, pair_d1, "pair_sum_w_down_1")
    (grad_x, dsh1, dsc1, dn1), landed_d1 = norm_modulate_bwd(du1, x0, dx1, norm1_g, sc1, "norm_bwd_1",
                                                            exchange=scatter_stage2([sum_d1]))

    loss_local = (0.5 / D_MODEL) * jnp.sum(sq)
    stats = jnp.concatenate(
        [v.reshape(-1) for v in (dsh1, dsc1, dgt1, dsh2, dsc2, dgt2, dsh3, dsc3, dgt3,
                                 dn1, dn2, dn3, dfinal, dbm, dconv)]
        + [jnp.broadcast_to(loss_local, (128,))]).reshape(1, -1)
    stats_all = all_gather_rows(stats, "gather_stats")
    n_mod = 9 * D_MODEL
    loss = jnp.sum(stats_all[:, 0, -1])
    dmod_all = stats_all[:, :, :n_mod]
    off = n_mod
    parts = {}
    for key in ("norm1_g", "norm2_g", "norm3_g", "final_g"):
        parts[key] = stats_all[:, :, off:off + D_MODEL]
        off += D_MODEL
    dbm_all = stats_all[:, 0, off:off + 2 * D_MODEL].reshape(N_DEV, 2, D_MODEL)
    off += 2 * D_MODEL
    dcw_all = stats_all[:, 0, off:off + 3 * CONV_W].reshape(N_DEV, 3, CONV_W)
    parts["b_merge"] = lax.dynamic_slice(dbm_all, (0, 0, me * 128), (N_DEV, 2, 128))
    parts["conv_w"] = lax.dynamic_slice(dcw_all, (0, 0, me * 64), (N_DEV, 3, 64))
    dmod_cols = lax.dynamic_slice(dmod_all[:, 0, :], (0, me * n_ada), (N_DEV, n_ada))
    parts["w_ada"] = ada_backward(c_all, dmod_cols, "ada_backward")[None]
    parts["b_ada"] = dmod_all
    parts["ffn2_w_gu"], parts["ffn2_w_down"] = landed3
    parts["w_mix_in"] = landed_mix[0]
    parts["w_conv_out"], parts["w_attn_out"], parts["w_out"] = landed_small
    parts["ffn1_w_gu"] = landed_gu1
    parts["ffn1_w_down"] = landed_d1[0]

    given = dict(w_ada=w_ada, b_ada=b_ada, norm1_g=norm1_g, ffn1_w_gu=ffn1_w_gu, ffn1_w_down=ffn1_w_down,
                 norm2_g=norm2_g, w_mix_in=w_mix_in, b_merge=b_merge, conv_w=conv_w, w_conv_out=w_conv_out,
                 w_attn_out=w_attn_out, w_out=w_out, norm3_g=norm3_g, ffn2_w_gu=ffn2_w_gu,
                 ffn2_w_down=ffn2_w_down, final_g=final_g)
    moments_m = dict(w_ada=m_w_ada, b_ada=m_b_ada, norm1_g=m_norm1_g, ffn1_w_gu=m_ffn1_w_gu,
                     ffn1_w_down=m_ffn1_w_down, norm2_g=m_norm2_g, w_mix_in=m_w_mix_in, b_merge=m_b_merge,
                     conv_w=m_conv_w, w_conv_out=m_w_conv_out, w_attn_out=m_w_attn_out, w_out=m_w_out,
                     norm3_g=m_norm3_g, ffn2_w_gu=m_ffn2_w_gu, ffn2_w_down=m_ffn2_w_down, final_g=m_final_g)
    moments_v = dict(w_ada=v_w_ada, b_ada=v_b_ada, norm1_g=v_norm1_g, ffn1_w_gu=v_ffn1_w_gu,
                     ffn1_w_down=v_ffn1_w_down, norm2_g=v_norm2_g, w_mix_in=v_w_mix_in, b_merge=v_b_merge,
                     conv_w=v_conv_w, w_conv_out=v_w_conv_out, w_attn_out=v_w_attn_out, w_out=v_w_out,
                     norm3_g=v_norm3_g, ffn2_w_gu=v_ffn2_w_gu, ffn2_w_down=v_ffn2_w_down, final_g=v_final_g)
    order = ["w_ada", "b_ada", "norm1_g", "ffn1_w_gu", "ffn1_w_down", "norm2_g", "w_mix_in", "b_merge",
             "conv_w", "w_conv_out", "w_attn_out", "w_out", "norm3_g", "ffn2_w_gu", "ffn2_w_down", "final_g"]
    grads, deltas, new_m, new_v = [], [], [], []
    for key in order:
        turn = flipped if key in ("ffn1_w_gu", "ffn2_w_gu") else (lambda a: a)
        shape = turn(given[key]).shape
        shape2 = (1, shape[0]) if len(shape) == 1 else shape[-2:]
        outs = adamw(turn(given[key]).reshape(shape2), turn(moments_m[key]).reshape(shape2),
                     turn(moments_v[key]).reshape(shape2), parts[key], f"adamw_{key}")
        for dst, val in zip((grads, deltas, new_m, new_v), outs):
            dst.append(turn(val.reshape(shape)))

    return (loss, grad_x[None], *grads, *deltas, *new_m, *new_v)
```

```python
import functools
from typing import Callable, NamedTuple

import jax
import jax.numpy as jnp
from jax import lax
from jax.experimental import pallas as pl
from jax.experimental.pallas import tpu as pltpu

F32 = jnp.float32
BF16 = jnp.bfloat16
MESH = pl.DeviceIdType.MESH
ANY = pl.BlockSpec(memory_space=pl.ANY)

N_DEV = 8
D_MODEL = 1024
D_FF = 2816
FF_BLK = D_FF // 4
N_HEADS = 8
HEAD_DIM = 64
CONV_W = 512
ATTN_W = 512
MIX_W = 3 * CONV_W + 3 * ATTN_W + 2 * D_MODEL
MIX_BLK = MIX_W // N_DEV
EPS = 1e-6
ATTN_SCALE = HEAD_DIM ** -0.5

ADAM_LR = 0.001
ADAM_B1 = 0.9
ADAM_B2 = 0.999
ADAM_EPS = 1e-08
ADAM_WD = 0.01
ADAM_STEP = 10
ADAM_BC1 = 1.0 - ADAM_B1 ** ADAM_STEP
ADAM_BC2 = 1.0 - ADAM_B2 ** ADAM_STEP

VMEM_LIMIT = 56 * 1024 * 1024
ROW_TILE = 512
ACC_TILE = 1024
ELT_TILE = 256
ATT_BLK = 256

NN = (((1,), (0,)), ((), ()))
NT = (((1,), (1,)), ((), ()))
TN = (((0,), (0,)), ((), ()))


def _dot(a, b, dims=NN):
    return lax.dot_general(a, b, dims, preferred_element_type=F32)


def _params(*sem):
    return pltpu.CompilerParams(dimension_semantics=sem, vmem_limit_bytes=VMEM_LIMIT)


def _sigmoid(x):
    return 1.0 / (1.0 + jnp.exp(-x))


def _me():
    x, y, c = lax.axis_index("x"), lax.axis_index("y"), lax.axis_index("c")
    return x, y, c, 4 * x + 2 * y + c


def _peer(k):
    x, y, c, _ = _me()
    px = 1 - x if (k >> 2) & 1 else x
    py = 1 - y if (k >> 1) & 1 else y
    pc = 1 - c if k & 1 else c
    return (px, py, pc), 4 * px + 2 * py + pc


class Exchange(NamedTuple):
    operands: tuple
    out_shapes: tuple
    aliases: dict
    n_remote: int
    n_local: int
    copies: Callable


CHIP_FLIPS = (2, 4, 6)
SIBLING = 1


def _remote(src, dst, send_sems, recv_sems, n, peer):
    return pltpu.make_async_remote_copy(src_ref=src, dst_ref=dst, send_sem=send_sems.at[n], recv_sem=recv_sems.at[n],
                                        device_id=peer, device_id_type=MESH)


def gather_stage1(shards):
    n = len(shards)
    rels = (SIBLING,) + CHIP_FLIPS

    def copies(ins, outs, send_sems, recv_sems, local_sems, rb, lb):
        _, _, _, me = _me()
        cps = []
        for w in range(n):
            cps.append(pltpu.make_async_copy(ins[w], outs[w].at[me], local_sems.at[lb + w]))
            for a, k in enumerate(rels):
                peer, _ = _peer(k)
                cps.append(_remote(ins[w], outs[w].at[me], send_sems, recv_sems, rb + len(rels) * w + a, peer))
        return cps

    shapes = tuple(jax.ShapeDtypeStruct((N_DEV,) + s.shape, s.dtype) for s in shards)
    return Exchange(tuple(shards), shapes, {}, len(rels) * n, n, copies)


def gather_stage2(fulls):
    n = len(fulls)

    def copies(ins, outs, send_sems, recv_sems, local_sems, rb, lb):
        sibling, _ = _peer(SIBLING)
        cps = []
        for w in range(n):
            for a, k in enumerate(CHIP_FLIPS):
                _, blk = _peer(k)
                cps.append(_remote(outs[w].at[blk], outs[w].at[blk], send_sems, recv_sems, rb + 3 * w + a, sibling))
        return cps

    shapes = tuple(jax.ShapeDtypeStruct(f.shape, f.dtype) for f in fulls)
    return Exchange(tuple(fulls), shapes, {w: w for w in range(n)}, 3 * n, 0, copies)


def scatter_stage1(fulls):
    n = len(fulls)

    def copies(ins, outs, send_sems, recv_sems, local_sems, rb, lb):
        _, _, c, _ = _me()
        sibling, _ = _peer(SIBLING)
        cps = []
        for w in range(n):
            for q in range(4):
                cps.append(_remote(ins[w].at[2 * q + (1 - c)], outs[w].at[q], send_sems, recv_sems, rb + 4 * w + q, sibling))
        return cps

    shapes = tuple(jax.ShapeDtypeStruct((4,) + f.shape[1:], f.dtype) for f in fulls)
    return Exchange(tuple(fulls), shapes, {}, 4 * n, 0, copies)


def scatter_stage2(sums):
    n = len(sums)

    def copies(ins, outs, send_sems, recv_sems, local_sems, rb, lb):
        x, y, _, _ = _me()
        mine = 2 * x + y
        cps = []
        for w in range(n):
            cps.append(pltpu.make_async_copy(ins[w].at[mine], outs[w].at[mine], local_sems.at[lb + w]))
            for a, k in enumerate(CHIP_FLIPS):
                peer, _ = _peer(k)
                cps.append(_remote(ins[w].at[2 * peer[0] + peer[1]], outs[w].at[mine], send_sems, recv_sems,
                                   rb + 3 * w + a, peer))
        return cps

    shapes = tuple(jax.ShapeDtypeStruct(s.shape, s.dtype) for s in sums)
    return Exchange(tuple(sums), shapes, {}, 3 * n, n, copies)


def merge_exchanges(a, b):
    na_in, na_out = len(a.operands), len(a.out_shapes)

    def copies(ins, outs, send_sems, recv_sems, local_sems, rb, lb):
        return (a.copies(ins[:na_in], outs[:na_out], send_sems, recv_sems, local_sems, rb, lb)
                + b.copies(ins[na_in:], outs[na_out:], send_sems, recv_sems, local_sems, rb + a.n_remote, lb + a.n_local))

    aliases = dict(a.aliases)
    aliases.update({na_in + i: na_out + o for i, o in b.aliases.items()})
    return Exchange(a.operands + b.operands, a.out_shapes + b.out_shapes, aliases,
                    a.n_remote + b.n_remote, a.n_local + b.n_local, copies)


def _exchange_scratch(ex):
    return [pltpu.SemaphoreType.DMA((ex.n_remote,)), pltpu.SemaphoreType.DMA((ex.n_remote,)),
            pltpu.SemaphoreType.DMA((max(ex.n_local, 1),))]


def run_exchange(ex, name):
    n_in, n_out = len(ex.operands), len(ex.out_shapes)

    def body(*refs):
        cps = ex.copies(refs[:n_in], refs[n_in:n_in + n_out], *refs[n_in + n_out:], 0, 0)
        for cp in cps:
            cp.start()
        for cp in cps:
            cp.wait()

    return pl.pallas_call(
        body, name=name, out_shape=list(ex.out_shapes), in_specs=[ANY] * n_in, out_specs=[ANY] * n_out,
        scratch_shapes=_exchange_scratch(ex), input_output_aliases=dict(ex.aliases),
    )(*ex.operands)


def _call(body, *, name, grid, in_specs, out_specs, out_shape, operands, scratch_shapes=(), semantics=(),
          exchange=None):
    if exchange is None:
        return pl.pallas_call(
            body, name=name, grid=grid, in_specs=in_specs, out_specs=out_specs, out_shape=out_shape,
            scratch_shapes=list(scratch_shapes), compiler_params=_params(*semantics))(*operands)
    single = not isinstance(out_shape, (list, tuple))
    out_shapes = [out_shape] if single else list(out_shape)
    out_specs_l = [out_specs] if single else list(out_specs)
    n_in, n_out, n_scr = len(operands), len(out_shapes), len(scratch_shapes)
    x_in, x_out = len(exchange.operands), len(exchange.out_shapes)

    def hosted(*refs):
        ins, refs = refs[:n_in], refs[n_in:]
        xin, refs = refs[:x_in], refs[x_in:]
        outs, refs = refs[:n_out], refs[n_out:]
        xout, refs = refs[:x_out], refs[x_out:]
        scr, sems = refs[:n_scr], refs[n_scr:]
        first = functools.reduce(jnp.logical_and, [pl.program_id(a) == 0 for a in range(len(grid))])
        last = functools.reduce(jnp.logical_and, [pl.program_id(a) == g - 1 for a, g in enumerate(grid)])

        @pl.when(first)
        def _():
            for cp in exchange.copies(xin, xout, *sems, 0, 0):
                cp.start()

        body(*ins, *outs, *scr)

        @pl.when(last)
        def _():
            for cp in exchange.copies(xin, xout, *sems, 0, 0):
                cp.wait()

    res = pl.pallas_call(
        hosted, name=name, grid=grid,
        in_specs=list(in_specs) + [ANY] * x_in, out_specs=out_specs_l + [ANY] * x_out,
        out_shape=out_shapes + list(exchange.out_shapes),
        scratch_shapes=list(scratch_shapes) + _exchange_scratch(exchange),
        input_output_aliases={n_in + i: n_out + o for i, o in exchange.aliases.items()},
        compiler_params=_params(*(["arbitrary"] * len(grid))),
    )(*operands, *exchange.operands)
    outs, xouts = res[:n_out], res[n_out:]
    return (outs[0] if single else outs), xouts


def all_gather_rows(v, name):
    r, n = v.shape

    def body(v_ref, out_ref, send_sems, recv_sems):
        _, _, _, me = _me()
        out_ref[me] = v_ref[...]
        copies = []
        for k in range(1, N_DEV):
            peer, _ = _peer(k)
            copies.append(_remote(v_ref, out_ref.at[me], send_sems, recv_sems, k - 1, peer))
        for cp in copies:
            cp.start()
        for cp in copies:
            cp.wait()

    return pl.pallas_call(
        body, name=name,
        out_shape=jax.ShapeDtypeStruct((N_DEV, r, n), v.dtype),
        in_specs=[pl.BlockSpec(memory_space=pltpu.VMEM)],
        out_specs=pl.BlockSpec(memory_space=pltpu.VMEM),
        scratch_shapes=[pltpu.SemaphoreType.DMA((N_DEV - 1,)), pltpu.SemaphoreType.DMA((N_DEV - 1,))],
    )(v)


def pair_sum(full, pair, name):
    _, r, c = full.shape
    t = r
    core = lax.axis_index("c").astype(jnp.int32).reshape(1)

    def body(core_ref, f_ref, p_ref, o_ref):
        o_ref[...] = (f_ref[...].astype(F32) + p_ref[...].astype(F32)).astype(BF16)

    return pl.pallas_call(
        body, name=name,
        grid_spec=pltpu.PrefetchScalarGridSpec(
            num_scalar_prefetch=1, grid=(4, r // t),
            in_specs=[pl.BlockSpec((None, None, t, c), lambda q, i, core_ref: (q, core_ref[0], i, 0)),
                      pl.BlockSpec((None, t, c), lambda q, i, core_ref: (q, i, 0))],
            out_specs=pl.BlockSpec((None, t, c), lambda q, i, core_ref: (q, i, 0))),
        out_shape=jax.ShapeDtypeStruct((4, r, c), BF16),
        compiler_params=_params("parallel", "parallel"),
    )(core, full.reshape(4, 2, r, c), pair)


def prologue(small_in, w_ada, b_cols, shards, name):
    ex = gather_stage1(shards)
    n_sh = len(shards)
    n_small = small_in.shape[1]
    cols = w_ada.shape[1]

    def body(*refs):
        small_ref, w_ref, b_ref = refs[:3]
        shard_refs = refs[3:3 + n_sh]
        small_out, mod_out = refs[3 + n_sh:5 + n_sh]
        fulls = refs[5 + n_sh:5 + 2 * n_sh]
        part_ref, send1, recv1, send2, recv2, wsend, wrecv, wlocal = refs[5 + 2 * n_sh:]
        _, _, _, me = _me()
        big = ex.copies(shard_refs, fulls, wsend, wrecv, wlocal, 0, 0)
        for cp in big:
            cp.start()

        def all_gather(src_ref, dst_ref, send_sems, recv_sems):
            cps = [_remote(src_ref, dst_ref.at[me], send_sems, recv_sems, k - 1, _peer(k)[0]) for k in range(1, N_DEV)]
            for cp in cps:
                cp.start()
            for cp in cps:
                cp.wait()

        small_out[me] = small_ref[...]
        all_gather(small_ref, small_out, send1, recv1)
        c_all = jnp.concatenate([small_out[d][:, :D_MODEL] for d in range(N_DEV)], axis=0)
        act = c_all * _sigmoid(c_all)
        part_ref[...] = jnp.dot(act, w_ref[...], precision=lax.Precision.HIGHEST,
                                preferred_element_type=F32) + b_ref[...]
        mod_out[me] = part_ref[...]
        all_gather(part_ref, mod_out, send2, recv2)
        for cp in big:
            cp.wait()

    vmem = pl.BlockSpec(memory_space=pltpu.VMEM)
    sems = pltpu.SemaphoreType.DMA((N_DEV - 1,))
    res = pl.pallas_call(
        body, name=name,
        out_shape=[jax.ShapeDtypeStruct((N_DEV, 1, n_small), F32), jax.ShapeDtypeStruct((N_DEV, N_DEV, cols), F32)]
        + list(ex.out_shapes),
        in_specs=[vmem, vmem, vmem] + [ANY] * n_sh, out_specs=[vmem, vmem] + [ANY] * n_sh,
        scratch_shapes=[pltpu.VMEM((N_DEV, cols), F32), sems, sems, sems, sems] + _exchange_scratch(ex),
        compiler_params=pltpu.CompilerParams(vmem_limit_bytes=VMEM_LIMIT),
    )(small_in, w_ada, b_cols, *shards)
    return res[0], res[1], res[2:]


def ada_backward(c_all, dmod_cols, name):
    n = dmod_cols.shape[1]

    def body(c_ref, d_ref, o_ref):
        c = c_ref[...]
        act = c * _sigmoid(c)
        o_ref[...] = lax.dot_general(act, d_ref[...], TN, precision=lax.Precision.HIGHEST,
                                     preferred_element_type=F32)

    return pl.pallas_call(
        body, name=name, out_shape=jax.ShapeDtypeStruct((D_MODEL, n), F32),
        compiler_params=pltpu.CompilerParams(vmem_limit_bytes=VMEM_LIMIT),
    )(c_all, dmod_cols)


def _row_spec(t, width=D_MODEL):
    return pl.BlockSpec((t, width), lambda i: (i, 0))


def _vec_spec(rows=1, width=D_MODEL):
    return pl.BlockSpec((rows, width), lambda i: (0, 0))


def norm_modulate(x, g, shift, scale, name):
    s = x.shape[0]
    t = min(ELT_TILE, s)

    def body(x_ref, g_ref, sh_ref, sc_ref, u_ref):
        xv = x_ref[...]
        r = lax.rsqrt(jnp.mean(xv * xv, axis=-1, keepdims=True) + EPS)
        a = (xv * r) * g_ref[...]
        u_ref[...] = (a * (1.0 + sc_ref[...]) + sh_ref[...]).astype(BF16)

    return pl.pallas_call(
        body, name=name, grid=(s // t,),
        in_specs=[_row_spec(t), _vec_spec(), _vec_spec(), _vec_spec()],
        out_specs=_row_spec(t),
        out_shape=jax.ShapeDtypeStruct((s, D_MODEL), BF16),
        compiler_params=_params("parallel"),
    )(x, g, shift, scale)


def loss_head(x, target, final_g, gate, y_prev, coef, name):
    s = x.shape[0]
    t = min(ELT_TILE, s)

    def body(x_ref, t_ref, fg_ref, gt_ref, y_ref, dx_ref, dy_ref, dgt_ref, dfg_ref, sq_ref):
        @pl.when(pl.program_id(0) == 0)
        def _():
            dgt_ref[...] = jnp.zeros_like(dgt_ref)
            dfg_ref[...] = jnp.zeros_like(dfg_ref)
            sq_ref[...] = jnp.zeros_like(sq_ref)

        xv = x_ref[...]
        fg = fg_ref[...]
        r = lax.rsqrt(jnp.mean(xv * xv, axis=-1, keepdims=True) + EPS)
        nrm = xv * r
        err = nrm * fg - t_ref[...]
        sq_ref[...] += jnp.sum(err * err, axis=0, keepdims=True)
        dout = err * (1.0 / D_MODEL)
        dfg_ref[...] += jnp.sum(dout * nrm, axis=0, keepdims=True)
        dn = dout * fg
        dx = r * (dn - nrm * jnp.mean(dn * nrm, axis=-1, keepdims=True))
        dx_ref[...] = dx
        dy_ref[...] = (coef * gt_ref[...] * dx).astype(BF16)
        dgt_ref[...] += coef * jnp.sum(dx * y_ref[...].astype(F32), axis=0, keepdims=True)

    vec = jax.ShapeDtypeStruct((1, D_MODEL), F32)
    return pl.pallas_call(
        body, name=name, grid=(s // t,),
        in_specs=[_row_spec(t), _row_spec(t), _vec_spec(), _vec_spec(), _row_spec(t)],
        out_specs=[_row_spec(t), _row_spec(t), _vec_spec(), _vec_spec(), _vec_spec()],
        out_shape=[jax.ShapeDtypeStruct((s, D_MODEL), F32), jax.ShapeDtypeStruct((s, D_MODEL), BF16),
                   vec, vec, vec],
        compiler_params=_params("arbitrary"),
    )(x, target, final_g, gate, y_prev)


def norm_modulate_bwd(du, x, dx_out, g, scale, name, prev=None, exchange=None):
    s = x.shape[0]
    t = min(ELT_TILE, s)
    has_prev = prev is not None

    def body(*refs):
        du_ref, x_ref, dxo_ref, g_ref, sc_ref = refs[:5]
        refs = refs[5:]
        if has_prev:
            gt_ref, y_ref = refs[:2]
            refs = refs[2:]
        dx_ref, dsh_ref, dsc_ref, dg_ref = refs[:4]

        @pl.when(pl.program_id(0) == 0)
        def _():
            dsh_ref[...] = jnp.zeros_like(dsh_ref)
            dsc_ref[...] = jnp.zeros_like(dsc_ref)
            dg_ref[...] = jnp.zeros_like(dg_ref)
            if has_prev:
                refs[5][...] = jnp.zeros_like(refs[5])

        xv = x_ref[...]
        duv = du_ref[...]
        gv = g_ref[...]
        r = lax.rsqrt(jnp.mean(xv * xv, axis=-1, keepdims=True) + EPS)
        nrm = xv * r
        a = nrm * gv
        dsh_ref[...] += jnp.sum(duv, axis=0, keepdims=True)
        dsc_ref[...] += jnp.sum(duv * a, axis=0, keepdims=True)
        da = duv * (1.0 + sc_ref[...])
        dg_ref[...] += jnp.sum(da * nrm, axis=0, keepdims=True)
        dn = da * gv
        dx = dxo_ref[...] + r * (dn - nrm * jnp.mean(dn * nrm, axis=-1, keepdims=True))
        dx_ref[...] = dx
        if has_prev:
            coef = prev[2]
            refs[4][...] = (coef * gt_ref[...] * dx).astype(BF16)
            refs[5][...] += coef * jnp.sum(dx * y_ref[...].astype(F32), axis=0, keepdims=True)

    vec = jax.ShapeDtypeStruct((1, D_MODEL), F32)
    operands = [du, x, dx_out, g, scale]
    in_specs = [_row_spec(t), _row_spec(t), _row_spec(t), _vec_spec(), _vec_spec()]
    out_specs = [_row_spec(t), _vec_spec(), _vec_spec(), _vec_spec()]
    out_shape = [jax.ShapeDtypeStruct((s, D_MODEL), F32), vec, vec, vec]
    if has_prev:
        operands += [prev[0], prev[1]]
        in_specs += [_vec_spec(), _row_spec(t)]
        out_specs += [_row_spec(t), _vec_spec()]
        out_shape += [jax.ShapeDtypeStruct((s, D_MODEL), BF16), vec]
    return _call(body, name=name, grid=(s // t,), in_specs=in_specs, out_specs=out_specs, out_shape=out_shape,
                 operands=operands, semantics=("arbitrary",), exchange=exchange)


def ffn_up(u, w_gu, name, exchange=None):
    s = u.shape[0]
    t = min(ROW_TILE, s)

    def body(u_ref, wg_ref, wu_ref, slope_ref, act_ref):
        uv = u_ref[...]
        g = _dot(uv, wg_ref[...], NT)
        up = _dot(uv, wu_ref[...], NT)
        sg = _sigmoid(g)
        silu = g * sg
        slope_ref[0] = (up * sg * (1.0 + g * (1.0 - sg))).astype(BF16)
        slope_ref[1] = silu.astype(BF16)
        act_ref[...] = (silu * up).astype(BF16)

    return _call(
        body, name=name, grid=(4, s // t),
        in_specs=[pl.BlockSpec((t, D_MODEL), lambda j, i: (i, 0)),
                  pl.BlockSpec((None, FF_BLK, D_MODEL), lambda j, i: (j, 0, 0)),
                  pl.BlockSpec((None, FF_BLK, D_MODEL), lambda j, i: (j + 4, 0, 0))],
        out_specs=[pl.BlockSpec((2, None, t, FF_BLK), lambda j, i: (0, j, i, 0)),
                   pl.BlockSpec((None, t, FF_BLK), lambda j, i: (j, i, 0))],
        out_shape=[jax.ShapeDtypeStruct((2, 4, s, FF_BLK), BF16),
                   jax.ShapeDtypeStruct((4, s, FF_BLK), BF16)],
        operands=(u, w_gu, w_gu), semantics=("parallel", "parallel"), exchange=exchange)


def residual_matmul(a, b, x, gate, coef, name, exchange=None):
    nk, s, kb = a.shape
    t = min(ROW_TILE, s)

    def body(a_ref, b_ref, x_ref, gt_ref, xo_ref, y_ref):
        y = _dot(a_ref[0], b_ref[0])
        for k in range(1, nk):
            y = y + _dot(a_ref[k], b_ref[k])
        y_ref[...] = y.astype(BF16)
        xo_ref[...] = x_ref[...] + coef * gt_ref[...] * y

    return _call(
        body, name=name, grid=(s // t,),
        in_specs=[pl.BlockSpec((nk, t, kb), lambda i: (0, i, 0)),
                  pl.BlockSpec((nk, kb, D_MODEL), lambda i: (0, 0, 0)),
                  _row_spec(t), _vec_spec()],
        out_specs=[_row_spec(t), _row_spec(t)],
        out_shape=[jax.ShapeDtypeStruct((s, D_MODEL), F32), jax.ShapeDtypeStruct((s, D_MODEL), BF16)],
        operands=(a, b, x, gate), semantics=("parallel",), exchange=exchange)


def _resident(shape):
    return pl.BlockSpec(shape, lambda i: (0,) * len(shape), pipeline_mode=pl.Buffered(1))


def ffn_tokens_bwd(dy, w_down, slopes, w_gu_t, name, exchange=None):
    s = dy.shape[0]
    t = min(ROW_TILE, s)
    with_du = w_gu_t is not None

    def body(*refs):
        if with_du:
            dy_ref, wd_ref, slope_ref, wgu_ref, dgu_ref, du_ref = refs
        else:
            dy_ref, wd_ref, slope_ref, dgu_ref = refs
        dyv = dy_ref[...]
        du = None
        for j in range(4):
            dact = _dot(dyv, wd_ref[j], NT)
            for half in range(2):
                d = (dact * slope_ref[half, j].astype(F32)).astype(BF16)
                dgu_ref[half, j] = d
                if with_du:
                    part = _dot(d, wgu_ref[4 * half + j])
                    du = part if du is None else du + part
        if with_du:
            du_ref[...] = du

    blocks = pl.BlockSpec((2, 4, t, FF_BLK), lambda i: (0, 0, i, 0))
    dgu_shape = jax.ShapeDtypeStruct((2, 4, s, FF_BLK), BF16)
    if with_du:
        return _call(
            body, name=name, grid=(s // t,),
            in_specs=[_row_spec(t), _resident(w_down.shape), blocks, _resident(w_gu_t.shape)],
            out_specs=[blocks, _row_spec(t)],
            out_shape=[dgu_shape, jax.ShapeDtypeStruct((s, D_MODEL), F32)],
            operands=(dy, w_down, slopes, w_gu_t), semantics=("parallel",), exchange=exchange)
    return _call(
        body, name=name, grid=(s // t,),
        in_specs=[_row_spec(t), _resident(w_down.shape), blocks], out_specs=blocks, out_shape=dgu_shape,
        operands=(dy, w_down, slopes), semantics=("parallel",), exchange=exchange)


def matmul_nt_acc(a, b, name, b_dims=NT, exchange=None):
    nk = b.shape[0]
    d, n = (b.shape[1], b.shape[2]) if b_dims == NT else (b.shape[2], b.shape[1])
    s = a.shape[-2]
    t = min(ROW_TILE, s)
    by_columns = a.ndim == 2

    def body(a_ref, b_ref, o_ref):
        def a_blk(k):
            return a_ref[:, k * n:(k + 1) * n] if by_columns else a_ref[k]

        acc = _dot(a_blk(0), b_ref[0], b_dims)
        for k in range(1, nk):
            acc = acc + _dot(a_blk(k), b_ref[k], b_dims)
        o_ref[...] = acc

    a_spec = _row_spec(t, nk * n) if by_columns else pl.BlockSpec((nk, t, n), lambda i: (0, i, 0))
    return _call(
        body, name=name, grid=(s // t,),
        in_specs=[a_spec, pl.BlockSpec(b.shape, lambda i: (0, 0, 0))],
        out_specs=pl.BlockSpec((t, d), lambda i: (i, 0)),
        out_shape=jax.ShapeDtypeStruct((s, d), F32),
        operands=(a, b), semantics=("parallel",), exchange=exchange)


def matmul_tn(a, b, name, group=(1, 1), b_cols=None, exchange=None):
    ja, s, m = a.shape
    by_columns = b.ndim == 2
    jb, n = (b.shape[1] // b_cols, b_cols) if by_columns else (b.shape[0], b.shape[2])
    ga, gb = group
    t = min(ACC_TILE, s)
    nk = s // t

    def body(a_ref, b_ref, o_ref, acc_ref):
        k = pl.program_id(2)

        @pl.when(k == 0)
        def _():
            acc_ref[...] = jnp.zeros_like(acc_ref)

        for p in range(ga):
            for q in range(gb):
                b_blk = b_ref[:, q * n:(q + 1) * n] if by_columns else b_ref[q]
                acc_ref[p, q] += _dot(a_ref[p], b_blk, TN)

        @pl.when(k == nk - 1)
        def _():
            o_ref[...] = acc_ref[...].astype(BF16)

    return _call(
        body, name=name, grid=(ja // ga, jb // gb, nk),
        in_specs=[pl.BlockSpec((ga, t, m), lambda p, q, k: (p, k, 0)),
                  pl.BlockSpec((t, gb * n), lambda p, q, k: (k, q)) if by_columns
                  else pl.BlockSpec((gb, t, n), lambda p, q, k: (q, k, 0))],
        out_specs=pl.BlockSpec((ga, gb, m, n), lambda p, q, k: (p, q, 0, 0)),
        out_shape=jax.ShapeDtypeStruct((ja, jb, m, n), BF16),
        operands=(a, b), scratch_shapes=[pltpu.VMEM((ga, gb, m, n), F32)],
        semantics=("parallel", "parallel", "arbitrary"), exchange=exchange)


def mix_in_proj(u, w_mix, name, exchange=None):
    s = u.shape[0]
    t = min(ROW_TILE, s)

    def body(u_ref, w_ref, o_ref):
        uv = u_ref[...]
        for j in range(N_DEV):
            o_ref[:, j * MIX_BLK:(j + 1) * MIX_BLK] = _dot(uv, w_ref[j]).astype(BF16)

    return _call(
        body, name=name, grid=(s // t,),
        in_specs=[_row_spec(t), pl.BlockSpec((N_DEV, D_MODEL, MIX_BLK), lambda i: (0, 0, 0))],
        out_specs=_row_spec(t, MIX_W),
        out_shape=jax.ShapeDtypeStruct((s, MIX_W), BF16),
        operands=(u, w_mix), semantics=("parallel",), exchange=exchange)


def _conv_taps(cc_ref, cx_ref, s):
    v = cc_ref[...].astype(F32) * cx_ref[...].astype(F32)
    tok = lax.broadcasted_iota(jnp.int32, v.shape, 0)
    v1 = jnp.where(tok >= 1, pltpu.roll(v, 1, 0), 0.0)
    v2 = jnp.where(tok >= 2, pltpu.roll(v, 2, 0), 0.0)
    return v, v1, v2, tok


def _proj_cols(s, first):
    return pl.BlockSpec((s, 128), lambda j: (0, first + j))


def short_conv(proj, conv_w, name):
    s = proj.shape[0]

    def body(cb_ref, cc_ref, cx_ref, w_ref, o_ref):
        v, v1, v2, _ = _conv_taps(cc_ref, cx_ref, s)
        y = w_ref[0:1, :] * v2 + w_ref[1:2, :] * v1 + w_ref[2:3, :] * v
        o_ref[...] = (cb_ref[...].astype(F32) * y).astype(BF16)

    return pl.pallas_call(
        body, name=name, grid=(CONV_W // 128,),
        in_specs=[_proj_cols(s, 0), _proj_cols(s, 4), _proj_cols(s, 8),
                  pl.BlockSpec((3, 128), lambda j: (0, j))],
        out_specs=pl.BlockSpec((s, 128), lambda j: (0, j)),
        out_shape=jax.ShapeDtypeStruct((s, CONV_W), BF16),
        compiler_params=_params("parallel"),
    )(proj, proj, proj, conv_w)


def short_conv_bwd(dsa, proj, conv_w, name):
    s = proj.shape[0]

    def body(dsa_ref, cb_ref, cc_ref, cx_ref, w_ref, dcb_ref, dcc_ref, dcx_ref, dw_ref):
        v, v1, v2, tok = _conv_taps(cc_ref, cx_ref, s)
        w0, w1, w2 = w_ref[0:1, :], w_ref[1:2, :], w_ref[2:3, :]
        y = w0 * v2 + w1 * v1 + w2 * v
        dsa_v = dsa_ref[...].astype(F32)
        dcb_ref[...] = (dsa_v * y).astype(BF16)
        dy = dsa_v * cb_ref[...].astype(F32)
        dw_ref[0:1, :] = jnp.sum(dy * v2, axis=0, keepdims=True)
        dw_ref[1:2, :] = jnp.sum(dy * v1, axis=0, keepdims=True)
        dw_ref[2:3, :] = jnp.sum(dy * v, axis=0, keepdims=True)
        dy1 = jnp.where(tok < s - 1, pltpu.roll(dy, s - 1, 0), 0.0)
        dy2 = jnp.where(tok < s - 2, pltpu.roll(dy, s - 2, 0), 0.0)
        dv = w2 * dy + w1 * dy1 + w0 * dy2
        dcc_ref[...] = (dv * cx_ref[...].astype(F32)).astype(BF16)
        dcx_ref[...] = (dv * cc_ref[...].astype(F32)).astype(BF16)

    col = pl.BlockSpec((s, 128), lambda j: (0, j))
    act = jax.ShapeDtypeStruct((s, CONV_W), BF16)
    return pl.pallas_call(
        body, name=name, grid=(CONV_W // 128,),
        in_specs=[col, _proj_cols(s, 0), _proj_cols(s, 4), _proj_cols(s, 8),
                  pl.BlockSpec((3, 128), lambda j: (0, j))],
        out_specs=[col, col, col, pl.BlockSpec((3, 128), lambda j: (0, j))],
        out_shape=[act, act, act, jax.ShapeDtypeStruct((3, CONV_W), F32)],
        compiler_params=_params("parallel"),
    )(dsa, proj, proj, proj, conv_w)


def _gate_specs(t):
    return [pl.BlockSpec((t, D_MODEL), lambda i: (i, 3)), pl.BlockSpec((t, D_MODEL), lambda i: (i, 4))]


def merge_forward(sa, o, proj, w_co, w_ao, b_merge, name, exchange=None):
    s = sa.shape[0]
    t = min(ROW_TILE, s)

    def body(sa_ref, o_ref, ga_ref, gb_ref, wco_ref, wao_ref, bm_ref, mg_ref, ya_ref, yb_ref):
        ya = _dot(sa_ref[...], wco_ref[...])
        yb = _dot(o_ref[...], wao_ref[...])
        sga = _sigmoid(ga_ref[...].astype(F32) + bm_ref[0:1, :])
        sgb = _sigmoid(gb_ref[...].astype(F32) + bm_ref[1:2, :])
        mg_ref[...] = (sga * ya + sgb * yb).astype(BF16)
        ya_ref[...] = ya.astype(BF16)
        yb_ref[...] = yb.astype(BF16)

    act = jax.ShapeDtypeStruct((s, D_MODEL), BF16)
    return _call(
        body, name=name, grid=(s // t,),
        in_specs=[_row_spec(t, CONV_W), _row_spec(t, ATTN_W)] + _gate_specs(t)
        + [_vec_spec(CONV_W), _vec_spec(ATTN_W), _vec_spec(2)],
        out_specs=[_row_spec(t)] * 3, out_shape=[act, act, act],
        operands=(sa, o, proj, proj, w_co, w_ao, b_merge), semantics=("parallel",), exchange=exchange)


def merge_backward(dy, w_out, proj, ya, yb, b_merge, name, exchange=None):
    s = dy.shape[0]
    t = min(ROW_TILE, s)

    def body(dy_ref, w_ref, ga_ref, gb_ref, ya_ref, yb_ref, bm_ref,
             dya_ref, dyb_ref, dga_ref, dgb_ref, dbm_ref):
        @pl.when(pl.program_id(0) == 0)
        def _():
            dbm_ref[...] = jnp.zeros_like(dbm_ref)

        dmg = _dot(dy_ref[...], w_ref[...], NT)
        sga = _sigmoid(ga_ref[...].astype(F32) + bm_ref[0:1, :])
        sgb = _sigmoid(gb_ref[...].astype(F32) + bm_ref[1:2, :])
        dya_ref[...] = (dmg * sga).astype(BF16)
        dyb_ref[...] = (dmg * sgb).astype(BF16)
        dga = dmg * ya_ref[...].astype(F32) * sga * (1.0 - sga)
        dgb = dmg * yb_ref[...].astype(F32) * sgb * (1.0 - sgb)
        dga_ref[...] = dga.astype(BF16)
        dgb_ref[...] = dgb.astype(BF16)
        dbm_ref[0:1, :] += jnp.sum(dga, axis=0, keepdims=True)
        dbm_ref[1:2, :] += jnp.sum(dgb, axis=0, keepdims=True)

    act = jax.ShapeDtypeStruct((s, D_MODEL), BF16)
    return _call(
        body, name=name, grid=(s // t,),
        in_specs=[_row_spec(t), _vec_spec(D_MODEL)] + _gate_specs(t)
        + [_row_spec(t), _row_spec(t), _vec_spec(2)],
        out_specs=[_row_spec(t)] * 4 + [_vec_spec(2)],
        out_shape=[act] * 4 + [jax.ShapeDtypeStruct((2, D_MODEL), F32)],
        operands=(dy, w_out, proj, proj, ya, yb, b_merge), semantics=("arbitrary",), exchange=exchange)


def out_proj_bwd(dya, dyb, w_co, w_ao, name):
    s = dya.shape[0]
    t = min(ROW_TILE, s)

    def body(dya_ref, dyb_ref, wco_ref, wao_ref, dsa_ref, do_ref):
        dsa_ref[...] = _dot(dya_ref[...], wco_ref[...], NT).astype(BF16)
        do_ref[...] = _dot(dyb_ref[...], wao_ref[...], NT).astype(BF16)

    return pl.pallas_call(
        body, name=name, grid=(s // t,),
        in_specs=[_row_spec(t), _row_spec(t), _vec_spec(CONV_W), _vec_spec(ATTN_W)],
        out_specs=[_row_spec(t, CONV_W), _row_spec(t, ATTN_W)],
        out_shape=[jax.ShapeDtypeStruct((s, CONV_W), BF16), jax.ShapeDtypeStruct((s, ATTN_W), BF16)],
        compiler_params=_params("parallel"),
    )(dya, dyb, w_co, w_ao)


ATT_HEADS = 4
ATT_LANES = ATT_HEADS * HEAD_DIM
ATT_UNDERFLOW = 110.0


def _softplus(z):
    return jnp.maximum(z, 0.0) + jnp.log(1.0 + jnp.exp(-jnp.abs(z)))


def _head_masks(rows):
    lane = lax.broadcasted_iota(jnp.int32, (rows, ATT_LANES), 1)
    return [(lane >= h * HEAD_DIM) & (lane < (h + 1) * HEAD_DIM) for h in range(ATT_HEADS)]


def _per_head(x, masks):
    return [jnp.where(m, x, jnp.zeros_like(x)) for m in masks]


def _att_specs(s, blk):
    first = {"q": 3 * CONV_W // ATT_LANES, "k": (3 * CONV_W + ATTN_W) // ATT_LANES,
             "v": (3 * CONV_W + 2 * ATTN_W) // ATT_LANES}
    return [pl.BlockSpec((blk, ATT_LANES), lambda h, i: (i, first["q"] + h)),
            pl.BlockSpec((s, ATT_LANES), lambda h, i: (0, first["k"] + h)),
            pl.BlockSpec((s, ATT_LANES), lambda h, i: (0, first["v"] + h))]


def _head_norms(x, masks):
    sq = jnp.square(x.astype(F32))
    return [jnp.sum(jnp.where(m, sq, 0.0), axis=1, keepdims=True) for m in masks]


def stick_breaking_fwd(proj, name, exchange=None):
    s = proj.shape[0]
    blk = ATT_BLK
    nq = s // blk

    def body(q_ref, k_ref, v_ref, o_ref, tot_ref, first_ref, kmax_ref):
        i = pl.program_id(1)
        row = lax.broadcasted_iota(jnp.int32, (blk, blk), 0)
        col = lax.broadcasted_iota(jnp.int32, (blk, blk), 1)
        tri = (row >= col).astype(BF16)
        causal = col < row
        masks = _head_masks(blk)
        q_all = q_ref[...] * ATTN_SCALE
        qs = _per_head(q_all, masks)

        @pl.when(i == 0)
        def _():
            def longest(n, best):
                norms = _head_norms(k_ref[pl.ds(pl.multiple_of(n * blk, blk), blk), :], masks)
                return tuple(jnp.maximum(b, v) for b, v in zip(best, norms))

            best = lax.fori_loop(0, nq, longest, tuple(jnp.zeros((blk, 1), F32) for _ in range(ATT_HEADS)))
            for h in range(ATT_HEADS):
                kmax_ref[h] = jnp.sqrt(jnp.max(best[h], axis=0, keepdims=True))

        needed = [jnp.sqrt(n) * kmax_ref[h] + ATT_UNDERFLOW for h, n in enumerate(_head_norms(q_all, masks))]

        def finished(laters):
            slack = laters[0] - needed[0]
            for h in range(1, ATT_HEADS):
                slack = jnp.minimum(slack, laters[h] - needed[h])
            return (jnp.min(slack) >= 0.0).astype(jnp.int32)

        def step(j, carry, diagonal):
            laters, acc = carry
            rows = pl.ds(pl.multiple_of(j * blk, blk), blk)
            kb = k_ref[rows, :]
            probs, new_laters = [], []
            for h in range(ATT_HEADS):
                z = _dot(qs[h], kb, NT)
                sp = _softplus(z)
                if diagonal:
                    sp = jnp.where(causal, sp, 0.0)
                a = jnp.exp(z - (_dot(sp.astype(BF16), tri) + laters[h]))
                if diagonal:
                    a = jnp.where(causal, a, 0.0)
                probs.append(a.astype(BF16))
                new_laters.append(laters[h] + jnp.sum(sp, axis=1, keepdims=True))
            v_heads = jnp.concatenate(_per_head(v_ref[rows, :], masks), axis=0)
            acc = acc + _dot(jnp.concatenate(probs, axis=1), v_heads)
            return tuple(new_laters), acc

        carry = (tuple(jnp.zeros((blk, 1), F32) for _ in range(ATT_HEADS)), jnp.zeros((blk, ATT_LANES), F32))
        laters, acc = step(i, carry, True)

        def further(state):
            n, _, laters, acc = state
            laters, acc = step(i - 1 - n, (laters, acc), False)
            return n + 1, finished(laters), laters, acc

        walked, _, laters, acc = lax.while_loop(
            lambda state: jnp.logical_and(state[0] < i, state[1] == 0), further,
            (jnp.int32(0), finished(laters), laters, acc))
        o_ref[...] = acc.astype(BF16)
        tot = jnp.zeros((blk, ATT_LANES), F32)
        for h in range(ATT_HEADS):
            tot = jnp.where(masks[h], laters[h], tot)
        tot_ref[...] = tot
        first_ref[...] = jnp.full(first_ref.shape, i - walked, jnp.int32).astype(F32)

    out_spec = pl.BlockSpec((blk, ATT_LANES), lambda h, i: (i, h))
    groups = N_HEADS // ATT_HEADS
    return _call(
        body, name=name, grid=(groups, nq),
        in_specs=_att_specs(s, blk),
        out_specs=[out_spec, out_spec, pl.BlockSpec((None, None, 8, 128), lambda h, i: (h, i, 0, 0))],
        out_shape=[jax.ShapeDtypeStruct((s, ATTN_W), BF16), jax.ShapeDtypeStruct((s, ATTN_W), F32),
                   jax.ShapeDtypeStruct((groups, nq, 8, 128), F32)],
        operands=(proj, proj, proj), scratch_shapes=[pltpu.VMEM((ATT_HEADS, 1, 1), F32)],
        semantics=("parallel", "arbitrary"), exchange=exchange)


def stick_breaking_bwd(proj, do, tot, first, name, exchange=None):
    s = proj.shape[0]
    blk = ATT_BLK
    nq = s // blk

    def body(q_ref, k_ref, v_ref, do_ref, tot_ref, first_ref, dq_ref, dk_ref, dv_ref):
        i = pl.program_id(1)
        start = jnp.clip(jnp.max(first_ref[...]).astype(jnp.int32), 0, i)

        @pl.when(i == 0)
        def _():
            dk_ref[...] = jnp.zeros_like(dk_ref)
            dv_ref[...] = jnp.zeros_like(dv_ref)

        row = lax.broadcasted_iota(jnp.int32, (blk, blk), 0)
        col = lax.broadcasted_iota(jnp.int32, (blk, blk), 1)
        before = (row < col).astype(BF16)
        upto = (row <= col).astype(BF16)
        causal = col < row
        masks = _head_masks(blk)
        qs = _per_head(q_ref[...] * ATTN_SCALE, masks)
        dos = _per_head(do_ref[...], masks)
        q_heads = jnp.concatenate(qs, axis=0)
        do_heads = jnp.concatenate(dos, axis=0)
        tot_all = tot_ref[...]
        totals = [jnp.max(jnp.where(m, tot_all, 0.0), axis=1, keepdims=True) for m in masks]

        def step(j, carry, diagonal):
            earliers, g_sums, dq = carry
            rows = pl.ds(pl.multiple_of(j * blk, blk), blk)
            kb = k_ref[rows, :]
            vb = v_ref[rows, :]
            probs, dzs, new_earliers, new_g_sums = [], [], [], []
            for h in range(ATT_HEADS):
                z = _dot(qs[h], kb, NT)
                sp = _softplus(z)
                if diagonal:
                    sp = jnp.where(causal, sp, 0.0)
                c = (totals[h] - earliers[h]) - _dot(sp.astype(BF16), before)
                a = jnp.exp(z - c)
                if diagonal:
                    a = jnp.where(causal, a, 0.0)
                g = a * _dot(dos[h], vb, NT)
                f = g_sums[h] + _dot(g.astype(BF16), upto)
                dz = g - jnp.exp(z - sp) * f
                if diagonal:
                    dz = jnp.where(causal, dz, 0.0)
                probs.append(a.astype(BF16))
                dzs.append(dz.astype(BF16))
                new_earliers.append(earliers[h] + jnp.sum(sp, axis=1, keepdims=True))
                new_g_sums.append(g_sums[h] + jnp.sum(g, axis=1, keepdims=True))
            k_heads = jnp.concatenate(_per_head(kb, masks), axis=0)
            dq = dq + _dot(jnp.concatenate(dzs, axis=1), k_heads)
            dk_ref[rows, :] += _dot(jnp.concatenate(dzs, axis=0), q_heads, TN)
            dv_ref[rows, :] += _dot(jnp.concatenate(probs, axis=0), do_heads, TN)
            return tuple(new_earliers), tuple(new_g_sums), dq

        zeros = tuple(jnp.zeros((blk, 1), F32) for _ in range(ATT_HEADS))
        carry = (zeros, zeros, jnp.zeros((blk, ATT_LANES), F32))
        carry = lax.fori_loop(start, i, lambda j, c: step(j, c, False), carry)
        dq = step(i, carry, True)[2]
        dq_ref[...] = (dq * ATTN_SCALE).astype(BF16)

    blk_spec = pl.BlockSpec((blk, ATT_LANES), lambda h, i: (i, h))
    full_spec = pl.BlockSpec((s, ATT_LANES), lambda h, i: (0, h))
    return _call(
        body, name=name, grid=(N_HEADS // ATT_HEADS, nq),
        in_specs=_att_specs(s, blk) + [blk_spec, blk_spec,
                                       pl.BlockSpec((None, None, 8, 128), lambda h, i: (h, i, 0, 0))],
        out_specs=[blk_spec, full_spec, full_spec],
        out_shape=[jax.ShapeDtypeStruct((s, ATTN_W), BF16), jax.ShapeDtypeStruct((s, ATTN_W), F32),
                   jax.ShapeDtypeStruct((s, ATTN_W), F32)],
        operands=(proj, proj, proj, do, tot, first), semantics=("parallel", "arbitrary"), exchange=exchange)


def adamw(w, m, v, parts, name):
    r, c = w.shape
    p = parts.shape[0]
    t = r
    for cand in (256, 176):
        if r % cand == 0 and r > cand:
            t = cand
            break

    def body(w_ref, m_ref, v_ref, p_ref, g_ref, d_ref, mo_ref, vo_ref):
        g = p_ref[0].astype(F32)
        for n in range(1, p):
            g = g + p_ref[n].astype(F32)
        m_new = ADAM_B1 * m_ref[...] + (1.0 - ADAM_B1) * g
        v_new = ADAM_B2 * v_ref[...] + (1.0 - ADAM_B2) * (g * g)
        m_hat = m_new / ADAM_BC1
        v_hat = v_new / ADAM_BC2
        g_ref[...] = g
        d_ref[...] = -ADAM_LR * (m_hat / (jnp.sqrt(v_hat) + ADAM_EPS) + ADAM_WD * w_ref[...])
        mo_ref[...] = m_new
        vo_ref[...] = v_new

    spec = pl.BlockSpec((t, c), lambda i: (i, 0))
    out = jax.ShapeDtypeStruct((r, c), F32)
    return pl.pallas_call(
        body, name=name, grid=(r // t,),
        in_specs=[spec, spec, spec, pl.BlockSpec((p, t, c), lambda i: (0, i, 0))],
        out_specs=[spec] * 4, out_shape=[out] * 4,
        compiler_params=_params("parallel"),
    )(w, m, v, parts)


def kernel(x, c, w_ada, b_ada, norm1_g, ffn1_w_gu, ffn1_w_down, norm2_g, w_mix_in, b_merge, conv_w, w_conv_out, w_attn_out, w_out, norm3_g, ffn2_w_gu, ffn2_w_down, final_g, loss_target, m_w_ada, m_b_ada, m_norm1_g, m_ffn1_w_gu, m_ffn1_w_down, m_norm2_g, m_w_mix_in, m_b_merge, m_conv_w, m_w_conv_out, m_w_attn_out, m_w_out, m_norm3_g, m_ffn2_w_gu, m_ffn2_w_down, m_final_g, v_w_ada, v_b_ada, v_norm1_g, v_ffn1_w_gu, v_ffn1_w_down, v_norm2_g, v_w_mix_in, v_b_merge, v_conv_w, v_w_conv_out, v_w_attn_out, v_w_out, v_norm3_g, v_ffn2_w_gu, v_ffn2_w_down, v_final_g):
    s = x.shape[1]
    me = 4 * lax.axis_index("x") + 2 * lax.axis_index("y") + lax.axis_index("c")
    x0 = x[0]
    target = loss_target[0]
    final_g2 = final_g.reshape(1, D_MODEL)

    def shard(w):
        return w[0].astype(BF16)

    def flipped(w):
        return jnp.swapaxes(w, 1, 2)

    def rows8(g):
        return g.reshape(N_DEV, -1, D_MODEL)

    small_in = jnp.concatenate([c.reshape(-1), b_merge.reshape(-1), conv_w.reshape(-1),
                                jnp.zeros((64,), F32)]).reshape(1, -1)
    n_ada = w_ada.shape[2]
    b_cols = lax.dynamic_slice(b_ada, (0, me * n_ada), (1, n_ada))
    small_all, mod_all, got = prologue(small_in, w_ada[0], b_cols,
                                       [shard(flipped(ffn1_w_gu)), shard(ffn1_w_down)], "prologue")
    wgu1, wd1 = run_exchange(gather_stage2(got), "gather_ffn1_cores")
    wd1 = wd1.reshape(4, FF_BLK, D_MODEL)
    small_all = small_all[:, 0, :]
    c_all = small_all[:, :D_MODEL]
    bm_full = small_all[:, 1024:1280].reshape(8, 2, 128).transpose(1, 0, 2).reshape(2, D_MODEL)
    cw_full = small_all[:, 1280:1472].reshape(8, 3, 64).transpose(1, 0, 2).reshape(3, CONV_W)
    mod = lax.dynamic_index_in_dim(mod_all, me, axis=1, keepdims=False).reshape(9, 1, D_MODEL)
    sh1, sc1, gt1, sh2, sc2, gt2, sh3, sc3, gt3 = [mod[n] for n in range(9)]

    u1 = norm_modulate(x0, norm1_g, sh1, sc1, "norm_mod_1")
    (gu1, act1), got = ffn_up(u1, wgu1, "ffn_up_1", exchange=gather_stage1([shard(w_mix_in)]))
    (x1, y1), (wmix, *got) = residual_matmul(
        act1, wd1, x0, gt1, 0.5, "ffn_down_1", exchange=merge_exchanges(
            gather_stage2(got), gather_stage1([shard(w_conv_out), shard(w_attn_out), shard(w_out)])))

    u2 = norm_modulate(x1, norm2_g, sh2, sc2, "norm_mod_2")
    proj, (wco, wao, wout) = mix_in_proj(u2, wmix, "mix_in", exchange=gather_stage2(got))
    wco = wco.transpose(1, 0, 2).reshape(CONV_W, D_MODEL)
    wao = wao.transpose(1, 0, 2).reshape(ATTN_W, D_MODEL)
    wout = wout.reshape(D_MODEL, D_MODEL)
    sa = short_conv(proj, cw_full, "short_conv")
    (o, tot, first), got = stick_breaking_fwd(proj, "attn_fwd",
                                       exchange=gather_stage1([shard(flipped(ffn2_w_gu)), shard(ffn2_w_down)]))
    (merged, ya, yb), (wgu3, wd3) = merge_forward(sa, o, proj, wco, wao, bm_full, "merge",
                                                  exchange=gather_stage2(got))
    wd3 = wd3.reshape(4, FF_BLK, D_MODEL)
    x2, y2 = residual_matmul(merged[None], wout[None], x1, gt2, 1.0, "out_proj")

    u3 = norm_modulate(x2, norm3_g, sh3, sc3, "norm_mod_3")
    gu3, act3 = ffn_up(u3, wgu3, "ffn_up_3")
    x3, y3 = residual_matmul(act3, wd3, x2, gt3, 0.5, "ffn_down_3")

    dx3, dy3, dgt3, dfinal, sq = loss_head(x3, target, final_g2, gt3, y3, 0.5, "loss_head")
    dgu3, du3 = ffn_tokens_bwd(dy3, wd3, gu3, wgu3, "ffn_bwd_3")
    dgu3 = dgu3.reshape(8, s, FF_BLK)
    g_wd3 = rows8(matmul_tn(act3, dy3[None], "grad_w_down_3", group=(4, 1)))
    g_wgu3 = matmul_tn(dgu3, u3[None], "grad_w_gu_3", group=(4, 1)).reshape(8, FF_BLK, D_MODEL)
    dx2, dsh3, dsc3, dn3, dy2, dgt2 = norm_modulate_bwd(du3, x2, dx3, norm3_g, sc3, "norm_bwd_3",
                                                        prev=(gt2, y2, 1.0))

    (dya, dyb, dga, dgb, dbm), pairs = merge_backward(dy2, wout, proj, ya, yb, bm_full, "merge_bwd",
                                                      exchange=scatter_stage1([g_wgu3, g_wd3]))
    sums3 = [pair_sum(g_wgu3, pairs[0], "pair_sum_w_gu_3"), pair_sum(g_wd3, pairs[1], "pair_sum_w_down_3")]
    g_wout = rows8(matmul_tn(merged[None], dy2[None], "grad_w_out"))
    dsa, do = out_proj_bwd(dya, dyb, wco, wao, "out_proj_bwd")
    g_wco = matmul_tn(sa[None], dya[None], "grad_w_conv_out").reshape(CONV_W, N_DEV, 128).transpose(1, 0, 2)
    g_wao = matmul_tn(o[None], dyb[None], "grad_w_attn_out").reshape(ATTN_W, N_DEV, 128).transpose(1, 0, 2)
    dcb, dcc, dcx, dconv = short_conv_bwd(dsa, proj, cw_full, "short_conv_bwd")
    (dq, dk, dv), landed3 = stick_breaking_bwd(proj, do, tot, first, "attn_bwd", exchange=scatter_stage2(sums3))
    dproj = jnp.concatenate([dcb, dcc, dcx, dq, dk.astype(BF16), dv.astype(BF16), dga, dgb], axis=1)
    du2 = matmul_nt_acc(dproj, wmix, "mix_in_du")
    g_wmix = matmul_tn(u2[None], dproj, "grad_w_mix_in", group=(1, 4), b_cols=MIX_BLK).reshape(
        N_DEV, D_MODEL, MIX_BLK)
    mixer_grads = [g_wmix, g_wco, g_wao, g_wout]
    (dx1, dsh2, dsc2, dn2, dy1, dgt1), pairs = norm_modulate_bwd(
        du2, x1, dx2, norm2_g, sc2, "norm_bwd_2", prev=(gt1, y1, 0.5), exchange=scatter_stage1(mixer_grads))
    sums_mix = [pair_sum(g, p, f"pair_sum_mixer_{n}") for n, (g, p) in enumerate(zip(mixer_grads, pairs))]

    dgu1, landed_mix = ffn_tokens_bwd(dy1, wd1, gu1, None, "ffn_dact_1", exchange=scatter_stage2(sums_mix[:1]))
    dgu1 = dgu1.reshape(8, s, FF_BLK)
    g_wgu1, landed_small = matmul_tn(dgu1, u1[None], "grad_w_gu_1", group=(4, 1),
                                     exchange=scatter_stage2(sums_mix[1:]))
    g_wgu1 = g_wgu1.reshape(8, FF_BLK, D_MODEL)
    g_wd1, pairs = matmul_tn(act1, dy1[None], "grad_w_down_1", group=(4, 1), exchange=scatter_stage1([g_wgu1]))
    g_wd1 = rows8(g_wd1)
    sum_gu1 = pair_sum(g_wgu1, pairs[0], "pair_sum_w_gu_1")
    du1, (landed_gu1, pair_d1) = matmul_nt_acc(
        dgu1, wgu1, "ffn_du_1", b_dims=NN, exchange=merge_exchanges(scatter_stage2([sum_gu1]), scatter_stage1([g_wd1])))
    sum_d1 = pair_sum(g_wd1, pair_d1, "pair_sum_w_down_1")
    (grad_x, dsh1, dsc1, dn1), landed_d1 = norm_modulate_bwd(du1, x0, dx1, norm1_g, sc1, "norm_bwd_1",
                                                            exchange=scatter_stage2([sum_d1]))

    loss_local = (0.5 / D_MODEL) * jnp.sum(sq)
    stats = jnp.concatenate(
        [v.reshape(-1) for v in (dsh1, dsc1, dgt1, dsh2, dsc2, dgt2, dsh3, dsc3, dgt3,
                                 dn1, dn2, dn3, dfinal, dbm, dconv)]
        + [jnp.broadcast_to(loss_local, (128,))]).reshape(1, -1)
    stats_all = all_gather_rows(stats, "gather_stats")
    n_mod = 9 * D_MODEL
    loss = jnp.sum(stats_all[:, 0, -1])
    dmod_all = stats_all[:, :, :n_mod]
    off = n_mod
    parts = {}
    for key in ("norm1_g", "norm2_g", "norm3_g", "final_g"):
        parts[key] = stats_all[:, :, off:off + D_MODEL]
        off += D_MODEL
    dbm_all = stats_all[:, 0, off:off + 2 * D_MODEL].reshape(N_DEV, 2, D_MODEL)
    off += 2 * D_MODEL
    dcw_all = stats_all[:, 0, off:off + 3 * CONV_W].reshape(N_DEV, 3, CONV_W)
    parts["b_merge"] = lax.dynamic_slice(dbm_all, (0, 0, me * 128), (N_DEV, 2, 128))
    parts["conv_w"] = lax.dynamic_slice(dcw_all, (0, 0, me * 64), (N_DEV, 3, 64))
    dmod_cols = lax.dynamic_slice(dmod_all[:, 0, :], (0, me * n_ada), (N_DEV, n_ada))
    parts["w_ada"] = ada_backward(c_all, dmod_cols, "ada_backward")[None]
    parts["b_ada"] = dmod_all
    parts["ffn2_w_gu"], parts["ffn2_w_down"] = landed3
    parts["w_mix_in"] = landed_mix[0]
    parts["w_conv_out"], parts["w_attn_out"], parts["w_out"] = landed_small
    parts["ffn1_w_gu"] = landed_gu1
    parts["ffn1_w_down"] = landed_d1[0]

    given = dict(w_ada=w_ada, b_ada=b_ada, norm1_g=norm1_g, ffn1_w_gu=ffn1_w_gu, ffn1_w_down=ffn1_w_down,
                 norm2_g=norm2_g, w_mix_in=w_mix_in, b_merge=b_merge, conv_w=conv_w, w_conv_out=w_conv_out,
                 w_attn_out=w_attn_out, w_out=w_out, norm3_g=norm3_g, ffn2_w_gu=ffn2_w_gu,
                 ffn2_w_down=ffn2_w_down, final_g=final_g)
    moments_m = dict(w_ada=m_w_ada, b_ada=m_b_ada, norm1_g=m_norm1_g, ffn1_w_gu=m_ffn1_w_gu,
                     ffn1_w_down=m_ffn1_w_down, norm2_g=m_norm2_g, w_mix_in=m_w_mix_in, b_merge=m_b_merge,
                     conv_w=m_conv_w, w_conv_out=m_w_conv_out, w_attn_out=m_w_attn_out, w_out=m_w_out,
                     norm3_g=m_norm3_g, ffn2_w_gu=m_ffn2_w_gu, ffn2_w_down=m_ffn2_w_down, final_g=m_final_g)
    moments_v = dict(w_ada=v_w_ada, b_ada=v_b_ada, norm1_g=v_norm1_g, ffn1_w_gu=v_ffn1_w_gu,
                     ffn1_w_down=v_ffn1_w_down, norm2_g=v_norm2_g, w_mix_in=v_w_mix_in, b_merge=v_b_merge,
                     conv_w=v_conv_w, w_conv_out=v_w_conv_out, w_attn_out=v_w_attn_out, w_out=v_w_out,
                     norm3_g=v_norm3_g, ffn2_w_gu=v_ffn2_w_gu, ffn2_w_down=v_ffn2_w_down, final_g=v_final_g)
    order = ["w_ada", "b_ada", "norm1_g", "ffn1_w_gu", "ffn1_w_down", "norm2_g", "w_mix_in", "b_merge",
             "conv_w", "w_conv_out", "w_attn_out", "w_out", "norm3_g", "ffn2_w_gu", "ffn2_w_down", "final_g"]
    grads, deltas, new_m, new_v = [], [], [], []
    for key in order:
        turn = flipped if key in ("ffn1_w_gu", "ffn2_w_gu") else (lambda a: a)
        shape = turn(given[key]).shape
        shape2 = (1, shape[0]) if len(shape) == 1 else shape[-2:]
        outs = adamw(turn(given[key]).reshape(shape2), turn(moments_m[key]).reshape(shape2),
                     turn(moments_v[key]).reshape(shape2), parts[key], f"adamw_{key}")
        for dst, val in zip((grads, deltas, new_m, new_v), outs):
            dst.append(turn(val.reshape(shape)))

    return (loss, grad_x[None], *grads, *deltas, *new_m, *new_v)
```

```python
import functools
from typing import Callable, NamedTuple

import jax
import jax.numpy as jnp
from jax import lax
from jax.experimental import pallas as pl
from jax.experimental.pallas import tpu as pltpu

F32 = jnp.float32
BF16 = jnp.bfloat16
MESH = pl.DeviceIdType.MESH
ANY = pl.BlockSpec(memory_space=pl.ANY)

N_DEV = 8
D_MODEL = 1024
D_FF = 2816
FF_BLK = D_FF // 4
N_HEADS = 8
HEAD_DIM = 64
CONV_W = 512
ATTN_W = 512
MIX_W = 3 * CONV_W + 3 * ATTN_W + 2 * D_MODEL
MIX_BLK = MIX_W // N_DEV
EPS = 1e-6
ATTN_SCALE = HEAD_DIM ** -0.5

ADAM_LR = 0.001
ADAM_B1 = 0.9
ADAM_B2 = 0.999
ADAM_EPS = 1e-08
ADAM_WD = 0.01
ADAM_STEP = 10
ADAM_BC1 = 1.0 - ADAM_B1 ** ADAM_STEP
ADAM_BC2 = 1.0 - ADAM_B2 ** ADAM_STEP

VMEM_LIMIT = 56 * 1024 * 1024
ROW_TILE = 512
ACC_TILE = 1024
ELT_TILE = 256
ATT_BLK = 256

NN = (((1,), (0,)), ((), ()))
NT = (((1,), (1,)), ((), ()))
TN = (((0,), (0,)), ((), ()))


def _dot(a, b, dims=NN):
    return lax.dot_general(a, b, dims, preferred_element_type=F32)


def _params(*sem):
    return pltpu.CompilerParams(dimension_semantics=sem, vmem_limit_bytes=VMEM_LIMIT)


def _sigmoid(x):
    return 1.0 / (1.0 + jnp.exp(-x))


def _me():
    x, y, c = lax.axis_index("x"), lax.axis_index("y"), lax.axis_index("c")
    return x, y, c, 4 * x + 2 * y + c


def _peer(k):
    x, y, c, _ = _me()
    px = 1 - x if (k >> 2) & 1 else x
    py = 1 - y if (k >> 1) & 1 else y
    pc = 1 - c if k & 1 else c
    return (px, py, pc), 4 * px + 2 * py + pc


class Exchange(NamedTuple):
    operands: tuple
    out_shapes: tuple
    aliases: dict
    n_remote: int
    n_local: int
    copies: Callable


CHIP_FLIPS = (2, 4, 6)
SIBLING = 1


def _remote(src, dst, send_sems, recv_sems, n, peer):
    return pltpu.make_async_remote_copy(src_ref=src, dst_ref=dst, send_sem=send_sems.at[n], recv_sem=recv_sems.at[n],
                                        device_id=peer, device_id_type=MESH)


def gather_stage1(shards):
    n = len(shards)
    rels = (SIBLING,) + CHIP_FLIPS

    def copies(ins, outs, send_sems, recv_sems, local_sems, rb, lb):
        _, _, _, me = _me()
        cps = []
        for w in range(n):
            cps.append(pltpu.make_async_copy(ins[w], outs[w].at[me], local_sems.at[lb + w]))
            for a, k in enumerate(rels):
                peer, _ = _peer(k)
                cps.append(_remote(ins[w], outs[w].at[me], send_sems, recv_sems, rb + len(rels) * w + a, peer))
        return cps

    shapes = tuple(jax.ShapeDtypeStruct((N_DEV,) + s.shape, s.dtype) for s in shards)
    return Exchange(tuple(shards), shapes, {}, len(rels) * n, n, copies)


def gather_stage2(fulls):
    n = len(fulls)

    def copies(ins, outs, send_sems, recv_sems, local_sems, rb, lb):
        sibling, _ = _peer(SIBLING)
        cps = []
        for w in range(n):
            for a, k in enumerate(CHIP_FLIPS):
                _, blk = _peer(k)
                cps.append(_remote(outs[w].at[blk], outs[w].at[blk], send_sems, recv_sems, rb + 3 * w + a, sibling))
        return cps

    shapes = tuple(jax.ShapeDtypeStruct(f.shape, f.dtype) for f in fulls)
    return Exchange(tuple(fulls), shapes, {w: w for w in range(n)}, 3 * n, 0, copies)


def scatter_stage1(fulls):
    n = len(fulls)

    def copies(ins, outs, send_sems, recv_sems, local_sems, rb, lb):
        _, _, c, _ = _me()
        sibling, _ = _peer(SIBLING)
        cps = []
        for w in range(n):
            for q in range(4):
                cps.append(_remote(ins[w].at[2 * q + (1 - c)], outs[w].at[q], send_sems, recv_sems, rb + 4 * w + q, sibling))
        return cps

    shapes = tuple(jax.ShapeDtypeStruct((4,) + f.shape[1:], f.dtype) for f in fulls)
    return Exchange(tuple(fulls), shapes, {}, 4 * n, 0, copies)


def scatter_stage2(sums):
    n = len(sums)

    def copies(ins, outs, send_sems, recv_sems, local_sems, rb, lb):
        x, y, _, _ = _me()
        mine = 2 * x + y
        cps = []
        for w in range(n):
            cps.append(pltpu.make_async_copy(ins[w].at[mine], outs[w].at[mine], local_sems.at[lb + w]))
            for a, k in enumerate(CHIP_FLIPS):
                peer, _ = _peer(k)
                cps.append(_remote(ins[w].at[2 * peer[0] + peer[1]], outs[w].at[mine], send_sems, recv_sems,
                                   rb + 3 * w + a, peer))
        return cps

    shapes = tuple(jax.ShapeDtypeStruct(s.shape, s.dtype) for s in sums)
    return Exchange(tuple(sums), shapes, {}, 3 * n, n, copies)


def merge_exchanges(a, b):
    na_in, na_out = len(a.operands), len(a.out_shapes)

    def copies(ins, outs, send_sems, recv_sems, local_sems, rb, lb):
        return (a.copies(ins[:na_in], outs[:na_out], send_sems, recv_sems, local_sems, rb, lb)
                + b.copies(ins[na_in:], outs[na_out:], send_sems, recv_sems, local_sems, rb + a.n_remote, lb + a.n_local))

    aliases = dict(a.aliases)
    aliases.update({na_in + i: na_out + o for i, o in b.aliases.items()})
    return Exchange(a.operands + b.operands, a.out_shapes + b.out_shapes, aliases,
                    a.n_remote + b.n_remote, a.n_local + b.n_local, copies)


def _exchange_scratch(ex):
    return [pltpu.SemaphoreType.DMA((ex.n_remote,)), pltpu.SemaphoreType.DMA((ex.n_remote,)),
            pltpu.SemaphoreType.DMA((max(ex.n_local, 1),))]


def run_exchange(ex, name):
    n_in, n_out = len(ex.operands), len(ex.out_shapes)

    def body(*refs):
        cps = ex.copies(refs[:n_in], refs[n_in:n_in + n_out], *refs[n_in + n_out:], 0, 0)
        for cp in cps:
            cp.start()
        for cp in cps:
            cp.wait()

    return pl.pallas_call(
        body, name=name, out_shape=list(ex.out_shapes), in_specs=[ANY] * n_in, out_specs=[ANY] * n_out,
        scratch_shapes=_exchange_scratch(ex), input_output_aliases=dict(ex.aliases),
    )(*ex.operands)


def _call(body, *, name, grid, in_specs, out_specs, out_shape, operands, scratch_shapes=(), semantics=(),
          exchange=None):
    if exchange is None:
        return pl.pallas_call(
            body, name=name, grid=grid, in_specs=in_specs, out_specs=out_specs, out_shape=out_shape,
            scratch_shapes=list(scratch_shapes), compiler_params=_params(*semantics))(*operands)
    single = not isinstance(out_shape, (list, tuple))
    out_shapes = [out_shape] if single else list(out_shape)
    out_specs_l = [out_specs] if single else list(out_specs)
    n_in, n_out, n_scr = len(operands), len(out_shapes), len(scratch_shapes)
    x_in, x_out = len(exchange.operands), len(exchange.out_shapes)

    def hosted(*refs):
        ins, refs = refs[:n_in], refs[n_in:]
        xin, refs = refs[:x_in], refs[x_in:]
        outs, refs = refs[:n_out], refs[n_out:]
        xout, refs = refs[:x_out], refs[x_out:]
        scr, sems = refs[:n_scr], refs[n_scr:]
        first = functools.reduce(jnp.logical_and, [pl.program_id(a) == 0 for a in range(len(grid))])
        last = functools.reduce(jnp.logical_and, [pl.program_id(a) == g - 1 for a, g in enumerate(grid)])

        @pl.when(first)
        def _():
            for cp in exchange.copies(xin, xout, *sems, 0, 0):
                cp.start()

        body(*ins, *outs, *scr)

        @pl.when(last)
        def _():
            for cp in exchange.copies(xin, xout, *sems, 0, 0):
                cp.wait()

    res = pl.pallas_call(
        hosted, name=name, grid=grid,
        in_specs=list(in_specs) + [ANY] * x_in, out_specs=out_specs_l + [ANY] * x_out,
        out_shape=out_shapes + list(exchange.out_shapes),
        scratch_shapes=list(scratch_shapes) + _exchange_scratch(exchange),
        input_output_aliases={n_in + i: n_out + o for i, o in exchange.aliases.items()},
        compiler_params=_params(*(["arbitrary"] * len(grid))),
    )(*operands, *exchange.operands)
    outs, xouts = res[:n_out], res[n_out:]
    return (outs[0] if single else outs), xouts


def all_gather_rows(v, name):
    r, n = v.shape

    def body(v_ref, out_ref, send_sems, recv_sems):
        _, _, _, me = _me()
        out_ref[me] = v_ref[...]
        copies = []
        for k in range(1, N_DEV):
            peer, _ = _peer(k)
            copies.append(_remote(v_ref, out_ref.at[me], send_sems, recv_sems, k - 1, peer))
        for cp in copies:
            cp.start()
        for cp in copies:
            cp.wait()

    return pl.pallas_call(
        body, name=name,
        out_shape=jax.ShapeDtypeStruct((N_DEV, r, n), v.dtype),
        in_specs=[pl.BlockSpec(memory_space=pltpu.VMEM)],
        out_specs=pl.BlockSpec(memory_space=pltpu.VMEM),
        scratch_shapes=[pltpu.SemaphoreType.DMA((N_DEV - 1,)), pltpu.SemaphoreType.DMA((N_DEV - 1,))],
    )(v)


def pair_sum(full, pair, name):
    _, r, c = full.shape
    t = r
    core = lax.axis_index("c").astype(jnp.int32).reshape(1)

    def body(core_ref, f_ref, p_ref, o_ref):
        o_ref[...] = (f_ref[...].astype(F32) + p_ref[...].astype(F32)).astype(BF16)

    return pl.pallas_call(
        body, name=name,
        grid_spec=pltpu.PrefetchScalarGridSpec(
            num_scalar_prefetch=1, grid=(4, r // t),
            in_specs=[pl.BlockSpec((None, None, t, c), lambda q, i, core_ref: (q, core_ref[0], i, 0)),
                      pl.BlockSpec((None, t, c), lambda q, i, core_ref: (q, i, 0))],
            out_specs=pl.BlockSpec((None, t, c), lambda q, i, core_ref: (q, i, 0))),
        out_shape=jax.ShapeDtypeStruct((4, r, c), BF16),
        compiler_params=_params("parallel", "parallel"),
    )(core, full.reshape(4, 2, r, c), pair)


def prologue(small_in, w_ada, b_cols, shards, name):
    ex = gather_stage1(shards)
    n_sh = len(shards)
    n_small = small_in.shape[1]
    cols = w_ada.shape[1]

    def body(*refs):
        small_ref, w_ref, b_ref = refs[:3]
        shard_refs = refs[3:3 + n_sh]
        small_out, mod_out = refs[3 + n_sh:5 + n_sh]
        fulls = refs[5 + n_sh:5 + 2 * n_sh]
        part_ref, send1, recv1, send2, recv2, wsend, wrecv, wlocal = refs[5 + 2 * n_sh:]
        _, _, _, me = _me()
        big = ex.copies(shard_refs, fulls, wsend, wrecv, wlocal, 0, 0)
        for cp in big:
            cp.start()

        def all_gather(src_ref, dst_ref, send_sems, recv_sems):
            cps = [_remote(src_ref, dst_ref.at[me], send_sems, recv_sems, k - 1, _peer(k)[0]) for k in range(1, N_DEV)]
            for cp in cps:
                cp.start()
            for cp in cps:
                cp.wait()

        small_out[me] = small_ref[...]
        all_gather(small_ref, small_out, send1, recv1)
        c_all = jnp.concatenate([small_out[d][:, :D_MODEL] for d in range(N_DEV)], axis=0)
        act = c_all * _sigmoid(c_all)
        part_ref[...] = jnp.dot(act, w_ref[...], precision=lax.Precision.HIGHEST,
                                preferred_element_type=F32) + b_ref[...]
        mod_out[me] = part_ref[...]
        all_gather(part_ref, mod_out, send2, recv2)
        for cp in big:
            cp.wait()

    vmem = pl.BlockSpec(memory_space=pltpu.VMEM)
    sems = pltpu.SemaphoreType.DMA((N_DEV - 1,))
    res = pl.pallas_call(
        body, name=name,
        out_shape=[jax.ShapeDtypeStruct((N_DEV, 1, n_small), F32), jax.ShapeDtypeStruct((N_DEV, N_DEV, cols), F32)]
        + list(ex.out_shapes),
        in_specs=[vmem, vmem, vmem] + [ANY] * n_sh, out_specs=[vmem, vmem] + [ANY] * n_sh,
        scratch_shapes=[pltpu.VMEM((N_DEV, cols), F32), sems, sems, sems, sems] + _exchange_scratch(ex),
        compiler_params=pltpu.CompilerParams(vmem_limit_bytes=VMEM_LIMIT),
    )(small_in, w_ada, b_cols, *shards)
    return res[0], res[1], res[2:]


def ada_backward(c_all, dmod_cols, name):
    n = dmod_cols.shape[1]

    def body(c_ref, d_ref, o_ref):
        c = c_ref[...]
        act = c * _sigmoid(c)
        o_ref[...] = lax.dot_general(act, d_ref[...], TN, precision=lax.Precision.HIGHEST,
                                     preferred_element_type=F32)

    return pl.pallas_call(
        body, name=name, out_shape=jax.ShapeDtypeStruct((D_MODEL, n), F32),
        compiler_params=pltpu.CompilerParams(vmem_limit_bytes=VMEM_LIMIT),
    )(c_all, dmod_cols)


def _row_spec(t, width=D_MODEL):
    return pl.BlockSpec((t, width), lambda i: (i, 0))


def _vec_spec(rows=1, width=D_MODEL):
    return pl.BlockSpec((rows, width), lambda i: (0, 0))


def _resident(shape):
    return pl.BlockSpec(shape, lambda i: (0,) * len(shape), pipeline_mode=pl.Buffered(1))


def _norm_modulate(x_ref, g_ref, shift_ref, scale_ref):
    xv = x_ref[...]
    r = lax.rsqrt(jnp.mean(xv * xv, axis=-1, keepdims=True) + EPS)
    a = (xv * r) * g_ref[...]
    return (a * (1.0 + scale_ref[...]) + shift_ref[...]).astype(BF16)


def loss_head(x, target, final_g, gate, y_prev, coef, name):
    s = x.shape[0]
    t = min(ELT_TILE, s)

    def body(x_ref, t_ref, fg_ref, gt_ref, y_ref, dx_ref, dy_ref, dgt_ref, dfg_ref, sq_ref):
        @pl.when(pl.program_id(0) == 0)
        def _():
            dgt_ref[...] = jnp.zeros_like(dgt_ref)
            dfg_ref[...] = jnp.zeros_like(dfg_ref)
            sq_ref[...] = jnp.zeros_like(sq_ref)

        xv = x_ref[...]
        fg = fg_ref[...]
        r = lax.rsqrt(jnp.mean(xv * xv, axis=-1, keepdims=True) + EPS)
        nrm = xv * r
        err = nrm * fg - t_ref[...]
        sq_ref[...] += jnp.sum(err * err, axis=0, keepdims=True)
        dout = err * (1.0 / D_MODEL)
        dfg_ref[...] += jnp.sum(dout * nrm, axis=0, keepdims=True)
        dn = dout * fg
        dx = r * (dn - nrm * jnp.mean(dn * nrm, axis=-1, keepdims=True))
        dx_ref[...] = dx
        dy_ref[...] = (coef * gt_ref[...] * dx).astype(BF16)
        dgt_ref[...] += coef * jnp.sum(dx * y_ref[...].astype(F32), axis=0, keepdims=True)

    vec = jax.ShapeDtypeStruct((1, D_MODEL), F32)
    return pl.pallas_call(
        body, name=name, grid=(s // t,),
        in_specs=[_row_spec(t), _row_spec(t), _vec_spec(), _vec_spec(), _row_spec(t)],
        out_specs=[_row_spec(t), _row_spec(t), _vec_spec(), _vec_spec(), _vec_spec()],
        out_shape=[jax.ShapeDtypeStruct((s, D_MODEL), F32), jax.ShapeDtypeStruct((s, D_MODEL), BF16),
                   vec, vec, vec],
        compiler_params=_params("arbitrary"),
    )(x, target, final_g, gate, y_prev)


def norm_modulate_bwd(du, x, dx_out, g, scale, name, prev=None, exchange=None):
    s = x.shape[0]
    t = min(ELT_TILE, s)
    has_prev = prev is not None

    def body(*refs):
        du_ref, x_ref, dxo_ref, g_ref, sc_ref = refs[:5]
        refs = refs[5:]
        if has_prev:
            gt_ref, y_ref = refs[:2]
            refs = refs[2:]
        dx_ref, dsh_ref, dsc_ref, dg_ref = refs[:4]

        @pl.when(pl.program_id(0) == 0)
        def _():
            dsh_ref[...] = jnp.zeros_like(dsh_ref)
            dsc_ref[...] = jnp.zeros_like(dsc_ref)
            dg_ref[...] = jnp.zeros_like(dg_ref)
            if has_prev:
                refs[5][...] = jnp.zeros_like(refs[5])

        xv = x_ref[...]
        duv = du_ref[...]
        gv = g_ref[...]
        r = lax.rsqrt(jnp.mean(xv * xv, axis=-1, keepdims=True) + EPS)
        nrm = xv * r
        a = nrm * gv
        dsh_ref[...] += jnp.sum(duv, axis=0, keepdims=True)
        dsc_ref[...] += jnp.sum(duv * a, axis=0, keepdims=True)
        da = duv * (1.0 + sc_ref[...])
        dg_ref[...] += jnp.sum(da * nrm, axis=0, keepdims=True)
        dn = da * gv
        dx = dxo_ref[...] + r * (dn - nrm * jnp.mean(dn * nrm, axis=-1, keepdims=True))
        dx_ref[...] = dx
        if has_prev:
            coef = prev[2]
            refs[4][...] = (coef * gt_ref[...] * dx).astype(BF16)
            refs[5][...] += coef * jnp.sum(dx * y_ref[...].astype(F32), axis=0, keepdims=True)

    vec = jax.ShapeDtypeStruct((1, D_MODEL), F32)
    operands = [du, x, dx_out, g, scale]
    in_specs = [_row_spec(t), _row_spec(t), _row_spec(t), _vec_spec(), _vec_spec()]
    out_specs = [_row_spec(t), _vec_spec(), _vec_spec(), _vec_spec()]
    out_shape = [jax.ShapeDtypeStruct((s, D_MODEL), F32), vec, vec, vec]
    if has_prev:
        operands += [prev[0], prev[1]]
        in_specs += [_vec_spec(), _row_spec(t)]
        out_specs += [_row_spec(t), _vec_spec()]
        out_shape += [jax.ShapeDtypeStruct((s, D_MODEL), BF16), vec]
    return _call(body, name=name, grid=(s // t,), in_specs=in_specs, out_specs=out_specs, out_shape=out_shape,
                 operands=operands, semantics=("arbitrary",), exchange=exchange)


def ffn_up(x, norm_g, shift, scale, w_gu_t, name, exchange=None):
    s = x.shape[0]
    t = min(ROW_TILE, s)

    def body(x_ref, g_ref, sh_ref, sc_ref, w_ref, u_ref, gu_ref, act_ref):
        uv = _norm_modulate(x_ref, g_ref, sh_ref, sc_ref)
        u_ref[...] = uv
        for j in range(4):
            g = _dot(uv, w_ref[j], NT)
            up = _dot(uv, w_ref[j + 4], NT)
            gu_ref[0, j] = g.astype(BF16)
            gu_ref[1, j] = up.astype(BF16)
            act_ref[j] = (g * _sigmoid(g) * up).astype(BF16)

    return _call(
        body, name=name, grid=(s // t,),
        in_specs=[_row_spec(t), _vec_spec(), _vec_spec(), _vec_spec(), _resident(w_gu_t.shape)],
        out_specs=[_row_spec(t), pl.BlockSpec((2, 4, t, FF_BLK), lambda i: (0, 0, i, 0)),
                   pl.BlockSpec((4, t, FF_BLK), lambda i: (0, i, 0))],
        out_shape=[jax.ShapeDtypeStruct((s, D_MODEL), BF16), jax.ShapeDtypeStruct((2, 4, s, FF_BLK), BF16),
                   jax.ShapeDtypeStruct((4, s, FF_BLK), BF16)],
        operands=(x, norm_g, shift, scale, w_gu_t), semantics=("parallel",), exchange=exchange)


def residual_matmul(a, b, x, gate, coef, name, exchange=None):
    nk, s, kb = a.shape
    t = min(ROW_TILE, s)

    def body(a_ref, b_ref, x_ref, gt_ref, xo_ref, y_ref):
        y = _dot(a_ref[0], b_ref[0])
        for k in range(1, nk):
            y = y + _dot(a_ref[k], b_ref[k])
        y_ref[...] = y.astype(BF16)
        xo_ref[...] = x_ref[...] + coef * gt_ref[...] * y

    return _call(
        body, name=name, grid=(s // t,),
        in_specs=[pl.BlockSpec((nk, t, kb), lambda i: (0, i, 0)),
                  pl.BlockSpec((nk, kb, D_MODEL), lambda i: (0, 0, 0)),
                  _row_spec(t), _vec_spec()],
        out_specs=[_row_spec(t), _row_spec(t)],
        out_shape=[jax.ShapeDtypeStruct((s, D_MODEL), F32), jax.ShapeDtypeStruct((s, D_MODEL), BF16)],
        operands=(a, b, x, gate), semantics=("parallel",), exchange=exchange)


def ffn_tokens_bwd(dy, w_down, gu, w_gu_t, name, exchange=None):
    s = dy.shape[0]
    t = min(ROW_TILE, s)
    with_du = w_gu_t is not None

    def body(*refs):
        if with_du:
            dy_ref, wd_ref, gu_ref, wgu_ref, dgu_ref, du_ref = refs
        else:
            dy_ref, wd_ref, gu_ref, dgu_ref = refs
        dyv = dy_ref[...]
        du = None
        for j in range(4):
            dact = _dot(dyv, wd_ref[j], NT)
            g = gu_ref[0, j].astype(F32)
            up = gu_ref[1, j].astype(F32)
            sg = _sigmoid(g)
            slopes = (up * sg * (1.0 + g * (1.0 - sg)), g * sg)
            for half in range(2):
                d = (dact * slopes[half]).astype(BF16)
                dgu_ref[half, j] = d
                if with_du:
                    part = _dot(d, wgu_ref[4 * half + j])
                    du = part if du is None else du + part
        if with_du:
            du_ref[...] = du

    blocks = pl.BlockSpec((2, 4, t, FF_BLK), lambda i: (0, 0, i, 0))
    dgu_shape = jax.ShapeDtypeStruct((2, 4, s, FF_BLK), BF16)
    if with_du:
        return _call(
            body, name=name, grid=(s // t,),
            in_specs=[_row_spec(t), _resident(w_down.shape), blocks, _resident(w_gu_t.shape)],
            out_specs=[blocks, _row_spec(t)],
            out_shape=[dgu_shape, jax.ShapeDtypeStruct((s, D_MODEL), F32)],
            operands=(dy, w_down, gu, w_gu_t), semantics=("parallel",), exchange=exchange)
    return _call(
        body, name=name, grid=(s // t,),
        in_specs=[_row_spec(t), _resident(w_down.shape), blocks], out_specs=blocks, out_shape=dgu_shape,
        operands=(dy, w_down, gu), semantics=("parallel",), exchange=exchange)


def matmul_nt_acc(a, b, name, b_dims=NT, exchange=None):
    nk = b.shape[0]
    d, n = (b.shape[1], b.shape[2]) if b_dims == NT else (b.shape[2], b.shape[1])
    s = a.shape[-2]
    t = min(ROW_TILE, s)
    by_columns = a.ndim == 2

    def body(a_ref, b_ref, o_ref):
        def a_blk(k):
            return a_ref[:, k * n:(k + 1) * n] if by_columns else a_ref[k]

        acc = _dot(a_blk(0), b_ref[0], b_dims)
        for k in range(1, nk):
            acc = acc + _dot(a_blk(k), b_ref[k], b_dims)
        o_ref[...] = acc

    a_spec = _row_spec(t, nk * n) if by_columns else pl.BlockSpec((nk, t, n), lambda i: (0, i, 0))
    return _call(
        body, name=name, grid=(s // t,),
        in_specs=[a_spec, pl.BlockSpec(b.shape, lambda i: (0, 0, 0))],
        out_specs=pl.BlockSpec((t, d), lambda i: (i, 0)),
        out_shape=jax.ShapeDtypeStruct((s, d), F32),
        operands=(a, b), semantics=("parallel",), exchange=exchange)


def matmul_tn(a, b, name, group=(1, 1), b_cols=None, exchange=None):
    ja, s, m = a.shape
    by_columns = b.ndim == 2
    jb, n = (b.shape[1] // b_cols, b_cols) if by_columns else (b.shape[0], b.shape[2])
    ga, gb = group
    t = min(ACC_TILE, s)
    nk = s // t

    def body(a_ref, b_ref, o_ref, acc_ref):
        k = pl.program_id(2)

        @pl.when(k == 0)
        def _():
            acc_ref[...] = jnp.zeros_like(acc_ref)

        for p in range(ga):
            for q in range(gb):
                b_blk = b_ref[:, q * n:(q + 1) * n] if by_columns else b_ref[q]
                acc_ref[p, q] += _dot(a_ref[p], b_blk, TN)

        @pl.when(k == nk - 1)
        def _():
            o_ref[...] = acc_ref[...].astype(BF16)

    return _call(
        body, name=name, grid=(ja // ga, jb // gb, nk),
        in_specs=[pl.BlockSpec((ga, t, m), lambda p, q, k: (p, k, 0)),
                  pl.BlockSpec((t, gb * n), lambda p, q, k: (k, q)) if by_columns
                  else pl.BlockSpec((gb, t, n), lambda p, q, k: (q, k, 0))],
        out_specs=pl.BlockSpec((ga, gb, m, n), lambda p, q, k: (p, q, 0, 0)),
        out_shape=jax.ShapeDtypeStruct((ja, jb, m, n), BF16),
        operands=(a, b), scratch_shapes=[pltpu.VMEM((ga, gb, m, n), F32)],
        semantics=("parallel", "parallel", "arbitrary"), exchange=exchange)


def mix_in_proj(x, norm_g, shift, scale, w_mix, name, exchange=None):
    s = x.shape[0]
    t = min(ROW_TILE, s)

    def body(x_ref, g_ref, sh_ref, sc_ref, w_ref, u_ref, o_ref):
        uv = _norm_modulate(x_ref, g_ref, sh_ref, sc_ref)
        u_ref[...] = uv
        for j in range(N_DEV):
            o_ref[:, j * MIX_BLK:(j + 1) * MIX_BLK] = _dot(uv, w_ref[j]).astype(BF16)

    return _call(
        body, name=name, grid=(s // t,),
        in_specs=[_row_spec(t), _vec_spec(), _vec_spec(), _vec_spec(), _resident(w_mix.shape)],
        out_specs=[_row_spec(t), _row_spec(t, MIX_W)],
        out_shape=[jax.ShapeDtypeStruct((s, D_MODEL), BF16), jax.ShapeDtypeStruct((s, MIX_W), BF16)],
        operands=(x, norm_g, shift, scale, w_mix), semantics=("parallel",), exchange=exchange)


def _conv_taps(cc_ref, cx_ref, s):
    v = cc_ref[...].astype(F32) * cx_ref[...].astype(F32)
    tok = lax.broadcasted_iota(jnp.int32, v.shape, 0)
    v1 = jnp.where(tok >= 1, pltpu.roll(v, 1, 0), 0.0)
    v2 = jnp.where(tok >= 2, pltpu.roll(v, 2, 0), 0.0)
    return v, v1, v2, tok


def _proj_cols(s, first):
    return pl.BlockSpec((s, 128), lambda j: (0, first + j))


def short_conv(proj, conv_w, name):
    s = proj.shape[0]

    def body(cb_ref, cc_ref, cx_ref, w_ref, o_ref):
        v, v1, v2, _ = _conv_taps(cc_ref, cx_ref, s)
        y = w_ref[0:1, :] * v2 + w_ref[1:2, :] * v1 + w_ref[2:3, :] * v
        o_ref[...] = (cb_ref[...].astype(F32) * y).astype(BF16)

    return pl.pallas_call(
        body, name=name, grid=(CONV_W // 128,),
        in_specs=[_proj_cols(s, 0), _proj_cols(s, 4), _proj_cols(s, 8),
                  pl.BlockSpec((3, 128), lambda j: (0, j))],
        out_specs=pl.BlockSpec((s, 128), lambda j: (0, j)),
        out_shape=jax.ShapeDtypeStruct((s, CONV_W), BF16),
        compiler_params=_params("parallel"),
    )(proj, proj, proj, conv_w)


def short_conv_bwd(dsa, proj, conv_w, name):
    s = proj.shape[0]

    def body(dsa_ref, cb_ref, cc_ref, cx_ref, w_ref, dcb_ref, dcc_ref, dcx_ref, dw_ref):
        v, v1, v2, tok = _conv_taps(cc_ref, cx_ref, s)
        w0, w1, w2 = w_ref[0:1, :], w_ref[1:2, :], w_ref[2:3, :]
        y = w0 * v2 + w1 * v1 + w2 * v
        dsa_v = dsa_ref[...].astype(F32)
        dcb_ref[...] = (dsa_v * y).astype(BF16)
        dy = dsa_v * cb_ref[...].astype(F32)
        dw_ref[0:1, :] = jnp.sum(dy * v2, axis=0, keepdims=True)
        dw_ref[1:2, :] = jnp.sum(dy * v1, axis=0, keepdims=True)
        dw_ref[2:3, :] = jnp.sum(dy * v, axis=0, keepdims=True)
        dy1 = jnp.where(tok < s - 1, pltpu.roll(dy, s - 1, 0), 0.0)
        dy2 = jnp.where(tok < s - 2, pltpu.roll(dy, s - 2, 0), 0.0)
        dv = w2 * dy + w1 * dy1 + w0 * dy2
        dcc_ref[...] = (dv * cx_ref[...].astype(F32)).astype(BF16)
        dcx_ref[...] = (dv * cc_ref[...].astype(F32)).astype(BF16)

    col = pl.BlockSpec((s, 128), lambda j: (0, j))
    act = jax.ShapeDtypeStruct((s, CONV_W), BF16)
    return pl.pallas_call(
        body, name=name, grid=(CONV_W // 128,),
        in_specs=[col, _proj_cols(s, 0), _proj_cols(s, 4), _proj_cols(s, 8),
                  pl.BlockSpec((3, 128), lambda j: (0, j))],
        out_specs=[col, col, col, pl.BlockSpec((3, 128), lambda j: (0, j))],
        out_shape=[act, act, act, jax.ShapeDtypeStruct((3, CONV_W), F32)],
        compiler_params=_params("parallel"),
    )(dsa, proj, proj, proj, conv_w)


def _gate_specs(t):
    return [pl.BlockSpec((t, D_MODEL), lambda i: (i, 3)), pl.BlockSpec((t, D_MODEL), lambda i: (i, 4))]


def merge_forward(sa, o, proj, w_co, w_ao, b_merge, name, exchange=None):
    s = sa.shape[0]
    t = min(ROW_TILE, s)

    def body(sa_ref, o_ref, ga_ref, gb_ref, wco_ref, wao_ref, bm_ref, mg_ref, ya_ref, yb_ref):
        ya = _dot(sa_ref[...], wco_ref[...])
        yb = _dot(o_ref[...], wao_ref[...])
        sga = _sigmoid(ga_ref[...].astype(F32) + bm_ref[0:1, :])
        sgb = _sigmoid(gb_ref[...].astype(F32) + bm_ref[1:2, :])
        mg_ref[...] = (sga * ya + sgb * yb).astype(BF16)
        ya_ref[...] = ya.astype(BF16)
        yb_ref[...] = yb.astype(BF16)

    act = jax.ShapeDtypeStruct((s, D_MODEL), BF16)
    return _call(
        body, name=name, grid=(s // t,),
        in_specs=[_row_spec(t, CONV_W), _row_spec(t, ATTN_W)] + _gate_specs(t)
        + [_vec_spec(CONV_W), _vec_spec(ATTN_W), _vec_spec(2)],
        out_specs=[_row_spec(t)] * 3, out_shape=[act, act, act],
        operands=(sa, o, proj, proj, w_co, w_ao, b_merge), semantics=("parallel",), exchange=exchange)


def merge_backward(dy, w_out, proj, ya, yb, b_merge, name, exchange=None):
    s = dy.shape[0]
    t = min(ROW_TILE, s)

    def body(dy_ref, w_ref, ga_ref, gb_ref, ya_ref, yb_ref, bm_ref,
             dya_ref, dyb_ref, dga_ref, dgb_ref, dbm_ref):
        @pl.when(pl.program_id(0) == 0)
        def _():
            dbm_ref[...] = jnp.zeros_like(dbm_ref)

        dmg = _dot(dy_ref[...], w_ref[...], NT)
        sga = _sigmoid(ga_ref[...].astype(F32) + bm_ref[0:1, :])
        sgb = _sigmoid(gb_ref[...].astype(F32) + bm_ref[1:2, :])
        dya_ref[...] = (dmg * sga).astype(BF16)
        dyb_ref[...] = (dmg * sgb).astype(BF16)
        dga = dmg * ya_ref[...].astype(F32) * sga * (1.0 - sga)
        dgb = dmg * yb_ref[...].astype(F32) * sgb * (1.0 - sgb)
        dga_ref[...] = dga.astype(BF16)
        dgb_ref[...] = dgb.astype(BF16)
        dbm_ref[0:1, :] += jnp.sum(dga, axis=0, keepdims=True)
        dbm_ref[1:2, :] += jnp.sum(dgb, axis=0, keepdims=True)

    act = jax.ShapeDtypeStruct((s, D_MODEL), BF16)
    return _call(
        body, name=name, grid=(s // t,),
        in_specs=[_row_spec(t), _vec_spec(D_MODEL)] + _gate_specs(t)
        + [_row_spec(t), _row_spec(t), _vec_spec(2)],
        out_specs=[_row_spec(t)] * 4 + [_vec_spec(2)],
        out_shape=[act] * 4 + [jax.ShapeDtypeStruct((2, D_MODEL), F32)],
        operands=(dy, w_out, proj, proj, ya, yb, b_merge), semantics=("arbitrary",), exchange=exchange)


def out_proj_bwd(dya, dyb, w_co, w_ao, name):
    s = dya.shape[0]
    t = min(ROW_TILE, s)

    def body(dya_ref, dyb_ref, wco_ref, wao_ref, dsa_ref, do_ref):
        dsa_ref[...] = _dot(dya_ref[...], wco_ref[...], NT).astype(BF16)
        do_ref[...] = _dot(dyb_ref[...], wao_ref[...], NT).astype(BF16)

    return pl.pallas_call(
        body, name=name, grid=(s // t,),
        in_specs=[_row_spec(t), _row_spec(t), _vec_spec(CONV_W), _vec_spec(ATTN_W)],
        out_specs=[_row_spec(t, CONV_W), _row_spec(t, ATTN_W)],
        out_shape=[jax.ShapeDtypeStruct((s, CONV_W), BF16), jax.ShapeDtypeStruct((s, ATTN_W), BF16)],
        compiler_params=_params("parallel"),
    )(dya, dyb, w_co, w_ao)


ATT_HEADS = 4
ATT_LANES = ATT_HEADS * HEAD_DIM
ATT_UNDERFLOW = 110.0


def _softplus(z):
    return jnp.maximum(z, 0.0) + jnp.log(1.0 + jnp.exp(-jnp.abs(z)))


def _head_masks(rows):
    lane = lax.broadcasted_iota(jnp.int32, (rows, ATT_LANES), 1)
    return [(lane >= h * HEAD_DIM) & (lane < (h + 1) * HEAD_DIM) for h in range(ATT_HEADS)]


def _per_head(x, masks):
    return [jnp.where(m, x, jnp.zeros_like(x)) for m in masks]


def _att_specs(s, blk):
    first = {"q": 3 * CONV_W // ATT_LANES, "k": (3 * CONV_W + ATTN_W) // ATT_LANES,
             "v": (3 * CONV_W + 2 * ATTN_W) // ATT_LANES}
    return [pl.BlockSpec((blk, ATT_LANES), lambda h, i: (i, first["q"] + h)),
            pl.BlockSpec((s, ATT_LANES), lambda h, i: (0, first["k"] + h)),
            pl.BlockSpec((s, ATT_LANES), lambda h, i: (0, first["v"] + h))]


def _head_norms(x, masks):
    sq = jnp.square(x.astype(F32))
    return [jnp.sum(jnp.where(m, sq, 0.0), axis=1, keepdims=True) for m in masks]


def stick_breaking_fwd(proj, name, exchange=None):
    s = proj.shape[0]
    blk = ATT_BLK
    nq = s // blk

    def body(q_ref, k_ref, v_ref, o_ref, tot_ref, first_ref, kmax_ref):
        i = pl.program_id(1)
        row = lax.broadcasted_iota(jnp.int32, (blk, blk), 0)
        col = lax.broadcasted_iota(jnp.int32, (blk, blk), 1)
        tri = (row >= col).astype(BF16)
        causal = col < row
        masks = _head_masks(blk)
        q_all = q_ref[...] * ATTN_SCALE
        qs = _per_head(q_all, masks)

        @pl.when(i == 0)
        def _():
            def longest(n, best):
                norms = _head_norms(k_ref[pl.ds(pl.multiple_of(n * blk, blk), blk), :], masks)
                return tuple(jnp.maximum(b, v) for b, v in zip(best, norms))

            best = lax.fori_loop(0, nq, longest, tuple(jnp.zeros((blk, 1), F32) for _ in range(ATT_HEADS)))
            for h in range(ATT_HEADS):
                kmax_ref[h] = jnp.sqrt(jnp.max(best[h], axis=0, keepdims=True))

        needed = [jnp.sqrt(n) * kmax_ref[h] + ATT_UNDERFLOW for h, n in enumerate(_head_norms(q_all, masks))]

        def finished(laters):
            slack = laters[0] - needed[0]
            for h in range(1, ATT_HEADS):
                slack = jnp.minimum(slack, laters[h] - needed[h])
            return (jnp.min(slack) >= 0.0).astype(jnp.int32)

        def step(j, carry, diagonal):
            laters, acc = carry
            rows = pl.ds(pl.multiple_of(j * blk, blk), blk)
            kb = k_ref[rows, :]
            probs, new_laters = [], []
            for h in range(ATT_HEADS):
                z = _dot(qs[h], kb, NT)
                sp = _softplus(z)
                if diagonal:
                    sp = jnp.where(causal, sp, 0.0)
                a = jnp.exp(z - (_dot(sp.astype(BF16), tri) + laters[h]))
                if diagonal:
                    a = jnp.where(causal, a, 0.0)
                probs.append(a.astype(BF16))
                new_laters.append(laters[h] + jnp.sum(sp, axis=1, keepdims=True))
            v_heads = jnp.concatenate(_per_head(v_ref[rows, :], masks), axis=0)
            acc = acc + _dot(jnp.concatenate(probs, axis=1), v_heads)
            return tuple(new_laters), acc

        carry = (tuple(jnp.zeros((blk, 1), F32) for _ in range(ATT_HEADS)), jnp.zeros((blk, ATT_LANES), F32))
        laters, acc = step(i, carry, True)

        def further(state):
            n, _, laters, acc = state
            laters, acc = step(i - 1 - n, (laters, acc), False)
            return n + 1, finished(laters), laters, acc

        walked, _, laters, acc = lax.while_loop(
            lambda state: jnp.logical_and(state[0] < i, state[1] == 0), further,
            (jnp.int32(0), finished(laters), laters, acc))
        o_ref[...] = acc.astype(BF16)
        tot = jnp.zeros((blk, ATT_LANES), F32)
        for h in range(ATT_HEADS):
            tot = jnp.where(masks[h], laters[h], tot)
        tot_ref[...] = tot
        first_ref[...] = jnp.full(first_ref.shape, i - walked, jnp.int32).astype(F32)

    out_spec = pl.BlockSpec((blk, ATT_LANES), lambda h, i: (i, h))
    groups = N_HEADS // ATT_HEADS
    return _call(
        body, name=name, grid=(groups, nq),
        in_specs=_att_specs(s, blk),
        out_specs=[out_spec, out_spec, pl.BlockSpec((None, None, 8, 128), lambda h, i: (h, i, 0, 0))],
        out_shape=[jax.ShapeDtypeStruct((s, ATTN_W), BF16), jax.ShapeDtypeStruct((s, ATTN_W), F32),
                   jax.ShapeDtypeStruct((groups, nq, 8, 128), F32)],
        operands=(proj, proj, proj), scratch_shapes=[pltpu.VMEM((ATT_HEADS, 1, 1), F32)],
        semantics=("parallel", "arbitrary"), exchange=exchange)


def stick_breaking_bwd(proj, do, tot, first, name, exchange=None):
    s = proj.shape[0]
    blk = ATT_BLK
    nq = s // blk

    def body(q_ref, k_ref, v_ref, do_ref, tot_ref, first_ref, dq_ref, dk_ref, dv_ref):
        i = pl.program_id(1)
        start = jnp.clip(jnp.max(first_ref[...]).astype(jnp.int32), 0, i)

        @pl.when(i == 0)
        def _():
            dk_ref[...] = jnp.zeros_like(dk_ref)
            dv_ref[...] = jnp.zeros_like(dv_ref)

        row = lax.broadcasted_iota(jnp.int32, (blk, blk), 0)
        col = lax.broadcasted_iota(jnp.int32, (blk, blk), 1)
        before = (row < col).astype(BF16)
        upto = (row <= col).astype(BF16)
        causal = col < row
        masks = _head_masks(blk)
        qs = _per_head(q_ref[...] * ATTN_SCALE, masks)
        dos = _per_head(do_ref[...], masks)
        q_heads = jnp.concatenate(qs, axis=0)
        do_heads = jnp.concatenate(dos, axis=0)
        tot_all = tot_ref[...]
        totals = [jnp.max(jnp.where(m, tot_all, 0.0), axis=1, keepdims=True) for m in masks]

        def step(j, carry, diagonal):
            earliers, g_sums, dq = carry
            rows = pl.ds(pl.multiple_of(j * blk, blk), blk)
            kb = k_ref[rows, :]
            vb = v_ref[rows, :]
            probs, dzs, new_earliers, new_g_sums = [], [], [], []
            for h in range(ATT_HEADS):
                z = _dot(qs[h], kb, NT)
                sp = _softplus(z)
                if diagonal:
                    sp = jnp.where(causal, sp, 0.0)
                c = (totals[h] - earliers[h]) - _dot(sp.astype(BF16), before)
                a = jnp.exp(z - c)
                if diagonal:
                    a = jnp.where(causal, a, 0.0)
                g = a * _dot(dos[h], vb, NT)
                f = g_sums[h] + _dot(g.astype(BF16), upto)
                dz = g - jnp.exp(z - sp) * f
                if diagonal:
                    dz = jnp.where(causal, dz, 0.0)
                probs.append(a.astype(BF16))
                dzs.append(dz.astype(BF16))
                new_earliers.append(earliers[h] + jnp.sum(sp, axis=1, keepdims=True))
                new_g_sums.append(g_sums[h] + jnp.sum(g, axis=1, keepdims=True))
            k_heads = jnp.concatenate(_per_head(kb, masks), axis=0)
            dq = dq + _dot(jnp.concatenate(dzs, axis=1), k_heads)
            dk_ref[rows, :] += _dot(jnp.concatenate(dzs, axis=0), q_heads, TN)
            dv_ref[rows, :] += _dot(jnp.concatenate(probs, axis=0), do_heads, TN)
            return tuple(new_earliers), tuple(new_g_sums), dq

        zeros = tuple(jnp.zeros((blk, 1), F32) for _ in range(ATT_HEADS))
        carry = (zeros, zeros, jnp.zeros((blk, ATT_LANES), F32))
        carry = lax.fori_loop(start, i, lambda j, c: step(j, c, False), carry)
        dq = step(i, carry, True)[2]
        dq_ref[...] = (dq * ATTN_SCALE).astype(BF16)

    blk_spec = pl.BlockSpec((blk, ATT_LANES), lambda h, i: (i, h))
    full_spec = pl.BlockSpec((s, ATT_LANES), lambda h, i: (0, h))
    return _call(
        body, name=name, grid=(N_HEADS // ATT_HEADS, nq),
        in_specs=_att_specs(s, blk) + [blk_spec, blk_spec,
                                       pl.BlockSpec((None, None, 8, 128), lambda h, i: (h, i, 0, 0))],
        out_specs=[blk_spec, full_spec, full_spec],
        out_shape=[jax.ShapeDtypeStruct((s, ATTN_W), BF16), jax.ShapeDtypeStruct((s, ATTN_W), F32),
                   jax.ShapeDtypeStruct((s, ATTN_W), F32)],
        operands=(proj, proj, proj, do, tot, first), semantics=("parallel", "arbitrary"), exchange=exchange)


def adamw(w, m, v, parts, name):
    r, c = w.shape
    p = parts.shape[0]
    t = r
    for cand in (256, 176):
        if r % cand == 0 and r > cand:
            t = cand
            break

    def body(w_ref, m_ref, v_ref, p_ref, g_ref, d_ref, mo_ref, vo_ref):
        g = p_ref[0].astype(F32)
        for n in range(1, p):
            g = g + p_ref[n].astype(F32)
        m_new = ADAM_B1 * m_ref[...] + (1.0 - ADAM_B1) * g
        v_new = ADAM_B2 * v_ref[...] + (1.0 - ADAM_B2) * (g * g)
        m_hat = m_new / ADAM_BC1
        v_hat = v_new / ADAM_BC2
        g_ref[...] = g
        d_ref[...] = -ADAM_LR * (m_hat / (jnp.sqrt(v_hat) + ADAM_EPS) + ADAM_WD * w_ref[...])
        mo_ref[...] = m_new
        vo_ref[...] = v_new

    spec = pl.BlockSpec((t, c), lambda i: (i, 0))
    out = jax.ShapeDtypeStruct((r, c), F32)
    return pl.pallas_call(
        body, name=name, grid=(r // t,),
        in_specs=[spec, spec, spec, pl.BlockSpec((p, t, c), lambda i: (0, i, 0))],
        out_specs=[spec] * 4, out_shape=[out] * 4,
        compiler_params=_params("parallel"),
    )(w, m, v, parts)


def kernel(x, c, w_ada, b_ada, norm1_g, ffn1_w_gu, ffn1_w_down, norm2_g, w_mix_in, b_merge, conv_w, w_conv_out, w_attn_out, w_out, norm3_g, ffn2_w_gu, ffn2_w_down, final_g, loss_target, m_w_ada, m_b_ada, m_norm1_g, m_ffn1_w_gu, m_ffn1_w_down, m_norm2_g, m_w_mix_in, m_b_merge, m_conv_w, m_w_conv_out, m_w_attn_out, m_w_out, m_norm3_g, m_ffn2_w_gu, m_ffn2_w_down, m_final_g, v_w_ada, v_b_ada, v_norm1_g, v_ffn1_w_gu, v_ffn1_w_down, v_norm2_g, v_w_mix_in, v_b_merge, v_conv_w, v_w_conv_out, v_w_attn_out, v_w_out, v_norm3_g, v_ffn2_w_gu, v_ffn2_w_down, v_final_g):
    s = x.shape[1]
    me = 4 * lax.axis_index("x") + 2 * lax.axis_index("y") + lax.axis_index("c")
    x0 = x[0]
    target = loss_target[0]
    final_g2 = final_g.reshape(1, D_MODEL)

    def shard(w):
        return w[0].astype(BF16)

    def flipped(w):
        return jnp.swapaxes(w, 1, 2)

    def rows8(g):
        return g.reshape(N_DEV, -1, D_MODEL)

    small_in = jnp.concatenate([c.reshape(-1), b_merge.reshape(-1), conv_w.reshape(-1),
                                jnp.zeros((64,), F32)]).reshape(1, -1)
    n_ada = w_ada.shape[2]
    b_cols = lax.dynamic_slice(b_ada, (0, me * n_ada), (1, n_ada))
    small_all, mod_all, got = prologue(small_in, w_ada[0], b_cols,
                                       [shard(flipped(ffn1_w_gu)), shard(ffn1_w_down)], "prologue")
    wgu1, wd1 = run_exchange(gather_stage2(got), "gather_ffn1_cores")
    wd1 = wd1.reshape(4, FF_BLK, D_MODEL)
    small_all = small_all[:, 0, :]
    c_all = small_all[:, :D_MODEL]
    bm_full = small_all[:, 1024:1280].reshape(8, 2, 128).transpose(1, 0, 2).reshape(2, D_MODEL)
    cw_full = small_all[:, 1280:1472].reshape(8, 3, 64).transpose(1, 0, 2).reshape(3, CONV_W)
    mod = lax.dynamic_index_in_dim(mod_all, me, axis=1, keepdims=False).reshape(9, 1, D_MODEL)
    sh1, sc1, gt1, sh2, sc2, gt2, sh3, sc3, gt3 = [mod[n] for n in range(9)]

    (u1, gu1, act1), got = ffn_up(x0, norm1_g, sh1, sc1, wgu1, "ffn_up_1",
                                  exchange=gather_stage1([shard(w_mix_in)]))
    (x1, y1), (wmix, *got) = residual_matmul(
        act1, wd1, x0, gt1, 0.5, "ffn_down_1", exchange=merge_exchanges(
            gather_stage2(got), gather_stage1([shard(w_conv_out), shard(w_attn_out), shard(w_out)])))

    (u2, proj), (wco, wao, wout) = mix_in_proj(x1, norm2_g, sh2, sc2, wmix, "mix_in",
                                               exchange=gather_stage2(got))
    wco = wco.transpose(1, 0, 2).reshape(CONV_W, D_MODEL)
    wao = wao.transpose(1, 0, 2).reshape(ATTN_W, D_MODEL)
    wout = wout.reshape(D_MODEL, D_MODEL)
    sa = short_conv(proj, cw_full, "short_conv")
    (o, tot, first), got = stick_breaking_fwd(proj, "attn_fwd",
                                       exchange=gather_stage1([shard(flipped(ffn2_w_gu)), shard(ffn2_w_down)]))
    (merged, ya, yb), (wgu3, wd3) = merge_forward(sa, o, proj, wco, wao, bm_full, "merge",
                                                  exchange=gather_stage2(got))
    wd3 = wd3.reshape(4, FF_BLK, D_MODEL)
    x2, y2 = residual_matmul(merged[None], wout[None], x1, gt2, 1.0, "out_proj")

    u3, gu3, act3 = ffn_up(x2, norm3_g, sh3, sc3, wgu3, "ffn_up_3")
    x3, y3 = residual_matmul(act3, wd3, x2, gt3, 0.5, "ffn_down_3")

    dx3, dy3, dgt3, dfinal, sq = loss_head(x3, target, final_g2, gt3, y3, 0.5, "loss_head")
    dgu3, du3 = ffn_tokens_bwd(dy3, wd3, gu3, wgu3, "ffn_bwd_3")
    dgu3 = dgu3.reshape(8, s, FF_BLK)
    g_wd3 = rows8(matmul_tn(act3, dy3[None], "grad_w_down_3", group=(4, 1)))
    g_wgu3 = matmul_tn(dgu3, u3[None], "grad_w_gu_3", group=(4, 1)).reshape(8, FF_BLK, D_MODEL)
    dx2, dsh3, dsc3, dn3, dy2, dgt2 = norm_modulate_bwd(du3, x2, dx3, norm3_g, sc3, "norm_bwd_3",
                                                        prev=(gt2, y2, 1.0))

    (dya, dyb, dga, dgb, dbm), pairs = merge_backward(dy2, wout, proj, ya, yb, bm_full, "merge_bwd",
                                                      exchange=scatter_stage1([g_wgu3, g_wd3]))
    sums3 = [pair_sum(g_wgu3, pairs[0], "pair_sum_w_gu_3"), pair_sum(g_wd3, pairs[1], "pair_sum_w_down_3")]
    g_wout = rows8(matmul_tn(merged[None], dy2[None], "grad_w_out"))
    dsa, do = out_proj_bwd(dya, dyb, wco, wao, "out_proj_bwd")
    g_wco = matmul_tn(sa[None], dya[None], "grad_w_conv_out").reshape(CONV_W, N_DEV, 128).transpose(1, 0, 2)
    g_wao = matmul_tn(o[None], dyb[None], "grad_w_attn_out").reshape(ATTN_W, N_DEV, 128).transpose(1, 0, 2)
    dcb, dcc, dcx, dconv = short_conv_bwd(dsa, proj, cw_full, "short_conv_bwd")
    (dq, dk, dv), landed3 = stick_breaking_bwd(proj, do, tot, first, "attn_bwd", exchange=scatter_stage2(sums3))
    dproj = jnp.concatenate([dcb, dcc, dcx, dq, dk.astype(BF16), dv.astype(BF16), dga, dgb], axis=1)
    du2 = matmul_nt_acc(dproj, wmix, "mix_in_du")
    g_wmix = matmul_tn(u2[None], dproj, "grad_w_mix_in", group=(1, 4), b_cols=MIX_BLK).reshape(
        N_DEV, D_MODEL, MIX_BLK)
    mixer_grads = [g_wmix, g_wco, g_wao, g_wout]
    (dx1, dsh2, dsc2, dn2, dy1, dgt1), pairs = norm_modulate_bwd(
        du2, x1, dx2, norm2_g, sc2, "norm_bwd_2", prev=(gt1, y1, 0.5), exchange=scatter_stage1(mixer_grads))
    sums_mix = [pair_sum(g, p, f"pair_sum_mixer_{n}") for n, (g, p) in enumerate(zip(mixer_grads, pairs))]

    dgu1, landed_mix = ffn_tokens_bwd(dy1, wd1, gu1, None, "ffn_dact_1", exchange=scatter_stage2(sums_mix[:1]))
    dgu1 = dgu1.reshape(8, s, FF_BLK)
    g_wgu1, landed_small = matmul_tn(dgu1, u1[None], "grad_w_gu_1", group=(4, 1),
                                     exchange=scatter_stage2(sums_mix[1:]))
    g_wgu1 = g_wgu1.reshape(8, FF_BLK, D_MODEL)
    g_wd1, pairs = matmul_tn(act1, dy1[None], "grad_w_down_1", group=(4, 1), exchange=scatter_stage1([g_wgu1]))
    g_wd1 = rows8(g_wd1)
    sum_gu1 = pair_sum(g_wgu1, pairs[0], "pair_sum_w_gu_1")
    du1, (landed_gu1, pair_d1) = matmul_nt_acc(
        dgu1, wgu1, "ffn_du_1", b_dims=NN, exchange=merge_exchanges(scatter_stage2([sum_gu1]), scatter_stage1([g_wd1])))
    sum_d1 = pair_sum(g_wd1, pair_d1, "pair_sum_w_down_1")
    (grad_x, dsh1, dsc1, dn1), landed_d1 = norm_modulate_bwd(du1, x0, dx1, norm1_g, sc1, "norm_bwd_1",
                                                            exchange=scatter_stage2([sum_d1]))

    loss_local = (0.5 / D_MODEL) * jnp.sum(sq)
    stats = jnp.concatenate(
        [v.reshape(-1) for v in (dsh1, dsc1, dgt1, dsh2, dsc2, dgt2, dsh3, dsc3, dgt3,
                                 dn1, dn2, dn3, dfinal, dbm, dconv)]
        + [jnp.broadcast_to(loss_local, (128,))]).reshape(1, -1)
    stats_all = all_gather_rows(stats, "gather_stats")
    n_mod = 9 * D_MODEL
    loss = jnp.sum(stats_all[:, 0, -1])
    dmod_all = stats_all[:, :, :n_mod]
    off = n_mod
    parts = {}
    for key in ("norm1_g", "norm2_g", "norm3_g", "final_g"):
        parts[key] = stats_all[:, :, off:off + D_MODEL]
        off += D_MODEL
    dbm_all = stats_all[:, 0, off:off + 2 * D_MODEL].reshape(N_DEV, 2, D_MODEL)
    off += 2 * D_MODEL
    dcw_all = stats_all[:, 0, off:off + 3 * CONV_W].reshape(N_DEV, 3, CONV_W)
    parts["b_merge"] = lax.dynamic_slice(dbm_all, (0, 0, me * 128), (N_DEV, 2, 128))
    parts["conv_w"] = lax.dynamic_slice(dcw_all, (0, 0, me * 64), (N_DEV, 3, 64))
    dmod_cols = lax.dynamic_slice(dmod_all[:, 0, :], (0, me * n_ada), (N_DEV, n_ada))
    parts["w_ada"] = ada_backward(c_all, dmod_cols, "ada_backward")[None]
    parts["b_ada"] = dmod_all
    parts["ffn2_w_gu"], parts["ffn2_w_down"] = landed3
    parts["w_mix_in"] = landed_mix[0]
    parts["w_conv_out"], parts["w_attn_out"], parts["w_out"] = landed_small
    parts["ffn1_w_gu"] = landed_gu1
    parts["ffn1_w_down"] = landed_d1[0]

    given = dict(w_ada=w_ada, b_ada=b_ada, norm1_g=norm1_g, ffn1_w_gu=ffn1_w_gu, ffn1_w_down=ffn1_w_down,
                 norm2_g=norm2_g, w_mix_in=w_mix_in, b_merge=b_merge, conv_w=conv_w, w_conv_out=w_conv_out,
                 w_attn_out=w_attn_out, w_out=w_out, norm3_g=norm3_g, ffn2_w_gu=ffn2_w_gu,
                 ffn2_w_down=ffn2_w_down, final_g=final_g)
    moments_m = dict(w_ada=m_w_ada, b_ada=m_b_ada, norm1_g=m_norm1_g, ffn1_w_gu=m_ffn1_w_gu,
                     ffn1_w_down=m_ffn1_w_down, norm2_g=m_norm2_g, w_mix_in=m_w_mix_in, b_merge=m_b_merge,
                     conv_w=m_conv_w, w_conv_out=m_w_conv_out, w_attn_out=m_w_attn_out, w_out=m_w_out,
                     norm3_g=m_norm3_g, ffn2_w_gu=m_ffn2_w_gu, ffn2_w_down=m_ffn2_w_down, final_g=m_final_g)
    moments_v = dict(w_ada=v_w_ada, b_ada=v_b_ada, norm1_g=v_norm1_g, ffn1_w_gu=v_ffn1_w_gu,
                     ffn1_w_down=v_ffn1_w_down, norm2_g=v_norm2_g, w_mix_in=v_w_mix_in, b_merge=v_b_merge,
                     conv_w=v_conv_w, w_conv_out=v_w_conv_out, w_attn_out=v_w_attn_out, w_out=v_w_out,
                     norm3_g=v_norm3_g, ffn2_w_gu=v_ffn2_w_gu, ffn2_w_down=v_ffn2_w_down, final_g=v_final_g)
    order = ["w_ada", "b_ada", "norm1_g", "ffn1_w_gu", "ffn1_w_down", "norm2_g", "w_mix_in", "b_merge",
             "conv_w", "w_conv_out", "w_attn_out", "w_out", "norm3_g", "ffn2_w_gu", "ffn2_w_down", "final_g"]
    grads, deltas, new_m, new_v = [], [], [], []
    for key in order:
        turn = flipped if key in ("ffn1_w_gu", "ffn2_w_gu") else (lambda a: a)
        shape = turn(given[key]).shape
        shape2 = (1, shape[0]) if len(shape) == 1 else shape[-2:]
        outs = adamw(turn(given[key]).reshape(shape2), turn(moments_m[key]).reshape(shape2),
                     turn(moments_v[key]).reshape(shape2), parts[key], f"adamw_{key}")
        for dst, val in zip((grads, deltas, new_m, new_v), outs):
            dst.append(turn(val.reshape(shape)))

    return (loss, grad_x[None], *grads, *deltas, *new_m, *new_v)
```

```python
import functools
from typing import Callable, NamedTuple

import jax
import jax.numpy as jnp
from jax import lax
from jax.experimental import pallas as pl
from jax.experimental.pallas import tpu as pltpu

F32 = jnp.float32
BF16 = jnp.bfloat16
MESH = pl.DeviceIdType.MESH
ANY = pl.BlockSpec(memory_space=pl.ANY)

N_DEV = 8
D_MODEL = 1024
D_FF = 2816
FF_BLK = D_FF // 4
N_HEADS = 8
HEAD_DIM = 64
CONV_W = 512
ATTN_W = 512
MIX_W = 3 * CONV_W + 3 * ATTN_W + 2 * D_MODEL
MIX_BLK = MIX_W // N_DEV
EPS = 1e-6
ATTN_SCALE = HEAD_DIM ** -0.5

ADAM_LR = 0.001
ADAM_B1 = 0.9
ADAM_B2 = 0.999
ADAM_EPS = 1e-08
ADAM_WD = 0.01
ADAM_STEP = 10
ADAM_BC1 = 1.0 - ADAM_B1 ** ADAM_STEP
ADAM_BC2 = 1.0 - ADAM_B2 ** ADAM_STEP

VMEM_LIMIT = 56 * 1024 * 1024
ROW_TILE = 512
ACC_TILE = 1024
ELT_TILE = 256
ATT_BLK = 256

NN = (((1,), (0,)), ((), ()))
NT = (((1,), (1,)), ((), ()))
TN = (((0,), (0,)), ((), ()))


def _dot(a, b, dims=NN):
    return lax.dot_general(a, b, dims, preferred_element_type=F32)


def _params(*sem):
    return pltpu.CompilerParams(dimension_semantics=sem, vmem_limit_bytes=VMEM_LIMIT)


def _sigmoid(x):
    return 1.0 / (1.0 + jnp.exp(-x))


def _me():
    x, y, c = lax.axis_index("x"), lax.axis_index("y"), lax.axis_index("c")
    return x, y, c, 4 * x + 2 * y + c


def _peer(k):
    x, y, c, _ = _me()
    px = 1 - x if (k >> 2) & 1 else x
    py = 1 - y if (k >> 1) & 1 else y
    pc = 1 - c if k & 1 else c
    return (px, py, pc), 4 * px + 2 * py + pc


class Exchange(NamedTuple):
    operands: tuple
    out_shapes: tuple
    aliases: dict
    n_remote: int
    n_local: int
    copies: Callable


CHIP_FLIPS = (2, 4, 6)
SIBLING = 1


def _remote(src, dst, send_sems, recv_sems, n, peer):
    return pltpu.make_async_remote_copy(src_ref=src, dst_ref=dst, send_sem=send_sems.at[n], recv_sem=recv_sems.at[n],
                                        device_id=peer, device_id_type=MESH)


def gather_stage1(shards):
    n = len(shards)
    rels = (SIBLING,) + CHIP_FLIPS

    def copies(ins, outs, send_sems, recv_sems, local_sems, rb, lb):
        _, _, _, me = _me()
        cps = []
        for w in range(n):
            cps.append(pltpu.make_async_copy(ins[w], outs[w].at[me], local_sems.at[lb + w]))
            for a, k in enumerate(rels):
                peer, _ = _peer(k)
                cps.append(_remote(ins[w], outs[w].at[me], send_sems, recv_sems, rb + len(rels) * w + a, peer))
        return cps

    shapes = tuple(jax.ShapeDtypeStruct((N_DEV,) + s.shape, s.dtype) for s in shards)
    return Exchange(tuple(shards), shapes, {}, len(rels) * n, n, copies)


def gather_stage2(fulls):
    n = len(fulls)

    def copies(ins, outs, send_sems, recv_sems, local_sems, rb, lb):
        sibling, _ = _peer(SIBLING)
        cps = []
        for w in range(n):
            for a, k in enumerate(CHIP_FLIPS):
                _, blk = _peer(k)
                cps.append(_remote(outs[w].at[blk], outs[w].at[blk], send_sems, recv_sems, rb + 3 * w + a, sibling))
        return cps

    shapes = tuple(jax.ShapeDtypeStruct(f.shape, f.dtype) for f in fulls)
    return Exchange(tuple(fulls), shapes, {w: w for w in range(n)}, 3 * n, 0, copies)


def scatter_stage1(fulls):
    n = len(fulls)

    def copies(ins, outs, send_sems, recv_sems, local_sems, rb, lb):
        _, _, c, _ = _me()
        sibling, _ = _peer(SIBLING)
        cps = []
        for w in range(n):
            for q in range(4):
                cps.append(_remote(ins[w].at[2 * q + (1 - c)], outs[w].at[q], send_sems, recv_sems, rb + 4 * w + q, sibling))
        return cps

    shapes = tuple(jax.ShapeDtypeStruct((4,) + f.shape[1:], f.dtype) for f in fulls)
    return Exchange(tuple(fulls), shapes, {}, 4 * n, 0, copies)


def scatter_stage2(sums):
    n = len(sums)

    def copies(ins, outs, send_sems, recv_sems, local_sems, rb, lb):
        x, y, _, _ = _me()
        mine = 2 * x + y
        cps = []
        for w in range(n):
            cps.append(pltpu.make_async_copy(ins[w].at[mine], outs[w].at[mine], local_sems.at[lb + w]))
            for a, k in enumerate(CHIP_FLIPS):
                peer, _ = _peer(k)
                cps.append(_remote(ins[w].at[2 * peer[0] + peer[1]], outs[w].at[mine], send_sems, recv_sems,
                                   rb + 3 * w + a, peer))
        return cps

    shapes = tuple(jax.ShapeDtypeStruct(s.shape, s.dtype) for s in sums)
    return Exchange(tuple(sums), shapes, {}, 3 * n, n, copies)


def merge_exchanges(a, b):
    na_in, na_out = len(a.operands), len(a.out_shapes)

    def copies(ins, outs, send_sems, recv_sems, local_sems, rb, lb):
        return (a.copies(ins[:na_in], outs[:na_out], send_sems, recv_sems, local_sems, rb, lb)
                + b.copies(ins[na_in:], outs[na_out:], send_sems, recv_sems, local_sems, rb + a.n_remote, lb + a.n_local))

    aliases = dict(a.aliases)
    aliases.update({na_in + i: na_out + o for i, o in b.aliases.items()})
    return Exchange(a.operands + b.operands, a.out_shapes + b.out_shapes, aliases,
                    a.n_remote + b.n_remote, a.n_local + b.n_local, copies)


def _exchange_scratch(ex):
    return [pltpu.SemaphoreType.DMA((ex.n_remote,)), pltpu.SemaphoreType.DMA((ex.n_remote,)),
            pltpu.SemaphoreType.DMA((max(ex.n_local, 1),))]


def run_exchange(ex, name):
    n_in, n_out = len(ex.operands), len(ex.out_shapes)

    def body(*refs):
        cps = ex.copies(refs[:n_in], refs[n_in:n_in + n_out], *refs[n_in + n_out:], 0, 0)
        for cp in cps:
            cp.start()
        for cp in cps:
            cp.wait()

    return pl.pallas_call(
        body, name=name, out_shape=list(ex.out_shapes), in_specs=[ANY] * n_in, out_specs=[ANY] * n_out,
        scratch_shapes=_exchange_scratch(ex), input_output_aliases=dict(ex.aliases),
    )(*ex.operands)


def _call(body, *, name, grid, in_specs, out_specs, out_shape, operands, scratch_shapes=(), semantics=(),
          exchange=None):
    if exchange is None:
        return pl.pallas_call(
            body, name=name, grid=grid, in_specs=in_specs, out_specs=out_specs, out_shape=out_shape,
            scratch_shapes=list(scratch_shapes), compiler_params=_params(*semantics))(*operands)
    single = not isinstance(out_shape, (list, tuple))
    out_shapes = [out_shape] if single else list(out_shape)
    out_specs_l = [out_specs] if single else list(out_specs)
    n_in, n_out, n_scr = len(operands), len(out_shapes), len(scratch_shapes)
    x_in, x_out = len(exchange.operands), len(exchange.out_shapes)

    def hosted(*refs):
        ins, refs = refs[:n_in], refs[n_in:]
        xin, refs = refs[:x_in], refs[x_in:]
        outs, refs = refs[:n_out], refs[n_out:]
        xout, refs = refs[:x_out], refs[x_out:]
        scr, sems = refs[:n_scr], refs[n_scr:]
        first = functools.reduce(jnp.logical_and, [pl.program_id(a) == 0 for a in range(len(grid))])
        last = functools.reduce(jnp.logical_and, [pl.program_id(a) == g - 1 for a, g in enumerate(grid)])

        @pl.when(first)
        def _():
            for cp in exchange.copies(xin, xout, *sems, 0, 0):
                cp.start()

        body(*ins, *outs, *scr)

        @pl.when(last)
        def _():
            for cp in exchange.copies(xin, xout, *sems, 0, 0):
                cp.wait()

    res = pl.pallas_call(
        hosted, name=name, grid=grid,
        in_specs=list(in_specs) + [ANY] * x_in, out_specs=out_specs_l + [ANY] * x_out,
        out_shape=out_shapes + list(exchange.out_shapes),
        scratch_shapes=list(scratch_shapes) + _exchange_scratch(exchange),
        input_output_aliases={n_in + i: n_out + o for i, o in exchange.aliases.items()},
        compiler_params=_params(*(["arbitrary"] * len(grid))),
    )(*operands, *exchange.operands)
    outs, xouts = res[:n_out], res[n_out:]
    return (outs[0] if single else outs), xouts


def all_gather_rows(v, name):
    r, n = v.shape

    def body(v_ref, out_ref, send_sems, recv_sems):
        _, _, _, me = _me()
        out_ref[me] = v_ref[...]
        copies = []
        for k in range(1, N_DEV):
            peer, _ = _peer(k)
            copies.append(_remote(v_ref, out_ref.at[me], send_sems, recv_sems, k - 1, peer))
        for cp in copies:
            cp.start()
        for cp in copies:
            cp.wait()

    return pl.pallas_call(
        body, name=name,
        out_shape=jax.ShapeDtypeStruct((N_DEV, r, n), v.dtype),
        in_specs=[pl.BlockSpec(memory_space=pltpu.VMEM)],
        out_specs=pl.BlockSpec(memory_space=pltpu.VMEM),
        scratch_shapes=[pltpu.SemaphoreType.DMA((N_DEV - 1,)), pltpu.SemaphoreType.DMA((N_DEV - 1,))],
    )(v)


def pair_sum(full, pair, name):
    _, r, c = full.shape
    t = r
    core = lax.axis_index("c").astype(jnp.int32).reshape(1)

    def body(core_ref, f_ref, p_ref, o_ref):
        o_ref[...] = (f_ref[...].astype(F32) + p_ref[...].astype(F32)).astype(BF16)

    return pl.pallas_call(
        body, name=name,
        grid_spec=pltpu.PrefetchScalarGridSpec(
            num_scalar_prefetch=1, grid=(4, r // t),
            in_specs=[pl.BlockSpec((None, None, t, c), lambda q, i, core_ref: (q, core_ref[0], i, 0)),
                      pl.BlockSpec((None, t, c), lambda q, i, core_ref: (q, i, 0))],
            out_specs=pl.BlockSpec((None, t, c), lambda q, i, core_ref: (q, i, 0))),
        out_shape=jax.ShapeDtypeStruct((4, r, c), BF16),
        compiler_params=_params("parallel", "parallel"),
    )(core, full.reshape(4, 2, r, c), pair)


def prologue(small_in, w_ada, b_cols, shards, name):
    ex = gather_stage1(shards)
    n_sh = len(shards)
    n_small = small_in.shape[1]
    cols = w_ada.shape[1]

    def body(*refs):
        small_ref, w_ref, b_ref = refs[:3]
        shard_refs = refs[3:3 + n_sh]
        small_out, mod_out = refs[3 + n_sh:5 + n_sh]
        fulls = refs[5 + n_sh:5 + 2 * n_sh]
        part_ref, send1, recv1, send2, recv2, wsend, wrecv, wlocal, fsend, frecv = refs[5 + 2 * n_sh:]
        _, _, _, me = _me()
        big = ex.copies(shard_refs, fulls, wsend, wrecv, wlocal, 0, 0)
        for cp in big:
            cp.start()

        def all_gather(src_ref, dst_ref, send_sems, recv_sems):
            cps = [_remote(src_ref, dst_ref.at[me], send_sems, recv_sems, k - 1, _peer(k)[0]) for k in range(1, N_DEV)]
            for cp in cps:
                cp.start()
            for cp in cps:
                cp.wait()

        small_out[me] = small_ref[...]
        all_gather(small_ref, small_out, send1, recv1)
        c_all = jnp.concatenate([small_out[d][:, :D_MODEL] for d in range(N_DEV)], axis=0)
        act = c_all * _sigmoid(c_all)
        part_ref[...] = jnp.dot(act, w_ref[...], precision=lax.Precision.HIGHEST,
                                preferred_element_type=F32) + b_ref[...]
        mod_out[me] = part_ref[...]
        all_gather(part_ref, mod_out, send2, recv2)
        per = 2 + len(CHIP_FLIPS)
        sibling, _ = _peer(SIBLING)
        onward = []
        for w in range(n_sh):
            for a, k in enumerate(CHIP_FLIPS):
                _, blk = _peer(k)
                big[per * w + 2 + a].wait_recv()
                cp = _remote(fulls[w].at[blk], fulls[w].at[blk], fsend, frecv, len(CHIP_FLIPS) * w + a, sibling)
                cp.start()
                onward.append(cp)
        for w in range(n_sh):
            big[per * w].wait()
            big[per * w + 1].wait()
            for a in range(len(CHIP_FLIPS)):
                big[per * w + 2 + a].wait_send()
        for cp in onward:
            cp.wait()

    vmem = pl.BlockSpec(memory_space=pltpu.VMEM)
    sems = pltpu.SemaphoreType.DMA((N_DEV - 1,))
    res = pl.pallas_call(
        body, name=name,
        out_shape=[jax.ShapeDtypeStruct((N_DEV, 1, n_small), F32), jax.ShapeDtypeStruct((N_DEV, N_DEV, cols), F32)]
        + list(ex.out_shapes),
        in_specs=[vmem, vmem, vmem] + [ANY] * n_sh, out_specs=[vmem, vmem] + [ANY] * n_sh,
        scratch_shapes=[pltpu.VMEM((N_DEV, cols), F32), sems, sems, sems, sems] + _exchange_scratch(ex)
        + [pltpu.SemaphoreType.DMA((len(CHIP_FLIPS) * n_sh,)), pltpu.SemaphoreType.DMA((len(CHIP_FLIPS) * n_sh,))],
        compiler_params=pltpu.CompilerParams(vmem_limit_bytes=VMEM_LIMIT),
    )(small_in, w_ada, b_cols, *shards)
    return res[0], res[1], res[2:]


def ada_backward(c_all, dmod_cols, name):
    n = dmod_cols.shape[1]

    def body(c_ref, d_ref, o_ref):
        c = c_ref[...]
        act = c * _sigmoid(c)
        o_ref[...] = lax.dot_general(act, d_ref[...], TN, precision=lax.Precision.HIGHEST,
                                     preferred_element_type=F32)

    return pl.pallas_call(
        body, name=name, out_shape=jax.ShapeDtypeStruct((D_MODEL, n), F32),
        compiler_params=pltpu.CompilerParams(vmem_limit_bytes=VMEM_LIMIT),
    )(c_all, dmod_cols)


def _row_spec(t, width=D_MODEL):
    return pl.BlockSpec((t, width), lambda i: (i, 0))


def _vec_spec(rows=1, width=D_MODEL):
    return pl.BlockSpec((rows, width), lambda i: (0, 0))


def _resident(shape):
    return pl.BlockSpec(shape, lambda i: (0,) * len(shape), pipeline_mode=pl.Buffered(1))


def _norm_modulate(x_ref, g_ref, shift_ref, scale_ref):
    xv = x_ref[...]
    r = lax.rsqrt(jnp.mean(xv * xv, axis=-1, keepdims=True) + EPS)
    a = (xv * r) * g_ref[...]
    return (a * (1.0 + scale_ref[...]) + shift_ref[...]).astype(BF16)


def norm_modulate_bwd(du, x, dx_out, g, scale, name, prev=None, exchange=None):
    s = x.shape[0]
    factors = isinstance(du, tuple)
    t = min(ROW_TILE if factors else ELT_TILE, s)
    has_prev = prev is not None

    def body(*refs):
        if factors:
            a_ref, b_ref = refs[:2]
            refs = refs[1:]
            nk, _, n = b_ref.shape
        du_ref, x_ref, dxo_ref, g_ref, sc_ref = refs[:5]
        refs = refs[5:]
        if has_prev:
            gt_ref, y_ref = refs[:2]
            refs = refs[2:]
        dx_ref, dsh_ref, dsc_ref, dg_ref = refs[:4]

        @pl.when(pl.program_id(0) == 0)
        def _():
            dsh_ref[...] = jnp.zeros_like(dsh_ref)
            dsc_ref[...] = jnp.zeros_like(dsc_ref)
            dg_ref[...] = jnp.zeros_like(dg_ref)
            if has_prev:
                refs[5][...] = jnp.zeros_like(refs[5])

        xv = x_ref[...]
        if factors:
            duv = _dot(a_ref[:, 0:n], b_ref[0], NT)
            for k in range(1, nk):
                duv = duv + _dot(a_ref[:, k * n:(k + 1) * n], b_ref[k], NT)
        else:
            duv = du_ref[...]
        gv = g_ref[...]
        r = lax.rsqrt(jnp.mean(xv * xv, axis=-1, keepdims=True) + EPS)
        nrm = xv * r
        a = nrm * gv
        dsh_ref[...] += jnp.sum(duv, axis=0, keepdims=True)
        dsc_ref[...] += jnp.sum(duv * a, axis=0, keepdims=True)
        da = duv * (1.0 + sc_ref[...])
        dg_ref[...] += jnp.sum(da * nrm, axis=0, keepdims=True)
        dn = da * gv
        dx = dxo_ref[...] + r * (dn - nrm * jnp.mean(dn * nrm, axis=-1, keepdims=True))
        dx_ref[...] = dx
        if has_prev:
            coef = prev[2]
            refs[4][...] = (coef * gt_ref[...] * dx).astype(BF16)
            refs[5][...] += coef * jnp.sum(dx * y_ref[...].astype(F32), axis=0, keepdims=True)

    vec = jax.ShapeDtypeStruct((1, D_MODEL), F32)
    if factors:
        operands = [du[0], du[1], x, dx_out, g, scale]
        in_specs = [_row_spec(t, du[0].shape[1]), _resident(du[1].shape)]
    else:
        operands = [du, x, dx_out, g, scale]
        in_specs = [_row_spec(t)]
    in_specs += [_row_spec(t), _row_spec(t), _vec_spec(), _vec_spec()]
    out_specs = [_row_spec(t), _vec_spec(), _vec_spec(), _vec_spec()]
    out_shape = [jax.ShapeDtypeStruct((s, D_MODEL), F32), vec, vec, vec]
    if has_prev:
        operands += [prev[0], prev[1]]
        in_specs += [_vec_spec(), _row_spec(t)]
        out_specs += [_row_spec(t), _vec_spec()]
        out_shape += [jax.ShapeDtypeStruct((s, D_MODEL), BF16), vec]
    return _call(body, name=name, grid=(s // t,), in_specs=in_specs, out_specs=out_specs, out_shape=out_shape,
                 operands=operands, semantics=("arbitrary",), exchange=exchange)


def ffn_up(x, norm_g, shift, scale, w_gu_t, name, exchange=None):
    s = x.shape[0]
    t = min(ROW_TILE, s)

    def body(x_ref, g_ref, sh_ref, sc_ref, w_ref, u_ref, gu_ref, act_ref):
        uv = _norm_modulate(x_ref, g_ref, sh_ref, sc_ref)
        u_ref[...] = uv
        for j in range(4):
            g = _dot(uv, w_ref[j], NT)
            up = _dot(uv, w_ref[j + 4], NT)
            gu_ref[0, j] = g.astype(BF16)
            gu_ref[1, j] = up.astype(BF16)
            act_ref[j] = (g * _sigmoid(g) * up).astype(BF16)

    return _call(
        body, name=name, grid=(s // t,),
        in_specs=[_row_spec(t), _vec_spec(), _vec_spec(), _vec_spec(), _resident(w_gu_t.shape)],
        out_specs=[_row_spec(t), pl.BlockSpec((2, 4, t, FF_BLK), lambda i: (0, 0, i, 0)),
                   pl.BlockSpec((4, t, FF_BLK), lambda i: (0, i, 0))],
        out_shape=[jax.ShapeDtypeStruct((s, D_MODEL), BF16), jax.ShapeDtypeStruct((2, 4, s, FF_BLK), BF16),
                   jax.ShapeDtypeStruct((4, s, FF_BLK), BF16)],
        operands=(x, norm_g, shift, scale, w_gu_t), semantics=("parallel",), exchange=exchange)


def residual_matmul(a, b, x, gate, coef, name, exchange=None):
    nk, s, kb = a.shape
    t = min(ROW_TILE, s)

    def body(a_ref, b_ref, x_ref, gt_ref, xo_ref, y_ref):
        y = _dot(a_ref[0], b_ref[0])
        for k in range(1, nk):
            y = y + _dot(a_ref[k], b_ref[k])
        y_ref[...] = y.astype(BF16)
        xo_ref[...] = x_ref[...] + coef * gt_ref[...] * y

    return _call(
        body, name=name, grid=(s // t,),
        in_specs=[pl.BlockSpec((nk, t, kb), lambda i: (0, i, 0)),
                  pl.BlockSpec((nk, kb, D_MODEL), lambda i: (0, 0, 0)),
                  _row_spec(t), _vec_spec()],
        out_specs=[_row_spec(t), _row_spec(t)],
        out_shape=[jax.ShapeDtypeStruct((s, D_MODEL), F32), jax.ShapeDtypeStruct((s, D_MODEL), BF16)],
        operands=(a, b, x, gate), semantics=("parallel",), exchange=exchange)


def residual_matmul_loss(a, b, x, gate, coef, target, final_g, name):
    nk, s, kb = a.shape
    t = min(ROW_TILE, s)

    def body(a_ref, b_ref, x_ref, gt_ref, t_ref, fg_ref, dx_ref, dy_ref, dgt_ref, dfg_ref, sq_ref):
        @pl.when(pl.program_id(0) == 0)
        def _():
            dgt_ref[...] = jnp.zeros_like(dgt_ref)
            dfg_ref[...] = jnp.zeros_like(dfg_ref)
            sq_ref[...] = jnp.zeros_like(sq_ref)

        y = _dot(a_ref[0], b_ref[0])
        for k in range(1, nk):
            y = y + _dot(a_ref[k], b_ref[k])
        gt = gt_ref[...]
        fg = fg_ref[...]
        xv = x_ref[...] + coef * gt * y
        r = lax.rsqrt(jnp.mean(xv * xv, axis=-1, keepdims=True) + EPS)
        nrm = xv * r
        err = nrm * fg - t_ref[...]
        sq_ref[...] += jnp.sum(err * err, axis=0, keepdims=True)
        dout = err * (1.0 / D_MODEL)
        dfg_ref[...] += jnp.sum(dout * nrm, axis=0, keepdims=True)
        dn = dout * fg
        dx = r * (dn - nrm * jnp.mean(dn * nrm, axis=-1, keepdims=True))
        dx_ref[...] = dx
        dy_ref[...] = (coef * gt * dx).astype(BF16)
        dgt_ref[...] += coef * jnp.sum(dx * y, axis=0, keepdims=True)

    vec = jax.ShapeDtypeStruct((1, D_MODEL), F32)
    return pl.pallas_call(
        body, name=name, grid=(s // t,),
        in_specs=[pl.BlockSpec((nk, t, kb), lambda i: (0, i, 0)), _resident(b.shape),
                  _row_spec(t), _vec_spec(), _row_spec(t), _vec_spec()],
        out_specs=[_row_spec(t), _row_spec(t), _vec_spec(), _vec_spec(), _vec_spec()],
        out_shape=[jax.ShapeDtypeStruct((s, D_MODEL), F32), jax.ShapeDtypeStruct((s, D_MODEL), BF16), vec, vec, vec],
        compiler_params=_params("arbitrary"),
    )(a, b, x, gate, target, final_g)


def ffn_tokens_bwd(dy, w_down, gu, w_gu_t, name, exchange=None):
    s = dy.shape[0]
    t = min(ROW_TILE, s)
    with_du = w_gu_t is not None

    def body(*refs):
        if with_du:
            dy_ref, wd_ref, gu_ref, wgu_ref, dgu_ref, du_ref = refs
        else:
            dy_ref, wd_ref, gu_ref, dgu_ref = refs
        dyv = dy_ref[...]
        du = None
        for j in range(4):
            dact = _dot(dyv, wd_ref[j], NT)
            g = gu_ref[0, j].astype(F32)
            up = gu_ref[1, j].astype(F32)
            sg = _sigmoid(g)
            slopes = (up * sg * (1.0 + g * (1.0 - sg)), g * sg)
            for half in range(2):
                d = (dact * slopes[half]).astype(BF16)
                dgu_ref[half, j] = d
                if with_du:
                    part = _dot(d, wgu_ref[4 * half + j])
                    du = part if du is None else du + part
        if with_du:
            du_ref[...] = du

    blocks = pl.BlockSpec((2, 4, t, FF_BLK), lambda i: (0, 0, i, 0))
    dgu_shape = jax.ShapeDtypeStruct((2, 4, s, FF_BLK), BF16)
    if with_du:
        return _call(
            body, name=name, grid=(s // t,),
            in_specs=[_row_spec(t), _resident(w_down.shape), blocks, _resident(w_gu_t.shape)],
            out_specs=[blocks, _row_spec(t)],
            out_shape=[dgu_shape, jax.ShapeDtypeStruct((s, D_MODEL), F32)],
            operands=(dy, w_down, gu, w_gu_t), semantics=("parallel",), exchange=exchange)
    return _call(
        body, name=name, grid=(s // t,),
        in_specs=[_row_spec(t), _resident(w_down.shape), blocks], out_specs=blocks, out_shape=dgu_shape,
        operands=(dy, w_down, gu), semantics=("parallel",), exchange=exchange)


def matmul_nt_acc(a, b, name, b_dims=NT, exchange=None):
    nk = b.shape[0]
    d, n = (b.shape[1], b.shape[2]) if b_dims == NT else (b.shape[2], b.shape[1])
    s = a.shape[-2]
    t = min(ROW_TILE, s)
    by_columns = a.ndim == 2

    def body(a_ref, b_ref, o_ref):
        def a_blk(k):
            return a_ref[:, k * n:(k + 1) * n] if by_columns else a_ref[k]

        acc = _dot(a_blk(0), b_ref[0], b_dims)
        for k in range(1, nk):
            acc = acc + _dot(a_blk(k), b_ref[k], b_dims)
        o_ref[...] = acc

    a_spec = _row_spec(t, nk * n) if by_columns else pl.BlockSpec((nk, t, n), lambda i: (0, i, 0))
    return _call(
        body, name=name, grid=(s // t,),
        in_specs=[a_spec, pl.BlockSpec(b.shape, lambda i: (0, 0, 0))],
        out_specs=pl.BlockSpec((t, d), lambda i: (i, 0)),
        out_shape=jax.ShapeDtypeStruct((s, d), F32),
        operands=(a, b), semantics=("parallel",), exchange=exchange)


def matmul_tn(a, b, name, group=(1, 1), b_cols=None, exchange=None):
    ja, s, m = a.shape
    by_columns = b.ndim == 2
    jb, n = (b.shape[1] // b_cols, b_cols) if by_columns else (b.shape[0], b.shape[2])
    ga, gb = group
    t = min(ACC_TILE, s)
    nk = s // t

    def body(a_ref, b_ref, o_ref, acc_ref):
        k = pl.program_id(2)

        @pl.when(k == 0)
        def _():
            acc_ref[...] = jnp.zeros_like(acc_ref)

        for p in range(ga):
            for q in range(gb):
                b_blk = b_ref[:, q * n:(q + 1) * n] if by_columns else b_ref[q]
                acc_ref[p, q] += _dot(a_ref[p], b_blk, TN)

        @pl.when(k == nk - 1)
        def _():
            o_ref[...] = acc_ref[...].astype(BF16)

    return _call(
        body, name=name, grid=(ja // ga, jb // gb, nk),
        in_specs=[pl.BlockSpec((ga, t, m), lambda p, q, k: (p, k, 0)),
                  pl.BlockSpec((t, gb * n), lambda p, q, k: (k, q)) if by_columns
                  else pl.BlockSpec((gb, t, n), lambda p, q, k: (q, k, 0))],
        out_specs=pl.BlockSpec((ga, gb, m, n), lambda p, q, k: (p, q, 0, 0)),
        out_shape=jax.ShapeDtypeStruct((ja, jb, m, n), BF16),
        operands=(a, b), scratch_shapes=[pltpu.VMEM((ga, gb, m, n), F32)],
        semantics=("parallel", "parallel", "arbitrary"), exchange=exchange)


def mix_in_proj(x, norm_g, shift, scale, w_mix, name, exchange=None):
    s = x.shape[0]
    t = min(ROW_TILE, s)

    def body(x_ref, g_ref, sh_ref, sc_ref, w_ref, u_ref, o_ref):
        uv = _norm_modulate(x_ref, g_ref, sh_ref, sc_ref)
        u_ref[...] = uv
        for j in range(N_DEV):
            o_ref[:, j * MIX_BLK:(j + 1) * MIX_BLK] = _dot(uv, w_ref[j]).astype(BF16)

    return _call(
        body, name=name, grid=(s // t,),
        in_specs=[_row_spec(t), _vec_spec(), _vec_spec(), _vec_spec(), _resident(w_mix.shape)],
        out_specs=[_row_spec(t), _row_spec(t, MIX_W)],
        out_shape=[jax.ShapeDtypeStruct((s, D_MODEL), BF16), jax.ShapeDtypeStruct((s, MIX_W), BF16)],
        operands=(x, norm_g, shift, scale, w_mix), semantics=("parallel",), exchange=exchange)


def _conv_taps(cc_ref, cx_ref, s):
    v = cc_ref[...].astype(F32) * cx_ref[...].astype(F32)
    tok = lax.broadcasted_iota(jnp.int32, v.shape, 0)
    v1 = jnp.where(tok >= 1, pltpu.roll(v, 1, 0), 0.0)
    v2 = jnp.where(tok >= 2, pltpu.roll(v, 2, 0), 0.0)
    return v, v1, v2, tok


def _proj_cols(s, first):
    return pl.BlockSpec((s, 128), lambda j: (0, first + j))


def short_conv(proj, conv_w, name):
    s = proj.shape[0]

    def body(cb_ref, cc_ref, cx_ref, w_ref, o_ref):
        v, v1, v2, _ = _conv_taps(cc_ref, cx_ref, s)
        y = w_ref[0:1, :] * v2 + w_ref[1:2, :] * v1 + w_ref[2:3, :] * v
        o_ref[...] = (cb_ref[...].astype(F32) * y).astype(BF16)

    return pl.pallas_call(
        body, name=name, grid=(CONV_W // 128,),
        in_specs=[_proj_cols(s, 0), _proj_cols(s, 4), _proj_cols(s, 8),
                  pl.BlockSpec((3, 128), lambda j: (0, j))],
        out_specs=pl.BlockSpec((s, 128), lambda j: (0, j)),
        out_shape=jax.ShapeDtypeStruct((s, CONV_W), BF16),
        compiler_params=_params("parallel"),
    )(proj, proj, proj, conv_w)


def short_conv_bwd(dsa, proj, conv_w, name):
    s = proj.shape[0]

    def body(dsa_ref, cb_ref, cc_ref, cx_ref, w_ref, dcb_ref, dcc_ref, dcx_ref, dw_ref):
        v, v1, v2, tok = _conv_taps(cc_ref, cx_ref, s)
        w0, w1, w2 = w_ref[0:1, :], w_ref[1:2, :], w_ref[2:3, :]
        y = w0 * v2 + w1 * v1 + w2 * v
        dsa_v = dsa_ref[...].astype(F32)
        dcb_ref[...] = (dsa_v * y).astype(BF16)
        dy = dsa_v * cb_ref[...].astype(F32)
        dw_ref[0:1, :] = jnp.sum(dy * v2, axis=0, keepdims=True)
        dw_ref[1:2, :] = jnp.sum(dy * v1, axis=0, keepdims=True)
        dw_ref[2:3, :] = jnp.sum(dy * v, axis=0, keepdims=True)
        dy1 = jnp.where(tok < s - 1, pltpu.roll(dy, s - 1, 0), 0.0)
        dy2 = jnp.where(tok < s - 2, pltpu.roll(dy, s - 2, 0), 0.0)
        dv = w2 * dy + w1 * dy1 + w0 * dy2
        dcc_ref[...] = (dv * cx_ref[...].astype(F32)).astype(BF16)
        dcx_ref[...] = (dv * cc_ref[...].astype(F32)).astype(BF16)

    col = pl.BlockSpec((s, 128), lambda j: (0, j))
    act = jax.ShapeDtypeStruct((s, CONV_W), BF16)
    return pl.pallas_call(
        body, name=name, grid=(CONV_W // 128,),
        in_specs=[col, _proj_cols(s, 0), _proj_cols(s, 4), _proj_cols(s, 8),
                  pl.BlockSpec((3, 128), lambda j: (0, j))],
        out_specs=[col, col, col, pl.BlockSpec((3, 128), lambda j: (0, j))],
        out_shape=[act, act, act, jax.ShapeDtypeStruct((3, CONV_W), F32)],
        compiler_params=_params("parallel"),
    )(dsa, proj, proj, proj, conv_w)


def _gate_specs(t):
    return [pl.BlockSpec((t, D_MODEL), lambda i: (i, 3)), pl.BlockSpec((t, D_MODEL), lambda i: (i, 4))]


def merge_forward(sa, o, proj, w_co, w_ao, b_merge, name, exchange=None):
    s = sa.shape[0]
    t = min(ROW_TILE, s)

    def body(sa_ref, o_ref, ga_ref, gb_ref, wco_ref, wao_ref, bm_ref, mg_ref, ya_ref, yb_ref):
        ya = _dot(sa_ref[...], wco_ref[...])
        yb = _dot(o_ref[...], wao_ref[...])
        sga = _sigmoid(ga_ref[...].astype(F32) + bm_ref[0:1, :])
        sgb = _sigmoid(gb_ref[...].astype(F32) + bm_ref[1:2, :])
        mg_ref[...] = (sga * ya + sgb * yb).astype(BF16)
        ya_ref[...] = ya.astype(BF16)
        yb_ref[...] = yb.astype(BF16)

    act = jax.ShapeDtypeStruct((s, D_MODEL), BF16)
    return _call(
        body, name=name, grid=(s // t,),
        in_specs=[_row_spec(t, CONV_W), _row_spec(t, ATTN_W)] + _gate_specs(t)
        + [_vec_spec(CONV_W), _vec_spec(ATTN_W), _vec_spec(2)],
        out_specs=[_row_spec(t)] * 3, out_shape=[act, act, act],
        operands=(sa, o, proj, proj, w_co, w_ao, b_merge), semantics=("parallel",), exchange=exchange)


def merge_backward(dy, w_out, proj, ya, yb, b_merge, name, exchange=None):
    s = dy.shape[0]
    t = min(ROW_TILE, s)

    def body(dy_ref, w_ref, ga_ref, gb_ref, ya_ref, yb_ref, bm_ref,
             dya_ref, dyb_ref, dga_ref, dgb_ref, dbm_ref):
        @pl.when(pl.program_id(0) == 0)
        def _():
            dbm_ref[...] = jnp.zeros_like(dbm_ref)

        dmg = _dot(dy_ref[...], w_ref[...], NT)
        sga = _sigmoid(ga_ref[...].astype(F32) + bm_ref[0:1, :])
        sgb = _sigmoid(gb_ref[...].astype(F32) + bm_ref[1:2, :])
        dya_ref[...] = (dmg * sga).astype(BF16)
        dyb_ref[...] = (dmg * sgb).astype(BF16)
        dga = dmg * ya_ref[...].astype(F32) * sga * (1.0 - sga)
        dgb = dmg * yb_ref[...].astype(F32) * sgb * (1.0 - sgb)
        dga_ref[...] = dga.astype(BF16)
        dgb_ref[...] = dgb.astype(BF16)
        dbm_ref[0:1, :] += jnp.sum(dga, axis=0, keepdims=True)
        dbm_ref[1:2, :] += jnp.sum(dgb, axis=0, keepdims=True)

    act = jax.ShapeDtypeStruct((s, D_MODEL), BF16)
    return _call(
        body, name=name, grid=(s // t,),
        in_specs=[_row_spec(t), _vec_spec(D_MODEL)] + _gate_specs(t)
        + [_row_spec(t), _row_spec(t), _vec_spec(2)],
        out_specs=[_row_spec(t)] * 4 + [_vec_spec(2)],
        out_shape=[act] * 4 + [jax.ShapeDtypeStruct((2, D_MODEL), F32)],
        operands=(dy, w_out, proj, proj, ya, yb, b_merge), semantics=("arbitrary",), exchange=exchange)


def out_proj_bwd(dya, dyb, w_co, w_ao, name):
    s = dya.shape[0]
    t = min(ROW_TILE, s)

    def body(dya_ref, dyb_ref, wco_ref, wao_ref, dsa_ref, do_ref):
        dsa_ref[...] = _dot(dya_ref[...], wco_ref[...], NT).astype(BF16)
        do_ref[...] = _dot(dyb_ref[...], wao_ref[...], NT).astype(BF16)

    return pl.pallas_call(
        body, name=name, grid=(s // t,),
        in_specs=[_row_spec(t), _row_spec(t), _vec_spec(CONV_W), _vec_spec(ATTN_W)],
        out_specs=[_row_spec(t, CONV_W), _row_spec(t, ATTN_W)],
        out_shape=[jax.ShapeDtypeStruct((s, CONV_W), BF16), jax.ShapeDtypeStruct((s, ATTN_W), BF16)],
        compiler_params=_params("parallel"),
    )(dya, dyb, w_co, w_ao)


ATT_HEADS = 4
ATT_LANES = ATT_HEADS * HEAD_DIM
ATT_UNDERFLOW = 110.0


def _softplus(z):
    return jnp.maximum(z, 0.0) + jnp.log(1.0 + jnp.exp(-jnp.abs(z)))


def _head_masks(rows):
    lane = lax.broadcasted_iota(jnp.int32, (rows, ATT_LANES), 1)
    return [(lane >= h * HEAD_DIM) & (lane < (h + 1) * HEAD_DIM) for h in range(ATT_HEADS)]


def _per_head(x, masks):
    return [jnp.where(m, x, jnp.zeros_like(x)) for m in masks]


def _att_specs(s, blk):
    first = {"q": 3 * CONV_W // ATT_LANES, "k": (3 * CONV_W + ATTN_W) // ATT_LANES,
             "v": (3 * CONV_W + 2 * ATTN_W) // ATT_LANES}
    return [pl.BlockSpec((blk, ATT_LANES), lambda h, i: (i, first["q"] + h)),
            pl.BlockSpec((s, ATT_LANES), lambda h, i: (0, first["k"] + h)),
            pl.BlockSpec((s, ATT_LANES), lambda h, i: (0, first["v"] + h))]


def _head_norms(x, masks):
    sq = jnp.square(x.astype(F32))
    return [jnp.sum(jnp.where(m, sq, 0.0), axis=1, keepdims=True) for m in masks]


def stick_breaking_fwd(proj, name, exchange=None):
    s = proj.shape[0]
    blk = ATT_BLK
    nq = s // blk

    def body(q_ref, k_ref, v_ref, o_ref, tot_ref, first_ref, kmax_ref):
        i = pl.program_id(1)
        row = lax.broadcasted_iota(jnp.int32, (blk, blk), 0)
        col = lax.broadcasted_iota(jnp.int32, (blk, blk), 1)
        tri = (row >= col).astype(BF16)
        causal = col < row
        masks = _head_masks(blk)
        q_all = q_ref[...] * ATTN_SCALE
        qs = _per_head(q_all, masks)

        @pl.when(i == 0)
        def _():
            def longest(n, best):
                norms = _head_norms(k_ref[pl.ds(pl.multiple_of(n * blk, blk), blk), :], masks)
                return tuple(jnp.maximum(b, v) for b, v in zip(best, norms))

            best = lax.fori_loop(0, nq, longest, tuple(jnp.zeros((blk, 1), F32) for _ in range(ATT_HEADS)))
            for h in range(ATT_HEADS):
                kmax_ref[h] = jnp.sqrt(jnp.max(best[h], axis=0, keepdims=True))

        needed = [jnp.sqrt(n) * kmax_ref[h] + ATT_UNDERFLOW for h, n in enumerate(_head_norms(q_all, masks))]

        def finished(laters):
            slack = laters[0] - needed[0]
            for h in range(1, ATT_HEADS):
                slack = jnp.minimum(slack, laters[h] - needed[h])
            return (jnp.min(slack) >= 0.0).astype(jnp.int32)

        def step(j, carry, diagonal):
            laters, acc = carry
            rows = pl.ds(pl.multiple_of(j * blk, blk), blk)
            kb = k_ref[rows, :]
            probs, new_laters = [], []
            for h in range(ATT_HEADS):
                z = _dot(qs[h], kb, NT)
                sp = _softplus(z)
                if diagonal:
                    sp = jnp.where(causal, sp, 0.0)
                a = jnp.exp(z - (_dot(sp.astype(BF16), tri) + laters[h]))
                if diagonal:
                    a = jnp.where(causal, a, 0.0)
                probs.append(a.astype(BF16))
                new_laters.append(laters[h] + jnp.sum(sp, axis=1, keepdims=True))
            v_heads = jnp.concatenate(_per_head(v_ref[rows, :], masks), axis=0)
            acc = acc + _dot(jnp.concatenate(probs, axis=1), v_heads)
            return tuple(new_laters), acc

        carry = (tuple(jnp.zeros((blk, 1), F32) for _ in range(ATT_HEADS)), jnp.zeros((blk, ATT_LANES), F32))
        laters, acc = step(i, carry, True)

        def further(state):
            n, _, laters, acc = state
            laters, acc = step(i - 1 - n, (laters, acc), False)
            return n + 1, finished(laters), laters, acc

        walked, _, laters, acc = lax.while_loop(
            lambda state: jnp.logical_and(state[0] < i, state[1] == 0), further,
            (jnp.int32(0), finished(laters), laters, acc))
        o_ref[...] = acc.astype(BF16)
        tot = jnp.zeros((blk, ATT_LANES), F32)
        for h in range(ATT_HEADS):
            tot = jnp.where(masks[h], laters[h], tot)
        tot_ref[...] = tot
        first_ref[...] = jnp.full(first_ref.shape, i - walked, jnp.int32).astype(F32)

    out_spec = pl.BlockSpec((blk, ATT_LANES), lambda h, i: (i, h))
    groups = N_HEADS // ATT_HEADS
    return _call(
        body, name=name, grid=(groups, nq),
        in_specs=_att_specs(s, blk),
        out_specs=[out_spec, out_spec, pl.BlockSpec((None, None, 8, 128), lambda h, i: (h, i, 0, 0))],
        out_shape=[jax.ShapeDtypeStruct((s, ATTN_W), BF16), jax.ShapeDtypeStruct((s, ATTN_W), F32),
                   jax.ShapeDtypeStruct((groups, nq, 8, 128), F32)],
        operands=(proj, proj, proj), scratch_shapes=[pltpu.VMEM((ATT_HEADS, 1, 1), F32)],
        semantics=("parallel", "arbitrary"), exchange=exchange)


def stick_breaking_bwd(proj, do, tot, first, name, exchange=None):
    s = proj.shape[0]
    blk = ATT_BLK
    nq = s // blk

    def body(q_ref, k_ref, v_ref, do_ref, tot_ref, first_ref, dq_ref, dk_ref, dv_ref):
        i = pl.program_id(1)
        start = jnp.clip(jnp.max(first_ref[...]).astype(jnp.int32), 0, i)

        @pl.when(i == 0)
        def _():
            dk_ref[...] = jnp.zeros_like(dk_ref)
            dv_ref[...] = jnp.zeros_like(dv_ref)

        row = lax.broadcasted_iota(jnp.int32, (blk, blk), 0)
        col = lax.broadcasted_iota(jnp.int32, (blk, blk), 1)
        before = (row < col).astype(BF16)
        upto = (row <= col).astype(BF16)
        causal = col < row
        masks = _head_masks(blk)
        qs = _per_head(q_ref[...] * ATTN_SCALE, masks)
        dos = _per_head(do_ref[...], masks)
        q_heads = jnp.concatenate(qs, axis=0)
        do_heads = jnp.concatenate(dos, axis=0)
        tot_all = tot_ref[...]
        totals = [jnp.max(jnp.where(m, tot_all, 0.0), axis=1, keepdims=True) for m in masks]

        def step(j, carry, diagonal):
            earliers, g_sums, dq = carry
            rows = pl.ds(pl.multiple_of(j * blk, blk), blk)
            kb = k_ref[rows, :]
            vb = v_ref[rows, :]
            probs, dzs, new_earliers, new_g_sums = [], [], [], []
            for h in range(ATT_HEADS):
                z = _dot(qs[h], kb, NT)
                sp = _softplus(z)
                if diagonal:
                    sp = jnp.where(causal, sp, 0.0)
                c = (totals[h] - earliers[h]) - _dot(sp.astype(BF16), before)
                a = jnp.exp(z - c)
                if diagonal:
                    a = jnp.where(causal, a, 0.0)
                g = a * _dot(dos[h], vb, NT)
                f = g_sums[h] + _dot(g.astype(BF16), upto)
                dz = g - jnp.exp(z - sp) * f
                if diagonal:
                    dz = jnp.where(causal, dz, 0.0)
                probs.append(a.astype(BF16))
                dzs.append(dz.astype(BF16))
                new_earliers.append(earliers[h] + jnp.sum(sp, axis=1, keepdims=True))
                new_g_sums.append(g_sums[h] + jnp.sum(g, axis=1, keepdims=True))
            k_heads = jnp.concatenate(_per_head(kb, masks), axis=0)
            dq = dq + _dot(jnp.concatenate(dzs, axis=1), k_heads)
            dk_ref[rows, :] += _dot(jnp.concatenate(dzs, axis=0), q_heads, TN)
            dv_ref[rows, :] += _dot(jnp.concatenate(probs, axis=0), do_heads, TN)
            return tuple(new_earliers), tuple(new_g_sums), dq

        zeros = tuple(jnp.zeros((blk, 1), F32) for _ in range(ATT_HEADS))
        carry = (zeros, zeros, jnp.zeros((blk, ATT_LANES), F32))
        carry = lax.fori_loop(start, i, lambda j, c: step(j, c, False), carry)
        dq = step(i, carry, True)[2]
        dq_ref[...] = (dq * ATTN_SCALE).astype(BF16)

    blk_spec = pl.BlockSpec((blk, ATT_LANES), lambda h, i: (i, h))
    full_spec = pl.BlockSpec((s, ATT_LANES), lambda h, i: (0, h))
    return _call(
        body, name=name, grid=(N_HEADS // ATT_HEADS, nq),
        in_specs=_att_specs(s, blk) + [blk_spec, blk_spec,
                                       pl.BlockSpec((None, None, 8, 128), lambda h, i: (h, i, 0, 0))],
        out_specs=[blk_spec, full_spec, full_spec],
        out_shape=[jax.ShapeDtypeStruct((s, ATTN_W), BF16), jax.ShapeDtypeStruct((s, ATTN_W), F32),
                   jax.ShapeDtypeStruct((s, ATTN_W), F32)],
        operands=(proj, proj, proj, do, tot, first), semantics=("parallel", "arbitrary"), exchange=exchange)


def adamw(w, m, v, parts, name):
    r, c = w.shape
    p = parts.shape[0]
    t = r
    for cand in (256, 176):
        if r % cand == 0 and r > cand:
            t = cand
            break

    def body(w_ref, m_ref, v_ref, p_ref, g_ref, d_ref, mo_ref, vo_ref):
        g = p_ref[0].astype(F32)
        for n in range(1, p):
            g = g + p_ref[n].astype(F32)
        m_new = ADAM_B1 * m_ref[...] + (1.0 - ADAM_B1) * g
        v_new = ADAM_B2 * v_ref[...] + (1.0 - ADAM_B2) * (g * g)
        m_hat = m_new / ADAM_BC1
        v_hat = v_new / ADAM_BC2
        g_ref[...] = g
        d_ref[...] = -ADAM_LR * (m_hat / (jnp.sqrt(v_hat) + ADAM_EPS) + ADAM_WD * w_ref[...])
        mo_ref[...] = m_new
        vo_ref[...] = v_new

    spec = pl.BlockSpec((t, c), lambda i: (i, 0))
    out = jax.ShapeDtypeStruct((r, c), F32)
    return pl.pallas_call(
        body, name=name, grid=(r // t,),
        in_specs=[spec, spec, spec, pl.BlockSpec((p, t, c), lambda i: (0, i, 0))],
        out_specs=[spec] * 4, out_shape=[out] * 4,
        compiler_params=_params("parallel"),
    )(w, m, v, parts)


def kernel(x, c, w_ada, b_ada, norm1_g, ffn1_w_gu, ffn1_w_down, norm2_g, w_mix_in, b_merge, conv_w, w_conv_out, w_attn_out, w_out, norm3_g, ffn2_w_gu, ffn2_w_down, final_g, loss_target, m_w_ada, m_b_ada, m_norm1_g, m_ffn1_w_gu, m_ffn1_w_down, m_norm2_g, m_w_mix_in, m_b_merge, m_conv_w, m_w_conv_out, m_w_attn_out, m_w_out, m_norm3_g, m_ffn2_w_gu, m_ffn2_w_down, m_final_g, v_w_ada, v_b_ada, v_norm1_g, v_ffn1_w_gu, v_ffn1_w_down, v_norm2_g, v_w_mix_in, v_b_merge, v_conv_w, v_w_conv_out, v_w_attn_out, v_w_out, v_norm3_g, v_ffn2_w_gu, v_ffn2_w_down, v_final_g):
    s = x.shape[1]
    me = 4 * lax.axis_index("x") + 2 * lax.axis_index("y") + lax.axis_index("c")
    x0 = x[0]
    target = loss_target[0]
    final_g2 = final_g.reshape(1, D_MODEL)

    def shard(w):
        return w[0].astype(BF16)

    def flipped(w):
        return jnp.swapaxes(w, 1, 2)

    def rows8(g):
        return g.reshape(N_DEV, -1, D_MODEL)

    small_in = jnp.concatenate([c.reshape(-1), b_merge.reshape(-1), conv_w.reshape(-1),
                                jnp.zeros((64,), F32)]).reshape(1, -1)
    n_ada = w_ada.shape[2]
    b_cols = lax.dynamic_slice(b_ada, (0, me * n_ada), (1, n_ada))
    small_all, mod_all, (wgu1, wd1) = prologue(small_in, w_ada[0], b_cols,
                                               [shard(flipped(ffn1_w_gu)), shard(ffn1_w_down)], "prologue")
    wd1 = wd1.reshape(4, FF_BLK, D_MODEL)
    small_all = small_all[:, 0, :]
    c_all = small_all[:, :D_MODEL]
    bm_full = small_all[:, 1024:1280].reshape(8, 2, 128).transpose(1, 0, 2).reshape(2, D_MODEL)
    cw_full = small_all[:, 1280:1472].reshape(8, 3, 64).transpose(1, 0, 2).reshape(3, CONV_W)
    mod = lax.dynamic_index_in_dim(mod_all, me, axis=1, keepdims=False).reshape(9, 1, D_MODEL)
    sh1, sc1, gt1, sh2, sc2, gt2, sh3, sc3, gt3 = [mod[n] for n in range(9)]

    (u1, gu1, act1), got = ffn_up(x0, norm1_g, sh1, sc1, wgu1, "ffn_up_1",
                                  exchange=gather_stage1([shard(w_mix_in)]))
    (x1, y1), (wmix, *got) = residual_matmul(
        act1, wd1, x0, gt1, 0.5, "ffn_down_1", exchange=merge_exchanges(
            gather_stage2(got), gather_stage1([shard(w_conv_out), shard(w_attn_out), shard(w_out)])))

    (u2, proj), (wco, wao, wout) = mix_in_proj(x1, norm2_g, sh2, sc2, wmix, "mix_in",
                                               exchange=gather_stage2(got))
    wco = wco.transpose(1, 0, 2).reshape(CONV_W, D_MODEL)
    wao = wao.transpose(1, 0, 2).reshape(ATTN_W, D_MODEL)
    wout = wout.reshape(D_MODEL, D_MODEL)
    sa = short_conv(proj, cw_full, "short_conv")
    (o, tot, first), got = stick_breaking_fwd(proj, "attn_fwd",
                                       exchange=gather_stage1([shard(flipped(ffn2_w_gu)), shard(ffn2_w_down)]))
    (merged, ya, yb), (wgu3, wd3) = merge_forward(sa, o, proj, wco, wao, bm_full, "merge",
                                                  exchange=gather_stage2(got))
    wd3 = wd3.reshape(4, FF_BLK, D_MODEL)
    x2, y2 = residual_matmul(merged[None], wout[None], x1, gt2, 1.0, "out_proj")

    u3, gu3, act3 = ffn_up(x2, norm3_g, sh3, sc3, wgu3, "ffn_up_3")

    dx3, dy3, dgt3, dfinal, sq = residual_matmul_loss(act3, wd3, x2, gt3, 0.5, target, final_g2, "ffn_down_3_loss")
    dgu3, du3 = ffn_tokens_bwd(dy3, wd3, gu3, wgu3, "ffn_bwd_3")
    dgu3 = dgu3.reshape(8, s, FF_BLK)
    g_wd3 = rows8(matmul_tn(act3, dy3[None], "grad_w_down_3", group=(4, 1)))
    g_wgu3 = matmul_tn(dgu3, u3[None], "grad_w_gu_3", group=(4, 1)).reshape(8, FF_BLK, D_MODEL)
    dx2, dsh3, dsc3, dn3, dy2, dgt2 = norm_modulate_bwd(du3, x2, dx3, norm3_g, sc3, "norm_bwd_3",
                                                        prev=(gt2, y2, 1.0))

    (dya, dyb, dga, dgb, dbm), pairs = merge_backward(dy2, wout, proj, ya, yb, bm_full, "merge_bwd",
                                                      exchange=scatter_stage1([g_wgu3, g_wd3]))
    sums3 = [pair_sum(g_wgu3, pairs[0], "pair_sum_w_gu_3"), pair_sum(g_wd3, pairs[1], "pair_sum_w_down_3")]
    g_wout = rows8(matmul_tn(merged[None], dy2[None], "grad_w_out"))
    dsa, do = out_proj_bwd(dya, dyb, wco, wao, "out_proj_bwd")
    g_wco = matmul_tn(sa[None], dya[None], "grad_w_conv_out").reshape(CONV_W, N_DEV, 128).transpose(1, 0, 2)
    g_wao = matmul_tn(o[None], dyb[None], "grad_w_attn_out").reshape(ATTN_W, N_DEV, 128).transpose(1, 0, 2)
    dcb, dcc, dcx, dconv = short_conv_bwd(dsa, proj, cw_full, "short_conv_bwd")
    (dq, dk, dv), landed3 = stick_breaking_bwd(proj, do, tot, first, "attn_bwd", exchange=scatter_stage2(sums3))
    dproj = jnp.concatenate([dcb, dcc, dcx, dq, dk.astype(BF16), dv.astype(BF16), dga, dgb], axis=1)
    g_wmix = matmul_tn(u2[None], dproj, "grad_w_mix_in", group=(1, 4), b_cols=MIX_BLK).reshape(
        N_DEV, D_MODEL, MIX_BLK)
    mixer_grads = [g_wmix, g_wco, g_wao, g_wout]
    (dx1, dsh2, dsc2, dn2, dy1, dgt1), pairs = norm_modulate_bwd(
        (dproj, wmix), x1, dx2, norm2_g, sc2, "norm_bwd_2", prev=(gt1, y1, 0.5),
        exchange=scatter_stage1(mixer_grads))
    sums_mix = [pair_sum(g, p, f"pair_sum_mixer_{n}") for n, (g, p) in enumerate(zip(mixer_grads, pairs))]

    dgu1, landed_mix = ffn_tokens_bwd(dy1, wd1, gu1, None, "ffn_dact_1", exchange=scatter_stage2(sums_mix[:1]))
    dgu1 = dgu1.reshape(8, s, FF_BLK)
    g_wgu1, landed_small = matmul_tn(dgu1, u1[None], "grad_w_gu_1", group=(4, 1),
                                     exchange=scatter_stage2(sums_mix[1:]))
    g_wgu1 = g_wgu1.reshape(8, FF_BLK, D_MODEL)
    g_wd1, pairs = matmul_tn(act1, dy1[None], "grad_w_down_1", group=(4, 1), exchange=scatter_stage1([g_wgu1]))
    g_wd1 = rows8(g_wd1)
    sum_gu1 = pair_sum(g_wgu1, pairs[0], "pair_sum_w_gu_1")
    du1, (landed_gu1, pair_d1) = matmul_nt_acc(
        dgu1, wgu1, "ffn_du_1", b_dims=NN, exchange=merge_exchanges(scatter_stage2([sum_gu1]), scatter_stage1([g_wd1])))
    sum_d1 = pair_sum(g_wd1, pair_d1, "pair_sum_w_down_1")
    (grad_x, dsh1, dsc1, dn1), landed_d1 = norm_modulate_bwd(du1, x0, dx1, norm1_g, sc1, "norm_bwd_1",
                                                            exchange=scatter_stage2([sum_d1]))

    loss_local = (0.5 / D_MODEL) * jnp.sum(sq)
    stats = jnp.concatenate(
        [v.reshape(-1) for v in (dsh1, dsc1, dgt1, dsh2, dsc2, dgt2, dsh3, dsc3, dgt3,
                                 dn1, dn2, dn3, dfinal, dbm, dconv)]
        + [jnp.broadcast_to(loss_local, (128,))]).reshape(1, -1)
    stats_all = all_gather_rows(stats, "gather_stats")
    n_mod = 9 * D_MODEL
    loss = jnp.sum(stats_all[:, 0, -1])
    dmod_all = stats_all[:, :, :n_mod]
    off = n_mod
    parts = {}
    for key in ("norm1_g", "norm2_g", "norm3_g", "final_g"):
        parts[key] = stats_all[:, :, off:off + D_MODEL]
        off += D_MODEL
    dbm_all = stats_all[:, 0, off:off + 2 * D_MODEL].reshape(N_DEV, 2, D_MODEL)
    off += 2 * D_MODEL
    dcw_all = stats_all[:, 0, off:off + 3 * CONV_W].reshape(N_DEV, 3, CONV_W)
    parts["b_merge"] = lax.dynamic_slice(dbm_all, (0, 0, me * 128), (N_DEV, 2, 128))
    parts["conv_w"] = lax.dynamic_slice(dcw_all, (0, 0, me * 64), (N_DEV, 3, 64))
    dmod_cols = lax.dynamic_slice(dmod_all[:, 0, :], (0, me * n_ada), (N_DEV, n_ada))
    parts["w_ada"] = ada_backward(c_all, dmod_cols, "ada_backward")[None]
    parts["b_ada"] = dmod_all
    parts["ffn2_w_gu"], parts["ffn2_w_down"] = landed3
    parts["w_mix_in"] = landed_mix[0]
    parts["w_conv_out"], parts["w_attn_out"], parts["w_out"] = landed_small
    parts["ffn1_w_gu"] = landed_gu1
    parts["ffn1_w_down"] = landed_d1[0]

    given = dict(w_ada=w_ada, b_ada=b_ada, norm1_g=norm1_g, ffn1_w_gu=ffn1_w_gu, ffn1_w_down=ffn1_w_down,
                 norm2_g=norm2_g, w_mix_in=w_mix_in, b_merge=b_merge, conv_w=conv_w, w_conv_out=w_conv_out,
                 w_attn_out=w_attn_out, w_out=w_out, norm3_g=norm3_g, ffn2_w_gu=ffn2_w_gu,
                 ffn2_w_down=ffn2_w_down, final_g=final_g)
    moments_m = dict(w_ada=m_w_ada, b_ada=m_b_ada, norm1_g=m_norm1_g, ffn1_w_gu=m_ffn1_w_gu,
                     ffn1_w_down=m_ffn1_w_down, norm2_g=m_norm2_g, w_mix_in=m_w_mix_in, b_merge=m_b_merge,
                     conv_w=m_conv_w, w_conv_out=m_w_conv_out, w_attn_out=m_w_attn_out, w_out=m_w_out,
                     norm3_g=m_norm3_g, ffn2_w_gu=m_ffn2_w_gu, ffn2_w_down=m_ffn2_w_down, final_g=m_final_g)
    moments_v = dict(w_ada=v_w_ada, b_ada=v_b_ada, norm1_g=v_norm1_g, ffn1_w_gu=v_ffn1_w_gu,
                     ffn1_w_down=v_ffn1_w_down, norm2_g=v_norm2_g, w_mix_in=v_w_mix_in, b_merge=v_b_merge,
                     conv_w=v_conv_w, w_conv_out=v_w_conv_out, w_attn_out=v_w_attn_out, w_out=v_w_out,
                     norm3_g=v_norm3_g, ffn2_w_gu=v_ffn2_w_gu, ffn2_w_down=v_ffn2_w_down, final_g=v_final_g)
    order = ["w_ada", "b_ada", "norm1_g", "ffn1_w_gu", "ffn1_w_down", "norm2_g", "w_mix_in", "b_merge",
             "conv_w", "w_conv_out", "w_attn_out", "w_out", "norm3_g", "ffn2_w_gu", "ffn2_w_down", "final_g"]
    grads, deltas, new_m, new_v = [], [], [], []
    for key in order:
        turn = flipped if key in ("ffn1_w_gu", "ffn2_w_gu") else (lambda a: a)
        shape = turn(given[key]).shape
        shape2 = (1, shape[0]) if len(shape) == 1 else shape[-2:]
        outs = adamw(turn(given[key]).reshape(shape2), turn(moments_m[key]).reshape(shape2),
                     turn(moments_v[key]).reshape(shape2), parts[key], f"adamw_{key}")
        for dst, val in zip((grads, deltas, new_m, new_v), outs):
            dst.append(turn(val.reshape(shape)))

    return (loss, grad_x[None], *grads, *deltas, *new_m, *new_v)
```

```python
import functools
from typing import Callable, NamedTuple

import jax
import jax.numpy as jnp
from jax import lax
from jax.experimental import pallas as pl
from jax.experimental.pallas import tpu as pltpu

F32 = jnp.float32
BF16 = jnp.bfloat16
MESH = pl.DeviceIdType.MESH
ANY = pl.BlockSpec(memory_space=pl.ANY)

N_DEV = 8
D_MODEL = 1024
D_FF = 2816
FF_BLK = D_FF // 4
N_HEADS = 8
HEAD_DIM = 64
CONV_W = 512
ATTN_W = 512
MIX_W = 3 * CONV_W + 3 * ATTN_W + 2 * D_MODEL
MIX_BLK = MIX_W // N_DEV
EPS = 1e-6
ATTN_SCALE = HEAD_DIM ** -0.5

ADAM_LR = 0.001
ADAM_B1 = 0.9
ADAM_B2 = 0.999
ADAM_EPS = 1e-08
ADAM_WD = 0.01
ADAM_STEP = 10
ADAM_BC1 = 1.0 - ADAM_B1 ** ADAM_STEP
ADAM_BC2 = 1.0 - ADAM_B2 ** ADAM_STEP

VMEM_LIMIT = 56 * 1024 * 1024
ROW_TILE = 512
ACC_TILE = 1024
ELT_TILE = 256
ATT_BLK = 256

NN = (((1,), (0,)), ((), ()))
NT = (((1,), (1,)), ((), ()))
TN = (((0,), (0,)), ((), ()))


def _dot(a, b, dims=NN):
    return lax.dot_general(a, b, dims, preferred_element_type=F32)


def _params(*sem):
    return pltpu.CompilerParams(dimension_semantics=sem, vmem_limit_bytes=VMEM_LIMIT)


def _sigmoid(x):
    return 1.0 / (1.0 + jnp.exp(-x))


def _me():
    x, y, c = lax.axis_index("x"), lax.axis_index("y"), lax.axis_index("c")
    return x, y, c, 4 * x + 2 * y + c


def _peer(k):
    x, y, c, _ = _me()
    px = 1 - x if (k >> 2) & 1 else x
    py = 1 - y if (k >> 1) & 1 else y
    pc = 1 - c if k & 1 else c
    return (px, py, pc), 4 * px + 2 * py + pc


class Exchange(NamedTuple):
    operands: tuple
    out_shapes: tuple
    aliases: dict
    n_remote: int
    n_local: int
    copies: Callable


CHIP_FLIPS = (2, 4, 6)
SIBLING = 1


def _remote(src, dst, send_sems, recv_sems, n, peer):
    return pltpu.make_async_remote_copy(src_ref=src, dst_ref=dst, send_sem=send_sems.at[n], recv_sem=recv_sems.at[n],
                                        device_id=peer, device_id_type=MESH)


def gather_stage1(shards):
    n = len(shards)
    rels = (SIBLING,) + CHIP_FLIPS

    def copies(ins, outs, send_sems, recv_sems, local_sems, rb, lb):
        _, _, _, me = _me()
        cps = []
        for w in range(n):
            cps.append(pltpu.make_async_copy(ins[w], outs[w].at[me], local_sems.at[lb + w]))
            for a, k in enumerate(rels):
                peer, _ = _peer(k)
                cps.append(_remote(ins[w], outs[w].at[me], send_sems, recv_sems, rb + len(rels) * w + a, peer))
        return cps

    shapes = tuple(jax.ShapeDtypeStruct((N_DEV,) + s.shape, s.dtype) for s in shards)
    return Exchange(tuple(shards), shapes, {}, len(rels) * n, n, copies)


def gather_stage2(fulls):
    n = len(fulls)

    def copies(ins, outs, send_sems, recv_sems, local_sems, rb, lb):
        sibling, _ = _peer(SIBLING)
        cps = []
        for w in range(n):
            for a, k in enumerate(CHIP_FLIPS):
                _, blk = _peer(k)
                cps.append(_remote(outs[w].at[blk], outs[w].at[blk], send_sems, recv_sems, rb + 3 * w + a, sibling))
        return cps

    shapes = tuple(jax.ShapeDtypeStruct(f.shape, f.dtype) for f in fulls)
    return Exchange(tuple(fulls), shapes, {w: w for w in range(n)}, 3 * n, 0, copies)


def scatter_stage1(fulls):
    n = len(fulls)

    def copies(ins, outs, send_sems, recv_sems, local_sems, rb, lb):
        _, _, c, _ = _me()
        sibling, _ = _peer(SIBLING)
        cps = []
        for w in range(n):
            for q in range(4):
                cps.append(_remote(ins[w].at[2 * q + (1 - c)], outs[w].at[q], send_sems, recv_sems, rb + 4 * w + q, sibling))
        return cps

    shapes = tuple(jax.ShapeDtypeStruct((4,) + f.shape[1:], f.dtype) for f in fulls)
    return Exchange(tuple(fulls), shapes, {}, 4 * n, 0, copies)


def scatter_stage2(sums):
    n = len(sums)

    def copies(ins, outs, send_sems, recv_sems, local_sems, rb, lb):
        x, y, _, _ = _me()
        mine = 2 * x + y
        cps = []
        for w in range(n):
            cps.append(pltpu.make_async_copy(ins[w].at[mine], outs[w].at[mine], local_sems.at[lb + w]))
            for a, k in enumerate(CHIP_FLIPS):
                peer, _ = _peer(k)
                cps.append(_remote(ins[w].at[2 * peer[0] + peer[1]], outs[w].at[mine], send_sems, recv_sems,
                                   rb + 3 * w + a, peer))
        return cps

    shapes = tuple(jax.ShapeDtypeStruct(s.shape, s.dtype) for s in sums)
    return Exchange(tuple(sums), shapes, {}, 3 * n, n, copies)


def merge_exchanges(a, b):
    na_in, na_out = len(a.operands), len(a.out_shapes)

    def copies(ins, outs, send_sems, recv_sems, local_sems, rb, lb):
        return (a.copies(ins[:na_in], outs[:na_out], send_sems, recv_sems, local_sems, rb, lb)
                + b.copies(ins[na_in:], outs[na_out:], send_sems, recv_sems, local_sems, rb + a.n_remote, lb + a.n_local))

    aliases = dict(a.aliases)
    aliases.update({na_in + i: na_out + o for i, o in b.aliases.items()})
    return Exchange(a.operands + b.operands, a.out_shapes + b.out_shapes, aliases,
                    a.n_remote + b.n_remote, a.n_local + b.n_local, copies)


def _exchange_scratch(ex):
    return [pltpu.SemaphoreType.DMA((ex.n_remote,)), pltpu.SemaphoreType.DMA((ex.n_remote,)),
            pltpu.SemaphoreType.DMA((max(ex.n_local, 1),))]


def run_exchange(ex, name):
    n_in, n_out = len(ex.operands), len(ex.out_shapes)

    def body(*refs):
        cps = ex.copies(refs[:n_in], refs[n_in:n_in + n_out], *refs[n_in + n_out:], 0, 0)
        for cp in cps:
            cp.start()
        for cp in cps:
            cp.wait()

    return pl.pallas_call(
        body, name=name, out_shape=list(ex.out_shapes), in_specs=[ANY] * n_in, out_specs=[ANY] * n_out,
        scratch_shapes=_exchange_scratch(ex), input_output_aliases=dict(ex.aliases),
    )(*ex.operands)


def _call(body, *, name, grid, in_specs, out_specs, out_shape, operands, scratch_shapes=(), semantics=(),
          exchange=None):
    if exchange is None:
        return pl.pallas_call(
            body, name=name, grid=grid, in_specs=in_specs, out_specs=out_specs, out_shape=out_shape,
            scratch_shapes=list(scratch_shapes), compiler_params=_params(*semantics))(*operands)
    single = not isinstance(out_shape, (list, tuple))
    out_shapes = [out_shape] if single else list(out_shape)
    out_specs_l = [out_specs] if single else list(out_specs)
    n_in, n_out, n_scr = len(operands), len(out_shapes), len(scratch_shapes)
    x_in, x_out = len(exchange.operands), len(exchange.out_shapes)

    def hosted(*refs):
        ins, refs = refs[:n_in], refs[n_in:]
        xin, refs = refs[:x_in], refs[x_in:]
        outs, refs = refs[:n_out], refs[n_out:]
        xout, refs = refs[:x_out], refs[x_out:]
        scr, sems = refs[:n_scr], refs[n_scr:]
        first = functools.reduce(jnp.logical_and, [pl.program_id(a) == 0 for a in range(len(grid))])
        last = functools.reduce(jnp.logical_and, [pl.program_id(a) == g - 1 for a, g in enumerate(grid)])

        @pl.when(first)
        def _():
            for cp in exchange.copies(xin, xout, *sems, 0, 0):
                cp.start()

        body(*ins, *outs, *scr)

        @pl.when(last)
        def _():
            for cp in exchange.copies(xin, xout, *sems, 0, 0):
                cp.wait()

    res = pl.pallas_call(
        hosted, name=name, grid=grid,
        in_specs=list(in_specs) + [ANY] * x_in, out_specs=out_specs_l + [ANY] * x_out,
        out_shape=out_shapes + list(exchange.out_shapes),
        scratch_shapes=list(scratch_shapes) + _exchange_scratch(exchange),
        input_output_aliases={n_in + i: n_out + o for i, o in exchange.aliases.items()},
        compiler_params=_params(*(["arbitrary"] * len(grid))),
    )(*operands, *exchange.operands)
    outs, xouts = res[:n_out], res[n_out:]
    return (outs[0] if single else outs), xouts


def all_gather_rows(v, name):
    r, n = v.shape

    def body(v_ref, out_ref, send_sems, recv_sems):
        _, _, _, me = _me()
        out_ref[me] = v_ref[...]
        copies = []
        for k in range(1, N_DEV):
            peer, _ = _peer(k)
            copies.append(_remote(v_ref, out_ref.at[me], send_sems, recv_sems, k - 1, peer))
        for cp in copies:
            cp.start()
        for cp in copies:
            cp.wait()

    return pl.pallas_call(
        body, name=name,
        out_shape=jax.ShapeDtypeStruct((N_DEV, r, n), v.dtype),
        in_specs=[pl.BlockSpec(memory_space=pltpu.VMEM)],
        out_specs=pl.BlockSpec(memory_space=pltpu.VMEM),
        scratch_shapes=[pltpu.SemaphoreType.DMA((N_DEV - 1,)), pltpu.SemaphoreType.DMA((N_DEV - 1,))],
    )(v)


def pair_sum(full, pair, name):
    _, r, c = full.shape
    t = r
    core = lax.axis_index("c").astype(jnp.int32).reshape(1)

    def body(core_ref, f_ref, p_ref, o_ref):
        o_ref[...] = (f_ref[...].astype(F32) + p_ref[...].astype(F32)).astype(BF16)

    return pl.pallas_call(
        body, name=name,
        grid_spec=pltpu.PrefetchScalarGridSpec(
            num_scalar_prefetch=1, grid=(4, r // t),
            in_specs=[pl.BlockSpec((None, None, t, c), lambda q, i, core_ref: (q, core_ref[0], i, 0)),
                      pl.BlockSpec((None, t, c), lambda q, i, core_ref: (q, i, 0))],
            out_specs=pl.BlockSpec((None, t, c), lambda q, i, core_ref: (q, i, 0))),
        out_shape=jax.ShapeDtypeStruct((4, r, c), BF16),
        compiler_params=_params("parallel", "parallel"),
    )(core, full.reshape(4, 2, r, c), pair)


def prologue(small_in, w_ada, b_cols, shards, name):
    ex = gather_stage1(shards)
    n_sh = len(shards)
    n_small = small_in.shape[1]
    cols = w_ada.shape[1]

    def body(*refs):
        small_ref, w_ref, b_ref = refs[:3]
        shard_refs = refs[3:3 + n_sh]
        small_out, mod_out = refs[3 + n_sh:5 + n_sh]
        fulls = refs[5 + n_sh:5 + 2 * n_sh]
        part_ref, send1, recv1, send2, recv2, wsend, wrecv, wlocal, fsend, frecv = refs[5 + 2 * n_sh:]
        _, _, _, me = _me()
        def start_gather(src_ref, dst_ref, send_sems, recv_sems):
            cps = [_remote(src_ref, dst_ref.at[me], send_sems, recv_sems, k - 1, _peer(k)[0]) for k in range(1, N_DEV)]
            for cp in cps:
                cp.start()
            return cps

        small_out[me] = small_ref[...]
        first = start_gather(small_ref, small_out, send1, recv1)
        big = ex.copies(shard_refs, fulls, wsend, wrecv, wlocal, 0, 0)
        for cp in big:
            cp.start()
        for cp in first:
            cp.wait()
        c_all = jnp.concatenate([small_out[d][:, :D_MODEL] for d in range(N_DEV)], axis=0)
        act = c_all * _sigmoid(c_all)
        part_ref[...] = jnp.dot(act, w_ref[...], precision=lax.Precision.HIGHEST,
                                preferred_element_type=F32) + b_ref[...]
        mod_out[me] = part_ref[...]
        for cp in start_gather(part_ref, mod_out, send2, recv2):
            cp.wait()
        per = 2 + len(CHIP_FLIPS)
        sibling, _ = _peer(SIBLING)
        onward = []
        for w in range(n_sh):
            for a, k in enumerate(CHIP_FLIPS):
                _, blk = _peer(k)
                big[per * w + 2 + a].wait_recv()
                cp = _remote(fulls[w].at[blk], fulls[w].at[blk], fsend, frecv, len(CHIP_FLIPS) * w + a, sibling)
                cp.start()
                onward.append(cp)
        for w in range(n_sh):
            big[per * w].wait()
            big[per * w + 1].wait()
            for a in range(len(CHIP_FLIPS)):
                big[per * w + 2 + a].wait_send()
        for cp in onward:
            cp.wait()

    vmem = pl.BlockSpec(memory_space=pltpu.VMEM)
    sems = pltpu.SemaphoreType.DMA((N_DEV - 1,))
    res = pl.pallas_call(
        body, name=name,
        out_shape=[jax.ShapeDtypeStruct((N_DEV, 1, n_small), F32), jax.ShapeDtypeStruct((N_DEV, N_DEV, cols), F32)]
        + list(ex.out_shapes),
        in_specs=[vmem, vmem, vmem] + [ANY] * n_sh, out_specs=[vmem, vmem] + [ANY] * n_sh,
        scratch_shapes=[pltpu.VMEM((N_DEV, cols), F32), sems, sems, sems, sems] + _exchange_scratch(ex)
        + [pltpu.SemaphoreType.DMA((len(CHIP_FLIPS) * n_sh,)), pltpu.SemaphoreType.DMA((len(CHIP_FLIPS) * n_sh,))],
        compiler_params=pltpu.CompilerParams(vmem_limit_bytes=VMEM_LIMIT),
    )(small_in, w_ada, b_cols, *shards)
    return res[0], res[1], res[2:]


def ada_backward(c_all, dmod_cols, name):
    n = dmod_cols.shape[1]

    def body(c_ref, d_ref, o_ref):
        c = c_ref[...]
        act = c * _sigmoid(c)
        o_ref[...] = lax.dot_general(act, d_ref[...], TN, precision=lax.Precision.HIGHEST,
                                     preferred_element_type=F32)

    return pl.pallas_call(
        body, name=name, out_shape=jax.ShapeDtypeStruct((D_MODEL, n), F32),
        compiler_params=pltpu.CompilerParams(vmem_limit_bytes=VMEM_LIMIT),
    )(c_all, dmod_cols)


def _row_spec(t, width=D_MODEL):
    return pl.BlockSpec((t, width), lambda i: (i, 0))


def _vec_spec(rows=1, width=D_MODEL):
    return pl.BlockSpec((rows, width), lambda i: (0, 0))


def _resident(shape):
    return pl.BlockSpec(shape, lambda i: (0,) * len(shape), pipeline_mode=pl.Buffered(1))


def _norm_modulate(x_ref, g_ref, shift_ref, scale_ref):
    xv = x_ref[...]
    r = lax.rsqrt(jnp.mean(xv * xv, axis=-1, keepdims=True) + EPS)
    a = (xv * r) * g_ref[...]
    return (a * (1.0 + scale_ref[...]) + shift_ref[...]).astype(BF16)


def norm_modulate_bwd(du, x, dx_out, g, scale, name, prev=None, exchange=None):
    s = x.shape[0]
    factors = isinstance(du, tuple)
    t = min(ROW_TILE if factors else ELT_TILE, s)
    has_prev = prev is not None

    def body(*refs):
        if factors:
            a_ref, b_ref = refs[:2]
            refs = refs[1:]
            nk, _, n = b_ref.shape
        du_ref, x_ref, dxo_ref, g_ref, sc_ref = refs[:5]
        refs = refs[5:]
        if has_prev:
            gt_ref, y_ref = refs[:2]
            refs = refs[2:]
        dx_ref, dsh_ref, dsc_ref, dg_ref = refs[:4]

        @pl.when(pl.program_id(0) == 0)
        def _():
            dsh_ref[...] = jnp.zeros_like(dsh_ref)
            dsc_ref[...] = jnp.zeros_like(dsc_ref)
            dg_ref[...] = jnp.zeros_like(dg_ref)
            if has_prev:
                refs[5][...] = jnp.zeros_like(refs[5])

        xv = x_ref[...]
        if factors:
            duv = _dot(a_ref[:, 0:n], b_ref[0], NT)
            for k in range(1, nk):
                duv = duv + _dot(a_ref[:, k * n:(k + 1) * n], b_ref[k], NT)
        else:
            duv = du_ref[...]
        gv = g_ref[...]
        r = lax.rsqrt(jnp.mean(xv * xv, axis=-1, keepdims=True) + EPS)
        nrm = xv * r
        a = nrm * gv
        dsh_ref[...] += jnp.sum(duv, axis=0, keepdims=True)
        dsc_ref[...] += jnp.sum(duv * a, axis=0, keepdims=True)
        da = duv * (1.0 + sc_ref[...])
        dg_ref[...] += jnp.sum(da * nrm, axis=0, keepdims=True)
        dn = da * gv
        dx = dxo_ref[...] + r * (dn - nrm * jnp.mean(dn * nrm, axis=-1, keepdims=True))
        dx_ref[...] = dx
        if has_prev:
            coef = prev[2]
            refs[4][...] = (coef * gt_ref[...] * dx).astype(BF16)
            refs[5][...] += coef * jnp.sum(dx * y_ref[...].astype(F32), axis=0, keepdims=True)

    vec = jax.ShapeDtypeStruct((1, D_MODEL), F32)
    if factors:
        operands = [du[0], du[1], x, dx_out, g, scale]
        in_specs = [_row_spec(t, du[0].shape[1]), _resident(du[1].shape)]
    else:
        operands = [du, x, dx_out, g, scale]
        in_specs = [_row_spec(t)]
    in_specs += [_row_spec(t), _row_spec(t), _vec_spec(), _vec_spec()]
    out_specs = [_row_spec(t), _vec_spec(), _vec_spec(), _vec_spec()]
    out_shape = [jax.ShapeDtypeStruct((s, D_MODEL), F32), vec, vec, vec]
    if has_prev:
        operands += [prev[0], prev[1]]
        in_specs += [_vec_spec(), _row_spec(t)]
        out_specs += [_row_spec(t), _vec_spec()]
        out_shape += [jax.ShapeDtypeStruct((s, D_MODEL), BF16), vec]
    return _call(body, name=name, grid=(s // t,), in_specs=in_specs, out_specs=out_specs, out_shape=out_shape,
                 operands=operands, semantics=("arbitrary",), exchange=exchange)


def ffn_up(x, norm_g, shift, scale, w_gu_t, name, exchange=None):
    s = x.shape[0]
    t = min(ROW_TILE, s)

    def body(x_ref, g_ref, sh_ref, sc_ref, w_ref, u_ref, gu_ref, act_ref):
        uv = _norm_modulate(x_ref, g_ref, sh_ref, sc_ref)
        u_ref[...] = uv
        for j in range(4):
            g = _dot(uv, w_ref[j], NT)
            up = _dot(uv, w_ref[j + 4], NT)
            gu_ref[0, j] = g.astype(BF16)
            gu_ref[1, j] = up.astype(BF16)
            act_ref[j] = (g * _sigmoid(g) * up).astype(BF16)

    return _call(
        body, name=name, grid=(s // t,),
        in_specs=[_row_spec(t), _vec_spec(), _vec_spec(), _vec_spec(), _resident(w_gu_t.shape)],
        out_specs=[_row_spec(t), pl.BlockSpec((2, 4, t, FF_BLK), lambda i: (0, 0, i, 0)),
                   pl.BlockSpec((4, t, FF_BLK), lambda i: (0, i, 0))],
        out_shape=[jax.ShapeDtypeStruct((s, D_MODEL), BF16), jax.ShapeDtypeStruct((2, 4, s, FF_BLK), BF16),
                   jax.ShapeDtypeStruct((4, s, FF_BLK), BF16)],
        operands=(x, norm_g, shift, scale, w_gu_t), semantics=("parallel",), exchange=exchange)


def residual_matmul(a, b, x, gate, coef, name, exchange=None):
    nk, s, kb = a.shape
    t = min(ROW_TILE, s)

    def body(a_ref, b_ref, x_ref, gt_ref, xo_ref, y_ref):
        y = _dot(a_ref[0], b_ref[0])
        for k in range(1, nk):
            y = y + _dot(a_ref[k], b_ref[k])
        y_ref[...] = y.astype(BF16)
        xo_ref[...] = x_ref[...] + coef * gt_ref[...] * y

    return _call(
        body, name=name, grid=(s // t,),
        in_specs=[pl.BlockSpec((nk, t, kb), lambda i: (0, i, 0)),
                  pl.BlockSpec((nk, kb, D_MODEL), lambda i: (0, 0, 0)),
                  _row_spec(t), _vec_spec()],
        out_specs=[_row_spec(t), _row_spec(t)],
        out_shape=[jax.ShapeDtypeStruct((s, D_MODEL), F32), jax.ShapeDtypeStruct((s, D_MODEL), BF16)],
        operands=(a, b, x, gate), semantics=("parallel",), exchange=exchange)


def residual_matmul_loss(a, b, x, gate, coef, target, final_g, name):
    nk, s, kb = a.shape
    t = min(ROW_TILE, s)

    def body(a_ref, b_ref, x_ref, gt_ref, t_ref, fg_ref, dx_ref, dy_ref, dgt_ref, dfg_ref, sq_ref):
        @pl.when(pl.program_id(0) == 0)
        def _():
            dgt_ref[...] = jnp.zeros_like(dgt_ref)
            dfg_ref[...] = jnp.zeros_like(dfg_ref)
            sq_ref[...] = jnp.zeros_like(sq_ref)

        y = _dot(a_ref[0], b_ref[0])
        for k in range(1, nk):
            y = y + _dot(a_ref[k], b_ref[k])
        gt = gt_ref[...]
        fg = fg_ref[...]
        xv = x_ref[...] + coef * gt * y
        r = lax.rsqrt(jnp.mean(xv * xv, axis=-1, keepdims=True) + EPS)
        nrm = xv * r
        err = nrm * fg - t_ref[...]
        sq_ref[...] += jnp.sum(err * err, axis=0, keepdims=True)
        dout = err * (1.0 / D_MODEL)
        dfg_ref[...] += jnp.sum(dout * nrm, axis=0, keepdims=True)
        dn = dout * fg
        dx = r * (dn - nrm * jnp.mean(dn * nrm, axis=-1, keepdims=True))
        dx_ref[...] = dx
        dy_ref[...] = (coef * gt * dx).astype(BF16)
        dgt_ref[...] += coef * jnp.sum(dx * y, axis=0, keepdims=True)

    vec = jax.ShapeDtypeStruct((1, D_MODEL), F32)
    return pl.pallas_call(
        body, name=name, grid=(s // t,),
        in_specs=[pl.BlockSpec((nk, t, kb), lambda i: (0, i, 0)), _resident(b.shape),
                  _row_spec(t), _vec_spec(), _row_spec(t), _vec_spec()],
        out_specs=[_row_spec(t), _row_spec(t), _vec_spec(), _vec_spec(), _vec_spec()],
        out_shape=[jax.ShapeDtypeStruct((s, D_MODEL), F32), jax.ShapeDtypeStruct((s, D_MODEL), BF16), vec, vec, vec],
        compiler_params=_params("arbitrary"),
    )(a, b, x, gate, target, final_g)


def ffn_tokens_bwd(dy, w_down, gu, w_gu_t, name, exchange=None):
    s = dy.shape[0]
    t = min(ROW_TILE, s)
    with_du = w_gu_t is not None

    def body(*refs):
        if with_du:
            dy_ref, wd_ref, gu_ref, wgu_ref, dgu_ref, du_ref = refs
        else:
            dy_ref, wd_ref, gu_ref, dgu_ref = refs
        dyv = dy_ref[...]
        du = None
        for j in range(4):
            dact = _dot(dyv, wd_ref[j], NT)
            g = gu_ref[0, j].astype(F32)
            up = gu_ref[1, j].astype(F32)
            sg = _sigmoid(g)
            slopes = (up * sg * (1.0 + g * (1.0 - sg)), g * sg)
            for half in range(2):
                d = (dact * slopes[half]).astype(BF16)
                dgu_ref[half, j] = d
                if with_du:
                    part = _dot(d, wgu_ref[4 * half + j])
                    du = part if du is None else du + part
        if with_du:
            du_ref[...] = du

    blocks = pl.BlockSpec((2, 4, t, FF_BLK), lambda i: (0, 0, i, 0))
    dgu_shape = jax.ShapeDtypeStruct((2, 4, s, FF_BLK), BF16)
    if with_du:
        return _call(
            body, name=name, grid=(s // t,),
            in_specs=[_row_spec(t), _resident(w_down.shape), blocks, _resident(w_gu_t.shape)],
            out_specs=[blocks, _row_spec(t)],
            out_shape=[dgu_shape, jax.ShapeDtypeStruct((s, D_MODEL), F32)],
            operands=(dy, w_down, gu, w_gu_t), semantics=("parallel",), exchange=exchange)
    return _call(
        body, name=name, grid=(s // t,),
        in_specs=[_row_spec(t), _resident(w_down.shape), blocks], out_specs=blocks, out_shape=dgu_shape,
        operands=(dy, w_down, gu), semantics=("parallel",), exchange=exchange)


def matmul_nt_acc(a, b, name, b_dims=NT, exchange=None):
    nk = b.shape[0]
    d, n = (b.shape[1], b.shape[2]) if b_dims == NT else (b.shape[2], b.shape[1])
    s = a.shape[-2]
    t = min(ROW_TILE, s)
    by_columns = a.ndim == 2

    def body(a_ref, b_ref, o_ref):
        def a_blk(k):
            return a_ref[:, k * n:(k + 1) * n] if by_columns else a_ref[k]

        acc = _dot(a_blk(0), b_ref[0], b_dims)
        for k in range(1, nk):
            acc = acc + _dot(a_blk(k), b_ref[k], b_dims)
        o_ref[...] = acc

    a_spec = _row_spec(t, nk * n) if by_columns else pl.BlockSpec((nk, t, n), lambda i: (0, i, 0))
    return _call(
        body, name=name, grid=(s // t,),
        in_specs=[a_spec, pl.BlockSpec(b.shape, lambda i: (0, 0, 0))],
        out_specs=pl.BlockSpec((t, d), lambda i: (i, 0)),
        out_shape=jax.ShapeDtypeStruct((s, d), F32),
        operands=(a, b), semantics=("parallel",), exchange=exchange)


def matmul_tn(a, b, name, group=(1, 1), b_cols=None, exchange=None):
    ja, s, m = a.shape
    by_columns = b.ndim == 2
    jb, n = (b.shape[1] // b_cols, b_cols) if by_columns else (b.shape[0], b.shape[2])
    ga, gb = group
    t = min(ACC_TILE, s)
    nk = s // t

    def body(a_ref, b_ref, o_ref, acc_ref):
        k = pl.program_id(2)

        @pl.when(k == 0)
        def _():
            acc_ref[...] = jnp.zeros_like(acc_ref)

        for p in range(ga):
            for q in range(gb):
                b_blk = b_ref[:, q * n:(q + 1) * n] if by_columns else b_ref[q]
                acc_ref[p, q] += _dot(a_ref[p], b_blk, TN)

        @pl.when(k == nk - 1)
        def _():
            o_ref[...] = acc_ref[...].astype(BF16)

    return _call(
        body, name=name, grid=(ja // ga, jb // gb, nk),
        in_specs=[pl.BlockSpec((ga, t, m), lambda p, q, k: (p, k, 0)),
                  pl.BlockSpec((t, gb * n), lambda p, q, k: (k, q)) if by_columns
                  else pl.BlockSpec((gb, t, n), lambda p, q, k: (q, k, 0))],
        out_specs=pl.BlockSpec((ga, gb, m, n), lambda p, q, k: (p, q, 0, 0)),
        out_shape=jax.ShapeDtypeStruct((ja, jb, m, n), BF16),
        operands=(a, b), scratch_shapes=[pltpu.VMEM((ga, gb, m, n), F32)],
        semantics=("parallel", "parallel", "arbitrary"), exchange=exchange)


def mix_in_proj(x, norm_g, shift, scale, w_mix, name, exchange=None):
    s = x.shape[0]
    t = min(ROW_TILE, s)

    def body(x_ref, g_ref, sh_ref, sc_ref, w_ref, u_ref, o_ref):
        uv = _norm_modulate(x_ref, g_ref, sh_ref, sc_ref)
        u_ref[...] = uv
        for j in range(N_DEV):
            o_ref[:, j * MIX_BLK:(j + 1) * MIX_BLK] = _dot(uv, w_ref[j]).astype(BF16)

    return _call(
        body, name=name, grid=(s // t,),
        in_specs=[_row_spec(t), _vec_spec(), _vec_spec(), _vec_spec(), _resident(w_mix.shape)],
        out_specs=[_row_spec(t), _row_spec(t, MIX_W)],
        out_shape=[jax.ShapeDtypeStruct((s, D_MODEL), BF16), jax.ShapeDtypeStruct((s, MIX_W), BF16)],
        operands=(x, norm_g, shift, scale, w_mix), semantics=("parallel",), exchange=exchange)


def _conv_taps(cc_ref, cx_ref, s):
    v = cc_ref[...].astype(F32) * cx_ref[...].astype(F32)
    tok = lax.broadcasted_iota(jnp.int32, v.shape, 0)
    v1 = jnp.where(tok >= 1, pltpu.roll(v, 1, 0), 0.0)
    v2 = jnp.where(tok >= 2, pltpu.roll(v, 2, 0), 0.0)
    return v, v1, v2, tok


def _proj_cols(s, first):
    return pl.BlockSpec((s, 128), lambda j: (0, first + j))


def short_conv(proj, conv_w, name):
    s = proj.shape[0]

    def body(cb_ref, cc_ref, cx_ref, w_ref, o_ref):
        v, v1, v2, _ = _conv_taps(cc_ref, cx_ref, s)
        y = w_ref[0:1, :] * v2 + w_ref[1:2, :] * v1 + w_ref[2:3, :] * v
        o_ref[...] = (cb_ref[...].astype(F32) * y).astype(BF16)

    return pl.pallas_call(
        body, name=name, grid=(CONV_W // 128,),
        in_specs=[_proj_cols(s, 0), _proj_cols(s, 4), _proj_cols(s, 8),
                  pl.BlockSpec((3, 128), lambda j: (0, j))],
        out_specs=pl.BlockSpec((s, 128), lambda j: (0, j)),
        out_shape=jax.ShapeDtypeStruct((s, CONV_W), BF16),
        compiler_params=_params("parallel"),
    )(proj, proj, proj, conv_w)


def short_conv_bwd(dsa, proj, conv_w, name):
    s = proj.shape[0]

    def body(dsa_ref, cb_ref, cc_ref, cx_ref, w_ref, dcb_ref, dcc_ref, dcx_ref, dw_ref):
        v, v1, v2, tok = _conv_taps(cc_ref, cx_ref, s)
        w0, w1, w2 = w_ref[0:1, :], w_ref[1:2, :], w_ref[2:3, :]
        y = w0 * v2 + w1 * v1 + w2 * v
        dsa_v = dsa_ref[...].astype(F32)
        dcb_ref[...] = (dsa_v * y).astype(BF16)
        dy = dsa_v * cb_ref[...].astype(F32)
        dw_ref[0:1, :] = jnp.sum(dy * v2, axis=0, keepdims=True)
        dw_ref[1:2, :] = jnp.sum(dy * v1, axis=0, keepdims=True)
        dw_ref[2:3, :] = jnp.sum(dy * v, axis=0, keepdims=True)
        dy1 = jnp.where(tok < s - 1, pltpu.roll(dy, s - 1, 0), 0.0)
        dy2 = jnp.where(tok < s - 2, pltpu.roll(dy, s - 2, 0), 0.0)
        dv = w2 * dy + w1 * dy1 + w0 * dy2
        dcc_ref[...] = (dv * cx_ref[...].astype(F32)).astype(BF16)
        dcx_ref[...] = (dv * cc_ref[...].astype(F32)).astype(BF16)

    col = pl.BlockSpec((s, 128), lambda j: (0, j))
    act = jax.ShapeDtypeStruct((s, CONV_W), BF16)
    return pl.pallas_call(
        body, name=name, grid=(CONV_W // 128,),
        in_specs=[col, _proj_cols(s, 0), _proj_cols(s, 4), _proj_cols(s, 8),
                  pl.BlockSpec((3, 128), lambda j: (0, j))],
        out_specs=[col, col, col, pl.BlockSpec((3, 128), lambda j: (0, j))],
        out_shape=[act, act, act, jax.ShapeDtypeStruct((3, CONV_W), F32)],
        compiler_params=_params("parallel"),
    )(dsa, proj, proj, proj, conv_w)


def _gate_specs(t):
    return [pl.BlockSpec((t, D_MODEL), lambda i: (i, 3)), pl.BlockSpec((t, D_MODEL), lambda i: (i, 4))]


def merge_forward(sa, o, proj, w_co, w_ao, b_merge, w_out, x, gate, name, exchange=None):
    s = sa.shape[0]
    t = min(ROW_TILE, s)

    def body(sa_ref, o_ref, ga_ref, gb_ref, wco_ref, wao_ref, bm_ref, wout_ref, x_ref, gt_ref,
             mg_ref, ya_ref, yb_ref, y_ref, xo_ref):
        ya = _dot(sa_ref[...], wco_ref[...])
        yb = _dot(o_ref[...], wao_ref[...])
        sga = _sigmoid(ga_ref[...].astype(F32) + bm_ref[0:1, :])
        sgb = _sigmoid(gb_ref[...].astype(F32) + bm_ref[1:2, :])
        merged = (sga * ya + sgb * yb).astype(BF16)
        mg_ref[...] = merged
        ya_ref[...] = ya.astype(BF16)
        yb_ref[...] = yb.astype(BF16)
        y = _dot(merged, wout_ref[...])
        y_ref[...] = y.astype(BF16)
        xo_ref[...] = x_ref[...] + gt_ref[...] * y

    act = jax.ShapeDtypeStruct((s, D_MODEL), BF16)
    return _call(
        body, name=name, grid=(s // t,),
        in_specs=[_row_spec(t, CONV_W), _row_spec(t, ATTN_W)] + _gate_specs(t)
        + [_vec_spec(CONV_W), _vec_spec(ATTN_W), _vec_spec(2), _vec_spec(D_MODEL), _row_spec(t), _vec_spec()],
        out_specs=[_row_spec(t)] * 5, out_shape=[act, act, act, act, jax.ShapeDtypeStruct((s, D_MODEL), F32)],
        operands=(sa, o, proj, proj, w_co, w_ao, b_merge, w_out, x, gate), semantics=("parallel",),
        exchange=exchange)


def merge_backward(dy, w_out, proj, ya, yb, b_merge, name, exchange=None):
    s = dy.shape[0]
    t = min(ROW_TILE, s)

    def body(dy_ref, w_ref, ga_ref, gb_ref, ya_ref, yb_ref, bm_ref,
             dya_ref, dyb_ref, dga_ref, dgb_ref, dbm_ref):
        @pl.when(pl.program_id(0) == 0)
        def _():
            dbm_ref[...] = jnp.zeros_like(dbm_ref)

        dmg = _dot(dy_ref[...], w_ref[...], NT)
        sga = _sigmoid(ga_ref[...].astype(F32) + bm_ref[0:1, :])
        sgb = _sigmoid(gb_ref[...].astype(F32) + bm_ref[1:2, :])
        dya_ref[...] = (dmg * sga).astype(BF16)
        dyb_ref[...] = (dmg * sgb).astype(BF16)
        dga = dmg * ya_ref[...].astype(F32) * sga * (1.0 - sga)
        dgb = dmg * yb_ref[...].astype(F32) * sgb * (1.0 - sgb)
        dga_ref[...] = dga.astype(BF16)
        dgb_ref[...] = dgb.astype(BF16)
        dbm_ref[0:1, :] += jnp.sum(dga, axis=0, keepdims=True)
        dbm_ref[1:2, :] += jnp.sum(dgb, axis=0, keepdims=True)

    act = jax.ShapeDtypeStruct((s, D_MODEL), BF16)
    return _call(
        body, name=name, grid=(s // t,),
        in_specs=[_row_spec(t), _vec_spec(D_MODEL)] + _gate_specs(t)
        + [_row_spec(t), _row_spec(t), _vec_spec(2)],
        out_specs=[_row_spec(t)] * 4 + [_vec_spec(2)],
        out_shape=[act] * 4 + [jax.ShapeDtypeStruct((2, D_MODEL), F32)],
        operands=(dy, w_out, proj, proj, ya, yb, b_merge), semantics=("arbitrary",), exchange=exchange)


def out_proj_bwd(dya, dyb, w_co, w_ao, name):
    s = dya.shape[0]
    t = min(ROW_TILE, s)

    def body(dya_ref, dyb_ref, wco_ref, wao_ref, dsa_ref, do_ref):
        dsa_ref[...] = _dot(dya_ref[...], wco_ref[...], NT).astype(BF16)
        do_ref[...] = _dot(dyb_ref[...], wao_ref[...], NT).astype(BF16)

    return pl.pallas_call(
        body, name=name, grid=(s // t,),
        in_specs=[_row_spec(t), _row_spec(t), _vec_spec(CONV_W), _vec_spec(ATTN_W)],
        out_specs=[_row_spec(t, CONV_W), _row_spec(t, ATTN_W)],
        out_shape=[jax.ShapeDtypeStruct((s, CONV_W), BF16), jax.ShapeDtypeStruct((s, ATTN_W), BF16)],
        compiler_params=_params("parallel"),
    )(dya, dyb, w_co, w_ao)


ATT_HEADS = 4
ATT_LANES = ATT_HEADS * HEAD_DIM
ATT_UNDERFLOW = 110.0


def _softplus(z):
    return jnp.maximum(z, 0.0) + jnp.log(1.0 + jnp.exp(-jnp.abs(z)))


def _head_masks(rows):
    lane = lax.broadcasted_iota(jnp.int32, (rows, ATT_LANES), 1)
    return [(lane >= h * HEAD_DIM) & (lane < (h + 1) * HEAD_DIM) for h in range(ATT_HEADS)]


def _per_head(x, masks):
    return [jnp.where(m, x, jnp.zeros_like(x)) for m in masks]


def _att_specs(s, blk):
    first = {"q": 3 * CONV_W // ATT_LANES, "k": (3 * CONV_W + ATTN_W) // ATT_LANES,
             "v": (3 * CONV_W + 2 * ATTN_W) // ATT_LANES}
    return [pl.BlockSpec((blk, ATT_LANES), lambda h, i: (i, first["q"] + h)),
            pl.BlockSpec((s, ATT_LANES), lambda h, i: (0, first["k"] + h)),
            pl.BlockSpec((s, ATT_LANES), lambda h, i: (0, first["v"] + h))]


def _head_norms(x, masks):
    sq = jnp.square(x.astype(F32))
    return [jnp.sum(jnp.where(m, sq, 0.0), axis=1, keepdims=True) for m in masks]


def stick_breaking_fwd(proj, name, exchange=None):
    s = proj.shape[0]
    blk = ATT_BLK
    nq = s // blk

    def body(q_ref, k_ref, v_ref, o_ref, tot_ref, first_ref, kmax_ref):
        i = pl.program_id(1)
        row = lax.broadcasted_iota(jnp.int32, (blk, blk), 0)
        col = lax.broadcasted_iota(jnp.int32, (blk, blk), 1)
        tri = (row >= col).astype(BF16)
        causal = col < row
        masks = _head_masks(blk)
        q_all = q_ref[...] * ATTN_SCALE
        qs = _per_head(q_all, masks)

        @pl.when(i == 0)
        def _():
            def longest(n, best):
                norms = _head_norms(k_ref[pl.ds(pl.multiple_of(n * blk, blk), blk), :], masks)
                return tuple(jnp.maximum(b, v) for b, v in zip(best, norms))

            best = lax.fori_loop(0, nq, longest, tuple(jnp.zeros((blk, 1), F32) for _ in range(ATT_HEADS)))
            for h in range(ATT_HEADS):
                kmax_ref[h] = jnp.sqrt(jnp.max(best[h], axis=0, keepdims=True))

        needed = [jnp.sqrt(n) * kmax_ref[h] + ATT_UNDERFLOW for h, n in enumerate(_head_norms(q_all, masks))]

        def finished(laters):
            slack = laters[0] - needed[0]
            for h in range(1, ATT_HEADS):
                slack = jnp.minimum(slack, laters[h] - needed[h])
            return (jnp.min(slack) >= 0.0).astype(jnp.int32)

        def step(j, carry, diagonal):
            laters, acc = carry
            rows = pl.ds(pl.multiple_of(j * blk, blk), blk)
            kb = k_ref[rows, :]
            probs, new_laters = [], []
            for h in range(ATT_HEADS):
                z = _dot(qs[h], kb, NT)
                sp = _softplus(z)
                if diagonal:
                    sp = jnp.where(causal, sp, 0.0)
                a = jnp.exp(z - (_dot(sp.astype(BF16), tri) + laters[h]))
                if diagonal:
                    a = jnp.where(causal, a, 0.0)
                probs.append(a.astype(BF16))
                new_laters.append(laters[h] + jnp.sum(sp, axis=1, keepdims=True))
            v_heads = jnp.concatenate(_per_head(v_ref[rows, :], masks), axis=0)
            acc = acc + _dot(jnp.concatenate(probs, axis=1), v_heads)
            return tuple(new_laters), acc

        carry = (tuple(jnp.zeros((blk, 1), F32) for _ in range(ATT_HEADS)), jnp.zeros((blk, ATT_LANES), F32))
        laters, acc = step(i, carry, True)

        def further(state):
            n, _, laters, acc = state
            laters, acc = step(i - 1 - n, (laters, acc), False)
            return n + 1, finished(laters), laters, acc

        walked, _, laters, acc = lax.while_loop(
            lambda state: jnp.logical_and(state[0] < i, state[1] == 0), further,
            (jnp.int32(0), finished(laters), laters, acc))
        o_ref[...] = acc.astype(BF16)
        tot = jnp.zeros((blk, ATT_LANES), F32)
        for h in range(ATT_HEADS):
            tot = jnp.where(masks[h], laters[h], tot)
        tot_ref[...] = tot
        first_ref[...] = jnp.full(first_ref.shape, i - walked, jnp.int32).astype(F32)

    out_spec = pl.BlockSpec((blk, ATT_LANES), lambda h, i: (i, h))
    groups = N_HEADS // ATT_HEADS
    return _call(
        body, name=name, grid=(groups, nq),
        in_specs=_att_specs(s, blk),
        out_specs=[out_spec, out_spec, pl.BlockSpec((None, None, 8, 128), lambda h, i: (h, i, 0, 0))],
        out_shape=[jax.ShapeDtypeStruct((s, ATTN_W), BF16), jax.ShapeDtypeStruct((s, ATTN_W), F32),
                   jax.ShapeDtypeStruct((groups, nq, 8, 128), F32)],
        operands=(proj, proj, proj), scratch_shapes=[pltpu.VMEM((ATT_HEADS, 1, 1), F32)],
        semantics=("parallel", "arbitrary"), exchange=exchange)


def stick_breaking_bwd(proj, do, tot, first, name, exchange=None):
    s = proj.shape[0]
    blk = ATT_BLK
    nq = s // blk

    def body(q_ref, k_ref, v_ref, do_ref, tot_ref, first_ref, dq_ref, dk_ref, dv_ref):
        i = pl.program_id(1)
        start = jnp.clip(jnp.max(first_ref[...]).astype(jnp.int32), 0, i)

        @pl.when(i == 0)
        def _():
            dk_ref[...] = jnp.zeros_like(dk_ref)
            dv_ref[...] = jnp.zeros_like(dv_ref)

        row = lax.broadcasted_iota(jnp.int32, (blk, blk), 0)
        col = lax.broadcasted_iota(jnp.int32, (blk, blk), 1)
        before = (row < col).astype(BF16)
        upto = (row <= col).astype(BF16)
        causal = col < row
        masks = _head_masks(blk)
        qs = _per_head(q_ref[...] * ATTN_SCALE, masks)
        dos = _per_head(do_ref[...], masks)
        q_heads = jnp.concatenate(qs, axis=0)
        do_heads = jnp.concatenate(dos, axis=0)
        tot_all = tot_ref[...]
        totals = [jnp.max(jnp.where(m, tot_all, 0.0), axis=1, keepdims=True) for m in masks]

        def step(j, carry, diagonal):
            earliers, g_sums, dq = carry
            rows = pl.ds(pl.multiple_of(j * blk, blk), blk)
            kb = k_ref[rows, :]
            vb = v_ref[rows, :]
            probs, dzs, new_earliers, new_g_sums = [], [], [], []
            for h in range(ATT_HEADS):
                z = _dot(qs[h], kb, NT)
                sp = _softplus(z)
                if diagonal:
                    sp = jnp.where(causal, sp, 0.0)
                c = (totals[h] - earliers[h]) - _dot(sp.astype(BF16), before)
                a = jnp.exp(z - c)
                if diagonal:
                    a = jnp.where(causal, a, 0.0)
                g = a * _dot(dos[h], vb, NT)
                f = g_sums[h] + _dot(g.astype(BF16), upto)
                dz = g - jnp.exp(z - sp) * f
                if diagonal:
                    dz = jnp.where(causal, dz, 0.0)
                probs.append(a.astype(BF16))
                dzs.append(dz.astype(BF16))
                new_earliers.append(earliers[h] + jnp.sum(sp, axis=1, keepdims=True))
                new_g_sums.append(g_sums[h] + jnp.sum(g, axis=1, keepdims=True))
            k_heads = jnp.concatenate(_per_head(kb, masks), axis=0)
            dq = dq + _dot(jnp.concatenate(dzs, axis=1), k_heads)
            dk_ref[rows, :] += _dot(jnp.concatenate(dzs, axis=0), q_heads, TN)
            dv_ref[rows, :] += _dot(jnp.concatenate(probs, axis=0), do_heads, TN)
            return tuple(new_earliers), tuple(new_g_sums), dq

        zeros = tuple(jnp.zeros((blk, 1), F32) for _ in range(ATT_HEADS))
        carry = (zeros, zeros, jnp.zeros((blk, ATT_LANES), F32))
        carry = lax.fori_loop(start, i, lambda j, c: step(j, c, False), carry)
        dq = step(i, carry, True)[2]
        dq_ref[...] = (dq * ATTN_SCALE).astype(BF16)

    blk_spec = pl.BlockSpec((blk, ATT_LANES), lambda h, i: (i, h))
    full_spec = pl.BlockSpec((s, ATT_LANES), lambda h, i: (0, h))
    return _call(
        body, name=name, grid=(N_HEADS // ATT_HEADS, nq),
        in_specs=_att_specs(s, blk) + [blk_spec, blk_spec,
                                       pl.BlockSpec((None, None, 8, 128), lambda h, i: (h, i, 0, 0))],
        out_specs=[blk_spec, full_spec, full_spec],
        out_shape=[jax.ShapeDtypeStruct((s, ATTN_W), BF16), jax.ShapeDtypeStruct((s, ATTN_W), F32),
                   jax.ShapeDtypeStruct((s, ATTN_W), F32)],
        operands=(proj, proj, proj, do, tot, first), semantics=("parallel", "arbitrary"), exchange=exchange)


def adamw(w, m, v, parts, name):
    r, c = w.shape
    p = parts.shape[0]
    t = r
    for cand in (256, 176):
        if r % cand == 0 and r > cand:
            t = cand
            break

    def body(w_ref, m_ref, v_ref, p_ref, g_ref, d_ref, mo_ref, vo_ref):
        g = p_ref[0].astype(F32)
        for n in range(1, p):
            g = g + p_ref[n].astype(F32)
        m_new = ADAM_B1 * m_ref[...] + (1.0 - ADAM_B1) * g
        v_new = ADAM_B2 * v_ref[...] + (1.0 - ADAM_B2) * (g * g)
        m_hat = m_new / ADAM_BC1
        v_hat = v_new / ADAM_BC2
        g_ref[...] = g
        d_ref[...] = -ADAM_LR * (m_hat / (jnp.sqrt(v_hat) + ADAM_EPS) + ADAM_WD * w_ref[...])
        mo_ref[...] = m_new
        vo_ref[...] = v_new

    spec = pl.BlockSpec((t, c), lambda i: (i, 0))
    out = jax.ShapeDtypeStruct((r, c), F32)
    return pl.pallas_call(
        body, name=name, grid=(r // t,),
        in_specs=[spec, spec, spec, pl.BlockSpec((p, t, c), lambda i: (0, i, 0))],
        out_specs=[spec] * 4, out_shape=[out] * 4,
        compiler_params=_params("parallel"),
    )(w, m, v, parts)


def kernel(x, c, w_ada, b_ada, norm1_g, ffn1_w_gu, ffn1_w_down, norm2_g, w_mix_in, b_merge, conv_w, w_conv_out, w_attn_out, w_out, norm3_g, ffn2_w_gu, ffn2_w_down, final_g, loss_target, m_w_ada, m_b_ada, m_norm1_g, m_ffn1_w_gu, m_ffn1_w_down, m_norm2_g, m_w_mix_in, m_b_merge, m_conv_w, m_w_conv_out, m_w_attn_out, m_w_out, m_norm3_g, m_ffn2_w_gu, m_ffn2_w_down, m_final_g, v_w_ada, v_b_ada, v_norm1_g, v_ffn1_w_gu, v_ffn1_w_down, v_norm2_g, v_w_mix_in, v_b_merge, v_conv_w, v_w_conv_out, v_w_attn_out, v_w_out, v_norm3_g, v_ffn2_w_gu, v_ffn2_w_down, v_final_g):
    s = x.shape[1]
    me = 4 * lax.axis_index("x") + 2 * lax.axis_index("y") + lax.axis_index("c")
    x0 = x[0]
    target = loss_target[0]
    final_g2 = final_g.reshape(1, D_MODEL)

    def shard(w):
        return w[0].astype(BF16)

    def flipped(w):
        return jnp.swapaxes(w, 1, 2)

    def rows8(g):
        return g.reshape(N_DEV, -1, D_MODEL)

    small_in = jnp.concatenate([c.reshape(-1), b_merge.reshape(-1), conv_w.reshape(-1),
                                jnp.zeros((64,), F32)]).reshape(1, -1)
    n_ada = w_ada.shape[2]
    b_cols = lax.dynamic_slice(b_ada, (0, me * n_ada), (1, n_ada))
    small_all, mod_all, (wgu1, wd1) = prologue(small_in, w_ada[0], b_cols,
                                               [shard(flipped(ffn1_w_gu)), shard(ffn1_w_down)], "prologue")
    wd1 = wd1.reshape(4, FF_BLK, D_MODEL)
    small_all = small_all[:, 0, :]
    c_all = small_all[:, :D_MODEL]
    bm_full = small_all[:, 1024:1280].reshape(8, 2, 128).transpose(1, 0, 2).reshape(2, D_MODEL)
    cw_full = small_all[:, 1280:1472].reshape(8, 3, 64).transpose(1, 0, 2).reshape(3, CONV_W)
    mod = lax.dynamic_index_in_dim(mod_all, me, axis=1, keepdims=False).reshape(9, 1, D_MODEL)
    sh1, sc1, gt1, sh2, sc2, gt2, sh3, sc3, gt3 = [mod[n] for n in range(9)]

    (u1, gu1, act1), got = ffn_up(x0, norm1_g, sh1, sc1, wgu1, "ffn_up_1",
                                  exchange=gather_stage1([shard(w_mix_in)]))
    (x1, y1), (wmix, *got) = residual_matmul(
        act1, wd1, x0, gt1, 0.5, "ffn_down_1", exchange=merge_exchanges(
            gather_stage2(got), gather_stage1([shard(w_conv_out), shard(w_attn_out), shard(w_out)])))

    (u2, proj), (wco, wao, wout) = mix_in_proj(x1, norm2_g, sh2, sc2, wmix, "mix_in",
                                               exchange=gather_stage2(got))
    wco = wco.transpose(1, 0, 2).reshape(CONV_W, D_MODEL)
    wao = wao.transpose(1, 0, 2).reshape(ATTN_W, D_MODEL)
    wout = wout.reshape(D_MODEL, D_MODEL)
    sa = short_conv(proj, cw_full, "short_conv")
    (o, tot, first), got = stick_breaking_fwd(proj, "attn_fwd",
                                       exchange=gather_stage1([shard(flipped(ffn2_w_gu)), shard(ffn2_w_down)]))
    (merged, ya, yb, y2, x2), (wgu3, wd3) = merge_forward(sa, o, proj, wco, wao, bm_full, wout, x1, gt2, "merge",
                                                          exchange=gather_stage2(got))
    wd3 = wd3.reshape(4, FF_BLK, D_MODEL)

    u3, gu3, act3 = ffn_up(x2, norm3_g, sh3, sc3, wgu3, "ffn_up_3")

    dx3, dy3, dgt3, dfinal, sq = residual_matmul_loss(act3, wd3, x2, gt3, 0.5, target, final_g2, "ffn_down_3_loss")
    dgu3, du3 = ffn_tokens_bwd(dy3, wd3, gu3, wgu3, "ffn_bwd_3")
    dgu3 = dgu3.reshape(8, s, FF_BLK)
    g_wd3 = rows8(matmul_tn(act3, dy3[None], "grad_w_down_3", group=(4, 1)))
    g_wgu3 = matmul_tn(dgu3, u3[None], "grad_w_gu_3", group=(4, 1)).reshape(8, FF_BLK, D_MODEL)
    dx2, dsh3, dsc3, dn3, dy2, dgt2 = norm_modulate_bwd(du3, x2, dx3, norm3_g, sc3, "norm_bwd_3",
                                                        prev=(gt2, y2, 1.0))

    (dya, dyb, dga, dgb, dbm), pairs = merge_backward(dy2, wout, proj, ya, yb, bm_full, "merge_bwd",
                                                      exchange=scatter_stage1([g_wgu3, g_wd3]))
    sums3 = [pair_sum(g_wgu3, pairs[0], "pair_sum_w_gu_3"), pair_sum(g_wd3, pairs[1], "pair_sum_w_down_3")]
    g_wout = rows8(matmul_tn(merged[None], dy2[None], "grad_w_out"))
    dsa, do = out_proj_bwd(dya, dyb, wco, wao, "out_proj_bwd")
    g_wco = matmul_tn(sa[None], dya[None], "grad_w_conv_out").reshape(CONV_W, N_DEV, 128).transpose(1, 0, 2)
    g_wao = matmul_tn(o[None], dyb[None], "grad_w_attn_out").reshape(ATTN_W, N_DEV, 128).transpose(1, 0, 2)
    dcb, dcc, dcx, dconv = short_conv_bwd(dsa, proj, cw_full, "short_conv_bwd")
    (dq, dk, dv), landed3 = stick_breaking_bwd(proj, do, tot, first, "attn_bwd", exchange=scatter_stage2(sums3))
    dproj = jnp.concatenate([dcb, dcc, dcx, dq, dk.astype(BF16), dv.astype(BF16), dga, dgb], axis=1)
    g_wmix = matmul_tn(u2[None], dproj, "grad_w_mix_in", group=(1, 4), b_cols=MIX_BLK).reshape(
        N_DEV, D_MODEL, MIX_BLK)
    mixer_grads = [g_wmix, g_wco, g_wao, g_wout]
    (dx1, dsh2, dsc2, dn2, dy1, dgt1), pairs = norm_modulate_bwd(
        (dproj, wmix), x1, dx2, norm2_g, sc2, "norm_bwd_2", prev=(gt1, y1, 0.5),
        exchange=scatter_stage1(mixer_grads))
    sums_mix = [pair_sum(g, p, f"pair_sum_mixer_{n}") for n, (g, p) in enumerate(zip(mixer_grads, pairs))]

    dgu1, landed_mix = ffn_tokens_bwd(dy1, wd1, gu1, None, "ffn_dact_1", exchange=scatter_stage2(sums_mix[:1]))
    dgu1 = dgu1.reshape(8, s, FF_BLK)
    g_wgu1, landed_small = matmul_tn(dgu1, u1[None], "grad_w_gu_1", group=(4, 1),
                                     exchange=scatter_stage2(sums_mix[1:]))
    g_wgu1 = g_wgu1.reshape(8, FF_BLK, D_MODEL)
    g_wd1, pairs = matmul_tn(act1, dy1[None], "grad_w_down_1", group=(4, 1), exchange=scatter_stage1([g_wgu1]))
    g_wd1 = rows8(g_wd1)
    sum_gu1 = pair_sum(g_wgu1, pairs[0], "pair_sum_w_gu_1")
    du1, (landed_gu1, pair_d1) = matmul_nt_acc(
        dgu1, wgu1, "ffn_du_1", b_dims=NN, exchange=merge_exchanges(scatter_stage2([sum_gu1]), scatter_stage1([g_wd1])))
    sum_d1 = pair_sum(g_wd1, pair_d1, "pair_sum_w_down_1")
    (grad_x, dsh1, dsc1, dn1), landed_d1 = norm_modulate_bwd(du1, x0, dx1, norm1_g, sc1, "norm_bwd_1",
                                                            exchange=scatter_stage2([sum_d1]))

    loss_local = (0.5 / D_MODEL) * jnp.sum(sq)
    stats = jnp.concatenate(
        [v.reshape(-1) for v in (dsh1, dsc1, dgt1, dsh2, dsc2, dgt2, dsh3, dsc3, dgt3,
                                 dn1, dn2, dn3, dfinal, dbm, dconv)]
        + [jnp.broadcast_to(loss_local, (128,))]).reshape(1, -1)
    stats_all = all_gather_rows(stats, "gather_stats")
    n_mod = 9 * D_MODEL
    loss = jnp.sum(stats_all[:, 0, -1])
    dmod_all = stats_all[:, :, :n_mod]
    off = n_mod
    parts = {}
    for key in ("norm1_g", "norm2_g", "norm3_g", "final_g"):
        parts[key] = stats_all[:, :, off:off + D_MODEL]
        off += D_MODEL
    dbm_all = stats_all[:, 0, off:off + 2 * D_MODEL].reshape(N_DEV, 2, D_MODEL)
    off += 2 * D_MODEL
    dcw_all = stats_all[:, 0, off:off + 3 * CONV_W].reshape(N_DEV, 3, CONV_W)
    parts["b_merge"] = lax.dynamic_slice(dbm_all, (0, 0, me * 128), (N_DEV, 2, 128))
    parts["conv_w"] = lax.dynamic_slice(dcw_all, (0, 0, me * 64), (N_DEV, 3, 64))
    dmod_cols = lax.dynamic_slice(dmod_all[:, 0, :], (0, me * n_ada), (N_DEV, n_ada))
    parts["w_ada"] = ada_backward(c_all, dmod_cols, "ada_backward")[None]
    parts["b_ada"] = dmod_all
    parts["ffn2_w_gu"], parts["ffn2_w_down"] = landed3
    parts["w_mix_in"] = landed_mix[0]
    parts["w_conv_out"], parts["w_attn_out"], parts["w_out"] = landed_small
    parts["ffn1_w_gu"] = landed_gu1
    parts["ffn1_w_down"] = landed_d1[0]

    given = dict(w_ada=w_ada, b_ada=b_ada, norm1_g=norm1_g, ffn1_w_gu=ffn1_w_gu, ffn1_w_down=ffn1_w_down,
                 norm2_g=norm2_g, w_mix_in=w_mix_in, b_merge=b_merge, conv_w=conv_w, w_conv_out=w_conv_out,
                 w_attn_out=w_attn_out, w_out=w_out, norm3_g=norm3_g, ffn2_w_gu=ffn2_w_gu,
                 ffn2_w_down=ffn2_w_down, final_g=final_g)
    moments_m = dict(w_ada=m_w_ada, b_ada=m_b_ada, norm1_g=m_norm1_g, ffn1_w_gu=m_ffn1_w_gu,
                     ffn1_w_down=m_ffn1_w_down, norm2_g=m_norm2_g, w_mix_in=m_w_mix_in, b_merge=m_b_merge,
                     conv_w=m_conv_w, w_conv_out=m_w_conv_out, w_attn_out=m_w_attn_out, w_out=m_w_out,
                     norm3_g=m_norm3_g, ffn2_w_gu=m_ffn2_w_gu, ffn2_w_down=m_ffn2_w_down, final_g=m_final_g)
    moments_v = dict(w_ada=v_w_ada, b_ada=v_b_ada, norm1_g=v_norm1_g, ffn1_w_gu=v_ffn1_w_gu,
                     ffn1_w_down=v_ffn1_w_down, norm2_g=v_norm2_g, w_mix_in=v_w_mix_in, b_merge=v_b_merge,
                     conv_w=v_conv_w, w_conv_out=v_w_conv_out, w_attn_out=v_w_attn_out, w_out=v_w_out,
                     norm3_g=v_norm3_g, ffn2_w_gu=v_ffn2_w_gu, ffn2_w_down=v_ffn2_w_down, final_g=v_final_g)
    order = ["w_ada", "b_ada", "norm1_g", "ffn1_w_gu", "ffn1_w_down", "norm2_g", "w_mix_in", "b_merge",
             "conv_w", "w_conv_out", "w_attn_out", "w_out", "norm3_g", "ffn2_w_gu", "ffn2_w_down", "final_g"]
    grads, deltas, new_m, new_v = [], [], [], []
    for key in order:
        turn = flipped if key in ("ffn1_w_gu", "ffn2_w_gu") else (lambda a: a)
        shape = turn(given[key]).shape
        shape2 = (1, shape[0]) if len(shape) == 1 else shape[-2:]
        outs = adamw(turn(given[key]).reshape(shape2), turn(moments_m[key]).reshape(shape2),
                     turn(moments_v[key]).reshape(shape2), parts[key], f"adamw_{key}")
        for dst, val in zip((grads, deltas, new_m, new_v), outs):
            dst.append(turn(val.reshape(shape)))

    return (loss, grad_x[None], *grads, *deltas, *new_m, *new_v)
```

```python
import functools
from typing import Callable, NamedTuple

import jax
import jax.numpy as jnp
from jax import lax
from jax.experimental import pallas as pl
from jax.experimental.pallas import tpu as pltpu

F32 = jnp.float32
BF16 = jnp.bfloat16
MESH = pl.DeviceIdType.MESH
ANY = pl.BlockSpec(memory_space=pl.ANY)

N_DEV = 8
D_MODEL = 1024
D_FF = 2816
FF_BLK = D_FF // 4
N_HEADS = 8
HEAD_DIM = 64
CONV_W = 512
ATTN_W = 512
MIX_W = 3 * CONV_W + 3 * ATTN_W + 2 * D_MODEL
MIX_BLK = MIX_W // N_DEV
EPS = 1e-6
ATTN_SCALE = HEAD_DIM ** -0.5

ADAM_LR = 0.001
ADAM_B1 = 0.9
ADAM_B2 = 0.999
ADAM_EPS = 1e-08
ADAM_WD = 0.01
ADAM_STEP = 10
ADAM_BC1 = 1.0 - ADAM_B1 ** ADAM_STEP
ADAM_BC2 = 1.0 - ADAM_B2 ** ADAM_STEP

VMEM_LIMIT = 56 * 1024 * 1024
ROW_TILE = 512
ACC_TILE = 1024
ELT_TILE = 256
ATT_BLK = 256

NN = (((1,), (0,)), ((), ()))
NT = (((1,), (1,)), ((), ()))
TN = (((0,), (0,)), ((), ()))


def _dot(a, b, dims=NN):
    return lax.dot_general(a, b, dims, preferred_element_type=F32)


def _params(*sem):
    return pltpu.CompilerParams(dimension_semantics=sem, vmem_limit_bytes=VMEM_LIMIT)


def _sigmoid(x):
    return 1.0 / (1.0 + jnp.exp(-x))


def _me():
    x, y, c = lax.axis_index("x"), lax.axis_index("y"), lax.axis_index("c")
    return x, y, c, 4 * x + 2 * y + c


def _peer(k):
    x, y, c, _ = _me()
    px = 1 - x if (k >> 2) & 1 else x
    py = 1 - y if (k >> 1) & 1 else y
    pc = 1 - c if k & 1 else c
    return (px, py, pc), 4 * px + 2 * py + pc


class Exchange(NamedTuple):
    operands: tuple
    out_shapes: tuple
    aliases: dict
    n_remote: int
    n_local: int
    copies: Callable


CHIP_FLIPS = (2, 4, 6)
SIBLING = 1


def _remote(src, dst, send_sems, recv_sems, n, peer):
    return pltpu.make_async_remote_copy(src_ref=src, dst_ref=dst, send_sem=send_sems.at[n], recv_sem=recv_sems.at[n],
                                        device_id=peer, device_id_type=MESH)


def gather_stage1(shards):
    n = len(shards)
    rels = (SIBLING,) + CHIP_FLIPS

    def copies(ins, outs, send_sems, recv_sems, local_sems, rb, lb):
        _, _, _, me = _me()
        cps = []
        for w in range(n):
            cps.append(pltpu.make_async_copy(ins[w], outs[w].at[me], local_sems.at[lb + w]))
            for a, k in enumerate(rels):
                peer, _ = _peer(k)
                cps.append(_remote(ins[w], outs[w].at[me], send_sems, recv_sems, rb + len(rels) * w + a, peer))
        return cps

    shapes = tuple(jax.ShapeDtypeStruct((N_DEV,) + s.shape, s.dtype) for s in shards)
    return Exchange(tuple(shards), shapes, {}, len(rels) * n, n, copies)


def gather_stage2(fulls):
    n = len(fulls)

    def copies(ins, outs, send_sems, recv_sems, local_sems, rb, lb):
        sibling, _ = _peer(SIBLING)
        cps = []
        for w in range(n):
            for a, k in enumerate(CHIP_FLIPS):
                _, blk = _peer(k)
                cps.append(_remote(outs[w].at[blk], outs[w].at[blk], send_sems, recv_sems, rb + 3 * w + a, sibling))
        return cps

    shapes = tuple(jax.ShapeDtypeStruct(f.shape, f.dtype) for f in fulls)
    return Exchange(tuple(fulls), shapes, {w: w for w in range(n)}, 3 * n, 0, copies)


def scatter_stage1(fulls):
    n = len(fulls)

    def copies(ins, outs, send_sems, recv_sems, local_sems, rb, lb):
        _, _, c, _ = _me()
        sibling, _ = _peer(SIBLING)
        cps = []
        for w in range(n):
            for q in range(4):
                cps.append(_remote(ins[w].at[2 * q + (1 - c)], outs[w].at[q], send_sems, recv_sems, rb + 4 * w + q, sibling))
        return cps

    shapes = tuple(jax.ShapeDtypeStruct((4,) + f.shape[1:], f.dtype) for f in fulls)
    return Exchange(tuple(fulls), shapes, {}, 4 * n, 0, copies)


def scatter_stage2(sums):
    n = len(sums)

    def copies(ins, outs, send_sems, recv_sems, local_sems, rb, lb):
        x, y, _, _ = _me()
        mine = 2 * x + y
        cps = []
        for w in range(n):
            cps.append(pltpu.make_async_copy(ins[w].at[mine], outs[w].at[mine], local_sems.at[lb + w]))
            for a, k in enumerate(CHIP_FLIPS):
                peer, _ = _peer(k)
                cps.append(_remote(ins[w].at[2 * peer[0] + peer[1]], outs[w].at[mine], send_sems, recv_sems,
                                   rb + 3 * w + a, peer))
        return cps

    shapes = tuple(jax.ShapeDtypeStruct(s.shape, s.dtype) for s in sums)
    return Exchange(tuple(sums), shapes, {}, 3 * n, n, copies)


def merge_exchanges(a, b):
    na_in, na_out = len(a.operands), len(a.out_shapes)

    def copies(ins, outs, send_sems, recv_sems, local_sems, rb, lb):
        return (a.copies(ins[:na_in], outs[:na_out], send_sems, recv_sems, local_sems, rb, lb)
                + b.copies(ins[na_in:], outs[na_out:], send_sems, recv_sems, local_sems, rb + a.n_remote, lb + a.n_local))

    aliases = dict(a.aliases)
    aliases.update({na_in + i: na_out + o for i, o in b.aliases.items()})
    return Exchange(a.operands + b.operands, a.out_shapes + b.out_shapes, aliases,
                    a.n_remote + b.n_remote, a.n_local + b.n_local, copies)


def _exchange_scratch(ex):
    return [pltpu.SemaphoreType.DMA((ex.n_remote,)), pltpu.SemaphoreType.DMA((ex.n_remote,)),
            pltpu.SemaphoreType.DMA((max(ex.n_local, 1),))]


def _call(body, *, name, grid, in_specs, out_specs, out_shape, operands, scratch_shapes=(), semantics=(),
          exchange=None):
    if exchange is None:
        return pl.pallas_call(
            body, name=name, grid=grid, in_specs=in_specs, out_specs=out_specs, out_shape=out_shape,
            scratch_shapes=list(scratch_shapes), compiler_params=_params(*semantics))(*operands)
    single = not isinstance(out_shape, (list, tuple))
    out_shapes = [out_shape] if single else list(out_shape)
    out_specs_l = [out_specs] if single else list(out_specs)
    n_in, n_out, n_scr = len(operands), len(out_shapes), len(scratch_shapes)
    x_in, x_out = len(exchange.operands), len(exchange.out_shapes)

    def hosted(*refs):
        ins, refs = refs[:n_in], refs[n_in:]
        xin, refs = refs[:x_in], refs[x_in:]
        outs, refs = refs[:n_out], refs[n_out:]
        xout, refs = refs[:x_out], refs[x_out:]
        scr, sems = refs[:n_scr], refs[n_scr:]
        first = functools.reduce(jnp.logical_and, [pl.program_id(a) == 0 for a in range(len(grid))])
        last = functools.reduce(jnp.logical_and, [pl.program_id(a) == g - 1 for a, g in enumerate(grid)])

        @pl.when(first)
        def _():
            for cp in exchange.copies(xin, xout, *sems, 0, 0):
                cp.start()

        body(*ins, *outs, *scr)

        @pl.when(last)
        def _():
            for cp in exchange.copies(xin, xout, *sems, 0, 0):
                cp.wait()

    res = pl.pallas_call(
        hosted, name=name, grid=grid,
        in_specs=list(in_specs) + [ANY] * x_in, out_specs=out_specs_l + [ANY] * x_out,
        out_shape=out_shapes + list(exchange.out_shapes),
        scratch_shapes=list(scratch_shapes) + _exchange_scratch(exchange),
        input_output_aliases={n_in + i: n_out + o for i, o in exchange.aliases.items()},
        compiler_params=_params(*(["arbitrary"] * len(grid))),
    )(*operands, *exchange.operands)
    outs, xouts = res[:n_out], res[n_out:]
    return (outs[0] if single else outs), xouts


def all_gather_rows(v, name):
    r, n = v.shape

    def body(v_ref, out_ref, send_sems, recv_sems):
        _, _, _, me = _me()
        out_ref[me] = v_ref[...]
        copies = []
        for k in range(1, N_DEV):
            peer, _ = _peer(k)
            copies.append(_remote(v_ref, out_ref.at[me], send_sems, recv_sems, k - 1, peer))
        for cp in copies:
            cp.start()
        for cp in copies:
            cp.wait()

    return pl.pallas_call(
        body, name=name,
        out_shape=jax.ShapeDtypeStruct((N_DEV, r, n), v.dtype),
        in_specs=[pl.BlockSpec(memory_space=pltpu.VMEM)],
        out_specs=pl.BlockSpec(memory_space=pltpu.VMEM),
        scratch_shapes=[pltpu.SemaphoreType.DMA((N_DEV - 1,)), pltpu.SemaphoreType.DMA((N_DEV - 1,))],
    )(v)


def pair_sum(full, pair, name):
    _, r, c = full.shape
    t = r
    core = lax.axis_index("c").astype(jnp.int32).reshape(1)

    def body(core_ref, f_ref, p_ref, o_ref):
        o_ref[...] = (f_ref[...].astype(F32) + p_ref[...].astype(F32)).astype(BF16)

    return pl.pallas_call(
        body, name=name,
        grid_spec=pltpu.PrefetchScalarGridSpec(
            num_scalar_prefetch=1, grid=(4, r // t),
            in_specs=[pl.BlockSpec((None, None, t, c), lambda q, i, core_ref: (q, core_ref[0], i, 0)),
                      pl.BlockSpec((None, t, c), lambda q, i, core_ref: (q, i, 0))],
            out_specs=pl.BlockSpec((None, t, c), lambda q, i, core_ref: (q, i, 0))),
        out_shape=jax.ShapeDtypeStruct((4, r, c), BF16),
        compiler_params=_params("parallel", "parallel"),
    )(core, full.reshape(4, 2, r, c), pair)


def prologue(small_in, w_ada, b_cols, shards, name):
    ex = gather_stage1(shards)
    n_sh = len(shards)
    n_small = small_in.shape[1]
    cols = w_ada.shape[1]

    def body(*refs):
        small_ref, w_ref, b_ref = refs[:3]
        shard_refs = refs[3:3 + n_sh]
        small_out, mod_out = refs[3 + n_sh:5 + n_sh]
        fulls = refs[5 + n_sh:5 + 2 * n_sh]
        part_ref, send1, recv1, send2, recv2, wsend, wrecv, wlocal, fsend, frecv = refs[5 + 2 * n_sh:]
        _, _, _, me = _me()
        def start_gather(src_ref, dst_ref, send_sems, recv_sems):
            cps = [_remote(src_ref, dst_ref.at[me], send_sems, recv_sems, k - 1, _peer(k)[0]) for k in range(1, N_DEV)]
            for cp in cps:
                cp.start()
            return cps

        small_out[me] = small_ref[...]
        first = start_gather(small_ref, small_out, send1, recv1)
        big = ex.copies(shard_refs, fulls, wsend, wrecv, wlocal, 0, 0)
        for cp in big:
            cp.start()
        for cp in first:
            cp.wait()
        c_all = jnp.concatenate([small_out[d][:, :D_MODEL] for d in range(N_DEV)], axis=0)
        act = c_all * _sigmoid(c_all)
        part_ref[...] = jnp.dot(act, w_ref[...], precision=lax.Precision.HIGHEST,
                                preferred_element_type=F32) + b_ref[...]
        mod_out[me] = part_ref[...]
        for cp in start_gather(part_ref, mod_out, send2, recv2):
            cp.wait()
        per = 2 + len(CHIP_FLIPS)
        sibling, _ = _peer(SIBLING)
        onward = []
        for w in range(n_sh):
            for a, k in enumerate(CHIP_FLIPS):
                _, blk = _peer(k)
                big[per * w + 2 + a].wait_recv()
                cp = _remote(fulls[w].at[blk], fulls[w].at[blk], fsend, frecv, len(CHIP_FLIPS) * w + a, sibling)
                cp.start()
                onward.append(cp)
        for w in range(n_sh):
            big[per * w].wait()
            big[per * w + 1].wait()
            for a in range(len(CHIP_FLIPS)):
                big[per * w + 2 + a].wait_send()
        for cp in onward:
            cp.wait()

    vmem = pl.BlockSpec(memory_space=pltpu.VMEM)
    sems = pltpu.SemaphoreType.DMA((N_DEV - 1,))
    res = pl.pallas_call(
        body, name=name,
        out_shape=[jax.ShapeDtypeStruct((N_DEV, 1, n_small), F32), jax.ShapeDtypeStruct((N_DEV, N_DEV, cols), F32)]
        + list(ex.out_shapes),
        in_specs=[vmem, vmem, vmem] + [ANY] * n_sh, out_specs=[vmem, vmem] + [ANY] * n_sh,
        scratch_shapes=[pltpu.VMEM((N_DEV, cols), F32), sems, sems, sems, sems] + _exchange_scratch(ex)
        + [pltpu.SemaphoreType.DMA((len(CHIP_FLIPS) * n_sh,)), pltpu.SemaphoreType.DMA((len(CHIP_FLIPS) * n_sh,))],
        compiler_params=pltpu.CompilerParams(vmem_limit_bytes=VMEM_LIMIT),
    )(small_in, w_ada, b_cols, *shards)
    return res[0], res[1], res[2:]


def ada_backward(c_all, dmod_cols, name):
    n = dmod_cols.shape[1]

    def body(c_ref, d_ref, o_ref):
        c = c_ref[...]
        act = c * _sigmoid(c)
        o_ref[...] = lax.dot_general(act, d_ref[...], TN, precision=lax.Precision.HIGHEST,
                                     preferred_element_type=F32)

    return pl.pallas_call(
        body, name=name, out_shape=jax.ShapeDtypeStruct((D_MODEL, n), F32),
        compiler_params=pltpu.CompilerParams(vmem_limit_bytes=VMEM_LIMIT),
    )(c_all, dmod_cols)


def _row_spec(t, width=D_MODEL):
    return pl.BlockSpec((t, width), lambda i: (i, 0))


def _vec_spec(rows=1, width=D_MODEL):
    return pl.BlockSpec((rows, width), lambda i: (0, 0))


def _resident(shape):
    return pl.BlockSpec(shape, lambda i: (0,) * len(shape), pipeline_mode=pl.Buffered(1))


def _norm_modulate(x_ref, g_ref, shift_ref, scale_ref):
    xv = x_ref[...]
    r = lax.rsqrt(jnp.mean(xv * xv, axis=-1, keepdims=True) + EPS)
    a = (xv * r) * g_ref[...]
    return (a * (1.0 + scale_ref[...]) + shift_ref[...]).astype(BF16)


def norm_modulate_bwd(du, x, dx_out, g, scale, name, prev=None, exchange=None):
    s = x.shape[0]
    factors = isinstance(du, tuple)
    t = min(ROW_TILE if factors else ELT_TILE, s)
    has_prev = prev is not None

    def body(*refs):
        if factors:
            a_ref, b_ref = refs[:2]
            refs = refs[1:]
            nk, _, n = b_ref.shape
        du_ref, x_ref, dxo_ref, g_ref, sc_ref = refs[:5]
        refs = refs[5:]
        if has_prev:
            gt_ref, y_ref = refs[:2]
            refs = refs[2:]
        dx_ref, dsh_ref, dsc_ref, dg_ref = refs[:4]

        @pl.when(pl.program_id(0) == 0)
        def _():
            dsh_ref[...] = jnp.zeros_like(dsh_ref)
            dsc_ref[...] = jnp.zeros_like(dsc_ref)
            dg_ref[...] = jnp.zeros_like(dg_ref)
            if has_prev:
                refs[5][...] = jnp.zeros_like(refs[5])

        xv = x_ref[...]
        if factors:
            duv = _dot(a_ref[:, 0:n], b_ref[0], NT)
            for k in range(1, nk):
                duv = duv + _dot(a_ref[:, k * n:(k + 1) * n], b_ref[k], NT)
        else:
            duv = du_ref[...].astype(F32)
        gv = g_ref[...]
        r = lax.rsqrt(jnp.mean(xv * xv, axis=-1, keepdims=True) + EPS)
        nrm = xv * r
        a = nrm * gv
        dsh_ref[...] += jnp.sum(duv, axis=0, keepdims=True)
        dsc_ref[...] += jnp.sum(duv * a, axis=0, keepdims=True)
        da = duv * (1.0 + sc_ref[...])
        dg_ref[...] += jnp.sum(da * nrm, axis=0, keepdims=True)
        dn = da * gv
        dx = dxo_ref[...] + r * (dn - nrm * jnp.mean(dn * nrm, axis=-1, keepdims=True))
        dx_ref[...] = dx
        if has_prev:
            coef = prev[2]
            refs[4][...] = (coef * gt_ref[...] * dx).astype(BF16)
            refs[5][...] += coef * jnp.sum(dx * y_ref[...].astype(F32), axis=0, keepdims=True)

    vec = jax.ShapeDtypeStruct((1, D_MODEL), F32)
    if factors:
        operands = [du[0], du[1], x, dx_out, g, scale]
        in_specs = [_row_spec(t, du[0].shape[1]), _resident(du[1].shape)]
    else:
        operands = [du, x, dx_out, g, scale]
        in_specs = [_row_spec(t)]
    in_specs += [_row_spec(t), _row_spec(t), _vec_spec(), _vec_spec()]
    out_specs = [_row_spec(t), _vec_spec(), _vec_spec(), _vec_spec()]
    out_shape = [jax.ShapeDtypeStruct((s, D_MODEL), F32), vec, vec, vec]
    if has_prev:
        operands += [prev[0], prev[1]]
        in_specs += [_vec_spec(), _row_spec(t)]
        out_specs += [_row_spec(t), _vec_spec()]
        out_shape += [jax.ShapeDtypeStruct((s, D_MODEL), BF16), vec]
    return _call(body, name=name, grid=(s // t,), in_specs=in_specs, out_specs=out_specs, out_shape=out_shape,
                 operands=operands, semantics=("arbitrary",), exchange=exchange)


def ffn_up(x, norm_g, shift, scale, w_gu_t, name, exchange=None):
    s = x.shape[0]
    t = min(ROW_TILE, s)

    def body(x_ref, g_ref, sh_ref, sc_ref, w_ref, u_ref, gu_ref, act_ref):
        uv = _norm_modulate(x_ref, g_ref, sh_ref, sc_ref)
        u_ref[...] = uv
        for j in range(4):
            g = _dot(uv, w_ref[j], NT)
            up = _dot(uv, w_ref[j + 4], NT)
            gu_ref[0, j] = g.astype(BF16)
            gu_ref[1, j] = up.astype(BF16)
            act_ref[j] = (g * _sigmoid(g) * up).astype(BF16)

    return _call(
        body, name=name, grid=(s // t,),
        in_specs=[_row_spec(t), _vec_spec(), _vec_spec(), _vec_spec(), _resident(w_gu_t.shape)],
        out_specs=[_row_spec(t), pl.BlockSpec((2, 4, t, FF_BLK), lambda i: (0, 0, i, 0)),
                   pl.BlockSpec((4, t, FF_BLK), lambda i: (0, i, 0))],
        out_shape=[jax.ShapeDtypeStruct((s, D_MODEL), BF16), jax.ShapeDtypeStruct((2, 4, s, FF_BLK), BF16),
                   jax.ShapeDtypeStruct((4, s, FF_BLK), BF16)],
        operands=(x, norm_g, shift, scale, w_gu_t), semantics=("parallel",), exchange=exchange)


def residual_matmul(a, b, x, gate, coef, name, exchange=None):
    nk, s, kb = a.shape
    t = min(ROW_TILE, s)

    def body(a_ref, b_ref, x_ref, gt_ref, xo_ref, y_ref):
        y = _dot(a_ref[0], b_ref[0])
        for k in range(1, nk):
            y = y + _dot(a_ref[k], b_ref[k])
        y_ref[...] = y.astype(BF16)
        xo_ref[...] = x_ref[...] + coef * gt_ref[...] * y

    return _call(
        body, name=name, grid=(s // t,),
        in_specs=[pl.BlockSpec((nk, t, kb), lambda i: (0, i, 0)),
                  pl.BlockSpec((nk, kb, D_MODEL), lambda i: (0, 0, 0)),
                  _row_spec(t), _vec_spec()],
        out_specs=[_row_spec(t), _row_spec(t)],
        out_shape=[jax.ShapeDtypeStruct((s, D_MODEL), F32), jax.ShapeDtypeStruct((s, D_MODEL), BF16)],
        operands=(a, b, x, gate), semantics=("parallel",), exchange=exchange)


def residual_matmul_loss(a, b, x, gate, coef, target, final_g, name):
    nk, s, kb = a.shape
    t = min(ROW_TILE, s)

    def body(a_ref, b_ref, x_ref, gt_ref, t_ref, fg_ref, dx_ref, dy_ref, dgt_ref, dfg_ref, sq_ref):
        @pl.when(pl.program_id(0) == 0)
        def _():
            dgt_ref[...] = jnp.zeros_like(dgt_ref)
            dfg_ref[...] = jnp.zeros_like(dfg_ref)
            sq_ref[...] = jnp.zeros_like(sq_ref)

        y = _dot(a_ref[0], b_ref[0])
        for k in range(1, nk):
            y = y + _dot(a_ref[k], b_ref[k])
        gt = gt_ref[...]
        fg = fg_ref[...]
        xv = x_ref[...] + coef * gt * y
        r = lax.rsqrt(jnp.mean(xv * xv, axis=-1, keepdims=True) + EPS)
        nrm = xv * r
        err = nrm * fg - t_ref[...]
        sq_ref[...] += jnp.sum(err * err, axis=0, keepdims=True)
        dout = err * (1.0 / D_MODEL)
        dfg_ref[...] += jnp.sum(dout * nrm, axis=0, keepdims=True)
        dn = dout * fg
        dx = r * (dn - nrm * jnp.mean(dn * nrm, axis=-1, keepdims=True))
        dx_ref[...] = dx
        dy_ref[...] = (coef * gt * dx).astype(BF16)
        dgt_ref[...] += coef * jnp.sum(dx * y, axis=0, keepdims=True)

    vec = jax.ShapeDtypeStruct((1, D_MODEL), F32)
    return pl.pallas_call(
        body, name=name, grid=(s // t,),
        in_specs=[pl.BlockSpec((nk, t, kb), lambda i: (0, i, 0)), _resident(b.shape),
                  _row_spec(t), _vec_spec(), _row_spec(t), _vec_spec()],
        out_specs=[_row_spec(t), _row_spec(t), _vec_spec(), _vec_spec(), _vec_spec()],
        out_shape=[jax.ShapeDtypeStruct((s, D_MODEL), F32), jax.ShapeDtypeStruct((s, D_MODEL), BF16), vec, vec, vec],
        compiler_params=_params("arbitrary"),
    )(a, b, x, gate, target, final_g)


def ffn_tokens_bwd(dy, w_down, gu, w_gu_t, name, exchange=None):
    s = dy.shape[0]
    t = min(ROW_TILE, s)
    with_du = w_gu_t is not None

    def body(*refs):
        if with_du:
            dy_ref, wd_ref, gu_ref, wgu_ref, dgu_ref, du_ref = refs
        else:
            dy_ref, wd_ref, gu_ref, dgu_ref = refs
        dyv = dy_ref[...]
        du = None
        for j in range(4):
            dact = _dot(dyv, wd_ref[j], NT)
            g = gu_ref[0, j].astype(F32)
            up = gu_ref[1, j].astype(F32)
            sg = _sigmoid(g)
            slopes = (up * sg * (1.0 + g * (1.0 - sg)), g * sg)
            for half in range(2):
                d = (dact * slopes[half]).astype(BF16)
                dgu_ref[half, j] = d
                if with_du:
                    part = _dot(d, wgu_ref[4 * half + j])
                    du = part if du is None else du + part
        if with_du:
            du_ref[...] = du.astype(BF16)

    blocks = pl.BlockSpec((2, 4, t, FF_BLK), lambda i: (0, 0, i, 0))
    dgu_shape = jax.ShapeDtypeStruct((2, 4, s, FF_BLK), BF16)
    if with_du:
        return _call(
            body, name=name, grid=(s // t,),
            in_specs=[_row_spec(t), _resident(w_down.shape), blocks, _resident(w_gu_t.shape)],
            out_specs=[blocks, _row_spec(t)],
            out_shape=[dgu_shape, jax.ShapeDtypeStruct((s, D_MODEL), BF16)],
            operands=(dy, w_down, gu, w_gu_t), semantics=("parallel",), exchange=exchange)
    return _call(
        body, name=name, grid=(s // t,),
        in_specs=[_row_spec(t), _resident(w_down.shape), blocks], out_specs=blocks, out_shape=dgu_shape,
        operands=(dy, w_down, gu), semantics=("parallel",), exchange=exchange)


def matmul_nt_acc(a, b, name, b_dims=NT, exchange=None):
    nk = b.shape[0]
    d, n = (b.shape[1], b.shape[2]) if b_dims == NT else (b.shape[2], b.shape[1])
    s = a.shape[-2]
    t = min(ROW_TILE, s)
    by_columns = a.ndim == 2

    def body(a_ref, b_ref, o_ref):
        def a_blk(k):
            return a_ref[:, k * n:(k + 1) * n] if by_columns else a_ref[k]

        acc = _dot(a_blk(0), b_ref[0], b_dims)
        for k in range(1, nk):
            acc = acc + _dot(a_blk(k), b_ref[k], b_dims)
        o_ref[...] = acc.astype(BF16)

    a_spec = _row_spec(t, nk * n) if by_columns else pl.BlockSpec((nk, t, n), lambda i: (0, i, 0))
    return _call(
        body, name=name, grid=(s // t,),
        in_specs=[a_spec, pl.BlockSpec(b.shape, lambda i: (0, 0, 0))],
        out_specs=pl.BlockSpec((t, d), lambda i: (i, 0)),
        out_shape=jax.ShapeDtypeStruct((s, d), BF16),
        operands=(a, b), semantics=("parallel",), exchange=exchange)


def matmul_tn(a, b, name, group=(1, 1), b_cols=None, exchange=None):
    ja, s, m = a.shape
    by_columns = b.ndim == 2
    jb, n = (b.shape[1] // b_cols, b_cols) if by_columns else (b.shape[0], b.shape[2])
    ga, gb = group
    t = min(ACC_TILE, s)
    nk = s // t

    def body(a_ref, b_ref, o_ref, acc_ref):
        k = pl.program_id(2)

        @pl.when(k == 0)
        def _():
            acc_ref[...] = jnp.zeros_like(acc_ref)

        for p in range(ga):
            for q in range(gb):
                b_blk = b_ref[:, q * n:(q + 1) * n] if by_columns else b_ref[q]
                acc_ref[p, q] += _dot(a_ref[p], b_blk, TN)

        @pl.when(k == nk - 1)
        def _():
            o_ref[...] = acc_ref[...].astype(BF16)

    return _call(
        body, name=name, grid=(ja // ga, jb // gb, nk),
        in_specs=[pl.BlockSpec((ga, t, m), lambda p, q, k: (p, k, 0)),
                  pl.BlockSpec((t, gb * n), lambda p, q, k: (k, q)) if by_columns
                  else pl.BlockSpec((gb, t, n), lambda p, q, k: (q, k, 0))],
        out_specs=pl.BlockSpec((ga, gb, m, n), lambda p, q, k: (p, q, 0, 0)),
        out_shape=jax.ShapeDtypeStruct((ja, jb, m, n), BF16),
        operands=(a, b), scratch_shapes=[pltpu.VMEM((ga, gb, m, n), F32)],
        semantics=("parallel", "parallel", "arbitrary"), exchange=exchange)


def mix_in_proj(x, norm_g, shift, scale, w_mix, name, exchange=None):
    s = x.shape[0]
    t = min(ROW_TILE, s)

    def body(x_ref, g_ref, sh_ref, sc_ref, w_ref, u_ref, o_ref):
        uv = _norm_modulate(x_ref, g_ref, sh_ref, sc_ref)
        u_ref[...] = uv
        for j in range(N_DEV):
            o_ref[:, j * MIX_BLK:(j + 1) * MIX_BLK] = _dot(uv, w_ref[j]).astype(BF16)

    return _call(
        body, name=name, grid=(s // t,),
        in_specs=[_row_spec(t), _vec_spec(), _vec_spec(), _vec_spec(), _resident(w_mix.shape)],
        out_specs=[_row_spec(t), _row_spec(t, MIX_W)],
        out_shape=[jax.ShapeDtypeStruct((s, D_MODEL), BF16), jax.ShapeDtypeStruct((s, MIX_W), BF16)],
        operands=(x, norm_g, shift, scale, w_mix), semantics=("parallel",), exchange=exchange)


def _conv_taps(cc_ref, cx_ref, s):
    v = cc_ref[...].astype(F32) * cx_ref[...].astype(F32)
    tok = lax.broadcasted_iota(jnp.int32, v.shape, 0)
    v1 = jnp.where(tok >= 1, pltpu.roll(v, 1, 0), 0.0)
    v2 = jnp.where(tok >= 2, pltpu.roll(v, 2, 0), 0.0)
    return v, v1, v2, tok


def _proj_cols(s, first):
    return pl.BlockSpec((s, 128), lambda j: (0, first + j))


def short_conv(proj, conv_w, name):
    s = proj.shape[0]

    def body(cb_ref, cc_ref, cx_ref, w_ref, o_ref):
        v, v1, v2, _ = _conv_taps(cc_ref, cx_ref, s)
        y = w_ref[0:1, :] * v2 + w_ref[1:2, :] * v1 + w_ref[2:3, :] * v
        o_ref[...] = (cb_ref[...].astype(F32) * y).astype(BF16)

    return pl.pallas_call(
        body, name=name, grid=(CONV_W // 128,),
        in_specs=[_proj_cols(s, 0), _proj_cols(s, 4), _proj_cols(s, 8),
                  pl.BlockSpec((3, 128), lambda j: (0, j))],
        out_specs=pl.BlockSpec((s, 128), lambda j: (0, j)),
        out_shape=jax.ShapeDtypeStruct((s, CONV_W), BF16),
        compiler_params=_params("parallel"),
    )(proj, proj, proj, conv_w)


def short_conv_bwd(dsa, proj, conv_w, name):
    s = proj.shape[0]

    def body(dsa_ref, cb_ref, cc_ref, cx_ref, w_ref, dcb_ref, dcc_ref, dcx_ref, dw_ref):
        v, v1, v2, tok = _conv_taps(cc_ref, cx_ref, s)
        w0, w1, w2 = w_ref[0:1, :], w_ref[1:2, :], w_ref[2:3, :]
        y = w0 * v2 + w1 * v1 + w2 * v
        dsa_v = dsa_ref[...].astype(F32)
        dcb_ref[...] = (dsa_v * y).astype(BF16)
        dy = dsa_v * cb_ref[...].astype(F32)
        dw_ref[0:1, :] = jnp.sum(dy * v2, axis=0, keepdims=True)
        dw_ref[1:2, :] = jnp.sum(dy * v1, axis=0, keepdims=True)
        dw_ref[2:3, :] = jnp.sum(dy * v, axis=0, keepdims=True)
        dy1 = jnp.where(tok < s - 1, pltpu.roll(dy, s - 1, 0), 0.0)
        dy2 = jnp.where(tok < s - 2, pltpu.roll(dy, s - 2, 0), 0.0)
        dv = w2 * dy + w1 * dy1 + w0 * dy2
        dcc_ref[...] = (dv * cx_ref[...].astype(F32)).astype(BF16)
        dcx_ref[...] = (dv * cc_ref[...].astype(F32)).astype(BF16)

    col = pl.BlockSpec((s, 128), lambda j: (0, j))
    act = jax.ShapeDtypeStruct((s, CONV_W), BF16)
    return pl.pallas_call(
        body, name=name, grid=(CONV_W // 128,),
        in_specs=[col, _proj_cols(s, 0), _proj_cols(s, 4), _proj_cols(s, 8),
                  pl.BlockSpec((3, 128), lambda j: (0, j))],
        out_specs=[col, col, col, pl.BlockSpec((3, 128), lambda j: (0, j))],
        out_shape=[act, act, act, jax.ShapeDtypeStruct((3, CONV_W), F32)],
        compiler_params=_params("parallel"),
    )(dsa, proj, proj, proj, conv_w)


def _gate_specs(t):
    return [pl.BlockSpec((t, D_MODEL), lambda i: (i, 3)), pl.BlockSpec((t, D_MODEL), lambda i: (i, 4))]


def merge_forward(sa, o, proj, w_co, w_ao, b_merge, w_out, x, gate, name, exchange=None):
    s = sa.shape[0]
    t = min(ROW_TILE, s)

    def body(sa_ref, o_ref, ga_ref, gb_ref, wco_ref, wao_ref, bm_ref, wout_ref, x_ref, gt_ref,
             mg_ref, ya_ref, yb_ref, y_ref, xo_ref):
        ya = _dot(sa_ref[...], wco_ref[...])
        yb = _dot(o_ref[...], wao_ref[...])
        sga = _sigmoid(ga_ref[...].astype(F32) + bm_ref[0:1, :])
        sgb = _sigmoid(gb_ref[...].astype(F32) + bm_ref[1:2, :])
        merged = (sga * ya + sgb * yb).astype(BF16)
        mg_ref[...] = merged
        ya_ref[...] = ya.astype(BF16)
        yb_ref[...] = yb.astype(BF16)
        y = _dot(merged, wout_ref[...])
        y_ref[...] = y.astype(BF16)
        xo_ref[...] = x_ref[...] + gt_ref[...] * y

    act = jax.ShapeDtypeStruct((s, D_MODEL), BF16)
    return _call(
        body, name=name, grid=(s // t,),
        in_specs=[_row_spec(t, CONV_W), _row_spec(t, ATTN_W)] + _gate_specs(t)
        + [_vec_spec(CONV_W), _vec_spec(ATTN_W), _vec_spec(2), _vec_spec(D_MODEL), _row_spec(t), _vec_spec()],
        out_specs=[_row_spec(t)] * 5, out_shape=[act, act, act, act, jax.ShapeDtypeStruct((s, D_MODEL), F32)],
        operands=(sa, o, proj, proj, w_co, w_ao, b_merge, w_out, x, gate), semantics=("parallel",),
        exchange=exchange)


def merge_backward(dy, w_out, proj, ya, yb, b_merge, name, exchange=None):
    s = dy.shape[0]
    t = min(ROW_TILE, s)

    def body(dy_ref, w_ref, ga_ref, gb_ref, ya_ref, yb_ref, bm_ref,
             dya_ref, dyb_ref, dga_ref, dgb_ref, dbm_ref):
        @pl.when(pl.program_id(0) == 0)
        def _():
            dbm_ref[...] = jnp.zeros_like(dbm_ref)

        dmg = _dot(dy_ref[...], w_ref[...], NT)
        sga = _sigmoid(ga_ref[...].astype(F32) + bm_ref[0:1, :])
        sgb = _sigmoid(gb_ref[...].astype(F32) + bm_ref[1:2, :])
        dya_ref[...] = (dmg * sga).astype(BF16)
        dyb_ref[...] = (dmg * sgb).astype(BF16)
        dga = dmg * ya_ref[...].astype(F32) * sga * (1.0 - sga)
        dgb = dmg * yb_ref[...].astype(F32) * sgb * (1.0 - sgb)
        dga_ref[...] = dga.astype(BF16)
        dgb_ref[...] = dgb.astype(BF16)
        dbm_ref[0:1, :] += jnp.sum(dga, axis=0, keepdims=True)
        dbm_ref[1:2, :] += jnp.sum(dgb, axis=0, keepdims=True)

    act = jax.ShapeDtypeStruct((s, D_MODEL), BF16)
    return _call(
        body, name=name, grid=(s // t,),
        in_specs=[_row_spec(t), _vec_spec(D_MODEL)] + _gate_specs(t)
        + [_row_spec(t), _row_spec(t), _vec_spec(2)],
        out_specs=[_row_spec(t)] * 4 + [_vec_spec(2)],
        out_shape=[act] * 4 + [jax.ShapeDtypeStruct((2, D_MODEL), F32)],
        operands=(dy, w_out, proj, proj, ya, yb, b_merge), semantics=("arbitrary",), exchange=exchange)


def out_proj_bwd(dya, dyb, w_co, w_ao, name):
    s = dya.shape[0]
    t = min(ROW_TILE, s)

    def body(dya_ref, dyb_ref, wco_ref, wao_ref, dsa_ref, do_ref):
        dsa_ref[...] = _dot(dya_ref[...], wco_ref[...], NT).astype(BF16)
        do_ref[...] = _dot(dyb_ref[...], wao_ref[...], NT).astype(BF16)

    return pl.pallas_call(
        body, name=name, grid=(s // t,),
        in_specs=[_row_spec(t), _row_spec(t), _vec_spec(CONV_W), _vec_spec(ATTN_W)],
        out_specs=[_row_spec(t, CONV_W), _row_spec(t, ATTN_W)],
        out_shape=[jax.ShapeDtypeStruct((s, CONV_W), BF16), jax.ShapeDtypeStruct((s, ATTN_W), BF16)],
        compiler_params=_params("parallel"),
    )(dya, dyb, w_co, w_ao)


ATT_HEADS = 4
ATT_LANES = ATT_HEADS * HEAD_DIM
ATT_UNDERFLOW = 110.0


def _softplus(z):
    return jnp.maximum(z, 0.0) + jnp.log(1.0 + jnp.exp(-jnp.abs(z)))


def _head_masks(rows):
    lane = lax.broadcasted_iota(jnp.int32, (rows, ATT_LANES), 1)
    return [(lane >= h * HEAD_DIM) & (lane < (h + 1) * HEAD_DIM) for h in range(ATT_HEADS)]


def _per_head(x, masks):
    return [jnp.where(m, x, jnp.zeros_like(x)) for m in masks]


def _att_specs(s, blk):
    first = {"q": 3 * CONV_W // ATT_LANES, "k": (3 * CONV_W + ATTN_W) // ATT_LANES,
             "v": (3 * CONV_W + 2 * ATTN_W) // ATT_LANES}
    return [pl.BlockSpec((blk, ATT_LANES), lambda h, i: (i, first["q"] + h)),
            pl.BlockSpec((s, ATT_LANES), lambda h, i: (0, first["k"] + h)),
            pl.BlockSpec((s, ATT_LANES), lambda h, i: (0, first["v"] + h))]


def _head_norms(x, masks):
    sq = jnp.square(x.astype(F32))
    return [jnp.sum(jnp.where(m, sq, 0.0), axis=1, keepdims=True) for m in masks]


def stick_breaking_fwd(proj, name, exchange=None):
    s = proj.shape[0]
    blk = ATT_BLK
    nq = s // blk

    def body(q_ref, k_ref, v_ref, o_ref, tot_ref, first_ref, kmax_ref):
        i = pl.program_id(1)
        row = lax.broadcasted_iota(jnp.int32, (blk, blk), 0)
        col = lax.broadcasted_iota(jnp.int32, (blk, blk), 1)
        tri = (row >= col).astype(BF16)
        causal = col < row
        masks = _head_masks(blk)
        q_all = q_ref[...] * ATTN_SCALE
        qs = _per_head(q_all, masks)

        @pl.when(i == 0)
        def _():
            def longest(n, best):
                norms = _head_norms(k_ref[pl.ds(pl.multiple_of(n * blk, blk), blk), :], masks)
                return tuple(jnp.maximum(b, v) for b, v in zip(best, norms))

            best = lax.fori_loop(0, nq, longest, tuple(jnp.zeros((blk, 1), F32) for _ in range(ATT_HEADS)))
            for h in range(ATT_HEADS):
                kmax_ref[h] = jnp.sqrt(jnp.max(best[h], axis=0, keepdims=True))

        needed = [jnp.sqrt(n) * kmax_ref[h] + ATT_UNDERFLOW for h, n in enumerate(_head_norms(q_all, masks))]

        def finished(laters):
            slack = laters[0] - needed[0]
            for h in range(1, ATT_HEADS):
                slack = jnp.minimum(slack, laters[h] - needed[h])
            return (jnp.min(slack) >= 0.0).astype(jnp.int32)

        def step(j, carry, diagonal):
            laters, acc = carry
            rows = pl.ds(pl.multiple_of(j * blk, blk), blk)
            kb = k_ref[rows, :]
            probs, new_laters = [], []
            for h in range(ATT_HEADS):
                z = _dot(qs[h], kb, NT)
                sp = _softplus(z)
                if diagonal:
                    sp = jnp.where(causal, sp, 0.0)
                a = jnp.exp(z - (_dot(sp.astype(BF16), tri) + laters[h]))
                if diagonal:
                    a = jnp.where(causal, a, 0.0)
                probs.append(a.astype(BF16))
                new_laters.append(laters[h] + jnp.sum(sp, axis=1, keepdims=True))
            v_heads = jnp.concatenate(_per_head(v_ref[rows, :], masks), axis=0)
            acc = acc + _dot(jnp.concatenate(probs, axis=1), v_heads)
            return tuple(new_laters), acc

        carry = (tuple(jnp.zeros((blk, 1), F32) for _ in range(ATT_HEADS)), jnp.zeros((blk, ATT_LANES), F32))
        laters, acc = step(i, carry, True)

        def further(state):
            n, _, laters, acc = state
            laters, acc = step(i - 1 - n, (laters, acc), False)
            return n + 1, finished(laters), laters, acc

        walked, _, laters, acc = lax.while_loop(
            lambda state: jnp.logical_and(state[0] < i, state[1] == 0), further,
            (jnp.int32(0), finished(laters), laters, acc))
        o_ref[...] = acc.astype(BF16)
        tot = jnp.zeros((blk, ATT_LANES), F32)
        for h in range(ATT_HEADS):
            tot = jnp.where(masks[h], laters[h], tot)
        tot_ref[...] = tot
        first_ref[...] = jnp.full(first_ref.shape, i - walked, jnp.int32).astype(F32)

    out_spec = pl.BlockSpec((blk, ATT_LANES), lambda h, i: (i, h))
    groups = N_HEADS // ATT_HEADS
    return _call(
        body, name=name, grid=(groups, nq),
        in_specs=_att_specs(s, blk),
        out_specs=[out_spec, out_spec, pl.BlockSpec((None, None, 8, 128), lambda h, i: (h, i, 0, 0))],
        out_shape=[jax.ShapeDtypeStruct((s, ATTN_W), BF16), jax.ShapeDtypeStruct((s, ATTN_W), F32),
                   jax.ShapeDtypeStruct((groups, nq, 8, 128), F32)],
        operands=(proj, proj, proj), scratch_shapes=[pltpu.VMEM((ATT_HEADS, 1, 1), F32)],
        semantics=("parallel", "arbitrary"), exchange=exchange)


def stick_breaking_bwd(proj, do, tot, first, name, exchange=None):
    s = proj.shape[0]
    blk = ATT_BLK
    nq = s // blk

    def body(q_ref, k_ref, v_ref, do_ref, tot_ref, first_ref, dq_ref, dk_ref, dv_ref):
        i = pl.program_id(1)
        start = jnp.clip(jnp.max(first_ref[...]).astype(jnp.int32), 0, i)

        @pl.when(i == 0)
        def _():
            dk_ref[...] = jnp.zeros_like(dk_ref)
            dv_ref[...] = jnp.zeros_like(dv_ref)

        row = lax.broadcasted_iota(jnp.int32, (blk, blk), 0)
        col = lax.broadcasted_iota(jnp.int32, (blk, blk), 1)
        before = (row < col).astype(BF16)
        upto = (row <= col).astype(BF16)
        causal = col < row
        masks = _head_masks(blk)
        qs = _per_head(q_ref[...] * ATTN_SCALE, masks)
        dos = _per_head(do_ref[...], masks)
        q_heads = jnp.concatenate(qs, axis=0)
        do_heads = jnp.concatenate(dos, axis=0)
        tot_all = tot_ref[...]
        totals = [jnp.max(jnp.where(m, tot_all, 0.0), axis=1, keepdims=True) for m in masks]

        def step(j, carry, diagonal):
            earliers, g_sums, dq = carry
            rows = pl.ds(pl.multiple_of(j * blk, blk), blk)
            kb = k_ref[rows, :]
            vb = v_ref[rows, :]
            probs, dzs, new_earliers, new_g_sums = [], [], [], []
            for h in range(ATT_HEADS):
                z = _dot(qs[h], kb, NT)
                sp = _softplus(z)
                if diagonal:
                    sp = jnp.where(causal, sp, 0.0)
                c = (totals[h] - earliers[h]) - _dot(sp.astype(BF16), before)
                a = jnp.exp(z - c)
                if diagonal:
                    a = jnp.where(causal, a, 0.0)
                g = a * _dot(dos[h], vb, NT)
                f = g_sums[h] + _dot(g.astype(BF16), upto)
                dz = g - jnp.exp(z - sp) * f
                if diagonal:
                    dz = jnp.where(causal, dz, 0.0)
                probs.append(a.astype(BF16))
                dzs.append(dz.astype(BF16))
                new_earliers.append(earliers[h] + jnp.sum(sp, axis=1, keepdims=True))
                new_g_sums.append(g_sums[h] + jnp.sum(g, axis=1, keepdims=True))
            k_heads = jnp.concatenate(_per_head(kb, masks), axis=0)
            dq = dq + _dot(jnp.concatenate(dzs, axis=1), k_heads)
            dk_ref[rows, :] += _dot(jnp.concatenate(dzs, axis=0), q_heads, TN)
            dv_ref[rows, :] += _dot(jnp.concatenate(probs, axis=0), do_heads, TN)
            return tuple(new_earliers), tuple(new_g_sums), dq

        zeros = tuple(jnp.zeros((blk, 1), F32) for _ in range(ATT_HEADS))
        carry = (zeros, zeros, jnp.zeros((blk, ATT_LANES), F32))
        carry = lax.fori_loop(start, i, lambda j, c: step(j, c, False), carry)
        dq = step(i, carry, True)[2]
        dq_ref[...] = (dq * ATTN_SCALE).astype(BF16)

    blk_spec = pl.BlockSpec((blk, ATT_LANES), lambda h, i: (i, h))
    full_spec = pl.BlockSpec((s, ATT_LANES), lambda h, i: (0, h))
    return _call(
        body, name=name, grid=(N_HEADS // ATT_HEADS, nq),
        in_specs=_att_specs(s, blk) + [blk_spec, blk_spec,
                                       pl.BlockSpec((None, None, 8, 128), lambda h, i: (h, i, 0, 0))],
        out_specs=[blk_spec, full_spec, full_spec],
        out_shape=[jax.ShapeDtypeStruct((s, ATTN_W), BF16), jax.ShapeDtypeStruct((s, ATTN_W), F32),
                   jax.ShapeDtypeStruct((s, ATTN_W), F32)],
        operands=(proj, proj, proj, do, tot, first), semantics=("parallel", "arbitrary"), exchange=exchange)


def adamw(w, m, v, parts, name):
    r, c = w.shape
    p = parts.shape[0]
    t = r
    for cand in (256, 176):
        if r % cand == 0 and r > cand:
            t = cand
            break

    def body(w_ref, m_ref, v_ref, p_ref, g_ref, d_ref, mo_ref, vo_ref):
        g = p_ref[0].astype(F32)
        for n in range(1, p):
            g = g + p_ref[n].astype(F32)
        m_new = ADAM_B1 * m_ref[...] + (1.0 - ADAM_B1) * g
        v_new = ADAM_B2 * v_ref[...] + (1.0 - ADAM_B2) * (g * g)
        m_hat = m_new / ADAM_BC1
        v_hat = v_new / ADAM_BC2
        g_ref[...] = g
        d_ref[...] = -ADAM_LR * (m_hat / (jnp.sqrt(v_hat) + ADAM_EPS) + ADAM_WD * w_ref[...])
        mo_ref[...] = m_new
        vo_ref[...] = v_new

    spec = pl.BlockSpec((t, c), lambda i: (i, 0))
    out = jax.ShapeDtypeStruct((r, c), F32)
    return pl.pallas_call(
        body, name=name, grid=(r // t,),
        in_specs=[spec, spec, spec, pl.BlockSpec((p, t, c), lambda i: (0, i, 0))],
        out_specs=[spec] * 4, out_shape=[out] * 4,
        compiler_params=_params("parallel"),
    )(w, m, v, parts)


def kernel(x, c, w_ada, b_ada, norm1_g, ffn1_w_gu, ffn1_w_down, norm2_g, w_mix_in, b_merge, conv_w, w_conv_out, w_attn_out, w_out, norm3_g, ffn2_w_gu, ffn2_w_down, final_g, loss_target, m_w_ada, m_b_ada, m_norm1_g, m_ffn1_w_gu, m_ffn1_w_down, m_norm2_g, m_w_mix_in, m_b_merge, m_conv_w, m_w_conv_out, m_w_attn_out, m_w_out, m_norm3_g, m_ffn2_w_gu, m_ffn2_w_down, m_final_g, v_w_ada, v_b_ada, v_norm1_g, v_ffn1_w_gu, v_ffn1_w_down, v_norm2_g, v_w_mix_in, v_b_merge, v_conv_w, v_w_conv_out, v_w_attn_out, v_w_out, v_norm3_g, v_ffn2_w_gu, v_ffn2_w_down, v_final_g):
    s = x.shape[1]
    me = 4 * lax.axis_index("x") + 2 * lax.axis_index("y") + lax.axis_index("c")
    x0 = x[0]
    target = loss_target[0]
    final_g2 = final_g.reshape(1, D_MODEL)

    def shard(w):
        return w[0].astype(BF16)

    def flipped(w):
        return jnp.swapaxes(w, 1, 2)

    def rows8(g):
        return g.reshape(N_DEV, -1, D_MODEL)

    small_in = jnp.concatenate([c.reshape(-1), b_merge.reshape(-1), conv_w.reshape(-1),
                                jnp.zeros((64,), F32)]).reshape(1, -1)
    n_ada = w_ada.shape[2]
    b_cols = lax.dynamic_slice(b_ada, (0, me * n_ada), (1, n_ada))
    small_all, mod_all, (wgu1, wd1) = prologue(small_in, w_ada[0], b_cols,
                                               [shard(flipped(ffn1_w_gu)), shard(ffn1_w_down)], "prologue")
    wd1 = wd1.reshape(4, FF_BLK, D_MODEL)
    small_all = small_all[:, 0, :]
    c_all = small_all[:, :D_MODEL]
    bm_full = small_all[:, 1024:1280].reshape(8, 2, 128).transpose(1, 0, 2).reshape(2, D_MODEL)
    cw_full = small_all[:, 1280:1472].reshape(8, 3, 64).transpose(1, 0, 2).reshape(3, CONV_W)
    mod = lax.dynamic_index_in_dim(mod_all, me, axis=1, keepdims=False).reshape(9, 1, D_MODEL)
    sh1, sc1, gt1, sh2, sc2, gt2, sh3, sc3, gt3 = [mod[n] for n in range(9)]

    (u1, gu1, act1), got = ffn_up(x0, norm1_g, sh1, sc1, wgu1, "ffn_up_1",
                                  exchange=gather_stage1([shard(w_mix_in)]))
    (x1, y1), (wmix, *got) = residual_matmul(
        act1, wd1, x0, gt1, 0.5, "ffn_down_1", exchange=merge_exchanges(
            gather_stage2(got), gather_stage1([shard(w_conv_out), shard(w_attn_out), shard(w_out)])))

    (u2, proj), (wco, wao, wout) = mix_in_proj(x1, norm2_g, sh2, sc2, wmix, "mix_in",
                                               exchange=gather_stage2(got))
    wco = wco.transpose(1, 0, 2).reshape(CONV_W, D_MODEL)
    wao = wao.transpose(1, 0, 2).reshape(ATTN_W, D_MODEL)
    wout = wout.reshape(D_MODEL, D_MODEL)
    sa = short_conv(proj, cw_full, "short_conv")
    (o, tot, first), got = stick_breaking_fwd(proj, "attn_fwd",
                                       exchange=gather_stage1([shard(flipped(ffn2_w_gu)), shard(ffn2_w_down)]))
    (merged, ya, yb, y2, x2), (wgu3, wd3) = merge_forward(sa, o, proj, wco, wao, bm_full, wout, x1, gt2, "merge",
                                                          exchange=gather_stage2(got))
    wd3 = wd3.reshape(4, FF_BLK, D_MODEL)

    u3, gu3, act3 = ffn_up(x2, norm3_g, sh3, sc3, wgu3, "ffn_up_3")

    dx3, dy3, dgt3, dfinal, sq = residual_matmul_loss(act3, wd3, x2, gt3, 0.5, target, final_g2, "ffn_down_3_loss")
    dgu3, du3 = ffn_tokens_bwd(dy3, wd3, gu3, wgu3, "ffn_bwd_3")
    dgu3 = dgu3.reshape(8, s, FF_BLK)
    g_wd3 = rows8(matmul_tn(act3, dy3[None], "grad_w_down_3", group=(4, 1)))
    g_wgu3 = matmul_tn(dgu3, u3[None], "grad_w_gu_3", group=(4, 1)).reshape(8, FF_BLK, D_MODEL)
    dx2, dsh3, dsc3, dn3, dy2, dgt2 = norm_modulate_bwd(du3, x2, dx3, norm3_g, sc3, "norm_bwd_3",
                                                        prev=(gt2, y2, 1.0))

    (dya, dyb, dga, dgb, dbm), pairs = merge_backward(dy2, wout, proj, ya, yb, bm_full, "merge_bwd",
                                                      exchange=scatter_stage1([g_wgu3, g_wd3]))
    sums3 = [pair_sum(g_wgu3, pairs[0], "pair_sum_w_gu_3"), pair_sum(g_wd3, pairs[1], "pair_sum_w_down_3")]
    g_wout = rows8(matmul_tn(merged[None], dy2[None], "grad_w_out"))
    dsa, do = out_proj_bwd(dya, dyb, wco, wao, "out_proj_bwd")
    g_wco = matmul_tn(sa[None], dya[None], "grad_w_conv_out").reshape(CONV_W, N_DEV, 128).transpose(1, 0, 2)
    g_wao = matmul_tn(o[None], dyb[None], "grad_w_attn_out").reshape(ATTN_W, N_DEV, 128).transpose(1, 0, 2)
    dcb, dcc, dcx, dconv = short_conv_bwd(dsa, proj, cw_full, "short_conv_bwd")
    (dq, dk, dv), landed3 = stick_breaking_bwd(proj, do, tot, first, "attn_bwd", exchange=scatter_stage2(sums3))
    dproj = jnp.concatenate([dcb, dcc, dcx, dq, dk.astype(BF16), dv.astype(BF16), dga, dgb], axis=1)
    g_wmix = matmul_tn(u2[None], dproj, "grad_w_mix_in", group=(1, 4), b_cols=MIX_BLK).reshape(
        N_DEV, D_MODEL, MIX_BLK)
    mixer_grads = [g_wmix, g_wco, g_wao, g_wout]
    (dx1, dsh2, dsc2, dn2, dy1, dgt1), pairs = norm_modulate_bwd(
        (dproj, wmix), x1, dx2, norm2_g, sc2, "norm_bwd_2", prev=(gt1, y1, 0.5),
        exchange=scatter_stage1(mixer_grads))
    sums_mix = [pair_sum(g, p, f"pair_sum_mixer_{n}") for n, (g, p) in enumerate(zip(mixer_grads, pairs))]

    dgu1, landed_mix = ffn_tokens_bwd(dy1, wd1, gu1, None, "ffn_dact_1", exchange=scatter_stage2(sums_mix[:1]))
    dgu1 = dgu1.reshape(8, s, FF_BLK)
    g_wgu1, landed_small = matmul_tn(dgu1, u1[None], "grad_w_gu_1", group=(4, 1),
                                     exchange=scatter_stage2(sums_mix[1:]))
    g_wgu1 = g_wgu1.reshape(8, FF_BLK, D_MODEL)
    g_wd1, pairs = matmul_tn(act1, dy1[None], "grad_w_down_1", group=(4, 1), exchange=scatter_stage1([g_wgu1]))
    g_wd1 = rows8(g_wd1)
    sum_gu1 = pair_sum(g_wgu1, pairs[0], "pair_sum_w_gu_1")
    du1, (landed_gu1, pair_d1) = matmul_nt_acc(
        dgu1, wgu1, "ffn_du_1", b_dims=NN, exchange=merge_exchanges(scatter_stage2([sum_gu1]), scatter_stage1([g_wd1])))
    sum_d1 = pair_sum(g_wd1, pair_d1, "pair_sum_w_down_1")
    (grad_x, dsh1, dsc1, dn1), landed_d1 = norm_modulate_bwd(du1, x0, dx1, norm1_g, sc1, "norm_bwd_1",
                                                            exchange=scatter_stage2([sum_d1]))

    loss_local = (0.5 / D_MODEL) * jnp.sum(sq)
    stats = jnp.concatenate(
        [v.reshape(-1) for v in (dsh1, dsc1, dgt1, dsh2, dsc2, dgt2, dsh3, dsc3, dgt3,
                                 dn1, dn2, dn3, dfinal, dbm, dconv)]
        + [jnp.broadcast_to(loss_local, (128,))]).reshape(1, -1)
    stats_all = all_gather_rows(stats, "gather_stats")
    n_mod = 9 * D_MODEL
    loss = jnp.sum(stats_all[:, 0, -1])
    dmod_all = stats_all[:, :, :n_mod]
    off = n_mod
    parts = {}
    for key in ("norm1_g", "norm2_g", "norm3_g", "final_g"):
        parts[key] = stats_all[:, :, off:off + D_MODEL]
        off += D_MODEL
    dbm_all = stats_all[:, 0, off:off + 2 * D_MODEL].reshape(N_DEV, 2, D_MODEL)
    off += 2 * D_MODEL
    dcw_all = stats_all[:, 0, off:off + 3 * CONV_W].reshape(N_DEV, 3, CONV_W)
    parts["b_merge"] = lax.dynamic_slice(dbm_all, (0, 0, me * 128), (N_DEV, 2, 128))
    parts["conv_w"] = lax.dynamic_slice(dcw_all, (0, 0, me * 64), (N_DEV, 3, 64))
    dmod_cols = lax.dynamic_slice(dmod_all[:, 0, :], (0, me * n_ada), (N_DEV, n_ada))
    parts["w_ada"] = ada_backward(c_all, dmod_cols, "ada_backward")[None]
    parts["b_ada"] = dmod_all
    parts["ffn2_w_gu"], parts["ffn2_w_down"] = landed3
    parts["w_mix_in"] = landed_mix[0]
    parts["w_conv_out"], parts["w_attn_out"], parts["w_out"] = landed_small
    parts["ffn1_w_gu"] = landed_gu1
    parts["ffn1_w_down"] = landed_d1[0]

    given = dict(w_ada=w_ada, b_ada=b_ada, norm1_g=norm1_g, ffn1_w_gu=ffn1_w_gu, ffn1_w_down=ffn1_w_down,
                 norm2_g=norm2_g, w_mix_in=w_mix_in, b_merge=b_merge, conv_w=conv_w, w_conv_out=w_conv_out,
                 w_attn_out=w_attn_out, w_out=w_out, norm3_g=norm3_g, ffn2_w_gu=ffn2_w_gu,
                 ffn2_w_down=ffn2_w_down, final_g=final_g)
    moments_m = dict(w_ada=m_w_ada, b_ada=m_b_ada, norm1_g=m_norm1_g, ffn1_w_gu=m_ffn1_w_gu,
                     ffn1_w_down=m_ffn1_w_down, norm2_g=m_norm2_g, w_mix_in=m_w_mix_in, b_merge=m_b_merge,
                     conv_w=m_conv_w, w_conv_out=m_w_conv_out, w_attn_out=m_w_attn_out, w_out=m_w_out,
                     norm3_g=m_norm3_g, ffn2_w_gu=m_ffn2_w_gu, ffn2_w_down=m_ffn2_w_down, final_g=m_final_g)
    moments_v = dict(w_ada=v_w_ada, b_ada=v_b_ada, norm1_g=v_norm1_g, ffn1_w_gu=v_ffn1_w_gu,
                     ffn1_w_down=v_ffn1_w_down, norm2_g=v_norm2_g, w_mix_in=v_w_mix_in, b_merge=v_b_merge,
                     conv_w=v_conv_w, w_conv_out=v_w_conv_out, w_attn_out=v_w_attn_out, w_out=v_w_out,
                     norm3_g=v_norm3_g, ffn2_w_gu=v_ffn2_w_gu, ffn2_w_down=v_ffn2_w_down, final_g=v_final_g)
    order = ["w_ada", "b_ada", "norm1_g", "ffn1_w_gu", "ffn1_w_down", "norm2_g", "w_mix_in", "b_merge",
             "conv_w", "w_conv_out", "w_attn_out", "w_out", "norm3_g", "ffn2_w_gu", "ffn2_w_down", "final_g"]
    grads, deltas, new_m, new_v = [], [], [], []
    for key in order:
        turn = flipped if key in ("ffn1_w_gu", "ffn2_w_gu") else (lambda a: a)
        shape = turn(given[key]).shape
        shape2 = (1, shape[0]) if len(shape) == 1 else shape[-2:]
        outs = adamw(turn(given[key]).reshape(shape2), turn(moments_m[key]).reshape(shape2),
                     turn(moments_v[key]).reshape(shape2), parts[key], f"adamw_{key}")
        for dst, val in zip((grads, deltas, new_m, new_v), outs):
            dst.append(turn(val.reshape(shape)))

    return (loss, grad_x[None], *grads, *deltas, *new_m, *new_v)
```

```python
import functools
from typing import Callable, NamedTuple

import jax
import jax.numpy as jnp
from jax import lax
from jax.experimental import pallas as pl
from jax.experimental.pallas import tpu as pltpu

F32 = jnp.float32
BF16 = jnp.bfloat16
MESH = pl.DeviceIdType.MESH
ANY = pl.BlockSpec(memory_space=pl.ANY)

N_DEV = 8
D_MODEL = 1024
D_FF = 2816
FF_BLK = D_FF // 4
N_HEADS = 8
HEAD_DIM = 64
CONV_W = 512
ATTN_W = 512
MIX_W = 3 * CONV_W + 3 * ATTN_W + 2 * D_MODEL
MIX_BLK = MIX_W // N_DEV
EPS = 1e-6
ATTN_SCALE = HEAD_DIM ** -0.5

ADAM_LR = 0.001
ADAM_B1 = 0.9
ADAM_B2 = 0.999
ADAM_EPS = 1e-08
ADAM_WD = 0.01
ADAM_STEP = 10
ADAM_BC1 = 1.0 - ADAM_B1 ** ADAM_STEP
ADAM_BC2 = 1.0 - ADAM_B2 ** ADAM_STEP

VMEM_LIMIT = 56 * 1024 * 1024
ROW_TILE = 512
ACC_TILE = 1024
ELT_TILE = 256
ATT_BLK = 256

NN = (((1,), (0,)), ((), ()))
NT = (((1,), (1,)), ((), ()))
TN = (((0,), (0,)), ((), ()))


def _dot(a, b, dims=NN):
    return lax.dot_general(a, b, dims, preferred_element_type=F32)


def _params(*sem):
    return pltpu.CompilerParams(dimension_semantics=sem, vmem_limit_bytes=VMEM_LIMIT)


def _sigmoid(x):
    return 1.0 / (1.0 + jnp.exp(-x))


def _me():
    x, y, c = lax.axis_index("x"), lax.axis_index("y"), lax.axis_index("c")
    return x, y, c, 4 * x + 2 * y + c


def _peer(k):
    x, y, c, _ = _me()
    px = 1 - x if (k >> 2) & 1 else x
    py = 1 - y if (k >> 1) & 1 else y
    pc = 1 - c if k & 1 else c
    return (px, py, pc), 4 * px + 2 * py + pc


class Exchange(NamedTuple):
    operands: tuple
    out_shapes: tuple
    aliases: dict
    n_remote: int
    n_local: int
    copies: Callable


CHIP_FLIPS = (2, 4, 6)
SIBLING = 1


def _remote(src, dst, send_sems, recv_sems, n, peer):
    return pltpu.make_async_remote_copy(src_ref=src, dst_ref=dst, send_sem=send_sems.at[n], recv_sem=recv_sems.at[n],
                                        device_id=peer, device_id_type=MESH)


def gather_stage1(shards):
    n = len(shards)
    rels = (SIBLING,) + CHIP_FLIPS

    def copies(ins, outs, send_sems, recv_sems, local_sems, rb, lb):
        _, _, _, me = _me()
        cps = []
        for w in range(n):
            cps.append(pltpu.make_async_copy(ins[w], outs[w].at[me], local_sems.at[lb + w]))
            for a, k in enumerate(rels):
                peer, _ = _peer(k)
                cps.append(_remote(ins[w], outs[w].at[me], send_sems, recv_sems, rb + len(rels) * w + a, peer))
        return cps

    shapes = tuple(jax.ShapeDtypeStruct((N_DEV,) + s.shape, s.dtype) for s in shards)
    return Exchange(tuple(shards), shapes, {}, len(rels) * n, n, copies)


def gather_stage2(fulls):
    n = len(fulls)

    def copies(ins, outs, send_sems, recv_sems, local_sems, rb, lb):
        sibling, _ = _peer(SIBLING)
        cps = []
        for w in range(n):
            for a, k in enumerate(CHIP_FLIPS):
                _, blk = _peer(k)
                cps.append(_remote(outs[w].at[blk], outs[w].at[blk], send_sems, recv_sems, rb + 3 * w + a, sibling))
        return cps

    shapes = tuple(jax.ShapeDtypeStruct(f.shape, f.dtype) for f in fulls)
    return Exchange(tuple(fulls), shapes, {w: w for w in range(n)}, 3 * n, 0, copies)


def scatter_stage1(fulls):
    n = len(fulls)

    def copies(ins, outs, send_sems, recv_sems, local_sems, rb, lb):
        _, _, c, _ = _me()
        sibling, _ = _peer(SIBLING)
        cps = []
        for w in range(n):
            for q in range(4):
                cps.append(_remote(ins[w].at[2 * q + (1 - c)], outs[w].at[q], send_sems, recv_sems, rb + 4 * w + q, sibling))
        return cps

    shapes = tuple(jax.ShapeDtypeStruct((4,) + f.shape[1:], f.dtype) for f in fulls)
    return Exchange(tuple(fulls), shapes, {}, 4 * n, 0, copies)


def scatter_stage2(sums):
    n = len(sums)

    def copies(ins, outs, send_sems, recv_sems, local_sems, rb, lb):
        x, y, _, _ = _me()
        mine = 2 * x + y
        cps = []
        for w in range(n):
            cps.append(pltpu.make_async_copy(ins[w].at[mine], outs[w].at[mine], local_sems.at[lb + w]))
            for a, k in enumerate(CHIP_FLIPS):
                peer, _ = _peer(k)
                cps.append(_remote(ins[w].at[2 * peer[0] + peer[1]], outs[w].at[mine], send_sems, recv_sems,
                                   rb + 3 * w + a, peer))
        return cps

    shapes = tuple(jax.ShapeDtypeStruct(s.shape, s.dtype) for s in sums)
    return Exchange(tuple(sums), shapes, {}, 3 * n, n, copies)


def merge_exchanges(a, b):
    na_in, na_out = len(a.operands), len(a.out_shapes)

    def copies(ins, outs, send_sems, recv_sems, local_sems, rb, lb):
        return (a.copies(ins[:na_in], outs[:na_out], send_sems, recv_sems, local_sems, rb, lb)
                + b.copies(ins[na_in:], outs[na_out:], send_sems, recv_sems, local_sems, rb + a.n_remote, lb + a.n_local))

    aliases = dict(a.aliases)
    aliases.update({na_in + i: na_out + o for i, o in b.aliases.items()})
    return Exchange(a.operands + b.operands, a.out_shapes + b.out_shapes, aliases,
                    a.n_remote + b.n_remote, a.n_local + b.n_local, copies)


def _exchange_scratch(ex):
    return [pltpu.SemaphoreType.DMA((ex.n_remote,)), pltpu.SemaphoreType.DMA((ex.n_remote,)),
            pltpu.SemaphoreType.DMA((max(ex.n_local, 1),))]


def _call(body, *, name, grid, in_specs, out_specs, out_shape, operands, scratch_shapes=(), semantics=(),
          exchange=None):
    if exchange is None:
        return pl.pallas_call(
            body, name=name, grid=grid, in_specs=in_specs, out_specs=out_specs, out_shape=out_shape,
            scratch_shapes=list(scratch_shapes), compiler_params=_params(*semantics))(*operands)
    single = not isinstance(out_shape, (list, tuple))
    out_shapes = [out_shape] if single else list(out_shape)
    out_specs_l = [out_specs] if single else list(out_specs)
    n_in, n_out, n_scr = len(operands), len(out_shapes), len(scratch_shapes)
    x_in, x_out = len(exchange.operands), len(exchange.out_shapes)

    def hosted(*refs):
        ins, refs = refs[:n_in], refs[n_in:]
        xin, refs = refs[:x_in], refs[x_in:]
        outs, refs = refs[:n_out], refs[n_out:]
        xout, refs = refs[:x_out], refs[x_out:]
        scr, sems = refs[:n_scr], refs[n_scr:]
        first = functools.reduce(jnp.logical_and, [pl.program_id(a) == 0 for a in range(len(grid))])
        last = functools.reduce(jnp.logical_and, [pl.program_id(a) == g - 1 for a, g in enumerate(grid)])

        @pl.when(first)
        def _():
            for cp in exchange.copies(xin, xout, *sems, 0, 0):
                cp.start()

        body(*ins, *outs, *scr)

        @pl.when(last)
        def _():
            for cp in exchange.copies(xin, xout, *sems, 0, 0):
                cp.wait()

    res = pl.pallas_call(
        hosted, name=name, grid=grid,
        in_specs=list(in_specs) + [ANY] * x_in, out_specs=out_specs_l + [ANY] * x_out,
        out_shape=out_shapes + list(exchange.out_shapes),
        scratch_shapes=list(scratch_shapes) + _exchange_scratch(exchange),
        input_output_aliases={n_in + i: n_out + o for i, o in exchange.aliases.items()},
        compiler_params=_params(*(["arbitrary"] * len(grid))),
    )(*operands, *exchange.operands)
    outs, xouts = res[:n_out], res[n_out:]
    return (outs[0] if single else outs), xouts


def all_gather_rows(v, name):
    r, n = v.shape

    def body(v_ref, out_ref, send_sems, recv_sems):
        _, _, _, me = _me()
        out_ref[me] = v_ref[...]
        copies = []
        for k in range(1, N_DEV):
            peer, _ = _peer(k)
            copies.append(_remote(v_ref, out_ref.at[me], send_sems, recv_sems, k - 1, peer))
        for cp in copies:
            cp.start()
        for cp in copies:
            cp.wait()

    return pl.pallas_call(
        body, name=name,
        out_shape=jax.ShapeDtypeStruct((N_DEV, r, n), v.dtype),
        in_specs=[pl.BlockSpec(memory_space=pltpu.VMEM)],
        out_specs=pl.BlockSpec(memory_space=pltpu.VMEM),
        scratch_shapes=[pltpu.SemaphoreType.DMA((N_DEV - 1,)), pltpu.SemaphoreType.DMA((N_DEV - 1,))],
    )(v)


def pair_sum(full, pair, name):
    _, r, c = full.shape
    t = r
    core = lax.axis_index("c").astype(jnp.int32).reshape(1)

    def body(core_ref, f_ref, p_ref, o_ref):
        o_ref[...] = (f_ref[...].astype(F32) + p_ref[...].astype(F32)).astype(BF16)

    return pl.pallas_call(
        body, name=name,
        grid_spec=pltpu.PrefetchScalarGridSpec(
            num_scalar_prefetch=1, grid=(4, r // t),
            in_specs=[pl.BlockSpec((None, None, t, c), lambda q, i, core_ref: (q, core_ref[0], i, 0)),
                      pl.BlockSpec((None, t, c), lambda q, i, core_ref: (q, i, 0))],
            out_specs=pl.BlockSpec((None, t, c), lambda q, i, core_ref: (q, i, 0))),
        out_shape=jax.ShapeDtypeStruct((4, r, c), BF16),
        compiler_params=_params("parallel", "parallel"),
    )(core, full.reshape(4, 2, r, c), pair)


def prologue(small_in, w_ada, b_cols, shards, name):
    ex = gather_stage1(shards)
    n_sh = len(shards)
    n_small = small_in.shape[1]
    cols = w_ada.shape[1]

    def body(*refs):
        small_ref, w_ref, b_ref = refs[:3]
        shard_refs = refs[3:3 + n_sh]
        small_out, mod_out = refs[3 + n_sh:5 + n_sh]
        fulls = refs[5 + n_sh:5 + 2 * n_sh]
        part_ref, send1, recv1, send2, recv2, wsend, wrecv, wlocal, fsend, frecv = refs[5 + 2 * n_sh:]
        _, _, _, me = _me()
        def start_gather(src_ref, dst_ref, send_sems, recv_sems):
            cps = [_remote(src_ref, dst_ref.at[me], send_sems, recv_sems, k - 1, _peer(k)[0]) for k in range(1, N_DEV)]
            for cp in cps:
                cp.start()
            return cps

        small_out[me] = small_ref[...]
        first = start_gather(small_ref, small_out, send1, recv1)
        big = ex.copies(shard_refs, fulls, wsend, wrecv, wlocal, 0, 0)
        for cp in big:
            cp.start()
        for cp in first:
            cp.wait()
        c_all = jnp.concatenate([small_out[d][:, :D_MODEL] for d in range(N_DEV)], axis=0)
        act = c_all * _sigmoid(c_all)
        part_ref[...] = jnp.dot(act, w_ref[...], precision=lax.Precision.HIGHEST,
                                preferred_element_type=F32) + b_ref[...]
        mod_out[me] = part_ref[...]
        for cp in start_gather(part_ref, mod_out, send2, recv2):
            cp.wait()
        per = 2 + len(CHIP_FLIPS)
        sibling, _ = _peer(SIBLING)
        onward = []
        for w in range(n_sh):
            for a, k in enumerate(CHIP_FLIPS):
                _, blk = _peer(k)
                big[per * w + 2 + a].wait_recv()
                cp = _remote(fulls[w].at[blk], fulls[w].at[blk], fsend, frecv, len(CHIP_FLIPS) * w + a, sibling)
                cp.start()
                onward.append(cp)
        for w in range(n_sh):
            big[per * w].wait()
            big[per * w + 1].wait()
            for a in range(len(CHIP_FLIPS)):
                big[per * w + 2 + a].wait_send()
        for cp in onward:
            cp.wait()

    vmem = pl.BlockSpec(memory_space=pltpu.VMEM)
    sems = pltpu.SemaphoreType.DMA((N_DEV - 1,))
    res = pl.pallas_call(
        body, name=name,
        out_shape=[jax.ShapeDtypeStruct((N_DEV, 1, n_small), F32), jax.ShapeDtypeStruct((N_DEV, N_DEV, cols), F32)]
        + list(ex.out_shapes),
        in_specs=[vmem, vmem, vmem] + [ANY] * n_sh, out_specs=[vmem, vmem] + [ANY] * n_sh,
        scratch_shapes=[pltpu.VMEM((N_DEV, cols), F32), sems, sems, sems, sems] + _exchange_scratch(ex)
        + [pltpu.SemaphoreType.DMA((len(CHIP_FLIPS) * n_sh,)), pltpu.SemaphoreType.DMA((len(CHIP_FLIPS) * n_sh,))],
        compiler_params=pltpu.CompilerParams(vmem_limit_bytes=VMEM_LIMIT),
    )(small_in, w_ada, b_cols, *shards)
    return res[0], res[1], res[2:]


def ada_backward(c_all, dmod_cols, name):
    n = dmod_cols.shape[1]

    def body(c_ref, d_ref, o_ref):
        c = c_ref[...]
        act = c * _sigmoid(c)
        o_ref[...] = lax.dot_general(act, d_ref[...], TN, precision=lax.Precision.HIGHEST,
                                     preferred_element_type=F32)

    return pl.pallas_call(
        body, name=name, out_shape=jax.ShapeDtypeStruct((D_MODEL, n), F32),
        compiler_params=pltpu.CompilerParams(vmem_limit_bytes=VMEM_LIMIT),
    )(c_all, dmod_cols)


def _row_spec(t, width=D_MODEL):
    return pl.BlockSpec((t, width), lambda i: (i, 0))


def _vec_spec(rows=1, width=D_MODEL):
    return pl.BlockSpec((rows, width), lambda i: (0, 0))


def _resident(shape):
    return pl.BlockSpec(shape, lambda i: (0,) * len(shape), pipeline_mode=pl.Buffered(1))


def _norm_modulate(x_ref, g_ref, shift_ref, scale_ref):
    xv = x_ref[...]
    r = lax.rsqrt(jnp.mean(xv * xv, axis=-1, keepdims=True) + EPS)
    a = (xv * r) * g_ref[...]
    return (a * (1.0 + scale_ref[...]) + shift_ref[...]).astype(BF16)


def norm_modulate_bwd(du, x, dx_out, g, scale, name, prev=None, exchange=None):
    s = x.shape[0]
    factors = isinstance(du, tuple)
    t = min(ROW_TILE if factors else ELT_TILE, s)
    has_prev = prev is not None

    def body(*refs):
        if factors:
            a_ref, b_ref = refs[:2]
            refs = refs[1:]
            nk, _, n = b_ref.shape
        du_ref, x_ref, dxo_ref, g_ref, sc_ref = refs[:5]
        refs = refs[5:]
        if has_prev:
            gt_ref, y_ref = refs[:2]
            refs = refs[2:]
        dx_ref, dsh_ref, dsc_ref, dg_ref = refs[:4]

        @pl.when(pl.program_id(0) == 0)
        def _():
            dsh_ref[...] = jnp.zeros_like(dsh_ref)
            dsc_ref[...] = jnp.zeros_like(dsc_ref)
            dg_ref[...] = jnp.zeros_like(dg_ref)
            if has_prev:
                refs[5][...] = jnp.zeros_like(refs[5])

        xv = x_ref[...]
        if factors:
            duv = _dot(a_ref[:, 0:n], b_ref[0], NT)
            for k in range(1, nk):
                duv = duv + _dot(a_ref[:, k * n:(k + 1) * n], b_ref[k], NT)
        else:
            duv = du_ref[...]
        gv = g_ref[...]
        r = lax.rsqrt(jnp.mean(xv * xv, axis=-1, keepdims=True) + EPS)
        nrm = xv * r
        a = nrm * gv
        dsh_ref[...] += jnp.sum(duv, axis=0, keepdims=True)
        dsc_ref[...] += jnp.sum(duv * a, axis=0, keepdims=True)
        da = duv * (1.0 + sc_ref[...])
        dg_ref[...] += jnp.sum(da * nrm, axis=0, keepdims=True)
        dn = da * gv
        dx = dxo_ref[...] + r * (dn - nrm * jnp.mean(dn * nrm, axis=-1, keepdims=True))
        dx_ref[...] = dx
        if has_prev:
            coef = prev[2]
            refs[4][...] = (coef * gt_ref[...] * dx).astype(BF16)
            refs[5][...] += coef * jnp.sum(dx * y_ref[...].astype(F32), axis=0, keepdims=True)

    vec = jax.ShapeDtypeStruct((1, D_MODEL), F32)
    if factors:
        operands = [du[0], du[1], x, dx_out, g, scale]
        in_specs = [_row_spec(t, du[0].shape[1]), _resident(du[1].shape)]
    else:
        operands = [du, x, dx_out, g, scale]
        in_specs = [_row_spec(t)]
    in_specs += [_row_spec(t), _row_spec(t), _vec_spec(), _vec_spec()]
    out_specs = [_row_spec(t), _vec_spec(), _vec_spec(), _vec_spec()]
    out_shape = [jax.ShapeDtypeStruct((s, D_MODEL), F32), vec, vec, vec]
    if has_prev:
        operands += [prev[0], prev[1]]
        in_specs += [_vec_spec(), _row_spec(t)]
        out_specs += [_row_spec(t), _vec_spec()]
        out_shape += [jax.ShapeDtypeStruct((s, D_MODEL), BF16), vec]
    return _call(body, name=name, grid=(s // t,), in_specs=in_specs, out_specs=out_specs, out_shape=out_shape,
                 operands=operands, semantics=("arbitrary",), exchange=exchange)


def ffn_up(x, norm_g, shift, scale, w_gu_t, name, exchange=None):
    s = x.shape[0]
    t = min(ROW_TILE, s)

    def body(x_ref, g_ref, sh_ref, sc_ref, w_ref, u_ref, gu_ref, act_ref):
        uv = _norm_modulate(x_ref, g_ref, sh_ref, sc_ref)
        u_ref[...] = uv
        for j in range(4):
            g = _dot(uv, w_ref[j], NT)
            up = _dot(uv, w_ref[j + 4], NT)
            gu_ref[0, j] = g.astype(BF16)
            gu_ref[1, j] = up.astype(BF16)
            act_ref[j] = (g * _sigmoid(g) * up).astype(BF16)

    return _call(
        body, name=name, grid=(s // t,),
        in_specs=[_row_spec(t), _vec_spec(), _vec_spec(), _vec_spec(), _resident(w_gu_t.shape)],
        out_specs=[_row_spec(t), pl.BlockSpec((2, 4, t, FF_BLK), lambda i: (0, 0, i, 0)),
                   pl.BlockSpec((4, t, FF_BLK), lambda i: (0, i, 0))],
        out_shape=[jax.ShapeDtypeStruct((s, D_MODEL), BF16), jax.ShapeDtypeStruct((2, 4, s, FF_BLK), BF16),
                   jax.ShapeDtypeStruct((4, s, FF_BLK), BF16)],
        operands=(x, norm_g, shift, scale, w_gu_t), semantics=("parallel",), exchange=exchange)


def residual_matmul(a, b, x, gate, coef, name, exchange=None):
    nk, s, kb = a.shape
    t = min(ROW_TILE, s)

    def body(a_ref, b_ref, x_ref, gt_ref, xo_ref, y_ref):
        y = _dot(a_ref[0], b_ref[0])
        for k in range(1, nk):
            y = y + _dot(a_ref[k], b_ref[k])
        y_ref[...] = y.astype(BF16)
        xo_ref[...] = x_ref[...] + coef * gt_ref[...] * y

    return _call(
        body, name=name, grid=(s // t,),
        in_specs=[pl.BlockSpec((nk, t, kb), lambda i: (0, i, 0)),
                  pl.BlockSpec((nk, kb, D_MODEL), lambda i: (0, 0, 0)),
                  _row_spec(t), _vec_spec()],
        out_specs=[_row_spec(t), _row_spec(t)],
        out_shape=[jax.ShapeDtypeStruct((s, D_MODEL), F32), jax.ShapeDtypeStruct((s, D_MODEL), BF16)],
        operands=(a, b, x, gate), semantics=("parallel",), exchange=exchange)


def residual_matmul_loss(a, b, x, gate, coef, target, final_g, name):
    nk, s, kb = a.shape
    t = min(ROW_TILE, s)

    def body(a_ref, b_ref, x_ref, gt_ref, t_ref, fg_ref, dx_ref, dy_ref, dgt_ref, dfg_ref, sq_ref):
        @pl.when(pl.program_id(0) == 0)
        def _():
            dgt_ref[...] = jnp.zeros_like(dgt_ref)
            dfg_ref[...] = jnp.zeros_like(dfg_ref)
            sq_ref[...] = jnp.zeros_like(sq_ref)

        y = _dot(a_ref[0], b_ref[0])
        for k in range(1, nk):
            y = y + _dot(a_ref[k], b_ref[k])
        gt = gt_ref[...]
        fg = fg_ref[...]
        xv = x_ref[...] + coef * gt * y
        r = lax.rsqrt(jnp.mean(xv * xv, axis=-1, keepdims=True) + EPS)
        nrm = xv * r
        err = nrm * fg - t_ref[...]
        sq_ref[...] += jnp.sum(err * err, axis=0, keepdims=True)
        dout = err * (1.0 / D_MODEL)
        dfg_ref[...] += jnp.sum(dout * nrm, axis=0, keepdims=True)
        dn = dout * fg
        dx = r * (dn - nrm * jnp.mean(dn * nrm, axis=-1, keepdims=True))
        dx_ref[...] = dx
        dy_ref[...] = (coef * gt * dx).astype(BF16)
        dgt_ref[...] += coef * jnp.sum(dx * y, axis=0, keepdims=True)

    vec = jax.ShapeDtypeStruct((1, D_MODEL), F32)
    return pl.pallas_call(
        body, name=name, grid=(s // t,),
        in_specs=[pl.BlockSpec((nk, t, kb), lambda i: (0, i, 0)), _resident(b.shape),
                  _row_spec(t), _vec_spec(), _row_spec(t), _vec_spec()],
        out_specs=[_row_spec(t), _row_spec(t), _vec_spec(), _vec_spec(), _vec_spec()],
        out_shape=[jax.ShapeDtypeStruct((s, D_MODEL), F32), jax.ShapeDtypeStruct((s, D_MODEL), BF16), vec, vec, vec],
        compiler_params=_params("arbitrary"),
    )(a, b, x, gate, target, final_g)


def ffn_tokens_bwd(dy, w_down, gu, w_gu_t, name, exchange=None):
    s = dy.shape[0]
    t = min(ROW_TILE, s)
    with_du = w_gu_t is not None

    def body(*refs):
        if with_du:
            dy_ref, wd_ref, gu_ref, wgu_ref, dgu_ref, du_ref = refs
        else:
            dy_ref, wd_ref, gu_ref, dgu_ref = refs
        dyv = dy_ref[...]
        du = None
        for j in range(4):
            dact = _dot(dyv, wd_ref[j], NT)
            g = gu_ref[0, j].astype(F32)
            up = gu_ref[1, j].astype(F32)
            sg = _sigmoid(g)
            slopes = (up * sg * (1.0 + g * (1.0 - sg)), g * sg)
            for half in range(2):
                d = (dact * slopes[half]).astype(BF16)
                dgu_ref[half, j] = d
                if with_du:
                    part = _dot(d, wgu_ref[4 * half + j])
                    du = part if du is None else du + part
        if with_du:
            du_ref[...] = du

    blocks = pl.BlockSpec((2, 4, t, FF_BLK), lambda i: (0, 0, i, 0))
    dgu_shape = jax.ShapeDtypeStruct((2, 4, s, FF_BLK), BF16)
    if with_du:
        return _call(
            body, name=name, grid=(s // t,),
            in_specs=[_row_spec(t), _resident(w_down.shape), blocks, _resident(w_gu_t.shape)],
            out_specs=[blocks, _row_spec(t)],
            out_shape=[dgu_shape, jax.ShapeDtypeStruct((s, D_MODEL), F32)],
            operands=(dy, w_down, gu, w_gu_t), semantics=("parallel",), exchange=exchange)
    return _call(
        body, name=name, grid=(s // t,),
        in_specs=[_row_spec(t), _resident(w_down.shape), blocks], out_specs=blocks, out_shape=dgu_shape,
        operands=(dy, w_down, gu), semantics=("parallel",), exchange=exchange)


def matmul_nt_acc(a, b, name, b_dims=NT, exchange=None):
    nk = b.shape[0]
    d, n = (b.shape[1], b.shape[2]) if b_dims == NT else (b.shape[2], b.shape[1])
    s = a.shape[-2]
    t = min(ROW_TILE, s)
    by_columns = a.ndim == 2

    def body(a_ref, b_ref, o_ref):
        def a_blk(k):
            return a_ref[:, k * n:(k + 1) * n] if by_columns else a_ref[k]

        acc = _dot(a_blk(0), b_ref[0], b_dims)
        for k in range(1, nk):
            acc = acc + _dot(a_blk(k), b_ref[k], b_dims)
        o_ref[...] = acc

    a_spec = _row_spec(t, nk * n) if by_columns else pl.BlockSpec((nk, t, n), lambda i: (0, i, 0))
    return _call(
        body, name=name, grid=(s // t,),
        in_specs=[a_spec, pl.BlockSpec(b.shape, lambda i: (0, 0, 0))],
        out_specs=pl.BlockSpec((t, d), lambda i: (i, 0)),
        out_shape=jax.ShapeDtypeStruct((s, d), F32),
        operands=(a, b), semantics=("parallel",), exchange=exchange)


def matmul_tn(a, b, name, group=(1, 1), b_cols=None, exchange=None):
    ja, s, m = a.shape
    by_columns = b.ndim == 2
    jb, n = (b.shape[1] // b_cols, b_cols) if by_columns else (b.shape[0], b.shape[2])
    ga, gb = group
    t = min(ACC_TILE, s)
    nk = s // t

    def body(a_ref, b_ref, o_ref, acc_ref):
        k = pl.program_id(2)

        @pl.when(k == 0)
        def _():
            acc_ref[...] = jnp.zeros_like(acc_ref)

        for p in range(ga):
            for q in range(gb):
                b_blk = b_ref[:, q * n:(q + 1) * n] if by_columns else b_ref[q]
                acc_ref[p, q] += _dot(a_ref[p], b_blk, TN)

        @pl.when(k == nk - 1)
        def _():
            o_ref[...] = acc_ref[...].astype(BF16)

    return _call(
        body, name=name, grid=(ja // ga, jb // gb, nk),
        in_specs=[pl.BlockSpec((ga, t, m), lambda p, q, k: (p, k, 0)),
                  pl.BlockSpec((t, gb * n), lambda p, q, k: (k, q)) if by_columns
                  else pl.BlockSpec((gb, t, n), lambda p, q, k: (q, k, 0))],
        out_specs=pl.BlockSpec((ga, gb, m, n), lambda p, q, k: (p, q, 0, 0)),
        out_shape=jax.ShapeDtypeStruct((ja, jb, m, n), BF16),
        operands=(a, b), scratch_shapes=[pltpu.VMEM((ga, gb, m, n), F32)],
        semantics=("parallel", "parallel", "arbitrary"), exchange=exchange)


def mix_in_proj(x, norm_g, shift, scale, w_mix, name, exchange=None):
    s = x.shape[0]
    t = min(ROW_TILE, s)

    def body(x_ref, g_ref, sh_ref, sc_ref, w_ref, u_ref, o_ref):
        uv = _norm_modulate(x_ref, g_ref, sh_ref, sc_ref)
        u_ref[...] = uv
        for j in range(N_DEV):
            o_ref[:, j * MIX_BLK:(j + 1) * MIX_BLK] = _dot(uv, w_ref[j]).astype(BF16)

    return _call(
        body, name=name, grid=(s // t,),
        in_specs=[_row_spec(t), _vec_spec(), _vec_spec(), _vec_spec(), _resident(w_mix.shape)],
        out_specs=[_row_spec(t), _row_spec(t, MIX_W)],
        out_shape=[jax.ShapeDtypeStruct((s, D_MODEL), BF16), jax.ShapeDtypeStruct((s, MIX_W), BF16)],
        operands=(x, norm_g, shift, scale, w_mix), semantics=("parallel",), exchange=exchange)


def _conv_taps(cc_ref, cx_ref, s):
    v = cc_ref[...].astype(F32) * cx_ref[...].astype(F32)
    tok = lax.broadcasted_iota(jnp.int32, v.shape, 0)
    v1 = jnp.where(tok >= 1, pltpu.roll(v, 1, 0), 0.0)
    v2 = jnp.where(tok >= 2, pltpu.roll(v, 2, 0), 0.0)
    return v, v1, v2, tok


def _proj_cols(s, first):
    return pl.BlockSpec((s, 128), lambda j: (0, first + j))


def short_conv(proj, conv_w, name):
    s = proj.shape[0]

    def body(cb_ref, cc_ref, cx_ref, w_ref, o_ref):
        v, v1, v2, _ = _conv_taps(cc_ref, cx_ref, s)
        y = w_ref[0:1, :] * v2 + w_ref[1:2, :] * v1 + w_ref[2:3, :] * v
        o_ref[...] = (cb_ref[...].astype(F32) * y).astype(BF16)

    return pl.pallas_call(
        body, name=name, grid=(CONV_W // 128,),
        in_specs=[_proj_cols(s, 0), _proj_cols(s, 4), _proj_cols(s, 8),
                  pl.BlockSpec((3, 128), lambda j: (0, j))],
        out_specs=pl.BlockSpec((s, 128), lambda j: (0, j)),
        out_shape=jax.ShapeDtypeStruct((s, CONV_W), BF16),
        compiler_params=_params("parallel"),
    )(proj, proj, proj, conv_w)


def short_conv_bwd(dsa, proj, conv_w, name):
    s = proj.shape[0]

    def body(dsa_ref, cb_ref, cc_ref, cx_ref, w_ref, dcb_ref, dcc_ref, dcx_ref, dw_ref):
        v, v1, v2, tok = _conv_taps(cc_ref, cx_ref, s)
        w0, w1, w2 = w_ref[0:1, :], w_ref[1:2, :], w_ref[2:3, :]
        y = w0 * v2 + w1 * v1 + w2 * v
        dsa_v = dsa_ref[...].astype(F32)
        dcb_ref[...] = (dsa_v * y).astype(BF16)
        dy = dsa_v * cb_ref[...].astype(F32)
        dw_ref[0:1, :] = jnp.sum(dy * v2, axis=0, keepdims=True)
        dw_ref[1:2, :] = jnp.sum(dy * v1, axis=0, keepdims=True)
        dw_ref[2:3, :] = jnp.sum(dy * v, axis=0, keepdims=True)
        dy1 = jnp.where(tok < s - 1, pltpu.roll(dy, s - 1, 0), 0.0)
        dy2 = jnp.where(tok < s - 2, pltpu.roll(dy, s - 2, 0), 0.0)
        dv = w2 * dy + w1 * dy1 + w0 * dy2
        dcc_ref[...] = (dv * cx_ref[...].astype(F32)).astype(BF16)
        dcx_ref[...] = (dv * cc_ref[...].astype(F32)).astype(BF16)

    col = pl.BlockSpec((s, 128), lambda j: (0, j))
    act = jax.ShapeDtypeStruct((s, CONV_W), BF16)
    return pl.pallas_call(
        body, name=name, grid=(CONV_W // 128,),
        in_specs=[col, _proj_cols(s, 0), _proj_cols(s, 4), _proj_cols(s, 8),
                  pl.BlockSpec((3, 128), lambda j: (0, j))],
        out_specs=[col, col, col, pl.BlockSpec((3, 128), lambda j: (0, j))],
        out_shape=[act, act, act, jax.ShapeDtypeStruct((3, CONV_W), F32)],
        compiler_params=_params("parallel"),
    )(dsa, proj, proj, proj, conv_w)


def _gate_specs(t):
    return [pl.BlockSpec((t, D_MODEL), lambda i: (i, 3)), pl.BlockSpec((t, D_MODEL), lambda i: (i, 4))]


def merge_forward(sa, o, proj, w_co, w_ao, b_merge, w_out, x, gate, name, exchange=None):
    s = sa.shape[0]
    t = min(ROW_TILE, s)

    def body(sa_ref, o_ref, ga_ref, gb_ref, wco_ref, wao_ref, bm_ref, wout_ref, x_ref, gt_ref,
             mg_ref, ya_ref, yb_ref, y_ref, xo_ref):
        ya = _dot(sa_ref[...], wco_ref[...])
        yb = _dot(o_ref[...], wao_ref[...])
        sga = _sigmoid(ga_ref[...].astype(F32) + bm_ref[0:1, :])
        sgb = _sigmoid(gb_ref[...].astype(F32) + bm_ref[1:2, :])
        merged = (sga * ya + sgb * yb).astype(BF16)
        mg_ref[...] = merged
        ya_ref[...] = ya.astype(BF16)
        yb_ref[...] = yb.astype(BF16)
        y = _dot(merged, wout_ref[...])
        y_ref[...] = y.astype(BF16)
        xo_ref[...] = x_ref[...] + gt_ref[...] * y

    act = jax.ShapeDtypeStruct((s, D_MODEL), BF16)
    return _call(
        body, name=name, grid=(s // t,),
        in_specs=[_row_spec(t, CONV_W), _row_spec(t, ATTN_W)] + _gate_specs(t)
        + [_vec_spec(CONV_W), _vec_spec(ATTN_W), _vec_spec(2), _vec_spec(D_MODEL), _row_spec(t), _vec_spec()],
        out_specs=[_row_spec(t)] * 5, out_shape=[act, act, act, act, jax.ShapeDtypeStruct((s, D_MODEL), F32)],
        operands=(sa, o, proj, proj, w_co, w_ao, b_merge, w_out, x, gate), semantics=("parallel",),
        exchange=exchange)


def merge_backward(dy, w_out, proj, ya, yb, b_merge, name, exchange=None):
    s = dy.shape[0]
    t = min(ROW_TILE, s)

    def body(dy_ref, w_ref, ga_ref, gb_ref, ya_ref, yb_ref, bm_ref,
             dya_ref, dyb_ref, dga_ref, dgb_ref, dbm_ref):
        @pl.when(pl.program_id(0) == 0)
        def _():
            dbm_ref[...] = jnp.zeros_like(dbm_ref)

        dmg = _dot(dy_ref[...], w_ref[...], NT)
        sga = _sigmoid(ga_ref[...].astype(F32) + bm_ref[0:1, :])
        sgb = _sigmoid(gb_ref[...].astype(F32) + bm_ref[1:2, :])
        dya_ref[...] = (dmg * sga).astype(BF16)
        dyb_ref[...] = (dmg * sgb).astype(BF16)
        dga = dmg * ya_ref[...].astype(F32) * sga * (1.0 - sga)
        dgb = dmg * yb_ref[...].astype(F32) * sgb * (1.0 - sgb)
        dga_ref[...] = dga.astype(BF16)
        dgb_ref[...] = dgb.astype(BF16)
        dbm_ref[0:1, :] += jnp.sum(dga, axis=0, keepdims=True)
        dbm_ref[1:2, :] += jnp.sum(dgb, axis=0, keepdims=True)

    act = jax.ShapeDtypeStruct((s, D_MODEL), BF16)
    return _call(
        body, name=name, grid=(s // t,),
        in_specs=[_row_spec(t), _vec_spec(D_MODEL)] + _gate_specs(t)
        + [_row_spec(t), _row_spec(t), _vec_spec(2)],
        out_specs=[_row_spec(t)] * 4 + [_vec_spec(2)],
        out_shape=[act] * 4 + [jax.ShapeDtypeStruct((2, D_MODEL), F32)],
        operands=(dy, w_out, proj, proj, ya, yb, b_merge), semantics=("arbitrary",), exchange=exchange)


def out_proj_bwd(dya, dyb, w_co, w_ao, name):
    s = dya.shape[0]
    t = min(ROW_TILE, s)

    def body(dya_ref, dyb_ref, wco_ref, wao_ref, dsa_ref, do_ref):
        dsa_ref[...] = _dot(dya_ref[...], wco_ref[...], NT).astype(BF16)
        do_ref[...] = _dot(dyb_ref[...], wao_ref[...], NT).astype(BF16)

    return pl.pallas_call(
        body, name=name, grid=(s // t,),
        in_specs=[_row_spec(t), _row_spec(t), _vec_spec(CONV_W), _vec_spec(ATTN_W)],
        out_specs=[_row_spec(t, CONV_W), _row_spec(t, ATTN_W)],
        out_shape=[jax.ShapeDtypeStruct((s, CONV_W), BF16), jax.ShapeDtypeStruct((s, ATTN_W), BF16)],
        compiler_params=_params("parallel"),
    )(dya, dyb, w_co, w_ao)


ATT_HEADS = 4
ATT_LANES = ATT_HEADS * HEAD_DIM


def _softplus(z):
    return jnp.maximum(z, 0.0) + jnp.log(1.0 + jnp.exp(-jnp.abs(z)))


def _head_masks(rows):
    lane = lax.broadcasted_iota(jnp.int32, (rows, ATT_LANES), 1)
    return [(lane >= h * HEAD_DIM) & (lane < (h + 1) * HEAD_DIM) for h in range(ATT_HEADS)]


def _per_head(x, masks):
    return [jnp.where(m, x, jnp.zeros_like(x)) for m in masks]


def _att_specs(s, blk):
    first = {"q": 3 * CONV_W // ATT_LANES, "k": (3 * CONV_W + ATTN_W) // ATT_LANES,
             "v": (3 * CONV_W + 2 * ATTN_W) // ATT_LANES}
    return [pl.BlockSpec((blk, ATT_LANES), lambda h, i: (i, first["q"] + h)),
            pl.BlockSpec((s, ATT_LANES), lambda h, i: (0, first["k"] + h)),
            pl.BlockSpec((s, ATT_LANES), lambda h, i: (0, first["v"] + h))]


def stick_breaking_fwd(proj, name, exchange=None):
    s = proj.shape[0]
    blk = ATT_BLK
    nq = s // blk

    def body(q_ref, k_ref, v_ref, o_ref, tot_ref):
        i = pl.program_id(1)
        row = lax.broadcasted_iota(jnp.int32, (blk, blk), 0)
        col = lax.broadcasted_iota(jnp.int32, (blk, blk), 1)
        tri = (row >= col).astype(BF16)
        causal = col < row
        masks = _head_masks(blk)
        qs = _per_head(q_ref[...] * ATTN_SCALE, masks)

        def step(j, carry, diagonal):
            laters, acc = carry
            rows = pl.ds(pl.multiple_of(j * blk, blk), blk)
            kb = k_ref[rows, :]
            probs, new_laters = [], []
            for h in range(ATT_HEADS):
                z = _dot(qs[h], kb, NT)
                sp = _softplus(z)
                if diagonal:
                    sp = jnp.where(causal, sp, 0.0)
                a = jnp.exp(z - (_dot(sp.astype(BF16), tri) + laters[h]))
                if diagonal:
                    a = jnp.where(causal, a, 0.0)
                probs.append(a.astype(BF16))
                new_laters.append(laters[h] + jnp.sum(sp, axis=1, keepdims=True))
            v_heads = jnp.concatenate(_per_head(v_ref[rows, :], masks), axis=0)
            acc = acc + _dot(jnp.concatenate(probs, axis=1), v_heads)
            return tuple(new_laters), acc

        carry = (tuple(jnp.zeros((blk, 1), F32) for _ in range(ATT_HEADS)), jnp.zeros((blk, ATT_LANES), F32))
        carry = step(i, carry, True)
        laters, acc = lax.fori_loop(0, i, lambda n, c: step(i - 1 - n, c, False), carry)
        o_ref[...] = acc.astype(BF16)
        tot = jnp.zeros((blk, ATT_LANES), F32)
        for h in range(ATT_HEADS):
            tot = jnp.where(masks[h], laters[h], tot)
        tot_ref[...] = tot

    out_spec = pl.BlockSpec((blk, ATT_LANES), lambda h, i: (i, h))
    return _call(
        body, name=name, grid=(N_HEADS // ATT_HEADS, nq),
        in_specs=_att_specs(s, blk), out_specs=[out_spec, out_spec],
        out_shape=[jax.ShapeDtypeStruct((s, ATTN_W), BF16), jax.ShapeDtypeStruct((s, ATTN_W), F32)],
        operands=(proj, proj, proj), semantics=("parallel", "arbitrary"), exchange=exchange)


def stick_breaking_bwd(proj, do, tot, name, exchange=None):
    s = proj.shape[0]
    blk = ATT_BLK
    nq = s // blk

    def body(q_ref, k_ref, v_ref, do_ref, tot_ref, dq_ref, dk_ref, dv_ref):
        i = pl.program_id(1)

        @pl.when(i == 0)
        def _():
            dk_ref[...] = jnp.zeros_like(dk_ref)
            dv_ref[...] = jnp.zeros_like(dv_ref)

        row = lax.broadcasted_iota(jnp.int32, (blk, blk), 0)
        col = lax.broadcasted_iota(jnp.int32, (blk, blk), 1)
        before = (row < col).astype(BF16)
        upto = (row <= col).astype(BF16)
        causal = col < row
        masks = _head_masks(blk)
        qs = _per_head(q_ref[...] * ATTN_SCALE, masks)
        dos = _per_head(do_ref[...], masks)
        q_heads = jnp.concatenate(qs, axis=0)
        do_heads = jnp.concatenate(dos, axis=0)
        tot_all = tot_ref[...]
        totals = [jnp.max(jnp.where(m, tot_all, 0.0), axis=1, keepdims=True) for m in masks]

        def step(j, carry, diagonal):
            earliers, g_sums, dq = carry
            rows = pl.ds(pl.multiple_of(j * blk, blk), blk)
            kb = k_ref[rows, :]
            vb = v_ref[rows, :]
            probs, dzs, new_earliers, new_g_sums = [], [], [], []
            for h in range(ATT_HEADS):
                z = _dot(qs[h], kb, NT)
                sp = _softplus(z)
                if diagonal:
                    sp = jnp.where(causal, sp, 0.0)
                c = (totals[h] - earliers[h]) - _dot(sp.astype(BF16), before)
                a = jnp.exp(z - c)
                if diagonal:
                    a = jnp.where(causal, a, 0.0)
                g = a * _dot(dos[h], vb, NT)
                f = g_sums[h] + _dot(g.astype(BF16), upto)
                dz = g - jnp.exp(z - sp) * f
                if diagonal:
                    dz = jnp.where(causal, dz, 0.0)
                probs.append(a.astype(BF16))
                dzs.append(dz.astype(BF16))
                new_earliers.append(earliers[h] + jnp.sum(sp, axis=1, keepdims=True))
                new_g_sums.append(g_sums[h] + jnp.sum(g, axis=1, keepdims=True))
            k_heads = jnp.concatenate(_per_head(kb, masks), axis=0)
            dq = dq + _dot(jnp.concatenate(dzs, axis=1), k_heads)
            dk_ref[rows, :] += _dot(jnp.concatenate(dzs, axis=0), q_heads, TN)
            dv_ref[rows, :] += _dot(jnp.concatenate(probs, axis=0), do_heads, TN)
            return tuple(new_earliers), tuple(new_g_sums), dq

        zeros = tuple(jnp.zeros((blk, 1), F32) for _ in range(ATT_HEADS))
        carry = (zeros, zeros, jnp.zeros((blk, ATT_LANES), F32))
        carry = lax.fori_loop(0, i, lambda j, c: step(j, c, False), carry)
        dq = step(i, carry, True)[2]
        dq_ref[...] = (dq * ATTN_SCALE).astype(BF16)

    blk_spec = pl.BlockSpec((blk, ATT_LANES), lambda h, i: (i, h))
    full_spec = pl.BlockSpec((s, ATT_LANES), lambda h, i: (0, h))
    return _call(
        body, name=name, grid=(N_HEADS // ATT_HEADS, nq),
        in_specs=_att_specs(s, blk) + [blk_spec, blk_spec],
        out_specs=[blk_spec, full_spec, full_spec],
        out_shape=[jax.ShapeDtypeStruct((s, ATTN_W), BF16), jax.ShapeDtypeStruct((s, ATTN_W), F32),
                   jax.ShapeDtypeStruct((s, ATTN_W), F32)],
        operands=(proj, proj, proj, do, tot), semantics=("parallel", "arbitrary"), exchange=exchange)


def adamw(w, m, v, parts, name):
    r, c = w.shape
    p = parts.shape[0]
    t = r
    for cand in (256, 176):
        if r % cand == 0 and r > cand:
            t = cand
            break

    def body(w_ref, m_ref, v_ref, p_ref, g_ref, d_ref, mo_ref, vo_ref):
        g = p_ref[0].astype(F32)
        for n in range(1, p):
            g = g + p_ref[n].astype(F32)
        m_new = ADAM_B1 * m_ref[...] + (1.0 - ADAM_B1) * g
        v_new = ADAM_B2 * v_ref[...] + (1.0 - ADAM_B2) * (g * g)
        m_hat = m_new / ADAM_BC1
        v_hat = v_new / ADAM_BC2
        g_ref[...] = g
        d_ref[...] = -ADAM_LR * (m_hat / (jnp.sqrt(v_hat) + ADAM_EPS) + ADAM_WD * w_ref[...])
        mo_ref[...] = m_new
        vo_ref[...] = v_new

    spec = pl.BlockSpec((t, c), lambda i: (i, 0))
    out = jax.ShapeDtypeStruct((r, c), F32)
    return pl.pallas_call(
        body, name=name, grid=(r // t,),
        in_specs=[spec, spec, spec, pl.BlockSpec((p, t, c), lambda i: (0, i, 0))],
        out_specs=[spec] * 4, out_shape=[out] * 4,
        compiler_params=_params("parallel"),
    )(w, m, v, parts)


def kernel(x, c, w_ada, b_ada, norm1_g, ffn1_w_gu, ffn1_w_down, norm2_g, w_mix_in, b_merge, conv_w, w_conv_out, w_attn_out, w_out, norm3_g, ffn2_w_gu, ffn2_w_down, final_g, loss_target, m_w_ada, m_b_ada, m_norm1_g, m_ffn1_w_gu, m_ffn1_w_down, m_norm2_g, m_w_mix_in, m_b_merge, m_conv_w, m_w_conv_out, m_w_attn_out, m_w_out, m_norm3_g, m_ffn2_w_gu, m_ffn2_w_down, m_final_g, v_w_ada, v_b_ada, v_norm1_g, v_ffn1_w_gu, v_ffn1_w_down, v_norm2_g, v_w_mix_in, v_b_merge, v_conv_w, v_w_conv_out, v_w_attn_out, v_w_out, v_norm3_g, v_ffn2_w_gu, v_ffn2_w_down, v_final_g):
    s = x.shape[1]
    me = 4 * lax.axis_index("x") + 2 * lax.axis_index("y") + lax.axis_index("c")
    x0 = x[0]
    target = loss_target[0]
    final_g2 = final_g.reshape(1, D_MODEL)

    def shard(w):
        return w[0].astype(BF16)

    def flipped(w):
        return jnp.swapaxes(w, 1, 2)

    def rows8(g):
        return g.reshape(N_DEV, -1, D_MODEL)

    small_in = jnp.concatenate([c.reshape(-1), b_merge.reshape(-1), conv_w.reshape(-1),
                                jnp.zeros((64,), F32)]).reshape(1, -1)
    n_ada = w_ada.shape[2]
    b_cols = lax.dynamic_slice(b_ada, (0, me * n_ada), (1, n_ada))
    small_all, mod_all, (wgu1, wd1) = prologue(small_in, w_ada[0], b_cols,
                                               [shard(flipped(ffn1_w_gu)), shard(ffn1_w_down)], "prologue")
    wd1 = wd1.reshape(4, FF_BLK, D_MODEL)
    small_all = small_all[:, 0, :]
    c_all = small_all[:, :D_MODEL]
    bm_full = small_all[:, 1024:1280].reshape(8, 2, 128).transpose(1, 0, 2).reshape(2, D_MODEL)
    cw_full = small_all[:, 1280:1472].reshape(8, 3, 64).transpose(1, 0, 2).reshape(3, CONV_W)
    mod = lax.dynamic_index_in_dim(mod_all, me, axis=1, keepdims=False).reshape(9, 1, D_MODEL)
    sh1, sc1, gt1, sh2, sc2, gt2, sh3, sc3, gt3 = [mod[n] for n in range(9)]

    (u1, gu1, act1), got = ffn_up(x0, norm1_g, sh1, sc1, wgu1, "ffn_up_1",
                                  exchange=gather_stage1([shard(w_mix_in)]))
    (x1, y1), (wmix, *got) = residual_matmul(
        act1, wd1, x0, gt1, 0.5, "ffn_down_1", exchange=merge_exchanges(
            gather_stage2(got), gather_stage1([shard(w_conv_out), shard(w_attn_out), shard(w_out)])))

    (u2, proj), (wco, wao, wout) = mix_in_proj(x1, norm2_g, sh2, sc2, wmix, "mix_in",
                                               exchange=gather_stage2(got))
    wco = wco.transpose(1, 0, 2).reshape(CONV_W, D_MODEL)
    wao = wao.transpose(1, 0, 2).reshape(ATTN_W, D_MODEL)
    wout = wout.reshape(D_MODEL, D_MODEL)
    sa = short_conv(proj, cw_full, "short_conv")
    (o, tot), got = stick_breaking_fwd(proj, "attn_fwd",
                                       exchange=gather_stage1([shard(flipped(ffn2_w_gu)), shard(ffn2_w_down)]))
    (merged, ya, yb, y2, x2), (wgu3, wd3) = merge_forward(sa, o, proj, wco, wao, bm_full, wout, x1, gt2, "merge",
                                                          exchange=gather_stage2(got))
    wd3 = wd3.reshape(4, FF_BLK, D_MODEL)

    u3, gu3, act3 = ffn_up(x2, norm3_g, sh3, sc3, wgu3, "ffn_up_3")

    dx3, dy3, dgt3, dfinal, sq = residual_matmul_loss(act3, wd3, x2, gt3, 0.5, target, final_g2, "ffn_down_3_loss")
    dgu3, du3 = ffn_tokens_bwd(dy3, wd3, gu3, wgu3, "ffn_bwd_3")
    dgu3 = dgu3.reshape(8, s, FF_BLK)
    g_wd3 = rows8(matmul_tn(act3, dy3[None], "grad_w_down_3", group=(4, 1)))
    g_wgu3 = matmul_tn(dgu3, u3[None], "grad_w_gu_3", group=(4, 1)).reshape(8, FF_BLK, D_MODEL)
    dx2, dsh3, dsc3, dn3, dy2, dgt2 = norm_modulate_bwd(du3, x2, dx3, norm3_g, sc3, "norm_bwd_3",
                                                        prev=(gt2, y2, 1.0))

    (dya, dyb, dga, dgb, dbm), pairs = merge_backward(dy2, wout, proj, ya, yb, bm_full, "merge_bwd",
                                                      exchange=scatter_stage1([g_wgu3, g_wd3]))
    sums3 = [pair_sum(g_wgu3, pairs[0], "pair_sum_w_gu_3"), pair_sum(g_wd3, pairs[1], "pair_sum_w_down_3")]
    g_wout = rows8(matmul_tn(merged[None], dy2[None], "grad_w_out"))
    dsa, do = out_proj_bwd(dya, dyb, wco, wao, "out_proj_bwd")
    g_wco = matmul_tn(sa[None], dya[None], "grad_w_conv_out").reshape(CONV_W, N_DEV, 128).transpose(1, 0, 2)
    g_wao = matmul_tn(o[None], dyb[None], "grad_w_attn_out").reshape(ATTN_W, N_DEV, 128).transpose(1, 0, 2)
    dcb, dcc, dcx, dconv = short_conv_bwd(dsa, proj, cw_full, "short_conv_bwd")
    (dq, dk, dv), landed3 = stick_breaking_bwd(proj, do, tot, "attn_bwd", exchange=scatter_stage2(sums3))
    dproj = jnp.concatenate([dcb, dcc, dcx, dq, dk.astype(BF16), dv.astype(BF16), dga, dgb], axis=1)
    g_wmix = matmul_tn(u2[None], dproj, "grad_w_mix_in", group=(1, 4), b_cols=MIX_BLK).reshape(
        N_DEV, D_MODEL, MIX_BLK)
    mixer_grads = [g_wmix, g_wco, g_wao, g_wout]
    (dx1, dsh2, dsc2, dn2, dy1, dgt1), pairs = norm_modulate_bwd(
        (dproj, wmix), x1, dx2, norm2_g, sc2, "norm_bwd_2", prev=(gt1, y1, 0.5),
        exchange=scatter_stage1(mixer_grads))
    sums_mix = [pair_sum(g, p, f"pair_sum_mixer_{n}") for n, (g, p) in enumerate(zip(mixer_grads, pairs))]

    dgu1, landed_mix = ffn_tokens_bwd(dy1, wd1, gu1, None, "ffn_dact_1", exchange=scatter_stage2(sums_mix[:1]))
    dgu1 = dgu1.reshape(8, s, FF_BLK)
    g_wgu1, landed_small = matmul_tn(dgu1, u1[None], "grad_w_gu_1", group=(4, 1),
                                     exchange=scatter_stage2(sums_mix[1:]))
    g_wgu1 = g_wgu1.reshape(8, FF_BLK, D_MODEL)
    g_wd1, pairs = matmul_tn(act1, dy1[None], "grad_w_down_1", group=(4, 1), exchange=scatter_stage1([g_wgu1]))
    g_wd1 = rows8(g_wd1)
    sum_gu1 = pair_sum(g_wgu1, pairs[0], "pair_sum_w_gu_1")
    du1, (landed_gu1, pair_d1) = matmul_nt_acc(
        dgu1, wgu1, "ffn_du_1", b_dims=NN, exchange=merge_exchanges(scatter_stage2([sum_gu1]), scatter_stage1([g_wd1])))
    sum_d1 = pair_sum(g_wd1, pair_d1, "pair_sum_w_down_1")
    (grad_x, dsh1, dsc1, dn1), landed_d1 = norm_modulate_bwd(du1, x0, dx1, norm1_g, sc1, "norm_bwd_1",
                                                            exchange=scatter_stage2([sum_d1]))

    loss_local = (0.5 / D_MODEL) * jnp.sum(sq)
    stats = jnp.concatenate(
        [v.reshape(-1) for v in (dsh1, dsc1, dgt1, dsh2, dsc2, dgt2, dsh3, dsc3, dgt3,
                                 dn1, dn2, dn3, dfinal, dbm, dconv)]
        + [jnp.broadcast_to(loss_local, (128,))]).reshape(1, -1)
    stats_all = all_gather_rows(stats, "gather_stats")
    n_mod = 9 * D_MODEL
    loss = jnp.sum(stats_all[:, 0, -1])
    dmod_all = stats_all[:, :, :n_mod]
    off = n_mod
    parts = {}
    for key in ("norm1_g", "norm2_g", "norm3_g", "final_g"):
        parts[key] = stats_all[:, :, off:off + D_MODEL]
        off += D_MODEL
    dbm_all = stats_all[:, 0, off:off + 2 * D_MODEL].reshape(N_DEV, 2, D_MODEL)
    off += 2 * D_MODEL
    dcw_all = stats_all[:, 0, off:off + 3 * CONV_W].reshape(N_DEV, 3, CONV_W)
    parts["b_merge"] = lax.dynamic_slice(dbm_all, (0, 0, me * 128), (N_DEV, 2, 128))
    parts["conv_w"] = lax.dynamic_slice(dcw_all, (0, 0, me * 64), (N_DEV, 3, 64))
    dmod_cols = lax.dynamic_slice(dmod_all[:, 0, :], (0, me * n_ada), (N_DEV, n_ada))
    parts["w_ada"] = ada_backward(c_all, dmod_cols, "ada_backward")[None]
    parts["b_ada"] = dmod_all
    parts["ffn2_w_gu"], parts["ffn2_w_down"] = landed3
    parts["w_mix_in"] = landed_mix[0]
    parts["w_conv_out"], parts["w_attn_out"], parts["w_out"] = landed_small
    parts["ffn1_w_gu"] = landed_gu1
    parts["ffn1_w_down"] = landed_d1[0]

    given = dict(w_ada=w_ada, b_ada=b_ada, norm1_g=norm1_g, ffn1_w_gu=ffn1_w_gu, ffn1_w_down=ffn1_w_down,
                 norm2_g=norm2_g, w_mix_in=w_mix_in, b_merge=b_merge, conv_w=conv_w, w_conv_out=w_conv_out,
                 w_attn_out=w_attn_out, w_out=w_out, norm3_g=norm3_g, ffn2_w_gu=ffn2_w_gu,
                 ffn2_w_down=ffn2_w_down, final_g=final_g)
    moments_m = dict(w_ada=m_w_ada, b_ada=m_b_ada, norm1_g=m_norm1_g, ffn1_w_gu=m_ffn1_w_gu,
                     ffn1_w_down=m_ffn1_w_down, norm2_g=m_norm2_g, w_mix_in=m_w_mix_in, b_merge=m_b_merge,
                     conv_w=m_conv_w, w_conv_out=m_w_conv_out, w_attn_out=m_w_attn_out, w_out=m_w_out,
                     norm3_g=m_norm3_g, ffn2_w_gu=m_ffn2_w_gu, ffn2_w_down=m_ffn2_w_down, final_g=m_final_g)
    moments_v = dict(w_ada=v_w_ada, b_ada=v_b_ada, norm1_g=v_norm1_g, ffn1_w_gu=v_ffn1_w_gu,
                     ffn1_w_down=v_ffn1_w_down, norm2_g=v_norm2_g, w_mix_in=v_w_mix_in, b_merge=v_b_merge,
                     conv_w=v_conv_w, w_conv_out=v_w_conv_out, w_attn_out=v_w_attn_out, w_out=v_w_out,
                     norm3_g=v_norm3_g, ffn2_w_gu=v_ffn2_w_gu, ffn2_w_down=v_ffn2_w_down, final_g=v_final_g)
    order = ["w_ada", "b_ada", "norm1_g", "ffn1_w_gu", "ffn1_w_down", "norm2_g", "w_mix_in", "b_merge",
             "conv_w", "w_conv_out", "w_attn_out", "w_out", "norm3_g", "ffn2_w_gu", "ffn2_w_down", "final_g"]
    grads, deltas, new_m, new_v = [], [], [], []
    for key in order:
        turn = flipped if key in ("ffn1_w_gu", "ffn2_w_gu") else (lambda a: a)
        shape = turn(given[key]).shape
        shape2 = (1, shape[0]) if len(shape) == 1 else shape[-2:]
        outs = adamw(turn(given[key]).reshape(shape2), turn(moments_m[key]).reshape(shape2),
                     turn(moments_v[key]).reshape(shape2), parts[key], f"adamw_{key}")
        for dst, val in zip((grads, deltas, new_m, new_v), outs):
            dst.append(turn(val.reshape(shape)))

    return (loss, grad_x[None], *grads, *deltas, *new_m, *new_v)
```

```python
import functools
from typing import Callable, NamedTuple

import jax
import jax.numpy as jnp
from jax import lax
from jax.experimental import pallas as pl
from jax.experimental.pallas import tpu as pltpu

F32 = jnp.float32
BF16 = jnp.bfloat16
MESH = pl.DeviceIdType.MESH
ANY = pl.BlockSpec(memory_space=pl.ANY)

N_DEV = 8
D_MODEL = 1024
D_FF = 2816
FF_BLK = D_FF // 4
N_HEADS = 8
HEAD_DIM = 64
CONV_W = 512
ATTN_W = 512
MIX_W = 3 * CONV_W + 3 * ATTN_W + 2 * D_MODEL
MIX_BLK = MIX_W // N_DEV
EPS = 1e-6
ATTN_SCALE = HEAD_DIM ** -0.5

ADAM_LR = 0.001
ADAM_B1 = 0.9
ADAM_B2 = 0.999
ADAM_EPS = 1e-08
ADAM_WD = 0.01
ADAM_STEP = 10
ADAM_BC1 = 1.0 - ADAM_B1 ** ADAM_STEP
ADAM_BC2 = 1.0 - ADAM_B2 ** ADAM_STEP

VMEM_LIMIT = 56 * 1024 * 1024
ROW_TILE = 512
ACC_TILE = 1024
ELT_TILE = 256
ATT_BLK = 256

NN = (((1,), (0,)), ((), ()))
NT = (((1,), (1,)), ((), ()))
TN = (((0,), (0,)), ((), ()))


def _dot(a, b, dims=NN):
    return lax.dot_general(a, b, dims, preferred_element_type=F32)


def _params(*sem):
    return pltpu.CompilerParams(dimension_semantics=sem, vmem_limit_bytes=VMEM_LIMIT)


def _sigmoid(x):
    return 1.0 / (1.0 + jnp.exp(-x))


def _me():
    x, y, c = lax.axis_index("x"), lax.axis_index("y"), lax.axis_index("c")
    return x, y, c, 4 * x + 2 * y + c


def _peer(k):
    x, y, c, _ = _me()
    px = 1 - x if (k >> 2) & 1 else x
    py = 1 - y if (k >> 1) & 1 else y
    pc = 1 - c if k & 1 else c
    return (px, py, pc), 4 * px + 2 * py + pc


class Exchange(NamedTuple):
    operands: tuple
    out_shapes: tuple
    aliases: dict
    n_remote: int
    n_local: int
    copies: Callable


CHIP_FLIPS = (2, 4, 6)
SIBLING = 1


def _remote(src, dst, send_sems, recv_sems, n, peer):
    return pltpu.make_async_remote_copy(src_ref=src, dst_ref=dst, send_sem=send_sems.at[n], recv_sem=recv_sems.at[n],
                                        device_id=peer, device_id_type=MESH)


def gather_stage1(shards):
    n = len(shards)
    rels = (SIBLING,) + CHIP_FLIPS

    def copies(ins, outs, send_sems, recv_sems, local_sems, rb, lb):
        _, _, _, me = _me()
        cps = []
        for w in range(n):
            cps.append(pltpu.make_async_copy(ins[w], outs[w].at[me], local_sems.at[lb + w]))
            for a, k in enumerate(rels):
                peer, _ = _peer(k)
                cps.append(_remote(ins[w], outs[w].at[me], send_sems, recv_sems, rb + len(rels) * w + a, peer))
        return cps

    shapes = tuple(jax.ShapeDtypeStruct((N_DEV,) + s.shape, s.dtype) for s in shards)
    return Exchange(tuple(shards), shapes, {}, len(rels) * n, n, copies)


def gather_stage2(fulls):
    n = len(fulls)

    def copies(ins, outs, send_sems, recv_sems, local_sems, rb, lb):
        sibling, _ = _peer(SIBLING)
        cps = []
        for w in range(n):
            for a, k in enumerate(CHIP_FLIPS):
                _, blk = _peer(k)
                cps.append(_remote(outs[w].at[blk], outs[w].at[blk], send_sems, recv_sems, rb + 3 * w + a, sibling))
        return cps

    shapes = tuple(jax.ShapeDtypeStruct(f.shape, f.dtype) for f in fulls)
    return Exchange(tuple(fulls), shapes, {w: w for w in range(n)}, 3 * n, 0, copies)


def scatter_stage1(fulls):
    n = len(fulls)

    def copies(ins, outs, send_sems, recv_sems, local_sems, rb, lb):
        _, _, c, _ = _me()
        sibling, _ = _peer(SIBLING)
        cps = []
        for w in range(n):
            for q in range(4):
                cps.append(_remote(ins[w].at[2 * q + (1 - c)], outs[w].at[q], send_sems, recv_sems, rb + 4 * w + q, sibling))
        return cps

    shapes = tuple(jax.ShapeDtypeStruct((4,) + f.shape[1:], f.dtype) for f in fulls)
    return Exchange(tuple(fulls), shapes, {}, 4 * n, 0, copies)


def scatter_stage2(sums):
    n = len(sums)

    def copies(ins, outs, send_sems, recv_sems, local_sems, rb, lb):
        x, y, _, _ = _me()
        mine = 2 * x + y
        cps = []
        for w in range(n):
            cps.append(pltpu.make_async_copy(ins[w].at[mine], outs[w].at[mine], local_sems.at[lb + w]))
            for a, k in enumerate(CHIP_FLIPS):
                peer, _ = _peer(k)
                cps.append(_remote(ins[w].at[2 * peer[0] + peer[1]], outs[w].at[mine], send_sems, recv_sems,
                                   rb + 3 * w + a, peer))
        return cps

    shapes = tuple(jax.ShapeDtypeStruct(s.shape, s.dtype) for s in sums)
    return Exchange(tuple(sums), shapes, {}, 3 * n, n, copies)


def merge_exchanges(a, b):
    na_in, na_out = len(a.operands), len(a.out_shapes)

    def copies(ins, outs, send_sems, recv_sems, local_sems, rb, lb):
        return (a.copies(ins[:na_in], outs[:na_out], send_sems, recv_sems, local_sems, rb, lb)
                + b.copies(ins[na_in:], outs[na_out:], send_sems, recv_sems, local_sems, rb + a.n_remote, lb + a.n_local))

    aliases = dict(a.aliases)
    aliases.update({na_in + i: na_out + o for i, o in b.aliases.items()})
    return Exchange(a.operands + b.operands, a.out_shapes + b.out_shapes, aliases,
                    a.n_remote + b.n_remote, a.n_local + b.n_local, copies)


def _exchange_scratch(ex):
    return [pltpu.SemaphoreType.DMA((ex.n_remote,)), pltpu.SemaphoreType.DMA((ex.n_remote,)),
            pltpu.SemaphoreType.DMA((max(ex.n_local, 1),))]


def _call(body, *, name, grid, in_specs, out_specs, out_shape, operands, scratch_shapes=(), semantics=(),
          exchange=None):
    if exchange is None:
        return pl.pallas_call(
            body, name=name, grid=grid, in_specs=in_specs, out_specs=out_specs, out_shape=out_shape,
            scratch_shapes=list(scratch_shapes), compiler_params=_params(*semantics))(*operands)
    single = not isinstance(out_shape, (list, tuple))
    out_shapes = [out_shape] if single else list(out_shape)
    out_specs_l = [out_specs] if single else list(out_specs)
    n_in, n_out, n_scr = len(operands), len(out_shapes), len(scratch_shapes)
    x_in, x_out = len(exchange.operands), len(exchange.out_shapes)

    def hosted(*refs):
        ins, refs = refs[:n_in], refs[n_in:]
        xin, refs = refs[:x_in], refs[x_in:]
        outs, refs = refs[:n_out], refs[n_out:]
        xout, refs = refs[:x_out], refs[x_out:]
        scr, sems = refs[:n_scr], refs[n_scr:]
        first = functools.reduce(jnp.logical_and, [pl.program_id(a) == 0 for a in range(len(grid))])
        last = functools.reduce(jnp.logical_and, [pl.program_id(a) == g - 1 for a, g in enumerate(grid)])

        @pl.when(first)
        def _():
            for cp in exchange.copies(xin, xout, *sems, 0, 0):
                cp.start()

        body(*ins, *outs, *scr)

        @pl.when(last)
        def _():
            for cp in exchange.copies(xin, xout, *sems, 0, 0):
                cp.wait()

    res = pl.pallas_call(
        hosted, name=name, grid=grid,
        in_specs=list(in_specs) + [ANY] * x_in, out_specs=out_specs_l + [ANY] * x_out,
        out_shape=out_shapes + list(exchange.out_shapes),
        scratch_shapes=list(scratch_shapes) + _exchange_scratch(exchange),
        input_output_aliases={n_in + i: n_out + o for i, o in exchange.aliases.items()},
        compiler_params=_params(*(["arbitrary"] * len(grid))),
    )(*operands, *exchange.operands)
    outs, xouts = res[:n_out], res[n_out:]
    return (outs[0] if single else outs), xouts


def all_gather_rows(v, name):
    r, n = v.shape

    def body(v_ref, out_ref, send_sems, recv_sems):
        _, _, _, me = _me()
        out_ref[me] = v_ref[...]
        copies = []
        for k in range(1, N_DEV):
            peer, _ = _peer(k)
            copies.append(_remote(v_ref, out_ref.at[me], send_sems, recv_sems, k - 1, peer))
        for cp in copies:
            cp.start()
        for cp in copies:
            cp.wait()

    return pl.pallas_call(
        body, name=name,
        out_shape=jax.ShapeDtypeStruct((N_DEV, r, n), v.dtype),
        in_specs=[pl.BlockSpec(memory_space=pltpu.VMEM)],
        out_specs=pl.BlockSpec(memory_space=pltpu.VMEM),
        scratch_shapes=[pltpu.SemaphoreType.DMA((N_DEV - 1,)), pltpu.SemaphoreType.DMA((N_DEV - 1,))],
    )(v)


def pair_sum(full, pair, name):
    _, r, c = full.shape
    t = r
    core = lax.axis_index("c").astype(jnp.int32).reshape(1)

    def body(core_ref, f_ref, p_ref, o_ref):
        o_ref[...] = (f_ref[...].astype(F32) + p_ref[...].astype(F32)).astype(BF16)

    return pl.pallas_call(
        body, name=name,
        grid_spec=pltpu.PrefetchScalarGridSpec(
            num_scalar_prefetch=1, grid=(4, r // t),
            in_specs=[pl.BlockSpec((None, None, t, c), lambda q, i, core_ref: (q, core_ref[0], i, 0)),
                      pl.BlockSpec((None, t, c), lambda q, i, core_ref: (q, i, 0))],
            out_specs=pl.BlockSpec((None, t, c), lambda q, i, core_ref: (q, i, 0))),
        out_shape=jax.ShapeDtypeStruct((4, r, c), BF16),
        compiler_params=_params("parallel", "parallel"),
    )(core, full.reshape(4, 2, r, c), pair)


def prologue(small_in, w_ada, b_cols, shards, name):
    ex = gather_stage1(shards)
    n_sh = len(shards)
    n_small = small_in.shape[1]
    cols = w_ada.shape[1]

    def body(*refs):
        small_ref, w_ref, b_ref = refs[:3]
        shard_refs = refs[3:3 + n_sh]
        small_out, mod_out = refs[3 + n_sh:5 + n_sh]
        fulls = refs[5 + n_sh:5 + 2 * n_sh]
        part_ref, send1, recv1, send2, recv2, wsend, wrecv, wlocal, fsend, frecv = refs[5 + 2 * n_sh:]
        _, _, _, me = _me()
        def start_gather(src_ref, dst_ref, send_sems, recv_sems):
            cps = [_remote(src_ref, dst_ref.at[me], send_sems, recv_sems, k - 1, _peer(k)[0]) for k in range(1, N_DEV)]
            for cp in cps:
                cp.start()
            return cps

        small_out[me] = small_ref[...]
        first = start_gather(small_ref, small_out, send1, recv1)
        big = ex.copies(shard_refs, fulls, wsend, wrecv, wlocal, 0, 0)
        for cp in big:
            cp.start()
        for cp in first:
            cp.wait()
        c_all = jnp.concatenate([small_out[d][:, :D_MODEL] for d in range(N_DEV)], axis=0)
        act = c_all * _sigmoid(c_all)
        part_ref[...] = jnp.dot(act, w_ref[...], precision=lax.Precision.HIGHEST,
                                preferred_element_type=F32) + b_ref[...]
        mod_out[me] = part_ref[...]
        for cp in start_gather(part_ref, mod_out, send2, recv2):
            cp.wait()
        per = 2 + len(CHIP_FLIPS)
        sibling, _ = _peer(SIBLING)
        onward = []
        for w in range(n_sh):
            for a, k in enumerate(CHIP_FLIPS):
                _, blk = _peer(k)
                big[per * w + 2 + a].wait_recv()
                cp = _remote(fulls[w].at[blk], fulls[w].at[blk], fsend, frecv, len(CHIP_FLIPS) * w + a, sibling)
                cp.start()
                onward.append(cp)
        for w in range(n_sh):
            big[per * w].wait()
            big[per * w + 1].wait()
            for a in range(len(CHIP_FLIPS)):
                big[per * w + 2 + a].wait_send()
        for cp in onward:
            cp.wait()

    vmem = pl.BlockSpec(memory_space=pltpu.VMEM)
    sems = pltpu.SemaphoreType.DMA((N_DEV - 1,))
    res = pl.pallas_call(
        body, name=name,
        out_shape=[jax.ShapeDtypeStruct((N_DEV, 1, n_small), F32), jax.ShapeDtypeStruct((N_DEV, N_DEV, cols), F32)]
        + list(ex.out_shapes),
        in_specs=[vmem, vmem, vmem] + [ANY] * n_sh, out_specs=[vmem, vmem] + [ANY] * n_sh,
        scratch_shapes=[pltpu.VMEM((N_DEV, cols), F32), sems, sems, sems, sems] + _exchange_scratch(ex)
        + [pltpu.SemaphoreType.DMA((len(CHIP_FLIPS) * n_sh,)), pltpu.SemaphoreType.DMA((len(CHIP_FLIPS) * n_sh,))],
        compiler_params=pltpu.CompilerParams(vmem_limit_bytes=VMEM_LIMIT),
    )(small_in, w_ada, b_cols, *shards)
    return res[0], res[1], res[2:]


def ada_backward(c_all, dmod_cols, name):
    n = dmod_cols.shape[1]

    def body(c_ref, d_ref, o_ref):
        c = c_ref[...]
        act = c * _sigmoid(c)
        o_ref[...] = lax.dot_general(act, d_ref[...], TN, precision=lax.Precision.HIGHEST,
                                     preferred_element_type=F32)

    return pl.pallas_call(
        body, name=name, out_shape=jax.ShapeDtypeStruct((D_MODEL, n), F32),
        compiler_params=pltpu.CompilerParams(vmem_limit_bytes=VMEM_LIMIT),
    )(c_all, dmod_cols)


def _row_spec(t, width=D_MODEL):
    return pl.BlockSpec((t, width), lambda i: (i, 0))


def _vec_spec(rows=1, width=D_MODEL):
    return pl.BlockSpec((rows, width), lambda i: (0, 0))


def _resident(shape):
    return pl.BlockSpec(shape, lambda i: (0,) * len(shape), pipeline_mode=pl.Buffered(1))


def _norm_modulate(x_ref, g_ref, shift_ref, scale_ref):
    xv = x_ref[...]
    r = lax.rsqrt(jnp.mean(xv * xv, axis=-1, keepdims=True) + EPS)
    a = (xv * r) * g_ref[...]
    return (a * (1.0 + scale_ref[...]) + shift_ref[...]).astype(BF16)


def norm_modulate_bwd(du, x, dx_out, g, scale, name, prev=None, exchange=None):
    s = x.shape[0]
    factors = isinstance(du, tuple)
    t = min(ROW_TILE if factors else ELT_TILE, s)
    has_prev = prev is not None

    def body(*refs):
        if factors:
            a_ref, b_ref = refs[:2]
            refs = refs[1:]
            nk, _, n = b_ref.shape
        du_ref, x_ref, dxo_ref, g_ref, sc_ref = refs[:5]
        refs = refs[5:]
        if has_prev:
            gt_ref, y_ref = refs[:2]
            refs = refs[2:]
        dx_ref, dsh_ref, dsc_ref, dg_ref = refs[:4]

        @pl.when(pl.program_id(0) == 0)
        def _():
            dsh_ref[...] = jnp.zeros_like(dsh_ref)
            dsc_ref[...] = jnp.zeros_like(dsc_ref)
            dg_ref[...] = jnp.zeros_like(dg_ref)
            if has_prev:
                refs[5][...] = jnp.zeros_like(refs[5])

        xv = x_ref[...]
        if factors:
            duv = None
            for k in range(0, nk, 2):
                pair = jnp.concatenate([b_ref[k], b_ref[k + 1]], axis=1)
                part = _dot(a_ref[:, k * n:(k + 2) * n], pair, NT)
                duv = part if duv is None else duv + part
        else:
            duv = du_ref[...]
        gv = g_ref[...]
        r = lax.rsqrt(jnp.mean(xv * xv, axis=-1, keepdims=True) + EPS)
        nrm = xv * r
        a = nrm * gv
        dsh_ref[...] += jnp.sum(duv, axis=0, keepdims=True)
        dsc_ref[...] += jnp.sum(duv * a, axis=0, keepdims=True)
        da = duv * (1.0 + sc_ref[...])
        dg_ref[...] += jnp.sum(da * nrm, axis=0, keepdims=True)
        dn = da * gv
        dx = dxo_ref[...] + r * (dn - nrm * jnp.mean(dn * nrm, axis=-1, keepdims=True))
        dx_ref[...] = dx
        if has_prev:
            coef = prev[2]
            refs[4][...] = (coef * gt_ref[...] * dx).astype(BF16)
            refs[5][...] += coef * jnp.sum(dx * y_ref[...].astype(F32), axis=0, keepdims=True)

    vec = jax.ShapeDtypeStruct((1, D_MODEL), F32)
    if factors:
        operands = [du[0], du[1], x, dx_out, g, scale]
        in_specs = [_row_spec(t, du[0].shape[1]), _resident(du[1].shape)]
    else:
        operands = [du, x, dx_out, g, scale]
        in_specs = [_row_spec(t)]
    in_specs += [_row_spec(t), _row_spec(t), _vec_spec(), _vec_spec()]
    out_specs = [_row_spec(t), _vec_spec(), _vec_spec(), _vec_spec()]
    out_shape = [jax.ShapeDtypeStruct((s, D_MODEL), F32), vec, vec, vec]
    if has_prev:
        operands += [prev[0], prev[1]]
        in_specs += [_vec_spec(), _row_spec(t)]
        out_specs += [_row_spec(t), _vec_spec()]
        out_shape += [jax.ShapeDtypeStruct((s, D_MODEL), BF16), vec]
    return _call(body, name=name, grid=(s // t,), in_specs=in_specs, out_specs=out_specs, out_shape=out_shape,
                 operands=operands, semantics=("arbitrary",), exchange=exchange)


def ffn_up(x, norm_g, shift, scale, w_gu_t, name, exchange=None):
    s = x.shape[0]
    t = min(ROW_TILE, s)

    def body(x_ref, g_ref, sh_ref, sc_ref, w_ref, u_ref, gu_ref, act_ref):
        uv = _norm_modulate(x_ref, g_ref, sh_ref, sc_ref)
        u_ref[...] = uv
        for j in range(4):
            g = _dot(uv, w_ref[j], NT)
            up = _dot(uv, w_ref[j + 4], NT)
            gu_ref[0, j] = g.astype(BF16)
            gu_ref[1, j] = up.astype(BF16)
            act_ref[j] = (g * _sigmoid(g) * up).astype(BF16)

    return _call(
        body, name=name, grid=(s // t,),
        in_specs=[_row_spec(t), _vec_spec(), _vec_spec(), _vec_spec(), _resident(w_gu_t.shape)],
        out_specs=[_row_spec(t), pl.BlockSpec((2, 4, t, FF_BLK), lambda i: (0, 0, i, 0)),
                   pl.BlockSpec((4, t, FF_BLK), lambda i: (0, i, 0))],
        out_shape=[jax.ShapeDtypeStruct((s, D_MODEL), BF16), jax.ShapeDtypeStruct((2, 4, s, FF_BLK), BF16),
                   jax.ShapeDtypeStruct((4, s, FF_BLK), BF16)],
        operands=(x, norm_g, shift, scale, w_gu_t), semantics=("parallel",), exchange=exchange)


def residual_matmul(a, b, x, gate, coef, name, exchange=None):
    nk, s, kb = a.shape
    t = min(ROW_TILE, s)

    def body(a_ref, b_ref, x_ref, gt_ref, xo_ref, y_ref):
        y = _dot(a_ref[0], b_ref[0])
        for k in range(1, nk):
            y = y + _dot(a_ref[k], b_ref[k])
        y_ref[...] = y.astype(BF16)
        xo_ref[...] = x_ref[...] + coef * gt_ref[...] * y

    return _call(
        body, name=name, grid=(s // t,),
        in_specs=[pl.BlockSpec((nk, t, kb), lambda i: (0, i, 0)),
                  pl.BlockSpec((nk, kb, D_MODEL), lambda i: (0, 0, 0)),
                  _row_spec(t), _vec_spec()],
        out_specs=[_row_spec(t), _row_spec(t)],
        out_shape=[jax.ShapeDtypeStruct((s, D_MODEL), F32), jax.ShapeDtypeStruct((s, D_MODEL), BF16)],
        operands=(a, b, x, gate), semantics=("parallel",), exchange=exchange)


def residual_matmul_loss(a, b, x, gate, coef, target, final_g, name):
    nk, s, kb = a.shape
    t = min(ROW_TILE, s)

    def body(a_ref, b_ref, x_ref, gt_ref, t_ref, fg_ref, dx_ref, dy_ref, dgt_ref, dfg_ref, sq_ref):
        @pl.when(pl.program_id(0) == 0)
        def _():
            dgt_ref[...] = jnp.zeros_like(dgt_ref)
            dfg_ref[...] = jnp.zeros_like(dfg_ref)
            sq_ref[...] = jnp.zeros_like(sq_ref)

        y = _dot(a_ref[0], b_ref[0])
        for k in range(1, nk):
            y = y + _dot(a_ref[k], b_ref[k])
        gt = gt_ref[...]
        fg = fg_ref[...]
        xv = x_ref[...] + coef * gt * y
        r = lax.rsqrt(jnp.mean(xv * xv, axis=-1, keepdims=True) + EPS)
        nrm = xv * r
        err = nrm * fg - t_ref[...]
        sq_ref[...] += jnp.sum(err * err, axis=0, keepdims=True)
        dout = err * (1.0 / D_MODEL)
        dfg_ref[...] += jnp.sum(dout * nrm, axis=0, keepdims=True)
        dn = dout * fg
        dx = r * (dn - nrm * jnp.mean(dn * nrm, axis=-1, keepdims=True))
        dx_ref[...] = dx
        dy_ref[...] = (coef * gt * dx).astype(BF16)
        dgt_ref[...] += coef * jnp.sum(dx * y, axis=0, keepdims=True)

    vec = jax.ShapeDtypeStruct((1, D_MODEL), F32)
    return pl.pallas_call(
        body, name=name, grid=(s // t,),
        in_specs=[pl.BlockSpec((nk, t, kb), lambda i: (0, i, 0)), _resident(b.shape),
                  _row_spec(t), _vec_spec(), _row_spec(t), _vec_spec()],
        out_specs=[_row_spec(t), _row_spec(t), _vec_spec(), _vec_spec(), _vec_spec()],
        out_shape=[jax.ShapeDtypeStruct((s, D_MODEL), F32), jax.ShapeDtypeStruct((s, D_MODEL), BF16), vec, vec, vec],
        compiler_params=_params("arbitrary"),
    )(a, b, x, gate, target, final_g)


def ffn_tokens_bwd(dy, w_down, gu, w_gu_t, name, exchange=None):
    s = dy.shape[0]
    t = min(ROW_TILE, s)
    with_du = w_gu_t is not None

    def body(*refs):
        if with_du:
            dy_ref, wd_ref, gu_ref, wgu_ref, dgu_ref, du_ref = refs
        else:
            dy_ref, wd_ref, gu_ref, dgu_ref = refs
        dyv = dy_ref[...]
        du = None
        for j in range(4):
            dact = _dot(dyv, wd_ref[j], NT)
            g = gu_ref[0, j].astype(F32)
            up = gu_ref[1, j].astype(F32)
            sg = _sigmoid(g)
            slopes = (up * sg * (1.0 + g * (1.0 - sg)), g * sg)
            for half in range(2):
                d = (dact * slopes[half]).astype(BF16)
                dgu_ref[half, j] = d
                if with_du:
                    part = _dot(d, wgu_ref[4 * half + j])
                    du = part if du is None else du + part
        if with_du:
            du_ref[...] = du

    blocks = pl.BlockSpec((2, 4, t, FF_BLK), lambda i: (0, 0, i, 0))
    dgu_shape = jax.ShapeDtypeStruct((2, 4, s, FF_BLK), BF16)
    if with_du:
        return _call(
            body, name=name, grid=(s // t,),
            in_specs=[_row_spec(t), _resident(w_down.shape), blocks, _resident(w_gu_t.shape)],
            out_specs=[blocks, _row_spec(t)],
            out_shape=[dgu_shape, jax.ShapeDtypeStruct((s, D_MODEL), F32)],
            operands=(dy, w_down, gu, w_gu_t), semantics=("parallel",), exchange=exchange)
    return _call(
        body, name=name, grid=(s // t,),
        in_specs=[_row_spec(t), _resident(w_down.shape), blocks], out_specs=blocks, out_shape=dgu_shape,
        operands=(dy, w_down, gu), semantics=("parallel",), exchange=exchange)


def matmul_nt_acc(a, b, name, b_dims=NT, exchange=None):
    nk = b.shape[0]
    d, n = (b.shape[1], b.shape[2]) if b_dims == NT else (b.shape[2], b.shape[1])
    s = a.shape[-2]
    t = min(ROW_TILE, s)
    by_columns = a.ndim == 2

    def body(a_ref, b_ref, o_ref):
        def a_blk(k):
            return a_ref[:, k * n:(k + 1) * n] if by_columns else a_ref[k]

        acc = _dot(a_blk(0), b_ref[0], b_dims)
        for k in range(1, nk):
            acc = acc + _dot(a_blk(k), b_ref[k], b_dims)
        o_ref[...] = acc

    a_spec = _row_spec(t, nk * n) if by_columns else pl.BlockSpec((nk, t, n), lambda i: (0, i, 0))
    return _call(
        body, name=name, grid=(s // t,),
        in_specs=[a_spec, pl.BlockSpec(b.shape, lambda i: (0, 0, 0))],
        out_specs=pl.BlockSpec((t, d), lambda i: (i, 0)),
        out_shape=jax.ShapeDtypeStruct((s, d), F32),
        operands=(a, b), semantics=("parallel",), exchange=exchange)


def matmul_tn(a, b, name, group=(1, 1), b_cols=None, exchange=None):
    ja, s, m = a.shape
    by_columns = b.ndim == 2
    jb, n = (b.shape[1] // b_cols, b_cols) if by_columns else (b.shape[0], b.shape[2])
    ga, gb = group
    t = min(ACC_TILE, s)
    nk = s // t

    def body(a_ref, b_ref, o_ref, acc_ref):
        k = pl.program_id(2)

        @pl.when(k == 0)
        def _():
            acc_ref[...] = jnp.zeros_like(acc_ref)

        for p in range(ga):
            if by_columns and gb % 2 == 0:
                for q in range(0, gb, 2):
                    both = _dot(a_ref[p], b_ref[:, q * n:(q + 2) * n], TN)
                    acc_ref[p, q] += both[:, :n]
                    acc_ref[p, q + 1] += both[:, n:]
                continue
            for q in range(gb):
                b_blk = b_ref[:, q * n:(q + 1) * n] if by_columns else b_ref[q]
                acc_ref[p, q] += _dot(a_ref[p], b_blk, TN)

        @pl.when(k == nk - 1)
        def _():
            o_ref[...] = acc_ref[...].astype(BF16)

    return _call(
        body, name=name, grid=(ja // ga, jb // gb, nk),
        in_specs=[pl.BlockSpec((ga, t, m), lambda p, q, k: (p, k, 0)),
                  pl.BlockSpec((t, gb * n), lambda p, q, k: (k, q)) if by_columns
                  else pl.BlockSpec((gb, t, n), lambda p, q, k: (q, k, 0))],
        out_specs=pl.BlockSpec((ga, gb, m, n), lambda p, q, k: (p, q, 0, 0)),
        out_shape=jax.ShapeDtypeStruct((ja, jb, m, n), BF16),
        operands=(a, b), scratch_shapes=[pltpu.VMEM((ga, gb, m, n), F32)],
        semantics=("parallel", "parallel", "arbitrary"), exchange=exchange)


def mix_in_proj(x, norm_g, shift, scale, w_mix, name, exchange=None):
    s = x.shape[0]
    t = min(ROW_TILE, s)

    def body(x_ref, g_ref, sh_ref, sc_ref, w_ref, u_ref, o_ref):
        uv = _norm_modulate(x_ref, g_ref, sh_ref, sc_ref)
        u_ref[...] = uv
        for j in range(0, N_DEV, 2):
            pair = jnp.concatenate([w_ref[j], w_ref[j + 1]], axis=1)
            o_ref[:, j * MIX_BLK:(j + 2) * MIX_BLK] = _dot(uv, pair).astype(BF16)

    return _call(
        body, name=name, grid=(s // t,),
        in_specs=[_row_spec(t), _vec_spec(), _vec_spec(), _vec_spec(), _resident(w_mix.shape)],
        out_specs=[_row_spec(t), _row_spec(t, MIX_W)],
        out_shape=[jax.ShapeDtypeStruct((s, D_MODEL), BF16), jax.ShapeDtypeStruct((s, MIX_W), BF16)],
        operands=(x, norm_g, shift, scale, w_mix), semantics=("parallel",), exchange=exchange)


def _conv_taps(cc_ref, cx_ref, s):
    v = cc_ref[...].astype(F32) * cx_ref[...].astype(F32)
    tok = lax.broadcasted_iota(jnp.int32, v.shape, 0)
    v1 = jnp.where(tok >= 1, pltpu.roll(v, 1, 0), 0.0)
    v2 = jnp.where(tok >= 2, pltpu.roll(v, 2, 0), 0.0)
    return v, v1, v2, tok


def _proj_cols(s, first):
    return pl.BlockSpec((s, 128), lambda j: (0, first + j))


def short_conv(proj, conv_w, name):
    s = proj.shape[0]

    def body(cb_ref, cc_ref, cx_ref, w_ref, o_ref):
        v, v1, v2, _ = _conv_taps(cc_ref, cx_ref, s)
        y = w_ref[0:1, :] * v2 + w_ref[1:2, :] * v1 + w_ref[2:3, :] * v
        o_ref[...] = (cb_ref[...].astype(F32) * y).astype(BF16)

    return pl.pallas_call(
        body, name=name, grid=(CONV_W // 128,),
        in_specs=[_proj_cols(s, 0), _proj_cols(s, 4), _proj_cols(s, 8),
                  pl.BlockSpec((3, 128), lambda j: (0, j))],
        out_specs=pl.BlockSpec((s, 128), lambda j: (0, j)),
        out_shape=jax.ShapeDtypeStruct((s, CONV_W), BF16),
        compiler_params=_params("parallel"),
    )(proj, proj, proj, conv_w)


def short_conv_bwd(dsa, proj, conv_w, name):
    s = proj.shape[0]

    def body(dsa_ref, cb_ref, cc_ref, cx_ref, w_ref, dcb_ref, dcc_ref, dcx_ref, dw_ref):
        v, v1, v2, tok = _conv_taps(cc_ref, cx_ref, s)
        w0, w1, w2 = w_ref[0:1, :], w_ref[1:2, :], w_ref[2:3, :]
        y = w0 * v2 + w1 * v1 + w2 * v
        dsa_v = dsa_ref[...].astype(F32)
        dcb_ref[...] = (dsa_v * y).astype(BF16)
        dy = dsa_v * cb_ref[...].astype(F32)
        dw_ref[0:1, :] = jnp.sum(dy * v2, axis=0, keepdims=True)
        dw_ref[1:2, :] = jnp.sum(dy * v1, axis=0, keepdims=True)
        dw_ref[2:3, :] = jnp.sum(dy * v, axis=0, keepdims=True)
        dy1 = jnp.where(tok < s - 1, pltpu.roll(dy, s - 1, 0), 0.0)
        dy2 = jnp.where(tok < s - 2, pltpu.roll(dy, s - 2, 0), 0.0)
        dv = w2 * dy + w1 * dy1 + w0 * dy2
        dcc_ref[...] = (dv * cx_ref[...].astype(F32)).astype(BF16)
        dcx_ref[...] = (dv * cc_ref[...].astype(F32)).astype(BF16)

    col = pl.BlockSpec((s, 128), lambda j: (0, j))
    act = jax.ShapeDtypeStruct((s, CONV_W), BF16)
    return pl.pallas_call(
        body, name=name, grid=(CONV_W // 128,),
        in_specs=[col, _proj_cols(s, 0), _proj_cols(s, 4), _proj_cols(s, 8),
                  pl.BlockSpec((3, 128), lambda j: (0, j))],
        out_specs=[col, col, col, pl.BlockSpec((3, 128), lambda j: (0, j))],
        out_shape=[act, act, act, jax.ShapeDtypeStruct((3, CONV_W), F32)],
        compiler_params=_params("parallel"),
    )(dsa, proj, proj, proj, conv_w)


def _gate_specs(t):
    return [pl.BlockSpec((t, D_MODEL), lambda i: (i, 3)), pl.BlockSpec((t, D_MODEL), lambda i: (i, 4))]


def merge_forward(sa, o, proj, w_co, w_ao, b_merge, w_out, x, gate, name, exchange=None):
    s = sa.shape[0]
    t = min(ROW_TILE, s)

    def body(sa_ref, o_ref, ga_ref, gb_ref, wco_ref, wao_ref, bm_ref, wout_ref, x_ref, gt_ref,
             mg_ref, ya_ref, yb_ref, y_ref, xo_ref):
        ya = _dot(sa_ref[...], wco_ref[...])
        yb = _dot(o_ref[...], wao_ref[...])
        sga = _sigmoid(ga_ref[...].astype(F32) + bm_ref[0:1, :])
        sgb = _sigmoid(gb_ref[...].astype(F32) + bm_ref[1:2, :])
        merged = (sga * ya + sgb * yb).astype(BF16)
        mg_ref[...] = merged
        ya_ref[...] = ya.astype(BF16)
        yb_ref[...] = yb.astype(BF16)
        y = _dot(merged, wout_ref[...])
        y_ref[...] = y.astype(BF16)
        xo_ref[...] = x_ref[...] + gt_ref[...] * y

    act = jax.ShapeDtypeStruct((s, D_MODEL), BF16)
    return _call(
        body, name=name, grid=(s // t,),
        in_specs=[_row_spec(t, CONV_W), _row_spec(t, ATTN_W)] + _gate_specs(t)
        + [_vec_spec(CONV_W), _vec_spec(ATTN_W), _vec_spec(2), _vec_spec(D_MODEL), _row_spec(t), _vec_spec()],
        out_specs=[_row_spec(t)] * 5, out_shape=[act, act, act, act, jax.ShapeDtypeStruct((s, D_MODEL), F32)],
        operands=(sa, o, proj, proj, w_co, w_ao, b_merge, w_out, x, gate), semantics=("parallel",),
        exchange=exchange)


def merge_backward(dy, w_out, proj, ya, yb, b_merge, name, exchange=None):
    s = dy.shape[0]
    t = min(ROW_TILE, s)

    def body(dy_ref, w_ref, ga_ref, gb_ref, ya_ref, yb_ref, bm_ref,
             dya_ref, dyb_ref, dga_ref, dgb_ref, dbm_ref):
        @pl.when(pl.program_id(0) == 0)
        def _():
            dbm_ref[...] = jnp.zeros_like(dbm_ref)

        dmg = _dot(dy_ref[...], w_ref[...], NT)
        sga = _sigmoid(ga_ref[...].astype(F32) + bm_ref[0:1, :])
        sgb = _sigmoid(gb_ref[...].astype(F32) + bm_ref[1:2, :])
        dya_ref[...] = (dmg * sga).astype(BF16)
        dyb_ref[...] = (dmg * sgb).astype(BF16)
        dga = dmg * ya_ref[...].astype(F32) * sga * (1.0 - sga)
        dgb = dmg * yb_ref[...].astype(F32) * sgb * (1.0 - sgb)
        dga_ref[...] = dga.astype(BF16)
        dgb_ref[...] = dgb.astype(BF16)
        dbm_ref[0:1, :] += jnp.sum(dga, axis=0, keepdims=True)
        dbm_ref[1:2, :] += jnp.sum(dgb, axis=0, keepdims=True)

    act = jax.ShapeDtypeStruct((s, D_MODEL), BF16)
    return _call(
        body, name=name, grid=(s // t,),
        in_specs=[_row_spec(t), _vec_spec(D_MODEL)] + _gate_specs(t)
        + [_row_spec(t), _row_spec(t), _vec_spec(2)],
        out_specs=[_row_spec(t)] * 4 + [_vec_spec(2)],
        out_shape=[act] * 4 + [jax.ShapeDtypeStruct((2, D_MODEL), F32)],
        operands=(dy, w_out, proj, proj, ya, yb, b_merge), semantics=("arbitrary",), exchange=exchange)


def out_proj_bwd(dya, dyb, w_co, w_ao, name):
    s = dya.shape[0]
    t = min(ROW_TILE, s)

    def body(dya_ref, dyb_ref, wco_ref, wao_ref, dsa_ref, do_ref):
        dsa_ref[...] = _dot(dya_ref[...], wco_ref[...], NT).astype(BF16)
        do_ref[...] = _dot(dyb_ref[...], wao_ref[...], NT).astype(BF16)

    return pl.pallas_call(
        body, name=name, grid=(s // t,),
        in_specs=[_row_spec(t), _row_spec(t), _vec_spec(CONV_W), _vec_spec(ATTN_W)],
        out_specs=[_row_spec(t, CONV_W), _row_spec(t, ATTN_W)],
        out_shape=[jax.ShapeDtypeStruct((s, CONV_W), BF16), jax.ShapeDtypeStruct((s, ATTN_W), BF16)],
        compiler_params=_params("parallel"),
    )(dya, dyb, w_co, w_ao)


ATT_HEADS = 4
ATT_LANES = ATT_HEADS * HEAD_DIM
ATT_UNDERFLOW = 110.0


def _softplus(z):
    return jnp.maximum(z, 0.0) + jnp.log(1.0 + jnp.exp(-jnp.abs(z)))


def _head_masks(rows):
    lane = lax.broadcasted_iota(jnp.int32, (rows, ATT_LANES), 1)
    return [(lane >= h * HEAD_DIM) & (lane < (h + 1) * HEAD_DIM) for h in range(ATT_HEADS)]


def _per_head(x, masks):
    return [jnp.where(m, x, jnp.zeros_like(x)) for m in masks]


def _att_specs(s, blk):
    first = {"q": 3 * CONV_W // ATT_LANES, "k": (3 * CONV_W + ATTN_W) // ATT_LANES,
             "v": (3 * CONV_W + 2 * ATTN_W) // ATT_LANES}
    return [pl.BlockSpec((blk, ATT_LANES), lambda h, i: (i, first["q"] + h)),
            pl.BlockSpec((s, ATT_LANES), lambda h, i: (0, first["k"] + h)),
            pl.BlockSpec((s, ATT_LANES), lambda h, i: (0, first["v"] + h))]


def _head_norms(x, masks):
    sq = jnp.square(x.astype(F32))
    return [jnp.sum(jnp.where(m, sq, 0.0), axis=1, keepdims=True) for m in masks]


def stick_breaking_fwd(proj, name, exchange=None):
    s = proj.shape[0]
    blk = ATT_BLK
    nq = s // blk

    def body(q_ref, k_ref, v_ref, o_ref, tot_ref, first_ref, kmax_ref):
        i = pl.program_id(1)
        row = lax.broadcasted_iota(jnp.int32, (blk, blk), 0)
        col = lax.broadcasted_iota(jnp.int32, (blk, blk), 1)
        tri = (row >= col).astype(BF16)
        causal = col < row
        masks = _head_masks(blk)
        q_all = q_ref[...] * ATTN_SCALE
        qs = _per_head(q_all, masks)

        @pl.when(i == 0)
        def _():
            def longest(n, best):
                norms = _head_norms(k_ref[pl.ds(pl.multiple_of(n * blk, blk), blk), :], masks)
                return tuple(jnp.maximum(b, v) for b, v in zip(best, norms))

            best = lax.fori_loop(0, nq, longest, tuple(jnp.zeros((blk, 1), F32) for _ in range(ATT_HEADS)))
            for h in range(ATT_HEADS):
                kmax_ref[h] = jnp.sqrt(jnp.max(best[h], axis=0, keepdims=True))

        needed = [jnp.sqrt(n) * kmax_ref[h] + ATT_UNDERFLOW for h, n in enumerate(_head_norms(q_all, masks))]

        def finished(laters):
            slack = laters[0] - needed[0]
            for h in range(1, ATT_HEADS):
                slack = jnp.minimum(slack, laters[h] - needed[h])
            return (jnp.min(slack) >= 0.0).astype(jnp.int32)

        def step(j, carry, diagonal):
            laters, acc = carry
            rows = pl.ds(pl.multiple_of(j * blk, blk), blk)
            kb = k_ref[rows, :]
            probs, new_laters = [], []
            for h in range(ATT_HEADS):
                z = _dot(qs[h], kb, NT)
                sp = _softplus(z)
                if diagonal:
                    sp = jnp.where(causal, sp, 0.0)
                a = jnp.exp(z - (_dot(sp.astype(BF16), tri) + laters[h]))
                if diagonal:
                    a = jnp.where(causal, a, 0.0)
                probs.append(a.astype(BF16))
                new_laters.append(laters[h] + jnp.sum(sp, axis=1, keepdims=True))
            v_heads = jnp.concatenate(_per_head(v_ref[rows, :], masks), axis=0)
            acc = acc + _dot(jnp.concatenate(probs, axis=1), v_heads)
            return tuple(new_laters), acc

        carry = (tuple(jnp.zeros((blk, 1), F32) for _ in range(ATT_HEADS)), jnp.zeros((blk, ATT_LANES), F32))
        laters, acc = step(i, carry, True)

        def further(state):
            n, _, laters, acc = state
            laters, acc = step(i - 1 - n, (laters, acc), False)
            return n + 1, finished(laters), laters, acc

        walked, _, laters, acc = lax.while_loop(
            lambda state: jnp.logical_and(state[0] < i, state[1] == 0), further,
            (jnp.int32(0), finished(laters), laters, acc))
        o_ref[...] = acc.astype(BF16)
        tot = jnp.zeros((blk, ATT_LANES), F32)
        for h in range(ATT_HEADS):
            tot = jnp.where(masks[h], laters[h], tot)
        tot_ref[...] = tot
        first_ref[...] = jnp.full(first_ref.shape, i - walked, jnp.int32).astype(F32)

    out_spec = pl.BlockSpec((blk, ATT_LANES), lambda h, i: (i, h))
    groups = N_HEADS // ATT_HEADS
    return _call(
        body, name=name, grid=(groups, nq),
        in_specs=_att_specs(s, blk),
        out_specs=[out_spec, out_spec, pl.BlockSpec((None, None, 8, 128), lambda h, i: (h, i, 0, 0))],
        out_shape=[jax.ShapeDtypeStruct((s, ATTN_W), BF16), jax.ShapeDtypeStruct((s, ATTN_W), F32),
                   jax.ShapeDtypeStruct((groups, nq, 8, 128), F32)],
        operands=(proj, proj, proj), scratch_shapes=[pltpu.VMEM((ATT_HEADS, 1, 1), F32)],
        semantics=("parallel", "arbitrary"), exchange=exchange)


def stick_breaking_bwd(proj, do, tot, first, name, exchange=None):
    s = proj.shape[0]
    blk = ATT_BLK
    nq = s // blk

    def body(q_ref, k_ref, v_ref, do_ref, tot_ref, first_ref, dq_ref, dk_ref, dv_ref):
        i = pl.program_id(1)
        start = jnp.clip(jnp.max(first_ref[...]).astype(jnp.int32), 0, i)

        @pl.when(i == 0)
        def _():
            dk_ref[...] = jnp.zeros_like(dk_ref)
            dv_ref[...] = jnp.zeros_like(dv_ref)

        row = lax.broadcasted_iota(jnp.int32, (blk, blk), 0)
        col = lax.broadcasted_iota(jnp.int32, (blk, blk), 1)
        before = (row < col).astype(BF16)
        upto = (row <= col).astype(BF16)
        causal = col < row
        masks = _head_masks(blk)
        qs = _per_head(q_ref[...] * ATTN_SCALE, masks)
        dos = _per_head(do_ref[...], masks)
        q_heads = jnp.concatenate(qs, axis=0)
        do_heads = jnp.concatenate(dos, axis=0)
        tot_all = tot_ref[...]
        totals = [jnp.max(jnp.where(m, tot_all, 0.0), axis=1, keepdims=True) for m in masks]

        def step(j, carry, diagonal):
            earliers, g_sums, dq = carry
            rows = pl.ds(pl.multiple_of(j * blk, blk), blk)
            kb = k_ref[rows, :]
            vb = v_ref[rows, :]
            probs, dzs, new_earliers, new_g_sums = [], [], [], []
            for h in range(ATT_HEADS):
                z = _dot(qs[h], kb, NT)
                sp = _softplus(z)
                if diagonal:
                    sp = jnp.where(causal, sp, 0.0)
                c = (totals[h] - earliers[h]) - _dot(sp.astype(BF16), before)
                a = jnp.exp(z - c)
                if diagonal:
                    a = jnp.where(causal, a, 0.0)
                g = a * _dot(dos[h], vb, NT)
                f = g_sums[h] + _dot(g.astype(BF16), upto)
                dz = g - jnp.exp(z - sp) * f
                if diagonal:
                    dz = jnp.where(causal, dz, 0.0)
                probs.append(a.astype(BF16))
                dzs.append(dz.astype(BF16))
                new_earliers.append(earliers[h] + jnp.sum(sp, axis=1, keepdims=True))
                new_g_sums.append(g_sums[h] + jnp.sum(g, axis=1, keepdims=True))
            k_heads = jnp.concatenate(_per_head(kb, masks), axis=0)
            dq = dq + _dot(jnp.concatenate(dzs, axis=1), k_heads)
            dk_ref[rows, :] += _dot(jnp.concatenate(dzs, axis=0), q_heads, TN)
            dv_ref[rows, :] += _dot(jnp.concatenate(probs, axis=0), do_heads, TN)
            return tuple(new_earliers), tuple(new_g_sums), dq

        zeros = tuple(jnp.zeros((blk, 1), F32) for _ in range(ATT_HEADS))
        carry = (zeros, zeros, jnp.zeros((blk, ATT_LANES), F32))
        carry = lax.fori_loop(start, i, lambda j, c: step(j, c, False), carry)
        dq = step(i, carry, True)[2]
        dq_ref[...] = (dq * ATTN_SCALE).astype(BF16)

    blk_spec = pl.BlockSpec((blk, ATT_LANES), lambda h, i: (i, h))
    full_spec = pl.BlockSpec((s, ATT_LANES), lambda h, i: (0, h))
    return _call(
        body, name=name, grid=(N_HEADS // ATT_HEADS, nq),
        in_specs=_att_specs(s, blk) + [blk_spec, blk_spec,
                                       pl.BlockSpec((None, None, 8, 128), lambda h, i: (h, i, 0, 0))],
        out_specs=[blk_spec, full_spec, full_spec],
        out_shape=[jax.ShapeDtypeStruct((s, ATTN_W), BF16), jax.ShapeDtypeStruct((s, ATTN_W), F32),
                   jax.ShapeDtypeStruct((s, ATTN_W), F32)],
        operands=(proj, proj, proj, do, tot, first), semantics=("parallel", "arbitrary"), exchange=exchange)


def adamw(w, m, v, parts, name):
    r, c = w.shape
    p = parts.shape[0]
    t = r
    for cand in (256, 176):
        if r % cand == 0 and r > cand:
            t = cand
            break

    def body(w_ref, m_ref, v_ref, p_ref, g_ref, d_ref, mo_ref, vo_ref):
        g = p_ref[0].astype(F32)
        for n in range(1, p):
            g = g + p_ref[n].astype(F32)
        m_new = ADAM_B1 * m_ref[...] + (1.0 - ADAM_B1) * g
        v_new = ADAM_B2 * v_ref[...] + (1.0 - ADAM_B2) * (g * g)
        m_hat = m_new / ADAM_BC1
        v_hat = v_new / ADAM_BC2
        g_ref[...] = g
        d_ref[...] = -ADAM_LR * (m_hat / (jnp.sqrt(v_hat) + ADAM_EPS) + ADAM_WD * w_ref[...])
        mo_ref[...] = m_new
        vo_ref[...] = v_new

    spec = pl.BlockSpec((t, c), lambda i: (i, 0))
    out = jax.ShapeDtypeStruct((r, c), F32)
    return pl.pallas_call(
        body, name=name, grid=(r // t,),
        in_specs=[spec, spec, spec, pl.BlockSpec((p, t, c), lambda i: (0, i, 0))],
        out_specs=[spec] * 4, out_shape=[out] * 4,
        compiler_params=_params("parallel"),
    )(w, m, v, parts)


def kernel(x, c, w_ada, b_ada, norm1_g, ffn1_w_gu, ffn1_w_down, norm2_g, w_mix_in, b_merge, conv_w, w_conv_out, w_attn_out, w_out, norm3_g, ffn2_w_gu, ffn2_w_down, final_g, loss_target, m_w_ada, m_b_ada, m_norm1_g, m_ffn1_w_gu, m_ffn1_w_down, m_norm2_g, m_w_mix_in, m_b_merge, m_conv_w, m_w_conv_out, m_w_attn_out, m_w_out, m_norm3_g, m_ffn2_w_gu, m_ffn2_w_down, m_final_g, v_w_ada, v_b_ada, v_norm1_g, v_ffn1_w_gu, v_ffn1_w_down, v_norm2_g, v_w_mix_in, v_b_merge, v_conv_w, v_w_conv_out, v_w_attn_out, v_w_out, v_norm3_g, v_ffn2_w_gu, v_ffn2_w_down, v_final_g):
    s = x.shape[1]
    me = 4 * lax.axis_index("x") + 2 * lax.axis_index("y") + lax.axis_index("c")
    x0 = x[0]
    target = loss_target[0]
    final_g2 = final_g.reshape(1, D_MODEL)

    def shard(w):
        return w[0].astype(BF16)

    def flipped(w):
        return jnp.swapaxes(w, 1, 2)

    def rows8(g):
        return g.reshape(N_DEV, -1, D_MODEL)

    small_in = jnp.concatenate([c.reshape(-1), b_merge.reshape(-1), conv_w.reshape(-1),
                                jnp.zeros((64,), F32)]).reshape(1, -1)
    n_ada = w_ada.shape[2]
    b_cols = lax.dynamic_slice(b_ada, (0, me * n_ada), (1, n_ada))
    small_all, mod_all, (wgu1, wd1) = prologue(small_in, w_ada[0], b_cols,
                                               [shard(flipped(ffn1_w_gu)), shard(ffn1_w_down)], "prologue")
    wd1 = wd1.reshape(4, FF_BLK, D_MODEL)
    small_all = small_all[:, 0, :]
    c_all = small_all[:, :D_MODEL]
    bm_full = small_all[:, 1024:1280].reshape(8, 2, 128).transpose(1, 0, 2).reshape(2, D_MODEL)
    cw_full = small_all[:, 1280:1472].reshape(8, 3, 64).transpose(1, 0, 2).reshape(3, CONV_W)
    mod = lax.dynamic_index_in_dim(mod_all, me, axis=1, keepdims=False).reshape(9, 1, D_MODEL)
    sh1, sc1, gt1, sh2, sc2, gt2, sh3, sc3, gt3 = [mod[n] for n in range(9)]

    (u1, gu1, act1), got = ffn_up(x0, norm1_g, sh1, sc1, wgu1, "ffn_up_1",
                                  exchange=gather_stage1([shard(w_mix_in)]))
    (x1, y1), (wmix, *got) = residual_matmul(
        act1, wd1, x0, gt1, 0.5, "ffn_down_1", exchange=merge_exchanges(
            gather_stage2(got), gather_stage1([shard(w_conv_out), shard(w_attn_out), shard(w_out)])))

    (u2, proj), (wco, wao, wout) = mix_in_proj(x1, norm2_g, sh2, sc2, wmix, "mix_in",
                                               exchange=gather_stage2(got))
    wco = wco.transpose(1, 0, 2).reshape(CONV_W, D_MODEL)
    wao = wao.transpose(1, 0, 2).reshape(ATTN_W, D_MODEL)
    wout = wout.reshape(D_MODEL, D_MODEL)
    sa = short_conv(proj, cw_full, "short_conv")
    (o, tot, first), got = stick_breaking_fwd(proj, "attn_fwd",
                                       exchange=gather_stage1([shard(flipped(ffn2_w_gu)), shard(ffn2_w_down)]))
    (merged, ya, yb, y2, x2), (wgu3, wd3) = merge_forward(sa, o, proj, wco, wao, bm_full, wout, x1, gt2, "merge",
                                                          exchange=gather_stage2(got))
    wd3 = wd3.reshape(4, FF_BLK, D_MODEL)

    u3, gu3, act3 = ffn_up(x2, norm3_g, sh3, sc3, wgu3, "ffn_up_3")

    dx3, dy3, dgt3, dfinal, sq = residual_matmul_loss(act3, wd3, x2, gt3, 0.5, target, final_g2, "ffn_down_3_loss")
    dgu3, du3 = ffn_tokens_bwd(dy3, wd3, gu3, wgu3, "ffn_bwd_3")
    dgu3 = dgu3.reshape(8, s, FF_BLK)
    g_wd3 = rows8(matmul_tn(act3, dy3[None], "grad_w_down_3", group=(4, 1)))
    g_wgu3 = matmul_tn(dgu3, u3[None], "grad_w_gu_3", group=(4, 1)).reshape(8, FF_BLK, D_MODEL)
    dx2, dsh3, dsc3, dn3, dy2, dgt2 = norm_modulate_bwd(du3, x2, dx3, norm3_g, sc3, "norm_bwd_3",
                                                        prev=(gt2, y2, 1.0))

    (dya, dyb, dga, dgb, dbm), pairs = merge_backward(dy2, wout, proj, ya, yb, bm_full, "merge_bwd",
                                                      exchange=scatter_stage1([g_wgu3, g_wd3]))
    sums3 = [pair_sum(g_wgu3, pairs[0], "pair_sum_w_gu_3"), pair_sum(g_wd3, pairs[1], "pair_sum_w_down_3")]
    g_wout = rows8(matmul_tn(merged[None], dy2[None], "grad_w_out"))
    dsa, do = out_proj_bwd(dya, dyb, wco, wao, "out_proj_bwd")
    g_wco = matmul_tn(sa[None], dya[None], "grad_w_conv_out").reshape(CONV_W, N_DEV, 128).transpose(1, 0, 2)
    g_wao = matmul_tn(o[None], dyb[None], "grad_w_attn_out").reshape(ATTN_W, N_DEV, 128).transpose(1, 0, 2)
    dcb, dcc, dcx, dconv = short_conv_bwd(dsa, proj, cw_full, "short_conv_bwd")
    (dq, dk, dv), landed3 = stick_breaking_bwd(proj, do, tot, first, "attn_bwd", exchange=scatter_stage2(sums3))
    dproj = jnp.concatenate([dcb, dcc, dcx, dq, dk.astype(BF16), dv.astype(BF16), dga, dgb], axis=1)
    g_wmix = matmul_tn(u2[None], dproj, "grad_w_mix_in", group=(1, 4), b_cols=MIX_BLK).reshape(
        N_DEV, D_MODEL, MIX_BLK)
    mixer_grads = [g_wmix, g_wco, g_wao, g_wout]
    (dx1, dsh2, dsc2, dn2, dy1, dgt1), pairs = norm_modulate_bwd(
        (dproj, wmix), x1, dx2, norm2_g, sc2, "norm_bwd_2", prev=(gt1, y1, 0.5),
        exchange=scatter_stage1(mixer_grads))
    sums_mix = [pair_sum(g, p, f"pair_sum_mixer_{n}") for n, (g, p) in enumerate(zip(mixer_grads, pairs))]

    dgu1, landed_mix = ffn_tokens_bwd(dy1, wd1, gu1, None, "ffn_dact_1", exchange=scatter_stage2(sums_mix[:1]))
    dgu1 = dgu1.reshape(8, s, FF_BLK)
    g_wgu1, landed_small = matmul_tn(dgu1, u1[None], "grad_w_gu_1", group=(4, 1),
                                     exchange=scatter_stage2(sums_mix[1:]))
    g_wgu1 = g_wgu1.reshape(8, FF_BLK, D_MODEL)
    g_wd1, pairs = matmul_tn(act1, dy1[None], "grad_w_down_1", group=(4, 1), exchange=scatter_stage1([g_wgu1]))
    g_wd1 = rows8(g_wd1)
    sum_gu1 = pair_sum(g_wgu1, pairs[0], "pair_sum_w_gu_1")
    du1, (landed_gu1, pair_d1) = matmul_nt_acc(
        dgu1, wgu1, "ffn_du_1", b_dims=NN, exchange=merge_exchanges(scatter_stage2([sum_gu1]), scatter_stage1([g_wd1])))
    sum_d1 = pair_sum(g_wd1, pair_d1, "pair_sum_w_down_1")
    (grad_x, dsh1, dsc1, dn1), landed_d1 = norm_modulate_bwd(du1, x0, dx1, norm1_g, sc1, "norm_bwd_1",
                                                            exchange=scatter_stage2([sum_d1]))

    loss_local = (0.5 / D_MODEL) * jnp.sum(sq)
    stats = jnp.concatenate(
        [v.reshape(-1) for v in (dsh1, dsc1, dgt1, dsh2, dsc2, dgt2, dsh3, dsc3, dgt3,
                                 dn1, dn2, dn3, dfinal, dbm, dconv)]
        + [jnp.broadcast_to(loss_local, (128,))]).reshape(1, -1)
    stats_all = all_gather_rows(stats, "gather_stats")
    n_mod = 9 * D_MODEL
    loss = jnp.sum(stats_all[:, 0, -1])
    dmod_all = stats_all[:, :, :n_mod]
    off = n_mod
    parts = {}
    for key in ("norm1_g", "norm2_g", "norm3_g", "final_g"):
        parts[key] = stats_all[:, :, off:off + D_MODEL]
        off += D_MODEL
    dbm_all = stats_all[:, 0, off:off + 2 * D_MODEL].reshape(N_DEV, 2, D_MODEL)
    off += 2 * D_MODEL
    dcw_all = stats_all[:, 0, off:off + 3 * CONV_W].reshape(N_DEV, 3, CONV_W)
    parts["b_merge"] = lax.dynamic_slice(dbm_all, (0, 0, me * 128), (N_DEV, 2, 128))
    parts["conv_w"] = lax.dynamic_slice(dcw_all, (0, 0, me * 64), (N_DEV, 3, 64))
    dmod_cols = lax.dynamic_slice(dmod_all[:, 0, :], (0, me * n_ada), (N_DEV, n_ada))
    parts["w_ada"] = ada_backward(c_all, dmod_cols, "ada_backward")[None]
    parts["b_ada"] = dmod_all
    parts["ffn2_w_gu"], parts["ffn2_w_down"] = landed3
    parts["w_mix_in"] = landed_mix[0]
    parts["w_conv_out"], parts["w_attn_out"], parts["w_out"] = landed_small
    parts["ffn1_w_gu"] = landed_gu1
    parts["ffn1_w_down"] = landed_d1[0]

    given = dict(w_ada=w_ada, b_ada=b_ada, norm1_g=norm1_g, ffn1_w_gu=ffn1_w_gu, ffn1_w_down=ffn1_w_down,
                 norm2_g=norm2_g, w_mix_in=w_mix_in, b_merge=b_merge, conv_w=conv_w, w_conv_out=w_conv_out,
                 w_attn_out=w_attn_out, w_out=w_out, norm3_g=norm3_g, ffn2_w_gu=ffn2_w_gu,
                 ffn2_w_down=ffn2_w_down, final_g=final_g)
    moments_m = dict(w_ada=m_w_ada, b_ada=m_b_ada, norm1_g=m_norm1_g, ffn1_w_gu=m_ffn1_w_gu,
                     ffn1_w_down=m_ffn1_w_down, norm2_g=m_norm2_g, w_mix_in=m_w_mix_in, b_merge=m_b_merge,
                     conv_w=m_conv_w, w_conv_out=m_w_conv_out, w_attn_out=m_w_attn_out, w_out=m_w_out,
                     norm3_g=m_norm3_g, ffn2_w_gu=m_ffn2_w_gu, ffn2_w_down=m_ffn2_w_down, final_g=m_final_g)
    moments_v = dict(w_ada=v_w_ada, b_ada=v_b_ada, norm1_g=v_norm1_g, ffn1_w_gu=v_ffn1_w_gu,
                     ffn1_w_down=v_ffn1_w_down, norm2_g=v_norm2_g, w_mix_in=v_w_mix_in, b_merge=v_b_merge,
                     conv_w=v_conv_w, w_conv_out=v_w_conv_out, w_attn_out=v_w_attn_out, w_out=v_w_out,
                     norm3_g=v_norm3_g, ffn2_w_gu=v_ffn2_w_gu, ffn2_w_down=v_ffn2_w_down, final_g=v_final_g)
    order = ["w_ada", "b_ada", "norm1_g", "ffn1_w_gu", "ffn1_w_down", "norm2_g", "w_mix_in", "b_merge",
             "conv_w", "w_conv_out", "w_attn_out", "w_out", "norm3_g", "ffn2_w_gu", "ffn2_w_down", "final_g"]
    grads, deltas, new_m, new_v = [], [], [], []
    for key in order:
        turn = flipped if key in ("ffn1_w_gu", "ffn2_w_gu") else (lambda a: a)
        shape = turn(given[key]).shape
        shape2 = (1, shape[0]) if len(shape) == 1 else shape[-2:]
        outs = adamw(turn(given[key]).reshape(shape2), turn(moments_m[key]).reshape(shape2),
                     turn(moments_v[key]).reshape(shape2), parts[key], f"adamw_{key}")
        for dst, val in zip((grads, deltas, new_m, new_v), outs):
            dst.append(turn(val.reshape(shape)))

    return (loss, grad_x[None], *grads, *deltas, *new_m, *new_v)
```

```python
import functools
from typing import Callable, NamedTuple

import jax
import jax.numpy as jnp
from jax import lax
from jax.experimental import pallas as pl
from jax.experimental.pallas import tpu as pltpu

F32 = jnp.float32
BF16 = jnp.bfloat16
MESH = pl.DeviceIdType.MESH
ANY = pl.BlockSpec(memory_space=pl.ANY)

N_DEV = 8
D_MODEL = 1024
D_FF = 2816
FF_BLK = D_FF // 4
N_HEADS = 8
HEAD_DIM = 64
CONV_W = 512
ATTN_W = 512
MIX_W = 3 * CONV_W + 3 * ATTN_W + 2 * D_MODEL
MIX_BLK = MIX_W // N_DEV
EPS = 1e-6
ATTN_SCALE = HEAD_DIM ** -0.5

ADAM_LR = 0.001
ADAM_B1 = 0.9
ADAM_B2 = 0.999
ADAM_EPS = 1e-08
ADAM_WD = 0.01
ADAM_STEP = 10
ADAM_BC1 = 1.0 - ADAM_B1 ** ADAM_STEP
ADAM_BC2 = 1.0 - ADAM_B2 ** ADAM_STEP

VMEM_LIMIT = 56 * 1024 * 1024
ROW_TILE = 512
ACC_TILE = 1024
ELT_TILE = 256
ATT_BLK = 256

NN = (((1,), (0,)), ((), ()))
NT = (((1,), (1,)), ((), ()))
TN = (((0,), (0,)), ((), ()))


def _dot(a, b, dims=NN):
    return lax.dot_general(a, b, dims, preferred_element_type=F32)


def _params(*sem):
    return pltpu.CompilerParams(dimension_semantics=sem, vmem_limit_bytes=VMEM_LIMIT)


def _sigmoid(x):
    return 1.0 / (1.0 + jnp.exp(-x))


def _me():
    x, y, c = lax.axis_index("x"), lax.axis_index("y"), lax.axis_index("c")
    return x, y, c, 4 * x + 2 * y + c


def _peer(k):
    x, y, c, _ = _me()
    px = 1 - x if (k >> 2) & 1 else x
    py = 1 - y if (k >> 1) & 1 else y
    pc = 1 - c if k & 1 else c
    return (px, py, pc), 4 * px + 2 * py + pc


class Exchange(NamedTuple):
    operands: tuple
    out_shapes: tuple
    aliases: dict
    n_remote: int
    n_local: int
    copies: Callable


CHIP_FLIPS = (2, 4, 6)
SIBLING = 1


def _remote(src, dst, send_sems, recv_sems, n, peer):
    return pltpu.make_async_remote_copy(src_ref=src, dst_ref=dst, send_sem=send_sems.at[n], recv_sem=recv_sems.at[n],
                                        device_id=peer, device_id_type=MESH)


def gather_stage1(shards):
    n = len(shards)
    rels = (SIBLING,) + CHIP_FLIPS

    def copies(ins, outs, send_sems, recv_sems, local_sems, rb, lb):
        _, _, _, me = _me()
        cps = []
        for w in range(n):
            cps.append(pltpu.make_async_copy(ins[w], outs[w].at[me], local_sems.at[lb + w]))
            for a, k in enumerate(rels):
                peer, _ = _peer(k)
                cps.append(_remote(ins[w], outs[w].at[me], send_sems, recv_sems, rb + len(rels) * w + a, peer))
        return cps

    shapes = tuple(jax.ShapeDtypeStruct((N_DEV,) + s.shape, s.dtype) for s in shards)
    return Exchange(tuple(shards), shapes, {}, len(rels) * n, n, copies)


def gather_stage2(fulls):
    n = len(fulls)

    def copies(ins, outs, send_sems, recv_sems, local_sems, rb, lb):
        sibling, _ = _peer(SIBLING)
        cps = []
        for w in range(n):
            for a, k in enumerate(CHIP_FLIPS):
                _, blk = _peer(k)
                cps.append(_remote(outs[w].at[blk], outs[w].at[blk], send_sems, recv_sems, rb + 3 * w + a, sibling))
        return cps

    shapes = tuple(jax.ShapeDtypeStruct(f.shape, f.dtype) for f in fulls)
    return Exchange(tuple(fulls), shapes, {w: w for w in range(n)}, 3 * n, 0, copies)


def scatter_stage1(fulls):
    n = len(fulls)

    def copies(ins, outs, send_sems, recv_sems, local_sems, rb, lb):
        _, _, c, _ = _me()
        sibling, _ = _peer(SIBLING)
        cps = []
        for w in range(n):
            for q in range(4):
                cps.append(_remote(ins[w].at[2 * q + (1 - c)], outs[w].at[q], send_sems, recv_sems, rb + 4 * w + q, sibling))
        return cps

    shapes = tuple(jax.ShapeDtypeStruct((4,) + f.shape[1:], f.dtype) for f in fulls)
    return Exchange(tuple(fulls), shapes, {}, 4 * n, 0, copies)


def scatter_stage2(sums):
    n = len(sums)

    def copies(ins, outs, send_sems, recv_sems, local_sems, rb, lb):
        x, y, _, _ = _me()
        mine = 2 * x + y
        cps = []
        for w in range(n):
            cps.append(pltpu.make_async_copy(ins[w].at[mine], outs[w].at[mine], local_sems.at[lb + w]))
            for a, k in enumerate(CHIP_FLIPS):
                peer, _ = _peer(k)
                cps.append(_remote(ins[w].at[2 * peer[0] + peer[1]], outs[w].at[mine], send_sems, recv_sems,
                                   rb + 3 * w + a, peer))
        return cps

    shapes = tuple(jax.ShapeDtypeStruct(s.shape, s.dtype) for s in sums)
    return Exchange(tuple(sums), shapes, {}, 3 * n, n, copies)


def merge_exchanges(a, b):
    na_in, na_out = len(a.operands), len(a.out_shapes)

    def copies(ins, outs, send_sems, recv_sems, local_sems, rb, lb):
        return (a.copies(ins[:na_in], outs[:na_out], send_sems, recv_sems, local_sems, rb, lb)
                + b.copies(ins[na_in:], outs[na_out:], send_sems, recv_sems, local_sems, rb + a.n_remote, lb + a.n_local))

    aliases = dict(a.aliases)
    aliases.update({na_in + i: na_out + o for i, o in b.aliases.items()})
    return Exchange(a.operands + b.operands, a.out_shapes + b.out_shapes, aliases,
                    a.n_remote + b.n_remote, a.n_local + b.n_local, copies)


def _exchange_scratch(ex):
    return [pltpu.SemaphoreType.DMA((ex.n_remote,)), pltpu.SemaphoreType.DMA((ex.n_remote,)),
            pltpu.SemaphoreType.DMA((max(ex.n_local, 1),))]


def _call(body, *, name, grid, in_specs, out_specs, out_shape, operands, scratch_shapes=(), semantics=(),
          exchange=None):
    if exchange is None:
        return pl.pallas_call(
            body, name=name, grid=grid, in_specs=in_specs, out_specs=out_specs, out_shape=out_shape,
            scratch_shapes=list(scratch_shapes), compiler_params=_params(*semantics))(*operands)
    single = not isinstance(out_shape, (list, tuple))
    out_shapes = [out_shape] if single else list(out_shape)
    out_specs_l = [out_specs] if single else list(out_specs)
    n_in, n_out, n_scr = len(operands), len(out_shapes), len(scratch_shapes)
    x_in, x_out = len(exchange.operands), len(exchange.out_shapes)

    def hosted(*refs):
        ins, refs = refs[:n_in], refs[n_in:]
        xin, refs = refs[:x_in], refs[x_in:]
        outs, refs = refs[:n_out], refs[n_out:]
        xout, refs = refs[:x_out], refs[x_out:]
        scr, sems = refs[:n_scr], refs[n_scr:]
        first = functools.reduce(jnp.logical_and, [pl.program_id(a) == 0 for a in range(len(grid))])
        last = functools.reduce(jnp.logical_and, [pl.program_id(a) == g - 1 for a, g in enumerate(grid)])

        @pl.when(first)
        def _():
            for cp in exchange.copies(xin, xout, *sems, 0, 0):
                cp.start()

        body(*ins, *outs, *scr)

        @pl.when(last)
        def _():
            for cp in exchange.copies(xin, xout, *sems, 0, 0):
                cp.wait()

    res = pl.pallas_call(
        hosted, name=name, grid=grid,
        in_specs=list(in_specs) + [ANY] * x_in, out_specs=out_specs_l + [ANY] * x_out,
        out_shape=out_shapes + list(exchange.out_shapes),
        scratch_shapes=list(scratch_shapes) + _exchange_scratch(exchange),
        input_output_aliases={n_in + i: n_out + o for i, o in exchange.aliases.items()},
        compiler_params=_params(*(["arbitrary"] * len(grid))),
    )(*operands, *exchange.operands)
    outs, xouts = res[:n_out], res[n_out:]
    return (outs[0] if single else outs), xouts


def all_gather_rows(v, name):
    r, n = v.shape

    def body(v_ref, out_ref, send_sems, recv_sems):
        _, _, _, me = _me()
        out_ref[me] = v_ref[...]
        copies = []
        for k in range(1, N_DEV):
            peer, _ = _peer(k)
            copies.append(_remote(v_ref, out_ref.at[me], send_sems, recv_sems, k - 1, peer))
        for cp in copies:
            cp.start()
        for cp in copies:
            cp.wait()

    return pl.pallas_call(
        body, name=name,
        out_shape=jax.ShapeDtypeStruct((N_DEV, r, n), v.dtype),
        in_specs=[pl.BlockSpec(memory_space=pltpu.VMEM)],
        out_specs=pl.BlockSpec(memory_space=pltpu.VMEM),
        scratch_shapes=[pltpu.SemaphoreType.DMA((N_DEV - 1,)), pltpu.SemaphoreType.DMA((N_DEV - 1,))],
    )(v)


def pair_sum(full, pair, name):
    _, r, c = full.shape
    t = r
    core = lax.axis_index("c").astype(jnp.int32).reshape(1)

    def body(core_ref, f_ref, p_ref, o_ref):
        o_ref[...] = (f_ref[...].astype(F32) + p_ref[...].astype(F32)).astype(BF16)

    return pl.pallas_call(
        body, name=name,
        grid_spec=pltpu.PrefetchScalarGridSpec(
            num_scalar_prefetch=1, grid=(4, r // t),
            in_specs=[pl.BlockSpec((None, None, t, c), lambda q, i, core_ref: (q, core_ref[0], i, 0)),
                      pl.BlockSpec((None, t, c), lambda q, i, core_ref: (q, i, 0))],
            out_specs=pl.BlockSpec((None, t, c), lambda q, i, core_ref: (q, i, 0))),
        out_shape=jax.ShapeDtypeStruct((4, r, c), BF16),
        compiler_params=_params("parallel", "parallel"),
    )(core, full.reshape(4, 2, r, c), pair)


def prologue(small_in, w_ada, b_cols, shards, name):
    ex = gather_stage1(shards)
    n_sh = len(shards)
    n_small = small_in.shape[1]
    cols = w_ada.shape[1]

    def body(*refs):
        small_ref, w_ref, b_ref = refs[:3]
        shard_refs = refs[3:3 + n_sh]
        small_out, mod_out = refs[3 + n_sh:5 + n_sh]
        fulls = refs[5 + n_sh:5 + 2 * n_sh]
        part_ref, send1, recv1, send2, recv2, wsend, wrecv, wlocal, fsend, frecv = refs[5 + 2 * n_sh:]
        _, _, _, me = _me()
        def start_gather(src_ref, dst_ref, send_sems, recv_sems):
            cps = [_remote(src_ref, dst_ref.at[me], send_sems, recv_sems, k - 1, _peer(k)[0]) for k in range(1, N_DEV)]
            for cp in cps:
                cp.start()
            return cps

        small_out[me] = small_ref[...]
        first = start_gather(small_ref, small_out, send1, recv1)
        big = ex.copies(shard_refs, fulls, wsend, wrecv, wlocal, 0, 0)
        for cp in big:
            cp.start()
        for cp in first:
            cp.wait()
        c_all = jnp.concatenate([small_out[d][:, :D_MODEL] for d in range(N_DEV)], axis=0)
        act = c_all * _sigmoid(c_all)
        part_ref[...] = jnp.dot(act, w_ref[...], precision=lax.Precision.HIGHEST,
                                preferred_element_type=F32) + b_ref[...]
        mod_out[me] = part_ref[...]
        for cp in start_gather(part_ref, mod_out, send2, recv2):
            cp.wait()
        per = 2 + len(CHIP_FLIPS)
        sibling, _ = _peer(SIBLING)
        onward = []
        for w in range(n_sh):
            for a, k in enumerate(CHIP_FLIPS):
                _, blk = _peer(k)
                big[per * w + 2 + a].wait_recv()
                cp = _remote(fulls[w].at[blk], fulls[w].at[blk], fsend, frecv, len(CHIP_FLIPS) * w + a, sibling)
                cp.start()
                onward.append(cp)
        for w in range(n_sh):
            big[per * w].wait()
            big[per * w + 1].wait()
            for a in range(len(CHIP_FLIPS)):
                big[per * w + 2 + a].wait_send()
        for cp in onward:
            cp.wait()

    vmem = pl.BlockSpec(memory_space=pltpu.VMEM)
    sems = pltpu.SemaphoreType.DMA((N_DEV - 1,))
    res = pl.pallas_call(
        body, name=name,
        out_shape=[jax.ShapeDtypeStruct((N_DEV, 1, n_small), F32), jax.ShapeDtypeStruct((N_DEV, N_DEV, cols), F32)]
        + list(ex.out_shapes),
        in_specs=[vmem, vmem, vmem] + [ANY] * n_sh, out_specs=[vmem, vmem] + [ANY] * n_sh,
        scratch_shapes=[pltpu.VMEM((N_DEV, cols), F32), sems, sems, sems, sems] + _exchange_scratch(ex)
        + [pltpu.SemaphoreType.DMA((len(CHIP_FLIPS) * n_sh,)), pltpu.SemaphoreType.DMA((len(CHIP_FLIPS) * n_sh,))],
        compiler_params=pltpu.CompilerParams(vmem_limit_bytes=VMEM_LIMIT),
    )(small_in, w_ada, b_cols, *shards)
    return res[0], res[1], res[2:]


def ada_backward(c_all, dmod_cols, name):
    n = dmod_cols.shape[1]

    def body(c_ref, d_ref, o_ref):
        c = c_ref[...]
        act = c * _sigmoid(c)
        o_ref[...] = lax.dot_general(act, d_ref[...], TN, precision=lax.Precision.HIGHEST,
                                     preferred_element_type=F32)

    return pl.pallas_call(
        body, name=name, out_shape=jax.ShapeDtypeStruct((D_MODEL, n), F32),
        compiler_params=pltpu.CompilerParams(vmem_limit_bytes=VMEM_LIMIT),
    )(c_all, dmod_cols)


def _row_spec(t, width=D_MODEL):
    return pl.BlockSpec((t, width), lambda i: (i, 0))


def _vec_spec(rows=1, width=D_MODEL):
    return pl.BlockSpec((rows, width), lambda i: (0, 0))


def _resident(shape):
    return pl.BlockSpec(shape, lambda i: (0,) * len(shape), pipeline_mode=pl.Buffered(1))


def _norm_modulate(x_ref, g_ref, shift_ref, scale_ref):
    xv = x_ref[...]
    r = lax.rsqrt(jnp.mean(xv * xv, axis=-1, keepdims=True) + EPS)
    a = (xv * r) * g_ref[...]
    return (a * (1.0 + scale_ref[...]) + shift_ref[...]).astype(BF16)


def norm_modulate_bwd(du, x, dx_out, g, scale, name, prev=None, exchange=None):
    s = x.shape[0]
    factors = isinstance(du, tuple)
    t = min(ROW_TILE if factors else ELT_TILE, s)
    has_prev = prev is not None

    def body(*refs):
        if factors:
            a_ref, b_ref = refs[:2]
            refs = refs[1:]
            nk, _, n = b_ref.shape
        du_ref, x_ref, dxo_ref, g_ref, sc_ref = refs[:5]
        refs = refs[5:]
        if has_prev:
            gt_ref, y_ref = refs[:2]
            refs = refs[2:]
        dx_ref, dsh_ref, dsc_ref, dg_ref = refs[:4]

        @pl.when(pl.program_id(0) == 0)
        def _():
            dsh_ref[...] = jnp.zeros_like(dsh_ref)
            dsc_ref[...] = jnp.zeros_like(dsc_ref)
            dg_ref[...] = jnp.zeros_like(dg_ref)
            if has_prev:
                refs[5][...] = jnp.zeros_like(refs[5])

        xv = x_ref[...]
        if factors:
            duv = None
            for k in range(0, nk, 2):
                pair = jnp.concatenate([b_ref[k], b_ref[k + 1]], axis=1)
                part = _dot(a_ref[:, k * n:(k + 2) * n], pair, NT)
                duv = part if duv is None else duv + part
        else:
            duv = du_ref[...]
        gv = g_ref[...]
        r = lax.rsqrt(jnp.mean(xv * xv, axis=-1, keepdims=True) + EPS)
        nrm = xv * r
        a = nrm * gv
        dsh_ref[...] += jnp.sum(duv, axis=0, keepdims=True)
        dsc_ref[...] += jnp.sum(duv * a, axis=0, keepdims=True)
        da = duv * (1.0 + sc_ref[...])
        dg_ref[...] += jnp.sum(da * nrm, axis=0, keepdims=True)
        dn = da * gv
        dx = dxo_ref[...] + r * (dn - nrm * jnp.mean(dn * nrm, axis=-1, keepdims=True))
        dx_ref[...] = dx
        if has_prev:
            coef = prev[2]
            refs[4][...] = (coef * gt_ref[...] * dx).astype(BF16)
            refs[5][...] += coef * jnp.sum(dx * y_ref[...].astype(F32), axis=0, keepdims=True)

    vec = jax.ShapeDtypeStruct((1, D_MODEL), F32)
    if factors:
        operands = [du[0], du[1], x, dx_out, g, scale]
        in_specs = [_row_spec(t, du[0].shape[1]), _resident(du[1].shape)]
    else:
        operands = [du, x, dx_out, g, scale]
        in_specs = [_row_spec(t)]
    in_specs += [_row_spec(t), _row_spec(t), _vec_spec(), _vec_spec()]
    out_specs = [_row_spec(t), _vec_spec(), _vec_spec(), _vec_spec()]
    out_shape = [jax.ShapeDtypeStruct((s, D_MODEL), F32), vec, vec, vec]
    if has_prev:
        operands += [prev[0], prev[1]]
        in_specs += [_vec_spec(), _row_spec(t)]
        out_specs += [_row_spec(t), _vec_spec()]
        out_shape += [jax.ShapeDtypeStruct((s, D_MODEL), BF16), vec]
    return _call(body, name=name, grid=(s // t,), in_specs=in_specs, out_specs=out_specs, out_shape=out_shape,
                 operands=operands, semantics=("arbitrary",), exchange=exchange)


def ffn_up(x, norm_g, shift, scale, w_gu_t, name, exchange=None):
    s = x.shape[0]
    t = min(ROW_TILE, s)

    def body(x_ref, g_ref, sh_ref, sc_ref, w_ref, u_ref, gu_ref, act_ref):
        uv = _norm_modulate(x_ref, g_ref, sh_ref, sc_ref)
        u_ref[...] = uv
        for j in range(4):
            g = _dot(uv, w_ref[j], NT)
            up = _dot(uv, w_ref[j + 4], NT)
            gu_ref[0, j] = g.astype(BF16)
            gu_ref[1, j] = up.astype(BF16)
            act_ref[j] = (g * _sigmoid(g) * up).astype(BF16)

    return _call(
        body, name=name, grid=(s // t,),
        in_specs=[_row_spec(t), _vec_spec(), _vec_spec(), _vec_spec(), _resident(w_gu_t.shape)],
        out_specs=[_row_spec(t), pl.BlockSpec((2, 4, t, FF_BLK), lambda i: (0, 0, i, 0)),
                   pl.BlockSpec((4, t, FF_BLK), lambda i: (0, i, 0))],
        out_shape=[jax.ShapeDtypeStruct((s, D_MODEL), BF16), jax.ShapeDtypeStruct((2, 4, s, FF_BLK), BF16),
                   jax.ShapeDtypeStruct((4, s, FF_BLK), BF16)],
        operands=(x, norm_g, shift, scale, w_gu_t), semantics=("parallel",), exchange=exchange)


def residual_matmul(a, b, x, gate, coef, name, exchange=None):
    nk, s, kb = a.shape
    t = min(ROW_TILE, s)

    def body(a_ref, b_ref, x_ref, gt_ref, xo_ref, y_ref):
        y = _dot(a_ref[0], b_ref[0])
        for k in range(1, nk):
            y = y + _dot(a_ref[k], b_ref[k])
        y_ref[...] = y.astype(BF16)
        xo_ref[...] = x_ref[...] + coef * gt_ref[...] * y

    return _call(
        body, name=name, grid=(s // t,),
        in_specs=[pl.BlockSpec((nk, t, kb), lambda i: (0, i, 0)),
                  pl.BlockSpec((nk, kb, D_MODEL), lambda i: (0, 0, 0)),
                  _row_spec(t), _vec_spec()],
        out_specs=[_row_spec(t), _row_spec(t)],
        out_shape=[jax.ShapeDtypeStruct((s, D_MODEL), F32), jax.ShapeDtypeStruct((s, D_MODEL), BF16)],
        operands=(a, b, x, gate), semantics=("parallel",), exchange=exchange)


def residual_matmul_loss(a, b, x, gate, coef, target, final_g, name):
    nk, s, kb = a.shape
    t = min(ROW_TILE, s)

    def body(a_ref, b_ref, x_ref, gt_ref, t_ref, fg_ref, dx_ref, dy_ref, dgt_ref, dfg_ref, sq_ref):
        @pl.when(pl.program_id(0) == 0)
        def _():
            dgt_ref[...] = jnp.zeros_like(dgt_ref)
            dfg_ref[...] = jnp.zeros_like(dfg_ref)
            sq_ref[...] = jnp.zeros_like(sq_ref)

        y = _dot(a_ref[0], b_ref[0])
        for k in range(1, nk):
            y = y + _dot(a_ref[k], b_ref[k])
        gt = gt_ref[...]
        fg = fg_ref[...]
        xv = x_ref[...] + coef * gt * y
        r = lax.rsqrt(jnp.mean(xv * xv, axis=-1, keepdims=True) + EPS)
        nrm = xv * r
        err = nrm * fg - t_ref[...]
        sq_ref[...] += jnp.sum(err * err, axis=0, keepdims=True)
        dout = err * (1.0 / D_MODEL)
        dfg_ref[...] += jnp.sum(dout * nrm, axis=0, keepdims=True)
        dn = dout * fg
        dx = r * (dn - nrm * jnp.mean(dn * nrm, axis=-1, keepdims=True))
        dx_ref[...] = dx
        dy_ref[...] = (coef * gt * dx).astype(BF16)
        dgt_ref[...] += coef * jnp.sum(dx * y, axis=0, keepdims=True)

    vec = jax.ShapeDtypeStruct((1, D_MODEL), F32)
    return pl.pallas_call(
        body, name=name, grid=(s // t,),
        in_specs=[pl.BlockSpec((nk, t, kb), lambda i: (0, i, 0)), _resident(b.shape),
                  _row_spec(t), _vec_spec(), _row_spec(t), _vec_spec()],
        out_specs=[_row_spec(t), _row_spec(t), _vec_spec(), _vec_spec(), _vec_spec()],
        out_shape=[jax.ShapeDtypeStruct((s, D_MODEL), F32), jax.ShapeDtypeStruct((s, D_MODEL), BF16), vec, vec, vec],
        compiler_params=_params("arbitrary"),
    )(a, b, x, gate, target, final_g)


def ffn_tokens_bwd(dy, w_down, gu, w_gu_t, name, exchange=None):
    s = dy.shape[0]
    t = min(ROW_TILE, s)
    with_du = w_gu_t is not None

    def body(*refs):
        if with_du:
            dy_ref, wd_ref, gu_ref, wgu_ref, dgu_ref, du_ref = refs
        else:
            dy_ref, wd_ref, gu_ref, dgu_ref = refs
        dyv = dy_ref[...]
        du = None
        for j in range(4):
            dact = _dot(dyv, wd_ref[j], NT)
            g = gu_ref[0, j].astype(F32)
            up = gu_ref[1, j].astype(F32)
            sg = _sigmoid(g)
            slopes = (up * sg * (1.0 + g * (1.0 - sg)), g * sg)
            for half in range(2):
                d = (dact * slopes[half]).astype(BF16)
                dgu_ref[half, j] = d
                if with_du:
                    part = _dot(d, wgu_ref[4 * half + j])
                    du = part if du is None else du + part
        if with_du:
            du_ref[...] = du

    blocks = pl.BlockSpec((2, 4, t, FF_BLK), lambda i: (0, 0, i, 0))
    dgu_shape = jax.ShapeDtypeStruct((2, 4, s, FF_BLK), BF16)
    if with_du:
        return _call(
            body, name=name, grid=(s // t,),
            in_specs=[_row_spec(t), _resident(w_down.shape), blocks, _resident(w_gu_t.shape)],
            out_specs=[blocks, _row_spec(t)],
            out_shape=[dgu_shape, jax.ShapeDtypeStruct((s, D_MODEL), F32)],
            operands=(dy, w_down, gu, w_gu_t), semantics=("parallel",), exchange=exchange)
    return _call(
        body, name=name, grid=(s // t,),
        in_specs=[_row_spec(t), _resident(w_down.shape), blocks], out_specs=blocks, out_shape=dgu_shape,
        operands=(dy, w_down, gu), semantics=("parallel",), exchange=exchange)


def matmul_nt_acc(a, b, name, b_dims=NT, exchange=None):
    nk = b.shape[0]
    d, n = (b.shape[1], b.shape[2]) if b_dims == NT else (b.shape[2], b.shape[1])
    s = a.shape[-2]
    t = min(ROW_TILE, s)
    by_columns = a.ndim == 2

    def body(a_ref, b_ref, o_ref):
        def a_blk(k):
            return a_ref[:, k * n:(k + 1) * n] if by_columns else a_ref[k]

        acc = _dot(a_blk(0), b_ref[0], b_dims)
        for k in range(1, nk):
            acc = acc + _dot(a_blk(k), b_ref[k], b_dims)
        o_ref[...] = acc

    a_spec = _row_spec(t, nk * n) if by_columns else pl.BlockSpec((nk, t, n), lambda i: (0, i, 0))
    return _call(
        body, name=name, grid=(s // t,),
        in_specs=[a_spec, pl.BlockSpec(b.shape, lambda i: (0, 0, 0))],
        out_specs=pl.BlockSpec((t, d), lambda i: (i, 0)),
        out_shape=jax.ShapeDtypeStruct((s, d), F32),
        operands=(a, b), semantics=("parallel",), exchange=exchange)


def matmul_tn(a, b, name, group=(1, 1), b_cols=None, exchange=None):
    ja, s, m = a.shape
    by_columns = b.ndim == 2
    jb, n = (b.shape[1] // b_cols, b_cols) if by_columns else (b.shape[0], b.shape[2])
    ga, gb = group
    t = min(ACC_TILE, s)
    nk = s // t

    def body(a_ref, b_ref, o_ref, acc_ref):
        k = pl.program_id(2)

        @pl.when(k == 0)
        def _():
            acc_ref[...] = jnp.zeros_like(acc_ref)

        for p in range(ga):
            if by_columns and gb % 2 == 0:
                for q in range(0, gb, 2):
                    both = _dot(a_ref[p], b_ref[:, q * n:(q + 2) * n], TN)
                    acc_ref[p, q] += both[:, :n]
                    acc_ref[p, q + 1] += both[:, n:]
                continue
            for q in range(gb):
                b_blk = b_ref[:, q * n:(q + 1) * n] if by_columns else b_ref[q]
                acc_ref[p, q] += _dot(a_ref[p], b_blk, TN)

        @pl.when(k == nk - 1)
        def _():
            o_ref[...] = acc_ref[...].astype(BF16)

    return _call(
        body, name=name, grid=(ja // ga, jb // gb, nk),
        in_specs=[pl.BlockSpec((ga, t, m), lambda p, q, k: (p, k, 0)),
                  pl.BlockSpec((t, gb * n), lambda p, q, k: (k, q)) if by_columns
                  else pl.BlockSpec((gb, t, n), lambda p, q, k: (q, k, 0))],
        out_specs=pl.BlockSpec((ga, gb, m, n), lambda p, q, k: (p, q, 0, 0)),
        out_shape=jax.ShapeDtypeStruct((ja, jb, m, n), BF16),
        operands=(a, b), scratch_shapes=[pltpu.VMEM((ga, gb, m, n), F32)],
        semantics=("parallel", "parallel", "arbitrary"), exchange=exchange)


def mix_in_proj(x, norm_g, shift, scale, w_mix, name, exchange=None):
    s = x.shape[0]
    t = min(ROW_TILE, s)

    def body(x_ref, g_ref, sh_ref, sc_ref, w_ref, u_ref, o_ref):
        uv = _norm_modulate(x_ref, g_ref, sh_ref, sc_ref)
        u_ref[...] = uv
        for j in range(0, N_DEV, 2):
            pair = jnp.concatenate([w_ref[j], w_ref[j + 1]], axis=1)
            o_ref[:, j * MIX_BLK:(j + 2) * MIX_BLK] = _dot(uv, pair).astype(BF16)

    return _call(
        body, name=name, grid=(s // t,),
        in_specs=[_row_spec(t), _vec_spec(), _vec_spec(), _vec_spec(), _resident(w_mix.shape)],
        out_specs=[_row_spec(t), _row_spec(t, MIX_W)],
        out_shape=[jax.ShapeDtypeStruct((s, D_MODEL), BF16), jax.ShapeDtypeStruct((s, MIX_W), BF16)],
        operands=(x, norm_g, shift, scale, w_mix), semantics=("parallel",), exchange=exchange)


def _conv_taps(cc_ref, cx_ref, s):
    v = cc_ref[...].astype(F32) * cx_ref[...].astype(F32)
    tok = lax.broadcasted_iota(jnp.int32, v.shape, 0)
    v1 = jnp.where(tok >= 1, pltpu.roll(v, 1, 0), 0.0)
    v2 = jnp.where(tok >= 2, pltpu.roll(v, 2, 0), 0.0)
    return v, v1, v2, tok


def _proj_cols(s, first):
    return pl.BlockSpec((s, 128), lambda j: (0, first + j))


def short_conv(proj, conv_w, name):
    s = proj.shape[0]

    def body(cb_ref, cc_ref, cx_ref, w_ref, o_ref):
        v, v1, v2, _ = _conv_taps(cc_ref, cx_ref, s)
        y = w_ref[0:1, :] * v2 + w_ref[1:2, :] * v1 + w_ref[2:3, :] * v
        o_ref[...] = (cb_ref[...].astype(F32) * y).astype(BF16)

    return pl.pallas_call(
        body, name=name, grid=(CONV_W // 128,),
        in_specs=[_proj_cols(s, 0), _proj_cols(s, 4), _proj_cols(s, 8),
                  pl.BlockSpec((3, 128), lambda j: (0, j))],
        out_specs=pl.BlockSpec((s, 128), lambda j: (0, j)),
        out_shape=jax.ShapeDtypeStruct((s, CONV_W), BF16),
        compiler_params=_params("parallel"),
    )(proj, proj, proj, conv_w)


def short_conv_bwd(dsa, proj, conv_w, name):
    s = proj.shape[0]

    def body(dsa_ref, cb_ref, cc_ref, cx_ref, w_ref, dcb_ref, dcc_ref, dcx_ref, dw_ref):
        v, v1, v2, tok = _conv_taps(cc_ref, cx_ref, s)
        w0, w1, w2 = w_ref[0:1, :], w_ref[1:2, :], w_ref[2:3, :]
        y = w0 * v2 + w1 * v1 + w2 * v
        dsa_v = dsa_ref[...].astype(F32)
        dcb_ref[...] = (dsa_v * y).astype(BF16)
        dy = dsa_v * cb_ref[...].astype(F32)
        dw_ref[0:1, :] = jnp.sum(dy * v2, axis=0, keepdims=True)
        dw_ref[1:2, :] = jnp.sum(dy * v1, axis=0, keepdims=True)
        dw_ref[2:3, :] = jnp.sum(dy * v, axis=0, keepdims=True)
        dy1 = jnp.where(tok < s - 1, pltpu.roll(dy, s - 1, 0), 0.0)
        dy2 = jnp.where(tok < s - 2, pltpu.roll(dy, s - 2, 0), 0.0)
        dv = w2 * dy + w1 * dy1 + w0 * dy2
        dcc_ref[...] = (dv * cx_ref[...].astype(F32)).astype(BF16)
        dcx_ref[...] = (dv * cc_ref[...].astype(F32)).astype(BF16)

    col = pl.BlockSpec((s, 128), lambda j: (0, j))
    act = jax.ShapeDtypeStruct((s, CONV_W), BF16)
    return pl.pallas_call(
        body, name=name, grid=(CONV_W // 128,),
        in_specs=[col, _proj_cols(s, 0), _proj_cols(s, 4), _proj_cols(s, 8),
                  pl.BlockSpec((3, 128), lambda j: (0, j))],
        out_specs=[col, col, col, pl.BlockSpec((3, 128), lambda j: (0, j))],
        out_shape=[act, act, act, jax.ShapeDtypeStruct((3, CONV_W), F32)],
        compiler_params=_params("parallel"),
    )(dsa, proj, proj, proj, conv_w)


def _gate_specs(t):
    return [pl.BlockSpec((t, D_MODEL), lambda i: (i, 3)), pl.BlockSpec((t, D_MODEL), lambda i: (i, 4))]


def merge_forward(sa, o, proj, w_co, w_ao, b_merge, w_out, x, gate, name, exchange=None):
    s = sa.shape[0]
    t = min(ROW_TILE, s)

    def body(sa_ref, o_ref, ga_ref, gb_ref, wco_ref, wao_ref, bm_ref, wout_ref, x_ref, gt_ref,
             mg_ref, ya_ref, yb_ref, y_ref, xo_ref):
        ya = _dot(sa_ref[...], wco_ref[...])
        yb = _dot(o_ref[...], wao_ref[...])
        sga = _sigmoid(ga_ref[...].astype(F32) + bm_ref[0:1, :])
        sgb = _sigmoid(gb_ref[...].astype(F32) + bm_ref[1:2, :])
        merged = (sga * ya + sgb * yb).astype(BF16)
        mg_ref[...] = merged
        ya_ref[...] = ya.astype(BF16)
        yb_ref[...] = yb.astype(BF16)
        y = _dot(merged, wout_ref[...])
        y_ref[...] = y.astype(BF16)
        xo_ref[...] = x_ref[...] + gt_ref[...] * y

    act = jax.ShapeDtypeStruct((s, D_MODEL), BF16)
    return _call(
        body, name=name, grid=(s // t,),
        in_specs=[_row_spec(t, CONV_W), _row_spec(t, ATTN_W)] + _gate_specs(t)
        + [_vec_spec(CONV_W), _vec_spec(ATTN_W), _vec_spec(2), _vec_spec(D_MODEL), _row_spec(t), _vec_spec()],
        out_specs=[_row_spec(t)] * 5, out_shape=[act, act, act, act, jax.ShapeDtypeStruct((s, D_MODEL), F32)],
        operands=(sa, o, proj, proj, w_co, w_ao, b_merge, w_out, x, gate), semantics=("parallel",),
        exchange=exchange)


def merge_backward(dy, w_out, proj, ya, yb, b_merge, name, exchange=None):
    s = dy.shape[0]
    t = min(ROW_TILE, s)

    def body(dy_ref, w_ref, ga_ref, gb_ref, ya_ref, yb_ref, bm_ref,
             dya_ref, dyb_ref, dga_ref, dgb_ref, dbm_ref):
        @pl.when(pl.program_id(0) == 0)
        def _():
            dbm_ref[...] = jnp.zeros_like(dbm_ref)

        dmg = _dot(dy_ref[...], w_ref[...], NT)
        sga = _sigmoid(ga_ref[...].astype(F32) + bm_ref[0:1, :])
        sgb = _sigmoid(gb_ref[...].astype(F32) + bm_ref[1:2, :])
        dya_ref[...] = (dmg * sga).astype(BF16)
        dyb_ref[...] = (dmg * sgb).astype(BF16)
        dga = dmg * ya_ref[...].astype(F32) * sga * (1.0 - sga)
        dgb = dmg * yb_ref[...].astype(F32) * sgb * (1.0 - sgb)
        dga_ref[...] = dga.astype(BF16)
        dgb_ref[...] = dgb.astype(BF16)
        dbm_ref[0:1, :] += jnp.sum(dga, axis=0, keepdims=True)
        dbm_ref[1:2, :] += jnp.sum(dgb, axis=0, keepdims=True)

    act = jax.ShapeDtypeStruct((s, D_MODEL), BF16)
    return _call(
        body, name=name, grid=(s // t,),
        in_specs=[_row_spec(t), _vec_spec(D_MODEL)] + _gate_specs(t)
        + [_row_spec(t), _row_spec(t), _vec_spec(2)],
        out_specs=[_row_spec(t)] * 4 + [_vec_spec(2)],
        out_shape=[act] * 4 + [jax.ShapeDtypeStruct((2, D_MODEL), F32)],
        operands=(dy, w_out, proj, proj, ya, yb, b_merge), semantics=("arbitrary",), exchange=exchange)


def out_proj_bwd(dya, dyb, w_co, w_ao, name):
    s = dya.shape[0]
    t = min(ROW_TILE, s)

    def body(dya_ref, dyb_ref, wco_ref, wao_ref, dsa_ref, do_ref):
        dsa_ref[...] = _dot(dya_ref[...], wco_ref[...], NT).astype(BF16)
        do_ref[...] = _dot(dyb_ref[...], wao_ref[...], NT).astype(BF16)

    return pl.pallas_call(
        body, name=name, grid=(s // t,),
        in_specs=[_row_spec(t), _row_spec(t), _vec_spec(CONV_W), _vec_spec(ATTN_W)],
        out_specs=[_row_spec(t, CONV_W), _row_spec(t, ATTN_W)],
        out_shape=[jax.ShapeDtypeStruct((s, CONV_W), BF16), jax.ShapeDtypeStruct((s, ATTN_W), BF16)],
        compiler_params=_params("parallel"),
    )(dya, dyb, w_co, w_ao)


ATT_HEADS = 4
ATT_LANES = ATT_HEADS * HEAD_DIM
ATT_UNDERFLOW = 110.0


def _softplus(z):
    return jnp.maximum(z, 0.0) + jnp.log(1.0 + jnp.exp(-jnp.abs(z)))


def _head_masks(rows):
    lane = lax.broadcasted_iota(jnp.int32, (rows, ATT_LANES), 1)
    return [(lane >= h * HEAD_DIM) & (lane < (h + 1) * HEAD_DIM) for h in range(ATT_HEADS)]


def _per_head(x, masks):
    return [jnp.where(m, x, jnp.zeros_like(x)) for m in masks]


def _att_specs(s, blk):
    first = {"q": 3 * CONV_W // ATT_LANES, "k": (3 * CONV_W + ATTN_W) // ATT_LANES,
             "v": (3 * CONV_W + 2 * ATTN_W) // ATT_LANES}
    return [pl.BlockSpec((blk, ATT_LANES), lambda h, i: (i, first["q"] + h)),
            pl.BlockSpec((s, ATT_LANES), lambda h, i: (0, first["k"] + h)),
            pl.BlockSpec((s, ATT_LANES), lambda h, i: (0, first["v"] + h))]


def _head_norms(x, masks):
    sq = jnp.square(x.astype(F32))
    return [jnp.sum(jnp.where(m, sq, 0.0), axis=1, keepdims=True) for m in masks]


def stick_breaking_fwd(proj, name, exchange=None):
    s = proj.shape[0]
    blk = ATT_BLK
    nq = s // blk

    def body(q_ref, k_ref, v_ref, o_ref, tot_ref, first_ref, kmax_ref):
        i = pl.program_id(1)
        row = lax.broadcasted_iota(jnp.int32, (blk, blk), 0)
        col = lax.broadcasted_iota(jnp.int32, (blk, blk), 1)
        tri = (row >= col).astype(BF16)
        causal = col < row
        masks = _head_masks(blk)
        q_all = q_ref[...] * ATTN_SCALE
        qs = _per_head(q_all, masks)

        @pl.when(i == 0)
        def _():
            def longest(n, best):
                norms = _head_norms(k_ref[pl.ds(pl.multiple_of(n * blk, blk), blk), :], masks)
                return tuple(jnp.maximum(b, v) for b, v in zip(best, norms))

            best = lax.fori_loop(0, nq, longest, tuple(jnp.zeros((blk, 1), F32) for _ in range(ATT_HEADS)))
            for h in range(ATT_HEADS):
                kmax_ref[h] = jnp.sqrt(jnp.max(best[h], axis=0, keepdims=True))

        needed = [jnp.sqrt(n) * kmax_ref[h] + ATT_UNDERFLOW for h, n in enumerate(_head_norms(q_all, masks))]

        def finished(laters):
            slack = laters[0] - needed[0]
            for h in range(1, ATT_HEADS):
                slack = jnp.minimum(slack, laters[h] - needed[h])
            return (jnp.min(slack) >= 0.0).astype(jnp.int32)

        def step(j, carry, diagonal):
            laters, acc = carry
            rows = pl.ds(pl.multiple_of(j * blk, blk), blk)
            kb = k_ref[rows, :]
            probs, new_laters = [], []
            for h in range(ATT_HEADS):
                z = _dot(qs[h], kb, NT)
                sp = _softplus(z)
                if diagonal:
                    sp = jnp.where(causal, sp, 0.0)
                a = jnp.exp(z - (_dot(sp.astype(BF16), tri) + laters[h]))
                if diagonal:
                    a = jnp.where(causal, a, 0.0)
                probs.append(a.astype(BF16))
                new_laters.append(laters[h] + jnp.sum(sp, axis=1, keepdims=True))
            v_heads = jnp.concatenate(_per_head(v_ref[rows, :], masks), axis=0)
            acc = acc + _dot(jnp.concatenate(probs, axis=1), v_heads)
            return tuple(new_laters), acc

        carry = (tuple(jnp.zeros((blk, 1), F32) for _ in range(ATT_HEADS)), jnp.zeros((blk, ATT_LANES), F32))
        laters, acc = step(i, carry, True)

        def further(state):
            n, _, laters, acc = state
            laters, acc = step(i - 1 - n, (laters, acc), False)
            return n + 1, finished(laters), laters, acc

        walked, _, laters, acc = lax.while_loop(
            lambda state: jnp.logical_and(state[0] < i, state[1] == 0), further,
            (jnp.int32(0), finished(laters), laters, acc))
        o_ref[...] = acc.astype(BF16)
        tot = jnp.zeros((blk, ATT_LANES), F32)
        for h in range(ATT_HEADS):
            tot = jnp.where(masks[h], laters[h], tot)
        tot_ref[...] = tot
        first_ref[...] = jnp.full(first_ref.shape, i - walked, jnp.int32).astype(F32)

    out_spec = pl.BlockSpec((blk, ATT_LANES), lambda h, i: (i, h))
    groups = N_HEADS // ATT_HEADS
    return _call(
        body, name=name, grid=(groups, nq),
        in_specs=_att_specs(s, blk),
        out_specs=[out_spec, out_spec, pl.BlockSpec((None, None, 8, 128), lambda h, i: (h, i, 0, 0))],
        out_shape=[jax.ShapeDtypeStruct((s, ATTN_W), BF16), jax.ShapeDtypeStruct((s, ATTN_W), F32),
                   jax.ShapeDtypeStruct((groups, nq, 8, 128), F32)],
        operands=(proj, proj, proj), scratch_shapes=[pltpu.VMEM((ATT_HEADS, 1, 1), F32)],
        semantics=("parallel", "arbitrary"), exchange=exchange)


def stick_breaking_bwd(proj, do, tot, first, name, exchange=None):
    s = proj.shape[0]
    blk = ATT_BLK
    nq = s // blk

    def body(q_ref, k_ref, v_ref, do_ref, tot_ref, first_ref, dq_ref, dk_ref, dv_ref):
        i = pl.program_id(1)
        start = jnp.clip(jnp.max(first_ref[...]).astype(jnp.int32), 0, i)

        @pl.when(i == 0)
        def _():
            dk_ref[...] = jnp.zeros_like(dk_ref)
            dv_ref[...] = jnp.zeros_like(dv_ref)

        row = lax.broadcasted_iota(jnp.int32, (blk, blk), 0)
        col = lax.broadcasted_iota(jnp.int32, (blk, blk), 1)
        before = (row < col).astype(BF16)
        upto = (row <= col).astype(BF16)
        causal = col < row
        masks = _head_masks(blk)
        qs = _per_head(q_ref[...] * ATTN_SCALE, masks)
        dos = _per_head(do_ref[...], masks)
        q_heads = jnp.concatenate(qs, axis=0)
        do_heads = jnp.concatenate(dos, axis=0)
        tot_all = tot_ref[...]
        totals = [jnp.max(jnp.where(m, tot_all, 0.0), axis=1, keepdims=True) for m in masks]

        def step(j, carry, diagonal):
            earliers, g_sums, dq = carry
            rows = pl.ds(pl.multiple_of(j * blk, blk), blk)
            kb = k_ref[rows, :]
            vb = v_ref[rows, :]
            probs, dzs, new_earliers, new_g_sums = [], [], [], []
            for h in range(ATT_HEADS):
                z = _dot(qs[h], kb, NT)
                sp = _softplus(z)
                if diagonal:
                    sp = jnp.where(causal, sp, 0.0)
                c = (totals[h] - earliers[h]) - _dot(sp.astype(BF16), before)
                a = jnp.exp(z - c)
                if diagonal:
                    a = jnp.where(causal, a, 0.0)
                g = a * _dot(dos[h], vb, NT)
                f = g_sums[h] + _dot(g.astype(BF16), upto)
                dz = g - jnp.exp(z - sp) * f
                if diagonal:
                    dz = jnp.where(causal, dz, 0.0)
                probs.append(a.astype(BF16))
                dzs.append(dz.astype(BF16))
                new_earliers.append(earliers[h] + jnp.sum(sp, axis=1, keepdims=True))
                new_g_sums.append(g_sums[h] + jnp.sum(g, axis=1, keepdims=True))
            k_heads = jnp.concatenate(_per_head(kb, masks), axis=0)
            dq = dq + _dot(jnp.concatenate(dzs, axis=1), k_heads)
            dk_ref[rows, :] += _dot(jnp.concatenate(dzs, axis=0), q_heads, TN)
            dv_ref[rows, :] += _dot(jnp.concatenate(probs, axis=0), do_heads, TN)
            return tuple(new_earliers), tuple(new_g_sums), dq

        zeros = tuple(jnp.zeros((blk, 1), F32) for _ in range(ATT_HEADS))
        carry = (zeros, zeros, jnp.zeros((blk, ATT_LANES), F32))
        carry = lax.fori_loop(start, i, lambda j, c: step(j, c, False), carry)
        dq = step(i, carry, True)[2]
        dq_ref[...] = (dq * ATTN_SCALE).astype(BF16)

    blk_spec = pl.BlockSpec((blk, ATT_LANES), lambda h, i: (i, h))
    full_spec = pl.BlockSpec((s, ATT_LANES), lambda h, i: (0, h))
    return _call(
        body, name=name, grid=(N_HEADS // ATT_HEADS, nq),
        in_specs=_att_specs(s, blk) + [blk_spec, blk_spec,
                                       pl.BlockSpec((None, None, 8, 128), lambda h, i: (h, i, 0, 0))],
        out_specs=[blk_spec, full_spec, full_spec],
        out_shape=[jax.ShapeDtypeStruct((s, ATTN_W), BF16), jax.ShapeDtypeStruct((s, ATTN_W), F32),
                   jax.ShapeDtypeStruct((s, ATTN_W), F32)],
        operands=(proj, proj, proj, do, tot, first), semantics=("parallel", "arbitrary"), exchange=exchange)


def adamw(w, m, v, parts, name):
    r, c = w.shape
    p = parts.shape[0]
    t = r
    for cand in (256, 176):
        if r % cand == 0 and r > cand:
            t = cand
            break

    def body(w_ref, m_ref, v_ref, p_ref, g_ref, d_ref, mo_ref, vo_ref):
        g = p_ref[0].astype(F32)
        for n in range(1, p):
            g = g + p_ref[n].astype(F32)
        m_new = ADAM_B1 * m_ref[...] + (1.0 - ADAM_B1) * g
        v_new = ADAM_B2 * v_ref[...] + (1.0 - ADAM_B2) * (g * g)
        m_hat = m_new / ADAM_BC1
        v_hat = v_new / ADAM_BC2
        g_ref[...] = g
        d_ref[...] = -ADAM_LR * (m_hat / (jnp.sqrt(v_hat) + ADAM_EPS) + ADAM_WD * w_ref[...])
        mo_ref[...] = m_new
        vo_ref[...] = v_new

    spec = pl.BlockSpec((t, c), lambda i: (i, 0))
    out = jax.ShapeDtypeStruct((r, c), F32)
    return pl.pallas_call(
        body, name=name, grid=(r // t,),
        in_specs=[spec, spec, spec, pl.BlockSpec((p, t, c), lambda i: (0, i, 0))],
        out_specs=[spec] * 4, out_shape=[out] * 4,
        compiler_params=_params("parallel"),
    )(w, m, v, parts)


def adamw_replicated(ws, ms, vs, stats_all, offsets, name):
    count = len(ws)

    def body(*refs):
        w_refs, m_refs, v_refs = refs[:count], refs[count:2 * count], refs[2 * count:3 * count]
        st_ref = refs[3 * count]
        outs = refs[3 * count + 1:]
        for k in range(count):
            lo, n = offsets[k], ws[k].shape[1]
            g = st_ref[0][:, lo:lo + n]
            for d in range(1, N_DEV):
                g = g + st_ref[d][:, lo:lo + n]
            m_new = ADAM_B1 * m_refs[k][...] + (1.0 - ADAM_B1) * g
            v_new = ADAM_B2 * v_refs[k][...] + (1.0 - ADAM_B2) * (g * g)
            m_hat = m_new / ADAM_BC1
            v_hat = v_new / ADAM_BC2
            outs[4 * k][...] = g
            outs[4 * k + 1][...] = -ADAM_LR * (m_hat / (jnp.sqrt(v_hat) + ADAM_EPS) + ADAM_WD * w_refs[k][...])
            outs[4 * k + 2][...] = m_new
            outs[4 * k + 3][...] = v_new

    return pl.pallas_call(
        body, name=name,
        out_shape=[jax.ShapeDtypeStruct(w.shape, F32) for w in ws for _ in range(4)],
        compiler_params=pltpu.CompilerParams(vmem_limit_bytes=VMEM_LIMIT),
    )(*ws, *ms, *vs, stats_all)


def kernel(x, c, w_ada, b_ada, norm1_g, ffn1_w_gu, ffn1_w_down, norm2_g, w_mix_in, b_merge, conv_w, w_conv_out, w_attn_out, w_out, norm3_g, ffn2_w_gu, ffn2_w_down, final_g, loss_target, m_w_ada, m_b_ada, m_norm1_g, m_ffn1_w_gu, m_ffn1_w_down, m_norm2_g, m_w_mix_in, m_b_merge, m_conv_w, m_w_conv_out, m_w_attn_out, m_w_out, m_norm3_g, m_ffn2_w_gu, m_ffn2_w_down, m_final_g, v_w_ada, v_b_ada, v_norm1_g, v_ffn1_w_gu, v_ffn1_w_down, v_norm2_g, v_w_mix_in, v_b_merge, v_conv_w, v_w_conv_out, v_w_attn_out, v_w_out, v_norm3_g, v_ffn2_w_gu, v_ffn2_w_down, v_final_g):
    s = x.shape[1]
    me = 4 * lax.axis_index("x") + 2 * lax.axis_index("y") + lax.axis_index("c")
    x0 = x[0]
    target = loss_target[0]
    final_g2 = final_g.reshape(1, D_MODEL)

    def shard(w):
        return w[0].astype(BF16)

    def flipped(w):
        return jnp.swapaxes(w, 1, 2)

    def rows8(g):
        return g.reshape(N_DEV, -1, D_MODEL)

    small_in = jnp.concatenate([c.reshape(-1), b_merge.reshape(-1), conv_w.reshape(-1),
                                jnp.zeros((64,), F32)]).reshape(1, -1)
    n_ada = w_ada.shape[2]
    b_cols = lax.dynamic_slice(b_ada, (0, me * n_ada), (1, n_ada))
    small_all, mod_all, (wgu1, wd1) = prologue(small_in, w_ada[0], b_cols,
                                               [shard(flipped(ffn1_w_gu)), shard(ffn1_w_down)], "prologue")
    wd1 = wd1.reshape(4, FF_BLK, D_MODEL)
    small_all = small_all[:, 0, :]
    c_all = small_all[:, :D_MODEL]
    bm_full = small_all[:, 1024:1280].reshape(8, 2, 128).transpose(1, 0, 2).reshape(2, D_MODEL)
    cw_full = small_all[:, 1280:1472].reshape(8, 3, 64).transpose(1, 0, 2).reshape(3, CONV_W)
    mod = lax.dynamic_index_in_dim(mod_all, me, axis=1, keepdims=False).reshape(9, 1, D_MODEL)
    sh1, sc1, gt1, sh2, sc2, gt2, sh3, sc3, gt3 = [mod[n] for n in range(9)]

    (u1, gu1, act1), got = ffn_up(x0, norm1_g, sh1, sc1, wgu1, "ffn_up_1",
                                  exchange=gather_stage1([shard(w_mix_in)]))
    (x1, y1), (wmix, *got) = residual_matmul(
        act1, wd1, x0, gt1, 0.5, "ffn_down_1", exchange=merge_exchanges(
            gather_stage2(got), gather_stage1([shard(w_conv_out), shard(w_attn_out), shard(w_out)])))

    (u2, proj), (wco, wao, wout) = mix_in_proj(x1, norm2_g, sh2, sc2, wmix, "mix_in",
                                               exchange=gather_stage2(got))
    wco = wco.transpose(1, 0, 2).reshape(CONV_W, D_MODEL)
    wao = wao.transpose(1, 0, 2).reshape(ATTN_W, D_MODEL)
    wout = wout.reshape(D_MODEL, D_MODEL)
    sa = short_conv(proj, cw_full, "short_conv")
    (o, tot, first), got = stick_breaking_fwd(proj, "attn_fwd",
                                       exchange=gather_stage1([shard(flipped(ffn2_w_gu)), shard(ffn2_w_down)]))
    (merged, ya, yb, y2, x2), (wgu3, wd3) = merge_forward(sa, o, proj, wco, wao, bm_full, wout, x1, gt2, "merge",
                                                          exchange=gather_stage2(got))
    wd3 = wd3.reshape(4, FF_BLK, D_MODEL)

    u3, gu3, act3 = ffn_up(x2, norm3_g, sh3, sc3, wgu3, "ffn_up_3")

    dx3, dy3, dgt3, dfinal, sq = residual_matmul_loss(act3, wd3, x2, gt3, 0.5, target, final_g2, "ffn_down_3_loss")
    dgu3, du3 = ffn_tokens_bwd(dy3, wd3, gu3, wgu3, "ffn_bwd_3")
    dgu3 = dgu3.reshape(8, s, FF_BLK)
    g_wd3 = rows8(matmul_tn(act3, dy3[None], "grad_w_down_3", group=(4, 1)))
    g_wgu3 = matmul_tn(dgu3, u3[None], "grad_w_gu_3", group=(4, 1)).reshape(8, FF_BLK, D_MODEL)
    dx2, dsh3, dsc3, dn3, dy2, dgt2 = norm_modulate_bwd(du3, x2, dx3, norm3_g, sc3, "norm_bwd_3",
                                                        prev=(gt2, y2, 1.0))

    (dya, dyb, dga, dgb, dbm), pairs = merge_backward(dy2, wout, proj, ya, yb, bm_full, "merge_bwd",
                                                      exchange=scatter_stage1([g_wgu3, g_wd3]))
    sums3 = [pair_sum(g_wgu3, pairs[0], "pair_sum_w_gu_3"), pair_sum(g_wd3, pairs[1], "pair_sum_w_down_3")]
    g_wout = rows8(matmul_tn(merged[None], dy2[None], "grad_w_out"))
    dsa, do = out_proj_bwd(dya, dyb, wco, wao, "out_proj_bwd")
    g_wco = matmul_tn(sa[None], dya[None], "grad_w_conv_out").reshape(CONV_W, N_DEV, 128).transpose(1, 0, 2)
    g_wao = matmul_tn(o[None], dyb[None], "grad_w_attn_out").reshape(ATTN_W, N_DEV, 128).transpose(1, 0, 2)
    dcb, dcc, dcx, dconv = short_conv_bwd(dsa, proj, cw_full, "short_conv_bwd")
    (dq, dk, dv), landed3 = stick_breaking_bwd(proj, do, tot, first, "attn_bwd", exchange=scatter_stage2(sums3))
    dproj = jnp.concatenate([dcb, dcc, dcx, dq, dk.astype(BF16), dv.astype(BF16), dga, dgb], axis=1)
    g_wmix = matmul_tn(u2[None], dproj, "grad_w_mix_in", group=(1, 4), b_cols=MIX_BLK).reshape(
        N_DEV, D_MODEL, MIX_BLK)
    mixer_grads = [g_wmix, g_wco, g_wao, g_wout]
    (dx1, dsh2, dsc2, dn2, dy1, dgt1), pairs = norm_modulate_bwd(
        (dproj, wmix), x1, dx2, norm2_g, sc2, "norm_bwd_2", prev=(gt1, y1, 0.5),
        exchange=scatter_stage1(mixer_grads))
    sums_mix = [pair_sum(g, p, f"pair_sum_mixer_{n}") for n, (g, p) in enumerate(zip(mixer_grads, pairs))]

    dgu1, landed_mix = ffn_tokens_bwd(dy1, wd1, gu1, None, "ffn_dact_1", exchange=scatter_stage2(sums_mix[:1]))
    dgu1 = dgu1.reshape(8, s, FF_BLK)
    g_wgu1, landed_small = matmul_tn(dgu1, u1[None], "grad_w_gu_1", group=(4, 1),
                                     exchange=scatter_stage2(sums_mix[1:]))
    g_wgu1 = g_wgu1.reshape(8, FF_BLK, D_MODEL)
    g_wd1, pairs = matmul_tn(act1, dy1[None], "grad_w_down_1", group=(4, 1), exchange=scatter_stage1([g_wgu1]))
    g_wd1 = rows8(g_wd1)
    sum_gu1 = pair_sum(g_wgu1, pairs[0], "pair_sum_w_gu_1")
    du1, (landed_gu1, pair_d1) = matmul_nt_acc(
        dgu1, wgu1, "ffn_du_1", b_dims=NN, exchange=merge_exchanges(scatter_stage2([sum_gu1]), scatter_stage1([g_wd1])))
    sum_d1 = pair_sum(g_wd1, pair_d1, "pair_sum_w_down_1")
    (grad_x, dsh1, dsc1, dn1), landed_d1 = norm_modulate_bwd(du1, x0, dx1, norm1_g, sc1, "norm_bwd_1",
                                                            exchange=scatter_stage2([sum_d1]))

    loss_local = (0.5 / D_MODEL) * jnp.sum(sq)
    stats = jnp.concatenate(
        [v.reshape(-1) for v in (dsh1, dsc1, dgt1, dsh2, dsc2, dgt2, dsh3, dsc3, dgt3,
                                 dn1, dn2, dn3, dfinal, dbm, dconv)]
        + [jnp.broadcast_to(loss_local, (128,))]).reshape(1, -1)
    stats_all = all_gather_rows(stats, "gather_stats")
    n_mod = 9 * D_MODEL
    loss = jnp.sum(stats_all[:, 0, -1])
    dmod_all = stats_all[:, :, :n_mod]
    off = n_mod
    parts = {}
    off += 4 * D_MODEL
    dbm_all = stats_all[:, 0, off:off + 2 * D_MODEL].reshape(N_DEV, 2, D_MODEL)
    off += 2 * D_MODEL
    dcw_all = stats_all[:, 0, off:off + 3 * CONV_W].reshape(N_DEV, 3, CONV_W)
    parts["b_merge"] = lax.dynamic_slice(dbm_all, (0, 0, me * 128), (N_DEV, 2, 128))
    parts["conv_w"] = lax.dynamic_slice(dcw_all, (0, 0, me * 64), (N_DEV, 3, 64))
    dmod_cols = lax.dynamic_slice(dmod_all[:, 0, :], (0, me * n_ada), (N_DEV, n_ada))
    parts["w_ada"] = ada_backward(c_all, dmod_cols, "ada_backward")[None]
    parts["ffn2_w_gu"], parts["ffn2_w_down"] = landed3
    parts["w_mix_in"] = landed_mix[0]
    parts["w_conv_out"], parts["w_attn_out"], parts["w_out"] = landed_small
    parts["ffn1_w_gu"] = landed_gu1
    parts["ffn1_w_down"] = landed_d1[0]

    given = dict(w_ada=w_ada, b_ada=b_ada, norm1_g=norm1_g, ffn1_w_gu=ffn1_w_gu, ffn1_w_down=ffn1_w_down,
                 norm2_g=norm2_g, w_mix_in=w_mix_in, b_merge=b_merge, conv_w=conv_w, w_conv_out=w_conv_out,
                 w_attn_out=w_attn_out, w_out=w_out, norm3_g=norm3_g, ffn2_w_gu=ffn2_w_gu,
                 ffn2_w_down=ffn2_w_down, final_g=final_g)
    moments_m = dict(w_ada=m_w_ada, b_ada=m_b_ada, norm1_g=m_norm1_g, ffn1_w_gu=m_ffn1_w_gu,
                     ffn1_w_down=m_ffn1_w_down, norm2_g=m_norm2_g, w_mix_in=m_w_mix_in, b_merge=m_b_merge,
                     conv_w=m_conv_w, w_conv_out=m_w_conv_out, w_attn_out=m_w_attn_out, w_out=m_w_out,
                     norm3_g=m_norm3_g, ffn2_w_gu=m_ffn2_w_gu, ffn2_w_down=m_ffn2_w_down, final_g=m_final_g)
    moments_v = dict(w_ada=v_w_ada, b_ada=v_b_ada, norm1_g=v_norm1_g, ffn1_w_gu=v_ffn1_w_gu,
                     ffn1_w_down=v_ffn1_w_down, norm2_g=v_norm2_g, w_mix_in=v_w_mix_in, b_merge=v_b_merge,
                     conv_w=v_conv_w, w_conv_out=v_w_conv_out, w_attn_out=v_w_attn_out, w_out=v_w_out,
                     norm3_g=v_norm3_g, ffn2_w_gu=v_ffn2_w_gu, ffn2_w_down=v_ffn2_w_down, final_g=v_final_g)
    order = ["w_ada", "b_ada", "norm1_g", "ffn1_w_gu", "ffn1_w_down", "norm2_g", "w_mix_in", "b_merge",
             "conv_w", "w_conv_out", "w_attn_out", "w_out", "norm3_g", "ffn2_w_gu", "ffn2_w_down", "final_g"]
    replicated = {"b_ada": 0, "norm1_g": n_mod, "norm2_g": n_mod + D_MODEL, "norm3_g": n_mod + 2 * D_MODEL,
                  "final_g": n_mod + 3 * D_MODEL}
    flat = adamw_replicated([given[k].reshape(1, -1) for k in replicated],
                            [moments_m[k].reshape(1, -1) for k in replicated],
                            [moments_v[k].reshape(1, -1) for k in replicated],
                            stats_all, list(replicated.values()), "adamw_replicated")
    done = {k: flat[4 * n:4 * n + 4] for n, k in enumerate(replicated)}
    grads, deltas, new_m, new_v = [], [], [], []
    for key in order:
        if key in done:
            for dst, val in zip((grads, deltas, new_m, new_v), done[key]):
                dst.append(val.reshape(given[key].shape))
            continue
        turn = flipped if key in ("ffn1_w_gu", "ffn2_w_gu") else (lambda a: a)
        shape = turn(given[key]).shape
        shape2 = (1, shape[0]) if len(shape) == 1 else shape[-2:]
        outs = adamw(turn(given[key]).reshape(shape2), turn(moments_m[key]).reshape(shape2),
                     turn(moments_v[key]).reshape(shape2), parts[key], f"adamw_{key}")
        for dst, val in zip((grads, deltas, new_m, new_v), outs):
            dst.append(turn(val.reshape(shape)))

    return (loss, grad_x[None], *grads, *deltas, *new_m, *new_v)
```
